```python
import math
import jax, jax.numpy as jnp
from jax import lax
import numpy as np


D_MODEL = 1024
BATCH = 16
SEQ = 2048
DEPTH = 2

HEAD_DIM = 64
BRANCH_WIDTH = D_MODEL // 2
N_BRANCHES = 3
DIFF_HEADS = BRANCH_WIDTH // (2 * HEAD_DIM)
RET_HEAD_DIM = 2 * HEAD_DIM
RET_HEADS = BRANCH_WIDTH // RET_HEAD_DIM
FOX_HEADS = BRANCH_WIDTH // HEAD_DIM
IN_SPLITS = (BRANCH_WIDTH,) * 3 + (BRANCH_WIDTH,) * 4 + (BRANCH_WIDTH,) * 3 + (FOX_HEADS, N_BRANCHES * D_MODEL)
IN_COLS = 10 * BRANCH_WIDTH + FOX_HEADS + N_BRANCHES * D_MODEL
Q_BLOCK = 128
RET_CHUNK = 128
ROPE_BASE = 10000.0
RET_GAMMA_EXP0 = 5.0
FOX_BIAS_INIT = 3.0
N_EXPERTS = 32
TOP_K = 4
D_FF = D_MODEL
SWIGLU_LIMIT = 7.0
SWIGLU_ALPHA = 1.702
NORM_EPS = 1e-5
NEG_INF = -1e30

kernel_name = "hybrid_diffattn_retnet_fox_moe_adaln"


def _rms_norm(t, g=None):
    tf = t.astype(jnp.float32)
    out = tf * lax.rsqrt(jnp.mean(tf * tf, axis=-1, keepdims=True) + NORM_EPS)
    if g is not None:
        out = out * g.astype(jnp.float32)
    return out.astype(t.dtype)


def _group_norm_heads(t):
    tf = t.astype(jnp.float32)
    mu = jnp.mean(tf, axis=-1, keepdims=True)
    var = jnp.mean(jnp.square(tf - mu), axis=-1, keepdims=True)
    return ((tf - mu) * lax.rsqrt(var + NORM_EPS)).astype(t.dtype)


def _modulate(hn, shift, scale):
    return hn * (1.0 + scale[:, None, :]) + shift[:, None, :]


def _split_columns(t, sizes):
    outs = []
    start = 0
    for w in sizes:
        outs.append(t[..., start:start + w])
        start += w
    return outs


def _heads(t, n_heads):
    b, s, w = t.shape
    return t.reshape(b, s, n_heads, w // n_heads).transpose(0, 2, 1, 3)


def _merge_heads(t):
    b, h, s, d = t.shape
    return t.transpose(0, 2, 1, 3).reshape(b, s, h * d)


def _rotary(t, pos):
    d = t.shape[-1]
    inv = ROPE_BASE ** (-jnp.arange(0, d, 2, dtype=jnp.float32) / d)
    ang = pos.astype(jnp.float32)[:, None] * inv[None, :]
    cos = jnp.cos(ang).astype(t.dtype)
    sin = jnp.sin(ang).astype(t.dtype)
    t1, t2 = t[..., : d // 2], t[..., d // 2:]
    return jnp.concatenate([t1 * cos - t2 * sin, t1 * sin + t2 * cos], axis=-1)


def _causal_attention_blocked(q, k, v, fcum=None):
    b, h, s, dk = q.shape
    dv = v.shape[-1]
    nb = s // Q_BLOCK
    scale = dk ** -0.5
    key_pos = jnp.arange(s)
    qb = q.reshape(b, h, nb, Q_BLOCK, dk).transpose(2, 0, 1, 3, 4)
    xs = [jnp.arange(nb), qb]
    if fcum is not None:
        xs.append(fcum.reshape(b, h, nb, Q_BLOCK).transpose(2, 0, 1, 3))

    def one_block(args):
        i, q_i = args[0], args[1]
        sc = jnp.einsum('bhqd,bhkd->bhqk', q_i, k).astype(jnp.float32) * scale
        if fcum is not None:
            sc = sc + args[2][..., :, None] - fcum[:, :, None, :]
        q_pos = i * Q_BLOCK + jnp.arange(Q_BLOCK)
        sc = jnp.where(key_pos[None, :] <= q_pos[:, None], sc, NEG_INF)
        p = jax.nn.softmax(sc, axis=-1).astype(v.dtype)
        return jnp.einsum('bhqk,bhkv->bhqv', p, v)

    out = lax.map(one_block, tuple(xs))
    return out.transpose(1, 2, 0, 3, 4).reshape(b, h, s, dv)


def _retention_chunkwise(q, k, v, log_gamma):
    out_dtype = v.dtype
    q, k, v = q.astype(jnp.float32), k.astype(jnp.float32), v.astype(jnp.float32)
    b, h, s, dk = q.shape
    dv = v.shape[-1]
    nc = s // RET_CHUNK
    idx = jnp.arange(RET_CHUNK, dtype=jnp.float32)
    dist = idx[:, None] - idx[None, :]
    inner_decay = jnp.where(dist >= 0, jnp.exp(log_gamma[:, None, None] * jnp.maximum(dist, 0.0)), 0.0)
    q_decay = jnp.exp(log_gamma[:, None] * (idx + 1.0))
    k_decay = jnp.exp(log_gamma[:, None] * (RET_CHUNK - 1.0 - idx))
    chunk_decay = jnp.exp(log_gamma * RET_CHUNK)

    def to_chunks(t):
        return t.reshape(b, h, nc, RET_CHUNK, t.shape[-1]).transpose(2, 0, 1, 3, 4)

    def step(state, args):
        q_i, k_i, v_i = args
        sc = jnp.einsum('bhqd,bhkd->bhqk', q_i, k_i) * inner_decay[None]
        inner = jnp.einsum('bhqk,bhkv->bhqv', sc, v_i)
        cross = jnp.einsum('bhqd,bhdv->bhqv', q_i, state) * q_decay[None, :, :, None]
        new_state = state * chunk_decay[None, :, None, None] + jnp.einsum(
            'bhkd,bhkv->bhdv', k_i * k_decay[None, :, :, None], v_i)
        return new_state, inner + cross

    state0 = jnp.zeros((b, h, dk, dv), jnp.float32)
    _, out = lax.scan(step, state0, (to_chunks(q), to_chunks(k), to_chunks(v)))
    return out.transpose(1, 2, 0, 3, 4).reshape(b, h, s, dv).astype(out_dtype)


def _hybrid_mixer(h, w_in, gate_b, fox_fb, lam_q1, lam_k1, lam_q2, lam_k2, diff_subln_g,
                  w_branch, w_out, lam_init):
    s = h.shape[1]
    proj = jnp.einsum('bsd,dc->bsc', h, w_in)
    (dq, dk_, dv, rq, rk, rv, rg, fq, fk, fv, f_logit, gates) = _split_columns(proj, IN_SPLITS)

    q = _heads(dq, DIFF_HEADS)
    k = _heads(dk_, DIFF_HEADS)
    v = _heads(dv, DIFF_HEADS)
    a1 = _causal_attention_blocked(q[..., :HEAD_DIM], k[..., :HEAD_DIM], v)
    a2 = _causal_attention_blocked(q[..., HEAD_DIM:], k[..., HEAD_DIM:], v)
    lam = (jnp.exp(jnp.sum(lam_q1.astype(jnp.float32) * lam_k1.astype(jnp.float32)))
           - jnp.exp(jnp.sum(lam_q2.astype(jnp.float32) * lam_k2.astype(jnp.float32)))
           + lam_init)
    o_diff = a1 - lam.astype(a1.dtype) * a2
    o_diff = _merge_heads(_rms_norm(o_diff, diff_subln_g) * (1.0 - lam_init))

    pos = jnp.arange(s)
    q = _rotary(_heads(rq, RET_HEADS), pos)
    k = _rotary(_heads(rk, RET_HEADS), pos) * (RET_HEAD_DIM ** -0.5)
    v = _heads(rv, RET_HEADS)
    log_gamma = jnp.log1p(-jnp.exp2(-RET_GAMMA_EXP0 - jnp.arange(RET_HEADS, dtype=jnp.float32)))
    o_ret = _merge_heads(_group_norm_heads(_retention_chunkwise(q, k, v, log_gamma)))
    o_ret = o_ret.astype(h.dtype) * jax.nn.silu(rg)

    q = _heads(fq, FOX_HEADS)
    k = _heads(fk, FOX_HEADS)
    v = _heads(fv, FOX_HEADS)
    log_f = jax.nn.log_sigmoid((f_logit + fox_fb).astype(jnp.float32))
    fcum = jnp.cumsum(log_f, axis=1).transpose(0, 2, 1)
    o_fox = _merge_heads(_causal_attention_blocked(q, k, v, fcum))

    g = jax.nn.sigmoid(gates + gate_b)
    g_diff, g_ret, g_fox = jnp.split(g, N_BRANCHES, axis=-1)
    merged = (g_diff * jnp.einsum('bsw,wd->bsd', o_diff, w_branch[0])
              + g_ret * jnp.einsum('bsw,wd->bsd', o_ret, w_branch[1])
              + g_fox * jnp.einsum('bsw,wd->bsd', o_fox, w_branch[2]))
    return jnp.einsum('bsd,de->bse', merged, w_out)


def _moe_ffn(h, router_w, router_b, w_up, b_up, w_down, b_down):
    b, s, d = h.shape
    n = b * s
    hf = h.reshape(n, d)
    logits = (hf @ router_w + router_b).astype(jnp.float32)
    top_logits, top_idx = lax.top_k(logits, TOP_K)
    top_w = jax.nn.softmax(top_logits, axis=-1)
    flat_e = top_idx.reshape(-1)
    order = jnp.argsort(flat_e)
    sorted_e = flat_e[order]
    token_of = order // TOP_K
    xs = hf[token_of]
    group_sizes = jnp.bincount(flat_e, length=N_EXPERTS).astype(jnp.int32)
    up = lax.ragged_dot(xs, w_up, group_sizes) + b_up[sorted_e]
    glu = jnp.minimum(up[:, 0::2], SWIGLU_LIMIT)
    lin = jnp.clip(up[:, 1::2], -SWIGLU_LIMIT, SWIGLU_LIMIT)
    act = glu * jax.nn.sigmoid(SWIGLU_ALPHA * glu) * (lin + 1.0)
    down = lax.ragged_dot(act, w_down, group_sizes) + b_down[sorted_e]
    weighted = down * top_w.reshape(-1)[order][:, None].astype(down.dtype)
    out = jax.ops.segment_sum(weighted, token_of, num_segments=n)
    return out.reshape(b, s, d)


def setup_inputs(seed: int = 0) -> dict:
    key = jax.random.key(seed)
    ks = jax.random.split(key, 24)
    f32 = jnp.float32

    def nrm(k, shape, scale):
        return jax.random.normal(k, shape, f32) * scale

    return {
        "x": nrm(ks[0], (BATCH, SEQ, D_MODEL), 1.0),
        "c": nrm(ks[1], (BATCH, D_MODEL), 1.0),
        "norm1_g": 1.0 + nrm(ks[2], (DEPTH, D_MODEL), 0.02),
        "norm2_g": 1.0 + nrm(ks[3], (DEPTH, D_MODEL), 0.02),
        "ada_w": nrm(ks[4], (DEPTH, D_MODEL, 6 * D_MODEL), 0.5 * D_MODEL ** -0.5),
        "ada_b": nrm(ks[5], (DEPTH, 6 * D_MODEL), 0.02),
        "w_in": nrm(ks[6], (DEPTH, D_MODEL, IN_COLS), D_MODEL ** -0.5),
        "gate_b": nrm(ks[7], (DEPTH, N_BRANCHES * D_MODEL), 0.02),
        "fox_fb": FOX_BIAS_INIT + nrm(ks[8], (DEPTH, FOX_HEADS), 0.5),
        "lam_q1": nrm(ks[9], (DEPTH, HEAD_DIM), 0.1),
        "lam_k1": nrm(ks[10], (DEPTH, HEAD_DIM), 0.1),
        "lam_q2": nrm(ks[11], (DEPTH, HEAD_DIM), 0.1),
        "lam_k2": nrm(ks[12], (DEPTH, HEAD_DIM), 0.1),
        "diff_subln_g": 1.0 + nrm(ks[13], (DEPTH, 2 * HEAD_DIM), 0.02),
        "w_branch": nrm(ks[14], (DEPTH, N_BRANCHES, BRANCH_WIDTH, D_MODEL), BRANCH_WIDTH ** -0.5),
        "w_out": nrm(ks[15], (DEPTH, D_MODEL, D_MODEL), D_MODEL ** -0.5),
        "router_w": nrm(ks[16], (DEPTH, D_MODEL, N_EXPERTS), D_MODEL ** -0.5),
        "router_b": nrm(ks[17], (DEPTH, N_EXPERTS), 0.01),
        "exp_w_up": nrm(ks[18], (DEPTH, N_EXPERTS, D_MODEL, 2 * D_FF), D_MODEL ** -0.5),
        "exp_b_up": nrm(ks[19], (DEPTH, N_EXPERTS, 2 * D_FF), 0.02),
        "exp_w_down": nrm(ks[20], (DEPTH, N_EXPERTS, D_FF, D_MODEL), D_FF ** -0.5),
        "exp_b_down": nrm(ks[21], (DEPTH, N_EXPERTS, D_MODEL), 0.02),
        "final_g": 1.0 + nrm(ks[22], (D_MODEL,), 0.02),
    }


def reference(x, c, norm1_g, norm2_g, ada_w, ada_b, w_in, gate_b, fox_fb, lam_q1, lam_k1,
              lam_q2, lam_k2, diff_subln_g, w_branch, w_out, router_w, router_b,
              exp_w_up, exp_b_up, exp_w_down, exp_b_down, final_g):
    c_act = jax.nn.silu(c)
    for l in range(DEPTH):
        lam_init = 0.8 - 0.6 * math.exp(-0.3 * l)
        mod = c_act @ ada_w[l] + ada_b[l]
        sh1, sc1, g1, sh2, sc2, g2 = jnp.split(mod, 6, axis=-1)
        h = _modulate(_rms_norm(x, norm1_g[l]), sh1, sc1)
        mix = _hybrid_mixer(h, w_in[l], gate_b[l], fox_fb[l], lam_q1[l], lam_k1[l], lam_q2[l],
                            lam_k2[l], diff_subln_g[l], w_branch[l], w_out[l], lam_init)
        x = x + g1[:, None, :] * mix
        h = _modulate(_rms_norm(x, norm2_g[l]), sh2, sc2)
        ffn = _moe_ffn(h, router_w[l], router_b[l], exp_w_up[l], exp_b_up[l],
                       exp_w_down[l], exp_b_down[l])
        x = x + g2[:, None, :] * ffn
    return _rms_norm(x, final_g)
```

```python
import functools
import math

import jax
import jax.numpy as jnp
from jax import lax
from jax.experimental import pallas as pl
from jax.experimental.pallas import tpu as pltpu

F32 = jnp.float32
BF16 = jnp.bfloat16
I32 = jnp.int32
U32 = jnp.uint32

D_MODEL = 1024
HEAD_DIM = 64
BRANCH_WIDTH = D_MODEL // 2
N_BRANCHES = 3
FOX_HEADS = BRANCH_WIDTH // HEAD_DIM
RET_HEADS = 4
RET_HEAD_DIM = 128
ROPE_BASE = 10000.0
RET_GAMMA_EXP0 = 5.0
N_EXPERTS = 32
TOP_K = 4
D_FF = D_MODEL
SWIGLU_LIMIT = 7.0
SWIGLU_ALPHA = 1.702
NORM_EPS = 1e-5
NEG_INF = -1e30

LANES = 128
MAIN_COLS = 8192
GATE_COL0 = 10 * BRANCH_WIDTH
HALF = D_MODEL // 2
VMEM_LIMIT = 48 * 1024 * 1024

ROW_TILE_IN = 512
COL_TILE_IN = 1024
ATTN_TILE = 256
RET_CHUNK = 256
ROW_TILE_OUT = 256
EXPERT_TILE = 256
CUM_CHUNK = 256


def _cparams(sem):
    return pltpu.CompilerParams(dimension_semantics=sem, vmem_limit_bytes=VMEM_LIMIT)


def _sigmoid(x):
    return 1.0 / (1.0 + jnp.exp(-x))


def _pack_pairs(v):
    u = lax.bitcast_convert_type(v.astype(BF16).astype(F32), U32)
    half = v.shape[-1] // 2
    w = (u[:, half:] & jnp.uint32(0xFFFF0000)) | (u[:, :half] >> 16)
    return lax.bitcast_convert_type(w, I32)


def _unpack_pairs(w):
    u = lax.bitcast_convert_type(w, U32)
    lo = lax.bitcast_convert_type(u << 16, F32)
    hi = lax.bitcast_convert_type(u & jnp.uint32(0xFFFF0000), F32)
    return lo, hi


def _ada_kernel(c_ref, w_ref, b_ref, o_ref):
    c = c_ref[...]
    ca = c * _sigmoid(c)
    o_ref[0] = jnp.dot(ca, w_ref[0], preferred_element_type=F32) + b_ref[0]


def _ada_mod(c, ada_w, ada_b):
    depth, d, cols = ada_w.shape
    b = c.shape[0]
    tn = 1024
    return pl.pallas_call(
        _ada_kernel,
        grid=(depth, cols // tn),
        in_specs=[
            pl.BlockSpec((b, d), lambda l, j: (0, 0)),
            pl.BlockSpec((1, d, tn), lambda l, j: (l, 0, j)),
            pl.BlockSpec((1, 1, tn), lambda l, j: (l, 0, j)),
        ],
        out_specs=pl.BlockSpec((1, b, tn), lambda l, j: (l, 0, j)),
        out_shape=jax.ShapeDtypeStruct((depth, b, cols), F32),
        compiler_params=_cparams(("arbitrary", "arbitrary")),
        name="ada_mod",
    )(c, ada_w, ada_b.reshape(depth, 1, cols))


def _inproj_kernel(x_ref, g_ref, sh_ref, sc_ref, w_ref, wf_ref, proj_ref, fl_ref, h_scr):
    @pl.when(pl.program_id(1) == 0)
    def _():
        x = x_ref[...]
        r = lax.rsqrt(jnp.mean(x * x, axis=-1, keepdims=True) + NORM_EPS)
        h = x * r * g_ref[...]
        h = h * (1.0 + sc_ref[0]) + sh_ref[0]
        hb = h.astype(BF16)
        h_scr[...] = hb
        fl_ref[...] = jnp.dot(hb, wf_ref[...], preferred_element_type=F32)

    proj_ref[...] = jnp.dot(h_scr[...], w_ref[...], preferred_element_type=F32).astype(BF16)


def _inproj(x2, g, sh, sc, w_main, w_f, seq):
    n, d = x2.shape
    tm = min(ROW_TILE_IN, seq)
    tn = COL_TILE_IN
    per_seq = seq // tm
    return pl.pallas_call(
        _inproj_kernel,
        grid=(n // tm, MAIN_COLS // tn),
        in_specs=[
            pl.BlockSpec((tm, d), lambda i, j: (i, 0)),
            pl.BlockSpec((1, d), lambda i, j: (0, 0)),
            pl.BlockSpec((1, 1, d), lambda i, j: (i // per_seq, 0, 0)),
            pl.BlockSpec((1, 1, d), lambda i, j: (i // per_seq, 0, 0)),
            pl.BlockSpec((d, tn), lambda i, j: (0, j)),
            pl.BlockSpec((d, LANES), lambda i, j: (0, 0)),
        ],
        out_specs=[
            pl.BlockSpec((tm, tn), lambda i, j: (i, j)),
            pl.BlockSpec((tm, LANES), lambda i, j: (i, 0)),
        ],
        out_shape=[
            jax.ShapeDtypeStruct((n, MAIN_COLS), BF16),
            jax.ShapeDtypeStruct((n, LANES), F32),
        ],
        scratch_shapes=[pltpu.VMEM((tm, d), BF16)],
        compiler_params=_cparams(("arbitrary", "arbitrary")),
        name="inproj",
    )(x2, g, sh, sc, w_main, w_f)


def _split3(v):
    hi = v.astype(BF16)
    r1 = v - hi.astype(F32)
    mid = r1.astype(BF16)
    lo = (r1 - mid.astype(F32)).astype(BF16)
    return hi, mid, lo


def _fcum_kernel(fl_ref, fb_ref, o_ref, *, seq):
    c = min(CUM_CHUNK, seq)
    row = lax.broadcasted_iota(I32, (c, c), 0)
    col = lax.broadcasted_iota(I32, (c, c), 1)
    lower = (col <= row).astype(BF16)
    carry = jnp.zeros((1, LANES), F32)
    for ci in range(seq // c):
        x = fl_ref[pl.ds(ci * c, c), :] + fb_ref[...]
        lf = jnp.minimum(x, 0.0) - jnp.log(1.0 + jnp.exp(-jnp.abs(x)))
        hi, mid, lo = _split3(lf)
        cs = (jnp.dot(lower, hi, preferred_element_type=F32)
              + jnp.dot(lower, mid, preferred_element_type=F32)
              + jnp.dot(lower, lo, preferred_element_type=F32)) + carry
        o_ref[pl.ds(ci * c, c), :] = cs
        carry = cs[c - 1:c, :]


def _fcum(flog, fb_pad, batch, seq):
    n = flog.shape[0]
    return pl.pallas_call(
        functools.partial(_fcum_kernel, seq=seq),
        grid=(batch,),
        in_specs=[
            pl.BlockSpec((seq, LANES), lambda b: (b, 0)),
            pl.BlockSpec((1, LANES), lambda b: (0, 0)),
        ],
        out_specs=pl.BlockSpec((seq, LANES), lambda b: (b, 0)),
        out_shape=jax.ShapeDtypeStruct((n, LANES), F32),
        compiler_params=_cparams(("arbitrary",)),
        name="fox_decay_cumsum",
    )(flog, fb_pad)


def _attn_kernel(*refs, mode, t, lam_init):
    if mode == "diff":
        q_ref, k_ref, v_ref, lam_ref, g_ref, o_ref, m_scr, l_scr, acc_scr = refs
        f_ref = None
    else:
        q_ref, k_ref, v_ref, f_ref, o_ref, m_scr, l_scr, acc_scr = refs
    qi = pl.program_id(2)
    q = q_ref[...] * jnp.asarray(HEAD_DIM ** -0.5, BF16)
    lane = lax.broadcasted_iota(I32, (t, LANES), 1)
    zero = jnp.zeros_like(q)
    qs = (jnp.where(lane < HEAD_DIM, q, zero), jnp.where(lane >= HEAD_DIM, q, zero))

    m_scr[...] = jnp.full(m_scr.shape, NEG_INF, F32)
    l_scr[...] = jnp.zeros(l_scr.shape, F32)
    acc_scr[...] = jnp.zeros(acc_scr.shape, F32)

    def step(j, masked):
        sl = pl.ds(pl.multiple_of(j * t, t), t)
        kj = k_ref[sl, :]
        vj = v_ref[sl, :]
        for idx in range(2):
            s = lax.dot_general(qs[idx], kj, (((1,), (1,)), ((), ())), preferred_element_type=F32)
            if f_ref is not None:
                s = s - f_ref[0, idx:idx + 1, sl]
            if masked:
                r = lax.broadcasted_iota(I32, (t, t), 0)
                c = lax.broadcasted_iota(I32, (t, t), 1)
                s = jnp.where(c <= r, s, NEG_INF)
            m_old = m_scr[idx]
            m_new = jnp.maximum(m_old, jnp.max(s, axis=-1, keepdims=True))
            alpha = jnp.exp(m_old - m_new)
            p = jnp.exp(s - m_new)
            l_scr[idx] = alpha * l_scr[idx] + jnp.sum(p, axis=-1, keepdims=True)
            acc_scr[idx] = alpha * acc_scr[idx] + jnp.dot(p.astype(BF16), vj, preferred_element_type=F32)
            m_scr[idx] = m_new

    def body(j, carry):
        step(j, False)
        return carry

    lax.fori_loop(0, qi, body, 0)
    step(qi, True)

    a = acc_scr[0] / l_scr[0]
    b = acc_scr[1] / l_scr[1]
    if mode == "diff":
        lv = lam_ref[...]
        lam = (jnp.exp(jnp.sum(lv[0:1, :] * lv[1:2, :])) - jnp.exp(jnp.sum(lv[2:3, :] * lv[3:4, :]))
               + lam_init)
        o = a - lam * b
        r = lax.rsqrt(jnp.mean(o * o, axis=-1, keepdims=True) + NORM_EPS)
        o = o * r * g_ref[...] * (1.0 - lam_init)
    else:
        o = jnp.where(lane < HEAD_DIM, a, b)
    o_ref[...] = o.astype(BF16)


def _attention(proj, mode, col_q, col_k, col_v, batch, seq, extra, lam_init=0.0):
    n = proj.shape[0]
    t = min(ATTN_TILE, seq)
    nq = seq // t
    n_blk = BRANCH_WIDTH // LANES
    in_specs = [
        pl.BlockSpec((t, LANES), lambda b, h, i: (b * nq + i, col_q + h)),
        pl.BlockSpec((seq, LANES), lambda b, h, i: (b, col_k + h)),
        pl.BlockSpec((seq, LANES), lambda b, h, i: (b, col_v + h)),
    ]
    if mode == "diff":
        lamv, g = extra
        in_specs += [pl.BlockSpec((4, HEAD_DIM), lambda b, h, i: (0, 0)),
                     pl.BlockSpec((1, LANES), lambda b, h, i: (0, 0))]
        args = (proj, proj, proj, lamv, g)
    else:
        (frow,) = extra
        in_specs += [pl.BlockSpec((1, 2, seq), lambda b, h, i: (b * n_blk + h, 0, 0))]
        args = (proj, proj, proj, frow)
    return pl.pallas_call(
        functools.partial(_attn_kernel, mode=mode, t=t, lam_init=lam_init),
        grid=(batch, n_blk, nq),
        in_specs=in_specs,
        out_specs=pl.BlockSpec((t, LANES), lambda b, h, i: (b * nq + i, h)),
        out_shape=jax.ShapeDtypeStruct((n, BRANCH_WIDTH), BF16),
        scratch_shapes=[pltpu.VMEM((2, t, 1), F32), pltpu.VMEM((2, t, 1), F32),
                        pltpu.VMEM((2, t, LANES), F32)],
        compiler_params=_cparams(("arbitrary", "arbitrary", "arbitrary")),
        name="attn_" + mode,
    )(*args)


def _ret_kernel(q_ref, k_ref, v_ref, g_ref, cos_ref, sin_ref, dm_ref, qd_ref, kd_ref, cd_ref,
                o_ref, st_scr, *, seq, c):
    st_scr[...] = jnp.zeros(st_scr.shape, F32)
    half = RET_HEAD_DIM // 2

    def body(ci, carry):
        sl = pl.ds(pl.multiple_of(ci * c, c), c)
        cos = cos_ref[sl, :]
        sin = sin_ref[sl, :]
        q = q_ref[sl, :].astype(F32)
        k = k_ref[sl, :].astype(F32)
        qr = q * cos + pltpu.roll(q, half, 1) * sin
        kr = (k * cos + pltpu.roll(k, half, 1) * sin) * (RET_HEAD_DIM ** -0.5)
        v = v_ref[sl, :]
        sc = lax.dot_general(qr.astype(BF16), kr.astype(BF16), (((1,), (1,)), ((), ())),
                             preferred_element_type=F32) * dm_ref[0]
        inner = jnp.dot(sc.astype(BF16), v, preferred_element_type=F32)
        st = st_scr[...]
        cross = jnp.dot((qr * qd_ref[0]).astype(BF16), st.astype(BF16), preferred_element_type=F32)
        o = inner + cross
        kt = (kr * kd_ref[0]).T.astype(BF16)
        st_scr[...] = st * cd_ref[0] + jnp.dot(kt, v, preferred_element_type=F32)
        mu = jnp.mean(o, axis=-1, keepdims=True)
        oc = o - mu
        var = jnp.mean(oc * oc, axis=-1, keepdims=True)
        on = oc * lax.rsqrt(var + NORM_EPS)
        g = g_ref[sl, :].astype(F32)
        o_ref[sl, :] = (on * (g * _sigmoid(g))).astype(BF16)
        return carry

    lax.fori_loop(0, seq // c, body, 0)


def _retention(proj, tables, batch, seq):
    n = proj.shape[0]
    c = min(RET_CHUNK, seq)
    cos, sin, dm, qd, kd, cd = tables
    col = BRANCH_WIDTH * 3 // LANES
    step = BRANCH_WIDTH // LANES

    def pspec(k):
        return pl.BlockSpec((seq, LANES), lambda b, h: (b, col + k * step + h))

    return pl.pallas_call(
        functools.partial(_ret_kernel, seq=seq, c=c),
        grid=(batch, RET_HEADS),
        in_specs=[
            pspec(0), pspec(1), pspec(2), pspec(3),
            pl.BlockSpec((seq, LANES), lambda b, h: (0, 0)),
            pl.BlockSpec((seq, LANES), lambda b, h: (0, 0)),
            pl.BlockSpec((1, c, c), lambda b, h: (h, 0, 0)),
            pl.BlockSpec((1, c, LANES), lambda b, h: (h, 0, 0)),
            pl.BlockSpec((1, c, LANES), lambda b, h: (h, 0, 0)),
            pl.BlockSpec((1, LANES, LANES), lambda b, h: (h, 0, 0)),
        ],
        out_specs=pl.BlockSpec((seq, LANES), lambda b, h: (b, h)),
        out_shape=jax.ShapeDtypeStruct((n, BRANCH_WIDTH), BF16),
        scratch_shapes=[pltpu.VMEM((RET_HEAD_DIM, RET_HEAD_DIM), F32)],
        compiler_params=_cparams(("arbitrary", "arbitrary")),
        name="retention",
    )(proj, proj, proj, proj, cos, sin, dm, qd, kd, cd)


def _retention_tables(seq):
    c = min(RET_CHUNK, seq)
    d = RET_HEAD_DIM
    inv = ROPE_BASE ** (-jnp.arange(0, d, 2, dtype=F32) / d)
    ang = jnp.arange(seq).astype(F32)[:, None] * inv[None, :]
    cos = jnp.cos(ang)
    sin = jnp.sin(ang)
    cos_full = jnp.concatenate([cos, cos], axis=-1)
    sin_signed = jnp.concatenate([-sin, sin], axis=-1)
    log_gamma = jnp.log1p(-jnp.exp2(-RET_GAMMA_EXP0 - jnp.arange(RET_HEADS, dtype=F32)))
    idx = jnp.arange(c, dtype=F32)
    dist = idx[:, None] - idx[None, :]
    dm = jnp.where(dist >= 0, jnp.exp(log_gamma[:, None, None] * jnp.maximum(dist, 0.0)), 0.0)
    qd = jnp.exp(log_gamma[:, None] * (idx + 1.0))
    kd = jnp.exp(log_gamma[:, None] * (c - 1.0 - idx))
    cd = jnp.exp(log_gamma * c)
    qd = jnp.broadcast_to(qd[:, :, None], (RET_HEADS, c, LANES))
    kd = jnp.broadcast_to(kd[:, :, None], (RET_HEADS, c, LANES))
    cd = jnp.broadcast_to(cd[:, None, None], (RET_HEADS, LANES, LANES))
    return cos_full, sin_signed, dm, qd, kd, cd


def _merge_kernel(od_ref, or_ref, of_ref, gd_ref, gr_ref, gf_ref, gb_ref, wb_ref, wo_ref, x_ref,
                  g1_ref, n2_ref, sh_ref, sc_ref, rw_ref, rb_ref,
                  xo_ref, hp_ref, ti_ref, tw_ref, *, tm):
    merged = None
    for i, (o_ref, gate_ref) in enumerate(((od_ref, gd_ref), (or_ref, gr_ref), (of_ref, gf_ref))):
        gate = _sigmoid(gate_ref[...].astype(F32) + gb_ref[i])
        term = gate * jnp.dot(o_ref[...], wb_ref[i], preferred_element_type=F32)
        merged = term if merged is None else merged + term
    mix = jnp.dot(merged.astype(BF16), wo_ref[...], preferred_element_type=F32)
    xn = x_ref[...] + g1_ref[0] * mix
    xo_ref[...] = xn

    r = lax.rsqrt(jnp.mean(xn * xn, axis=-1, keepdims=True) + NORM_EPS)
    h = xn * r * n2_ref[...]
    h = h * (1.0 + sc_ref[0]) + sh_ref[0]
    hp_ref[...] = _pack_pairs(h)

    hh, hm, _ = _split3(h)
    wh, wm, _ = _split3(rw_ref[...])
    logits = (jnp.dot(hh, wh, preferred_element_type=F32)
              + jnp.dot(hh, wm, preferred_element_type=F32)
              + jnp.dot(hm, wh, preferred_element_type=F32)) + rb_ref[...]

    lane = lax.broadcasted_iota(I32, (tm, LANES), 1)
    vals, idxs = [], []
    for _ in range(TOP_K):
        m = jnp.max(logits, axis=-1, keepdims=True)
        idx = jnp.min(jnp.where(logits == m, lane, LANES), axis=-1, keepdims=True)
        vals.append(m)
        idxs.append(idx)
        logits = jnp.where(lane == idx, -3.0e38, logits)
    es = [jnp.exp(v - vals[0]) for v in vals]
    den = es[0] + es[1] + es[2] + es[3]
    ti = jnp.zeros((tm, LANES), I32)
    tw = jnp.zeros((tm, LANES), F32)
    for k in range(TOP_K):
        ti = jnp.where(lane == k, idxs[k], ti)
        tw = jnp.where(lane == k, es[k] / den, tw)
    ti_ref[...] = ti
    tw_ref[...] = tw


def _merge(o_diff, o_ret, o_fox, proj, gate_b, w_branch, w_out, x2, g1, n2g, sh2, sc2, rw, rb, seq):
    n, d = x2.shape
    tm = min(ROW_TILE_OUT, seq)
    per_seq = seq // tm
    gcol = GATE_COL0 // d

    def rows(w):
        return pl.BlockSpec((tm, w), lambda i: (i, 0))

    def gspec(k):
        return pl.BlockSpec((tm, d), lambda i: (i, gcol + k))

    def whole(shape):
        return pl.BlockSpec(shape, lambda i: (0,) * len(shape))

    def per_b():
        return pl.BlockSpec((1, 1, d), lambda i: (i // per_seq, 0, 0))

    return pl.pallas_call(
        functools.partial(_merge_kernel, tm=tm),
        grid=(n // tm,),
        in_specs=[
            rows(BRANCH_WIDTH), rows(BRANCH_WIDTH), rows(BRANCH_WIDTH),
            gspec(0), gspec(1), gspec(2),
            whole((N_BRANCHES, 1, d)), whole((N_BRANCHES, BRANCH_WIDTH, d)), whole((d, d)),
            rows(d), per_b(), whole((1, d)), per_b(), per_b(),
            whole((d, LANES)), whole((1, LANES)),
        ],
        out_specs=[rows(d), rows(HALF), rows(LANES), rows(LANES)],
        out_shape=[
            jax.ShapeDtypeStruct((n, d), F32),
            jax.ShapeDtypeStruct((n, HALF), I32),
            jax.ShapeDtypeStruct((n, LANES), I32),
            jax.ShapeDtypeStruct((n, LANES), F32),
        ],
        compiler_params=_cparams(("arbitrary",)),
        name="merge_outproj_router",
    )(o_diff, o_ret, o_fox, proj, proj, proj, gate_b, w_branch, w_out, x2, g1, n2g, sh2, sc2, rw, rb)


def _expert_kernel(te_ref, nu_ref, xs_ref, wu_ref, bu_ref, wd_ref, bd_ref, y_ref):
    @pl.when(pl.program_id(0) < nu_ref[0])
    def _():
        lo, hi = _unpack_pairs(xs_ref[...])
        up = (jnp.dot(lo.astype(BF16), wu_ref[0, :HALF, :], preferred_element_type=F32)
              + jnp.dot(hi.astype(BF16), wu_ref[0, HALF:, :], preferred_element_type=F32)) + bu_ref[0]
        glu = jnp.minimum(up[:, :D_FF], SWIGLU_LIMIT)
        lin = jnp.clip(up[:, D_FF:], -SWIGLU_LIMIT, SWIGLU_LIMIT)
        act = glu * _sigmoid(SWIGLU_ALPHA * glu) * (lin + 1.0)
        down = jnp.dot(act.astype(BF16), wd_ref[0], preferred_element_type=F32) + bd_ref[0]
        y_ref[...] = _pack_pairs(down)

    @pl.when(pl.program_id(0) >= nu_ref[0])
    def _():
        y_ref[...] = jnp.zeros(y_ref.shape, I32)


def _experts(tile_e, n_used, xs, w_up, b_up, w_down, b_down):
    r = xs.shape[0]
    tm = EXPERT_TILE
    d = D_MODEL
    grid_spec = pltpu.PrefetchScalarGridSpec(
        num_scalar_prefetch=2,
        grid=(r // tm,),
        in_specs=[
            pl.BlockSpec((tm, HALF), lambda i, te, nu: (i, 0)),
            pl.BlockSpec((1, d, 2 * D_FF), lambda i, te, nu: (te[i], 0, 0)),
            pl.BlockSpec((1, 1, 2 * D_FF), lambda i, te, nu: (te[i], 0, 0)),
            pl.BlockSpec((1, D_FF, d), lambda i, te, nu: (te[i], 0, 0)),
            pl.BlockSpec((1, 1, d), lambda i, te, nu: (te[i], 0, 0)),
        ],
        out_specs=pl.BlockSpec((tm, HALF), lambda i, te, nu: (i, 0)),
    )
    return pl.pallas_call(
        _expert_kernel,
        grid_spec=grid_spec,
        out_shape=jax.ShapeDtypeStruct((r, HALF), I32),
        compiler_params=_cparams(("arbitrary",)),
        name="expert_mlp",
    )(tile_e, n_used, xs, w_up, b_up, w_down, b_down)


def _combine_kernel(y_ref, tw_ref, x_ref, g2_ref, fg_ref, o_ref, *, final):
    tw = tw_ref[...]
    acc_lo = None
    for k in range(TOP_K):
        lo, hi = _unpack_pairs(y_ref[:, k * HALF:(k + 1) * HALF])
        w = tw[:, k:k + 1]
        acc_lo = w * lo if acc_lo is None else acc_lo + w * lo
        acc_hi = w * hi if k == 0 else acc_hi + w * hi
    ffn = jnp.concatenate([acc_lo, acc_hi], axis=-1)
    xn = x_ref[...] + g2_ref[0] * ffn
    if final:
        r = lax.rsqrt(jnp.mean(xn * xn, axis=-1, keepdims=True) + NORM_EPS)
        xn = xn * r * fg_ref[...]
    o_ref[...] = xn


def _combine(ytok, top_w, x2, g2, final_g, seq, final):
    n, d = x2.shape
    tm = min(ROW_TILE_OUT, seq)
    per_seq = seq // tm
    return pl.pallas_call(
        functools.partial(_combine_kernel, final=final),
        grid=(n // tm,),
        in_specs=[
            pl.BlockSpec((tm, TOP_K * HALF), lambda i: (i, 0)),
            pl.BlockSpec((tm, LANES), lambda i: (i, 0)),
            pl.BlockSpec((tm, d), lambda i: (i, 0)),
            pl.BlockSpec((1, 1, d), lambda i: (i // per_seq, 0, 0)),
            pl.BlockSpec((1, d), lambda i: (0, 0)),
        ],
        out_specs=pl.BlockSpec((tm, d), lambda i: (i, 0)),
        out_shape=jax.ShapeDtypeStruct((n, d), F32),
        compiler_params=_cparams(("arbitrary",)),
        name="moe_combine",
    )(ytok, top_w, x2, g2, final_g)


def _route(top_idx, n_rows):
    tm = EXPERT_TILE
    flat_e = top_idx.reshape(-1)
    onehot = (flat_e[:, None] == jnp.arange(N_EXPERTS, dtype=I32)[None, :]).astype(I32)
    csum = jnp.cumsum(onehot, axis=0)
    rank = jnp.take_along_axis(csum, flat_e[:, None], axis=1)[:, 0] - 1
    sizes = csum[-1]
    padded = ((sizes + tm - 1) // tm) * tm
    pend = jnp.cumsum(padded)
    pstart = pend - padded
    pos = pstart[flat_e] + rank
    tiles = jnp.arange(n_rows // tm, dtype=I32) * tm
    tile_e = jnp.minimum(jnp.searchsorted(pend, tiles, side="right"), N_EXPERTS - 1).astype(I32)
    n_used = (pend[-1] // tm).astype(I32).reshape(1)
    return pos.astype(I32), tile_e, n_used


def kernel(x, c, norm1_g, norm2_g, ada_w, ada_b, w_in, gate_b, fox_fb, lam_q1, lam_k1, lam_q2, lam_k2,
           diff_subln_g, w_branch, w_out, router_w, router_b, exp_w_up, exp_b_up, exp_w_down,
           exp_b_down, final_g):
    batch, seq, d = x.shape
    depth = ada_w.shape[0]
    n = batch * seq
    x2 = x.reshape(n, d)

    mod = _ada_mod(c, ada_w, ada_b)
    tables = _retention_tables(seq)
    n_rows = n * TOP_K + N_EXPERTS * EXPERT_TILE
    blk = BRANCH_WIDTH // LANES

    for l in range(depth):
        lam_init = 0.8 - 0.6 * math.exp(-0.3 * l)
        sh1, sc1, g1, sh2, sc2, g2 = [m.reshape(batch, 1, d) for m in jnp.split(mod[l], 6, axis=-1)]

        wl = w_in[l]
        w_main = jnp.concatenate([wl[:, :GATE_COL0], wl[:, GATE_COL0 + FOX_HEADS:]], axis=1).astype(BF16)
        w_f = jnp.pad(wl[:, GATE_COL0:GATE_COL0 + FOX_HEADS], ((0, 0), (0, LANES - FOX_HEADS))).astype(BF16)
        proj, flog = _inproj(x2, norm1_g[l].reshape(1, d), sh1, sc1, w_main, w_f, seq)

        fb_pad = jnp.pad(fox_fb[l], (0, LANES - FOX_HEADS)).reshape(1, LANES)
        fc = _fcum(flog, fb_pad, batch, seq)
        frow = fc.reshape(batch, seq, LANES)[:, :, :FOX_HEADS].transpose(0, 2, 1)
        frow = frow.reshape(batch * blk, 2, seq)

        lamv = jnp.stack([lam_q1[l], lam_k1[l], lam_q2[l], lam_k2[l]])
        o_diff = _attention(proj, "diff", 0, blk, 2 * blk, batch, seq,
                            (lamv, diff_subln_g[l].reshape(1, LANES)), lam_init)
        o_ret = _retention(proj, tables, batch, seq)
        o_fox = _attention(proj, "fox", 7 * blk, 8 * blk, 9 * blk, batch, seq, (frow,))

        rw = jnp.pad(router_w[l], ((0, 0), (0, LANES - N_EXPERTS)))
        rb = jnp.pad(router_b[l], (0, LANES - N_EXPERTS), constant_values=NEG_INF).reshape(1, LANES)
        x2, h2p, top_idx, top_w = _merge(
            o_diff, o_ret, o_fox, proj, gate_b[l].reshape(N_BRANCHES, 1, d), w_branch[l].astype(BF16),
            w_out[l].astype(BF16), x2, g1, norm2_g[l].reshape(1, d), sh2, sc2, rw, rb, seq)

        pos, tile_e, n_used = _route(top_idx[:, :TOP_K], n_rows)
        xs = jnp.zeros((n_rows, HALF), I32).at[pos].set(jnp.repeat(h2p, TOP_K, axis=0))

        wu = exp_w_up[l]
        w_up = jnp.concatenate([wu[:, :, 0::2], wu[:, :, 1::2]], axis=-1).astype(BF16)
        bu = exp_b_up[l]
        b_up = jnp.concatenate([bu[:, 0::2], bu[:, 1::2]], axis=-1).reshape(N_EXPERTS, 1, 2 * D_FF)
        y = _experts(tile_e, n_used, xs, w_up, b_up, exp_w_down[l].astype(BF16),
                     exp_b_down[l].reshape(N_EXPERTS, 1, d))
        ytok = y[pos].reshape(n, TOP_K * HALF)
        x2 = _combine(ytok, top_w, x2, g2, final_g.reshape(1, d), seq, final=(l == depth - 1))

    return x2.reshape(batch, seq, d)
```

```python
import functools
import math

import jax
import jax.numpy as jnp
from jax import lax
from jax.experimental import pallas as pl
from jax.experimental.pallas import tpu as pltpu

F32 = jnp.float32
BF16 = jnp.bfloat16
I32 = jnp.int32
U32 = jnp.uint32

D_MODEL = 1024
HEAD_DIM = 64
BRANCH_WIDTH = D_MODEL // 2
N_BRANCHES = 3
FOX_HEADS = BRANCH_WIDTH // HEAD_DIM
RET_HEADS = 4
RET_HEAD_DIM = 128
ROPE_BASE = 10000.0
RET_GAMMA_EXP0 = 5.0
N_EXPERTS = 32
TOP_K = 4
D_FF = D_MODEL
SWIGLU_LIMIT = 7.0
SWIGLU_ALPHA = 1.702
NORM_EPS = 1e-5
NEG_INF = -1e30

LANES = 128
MAIN_COLS = 8192
GATE_COL0 = 10 * BRANCH_WIDTH
HALF = D_MODEL // 2
VMEM_LIMIT = 48 * 1024 * 1024

ROW_TILE_IN = 512
COL_TILE_IN = 1024
ATTN_TILE = 256
RET_CHUNK = 256
ROW_TILE_OUT = 256
EXPERT_TILE = 256
CUM_CHUNK = 256


def _cparams(sem):
    return pltpu.CompilerParams(dimension_semantics=sem, vmem_limit_bytes=VMEM_LIMIT)


def _sigmoid(x):
    return 1.0 / (1.0 + jnp.exp(-x))


def _pack_pairs(v):
    u = lax.bitcast_convert_type(v.astype(BF16).astype(F32), U32)
    half = v.shape[-1] // 2
    w = (u[:, half:] & jnp.uint32(0xFFFF0000)) | (u[:, :half] >> 16)
    return lax.bitcast_convert_type(w, I32)


def _unpack_pairs(w):
    u = lax.bitcast_convert_type(w, U32)
    lo = lax.bitcast_convert_type(u << 16, F32)
    hi = lax.bitcast_convert_type(u & jnp.uint32(0xFFFF0000), F32)
    return lo, hi


def _ada_kernel(c_ref, w_ref, b_ref, o_ref):
    c = c_ref[...]
    ca = c * _sigmoid(c)
    o_ref[0] = jnp.dot(ca, w_ref[0], preferred_element_type=F32) + b_ref[0]


def _ada_mod(c, ada_w, ada_b):
    depth, d, cols = ada_w.shape
    b = c.shape[0]
    tn = 1024
    return pl.pallas_call(
        _ada_kernel,
        grid=(depth, cols // tn),
        in_specs=[
            pl.BlockSpec((b, d), lambda l, j: (0, 0)),
            pl.BlockSpec((1, d, tn), lambda l, j: (l, 0, j)),
            pl.BlockSpec((1, 1, tn), lambda l, j: (l, 0, j)),
        ],
        out_specs=pl.BlockSpec((1, b, tn), lambda l, j: (l, 0, j)),
        out_shape=jax.ShapeDtypeStruct((depth, b, cols), F32),
        compiler_params=_cparams(("arbitrary", "arbitrary")),
        name="ada_mod",
    )(c, ada_w, ada_b.reshape(depth, 1, cols))


def _inproj_kernel(x_ref, g_ref, sh_ref, sc_ref, w_ref, wf_ref, proj_ref, fl_ref, h_scr):
    @pl.when(pl.program_id(1) == 0)
    def _():
        x = x_ref[...]
        r = lax.rsqrt(jnp.mean(x * x, axis=-1, keepdims=True) + NORM_EPS)
        h = x * r * g_ref[...]
        h = h * (1.0 + sc_ref[0]) + sh_ref[0]
        hb = h.astype(BF16)
        h_scr[...] = hb
        fl_ref[...] = jnp.dot(hb, wf_ref[...], preferred_element_type=F32)

    proj_ref[...] = jnp.dot(h_scr[...], w_ref[...], preferred_element_type=F32).astype(BF16)


def _inproj(x2, g, sh, sc, w_main, w_f, seq):
    n, d = x2.shape
    tm = min(ROW_TILE_IN, seq)
    tn = COL_TILE_IN
    per_seq = seq // tm
    return pl.pallas_call(
        _inproj_kernel,
        grid=(n // tm, MAIN_COLS // tn),
        in_specs=[
            pl.BlockSpec((tm, d), lambda i, j: (i, 0)),
            pl.BlockSpec((1, d), lambda i, j: (0, 0)),
            pl.BlockSpec((1, 1, d), lambda i, j: (i // per_seq, 0, 0)),
            pl.BlockSpec((1, 1, d), lambda i, j: (i // per_seq, 0, 0)),
            pl.BlockSpec((d, tn), lambda i, j: (0, j)),
            pl.BlockSpec((d, LANES), lambda i, j: (0, 0)),
        ],
        out_specs=[
            pl.BlockSpec((tm, tn), lambda i, j: (i, j)),
            pl.BlockSpec((tm, LANES), lambda i, j: (i, 0)),
        ],
        out_shape=[
            jax.ShapeDtypeStruct((n, MAIN_COLS), BF16),
            jax.ShapeDtypeStruct((n, LANES), F32),
        ],
        scratch_shapes=[pltpu.VMEM((tm, d), BF16)],
        compiler_params=_cparams(("arbitrary", "arbitrary")),
        name="inproj",
    )(x2, g, sh, sc, w_main, w_f)


def _split3(v):
    hi = v.astype(BF16)
    r1 = v - hi.astype(F32)
    mid = r1.astype(BF16)
    lo = (r1 - mid.astype(F32)).astype(BF16)
    return hi, mid, lo


def _fcum_kernel(fl_ref, fb_ref, o_ref, *, seq):
    c = min(CUM_CHUNK, seq)
    row = lax.broadcasted_iota(I32, (c, c), 0)
    col = lax.broadcasted_iota(I32, (c, c), 1)
    lower = (col <= row).astype(BF16)
    carry = jnp.zeros((1, LANES), F32)
    for ci in range(seq // c):
        x = fl_ref[pl.ds(ci * c, c), :] + fb_ref[...]
        lf = jnp.minimum(x, 0.0) - jnp.log(1.0 + jnp.exp(-jnp.abs(x)))
        hi, mid, lo = _split3(lf)
        cs = (jnp.dot(lower, hi, preferred_element_type=F32)
              + jnp.dot(lower, mid, preferred_element_type=F32)
              + jnp.dot(lower, lo, preferred_element_type=F32)) + carry
        o_ref[pl.ds(ci * c, c), :] = cs
        carry = cs[c - 1:c, :]


def _fcum(flog, fb_pad, batch, seq):
    n = flog.shape[0]
    return pl.pallas_call(
        functools.partial(_fcum_kernel, seq=seq),
        grid=(batch,),
        in_specs=[
            pl.BlockSpec((seq, LANES), lambda b: (b, 0)),
            pl.BlockSpec((1, LANES), lambda b: (0, 0)),
        ],
        out_specs=pl.BlockSpec((seq, LANES), lambda b: (b, 0)),
        out_shape=jax.ShapeDtypeStruct((n, LANES), F32),
        compiler_params=_cparams(("arbitrary",)),
        name="fox_decay_cumsum",
    )(flog, fb_pad)


def _attn_kernel(*refs, mode, t, seq, lam_init):
    if mode == "diff":
        q_ref, k_ref, v_ref, lam_ref, g_ref, o_ref = refs
        f_ref = None
        lv = lam_ref[...]
        lam = (jnp.exp(jnp.sum(lv[0:1, :] * lv[1:2, :])) - jnp.exp(jnp.sum(lv[2:3, :] * lv[3:4, :]))
               + lam_init)
    else:
        q_ref, k_ref, v_ref, f_ref, o_ref = refs
    lane = lax.broadcasted_iota(I32, (t, LANES), 1)
    row = lax.broadcasted_iota(I32, (t, t), 0)
    col = lax.broadcasted_iota(I32, (t, t), 1)
    nt = (((1,), (1,)), ((), ()))

    for qi in range(seq // t):
        lo = qi * t
        q = q_ref[lo:lo + t, :] * jnp.asarray(HEAD_DIM ** -0.5, BF16)
        zero = jnp.zeros_like(q)
        outs = []
        for idx in range(2):
            qm = jnp.where(lane < HEAD_DIM, q, zero) if idx == 0 else jnp.where(lane >= HEAD_DIM, q, zero)
            sd = lax.dot_general(qm, k_ref[lo:lo + t, :], nt, preferred_element_type=F32)
            if f_ref is not None:
                sd = sd - f_ref[0, idx:idx + 1, lo:lo + t]
            sd = jnp.where(col <= row, sd, NEG_INF)
            m = jnp.max(sd, axis=-1, keepdims=True)
            if qi > 0:
                sm = lax.dot_general(qm, k_ref[0:lo, :], nt, preferred_element_type=F32)
                if f_ref is not None:
                    sm = sm - f_ref[0, idx:idx + 1, 0:lo]
                m = jnp.maximum(m, jnp.max(sm, axis=-1, keepdims=True))
            pd = jnp.exp(sd - m)
            l = jnp.sum(pd, axis=-1, keepdims=True)
            acc = jnp.dot(pd.astype(BF16), v_ref[lo:lo + t, :], preferred_element_type=F32)
            if qi > 0:
                pm = jnp.exp(sm - m)
                l = l + jnp.sum(pm, axis=-1, keepdims=True)
                acc = acc + jnp.dot(pm.astype(BF16), v_ref[0:lo, :], preferred_element_type=F32)
            outs.append(acc / l)
        if mode == "diff":
            o = outs[0] - lam * outs[1]
            r = lax.rsqrt(jnp.mean(o * o, axis=-1, keepdims=True) + NORM_EPS)
            o = o * r * g_ref[...] * (1.0 - lam_init)
        else:
            o = jnp.where(lane < HEAD_DIM, outs[0], outs[1])
        o_ref[lo:lo + t, :] = o.astype(BF16)


def _attention(proj, mode, col_q, col_k, col_v, batch, seq, extra, lam_init=0.0):
    n = proj.shape[0]
    t = min(ATTN_TILE, seq)
    n_blk = BRANCH_WIDTH // LANES
    in_specs = [
        pl.BlockSpec((seq, LANES), lambda b, h: (b, col_q + h)),
        pl.BlockSpec((seq, LANES), lambda b, h: (b, col_k + h)),
        pl.BlockSpec((seq, LANES), lambda b, h: (b, col_v + h)),
    ]
    if mode == "diff":
        lamv, g = extra
        in_specs += [pl.BlockSpec((4, HEAD_DIM), lambda b, h: (0, 0)),
                     pl.BlockSpec((1, LANES), lambda b, h: (0, 0))]
        args = (proj, proj, proj, lamv, g)
    else:
        (frow,) = extra
        in_specs += [pl.BlockSpec((1, 2, seq), lambda b, h: (b * n_blk + h, 0, 0))]
        args = (proj, proj, proj, frow)
    return pl.pallas_call(
        functools.partial(_attn_kernel, mode=mode, t=t, seq=seq, lam_init=lam_init),
        grid=(batch, n_blk),
        in_specs=in_specs,
        out_specs=pl.BlockSpec((seq, LANES), lambda b, h: (b, h)),
        out_shape=jax.ShapeDtypeStruct((n, BRANCH_WIDTH), BF16),
        compiler_params=_cparams(("arbitrary", "arbitrary")),
        name="attn_" + mode,
    )(*args)


def _ret_kernel(q_ref, k_ref, v_ref, g_ref, cos_ref, sin_ref, dm_ref, qd_ref, kd_ref, cd_ref,
                o_ref, st_scr, *, seq, c):
    st_scr[...] = jnp.zeros(st_scr.shape, F32)
    half = RET_HEAD_DIM // 2

    def body(ci, carry):
        sl = pl.ds(pl.multiple_of(ci * c, c), c)
        cos = cos_ref[sl, :]
        sin = sin_ref[sl, :]
        q = q_ref[sl, :].astype(F32)
        k = k_ref[sl, :].astype(F32)
        qr = q * cos + pltpu.roll(q, half, 1) * sin
        kr = (k * cos + pltpu.roll(k, half, 1) * sin) * (RET_HEAD_DIM ** -0.5)
        v = v_ref[sl, :]
        sc = lax.dot_general(qr.astype(BF16), kr.astype(BF16), (((1,), (1,)), ((), ())),
                             preferred_element_type=F32) * dm_ref[0]
        inner = jnp.dot(sc.astype(BF16), v, preferred_element_type=F32)
        st = st_scr[...]
        cross = jnp.dot((qr * qd_ref[0]).astype(BF16), st.astype(BF16), preferred_element_type=F32)
        o = inner + cross
        kt = (kr * kd_ref[0]).T.astype(BF16)
        st_scr[...] = st * cd_ref[0] + jnp.dot(kt, v, preferred_element_type=F32)
        mu = jnp.mean(o, axis=-1, keepdims=True)
        oc = o - mu
        var = jnp.mean(oc * oc, axis=-1, keepdims=True)
        on = oc * lax.rsqrt(var + NORM_EPS)
        g = g_ref[sl, :].astype(F32)
        o_ref[sl, :] = (on * (g * _sigmoid(g))).astype(BF16)
        return carry

    lax.fori_loop(0, seq // c, body, 0)


def _retention(proj, tables, batch, seq):
    n = proj.shape[0]
    c = min(RET_CHUNK, seq)
    cos, sin, dm, qd, kd, cd = tables
    col = BRANCH_WIDTH * 3 // LANES
    step = BRANCH_WIDTH // LANES

    def pspec(k):
        return pl.BlockSpec((seq, LANES), lambda b, h: (b, col + k * step + h))

    return pl.pallas_call(
        functools.partial(_ret_kernel, seq=seq, c=c),
        grid=(batch, RET_HEADS),
        in_specs=[
            pspec(0), pspec(1), pspec(2), pspec(3),
            pl.BlockSpec((seq, LANES), lambda b, h: (0, 0)),
            pl.BlockSpec((seq, LANES), lambda b, h: (0, 0)),
            pl.BlockSpec((1, c, c), lambda b, h: (h, 0, 0)),
            pl.BlockSpec((1, c, LANES), lambda b, h: (h, 0, 0)),
            pl.BlockSpec((1, c, LANES), lambda b, h: (h, 0, 0)),
            pl.BlockSpec((1, LANES, LANES), lambda b, h: (h, 0, 0)),
        ],
        out_specs=pl.BlockSpec((seq, LANES), lambda b, h: (b, h)),
        out_shape=jax.ShapeDtypeStruct((n, BRANCH_WIDTH), BF16),
        scratch_shapes=[pltpu.VMEM((RET_HEAD_DIM, RET_HEAD_DIM), F32)],
        compiler_params=_cparams(("arbitrary", "arbitrary")),
        name="retention",
    )(proj, proj, proj, proj, cos, sin, dm, qd, kd, cd)


def _retention_tables(seq):
    c = min(RET_CHUNK, seq)
    d = RET_HEAD_DIM
    inv = ROPE_BASE ** (-jnp.arange(0, d, 2, dtype=F32) / d)
    ang = jnp.arange(seq).astype(F32)[:, None] * inv[None, :]
    cos = jnp.cos(ang)
    sin = jnp.sin(ang)
    cos_full = jnp.concatenate([cos, cos], axis=-1)
    sin_signed = jnp.concatenate([-sin, sin], axis=-1)
    log_gamma = jnp.log1p(-jnp.exp2(-RET_GAMMA_EXP0 - jnp.arange(RET_HEADS, dtype=F32)))
    idx = jnp.arange(c, dtype=F32)
    dist = idx[:, None] - idx[None, :]
    dm = jnp.where(dist >= 0, jnp.exp(log_gamma[:, None, None] * jnp.maximum(dist, 0.0)), 0.0)
    qd = jnp.exp(log_gamma[:, None] * (idx + 1.0))
    kd = jnp.exp(log_gamma[:, None] * (c - 1.0 - idx))
    cd = jnp.exp(log_gamma * c)
    qd = jnp.broadcast_to(qd[:, :, None], (RET_HEADS, c, LANES))
    kd = jnp.broadcast_to(kd[:, :, None], (RET_HEADS, c, LANES))
    cd = jnp.broadcast_to(cd[:, None, None], (RET_HEADS, LANES, LANES))
    return cos_full, sin_signed, dm, qd, kd, cd


def _merge_kernel(od_ref, or_ref, of_ref, gd_ref, gr_ref, gf_ref, gb_ref, wb_ref, wo_ref, x_ref,
                  g1_ref, n2_ref, sh_ref, sc_ref, rw_ref, rb_ref,
                  xo_ref, hp_ref, ti_ref, tw_ref, *, tm):
    merged = None
    for i, (o_ref, gate_ref) in enumerate(((od_ref, gd_ref), (or_ref, gr_ref), (of_ref, gf_ref))):
        gate = _sigmoid(gate_ref[...].astype(F32) + gb_ref[i])
        term = gate * jnp.dot(o_ref[...], wb_ref[i], preferred_element_type=F32)
        merged = term if merged is None else merged + term
    mix = jnp.dot(merged.astype(BF16), wo_ref[...], preferred_element_type=F32)
    xn = x_ref[...] + g1_ref[0] * mix
    xo_ref[...] = xn

    r = lax.rsqrt(jnp.mean(xn * xn, axis=-1, keepdims=True) + NORM_EPS)
    h = xn * r * n2_ref[...]
    h = h * (1.0 + sc_ref[0]) + sh_ref[0]
    hp_ref[...] = _pack_pairs(h)

    hh, hm, _ = _split3(h)
    wh, wm, _ = _split3(rw_ref[...])
    logits = (jnp.dot(hh, wh, preferred_element_type=F32)
              + jnp.dot(hh, wm, preferred_element_type=F32)
              + jnp.dot(hm, wh, preferred_element_type=F32)) + rb_ref[...]

    lane = lax.broadcasted_iota(I32, (tm, LANES), 1)
    vals, idxs = [], []
    for _ in range(TOP_K):
        m = jnp.max(logits, axis=-1, keepdims=True)
        idx = jnp.min(jnp.where(logits == m, lane, LANES), axis=-1, keepdims=True)
        vals.append(m)
        idxs.append(idx)
        logits = jnp.where(lane == idx, -3.0e38, logits)
    es = [jnp.exp(v - vals[0]) for v in vals]
    den = es[0] + es[1] + es[2] + es[3]
    ti = jnp.zeros((tm, LANES), I32)
    tw = jnp.zeros((tm, LANES), F32)
    for k in range(TOP_K):
        ti = jnp.where(lane == k, idxs[k], ti)
        tw = jnp.where(lane == k, es[k] / den, tw)
    ti_ref[...] = ti
    tw_ref[...] = tw


def _merge(o_diff, o_ret, o_fox, proj, gate_b, w_branch, w_out, x2, g1, n2g, sh2, sc2, rw, rb, seq):
    n, d = x2.shape
    tm = min(ROW_TILE_OUT, seq)
    per_seq = seq // tm
    gcol = GATE_COL0 // d

    def rows(w):
        return pl.BlockSpec((tm, w), lambda i: (i, 0))

    def gspec(k):
        return pl.BlockSpec((tm, d), lambda i: (i, gcol + k))

    def whole(shape):
        return pl.BlockSpec(shape, lambda i: (0,) * len(shape))

    def per_b():
        return pl.BlockSpec((1, 1, d), lambda i: (i // per_seq, 0, 0))

    return pl.pallas_call(
        functools.partial(_merge_kernel, tm=tm),
        grid=(n // tm,),
        in_specs=[
            rows(BRANCH_WIDTH), rows(BRANCH_WIDTH), rows(BRANCH_WIDTH),
            gspec(0), gspec(1), gspec(2),
            whole((N_BRANCHES, 1, d)), whole((N_BRANCHES, BRANCH_WIDTH, d)), whole((d, d)),
            rows(d), per_b(), whole((1, d)), per_b(), per_b(),
            whole((d, LANES)), whole((1, LANES)),
        ],
        out_specs=[rows(d), rows(HALF), rows(LANES), rows(LANES)],
        out_shape=[
            jax.ShapeDtypeStruct((n, d), F32),
            jax.ShapeDtypeStruct((n, HALF), I32),
            jax.ShapeDtypeStruct((n, LANES), I32),
            jax.ShapeDtypeStruct((n, LANES), F32),
        ],
        compiler_params=_cparams(("arbitrary",)),
        name="merge_outproj_router",
    )(o_diff, o_ret, o_fox, proj, proj, proj, gate_b, w_branch, w_out, x2, g1, n2g, sh2, sc2, rw, rb)


def _prep_up_kernel(w_ref, o_ref, t_scr):
    t_scr[...] = w_ref[0].T
    f = t_scr.shape[0] // 2
    o_ref[0, 0:f, :] = t_scr[pl.ds(0, f, stride=2), :].astype(BF16)
    o_ref[0, f:, :] = t_scr[pl.ds(1, f, stride=2), :].astype(BF16)


def _prep_up(w_up):
    e, d, f2 = w_up.shape
    kt = LANES
    return pl.pallas_call(
        _prep_up_kernel,
        grid=(e, d // kt),
        in_specs=[pl.BlockSpec((1, kt, f2), lambda i, j: (i, j, 0))],
        out_specs=pl.BlockSpec((1, f2, kt), lambda i, j: (i, 0, j)),
        out_shape=jax.ShapeDtypeStruct((e, f2, d), BF16),
        scratch_shapes=[pltpu.VMEM((f2, kt), F32)],
        compiler_params=_cparams(("arbitrary", "arbitrary")),
        name="expert_up_weight_prep",
    )(w_up)


def _expert_kernel(te_ref, nu_ref, xs_ref, wu_ref, bu_ref, wd_ref, bd_ref, y_ref):
    nt = (((1,), (1,)), ((), ()))

    @pl.when(pl.program_id(0) < nu_ref[0])
    def _():
        lo, hi = _unpack_pairs(xs_ref[...])
        lo = lo.astype(BF16)
        hi = hi.astype(BF16)

        def up_half(r0):
            return (lax.dot_general(lo, wu_ref[0, r0:r0 + D_FF, :HALF], nt, preferred_element_type=F32)
                    + lax.dot_general(hi, wu_ref[0, r0:r0 + D_FF, HALF:], nt, preferred_element_type=F32))

        glu = jnp.minimum(up_half(0) + bu_ref[0, :, :D_FF], SWIGLU_LIMIT)
        lin = jnp.clip(up_half(D_FF) + bu_ref[0, :, D_FF:], -SWIGLU_LIMIT, SWIGLU_LIMIT)
        act = glu * _sigmoid(SWIGLU_ALPHA * glu) * (lin + 1.0)
        down = jnp.dot(act.astype(BF16), wd_ref[0], preferred_element_type=F32) + bd_ref[0]
        y_ref[...] = _pack_pairs(down)

    @pl.when(pl.program_id(0) >= nu_ref[0])
    def _():
        y_ref[...] = jnp.zeros(y_ref.shape, I32)


def _experts(tile_e, n_used, xs, w_up, b_up, w_down, b_down):
    r = xs.shape[0]
    tm = EXPERT_TILE
    d = D_MODEL
    grid_spec = pltpu.PrefetchScalarGridSpec(
        num_scalar_prefetch=2,
        grid=(r // tm,),
        in_specs=[
            pl.BlockSpec((tm, HALF), lambda i, te, nu: (i, 0)),
            pl.BlockSpec((1, 2 * D_FF, d), lambda i, te, nu: (te[i], 0, 0)),
            pl.BlockSpec((1, 1, 2 * D_FF), lambda i, te, nu: (te[i], 0, 0)),
            pl.BlockSpec((1, D_FF, d), lambda i, te, nu: (te[i], 0, 0)),
            pl.BlockSpec((1, 1, d), lambda i, te, nu: (te[i], 0, 0)),
        ],
        out_specs=pl.BlockSpec((tm, HALF), lambda i, te, nu: (i, 0)),
    )
    return pl.pallas_call(
        _expert_kernel,
        grid_spec=grid_spec,
        out_shape=jax.ShapeDtypeStruct((r, HALF), I32),
        compiler_params=_cparams(("arbitrary",)),
        name="expert_mlp",
    )(tile_e, n_used, xs, w_up, b_up, w_down, b_down)


def _combine_kernel(y_ref, tw_ref, x_ref, g2_ref, fg_ref, o_ref, *, final):
    tw = tw_ref[...]
    acc_lo = None
    for k in range(TOP_K):
        lo, hi = _unpack_pairs(y_ref[:, k * HALF:(k + 1) * HALF])
        w = tw[:, k:k + 1]
        acc_lo = w * lo if acc_lo is None else acc_lo + w * lo
        acc_hi = w * hi if k == 0 else acc_hi + w * hi
    ffn = jnp.concatenate([acc_lo, acc_hi], axis=-1)
    xn = x_ref[...] + g2_ref[0] * ffn
    if final:
        r = lax.rsqrt(jnp.mean(xn * xn, axis=-1, keepdims=True) + NORM_EPS)
        xn = xn * r * fg_ref[...]
    o_ref[...] = xn


def _combine(ytok, top_w, x2, g2, final_g, seq, final):
    n, d = x2.shape
    tm = min(ROW_TILE_OUT, seq)
    per_seq = seq // tm
    return pl.pallas_call(
        functools.partial(_combine_kernel, final=final),
        grid=(n // tm,),
        in_specs=[
            pl.BlockSpec((tm, TOP_K * HALF), lambda i: (i, 0)),
            pl.BlockSpec((tm, LANES), lambda i: (i, 0)),
            pl.BlockSpec((tm, d), lambda i: (i, 0)),
            pl.BlockSpec((1, 1, d), lambda i: (i // per_seq, 0, 0)),
            pl.BlockSpec((1, d), lambda i: (0, 0)),
        ],
        out_specs=pl.BlockSpec((tm, d), lambda i: (i, 0)),
        out_shape=jax.ShapeDtypeStruct((n, d), F32),
        compiler_params=_cparams(("arbitrary",)),
        name="moe_combine",
    )(ytok, top_w, x2, g2, final_g)


def _route(top_idx, n_rows):
    tm = EXPERT_TILE
    flat_e = top_idx.reshape(-1)
    onehot = (flat_e[:, None] == jnp.arange(N_EXPERTS, dtype=I32)[None, :]).astype(I32)
    csum = jnp.cumsum(onehot, axis=0)
    rank = jnp.take_along_axis(csum, flat_e[:, None], axis=1)[:, 0] - 1
    sizes = csum[-1]
    padded = ((sizes + tm - 1) // tm) * tm
    pend = jnp.cumsum(padded)
    pstart = pend - padded
    pos = pstart[flat_e] + rank
    tiles = jnp.arange(n_rows // tm, dtype=I32) * tm
    tile_e = jnp.minimum(jnp.searchsorted(pend, tiles, side="right"), N_EXPERTS - 1).astype(I32)
    n_used = (pend[-1] // tm).astype(I32).reshape(1)
    return pos.astype(I32), tile_e, n_used


def kernel(x, c, norm1_g, norm2_g, ada_w, ada_b, w_in, gate_b, fox_fb, lam_q1, lam_k1, lam_q2, lam_k2,
           diff_subln_g, w_branch, w_out, router_w, router_b, exp_w_up, exp_b_up, exp_w_down,
           exp_b_down, final_g):
    batch, seq, d = x.shape
    depth = ada_w.shape[0]
    n = batch * seq
    x2 = x.reshape(n, d)

    mod = _ada_mod(c, ada_w, ada_b)
    tables = _retention_tables(seq)
    n_rows = n * TOP_K + N_EXPERTS * EXPERT_TILE
    blk = BRANCH_WIDTH // LANES

    for l in range(depth):
        lam_init = 0.8 - 0.6 * math.exp(-0.3 * l)
        sh1, sc1, g1, sh2, sc2, g2 = [m.reshape(batch, 1, d) for m in jnp.split(mod[l], 6, axis=-1)]

        wl = w_in[l]
        w_main = jnp.concatenate([wl[:, :GATE_COL0], wl[:, GATE_COL0 + FOX_HEADS:]], axis=1).astype(BF16)
        w_f = jnp.pad(wl[:, GATE_COL0:GATE_COL0 + FOX_HEADS], ((0, 0), (0, LANES - FOX_HEADS))).astype(BF16)
        proj, flog = _inproj(x2, norm1_g[l].reshape(1, d), sh1, sc1, w_main, w_f, seq)

        fb_pad = jnp.pad(fox_fb[l], (0, LANES - FOX_HEADS)).reshape(1, LANES)
        fc = _fcum(flog, fb_pad, batch, seq)
        frow = fc.reshape(batch, seq, LANES)[:, :, :FOX_HEADS].transpose(0, 2, 1)
        frow = frow.reshape(batch * blk, 2, seq)

        lamv = jnp.stack([lam_q1[l], lam_k1[l], lam_q2[l], lam_k2[l]])
        o_diff = _attention(proj, "diff", 0, blk, 2 * blk, batch, seq,
                            (lamv, diff_subln_g[l].reshape(1, LANES)), lam_init)
        o_ret = _retention(proj, tables, batch, seq)
        o_fox = _attention(proj, "fox", 7 * blk, 8 * blk, 9 * blk, batch, seq, (frow,))

        rw = jnp.pad(router_w[l], ((0, 0), (0, LANES - N_EXPERTS)))
        rb = jnp.pad(router_b[l], (0, LANES - N_EXPERTS), constant_values=NEG_INF).reshape(1, LANES)
        x2, h2p, top_idx, top_w = _merge(
            o_diff, o_ret, o_fox, proj, gate_b[l].reshape(N_BRANCHES, 1, d), w_branch[l].astype(BF16),
            w_out[l].astype(BF16), x2, g1, norm2_g[l].reshape(1, d), sh2, sc2, rw, rb, seq)

        pos, tile_e, n_used = _route(top_idx[:, :TOP_K], n_rows)
        xs = jnp.zeros((n_rows, HALF), I32).at[pos].set(jnp.repeat(h2p, TOP_K, axis=0))

        w_up = _prep_up(exp_w_up[l])
        bu = exp_b_up[l]
        b_up = jnp.concatenate([bu[:, 0::2], bu[:, 1::2]], axis=-1).reshape(N_EXPERTS, 1, 2 * D_FF)
        y = _experts(tile_e, n_used, xs, w_up, b_up, exp_w_down[l].astype(BF16),
                     exp_b_down[l].reshape(N_EXPERTS, 1, d))
        ytok = y[pos].reshape(n, TOP_K * HALF)
        x2 = _combine(ytok, top_w, x2, g2, final_g.reshape(1, d), seq, final=(l == depth - 1))

    return x2.reshape(batch, seq, d)
```

```python
import functools
import math

import jax
import jax.numpy as jnp
from jax import lax
from jax.experimental import pallas as pl
from jax.experimental.pallas import tpu as pltpu
from jax.experimental.pallas import tpu_sc as plsc

F32 = jnp.float32
BF16 = jnp.bfloat16
I32 = jnp.int32
U32 = jnp.uint32

D_MODEL = 1024
HEAD_DIM = 64
BRANCH_WIDTH = D_MODEL // 2
N_BRANCHES = 3
FOX_HEADS = BRANCH_WIDTH // HEAD_DIM
RET_HEADS = 4
RET_HEAD_DIM = 128
ROPE_BASE = 10000.0
RET_GAMMA_EXP0 = 5.0
N_EXPERTS = 32
TOP_K = 4
D_FF = D_MODEL
SWIGLU_LIMIT = 7.0
SWIGLU_ALPHA = 1.702
NORM_EPS = 1e-5
NEG_INF = -1e30

LANES = 128
MAIN_COLS = 8192
GATE_COL0 = 10 * BRANCH_WIDTH
HALF = D_MODEL // 2
QUART = HALF // 2
SC_WINDOW = 128
SC_WORKERS = 32
VMEM_LIMIT = 48 * 1024 * 1024

ROW_TILE_IN = 512
COL_TILE_IN = 1024
ATTN_TILE = 256
RET_CHUNK = 256
ROW_TILE_OUT = 256
EXPERT_TILE = 256
CUM_CHUNK = 256


def _cparams(sem):
    return pltpu.CompilerParams(dimension_semantics=sem, vmem_limit_bytes=VMEM_LIMIT)


def _sigmoid(x):
    return 1.0 / (1.0 + jnp.exp(-x))


def _pack_pairs(v):
    u = lax.bitcast_convert_type(v.astype(BF16).astype(F32), U32)
    half = v.shape[-1] // 2
    w = (u[:, half:] & jnp.uint32(0xFFFF0000)) | (u[:, :half] >> 16)
    return lax.bitcast_convert_type(w, I32)


def _unpack_pairs(w):
    u = lax.bitcast_convert_type(w, U32)
    lo = lax.bitcast_convert_type(u << 16, F32)
    hi = lax.bitcast_convert_type(u & jnp.uint32(0xFFFF0000), F32)
    return lo, hi


def _ada_kernel(c_ref, w_ref, b_ref, o_ref):
    c = c_ref[...]
    ca = c * _sigmoid(c)
    o_ref[0] = jnp.dot(ca, w_ref[0], preferred_element_type=F32) + b_ref[0]


def _ada_mod(c, ada_w, ada_b):
    depth, d, cols = ada_w.shape
    b = c.shape[0]
    tn = 1024
    return pl.pallas_call(
        _ada_kernel,
        grid=(depth, cols // tn),
        in_specs=[
            pl.BlockSpec((b, d), lambda l, j: (0, 0)),
            pl.BlockSpec((1, d, tn), lambda l, j: (l, 0, j)),
            pl.BlockSpec((1, 1, tn), lambda l, j: (l, 0, j)),
        ],
        out_specs=pl.BlockSpec((1, b, tn), lambda l, j: (l, 0, j)),
        out_shape=jax.ShapeDtypeStruct((depth, b, cols), F32),
        compiler_params=_cparams(("arbitrary", "arbitrary")),
        name="ada_mod",
    )(c, ada_w, ada_b.reshape(depth, 1, cols))


def _inproj_kernel(x_ref, g_ref, sh_ref, sc_ref, w_ref, wf_ref, proj_ref, fl_ref, h_scr):
    @pl.when(pl.program_id(1) == 0)
    def _():
        x = x_ref[...]
        r = lax.rsqrt(jnp.mean(x * x, axis=-1, keepdims=True) + NORM_EPS)
        h = x * r * g_ref[...]
        h = h * (1.0 + sc_ref[0]) + sh_ref[0]
        hb = h.astype(BF16)
        h_scr[...] = hb
        fl_ref[...] = jnp.dot(hb, wf_ref[...], preferred_element_type=F32)

    proj_ref[...] = jnp.dot(h_scr[...], w_ref[...], preferred_element_type=F32).astype(BF16)


def _inproj(x2, g, sh, sc, w_main, w_f, seq):
    n, d = x2.shape
    tm = min(ROW_TILE_IN, seq)
    tn = COL_TILE_IN
    per_seq = seq // tm
    return pl.pallas_call(
        _inproj_kernel,
        grid=(n // tm, MAIN_COLS // tn),
        in_specs=[
            pl.BlockSpec((tm, d), lambda i, j: (i, 0)),
            pl.BlockSpec((1, d), lambda i, j: (0, 0)),
            pl.BlockSpec((1, 1, d), lambda i, j: (i // per_seq, 0, 0)),
            pl.BlockSpec((1, 1, d), lambda i, j: (i // per_seq, 0, 0)),
            pl.BlockSpec((d, tn), lambda i, j: (0, j)),
            pl.BlockSpec((d, LANES), lambda i, j: (0, 0)),
        ],
        out_specs=[
            pl.BlockSpec((tm, tn), lambda i, j: (i, j)),
            pl.BlockSpec((tm, LANES), lambda i, j: (i, 0)),
        ],
        out_shape=[
            jax.ShapeDtypeStruct((n, MAIN_COLS), BF16),
            jax.ShapeDtypeStruct((n, LANES), F32),
        ],
        scratch_shapes=[pltpu.VMEM((tm, d), BF16)],
        compiler_params=_cparams(("arbitrary", "arbitrary")),
        name="inproj",
    )(x2, g, sh, sc, w_main, w_f)


def _split3(v):
    hi = v.astype(BF16)
    r1 = v - hi.astype(F32)
    mid = r1.astype(BF16)
    lo = (r1 - mid.astype(F32)).astype(BF16)
    return hi, mid, lo


def _fcum_kernel(fl_ref, fb_ref, o_ref, *, seq):
    c = min(CUM_CHUNK, seq)
    row = lax.broadcasted_iota(I32, (c, c), 0)
    col = lax.broadcasted_iota(I32, (c, c), 1)
    lower = (col <= row).astype(BF16)
    carry = jnp.zeros((1, LANES), F32)
    for ci in range(seq // c):
        x = fl_ref[pl.ds(ci * c, c), :] + fb_ref[...]
        lf = jnp.minimum(x, 0.0) - jnp.log(1.0 + jnp.exp(-jnp.abs(x)))
        hi, mid, lo = _split3(lf)
        cs = (jnp.dot(lower, hi, preferred_element_type=F32)
              + jnp.dot(lower, mid, preferred_element_type=F32)
              + jnp.dot(lower, lo, preferred_element_type=F32)) + carry
        o_ref[pl.ds(ci * c, c), :] = cs
        carry = cs[c - 1:c, :]


def _fcum(flog, fb_pad, batch, seq):
    n = flog.shape[0]
    return pl.pallas_call(
        functools.partial(_fcum_kernel, seq=seq),
        grid=(batch,),
        in_specs=[
            pl.BlockSpec((seq, LANES), lambda b: (b, 0)),
            pl.BlockSpec((1, LANES), lambda b: (0, 0)),
        ],
        out_specs=pl.BlockSpec((seq, LANES), lambda b: (b, 0)),
        out_shape=jax.ShapeDtypeStruct((n, LANES), F32),
        compiler_params=_cparams(("arbitrary",)),
        name="fox_decay_cumsum",
    )(flog, fb_pad)


def _attn_kernel(*refs, mode, t, seq, lam_init):
    if mode == "diff":
        q_ref, k_ref, v_ref, lam_ref, g_ref, o_ref = refs
        f_ref = None
        lv = lam_ref[...]
        lam = (jnp.exp(jnp.sum(lv[0:1, :] * lv[1:2, :])) - jnp.exp(jnp.sum(lv[2:3, :] * lv[3:4, :]))
               + lam_init)
    else:
        q_ref, k_ref, v_ref, f_ref, o_ref = refs
    lane = lax.broadcasted_iota(I32, (t, LANES), 1)
    row = lax.broadcasted_iota(I32, (t, t), 0)
    col = lax.broadcasted_iota(I32, (t, t), 1)
    nt = (((1,), (1,)), ((), ()))

    for qi in range(seq // t):
        lo = qi * t
        q = q_ref[lo:lo + t, :] * jnp.asarray(HEAD_DIM ** -0.5, BF16)
        zero = jnp.zeros_like(q)
        outs = []
        for idx in range(2):
            qm = jnp.where(lane < HEAD_DIM, q, zero) if idx == 0 else jnp.where(lane >= HEAD_DIM, q, zero)
            sd = lax.dot_general(qm, k_ref[lo:lo + t, :], nt, preferred_element_type=F32)
            if f_ref is not None:
                sd = sd - f_ref[0, idx:idx + 1, lo:lo + t]
            sd = jnp.where(col <= row, sd, NEG_INF)
            m = jnp.max(sd, axis=-1, keepdims=True)
            if qi > 0:
                sm = lax.dot_general(qm, k_ref[0:lo, :], nt, preferred_element_type=F32)
                if f_ref is not None:
                    sm = sm - f_ref[0, idx:idx + 1, 0:lo]
                m = jnp.maximum(m, jnp.max(sm, axis=-1, keepdims=True))
            pd = jnp.exp(sd - m)
            l = jnp.sum(pd, axis=-1, keepdims=True)
            acc = jnp.dot(pd.astype(BF16), v_ref[lo:lo + t, :], preferred_element_type=F32)
            if qi > 0:
                pm = jnp.exp(sm - m)
                l = l + jnp.sum(pm, axis=-1, keepdims=True)
                acc = acc + jnp.dot(pm.astype(BF16), v_ref[0:lo, :], preferred_element_type=F32)
            outs.append(acc / l)
        if mode == "diff":
            o = outs[0] - lam * outs[1]
            r = lax.rsqrt(jnp.mean(o * o, axis=-1, keepdims=True) + NORM_EPS)
            o = o * r * g_ref[...] * (1.0 - lam_init)
        else:
            o = jnp.where(lane < HEAD_DIM, outs[0], outs[1])
        o_ref[lo:lo + t, :] = o.astype(BF16)


def _attention(proj, mode, col_q, col_k, col_v, batch, seq, extra, lam_init=0.0):
    n = proj.shape[0]
    t = min(ATTN_TILE, seq)
    n_blk = BRANCH_WIDTH // LANES
    in_specs = [
        pl.BlockSpec((seq, LANES), lambda b, h: (b, col_q + h)),
        pl.BlockSpec((seq, LANES), lambda b, h: (b, col_k + h)),
        pl.BlockSpec((seq, LANES), lambda b, h: (b, col_v + h)),
    ]
    if mode == "diff":
        lamv, g = extra
        in_specs += [pl.BlockSpec((4, HEAD_DIM), lambda b, h: (0, 0)),
                     pl.BlockSpec((1, LANES), lambda b, h: (0, 0))]
        args = (proj, proj, proj, lamv, g)
    else:
        (frow,) = extra
        in_specs += [pl.BlockSpec((1, 2, seq), lambda b, h: (b * n_blk + h, 0, 0))]
        args = (proj, proj, proj, frow)
    return pl.pallas_call(
        functools.partial(_attn_kernel, mode=mode, t=t, seq=seq, lam_init=lam_init),
        grid=(batch, n_blk),
        in_specs=in_specs,
        out_specs=pl.BlockSpec((seq, LANES), lambda b, h: (b, h)),
        out_shape=jax.ShapeDtypeStruct((n, BRANCH_WIDTH), BF16),
        compiler_params=_cparams(("arbitrary", "arbitrary")),
        name="attn_" + mode,
    )(*args)


def _ret_kernel(q_ref, k_ref, v_ref, g_ref, cos_ref, sin_ref, dm_ref, qd_ref, kd_ref, cd_ref,
                o_ref, st_scr, *, seq, c):
    st_scr[...] = jnp.zeros(st_scr.shape, F32)
    half = RET_HEAD_DIM // 2

    def body(ci, carry):
        sl = pl.ds(pl.multiple_of(ci * c, c), c)
        cos = cos_ref[sl, :]
        sin = sin_ref[sl, :]
        q = q_ref[sl, :].astype(F32)
        k = k_ref[sl, :].astype(F32)
        qr = q * cos + pltpu.roll(q, half, 1) * sin
        kr = (k * cos + pltpu.roll(k, half, 1) * sin) * (RET_HEAD_DIM ** -0.5)
        v = v_ref[sl, :]
        sc = lax.dot_general(qr.astype(BF16), kr.astype(BF16), (((1,), (1,)), ((), ())),
                             preferred_element_type=F32) * dm_ref[0]
        inner = jnp.dot(sc.astype(BF16), v, preferred_element_type=F32)
        st = st_scr[...]
        cross = jnp.dot((qr * qd_ref[0]).astype(BF16), st.astype(BF16), preferred_element_type=F32)
        o = inner + cross
        kt = (kr * kd_ref[0]).T.astype(BF16)
        st_scr[...] = st * cd_ref[0] + jnp.dot(kt, v, preferred_element_type=F32)
        mu = jnp.mean(o, axis=-1, keepdims=True)
        oc = o - mu
        var = jnp.mean(oc * oc, axis=-1, keepdims=True)
        on = oc * lax.rsqrt(var + NORM_EPS)
        g = g_ref[sl, :].astype(F32)
        o_ref[sl, :] = (on * (g * _sigmoid(g))).astype(BF16)
        return carry

    lax.fori_loop(0, seq // c, body, 0)


def _retention(proj, tables, batch, seq):
    n = proj.shape[0]
    c = min(RET_CHUNK, seq)
    cos, sin, dm, qd, kd, cd = tables
    col = BRANCH_WIDTH * 3 // LANES
    step = BRANCH_WIDTH // LANES

    def pspec(k):
        return pl.BlockSpec((seq, LANES), lambda b, h: (b, col + k * step + h))

    return pl.pallas_call(
        functools.partial(_ret_kernel, seq=seq, c=c),
        grid=(batch, RET_HEADS),
        in_specs=[
            pspec(0), pspec(1), pspec(2), pspec(3),
            pl.BlockSpec((seq, LANES), lambda b, h: (0, 0)),
            pl.BlockSpec((seq, LANES), lambda b, h: (0, 0)),
            pl.BlockSpec((1, c, c), lambda b, h: (h, 0, 0)),
            pl.BlockSpec((1, c, LANES), lambda b, h: (h, 0, 0)),
            pl.BlockSpec((1, c, LANES), lambda b, h: (h, 0, 0)),
            pl.BlockSpec((1, LANES, LANES), lambda b, h: (h, 0, 0)),
        ],
        out_specs=pl.BlockSpec((seq, LANES), lambda b, h: (b, h)),
        out_shape=jax.ShapeDtypeStruct((n, BRANCH_WIDTH), BF16),
        scratch_shapes=[pltpu.VMEM((RET_HEAD_DIM, RET_HEAD_DIM), F32)],
        compiler_params=_cparams(("arbitrary", "arbitrary")),
        name="retention",
    )(proj, proj, proj, proj, cos, sin, dm, qd, kd, cd)


def _retention_tables(seq):
    c = min(RET_CHUNK, seq)
    d = RET_HEAD_DIM
    inv = ROPE_BASE ** (-jnp.arange(0, d, 2, dtype=F32) / d)
    ang = jnp.arange(seq).astype(F32)[:, None] * inv[None, :]
    cos = jnp.cos(ang)
    sin = jnp.sin(ang)
    cos_full = jnp.concatenate([cos, cos], axis=-1)
    sin_signed = jnp.concatenate([-sin, sin], axis=-1)
    log_gamma = jnp.log1p(-jnp.exp2(-RET_GAMMA_EXP0 - jnp.arange(RET_HEADS, dtype=F32)))
    idx = jnp.arange(c, dtype=F32)
    dist = idx[:, None] - idx[None, :]
    dm = jnp.where(dist >= 0, jnp.exp(log_gamma[:, None, None] * jnp.maximum(dist, 0.0)), 0.0)
    qd = jnp.exp(log_gamma[:, None] * (idx + 1.0))
    kd = jnp.exp(log_gamma[:, None] * (c - 1.0 - idx))
    cd = jnp.exp(log_gamma * c)
    qd = jnp.broadcast_to(qd[:, :, None], (RET_HEADS, c, LANES))
    kd = jnp.broadcast_to(kd[:, :, None], (RET_HEADS, c, LANES))
    cd = jnp.broadcast_to(cd[:, None, None], (RET_HEADS, LANES, LANES))
    return cos_full, sin_signed, dm, qd, kd, cd


def _merge_kernel(od_ref, or_ref, of_ref, gd_ref, gr_ref, gf_ref, gb_ref, wb_ref, wo_ref, x_ref,
                  g1_ref, n2_ref, sh_ref, sc_ref, rw_ref, rb_ref,
                  xo_ref, hp_ref, ti_ref, tw_ref, *, tm):
    merged = None
    for i, (o_ref, gate_ref) in enumerate(((od_ref, gd_ref), (or_ref, gr_ref), (of_ref, gf_ref))):
        gate = _sigmoid(gate_ref[...].astype(F32) + gb_ref[i])
        term = gate * jnp.dot(o_ref[...], wb_ref[i], preferred_element_type=F32)
        merged = term if merged is None else merged + term
    mix = jnp.dot(merged.astype(BF16), wo_ref[...], preferred_element_type=F32)
    xn = x_ref[...] + g1_ref[0] * mix
    xo_ref[...] = xn

    r = lax.rsqrt(jnp.mean(xn * xn, axis=-1, keepdims=True) + NORM_EPS)
    h = xn * r * n2_ref[...]
    h = h * (1.0 + sc_ref[0]) + sh_ref[0]
    hp = _pack_pairs(h)
    hp_ref[0] = hp[:, :QUART]
    hp_ref[1] = hp[:, QUART:]

    hh, hm, _ = _split3(h)
    wh, wm, _ = _split3(rw_ref[...])
    logits = (jnp.dot(hh, wh, preferred_element_type=F32)
              + jnp.dot(hh, wm, preferred_element_type=F32)
              + jnp.dot(hm, wh, preferred_element_type=F32)) + rb_ref[...]

    lane = lax.broadcasted_iota(I32, (tm, LANES), 1)
    vals, idxs = [], []
    for _ in range(TOP_K):
        m = jnp.max(logits, axis=-1, keepdims=True)
        idx = jnp.min(jnp.where(logits == m, lane, LANES), axis=-1, keepdims=True)
        vals.append(m)
        idxs.append(idx)
        logits = jnp.where(lane == idx, -3.0e38, logits)
    es = [jnp.exp(v - vals[0]) for v in vals]
    den = es[0] + es[1] + es[2] + es[3]
    ti = jnp.zeros((tm, LANES), I32)
    tw = jnp.zeros((tm, LANES), F32)
    for k in range(TOP_K):
        ti = jnp.where(lane == k, idxs[k], ti)
        tw = jnp.where(lane == k, es[k] / den, tw)
    ti_ref[...] = ti
    tw_ref[...] = tw


def _merge(o_diff, o_ret, o_fox, proj, gate_b, w_branch, w_out, x2, g1, n2g, sh2, sc2, rw, rb, seq):
    n, d = x2.shape
    tm = min(ROW_TILE_OUT, seq)
    per_seq = seq // tm
    gcol = GATE_COL0 // d

    def rows(w):
        return pl.BlockSpec((tm, w), lambda i: (i, 0))

    def gspec(k):
        return pl.BlockSpec((tm, d), lambda i: (i, gcol + k))

    def whole(shape):
        return pl.BlockSpec(shape, lambda i: (0,) * len(shape))

    def per_b():
        return pl.BlockSpec((1, 1, d), lambda i: (i // per_seq, 0, 0))

    return pl.pallas_call(
        functools.partial(_merge_kernel, tm=tm),
        grid=(n // tm,),
        in_specs=[
            rows(BRANCH_WIDTH), rows(BRANCH_WIDTH), rows(BRANCH_WIDTH),
            gspec(0), gspec(1), gspec(2),
            whole((N_BRANCHES, 1, d)), whole((N_BRANCHES, BRANCH_WIDTH, d)), whole((d, d)),
            rows(d), per_b(), whole((1, d)), per_b(), per_b(),
            whole((d, LANES)), whole((1, LANES)),
        ],
        out_specs=[rows(d), pl.BlockSpec((2, tm, QUART), lambda i: (0, i, 0)), rows(LANES), rows(LANES)],
        out_shape=[
            jax.ShapeDtypeStruct((n, d), F32),
            jax.ShapeDtypeStruct((2, n, QUART), I32),
            jax.ShapeDtypeStruct((n, LANES), I32),
            jax.ShapeDtypeStruct((n, LANES), F32),
        ],
        compiler_params=_cparams(("arbitrary",)),
        name="merge_outproj_router",
    )(o_diff, o_ret, o_fox, proj, proj, proj, gate_b, w_branch, w_out, x2, g1, n2g, sh2, sc2, rw, rb)


def _prep_up_kernel(w_ref, o_ref, t_scr):
    t_scr[...] = w_ref[0].T
    f = t_scr.shape[0] // 2
    o_ref[0, 0:f, :] = t_scr[pl.ds(0, f, stride=2), :].astype(BF16)
    o_ref[0, f:, :] = t_scr[pl.ds(1, f, stride=2), :].astype(BF16)


def _prep_up(w_up):
    e, d, f2 = w_up.shape
    kt = LANES
    return pl.pallas_call(
        _prep_up_kernel,
        grid=(e, d // kt),
        in_specs=[pl.BlockSpec((1, kt, f2), lambda i, j: (i, j, 0))],
        out_specs=pl.BlockSpec((1, f2, kt), lambda i, j: (i, 0, j)),
        out_shape=jax.ShapeDtypeStruct((e, f2, d), BF16),
        scratch_shapes=[pltpu.VMEM((f2, kt), F32)],
        compiler_params=_cparams(("arbitrary", "arbitrary")),
        name="expert_up_weight_prep",
    )(w_up)


def _expert_kernel(te_ref, nu_ref, xs_ref, wu_ref, bu_ref, wd_ref, bd_ref, y_ref):
    nt = (((1,), (1,)), ((), ()))

    @pl.when(pl.program_id(0) < nu_ref[0])
    def _():
        lo, hi = _unpack_pairs(jnp.concatenate([xs_ref[0], xs_ref[1]], axis=1))
        lo = lo.astype(BF16)
        hi = hi.astype(BF16)

        def up_half(r0):
            return (lax.dot_general(lo, wu_ref[0, r0:r0 + D_FF, :HALF], nt, preferred_element_type=F32)
                    + lax.dot_general(hi, wu_ref[0, r0:r0 + D_FF, HALF:], nt, preferred_element_type=F32))

        glu = jnp.minimum(up_half(0) + bu_ref[0, :, :D_FF], SWIGLU_LIMIT)
        lin = jnp.clip(up_half(D_FF) + bu_ref[0, :, D_FF:], -SWIGLU_LIMIT, SWIGLU_LIMIT)
        act = glu * _sigmoid(SWIGLU_ALPHA * glu) * (lin + 1.0)
        down = jnp.dot(act.astype(BF16), wd_ref[0], preferred_element_type=F32) + bd_ref[0]
        yp = _pack_pairs(down)
        y_ref[0] = yp[:, :QUART]
        y_ref[1] = yp[:, QUART:]

    @pl.when(pl.program_id(0) >= nu_ref[0])
    def _():
        y_ref[...] = jnp.zeros(y_ref.shape, I32)


def _experts(tile_e, n_used, xs, w_up, b_up, w_down, b_down):
    r = xs.shape[1]
    tm = EXPERT_TILE
    d = D_MODEL
    grid_spec = pltpu.PrefetchScalarGridSpec(
        num_scalar_prefetch=2,
        grid=(r // tm,),
        in_specs=[
            pl.BlockSpec((2, tm, QUART), lambda i, te, nu: (0, i, 0)),
            pl.BlockSpec((1, 2 * D_FF, d), lambda i, te, nu: (te[i], 0, 0)),
            pl.BlockSpec((1, 1, 2 * D_FF), lambda i, te, nu: (te[i], 0, 0)),
            pl.BlockSpec((1, D_FF, d), lambda i, te, nu: (te[i], 0, 0)),
            pl.BlockSpec((1, 1, d), lambda i, te, nu: (te[i], 0, 0)),
        ],
        out_specs=pl.BlockSpec((2, tm, QUART), lambda i, te, nu: (0, i, 0)),
    )
    return pl.pallas_call(
        _expert_kernel,
        grid_spec=grid_spec,
        out_shape=jax.ShapeDtypeStruct((2, r, QUART), I32),
        compiler_params=_cparams(("arbitrary",)),
        name="expert_mlp",
    )(tile_e, n_used, xs, w_up, b_up, w_down, b_down)


def _combine_kernel(y_ref, tw_ref, x_ref, g2_ref, fg_ref, o_ref, *, final):
    tw = tw_ref[...]
    parts = [None] * 4
    for k in range(TOP_K):
        w = tw[:, k:k + 1]
        for piece in range(2):
            lo, hi = _unpack_pairs(y_ref[piece, :, k * QUART:(k + 1) * QUART])
            for slot, v in ((piece, lo), (2 + piece, hi)):
                parts[slot] = w * v if parts[slot] is None else parts[slot] + w * v
    ffn = jnp.concatenate(parts, axis=-1)
    xn = x_ref[...] + g2_ref[0] * ffn
    if final:
        r = lax.rsqrt(jnp.mean(xn * xn, axis=-1, keepdims=True) + NORM_EPS)
        xn = xn * r * fg_ref[...]
    o_ref[...] = xn


def _combine(ytok, top_w, x2, g2, final_g, seq, final):
    n, d = x2.shape
    tm = min(ROW_TILE_OUT, seq)
    per_seq = seq // tm
    return pl.pallas_call(
        functools.partial(_combine_kernel, final=final),
        grid=(n // tm,),
        in_specs=[
            pl.BlockSpec((2, tm, TOP_K * QUART), lambda i: (0, i, 0)),
            pl.BlockSpec((tm, LANES), lambda i: (i, 0)),
            pl.BlockSpec((tm, d), lambda i: (i, 0)),
            pl.BlockSpec((1, 1, d), lambda i: (i // per_seq, 0, 0)),
            pl.BlockSpec((1, d), lambda i: (0, 0)),
        ],
        out_specs=pl.BlockSpec((tm, d), lambda i: (i, 0)),
        out_shape=jax.ShapeDtypeStruct((n, d), F32),
        compiler_params=_cparams(("arbitrary",)),
        name="moe_combine",
    )(ytok, top_w, x2, g2, final_g)


def _route(top_idx, n_rows):
    tm = EXPERT_TILE
    flat_e = top_idx.reshape(-1)
    nk = flat_e.shape[0]
    ar = jnp.arange(nk, dtype=I32)
    sorted_e, order = lax.sort((flat_e, ar), num_keys=1, is_stable=True)
    bounds = jnp.searchsorted(sorted_e, jnp.arange(N_EXPERTS + 1, dtype=I32), side="left").astype(I32)
    ustart = bounds[:-1]
    sizes = bounds[1:] - ustart
    padded = ((sizes + tm - 1) // tm) * tm
    pend = jnp.cumsum(padded)
    pstart = pend - padded
    dest = pstart[sorted_e] + (ar - ustart[sorted_e])
    _, inv = lax.sort((order, ar), num_keys=1)
    pos = dest[inv]
    tiles = jnp.arange(n_rows // tm, dtype=I32) * tm
    tile_e = jnp.minimum(jnp.searchsorted(pend, tiles, side="right"), N_EXPERTS - 1).astype(I32)
    n_used = (pend[-1] // tm).astype(I32).reshape(1)
    rows = jnp.arange(n_rows, dtype=I32)
    e_r = jnp.repeat(tile_e, tm)
    p_r = jnp.clip(ustart[e_r] + rows - pstart[e_r], 0, nk - 1)
    src_tok = order[p_r] // TOP_K
    return pos.astype(I32), src_tok.astype(I32), tile_e, n_used


def _sc_gather(table, idx):
    m = idx.shape[0]
    width = table.shape[1]
    assert m % (SC_WINDOW * SC_WORKERS) == 0
    mesh = plsc.VectorSubcoreMesh(core_axis_name="c", subcore_axis_name="s")

    @pl.kernel(out_type=jax.ShapeDtypeStruct((m, width), table.dtype), mesh=mesh)
    def gather_rows(t_hbm, i_hbm, o_hbm):
        def body(i_vmem, o_vmem):
            pltpu.sync_copy(t_hbm.at[i_vmem.at[0]], o_vmem)

        pltpu.emit_pipeline(
            body,
            grid=(m // SC_WINDOW,),
            in_specs=[pl.BlockSpec((1, SC_WINDOW), lambda i: (0, i))],
            out_specs=[pl.BlockSpec((SC_WINDOW, width), lambda i: (i, 0))],
            core_axis_name=("c", "s"),
            dimension_semantics=(pltpu.PARALLEL,),
        )(i_hbm, o_hbm)

    return gather_rows(table, idx.reshape(1, m))


def _gather_pieces(table2, idx):
    t = table2.shape[1]
    out = _sc_gather(table2.reshape(2 * t, QUART), jnp.concatenate([idx, idx + t]))
    return out.reshape(2, idx.shape[0], QUART)


def kernel(x, c, norm1_g, norm2_g, ada_w, ada_b, w_in, gate_b, fox_fb, lam_q1, lam_k1, lam_q2, lam_k2,
           diff_subln_g, w_branch, w_out, router_w, router_b, exp_w_up, exp_b_up, exp_w_down,
           exp_b_down, final_g):
    batch, seq, d = x.shape
    depth = ada_w.shape[0]
    n = batch * seq
    x2 = x.reshape(n, d)

    mod = _ada_mod(c, ada_w, ada_b)
    tables = _retention_tables(seq)
    n_rows = n * TOP_K + N_EXPERTS * EXPERT_TILE
    blk = BRANCH_WIDTH // LANES

    for l in range(depth):
        lam_init = 0.8 - 0.6 * math.exp(-0.3 * l)
        sh1, sc1, g1, sh2, sc2, g2 = [m.reshape(batch, 1, d) for m in jnp.split(mod[l], 6, axis=-1)]

        wl = w_in[l]
        w_main = jnp.concatenate([wl[:, :GATE_COL0], wl[:, GATE_COL0 + FOX_HEADS:]], axis=1).astype(BF16)
        w_f = jnp.pad(wl[:, GATE_COL0:GATE_COL0 + FOX_HEADS], ((0, 0), (0, LANES - FOX_HEADS))).astype(BF16)
        proj, flog = _inproj(x2, norm1_g[l].reshape(1, d), sh1, sc1, w_main, w_f, seq)

        fb_pad = jnp.pad(fox_fb[l], (0, LANES - FOX_HEADS)).reshape(1, LANES)
        fc = _fcum(flog, fb_pad, batch, seq)
        frow = fc.reshape(batch, seq, LANES)[:, :, :FOX_HEADS].transpose(0, 2, 1)
        frow = frow.reshape(batch * blk, 2, seq)

        lamv = jnp.stack([lam_q1[l], lam_k1[l], lam_q2[l], lam_k2[l]])
        o_diff = _attention(proj, "diff", 0, blk, 2 * blk, batch, seq,
                            (lamv, diff_subln_g[l].reshape(1, LANES)), lam_init)
        o_ret = _retention(proj, tables, batch, seq)
        o_fox = _attention(proj, "fox", 7 * blk, 8 * blk, 9 * blk, batch, seq, (frow,))

        rw = jnp.pad(router_w[l], ((0, 0), (0, LANES - N_EXPERTS)))
        rb = jnp.pad(router_b[l], (0, LANES - N_EXPERTS), constant_values=NEG_INF).reshape(1, LANES)
        x2, h2p, top_idx, top_w = _merge(
            o_diff, o_ret, o_fox, proj, gate_b[l].reshape(N_BRANCHES, 1, d), w_branch[l].astype(BF16),
            w_out[l].astype(BF16), x2, g1, norm2_g[l].reshape(1, d), sh2, sc2, rw, rb, seq)

        pos, src_tok, tile_e, n_used = _route(top_idx[:, :TOP_K], n_rows)
        xs = _gather_pieces(h2p, src_tok)

        w_up = _prep_up(exp_w_up[l])
        bu = exp_b_up[l]
        b_up = jnp.concatenate([bu[:, 0::2], bu[:, 1::2]], axis=-1).reshape(N_EXPERTS, 1, 2 * D_FF)
        y = _experts(tile_e, n_used, xs, w_up, b_up, exp_w_down[l].astype(BF16),
                     exp_b_down[l].reshape(N_EXPERTS, 1, d))
        ytok = _gather_pieces(y, pos).reshape(2, n, TOP_K * QUART)
        x2 = _combine(ytok, top_w, x2, g2, final_g.reshape(1, d), seq, final=(l == depth - 1))

    return x2.reshape(batch, seq, d)
```

```python
import functools
import math

import jax
import jax.numpy as jnp
from jax import lax
from jax.experimental import pallas as pl
from jax.experimental.pallas import tpu as pltpu
from jax.experimental.pallas import tpu_sc as plsc

F32 = jnp.float32
BF16 = jnp.bfloat16
I32 = jnp.int32
U32 = jnp.uint32

D_MODEL = 1024
HEAD_DIM = 64
BRANCH_WIDTH = D_MODEL // 2
N_BRANCHES = 3
FOX_HEADS = BRANCH_WIDTH // HEAD_DIM
RET_HEADS = 4
RET_HEAD_DIM = 128
ROPE_BASE = 10000.0
RET_GAMMA_EXP0 = 5.0
N_EXPERTS = 32
TOP_K = 4
D_FF = D_MODEL
SWIGLU_LIMIT = 7.0
SWIGLU_ALPHA = 1.702
NORM_EPS = 1e-5
NEG_INF = -1e30

LANES = 128
MAIN_COLS = 8192
GATE_COL0 = 10 * BRANCH_WIDTH
HALF = D_MODEL // 2
QUART = HALF // 2
SC_WINDOW = 128
SC_WORKERS = 32
VMEM_LIMIT = 48 * 1024 * 1024

ROW_TILE_IN = 512
COL_TILE_IN = 1024
ATTN_TILE = 256
RET_CHUNK = 256
ROW_TILE_OUT = 256
EXPERT_TILE = 256
CUM_CHUNK = 256


def _cparams(sem):
    return pltpu.CompilerParams(dimension_semantics=sem, vmem_limit_bytes=VMEM_LIMIT)


def _sigmoid(x):
    return 1.0 / (1.0 + jnp.exp(-x))


def _pack_pairs(v):
    u = lax.bitcast_convert_type(v.astype(BF16).astype(F32), U32)
    half = v.shape[-1] // 2
    w = (u[:, half:] & jnp.uint32(0xFFFF0000)) | (u[:, :half] >> 16)
    return lax.bitcast_convert_type(w, I32)


def _unpack_pairs(w):
    u = lax.bitcast_convert_type(w, U32)
    lo = lax.bitcast_convert_type(u << 16, F32)
    hi = lax.bitcast_convert_type(u & jnp.uint32(0xFFFF0000), F32)
    return lo, hi


def _ada_kernel(c_ref, w_ref, b_ref, o_ref):
    c = c_ref[...]
    ca = c * _sigmoid(c)
    o_ref[0] = jnp.dot(ca, w_ref[0], preferred_element_type=F32) + b_ref[0]


def _ada_mod(c, ada_w, ada_b):
    depth, d, cols = ada_w.shape
    b = c.shape[0]
    tn = 1024
    return pl.pallas_call(
        _ada_kernel,
        grid=(depth, cols // tn),
        in_specs=[
            pl.BlockSpec((b, d), lambda l, j: (0, 0)),
            pl.BlockSpec((1, d, tn), lambda l, j: (l, 0, j)),
            pl.BlockSpec((1, 1, tn), lambda l, j: (l, 0, j)),
        ],
        out_specs=pl.BlockSpec((1, b, tn), lambda l, j: (l, 0, j)),
        out_shape=jax.ShapeDtypeStruct((depth, b, cols), F32),
        compiler_params=_cparams(("arbitrary", "arbitrary")),
        name="ada_mod",
    )(c, ada_w, ada_b.reshape(depth, 1, cols))


def _prep_win_kernel(w_ref, wm_ref, wf_ref):
    x = w_ref[...]
    wm_ref[:, :GATE_COL0] = x[:, :GATE_COL0].astype(BF16)
    wm_ref[:, GATE_COL0:] = x[:, GATE_COL0 + FOX_HEADS:].astype(BF16)
    lane = lax.broadcasted_iota(I32, (x.shape[0], LANES), 1)
    wf_ref[...] = jnp.where(lane < FOX_HEADS, x[:, GATE_COL0:GATE_COL0 + LANES], 0.0).astype(BF16)


def _prep_win(w):
    d, cols = w.shape
    rt = 128
    return pl.pallas_call(
        _prep_win_kernel,
        grid=(d // rt,),
        in_specs=[pl.BlockSpec((rt, cols), lambda i: (i, 0))],
        out_specs=[pl.BlockSpec((rt, MAIN_COLS), lambda i: (i, 0)), pl.BlockSpec((rt, LANES), lambda i: (i, 0))],
        out_shape=[jax.ShapeDtypeStruct((d, MAIN_COLS), BF16), jax.ShapeDtypeStruct((d, LANES), BF16)],
        compiler_params=_cparams(("arbitrary",)),
        name="inproj_weight_prep",
    )(w)


def _inproj_kernel(x_ref, g_ref, sh_ref, sc_ref, w_ref, wf_ref, proj_ref, fl_ref, h_scr):
    @pl.when(pl.program_id(1) == 0)
    def _():
        x = x_ref[...]
        r = lax.rsqrt(jnp.mean(x * x, axis=-1, keepdims=True) + NORM_EPS)
        h = x * r * g_ref[...]
        h = h * (1.0 + sc_ref[0]) + sh_ref[0]
        hb = h.astype(BF16)
        h_scr[...] = hb
        fl_ref[...] = jnp.dot(hb, wf_ref[...], preferred_element_type=F32)

    proj_ref[...] = jnp.dot(h_scr[...], w_ref[...], preferred_element_type=F32).astype(BF16)


def _inproj(x2, g, sh, sc, w_main, w_f, seq):
    n, d = x2.shape
    tm = min(ROW_TILE_IN, seq)
    tn = COL_TILE_IN
    per_seq = seq // tm
    return pl.pallas_call(
        _inproj_kernel,
        grid=(n // tm, MAIN_COLS // tn),
        in_specs=[
            pl.BlockSpec((tm, d), lambda i, j: (i, 0)),
            pl.BlockSpec((1, d), lambda i, j: (0, 0)),
            pl.BlockSpec((1, 1, d), lambda i, j: (i // per_seq, 0, 0)),
            pl.BlockSpec((1, 1, d), lambda i, j: (i // per_seq, 0, 0)),
            pl.BlockSpec((d, tn), lambda i, j: (0, j)),
            pl.BlockSpec((d, LANES), lambda i, j: (0, 0)),
        ],
        out_specs=[
            pl.BlockSpec((tm, tn), lambda i, j: (i, j)),
            pl.BlockSpec((tm, LANES), lambda i, j: (i, 0)),
        ],
        out_shape=[
            jax.ShapeDtypeStruct((n, MAIN_COLS), BF16),
            jax.ShapeDtypeStruct((n, LANES), F32),
        ],
        scratch_shapes=[pltpu.VMEM((tm, d), BF16)],
        compiler_params=_cparams(("arbitrary", "arbitrary")),
        name="inproj",
    )(x2, g, sh, sc, w_main, w_f)


def _split3(v):
    hi = v.astype(BF16)
    r1 = v - hi.astype(F32)
    mid = r1.astype(BF16)
    lo = (r1 - mid.astype(F32)).astype(BF16)
    return hi, mid, lo


def _fcum_kernel(fl_ref, fb_ref, o_ref, *, seq):
    c = min(CUM_CHUNK, seq)
    row = lax.broadcasted_iota(I32, (c, c), 0)
    col = lax.broadcasted_iota(I32, (c, c), 1)
    lower = (col <= row).astype(BF16)
    carry = jnp.zeros((1, LANES), F32)
    for ci in range(seq // c):
        x = fl_ref[pl.ds(ci * c, c), :] + fb_ref[...]
        lf = jnp.minimum(x, 0.0) - jnp.log(1.0 + jnp.exp(-jnp.abs(x)))
        hi, mid, lo = _split3(lf)
        cs = (jnp.dot(lower, hi, preferred_element_type=F32)
              + jnp.dot(lower, mid, preferred_element_type=F32)
              + jnp.dot(lower, lo, preferred_element_type=F32)) + carry
        o_ref[pl.ds(ci * c, c), :] = cs
        carry = cs[c - 1:c, :]


def _fcum(flog, fb_pad, batch, seq):
    n = flog.shape[0]
    return pl.pallas_call(
        functools.partial(_fcum_kernel, seq=seq),
        grid=(batch,),
        in_specs=[
            pl.BlockSpec((seq, LANES), lambda b: (b, 0)),
            pl.BlockSpec((1, LANES), lambda b: (0, 0)),
        ],
        out_specs=pl.BlockSpec((seq, LANES), lambda b: (b, 0)),
        out_shape=jax.ShapeDtypeStruct((n, LANES), F32),
        compiler_params=_cparams(("arbitrary",)),
        name="fox_decay_cumsum",
    )(flog, fb_pad)


def _attn_kernel(*refs, mode, t, seq, lam_init):
    if mode == "diff":
        q_ref, k_ref, v_ref, lam_ref, g_ref, o_ref = refs
        f_ref = None
        lv = lam_ref[...]
        lam = (jnp.exp(jnp.sum(lv[0:1, :] * lv[1:2, :])) - jnp.exp(jnp.sum(lv[2:3, :] * lv[3:4, :]))
               + lam_init)
    else:
        q_ref, k_ref, v_ref, f_ref, o_ref = refs
    lane = lax.broadcasted_iota(I32, (t, LANES), 1)
    row = lax.broadcasted_iota(I32, (t, t), 0)
    col = lax.broadcasted_iota(I32, (t, t), 1)
    nt = (((1,), (1,)), ((), ()))

    for qi in range(seq // t):
        lo = qi * t
        q = q_ref[lo:lo + t, :] * jnp.asarray(HEAD_DIM ** -0.5, BF16)
        zero = jnp.zeros_like(q)
        outs = []
        for idx in range(2):
            qm = jnp.where(lane < HEAD_DIM, q, zero) if idx == 0 else jnp.where(lane >= HEAD_DIM, q, zero)
            sd = lax.dot_general(qm, k_ref[lo:lo + t, :], nt, preferred_element_type=F32)
            if f_ref is not None:
                sd = sd - f_ref[0, idx:idx + 1, lo:lo + t]
            sd = jnp.where(col <= row, sd, NEG_INF)
            m = jnp.max(sd, axis=-1, keepdims=True)
            if qi > 0:
                sm = lax.dot_general(qm, k_ref[0:lo, :], nt, preferred_element_type=F32)
                if f_ref is not None:
                    sm = sm - f_ref[0, idx:idx + 1, 0:lo]
                m = jnp.maximum(m, jnp.max(sm, axis=-1, keepdims=True))
            pd = jnp.exp(sd - m)
            l = jnp.sum(pd, axis=-1, keepdims=True)
            acc = jnp.dot(pd.astype(BF16), v_ref[lo:lo + t, :], preferred_element_type=F32)
            if qi > 0:
                pm = jnp.exp(sm - m)
                l = l + jnp.sum(pm, axis=-1, keepdims=True)
                acc = acc + jnp.dot(pm.astype(BF16), v_ref[0:lo, :], preferred_element_type=F32)
            outs.append(acc / l)
        if mode == "diff":
            o = outs[0] - lam * outs[1]
            r = lax.rsqrt(jnp.mean(o * o, axis=-1, keepdims=True) + NORM_EPS)
            o = o * r * g_ref[...] * (1.0 - lam_init)
        else:
            o = jnp.where(lane < HEAD_DIM, outs[0], outs[1])
        o_ref[lo:lo + t, :] = o.astype(BF16)


def _attention(proj, mode, col_q, col_k, col_v, batch, seq, extra, lam_init=0.0):
    n = proj.shape[0]
    t = min(ATTN_TILE, seq)
    n_blk = BRANCH_WIDTH // LANES
    in_specs = [
        pl.BlockSpec((seq, LANES), lambda b, h: (b, col_q + h)),
        pl.BlockSpec((seq, LANES), lambda b, h: (b, col_k + h)),
        pl.BlockSpec((seq, LANES), lambda b, h: (b, col_v + h)),
    ]
    if mode == "diff":
        lamv, g = extra
        in_specs += [pl.BlockSpec((4, HEAD_DIM), lambda b, h: (0, 0)),
                     pl.BlockSpec((1, LANES), lambda b, h: (0, 0))]
        args = (proj, proj, proj, lamv, g)
    else:
        (frow,) = extra
        in_specs += [pl.BlockSpec((1, 2, seq), lambda b, h: (b * n_blk + h, 0, 0))]
        args = (proj, proj, proj, frow)
    return pl.pallas_call(
        functools.partial(_attn_kernel, mode=mode, t=t, seq=seq, lam_init=lam_init),
        grid=(batch, n_blk),
        in_specs=in_specs,
        out_specs=pl.BlockSpec((seq, LANES), lambda b, h: (b, h)),
        out_shape=jax.ShapeDtypeStruct((n, BRANCH_WIDTH), BF16),
        compiler_params=_cparams(("arbitrary", "arbitrary")),
        name="attn_" + mode,
    )(*args)


def _ret_kernel(q_ref, k_ref, v_ref, g_ref, cos_ref, sin_ref, dm_ref, qd_ref, kd_ref, cd_ref,
                o_ref, st_scr, *, seq, c):
    st_scr[...] = jnp.zeros(st_scr.shape, F32)
    half = RET_HEAD_DIM // 2

    def body(ci, carry):
        sl = pl.ds(pl.multiple_of(ci * c, c), c)
        cos = cos_ref[sl, :]
        sin = sin_ref[sl, :]
        q = q_ref[sl, :].astype(F32)
        k = k_ref[sl, :].astype(F32)
        qr = q * cos + pltpu.roll(q, half, 1) * sin
        kr = (k * cos + pltpu.roll(k, half, 1) * sin) * (RET_HEAD_DIM ** -0.5)
        v = v_ref[sl, :]
        sc = lax.dot_general(qr.astype(BF16), kr.astype(BF16), (((1,), (1,)), ((), ())),
                             preferred_element_type=F32) * dm_ref[0]
        inner = jnp.dot(sc.astype(BF16), v, preferred_element_type=F32)
        st = st_scr[...]
        cross = jnp.dot((qr * qd_ref[0]).astype(BF16), st.astype(BF16), preferred_element_type=F32)
        o = inner + cross
        kt = (kr * kd_ref[0]).T.astype(BF16)
        st_scr[...] = st * cd_ref[0] + jnp.dot(kt, v, preferred_element_type=F32)
        mu = jnp.mean(o, axis=-1, keepdims=True)
        oc = o - mu
        var = jnp.mean(oc * oc, axis=-1, keepdims=True)
        on = oc * lax.rsqrt(var + NORM_EPS)
        g = g_ref[sl, :].astype(F32)
        o_ref[sl, :] = (on * (g * _sigmoid(g))).astype(BF16)
        return carry

    lax.fori_loop(0, seq // c, body, 0)


def _retention(proj, tables, batch, seq):
    n = proj.shape[0]
    c = min(RET_CHUNK, seq)
    cos, sin, dm, qd, kd, cd = tables
    col = BRANCH_WIDTH * 3 // LANES
    step = BRANCH_WIDTH // LANES

    def pspec(k):
        return pl.BlockSpec((seq, LANES), lambda b, h: (b, col + k * step + h))

    return pl.pallas_call(
        functools.partial(_ret_kernel, seq=seq, c=c),
        grid=(batch, RET_HEADS),
        in_specs=[
            pspec(0), pspec(1), pspec(2), pspec(3),
            pl.BlockSpec((seq, LANES), lambda b, h: (0, 0)),
            pl.BlockSpec((seq, LANES), lambda b, h: (0, 0)),
            pl.BlockSpec((1, c, c), lambda b, h: (h, 0, 0)),
            pl.BlockSpec((1, c, LANES), lambda b, h: (h, 0, 0)),
            pl.BlockSpec((1, c, LANES), lambda b, h: (h, 0, 0)),
            pl.BlockSpec((1, LANES, LANES), lambda b, h: (h, 0, 0)),
        ],
        out_specs=pl.BlockSpec((seq, LANES), lambda b, h: (b, h)),
        out_shape=jax.ShapeDtypeStruct((n, BRANCH_WIDTH), BF16),
        scratch_shapes=[pltpu.VMEM((RET_HEAD_DIM, RET_HEAD_DIM), F32)],
        compiler_params=_cparams(("arbitrary", "arbitrary")),
        name="retention",
    )(proj, proj, proj, proj, cos, sin, dm, qd, kd, cd)


def _retention_tables(seq):
    c = min(RET_CHUNK, seq)
    d = RET_HEAD_DIM
    inv = ROPE_BASE ** (-jnp.arange(0, d, 2, dtype=F32) / d)
    ang = jnp.arange(seq).astype(F32)[:, None] * inv[None, :]
    cos = jnp.cos(ang)
    sin = jnp.sin(ang)
    cos_full = jnp.concatenate([cos, cos], axis=-1)
    sin_signed = jnp.concatenate([-sin, sin], axis=-1)
    log_gamma = jnp.log1p(-jnp.exp2(-RET_GAMMA_EXP0 - jnp.arange(RET_HEADS, dtype=F32)))
    idx = jnp.arange(c, dtype=F32)
    dist = idx[:, None] - idx[None, :]
    dm = jnp.where(dist >= 0, jnp.exp(log_gamma[:, None, None] * jnp.maximum(dist, 0.0)), 0.0)
    qd = jnp.exp(log_gamma[:, None] * (idx + 1.0))
    kd = jnp.exp(log_gamma[:, None] * (c - 1.0 - idx))
    cd = jnp.exp(log_gamma * c)
    qd = jnp.broadcast_to(qd[:, :, None], (RET_HEADS, c, LANES))
    kd = jnp.broadcast_to(kd[:, :, None], (RET_HEADS, c, LANES))
    cd = jnp.broadcast_to(cd[:, None, None], (RET_HEADS, LANES, LANES))
    return cos_full, sin_signed, dm, qd, kd, cd


def _merge_kernel(od_ref, or_ref, of_ref, gd_ref, gr_ref, gf_ref, gb_ref, wb_ref, wo_ref, x_ref,
                  g1_ref, n2_ref, sh_ref, sc_ref, rw_ref, rb_ref,
                  xo_ref, hp_ref, ti_ref, tw_ref, cnt_ref, cnt_scr, *, tm):
    merged = None
    for i, (o_ref, gate_ref) in enumerate(((od_ref, gd_ref), (or_ref, gr_ref), (of_ref, gf_ref))):
        gate = _sigmoid(gate_ref[...].astype(F32) + gb_ref[i])
        term = gate * jnp.dot(o_ref[...], wb_ref[i], preferred_element_type=F32)
        merged = term if merged is None else merged + term
    mix = jnp.dot(merged.astype(BF16), wo_ref[...], preferred_element_type=F32)
    xn = x_ref[...] + g1_ref[0] * mix
    xo_ref[...] = xn

    r = lax.rsqrt(jnp.mean(xn * xn, axis=-1, keepdims=True) + NORM_EPS)
    h = xn * r * n2_ref[...]
    h = h * (1.0 + sc_ref[0]) + sh_ref[0]
    hp = _pack_pairs(h)
    hp_ref[0] = hp[:, :QUART]
    hp_ref[1] = hp[:, QUART:]

    hh, hm, _ = _split3(h)
    wh, wm, _ = _split3(rw_ref[...])
    logits = (jnp.dot(hh, wh, preferred_element_type=F32)
              + jnp.dot(hh, wm, preferred_element_type=F32)
              + jnp.dot(hm, wh, preferred_element_type=F32)) + rb_ref[...]

    lane = lax.broadcasted_iota(I32, (tm, LANES), 1)
    vals, idxs = [], []
    for _ in range(TOP_K):
        m = jnp.max(logits, axis=-1, keepdims=True)
        idx = jnp.min(jnp.where(logits == m, lane, LANES), axis=-1, keepdims=True)
        vals.append(m)
        idxs.append(idx)
        logits = jnp.where(lane == idx, -3.0e38, logits)
    es = [jnp.exp(v - vals[0]) for v in vals]
    den = es[0] + es[1] + es[2] + es[3]

    @pl.when(pl.program_id(0) == 0)
    def _():
        cnt_scr[...] = jnp.zeros(cnt_scr.shape, F32)

    hits = [lane == idxs[k] for k in range(TOP_K)]
    picked = (hits[0] | hits[1] | hits[2] | hits[3])
    r_i = lax.broadcasted_iota(I32, (tm, tm), 0)
    c_i = lax.broadcasted_iota(I32, (tm, tm), 1)
    before = jnp.dot((c_i < r_i).astype(BF16), picked.astype(BF16), preferred_element_type=F32)
    base = before + cnt_scr[...]
    cnt_scr[...] = cnt_scr[...] + jnp.sum(picked.astype(F32), axis=0, keepdims=True)
    cnt_ref[...] = cnt_scr[...].astype(I32)

    ti = jnp.zeros((tm, LANES), I32)
    tw = jnp.zeros((tm, LANES), F32)
    for k in range(TOP_K):
        rank = jnp.sum(jnp.where(hits[k], base, 0.0), axis=-1, keepdims=True).astype(I32)
        ti = jnp.where(lane == k, idxs[k], ti)
        ti = jnp.where(lane == TOP_K + k, rank, ti)
        tw = jnp.where(lane == k, es[k] / den, tw)
    ti_ref[...] = ti
    tw_ref[...] = tw


def _merge(o_diff, o_ret, o_fox, proj, gate_b, w_branch, w_out, x2, g1, n2g, sh2, sc2, rw, rb, seq):
    n, d = x2.shape
    tm = min(ROW_TILE_OUT, seq)
    per_seq = seq // tm
    gcol = GATE_COL0 // d

    def rows(w):
        return pl.BlockSpec((tm, w), lambda i: (i, 0))

    def gspec(k):
        return pl.BlockSpec((tm, d), lambda i: (i, gcol + k))

    def whole(shape):
        return pl.BlockSpec(shape, lambda i: (0,) * len(shape))

    def per_b():
        return pl.BlockSpec((1, 1, d), lambda i: (i // per_seq, 0, 0))

    return pl.pallas_call(
        functools.partial(_merge_kernel, tm=tm),
        grid=(n // tm,),
        in_specs=[
            rows(BRANCH_WIDTH), rows(BRANCH_WIDTH), rows(BRANCH_WIDTH),
            gspec(0), gspec(1), gspec(2),
            whole((N_BRANCHES, 1, d)), whole((N_BRANCHES, BRANCH_WIDTH, d)), whole((d, d)),
            rows(d), per_b(), whole((1, d)), per_b(), per_b(),
            whole((d, LANES)), whole((1, LANES)),
        ],
        out_specs=[rows(d), pl.BlockSpec((2, tm, QUART), lambda i: (0, i, 0)), rows(LANES), rows(LANES),
                   whole((1, LANES))],
        out_shape=[
            jax.ShapeDtypeStruct((n, d), F32),
            jax.ShapeDtypeStruct((2, n, QUART), I32),
            jax.ShapeDtypeStruct((n, LANES), I32),
            jax.ShapeDtypeStruct((n, LANES), F32),
            jax.ShapeDtypeStruct((1, LANES), I32),
        ],
        scratch_shapes=[pltpu.VMEM((1, LANES), F32)],
        compiler_params=_cparams(("arbitrary",)),
        name="merge_outproj_router",
    )(o_diff, o_ret, o_fox, proj, proj, proj, gate_b, w_branch, w_out, x2, g1, n2g, sh2, sc2, rw, rb)


def _prep_up_kernel(w_ref, o_ref, t_scr):
    t_scr[...] = w_ref[0].T
    f = t_scr.shape[0] // 2
    o_ref[0, 0:f, :] = t_scr[pl.ds(0, f, stride=2), :].astype(BF16)
    o_ref[0, f:, :] = t_scr[pl.ds(1, f, stride=2), :].astype(BF16)


def _prep_up(w_up):
    e, d, f2 = w_up.shape
    kt = LANES
    return pl.pallas_call(
        _prep_up_kernel,
        grid=(e, d // kt),
        in_specs=[pl.BlockSpec((1, kt, f2), lambda i, j: (i, j, 0))],
        out_specs=pl.BlockSpec((1, f2, kt), lambda i, j: (i, 0, j)),
        out_shape=jax.ShapeDtypeStruct((e, f2, d), BF16),
        scratch_shapes=[pltpu.VMEM((f2, kt), F32)],
        compiler_params=_cparams(("arbitrary", "arbitrary")),
        name="expert_up_weight_prep",
    )(w_up)


def _expert_kernel(te_ref, nu_ref, xs_ref, wu_ref, bu_ref, wd_ref, bd_ref, y_ref):
    nt = (((1,), (1,)), ((), ()))

    @pl.when(pl.program_id(0) < nu_ref[0])
    def _():
        lo, hi = _unpack_pairs(jnp.concatenate([xs_ref[0], xs_ref[1]], axis=1))
        lo = lo.astype(BF16)
        hi = hi.astype(BF16)

        def up_half(r0):
            return (lax.dot_general(lo, wu_ref[0, r0:r0 + D_FF, :HALF], nt, preferred_element_type=F32)
                    + lax.dot_general(hi, wu_ref[0, r0:r0 + D_FF, HALF:], nt, preferred_element_type=F32))

        glu = jnp.minimum(up_half(0) + bu_ref[0, :, :D_FF], SWIGLU_LIMIT)
        lin = jnp.clip(up_half(D_FF) + bu_ref[0, :, D_FF:], -SWIGLU_LIMIT, SWIGLU_LIMIT)
        act = glu * _sigmoid(SWIGLU_ALPHA * glu) * (lin + 1.0)
        down = jnp.dot(act.astype(BF16), wd_ref[0], preferred_element_type=F32) + bd_ref[0]
        yp = _pack_pairs(down)
        y_ref[0] = yp[:, :QUART]
        y_ref[1] = yp[:, QUART:]

    @pl.when(pl.program_id(0) >= nu_ref[0])
    def _():
        y_ref[...] = jnp.zeros(y_ref.shape, I32)


def _experts(tile_e, n_used, xs, w_up, b_up, w_down, b_down):
    r = xs.shape[1]
    tm = EXPERT_TILE
    d = D_MODEL
    grid_spec = pltpu.PrefetchScalarGridSpec(
        num_scalar_prefetch=2,
        grid=(r // tm,),
        in_specs=[
            pl.BlockSpec((2, tm, QUART), lambda i, te, nu: (0, i, 0)),
            pl.BlockSpec((1, 2 * D_FF, d), lambda i, te, nu: (te[i], 0, 0)),
            pl.BlockSpec((1, 1, 2 * D_FF), lambda i, te, nu: (te[i], 0, 0)),
            pl.BlockSpec((1, D_FF, d), lambda i, te, nu: (te[i], 0, 0)),
            pl.BlockSpec((1, 1, d), lambda i, te, nu: (te[i], 0, 0)),
        ],
        out_specs=pl.BlockSpec((2, tm, QUART), lambda i, te, nu: (0, i, 0)),
    )
    return pl.pallas_call(
        _expert_kernel,
        grid_spec=grid_spec,
        out_shape=jax.ShapeDtypeStruct((2, r, QUART), I32),
        compiler_params=_cparams(("arbitrary",)),
        name="expert_mlp",
    )(tile_e, n_used, xs, w_up, b_up, w_down, b_down)


def _combine_kernel(y_ref, tw_ref, x_ref, g2_ref, fg_ref, o_ref, *, final):
    tw = tw_ref[...]
    parts = [None] * 4
    for k in range(TOP_K):
        w = tw[:, k:k + 1]
        for piece in range(2):
            lo, hi = _unpack_pairs(y_ref[piece, k])
            for slot, v in ((piece, lo), (2 + piece, hi)):
                parts[slot] = w * v if parts[slot] is None else parts[slot] + w * v
    ffn = jnp.concatenate(parts, axis=-1)
    xn = x_ref[...] + g2_ref[0] * ffn
    if final:
        r = lax.rsqrt(jnp.mean(xn * xn, axis=-1, keepdims=True) + NORM_EPS)
        xn = xn * r * fg_ref[...]
    o_ref[...] = xn


def _combine(ytok, top_w, x2, g2, final_g, seq, final):
    n, d = x2.shape
    tm = min(ROW_TILE_OUT, seq)
    per_seq = seq // tm
    return pl.pallas_call(
        functools.partial(_combine_kernel, final=final),
        grid=(n // tm,),
        in_specs=[
            pl.BlockSpec((2, TOP_K, tm, QUART), lambda i: (0, 0, i, 0)),
            pl.BlockSpec((tm, LANES), lambda i: (i, 0)),
            pl.BlockSpec((tm, d), lambda i: (i, 0)),
            pl.BlockSpec((1, 1, d), lambda i: (i // per_seq, 0, 0)),
            pl.BlockSpec((1, d), lambda i: (0, 0)),
        ],
        out_specs=pl.BlockSpec((tm, d), lambda i: (i, 0)),
        out_shape=jax.ShapeDtypeStruct((n, d), F32),
        compiler_params=_cparams(("arbitrary",)),
        name="moe_combine",
    )(ytok, top_w, x2, g2, final_g)


def _route(top_idx, counts, n_rows):
    tm = EXPERT_TILE
    e = top_idx[:, :TOP_K]
    rank = top_idx[:, TOP_K:2 * TOP_K]
    sizes = counts[0, :N_EXPERTS]
    padded = ((sizes + tm - 1) // tm) * tm
    pend = jnp.cumsum(padded)
    pstart = pend - padded
    onehot = e[:, :, None] == jnp.arange(N_EXPERTS, dtype=I32)[None, None, :]
    pos = jnp.sum(jnp.where(onehot, pstart[None, None, :], 0), axis=-1) + rank
    tiles = jnp.arange(n_rows // tm, dtype=I32) * tm
    tile_e = jnp.minimum(jnp.sum(tiles[:, None] >= pend[None, :], axis=-1), N_EXPERTS - 1).astype(I32)
    n_used = (pend[-1] // tm).astype(I32).reshape(1)
    return pos.T.astype(I32), tile_e, n_used


def _sc_gather(table, idx):
    m = idx.shape[0]
    width = table.shape[1]
    assert m % (SC_WINDOW * SC_WORKERS) == 0
    mesh = plsc.VectorSubcoreMesh(core_axis_name="c", subcore_axis_name="s")

    @pl.kernel(out_type=jax.ShapeDtypeStruct((m, width), table.dtype), mesh=mesh)
    def gather_rows(t_hbm, i_hbm, o_hbm):
        def body(i_vmem, o_vmem):
            pltpu.sync_copy(t_hbm.at[i_vmem.at[0]], o_vmem)

        pltpu.emit_pipeline(
            body,
            grid=(m // SC_WINDOW,),
            in_specs=[pl.BlockSpec((1, SC_WINDOW), lambda i: (0, i))],
            out_specs=[pl.BlockSpec((SC_WINDOW, width), lambda i: (i, 0))],
            core_axis_name=("c", "s"),
            dimension_semantics=(pltpu.PARALLEL,),
        )(i_hbm, o_hbm)

    return gather_rows(table, idx.reshape(1, m))


def _sc_scatter(rows, idx, n_out):
    t, width = rows.shape
    picks = idx.shape[0]
    assert t % SC_WINDOW == 0
    mesh = plsc.VectorSubcoreMesh(core_axis_name="c", subcore_axis_name="s")

    @pl.kernel(out_type=jax.ShapeDtypeStruct((n_out, width), rows.dtype), mesh=mesh)
    def scatter_rows(x_hbm, i_hbm, o_hbm):
        def body(x_vmem, i_vmem):
            for k in range(picks):
                pltpu.sync_copy(x_vmem, o_hbm.at[i_vmem.at[k]])

        pltpu.emit_pipeline(
            body,
            grid=(t // SC_WINDOW,),
            in_specs=[pl.BlockSpec((SC_WINDOW, width), lambda i: (i, 0)),
                      pl.BlockSpec((picks, SC_WINDOW), lambda i: (0, i))],
            out_specs=[],
            core_axis_name=("c", "s"),
            dimension_semantics=(pltpu.PARALLEL,),
        )(x_hbm, i_hbm)

    return scatter_rows(rows, idx)


def _dispatch(h2p, pos_t, n_rows):
    n = h2p.shape[1]
    idx = jnp.concatenate([pos_t, pos_t + n_rows], axis=1)
    return _sc_scatter(h2p.reshape(2 * n, QUART), idx, 2 * n_rows).reshape(2, n_rows, QUART)


def _collect(y, pos_t):
    n_rows = y.shape[1]
    flat = pos_t.reshape(-1)
    out = _sc_gather(y.reshape(2 * n_rows, QUART), jnp.concatenate([flat, flat + n_rows]))
    return out.reshape(2, TOP_K, pos_t.shape[1], QUART)


def kernel(x, c, norm1_g, norm2_g, ada_w, ada_b, w_in, gate_b, fox_fb, lam_q1, lam_k1, lam_q2, lam_k2,
           diff_subln_g, w_branch, w_out, router_w, router_b, exp_w_up, exp_b_up, exp_w_down,
           exp_b_down, final_g):
    batch, seq, d = x.shape
    depth = ada_w.shape[0]
    n = batch * seq
    x2 = x.reshape(n, d)

    mod = _ada_mod(c, ada_w, ada_b)
    tables = _retention_tables(seq)
    n_rows = n * TOP_K + N_EXPERTS * EXPERT_TILE
    blk = BRANCH_WIDTH // LANES

    for l in range(depth):
        lam_init = 0.8 - 0.6 * math.exp(-0.3 * l)
        sh1, sc1, g1, sh2, sc2, g2 = [m.reshape(batch, 1, d) for m in jnp.split(mod[l], 6, axis=-1)]

        w_main, w_f = _prep_win(w_in[l])
        proj, flog = _inproj(x2, norm1_g[l].reshape(1, d), sh1, sc1, w_main, w_f, seq)

        fb_pad = jnp.pad(fox_fb[l], (0, LANES - FOX_HEADS)).reshape(1, LANES)
        fc = _fcum(flog, fb_pad, batch, seq)
        frow = fc.reshape(batch, seq, LANES)[:, :, :FOX_HEADS].transpose(0, 2, 1)
        frow = frow.reshape(batch * blk, 2, seq)

        lamv = jnp.stack([lam_q1[l], lam_k1[l], lam_q2[l], lam_k2[l]])
        o_diff = _attention(proj, "diff", 0, blk, 2 * blk, batch, seq,
                            (lamv, diff_subln_g[l].reshape(1, LANES)), lam_init)
        o_ret = _retention(proj, tables, batch, seq)
        o_fox = _attention(proj, "fox", 7 * blk, 8 * blk, 9 * blk, batch, seq, (frow,))

        rw = jnp.pad(router_w[l], ((0, 0), (0, LANES - N_EXPERTS)))
        rb = jnp.pad(router_b[l], (0, LANES - N_EXPERTS), constant_values=NEG_INF).reshape(1, LANES)
        x2, h2p, top_idx, top_w, counts = _merge(
            o_diff, o_ret, o_fox, proj, gate_b[l].reshape(N_BRANCHES, 1, d), w_branch[l].astype(BF16),
            w_out[l].astype(BF16), x2, g1, norm2_g[l].reshape(1, d), sh2, sc2, rw, rb, seq)

        pos_t, tile_e, n_used = _route(top_idx, counts, n_rows)
        xs = _dispatch(h2p, pos_t, n_rows)

        w_up = _prep_up(exp_w_up[l])
        bu = exp_b_up[l]
        b_up = jnp.concatenate([bu[:, 0::2], bu[:, 1::2]], axis=-1).reshape(N_EXPERTS, 1, 2 * D_FF)
        y = _experts(tile_e, n_used, xs, w_up, b_up, exp_w_down[l].astype(BF16),
                     exp_b_down[l].reshape(N_EXPERTS, 1, d))
        ytok = _collect(y, pos_t)
        x2 = _combine(ytok, top_w, x2, g2, final_g.reshape(1, d), seq, final=(l == depth - 1))

    return x2.reshape(batch, seq, d)
```

```python
import functools
import math

import jax
import jax.numpy as jnp
from jax import lax
from jax.experimental import pallas as pl
from jax.experimental.pallas import tpu as pltpu
from jax.experimental.pallas import tpu_sc as plsc

F32 = jnp.float32
BF16 = jnp.bfloat16
I32 = jnp.int32
U32 = jnp.uint32

D_MODEL = 1024
HEAD_DIM = 64
BRANCH_WIDTH = D_MODEL // 2
N_BRANCHES = 3
FOX_HEADS = BRANCH_WIDTH // HEAD_DIM
RET_HEADS = 4
RET_HEAD_DIM = 128
ROPE_BASE = 10000.0
RET_GAMMA_EXP0 = 5.0
N_EXPERTS = 32
TOP_K = 4
D_FF = D_MODEL
SWIGLU_LIMIT = 7.0
SWIGLU_ALPHA = 1.702
NORM_EPS = 1e-5
NEG_INF = -1e30

LANES = 128
MAIN_COLS = 8192
GATE_COL0 = 10 * BRANCH_WIDTH
HALF = D_MODEL // 2
QUART = HALF // 2
SC_WINDOW = 128
SC_WORKERS = 32
VMEM_LIMIT = 48 * 1024 * 1024

ROW_TILE_IN = 512
COL_TILE_IN = 1024
ATTN_TILE = 256
RET_CHUNK = 256
ROW_TILE_OUT = 256
EXPERT_TILE = 256
CUM_CHUNK = 256


def _cparams(sem):
    return pltpu.CompilerParams(dimension_semantics=sem, vmem_limit_bytes=VMEM_LIMIT)


def _sigmoid(x):
    return 0.5 * jnp.tanh(0.5 * x) + 0.5


def _pack_pairs(v):
    u = lax.bitcast_convert_type(v.astype(BF16).astype(F32), U32)
    half = v.shape[-1] // 2
    w = (u[:, half:] & jnp.uint32(0xFFFF0000)) | (u[:, :half] >> 16)
    return lax.bitcast_convert_type(w, I32)


def _unpack_pairs(w):
    u = lax.bitcast_convert_type(w, U32)
    lo = lax.bitcast_convert_type(u << 16, F32)
    hi = lax.bitcast_convert_type(u & jnp.uint32(0xFFFF0000), F32)
    return lo, hi


def _ada_kernel(c_ref, w_ref, b_ref, o_ref):
    c = c_ref[...]
    ca = c * _sigmoid(c)
    o_ref[0] = jnp.dot(ca, w_ref[0], preferred_element_type=F32) + b_ref[0]


def _ada_mod(c, ada_w, ada_b):
    depth, d, cols = ada_w.shape
    b = c.shape[0]
    tn = 1024
    return pl.pallas_call(
        _ada_kernel,
        grid=(depth, cols // tn),
        in_specs=[
            pl.BlockSpec((b, d), lambda l, j: (0, 0)),
            pl.BlockSpec((1, d, tn), lambda l, j: (l, 0, j)),
            pl.BlockSpec((1, 1, tn), lambda l, j: (l, 0, j)),
        ],
        out_specs=pl.BlockSpec((1, b, tn), lambda l, j: (l, 0, j)),
        out_shape=jax.ShapeDtypeStruct((depth, b, cols), F32),
        compiler_params=_cparams(("arbitrary", "arbitrary")),
        name="ada_mod",
    )(c, ada_w, ada_b.reshape(depth, 1, cols))


def _prep_win_kernel(w_ref, wm_ref, wf_ref):
    x = w_ref[...]
    wm_ref[:, :GATE_COL0] = x[:, :GATE_COL0].astype(BF16)
    wm_ref[:, GATE_COL0:] = x[:, GATE_COL0 + FOX_HEADS:].astype(BF16)
    lane = lax.broadcasted_iota(I32, (x.shape[0], LANES), 1)
    wf_ref[...] = jnp.where(lane < FOX_HEADS, x[:, GATE_COL0:GATE_COL0 + LANES], 0.0).astype(BF16)


def _prep_win(w):
    d, cols = w.shape
    rt = 128
    return pl.pallas_call(
        _prep_win_kernel,
        grid=(d // rt,),
        in_specs=[pl.BlockSpec((rt, cols), lambda i: (i, 0))],
        out_specs=[pl.BlockSpec((rt, MAIN_COLS), lambda i: (i, 0)), pl.BlockSpec((rt, LANES), lambda i: (i, 0))],
        out_shape=[jax.ShapeDtypeStruct((d, MAIN_COLS), BF16), jax.ShapeDtypeStruct((d, LANES), BF16)],
        compiler_params=_cparams(("arbitrary",)),
        name="inproj_weight_prep",
    )(w)


def _inproj_kernel(x_ref, g_ref, sh_ref, sc_ref, w_ref, wf_ref, proj_ref, fl_ref):
    x = x_ref[...]
    r = lax.rsqrt(jnp.mean(x * x, axis=-1, keepdims=True) + NORM_EPS)
    h = x * r * g_ref[...]
    h = h * (1.0 + sc_ref[0]) + sh_ref[0]
    hb = h.astype(BF16)
    fl_ref[...] = jnp.dot(hb, wf_ref[...], preferred_element_type=F32)
    for j in range(MAIN_COLS // COL_TILE_IN):
        cs = slice(j * COL_TILE_IN, (j + 1) * COL_TILE_IN)
        proj_ref[:, cs] = jnp.dot(hb, w_ref[:, cs], preferred_element_type=F32).astype(BF16)


def _inproj(x2, g, sh, sc, w_main, w_f, seq):
    n, d = x2.shape
    tm = min(ROW_TILE_IN, seq)
    per_seq = seq // tm
    resident = pl.Buffered(1)
    return pl.pallas_call(
        _inproj_kernel,
        grid=(n // tm,),
        in_specs=[
            pl.BlockSpec((tm, d), lambda i: (i, 0)),
            pl.BlockSpec((1, d), lambda i: (0, 0)),
            pl.BlockSpec((1, 1, d), lambda i: (i // per_seq, 0, 0)),
            pl.BlockSpec((1, 1, d), lambda i: (i // per_seq, 0, 0)),
            pl.BlockSpec((d, MAIN_COLS), lambda i: (0, 0), pipeline_mode=resident),
            pl.BlockSpec((d, LANES), lambda i: (0, 0), pipeline_mode=resident),
        ],
        out_specs=[
            pl.BlockSpec((tm, MAIN_COLS), lambda i: (i, 0)),
            pl.BlockSpec((tm, LANES), lambda i: (i, 0)),
        ],
        out_shape=[
            jax.ShapeDtypeStruct((n, MAIN_COLS), BF16),
            jax.ShapeDtypeStruct((n, LANES), F32),
        ],
        compiler_params=_cparams(("arbitrary",)),
        name="inproj",
    )(x2, g, sh, sc, w_main, w_f)


def _split3(v):
    hi = v.astype(BF16)
    r1 = v - hi.astype(F32)
    mid = r1.astype(BF16)
    lo = (r1 - mid.astype(F32)).astype(BF16)
    return hi, mid, lo


def _fcum_kernel(fl_ref, fb_ref, o_ref, *, seq):
    c = min(CUM_CHUNK, seq)
    row = lax.broadcasted_iota(I32, (c, c), 0)
    col = lax.broadcasted_iota(I32, (c, c), 1)
    lower = (col <= row).astype(BF16)
    carry = jnp.zeros((1, LANES), F32)
    for ci in range(seq // c):
        x = fl_ref[pl.ds(ci * c, c), :] + fb_ref[...]
        lf = jnp.minimum(x, 0.0) - jnp.log(1.0 + jnp.exp(-jnp.abs(x)))
        hi, mid, lo = _split3(lf)
        cs = (jnp.dot(lower, hi, preferred_element_type=F32)
              + jnp.dot(lower, mid, preferred_element_type=F32)
              + jnp.dot(lower, lo, preferred_element_type=F32)) + carry
        o_ref[0, :, pl.ds(ci * c, c)] = cs.T[:FOX_HEADS, :]
        carry = cs[c - 1:c, :]


def _fcum(flog, fb_pad, batch, seq):
    return pl.pallas_call(
        functools.partial(_fcum_kernel, seq=seq),
        grid=(batch,),
        in_specs=[
            pl.BlockSpec((seq, LANES), lambda b: (b, 0)),
            pl.BlockSpec((1, LANES), lambda b: (0, 0)),
        ],
        out_specs=pl.BlockSpec((1, FOX_HEADS, seq), lambda b: (b, 0, 0)),
        out_shape=jax.ShapeDtypeStruct((batch, FOX_HEADS, seq), F32),
        compiler_params=_cparams(("arbitrary",)),
        name="fox_decay_cumsum",
    )(flog, fb_pad)


def _attn_kernel(*refs, mode, t, seq, lam_init):
    if mode == "diff":
        q_ref, k_ref, v_ref, lam_ref, g_ref, o_ref = refs
        f_ref = None
        lv = lam_ref[...]
        lam = (jnp.exp(jnp.sum(lv[0:1, :] * lv[1:2, :])) - jnp.exp(jnp.sum(lv[2:3, :] * lv[3:4, :]))
               + lam_init)
    else:
        q_ref, k_ref, v_ref, f_ref, o_ref = refs
    lane = lax.broadcasted_iota(I32, (t, LANES), 1)
    row = lax.broadcasted_iota(I32, (t, t), 0)
    col = lax.broadcasted_iota(I32, (t, t), 1)
    nt = (((1,), (1,)), ((), ()))

    for qi in range(seq // t):
        lo = qi * t
        q = q_ref[lo:lo + t, :] * jnp.asarray(HEAD_DIM ** -0.5, BF16)
        zero = jnp.zeros_like(q)
        outs = []
        for idx in range(2):
            qm = jnp.where(lane < HEAD_DIM, q, zero) if idx == 0 else jnp.where(lane >= HEAD_DIM, q, zero)
            sd = lax.dot_general(qm, k_ref[lo:lo + t, :], nt, preferred_element_type=F32)
            if f_ref is not None:
                sd = sd - f_ref[0, idx:idx + 1, lo:lo + t]
            sd = jnp.where(col <= row, sd, NEG_INF)
            m = jnp.max(sd, axis=-1, keepdims=True)
            if qi > 0:
                sm = lax.dot_general(qm, k_ref[0:lo, :], nt, preferred_element_type=F32)
                if f_ref is not None:
                    sm = sm - f_ref[0, idx:idx + 1, 0:lo]
                m = jnp.maximum(m, jnp.max(sm, axis=-1, keepdims=True))
            pd = jnp.exp(sd - m)
            l = jnp.sum(pd, axis=-1, keepdims=True)
            acc = jnp.dot(pd.astype(BF16), v_ref[lo:lo + t, :], preferred_element_type=F32)
            if qi > 0:
                pm = jnp.exp(sm - m)
                l = l + jnp.sum(pm, axis=-1, keepdims=True)
                acc = acc + jnp.dot(pm.astype(BF16), v_ref[0:lo, :], preferred_element_type=F32)
            outs.append(acc / l)
        if mode == "diff":
            o = outs[0] - lam * outs[1]
            r = lax.rsqrt(jnp.mean(o * o, axis=-1, keepdims=True) + NORM_EPS)
            o = o * r * g_ref[...] * (1.0 - lam_init)
        else:
            o = jnp.where(lane < HEAD_DIM, outs[0], outs[1])
        o_ref[lo:lo + t, :] = o.astype(BF16)


def _attention(proj, mode, col_q, col_k, col_v, batch, seq, extra, lam_init=0.0):
    n = proj.shape[0]
    t = min(ATTN_TILE, seq)
    n_blk = BRANCH_WIDTH // LANES
    in_specs = [
        pl.BlockSpec((seq, LANES), lambda b, h: (b, col_q + h)),
        pl.BlockSpec((seq, LANES), lambda b, h: (b, col_k + h)),
        pl.BlockSpec((seq, LANES), lambda b, h: (b, col_v + h)),
    ]
    if mode == "diff":
        lamv, g = extra
        in_specs += [pl.BlockSpec((4, HEAD_DIM), lambda b, h: (0, 0)),
                     pl.BlockSpec((1, LANES), lambda b, h: (0, 0))]
        args = (proj, proj, proj, lamv, g)
    else:
        (frow,) = extra
        in_specs += [pl.BlockSpec((1, 2, seq), lambda b, h: (b * n_blk + h, 0, 0))]
        args = (proj, proj, proj, frow)
    return pl.pallas_call(
        functools.partial(_attn_kernel, mode=mode, t=t, seq=seq, lam_init=lam_init),
        grid=(batch, n_blk),
        in_specs=in_specs,
        out_specs=pl.BlockSpec((seq, LANES), lambda b, h: (b, h)),
        out_shape=jax.ShapeDtypeStruct((n, BRANCH_WIDTH), BF16),
        compiler_params=_cparams(("arbitrary", "arbitrary")),
        name="attn_" + mode,
    )(*args)


def _ret_kernel(q_ref, k_ref, v_ref, g_ref, cos_ref, sin_ref, dm_ref, qd_ref, kd_ref, cd_ref,
                o_ref, *, seq, c):
    st = jnp.zeros((RET_HEAD_DIM, RET_HEAD_DIM), F32)
    half = RET_HEAD_DIM // 2

    for ci in range(seq // c):
        sl = pl.ds(ci * c, c)
        cos = cos_ref[sl, :]
        sin = sin_ref[sl, :]
        q = q_ref[sl, :].astype(F32)
        k = k_ref[sl, :].astype(F32)
        qr = q * cos + pltpu.roll(q, half, 1) * sin
        kr = (k * cos + pltpu.roll(k, half, 1) * sin) * (RET_HEAD_DIM ** -0.5)
        v = v_ref[sl, :]
        sc = lax.dot_general(qr.astype(BF16), kr.astype(BF16), (((1,), (1,)), ((), ())),
                             preferred_element_type=F32) * dm_ref[0]
        inner = jnp.dot(sc.astype(BF16), v, preferred_element_type=F32)
        cross = jnp.dot((qr * qd_ref[0]).astype(BF16), st.astype(BF16), preferred_element_type=F32)
        o = inner + cross
        kt = (kr * kd_ref[0]).T.astype(BF16)
        st = st * cd_ref[0] + jnp.dot(kt, v, preferred_element_type=F32)
        mu = jnp.mean(o, axis=-1, keepdims=True)
        oc = o - mu
        var = jnp.mean(oc * oc, axis=-1, keepdims=True)
        on = oc * lax.rsqrt(var + NORM_EPS)
        g = g_ref[sl, :].astype(F32)
        o_ref[sl, :] = (on * (g * _sigmoid(g))).astype(BF16)


def _retention(proj, tables, batch, seq):
    n = proj.shape[0]
    c = min(RET_CHUNK, seq)
    cos, sin, dm, qd, kd, cd = tables
    col = BRANCH_WIDTH * 3 // LANES
    step = BRANCH_WIDTH // LANES

    def pspec(k):
        return pl.BlockSpec((seq, LANES), lambda b, h: (b, col + k * step + h))

    return pl.pallas_call(
        functools.partial(_ret_kernel, seq=seq, c=c),
        grid=(batch, RET_HEADS),
        in_specs=[
            pspec(0), pspec(1), pspec(2), pspec(3),
            pl.BlockSpec((seq, LANES), lambda b, h: (0, 0)),
            pl.BlockSpec((seq, LANES), lambda b, h: (0, 0)),
            pl.BlockSpec((1, c, c), lambda b, h: (h, 0, 0)),
            pl.BlockSpec((1, c, LANES), lambda b, h: (h, 0, 0)),
            pl.BlockSpec((1, c, LANES), lambda b, h: (h, 0, 0)),
            pl.BlockSpec((1, LANES, LANES), lambda b, h: (h, 0, 0)),
        ],
        out_specs=pl.BlockSpec((seq, LANES), lambda b, h: (b, h)),
        out_shape=jax.ShapeDtypeStruct((n, BRANCH_WIDTH), BF16),
        compiler_params=_cparams(("arbitrary", "arbitrary")),
        name="retention",
    )(proj, proj, proj, proj, cos, sin, dm, qd, kd, cd)


def _retention_tables(seq):
    c = min(RET_CHUNK, seq)
    d = RET_HEAD_DIM
    inv = ROPE_BASE ** (-jnp.arange(0, d, 2, dtype=F32) / d)
    ang = jnp.arange(seq).astype(F32)[:, None] * inv[None, :]
    cos = jnp.cos(ang)
    sin = jnp.sin(ang)
    cos_full = jnp.concatenate([cos, cos], axis=-1)
    sin_signed = jnp.concatenate([-sin, sin], axis=-1)
    log_gamma = jnp.log1p(-jnp.exp2(-RET_GAMMA_EXP0 - jnp.arange(RET_HEADS, dtype=F32)))
    idx = jnp.arange(c, dtype=F32)
    dist = idx[:, None] - idx[None, :]
    dm = jnp.where(dist >= 0, jnp.exp(log_gamma[:, None, None] * jnp.maximum(dist, 0.0)), 0.0)
    qd = jnp.exp(log_gamma[:, None] * (idx + 1.0))
    kd = jnp.exp(log_gamma[:, None] * (c - 1.0 - idx))
    cd = jnp.exp(log_gamma * c)
    qd = jnp.broadcast_to(qd[:, :, None], (RET_HEADS, c, LANES))
    kd = jnp.broadcast_to(kd[:, :, None], (RET_HEADS, c, LANES))
    cd = jnp.broadcast_to(cd[:, None, None], (RET_HEADS, LANES, LANES))
    return cos_full, sin_signed, dm, qd, kd, cd


def _merge_kernel(od_ref, or_ref, of_ref, gd_ref, gr_ref, gf_ref, gb_ref, wb_ref, wo_ref, x_ref,
                  g1_ref, n2_ref, sh_ref, sc_ref, rw_ref, rb_ref,
                  xo_ref, hp_ref, ti_ref, tw_ref, cnt_ref, cnt_scr, *, tm):
    merged = None
    for i, (o_ref, gate_ref) in enumerate(((od_ref, gd_ref), (or_ref, gr_ref), (of_ref, gf_ref))):
        gate = _sigmoid(gate_ref[...].astype(F32) + gb_ref[i])
        term = gate * jnp.dot(o_ref[...], wb_ref[i], preferred_element_type=F32)
        merged = term if merged is None else merged + term
    mix = jnp.dot(merged.astype(BF16), wo_ref[...], preferred_element_type=F32)
    xn = x_ref[...] + g1_ref[0] * mix
    xo_ref[...] = xn

    r = lax.rsqrt(jnp.mean(xn * xn, axis=-1, keepdims=True) + NORM_EPS)
    h = xn * r * n2_ref[...]
    h = h * (1.0 + sc_ref[0]) + sh_ref[0]
    hp = _pack_pairs(h)
    hp_ref[0] = hp[:, :QUART]
    hp_ref[1] = hp[:, QUART:]

    hh, hm, _ = _split3(h)
    wh = rw_ref[0]
    wm = rw_ref[1]
    logits = (jnp.dot(hh, wh, preferred_element_type=F32)
              + jnp.dot(hh, wm, preferred_element_type=F32)
              + jnp.dot(hm, wh, preferred_element_type=F32)) + rb_ref[...]

    lane = lax.broadcasted_iota(I32, (tm, LANES), 1)
    vals, idxs = [], []
    for _ in range(TOP_K):
        m = jnp.max(logits, axis=-1, keepdims=True)
        idx = jnp.min(jnp.where(logits == m, lane, LANES), axis=-1, keepdims=True)
        vals.append(m)
        idxs.append(idx)
        logits = jnp.where(lane == idx, -3.0e38, logits)
    es = [jnp.exp(v - vals[0]) for v in vals]
    den = es[0] + es[1] + es[2] + es[3]

    @pl.when(pl.program_id(0) == 0)
    def _():
        cnt_scr[...] = jnp.zeros(cnt_scr.shape, F32)

    hits = [lane == idxs[k] for k in range(TOP_K)]
    picked = (hits[0] | hits[1] | hits[2] | hits[3])
    r_i = lax.broadcasted_iota(I32, (tm, tm), 0)
    c_i = lax.broadcasted_iota(I32, (tm, tm), 1)
    before = jnp.dot((c_i < r_i).astype(BF16), picked.astype(BF16), preferred_element_type=F32)
    base = before + cnt_scr[...]
    cnt_scr[...] = cnt_scr[...] + jnp.sum(picked.astype(F32), axis=0, keepdims=True)
    cnt_ref[...] = cnt_scr[...].astype(I32)

    ti = jnp.zeros((tm, LANES), I32)
    tw = jnp.zeros((tm, LANES), F32)
    for k in range(TOP_K):
        rank = jnp.sum(jnp.where(hits[k], base, 0.0), axis=-1, keepdims=True).astype(I32)
        ti = jnp.where(lane == k, idxs[k], ti)
        ti = jnp.where(lane == TOP_K + k, rank, ti)
        tw = jnp.where(lane == k, es[k] / den, tw)
    ti_ref[...] = ti
    tw_ref[...] = tw


def _merge(o_diff, o_ret, o_fox, proj, gate_b, w_branch, w_out, x2, g1, n2g, sh2, sc2, rw, rb, seq):
    n, d = x2.shape
    tm = min(ROW_TILE_OUT, seq)
    per_seq = seq // tm
    gcol = GATE_COL0 // d

    def rows(w):
        return pl.BlockSpec((tm, w), lambda i: (i, 0))

    def gspec(k):
        return pl.BlockSpec((tm, d), lambda i: (i, gcol + k))

    def whole(shape):
        return pl.BlockSpec(shape, lambda i: (0,) * len(shape))

    def per_b():
        return pl.BlockSpec((1, 1, d), lambda i: (i // per_seq, 0, 0))

    return pl.pallas_call(
        functools.partial(_merge_kernel, tm=tm),
        grid=(n // tm,),
        in_specs=[
            rows(BRANCH_WIDTH), rows(BRANCH_WIDTH), rows(BRANCH_WIDTH),
            gspec(0), gspec(1), gspec(2),
            whole((N_BRANCHES, 1, d)), whole((N_BRANCHES, BRANCH_WIDTH, d)), whole((d, d)),
            rows(d), per_b(), whole((1, d)), per_b(), per_b(),
            whole((2, d, LANES)), whole((1, LANES)),
        ],
        out_specs=[rows(d), pl.BlockSpec((2, tm, QUART), lambda i: (0, i, 0)), rows(LANES), rows(LANES),
                   whole((1, LANES))],
        out_shape=[
            jax.ShapeDtypeStruct((n, d), F32),
            jax.ShapeDtypeStruct((2, n, QUART), I32),
            jax.ShapeDtypeStruct((n, LANES), I32),
            jax.ShapeDtypeStruct((n, LANES), F32),
            jax.ShapeDtypeStruct((1, LANES), I32),
        ],
        scratch_shapes=[pltpu.VMEM((1, LANES), F32)],
        compiler_params=_cparams(("arbitrary",)),
        name="merge_outproj_router",
    )(o_diff, o_ret, o_fox, proj, proj, proj, gate_b, w_branch, w_out, x2, g1, n2g, sh2, sc2, rw, rb)


def _prep_up_kernel(w_ref, o_ref, t_scr):
    t_scr[...] = w_ref[0].T
    f = t_scr.shape[0] // 2
    o_ref[0, 0:f, :] = t_scr[pl.ds(0, f, stride=2), :].astype(BF16)
    o_ref[0, f:, :] = t_scr[pl.ds(1, f, stride=2), :].astype(BF16)


def _prep_up(w_up):
    e, d, f2 = w_up.shape
    kt = LANES
    return pl.pallas_call(
        _prep_up_kernel,
        grid=(e, d // kt),
        in_specs=[pl.BlockSpec((1, kt, f2), lambda i, j: (i, j, 0))],
        out_specs=pl.BlockSpec((1, f2, kt), lambda i, j: (i, 0, j)),
        out_shape=jax.ShapeDtypeStruct((e, f2, d), BF16),
        scratch_shapes=[pltpu.VMEM((f2, kt), F32)],
        compiler_params=_cparams(("arbitrary", "arbitrary")),
        name="expert_up_weight_prep",
    )(w_up)


def _expert_kernel(te_ref, nu_ref, xs_ref, wu_ref, bu_ref, wd_ref, bd_ref, y_ref):
    nt = (((1,), (1,)), ((), ()))

    @pl.when(pl.program_id(0) < nu_ref[0])
    def _():
        lo, hi = _unpack_pairs(jnp.concatenate([xs_ref[0], xs_ref[1]], axis=1))
        lo = lo.astype(BF16)
        hi = hi.astype(BF16)

        def up_half(r0):
            return (lax.dot_general(lo, wu_ref[0, r0:r0 + D_FF, :HALF], nt, preferred_element_type=F32)
                    + lax.dot_general(hi, wu_ref[0, r0:r0 + D_FF, HALF:], nt, preferred_element_type=F32))

        glu = jnp.minimum(up_half(0) + bu_ref[0, :, :D_FF], SWIGLU_LIMIT)
        lin = jnp.clip(up_half(D_FF) + bu_ref[0, :, D_FF:], -SWIGLU_LIMIT, SWIGLU_LIMIT)
        act = glu * _sigmoid(SWIGLU_ALPHA * glu) * (lin + 1.0)
        down = jnp.dot(act.astype(BF16), wd_ref[0], preferred_element_type=F32) + bd_ref[0]
        yp = _pack_pairs(down)
        y_ref[0] = yp[:, :QUART]
        y_ref[1] = yp[:, QUART:]

    @pl.when(pl.program_id(0) >= nu_ref[0])
    def _():
        y_ref[...] = jnp.zeros(y_ref.shape, I32)


def _experts(tile_e, n_used, xs, w_up, b_up, w_down, b_down):
    r = xs.shape[1]
    tm = EXPERT_TILE
    d = D_MODEL
    grid_spec = pltpu.PrefetchScalarGridSpec(
        num_scalar_prefetch=2,
        grid=(r // tm,),
        in_specs=[
            pl.BlockSpec((2, tm, QUART), lambda i, te, nu: (0, i, 0)),
            pl.BlockSpec((1, 2 * D_FF, d), lambda i, te, nu: (te[i], 0, 0)),
            pl.BlockSpec((1, 1, 2 * D_FF), lambda i, te, nu: (te[i], 0, 0)),
            pl.BlockSpec((1, D_FF, d), lambda i, te, nu: (te[i], 0, 0)),
            pl.BlockSpec((1, 1, d), lambda i, te, nu: (te[i], 0, 0)),
        ],
        out_specs=pl.BlockSpec((2, tm, QUART), lambda i, te, nu: (0, i, 0)),
    )
    return pl.pallas_call(
        _expert_kernel,
        grid_spec=grid_spec,
        out_shape=jax.ShapeDtypeStruct((2, r, QUART), I32),
        compiler_params=_cparams(("arbitrary",)),
        name="expert_mlp",
    )(tile_e, n_used, xs, w_up, b_up, w_down, b_down)


def _combine_kernel(y_ref, tw_ref, x_ref, g2_ref, fg_ref, o_ref, *, final):
    tw = tw_ref[...]
    parts = [None] * 4
    for k in range(TOP_K):
        w = tw[:, k:k + 1]
        for piece in range(2):
            lo, hi = _unpack_pairs(y_ref[piece, k])
            for slot, v in ((piece, lo), (2 + piece, hi)):
                parts[slot] = w * v if parts[slot] is None else parts[slot] + w * v
    ffn = jnp.concatenate(parts, axis=-1)
    xn = x_ref[...] + g2_ref[0] * ffn
    if final:
        r = lax.rsqrt(jnp.mean(xn * xn, axis=-1, keepdims=True) + NORM_EPS)
        xn = xn * r * fg_ref[...]
    o_ref[...] = xn


def _combine(ytok, top_w, x2, g2, final_g, seq, final):
    n, d = x2.shape
    tm = min(ROW_TILE_OUT, seq)
    per_seq = seq // tm
    return pl.pallas_call(
        functools.partial(_combine_kernel, final=final),
        grid=(n // tm,),
        in_specs=[
            pl.BlockSpec((2, TOP_K, tm, QUART), lambda i: (0, 0, i, 0)),
            pl.BlockSpec((tm, LANES), lambda i: (i, 0)),
            pl.BlockSpec((tm, d), lambda i: (i, 0)),
            pl.BlockSpec((1, 1, d), lambda i: (i // per_seq, 0, 0)),
            pl.BlockSpec((1, d), lambda i: (0, 0)),
        ],
        out_specs=pl.BlockSpec((tm, d), lambda i: (i, 0)),
        out_shape=jax.ShapeDtypeStruct((n, d), F32),
        compiler_params=_cparams(("arbitrary",)),
        name="moe_combine",
    )(ytok, top_w, x2, g2, final_g)


def _route(top_idx, counts, n_rows):
    tm = EXPERT_TILE
    e = top_idx[:, :TOP_K]
    rank = top_idx[:, TOP_K:2 * TOP_K]
    sizes = counts[0, :N_EXPERTS]
    padded = ((sizes + tm - 1) // tm) * tm
    pend = jnp.cumsum(padded)
    pstart = pend - padded
    onehot = e[:, :, None] == jnp.arange(N_EXPERTS, dtype=I32)[None, None, :]
    pos = jnp.sum(jnp.where(onehot, pstart[None, None, :], 0), axis=-1) + rank
    tiles = jnp.arange(n_rows // tm, dtype=I32) * tm
    tile_e = jnp.minimum(jnp.sum(tiles[:, None] >= pend[None, :], axis=-1), N_EXPERTS - 1).astype(I32)
    n_used = (pend[-1] // tm).astype(I32).reshape(1)
    return pos.T.astype(I32), tile_e, n_used


def _sc_gather(table, idx):
    m = idx.shape[0]
    width = table.shape[1]
    assert m % (SC_WINDOW * SC_WORKERS) == 0
    mesh = plsc.VectorSubcoreMesh(core_axis_name="c", subcore_axis_name="s")

    @pl.kernel(out_type=jax.ShapeDtypeStruct((m, width), table.dtype), mesh=mesh)
    def gather_rows(t_hbm, i_hbm, o_hbm):
        def body(i_vmem, o_vmem):
            pltpu.sync_copy(t_hbm.at[i_vmem.at[0]], o_vmem)

        pltpu.emit_pipeline(
            body,
            grid=(m // SC_WINDOW,),
            in_specs=[pl.BlockSpec((1, SC_WINDOW), lambda i: (0, i))],
            out_specs=[pl.BlockSpec((SC_WINDOW, width), lambda i: (i, 0))],
            core_axis_name=("c", "s"),
            dimension_semantics=(pltpu.PARALLEL,),
        )(i_hbm, o_hbm)

    return gather_rows(table, idx.reshape(1, m))


def _sc_scatter(rows, idx, n_out):
    t, width = rows.shape
    picks = idx.shape[0]
    assert t % SC_WINDOW == 0
    mesh = plsc.VectorSubcoreMesh(core_axis_name="c", subcore_axis_name="s")

    @pl.kernel(out_type=jax.ShapeDtypeStruct((n_out, width), rows.dtype), mesh=mesh)
    def scatter_rows(x_hbm, i_hbm, o_hbm):
        def body(x_vmem, i_vmem):
            for k in range(picks):
                pltpu.sync_copy(x_vmem, o_hbm.at[i_vmem.at[k]])

        pltpu.emit_pipeline(
            body,
            grid=(t // SC_WINDOW,),
            in_specs=[pl.BlockSpec((SC_WINDOW, width), lambda i: (i, 0)),
                      pl.BlockSpec((picks, SC_WINDOW), lambda i: (0, i))],
            out_specs=[],
            core_axis_name=("c", "s"),
            dimension_semantics=(pltpu.PARALLEL,),
        )(x_hbm, i_hbm)

    return scatter_rows(rows, idx)


def _dispatch(h2p, pos_t, n_rows):
    n = h2p.shape[1]
    idx = jnp.concatenate([pos_t, pos_t + n_rows], axis=1)
    return _sc_scatter(h2p.reshape(2 * n, QUART), idx, 2 * n_rows).reshape(2, n_rows, QUART)


def _collect(y, pos_t):
    n_rows = y.shape[1]
    flat = pos_t.reshape(-1)
    out = _sc_gather(y.reshape(2 * n_rows, QUART), jnp.concatenate([flat, flat + n_rows]))
    return out.reshape(2, TOP_K, pos_t.shape[1], QUART)


def kernel(x, c, norm1_g, norm2_g, ada_w, ada_b, w_in, gate_b, fox_fb, lam_q1, lam_k1, lam_q2, lam_k2,
           diff_subln_g, w_branch, w_out, router_w, router_b, exp_w_up, exp_b_up, exp_w_down,
           exp_b_down, final_g):
    batch, seq, d = x.shape
    depth = ada_w.shape[0]
    n = batch * seq
    x2 = x.reshape(n, d)

    mod = _ada_mod(c, ada_w, ada_b)
    tables = _retention_tables(seq)
    n_rows = n * TOP_K + N_EXPERTS * EXPERT_TILE
    blk = BRANCH_WIDTH // LANES

    for l in range(depth):
        lam_init = 0.8 - 0.6 * math.exp(-0.3 * l)
        sh1, sc1, g1, sh2, sc2, g2 = [m.reshape(batch, 1, d) for m in jnp.split(mod[l], 6, axis=-1)]

        w_main, w_f = _prep_win(w_in[l])
        proj, flog = _inproj(x2, norm1_g[l].reshape(1, d), sh1, sc1, w_main, w_f, seq)

        fb_pad = jnp.pad(fox_fb[l], (0, LANES - FOX_HEADS)).reshape(1, LANES)
        frow = _fcum(flog, fb_pad, batch, seq).reshape(batch * blk, 2, seq)

        lamv = jnp.stack([lam_q1[l], lam_k1[l], lam_q2[l], lam_k2[l]])
        o_diff = _attention(proj, "diff", 0, blk, 2 * blk, batch, seq,
                            (lamv, diff_subln_g[l].reshape(1, LANES)), lam_init)
        o_ret = _retention(proj, tables, batch, seq)
        o_fox = _attention(proj, "fox", 7 * blk, 8 * blk, 9 * blk, batch, seq, (frow,))

        rw = jnp.pad(router_w[l], ((0, 0), (0, LANES - N_EXPERTS)))
        rw_hi = rw.astype(BF16)
        rw = jnp.stack([rw_hi, (rw - rw_hi.astype(F32)).astype(BF16)])
        rb = jnp.pad(router_b[l], (0, LANES - N_EXPERTS), constant_values=NEG_INF).reshape(1, LANES)
        x2, h2p, top_idx, top_w, counts = _merge(
            o_diff, o_ret, o_fox, proj, gate_b[l].reshape(N_BRANCHES, 1, d), w_branch[l].astype(BF16),
            w_out[l].astype(BF16), x2, g1, norm2_g[l].reshape(1, d), sh2, sc2, rw, rb, seq)

        pos_t, tile_e, n_used = _route(top_idx, counts, n_rows)
        xs = _dispatch(h2p, pos_t, n_rows)

        w_up = _prep_up(exp_w_up[l])
        bu = exp_b_up[l]
        b_up = jnp.concatenate([bu[:, 0::2], bu[:, 1::2]], axis=-1).reshape(N_EXPERTS, 1, 2 * D_FF)
        y = _experts(tile_e, n_used, xs, w_up, b_up, exp_w_down[l].astype(BF16),
                     exp_b_down[l].reshape(N_EXPERTS, 1, d))
        ytok = _collect(y, pos_t)
        x2 = _combine(ytok, top_w, x2, g2, final_g.reshape(1, d), seq, final=(l == depth - 1))

    return x2.reshape(batch, seq, d)
```

```python
import functools
import math

import jax
import jax.numpy as jnp
from jax import lax
from jax.experimental import pallas as pl
from jax.experimental.pallas import tpu as pltpu
from jax.experimental.pallas import tpu_sc as plsc

F32 = jnp.float32
BF16 = jnp.bfloat16
I32 = jnp.int32
U32 = jnp.uint32

D_MODEL = 1024
HEAD_DIM = 64
BRANCH_WIDTH = D_MODEL // 2
N_BRANCHES = 3
FOX_HEADS = BRANCH_WIDTH // HEAD_DIM
RET_HEADS = 4
RET_HEAD_DIM = 128
ROPE_BASE = 10000.0
RET_GAMMA_EXP0 = 5.0
N_EXPERTS = 32
TOP_K = 4
D_FF = D_MODEL
SWIGLU_LIMIT = 7.0
SWIGLU_ALPHA = 1.702
NORM_EPS = 1e-5
NEG_INF = -1e30

LANES = 128
MAIN_COLS = 8192
GATE_COL0 = 10 * BRANCH_WIDTH
HALF = D_MODEL // 2
QUART = HALF // 2
SC_WINDOW = 128
SC_WORKERS = 32
VMEM_LIMIT = 48 * 1024 * 1024

ROW_TILE_IN = 512
COL_TILE_IN = 1024
ATTN_TILE = 256
RET_CHUNK = 256
ROW_TILE_OUT = 256
EXPERT_TILE = 256
CUM_CHUNK = 256


def _cparams(sem):
    return pltpu.CompilerParams(dimension_semantics=sem, vmem_limit_bytes=VMEM_LIMIT)


def _sigmoid(x):
    return 0.5 * jnp.tanh(0.5 * x) + 0.5


def _pack_pairs(v):
    u = lax.bitcast_convert_type(v.astype(BF16).astype(F32), U32)
    half = v.shape[-1] // 2
    w = (u[:, half:] & jnp.uint32(0xFFFF0000)) | (u[:, :half] >> 16)
    return lax.bitcast_convert_type(w, I32)


def _unpack_pairs(w):
    u = lax.bitcast_convert_type(w, U32)
    lo = lax.bitcast_convert_type(u << 16, F32)
    hi = lax.bitcast_convert_type(u & jnp.uint32(0xFFFF0000), F32)
    return lo, hi


def _ada_kernel(c_ref, w_ref, b_ref, o_ref):
    c = c_ref[...]
    ca = c * _sigmoid(c)
    o_ref[0] = jnp.dot(ca, w_ref[0], preferred_element_type=F32) + b_ref[0]


def _ada_mod(c, ada_w, ada_b):
    depth, d, cols = ada_w.shape
    b = c.shape[0]
    tn = 1024
    return pl.pallas_call(
        _ada_kernel,
        grid=(depth, cols // tn),
        in_specs=[
            pl.BlockSpec((b, d), lambda l, j: (0, 0)),
            pl.BlockSpec((1, d, tn), lambda l, j: (l, 0, j)),
            pl.BlockSpec((1, 1, tn), lambda l, j: (l, 0, j)),
        ],
        out_specs=pl.BlockSpec((1, b, tn), lambda l, j: (l, 0, j)),
        out_shape=jax.ShapeDtypeStruct((depth, b, cols), F32),
        compiler_params=_cparams(("arbitrary", "arbitrary")),
        name="ada_mod",
    )(c, ada_w, ada_b.reshape(depth, 1, cols))


def _prep_win_kernel(w_ref, wm_ref, wf_ref):
    x = w_ref[0]
    wm_ref[:, :GATE_COL0] = x[:, :GATE_COL0].astype(BF16)
    wm_ref[:, GATE_COL0:] = x[:, GATE_COL0 + FOX_HEADS:].astype(BF16)
    lane = lax.broadcasted_iota(I32, (x.shape[0], LANES), 1)
    wf_ref[...] = jnp.where(lane < FOX_HEADS, x[:, GATE_COL0:GATE_COL0 + LANES], 0.0).astype(BF16)


def _prep_win(w, layer):
    _, d, cols = w.shape
    rt = 128
    return pl.pallas_call(
        _prep_win_kernel,
        grid=(d // rt,),
        in_specs=[pl.BlockSpec((1, rt, cols), lambda i: (layer, i, 0))],
        out_specs=[pl.BlockSpec((rt, MAIN_COLS), lambda i: (i, 0)), pl.BlockSpec((rt, LANES), lambda i: (i, 0))],
        out_shape=[jax.ShapeDtypeStruct((d, MAIN_COLS), BF16), jax.ShapeDtypeStruct((d, LANES), BF16)],
        compiler_params=_cparams(("arbitrary",)),
        name="inproj_weight_prep",
    )(w)


def _inproj_kernel(x_ref, g_ref, sh_ref, sc_ref, w_ref, wf_ref, proj_ref, fl_ref):
    x = x_ref[...]
    r = lax.rsqrt(jnp.mean(x * x, axis=-1, keepdims=True) + NORM_EPS)
    h = x * r * g_ref[...]
    h = h * (1.0 + sc_ref[0]) + sh_ref[0]
    hb = h.astype(BF16)
    fl_ref[...] = jnp.dot(hb, wf_ref[...], preferred_element_type=F32)
    for j in range(MAIN_COLS // COL_TILE_IN):
        cs = slice(j * COL_TILE_IN, (j + 1) * COL_TILE_IN)
        proj_ref[:, cs] = jnp.dot(hb, w_ref[:, cs], preferred_element_type=F32).astype(BF16)


def _inproj(x2, g, sh, sc, w_main, w_f, seq):
    n, d = x2.shape
    tm = min(ROW_TILE_IN, seq)
    per_seq = seq // tm
    resident = pl.Buffered(1)
    return pl.pallas_call(
        _inproj_kernel,
        grid=(n // tm,),
        in_specs=[
            pl.BlockSpec((tm, d), lambda i: (i, 0)),
            pl.BlockSpec((1, d), lambda i: (0, 0)),
            pl.BlockSpec((1, 1, d), lambda i: (i // per_seq, 0, 0)),
            pl.BlockSpec((1, 1, d), lambda i: (i // per_seq, 0, 0)),
            pl.BlockSpec((d, MAIN_COLS), lambda i: (0, 0), pipeline_mode=resident),
            pl.BlockSpec((d, LANES), lambda i: (0, 0), pipeline_mode=resident),
        ],
        out_specs=[
            pl.BlockSpec((tm, MAIN_COLS), lambda i: (i, 0)),
            pl.BlockSpec((tm, LANES), lambda i: (i, 0)),
        ],
        out_shape=[
            jax.ShapeDtypeStruct((n, MAIN_COLS), BF16),
            jax.ShapeDtypeStruct((n, LANES), F32),
        ],
        compiler_params=_cparams(("arbitrary",)),
        name="inproj",
    )(x2, g, sh, sc, w_main, w_f)


def _split3(v):
    hi = v.astype(BF16)
    r1 = v - hi.astype(F32)
    mid = r1.astype(BF16)
    lo = (r1 - mid.astype(F32)).astype(BF16)
    return hi, mid, lo


def _fcum_kernel(fl_ref, fb_ref, o_ref, *, seq):
    c = min(CUM_CHUNK, seq)
    row = lax.broadcasted_iota(I32, (c, c), 0)
    col = lax.broadcasted_iota(I32, (c, c), 1)
    lower = (col <= row).astype(BF16)
    carry = jnp.zeros((1, LANES), F32)
    for ci in range(seq // c):
        x = fl_ref[pl.ds(ci * c, c), :] + fb_ref[...]
        lf = jnp.minimum(x, 0.0) - jnp.log(1.0 + jnp.exp(-jnp.abs(x)))
        hi, mid, lo = _split3(lf)
        cs = (jnp.dot(lower, hi, preferred_element_type=F32)
              + jnp.dot(lower, mid, preferred_element_type=F32)
              + jnp.dot(lower, lo, preferred_element_type=F32)) + carry
        o_ref[0, :, pl.ds(ci * c, c)] = cs.T[:FOX_HEADS, :]
        carry = cs[c - 1:c, :]


def _fcum(flog, fb_pad, batch, seq):
    return pl.pallas_call(
        functools.partial(_fcum_kernel, seq=seq),
        grid=(batch,),
        in_specs=[
            pl.BlockSpec((seq, LANES), lambda b: (b, 0)),
            pl.BlockSpec((1, LANES), lambda b: (0, 0)),
        ],
        out_specs=pl.BlockSpec((1, FOX_HEADS, seq), lambda b: (b, 0, 0)),
        out_shape=jax.ShapeDtypeStruct((batch, FOX_HEADS, seq), F32),
        compiler_params=_cparams(("arbitrary",)),
        name="fox_decay_cumsum",
    )(flog, fb_pad)


def _attn_kernel(*refs, mode, t, seq, lam_init):
    if mode == "diff":
        q_ref, k_ref, v_ref, lam_ref, g_ref, o_ref = refs
        f_ref = None
        lv = lam_ref[...]
        lam = (jnp.exp(jnp.sum(lv[0:1, :] * lv[1:2, :])) - jnp.exp(jnp.sum(lv[2:3, :] * lv[3:4, :]))
               + lam_init)
    else:
        q_ref, k_ref, v_ref, f_ref, o_ref = refs
    lane = lax.broadcasted_iota(I32, (t, LANES), 1)
    row = lax.broadcasted_iota(I32, (t, t), 0)
    col = lax.broadcasted_iota(I32, (t, t), 1)
    nt = (((1,), (1,)), ((), ()))

    for qi in range(seq // t):
        lo = qi * t
        q = q_ref[lo:lo + t, :] * jnp.asarray(HEAD_DIM ** -0.5, BF16)
        zero = jnp.zeros_like(q)
        outs = []
        for idx in range(2):
            qm = jnp.where(lane < HEAD_DIM, q, zero) if idx == 0 else jnp.where(lane >= HEAD_DIM, q, zero)
            sd = lax.dot_general(qm, k_ref[lo:lo + t, :], nt, preferred_element_type=F32)
            if f_ref is not None:
                sd = sd - f_ref[0, idx:idx + 1, lo:lo + t]
            sd = jnp.where(col <= row, sd, NEG_INF)
            m = jnp.max(sd, axis=-1, keepdims=True)
            if qi > 0:
                sm = lax.dot_general(qm, k_ref[0:lo, :], nt, preferred_element_type=F32)
                if f_ref is not None:
                    sm = sm - f_ref[0, idx:idx + 1, 0:lo]
                m = jnp.maximum(m, jnp.max(sm, axis=-1, keepdims=True))
            pd = jnp.exp(sd - m)
            l = jnp.sum(pd, axis=-1, keepdims=True)
            acc = jnp.dot(pd.astype(BF16), v_ref[lo:lo + t, :], preferred_element_type=F32)
            if qi > 0:
                pm = jnp.exp(sm - m)
                l = l + jnp.sum(pm, axis=-1, keepdims=True)
                acc = acc + jnp.dot(pm.astype(BF16), v_ref[0:lo, :], preferred_element_type=F32)
            outs.append(acc / l)
        if mode == "diff":
            o = outs[0] - lam * outs[1]
            r = lax.rsqrt(jnp.mean(o * o, axis=-1, keepdims=True) + NORM_EPS)
            o = o * r * g_ref[...] * (1.0 - lam_init)
        else:
            o = jnp.where(lane < HEAD_DIM, outs[0], outs[1])
        o_ref[lo:lo + t, :] = o.astype(BF16)


def _attention(proj, mode, col_q, col_k, col_v, batch, seq, extra, lam_init=0.0):
    n = proj.shape[0]
    t = min(ATTN_TILE, seq)
    n_blk = BRANCH_WIDTH // LANES
    in_specs = [
        pl.BlockSpec((seq, LANES), lambda b, h: (b, col_q + h)),
        pl.BlockSpec((seq, LANES), lambda b, h: (b, col_k + h)),
        pl.BlockSpec((seq, LANES), lambda b, h: (b, col_v + h)),
    ]
    if mode == "diff":
        lamv, g = extra
        in_specs += [pl.BlockSpec((4, HEAD_DIM), lambda b, h: (0, 0)),
                     pl.BlockSpec((1, LANES), lambda b, h: (0, 0))]
        args = (proj, proj, proj, lamv, g)
    else:
        (frow,) = extra
        in_specs += [pl.BlockSpec((1, 2, seq), lambda b, h: (b * n_blk + h, 0, 0))]
        args = (proj, proj, proj, frow)
    return pl.pallas_call(
        functools.partial(_attn_kernel, mode=mode, t=t, seq=seq, lam_init=lam_init),
        grid=(batch, n_blk),
        in_specs=in_specs,
        out_specs=pl.BlockSpec((seq, LANES), lambda b, h: (b, h)),
        out_shape=jax.ShapeDtypeStruct((n, BRANCH_WIDTH), BF16),
        compiler_params=_cparams(("arbitrary", "arbitrary")),
        name="attn_" + mode,
    )(*args)


def _ret_kernel(q_ref, k_ref, v_ref, g_ref, cos_ref, sin_ref, dm_ref, qd_ref, kd_ref, cd_ref,
                o_ref, *, seq, c):
    st = jnp.zeros((RET_HEAD_DIM, RET_HEAD_DIM), F32)
    half = RET_HEAD_DIM // 2

    for ci in range(seq // c):
        sl = pl.ds(ci * c, c)
        cos = cos_ref[sl, :]
        sin = sin_ref[sl, :]
        q = q_ref[sl, :].astype(F32)
        k = k_ref[sl, :].astype(F32)
        qr = q * cos + pltpu.roll(q, half, 1) * sin
        kr = (k * cos + pltpu.roll(k, half, 1) * sin) * (RET_HEAD_DIM ** -0.5)
        v = v_ref[sl, :]
        sc = lax.dot_general(qr.astype(BF16), kr.astype(BF16), (((1,), (1,)), ((), ())),
                             preferred_element_type=F32) * dm_ref[0]
        inner = jnp.dot(sc.astype(BF16), v, preferred_element_type=F32)
        cross = jnp.dot((qr * qd_ref[0]).astype(BF16), st.astype(BF16), preferred_element_type=F32)
        o = inner + cross
        kt = (kr * kd_ref[0]).T.astype(BF16)
        st = st * cd_ref[0] + jnp.dot(kt, v, preferred_element_type=F32)
        mu = jnp.mean(o, axis=-1, keepdims=True)
        oc = o - mu
        var = jnp.mean(oc * oc, axis=-1, keepdims=True)
        on = oc * lax.rsqrt(var + NORM_EPS)
        g = g_ref[sl, :].astype(F32)
        o_ref[sl, :] = (on * (g * _sigmoid(g))).astype(BF16)


def _retention(proj, tables, batch, seq):
    n = proj.shape[0]
    c = min(RET_CHUNK, seq)
    cos, sin, dm, qd, kd, cd = tables
    col = BRANCH_WIDTH * 3 // LANES
    step = BRANCH_WIDTH // LANES

    def pspec(k):
        return pl.BlockSpec((seq, LANES), lambda b, h: (b, col + k * step + h))

    return pl.pallas_call(
        functools.partial(_ret_kernel, seq=seq, c=c),
        grid=(batch, RET_HEADS),
        in_specs=[
            pspec(0), pspec(1), pspec(2), pspec(3),
            pl.BlockSpec((seq, LANES), lambda b, h: (0, 0)),
            pl.BlockSpec((seq, LANES), lambda b, h: (0, 0)),
            pl.BlockSpec((1, c, c), lambda b, h: (h, 0, 0)),
            pl.BlockSpec((1, c, LANES), lambda b, h: (h, 0, 0)),
            pl.BlockSpec((1, c, LANES), lambda b, h: (h, 0, 0)),
            pl.BlockSpec((1, LANES, LANES), lambda b, h: (h, 0, 0)),
        ],
        out_specs=pl.BlockSpec((seq, LANES), lambda b, h: (b, h)),
        out_shape=jax.ShapeDtypeStruct((n, BRANCH_WIDTH), BF16),
        compiler_params=_cparams(("arbitrary", "arbitrary")),
        name="retention",
    )(proj, proj, proj, proj, cos, sin, dm, qd, kd, cd)


def _retention_tables(seq):
    c = min(RET_CHUNK, seq)
    d = RET_HEAD_DIM
    inv = ROPE_BASE ** (-jnp.arange(0, d, 2, dtype=F32) / d)
    ang = jnp.arange(seq).astype(F32)[:, None] * inv[None, :]
    cos = jnp.cos(ang)
    sin = jnp.sin(ang)
    cos_full = jnp.concatenate([cos, cos], axis=-1)
    sin_signed = jnp.concatenate([-sin, sin], axis=-1)
    log_gamma = jnp.log1p(-jnp.exp2(-RET_GAMMA_EXP0 - jnp.arange(RET_HEADS, dtype=F32)))
    idx = jnp.arange(c, dtype=F32)
    dist = idx[:, None] - idx[None, :]
    dm = jnp.where(dist >= 0, jnp.exp(log_gamma[:, None, None] * jnp.maximum(dist, 0.0)), 0.0)
    qd = jnp.exp(log_gamma[:, None] * (idx + 1.0))
    kd = jnp.exp(log_gamma[:, None] * (c - 1.0 - idx))
    cd = jnp.exp(log_gamma * c)
    qd = jnp.broadcast_to(qd[:, :, None], (RET_HEADS, c, LANES))
    kd = jnp.broadcast_to(kd[:, :, None], (RET_HEADS, c, LANES))
    cd = jnp.broadcast_to(cd[:, None, None], (RET_HEADS, LANES, LANES))
    return cos_full, sin_signed, dm, qd, kd, cd


def _merge_kernel(od_ref, or_ref, of_ref, gd_ref, gr_ref, gf_ref, gb_ref, wb_ref, wo_ref, x_ref,
                  g1_ref, n2_ref, sh_ref, sc_ref, rw_ref, rb_ref,
                  xo_ref, hp_ref, ti_ref, tw_ref, cnt_ref, cnt_scr, *, tm):
    merged = None
    for i, (o_ref, gate_ref) in enumerate(((od_ref, gd_ref), (or_ref, gr_ref), (of_ref, gf_ref))):
        gate = _sigmoid(gate_ref[...] + gb_ref[i].astype(BF16))
        term = gate.astype(F32) * jnp.dot(o_ref[...], wb_ref[i], preferred_element_type=F32)
        merged = term if merged is None else merged + term
    mix = jnp.dot(merged.astype(BF16), wo_ref[...], preferred_element_type=F32)
    xn = x_ref[...] + g1_ref[0] * mix
    xo_ref[...] = xn

    r = lax.rsqrt(jnp.mean(xn * xn, axis=-1, keepdims=True) + NORM_EPS)
    h = xn * r * n2_ref[...]
    h = h * (1.0 + sc_ref[0]) + sh_ref[0]
    hp = _pack_pairs(h)
    hp_ref[0] = hp[:, :QUART]
    hp_ref[1] = hp[:, QUART:]

    hh, hm, _ = _split3(h)
    wh = rw_ref[0]
    wm = rw_ref[1]
    logits = (jnp.dot(hh, wh, preferred_element_type=F32)
              + jnp.dot(hh, wm, preferred_element_type=F32)
              + jnp.dot(hm, wh, preferred_element_type=F32)) + rb_ref[...]

    lane = lax.broadcasted_iota(I32, (tm, LANES), 1)
    vals, idxs = [], []
    for _ in range(TOP_K):
        m = jnp.max(logits, axis=-1, keepdims=True)
        idx = jnp.min(jnp.where(logits == m, lane, LANES), axis=-1, keepdims=True)
        vals.append(m)
        idxs.append(idx)
        logits = jnp.where(lane == idx, -3.0e38, logits)
    es = [jnp.exp(v - vals[0]) for v in vals]
    den = es[0] + es[1] + es[2] + es[3]

    @pl.when(pl.program_id(0) == 0)
    def _():
        cnt_scr[...] = jnp.zeros(cnt_scr.shape, F32)

    hits = [lane == idxs[k] for k in range(TOP_K)]
    picked = (hits[0] | hits[1] | hits[2] | hits[3])
    r_i = lax.broadcasted_iota(I32, (tm, tm), 0)
    c_i = lax.broadcasted_iota(I32, (tm, tm), 1)
    before = jnp.dot((c_i < r_i).astype(BF16), picked.astype(BF16), preferred_element_type=F32)
    base = before + cnt_scr[...]
    cnt_scr[...] = cnt_scr[...] + jnp.sum(picked.astype(F32), axis=0, keepdims=True)
    cnt_ref[...] = cnt_scr[...].astype(I32)

    ti = jnp.zeros((tm, LANES), I32)
    tw = jnp.zeros((tm, LANES), F32)
    for k in range(TOP_K):
        rank = jnp.sum(jnp.where(hits[k], base, 0.0), axis=-1, keepdims=True).astype(I32)
        ti = jnp.where(lane == k, idxs[k], ti)
        ti = jnp.where(lane == TOP_K + k, rank, ti)
        tw = jnp.where(lane == k, es[k] / den, tw)
    ti_ref[...] = ti
    tw_ref[...] = tw


def _merge(o_diff, o_ret, o_fox, proj, gate_b, w_branch, w_out, x2, g1, n2g, sh2, sc2, rw, rb, seq):
    n, d = x2.shape
    tm = min(ROW_TILE_OUT, seq)
    per_seq = seq // tm
    gcol = GATE_COL0 // d

    def rows(w):
        return pl.BlockSpec((tm, w), lambda i: (i, 0))

    def gspec(k):
        return pl.BlockSpec((tm, d), lambda i: (i, gcol + k))

    def whole(shape):
        return pl.BlockSpec(shape, lambda i: (0,) * len(shape))

    def per_b():
        return pl.BlockSpec((1, 1, d), lambda i: (i // per_seq, 0, 0))

    return pl.pallas_call(
        functools.partial(_merge_kernel, tm=tm),
        grid=(n // tm,),
        in_specs=[
            rows(BRANCH_WIDTH), rows(BRANCH_WIDTH), rows(BRANCH_WIDTH),
            gspec(0), gspec(1), gspec(2),
            whole((N_BRANCHES, 1, d)), whole((N_BRANCHES, BRANCH_WIDTH, d)), whole((d, d)),
            rows(d), per_b(), whole((1, d)), per_b(), per_b(),
            whole((2, d, LANES)), whole((1, LANES)),
        ],
        out_specs=[rows(d), pl.BlockSpec((2, tm, QUART), lambda i: (0, i, 0)), rows(LANES), rows(LANES),
                   whole((1, LANES))],
        out_shape=[
            jax.ShapeDtypeStruct((n, d), F32),
            jax.ShapeDtypeStruct((2, n, QUART), I32),
            jax.ShapeDtypeStruct((n, LANES), I32),
            jax.ShapeDtypeStruct((n, LANES), F32),
            jax.ShapeDtypeStruct((1, LANES), I32),
        ],
        scratch_shapes=[pltpu.VMEM((1, LANES), F32)],
        compiler_params=_cparams(("arbitrary",)),
        name="merge_outproj_router",
    )(o_diff, o_ret, o_fox, proj, proj, proj, gate_b, w_branch, w_out, x2, g1, n2g, sh2, sc2, rw, rb)


def _prep_up_kernel(w_ref, o_ref, t_scr):
    t_scr[...] = w_ref[0, 0].T
    f = t_scr.shape[0] // 2
    o_ref[0, 0:f, :] = t_scr[pl.ds(0, f, stride=2), :].astype(BF16)
    o_ref[0, f:, :] = t_scr[pl.ds(1, f, stride=2), :].astype(BF16)


def _prep_up(w_up, layer):
    _, e, d, f2 = w_up.shape
    kt = LANES
    return pl.pallas_call(
        _prep_up_kernel,
        grid=(e, d // kt),
        in_specs=[pl.BlockSpec((1, 1, kt, f2), lambda i, j: (layer, i, j, 0))],
        out_specs=pl.BlockSpec((1, f2, kt), lambda i, j: (i, 0, j)),
        out_shape=jax.ShapeDtypeStruct((e, f2, d), BF16),
        scratch_shapes=[pltpu.VMEM((f2, kt), F32)],
        compiler_params=_cparams(("arbitrary", "arbitrary")),
        name="expert_up_weight_prep",
    )(w_up)


def _cast_kernel(w_ref, o_ref):
    o_ref[0] = w_ref[0, 0].astype(BF16)


def _prep_down(w_down, layer):
    _, e, f, d = w_down.shape
    return pl.pallas_call(
        _cast_kernel,
        grid=(e,),
        in_specs=[pl.BlockSpec((1, 1, f, d), lambda i: (layer, i, 0, 0))],
        out_specs=pl.BlockSpec((1, f, d), lambda i: (i, 0, 0)),
        out_shape=jax.ShapeDtypeStruct((e, f, d), BF16),
        compiler_params=_cparams(("arbitrary",)),
        name="expert_down_weight_cast",
    )(w_down)


def _expert_kernel(te_ref, nu_ref, xs_ref, wu_ref, bu_ref, wd_ref, bd_ref, y_ref):
    nt = (((1,), (1,)), ((), ()))

    @pl.when(pl.program_id(0) < nu_ref[0])
    def _():
        lo, hi = _unpack_pairs(jnp.concatenate([xs_ref[0], xs_ref[1]], axis=1))
        lo = lo.astype(BF16)
        hi = hi.astype(BF16)

        def up_half(r0):
            return (lax.dot_general(lo, wu_ref[0, r0:r0 + D_FF, :HALF], nt, preferred_element_type=F32)
                    + lax.dot_general(hi, wu_ref[0, r0:r0 + D_FF, HALF:], nt, preferred_element_type=F32))

        glu = jnp.minimum(up_half(0) + bu_ref[0, :, :D_FF], SWIGLU_LIMIT)
        lin = jnp.clip(up_half(D_FF) + bu_ref[0, :, D_FF:], -SWIGLU_LIMIT, SWIGLU_LIMIT)
        act = glu * _sigmoid(SWIGLU_ALPHA * glu) * (lin + 1.0)
        down = jnp.dot(act.astype(BF16), wd_ref[0], preferred_element_type=F32) + bd_ref[0]
        yp = _pack_pairs(down)
        y_ref[0] = yp[:, :QUART]
        y_ref[1] = yp[:, QUART:]

    @pl.when(pl.program_id(0) >= nu_ref[0])
    def _():
        y_ref[...] = jnp.zeros(y_ref.shape, I32)


def _experts(tile_e, n_used, xs, w_up, b_up, w_down, b_down):
    r = xs.shape[1]
    tm = EXPERT_TILE
    d = D_MODEL
    grid_spec = pltpu.PrefetchScalarGridSpec(
        num_scalar_prefetch=2,
        grid=(r // tm,),
        in_specs=[
            pl.BlockSpec((2, tm, QUART), lambda i, te, nu: (0, i, 0)),
            pl.BlockSpec((1, 2 * D_FF, d), lambda i, te, nu: (te[i], 0, 0)),
            pl.BlockSpec((1, 1, 2 * D_FF), lambda i, te, nu: (te[i], 0, 0)),
            pl.BlockSpec((1, D_FF, d), lambda i, te, nu: (te[i], 0, 0)),
            pl.BlockSpec((1, 1, d), lambda i, te, nu: (te[i], 0, 0)),
        ],
        out_specs=pl.BlockSpec((2, tm, QUART), lambda i, te, nu: (0, i, 0)),
    )
    return pl.pallas_call(
        _expert_kernel,
        grid_spec=grid_spec,
        out_shape=jax.ShapeDtypeStruct((2, r, QUART), I32),
        compiler_params=_cparams(("arbitrary",)),
        name="expert_mlp",
    )(tile_e, n_used, xs, w_up, b_up, w_down, b_down)


def _combine_kernel(y_ref, tw_ref, x_ref, g2_ref, fg_ref, o_ref, *, final):
    tw = tw_ref[...]
    parts = [None] * 4
    for k in range(TOP_K):
        w = tw[:, k:k + 1]
        for piece in range(2):
            lo, hi = _unpack_pairs(y_ref[piece, k])
            for slot, v in ((piece, lo), (2 + piece, hi)):
                parts[slot] = w * v if parts[slot] is None else parts[slot] + w * v
    ffn = jnp.concatenate(parts, axis=-1)
    xn = x_ref[...] + g2_ref[0] * ffn
    if final:
        r = lax.rsqrt(jnp.mean(xn * xn, axis=-1, keepdims=True) + NORM_EPS)
        xn = xn * r * fg_ref[...]
    o_ref[...] = xn


def _combine(ytok, top_w, x2, g2, final_g, seq, final):
    n, d = x2.shape
    tm = min(ROW_TILE_OUT, seq)
    per_seq = seq // tm
    return pl.pallas_call(
        functools.partial(_combine_kernel, final=final),
        grid=(n // tm,),
        in_specs=[
            pl.BlockSpec((2, TOP_K, tm, QUART), lambda i: (0, 0, i, 0)),
            pl.BlockSpec((tm, LANES), lambda i: (i, 0)),
            pl.BlockSpec((tm, d), lambda i: (i, 0)),
            pl.BlockSpec((1, 1, d), lambda i: (i // per_seq, 0, 0)),
            pl.BlockSpec((1, d), lambda i: (0, 0)),
        ],
        out_specs=pl.BlockSpec((tm, d), lambda i: (i, 0)),
        out_shape=jax.ShapeDtypeStruct((n, d), F32),
        compiler_params=_cparams(("arbitrary",)),
        name="moe_combine",
    )(ytok, top_w, x2, g2, final_g)


def _route(top_idx, counts, n_rows):
    tm = EXPERT_TILE
    e = top_idx[:, :TOP_K]
    rank = top_idx[:, TOP_K:2 * TOP_K]
    sizes = counts[0, :N_EXPERTS]
    padded = ((sizes + tm - 1) // tm) * tm
    pend = jnp.cumsum(padded)
    pstart = pend - padded
    onehot = e[:, :, None] == jnp.arange(N_EXPERTS, dtype=I32)[None, None, :]
    pos = jnp.sum(jnp.where(onehot, pstart[None, None, :], 0), axis=-1) + rank
    tiles = jnp.arange(n_rows // tm, dtype=I32) * tm
    tile_e = jnp.minimum(jnp.sum(tiles[:, None] >= pend[None, :], axis=-1), N_EXPERTS - 1).astype(I32)
    n_used = (pend[-1] // tm).astype(I32).reshape(1)
    return pos.T.astype(I32), tile_e, n_used


def _sc_gather(table, idx):
    m = idx.shape[0]
    width = table.shape[1]
    assert m % (SC_WINDOW * SC_WORKERS) == 0
    mesh = plsc.VectorSubcoreMesh(core_axis_name="c", subcore_axis_name="s")

    @pl.kernel(out_type=jax.ShapeDtypeStruct((m, width), table.dtype), mesh=mesh)
    def gather_rows(t_hbm, i_hbm, o_hbm):
        def body(i_vmem, o_vmem):
            pltpu.sync_copy(t_hbm.at[i_vmem.at[0]], o_vmem)

        pltpu.emit_pipeline(
            body,
            grid=(m // SC_WINDOW,),
            in_specs=[pl.BlockSpec((1, SC_WINDOW), lambda i: (0, i))],
            out_specs=[pl.BlockSpec((SC_WINDOW, width), lambda i: (i, 0))],
            core_axis_name=("c", "s"),
            dimension_semantics=(pltpu.PARALLEL,),
        )(i_hbm, o_hbm)

    return gather_rows(table, idx.reshape(1, m))


def _sc_scatter(rows, idx, n_out):
    t, width = rows.shape
    picks = idx.shape[0]
    assert t % SC_WINDOW == 0
    mesh = plsc.VectorSubcoreMesh(core_axis_name="c", subcore_axis_name="s")

    @pl.kernel(out_type=jax.ShapeDtypeStruct((n_out, width), rows.dtype), mesh=mesh)
    def scatter_rows(x_hbm, i_hbm, o_hbm):
        def body(x_vmem, i_vmem):
            for k in range(picks):
                pltpu.sync_copy(x_vmem, o_hbm.at[i_vmem.at[k]])

        pltpu.emit_pipeline(
            body,
            grid=(t // SC_WINDOW,),
            in_specs=[pl.BlockSpec((SC_WINDOW, width), lambda i: (i, 0)),
                      pl.BlockSpec((picks, SC_WINDOW), lambda i: (0, i))],
            out_specs=[],
            core_axis_name=("c", "s"),
            dimension_semantics=(pltpu.PARALLEL,),
        )(x_hbm, i_hbm)

    return scatter_rows(rows, idx)


def _dispatch(h2p, pos_t, n_rows):
    n = h2p.shape[1]
    idx = jnp.concatenate([pos_t, pos_t + n_rows], axis=1)
    return _sc_scatter(h2p.reshape(2 * n, QUART), idx, 2 * n_rows).reshape(2, n_rows, QUART)


def _collect(y, pos_t):
    n_rows = y.shape[1]
    flat = pos_t.reshape(-1)
    out = _sc_gather(y.reshape(2 * n_rows, QUART), jnp.concatenate([flat, flat + n_rows]))
    return out.reshape(2, TOP_K, pos_t.shape[1], QUART)


def kernel(x, c, norm1_g, norm2_g, ada_w, ada_b, w_in, gate_b, fox_fb, lam_q1, lam_k1, lam_q2, lam_k2,
           diff_subln_g, w_branch, w_out, router_w, router_b, exp_w_up, exp_b_up, exp_w_down,
           exp_b_down, final_g):
    batch, seq, d = x.shape
    depth = ada_w.shape[0]
    n = batch * seq
    x2 = x.reshape(n, d)

    mod = _ada_mod(c, ada_w, ada_b)
    tables = _retention_tables(seq)
    n_rows = n * TOP_K + N_EXPERTS * EXPERT_TILE
    blk = BRANCH_WIDTH // LANES

    for l in range(depth):
        lam_init = 0.8 - 0.6 * math.exp(-0.3 * l)
        sh1, sc1, g1, sh2, sc2, g2 = [m.reshape(batch, 1, d) for m in jnp.split(mod[l], 6, axis=-1)]

        w_main, w_f = _prep_win(w_in, l)
        proj, flog = _inproj(x2, norm1_g[l].reshape(1, d), sh1, sc1, w_main, w_f, seq)

        fb_pad = jnp.pad(fox_fb[l], (0, LANES - FOX_HEADS)).reshape(1, LANES)
        frow = _fcum(flog, fb_pad, batch, seq).reshape(batch * blk, 2, seq)

        lamv = jnp.stack([lam_q1[l], lam_k1[l], lam_q2[l], lam_k2[l]])
        o_diff = _attention(proj, "diff", 0, blk, 2 * blk, batch, seq,
                            (lamv, diff_subln_g[l].reshape(1, LANES)), lam_init)
        o_ret = _retention(proj, tables, batch, seq)
        o_fox = _attention(proj, "fox", 7 * blk, 8 * blk, 9 * blk, batch, seq, (frow,))

        rw = jnp.pad(router_w[l], ((0, 0), (0, LANES - N_EXPERTS)))
        rw_hi = rw.astype(BF16)
        rw = jnp.stack([rw_hi, (rw - rw_hi.astype(F32)).astype(BF16)])
        rb = jnp.pad(router_b[l], (0, LANES - N_EXPERTS), constant_values=NEG_INF).reshape(1, LANES)
        x2, h2p, top_idx, top_w, counts = _merge(
            o_diff, o_ret, o_fox, proj, gate_b[l].reshape(N_BRANCHES, 1, d), w_branch[l].astype(BF16),
            w_out[l].astype(BF16), x2, g1, norm2_g[l].reshape(1, d), sh2, sc2, rw, rb, seq)

        pos_t, tile_e, n_used = _route(top_idx, counts, n_rows)
        xs = _dispatch(h2p, pos_t, n_rows)

        w_up = _prep_up(exp_w_up, l)
        bu = exp_b_up[l]
        b_up = jnp.concatenate([bu[:, 0::2], bu[:, 1::2]], axis=-1).reshape(N_EXPERTS, 1, 2 * D_FF)
        y = _experts(tile_e, n_used, xs, w_up, b_up, _prep_down(exp_w_down, l),
                     exp_b_down[l].reshape(N_EXPERTS, 1, d))
        ytok = _collect(y, pos_t)
        x2 = _combine(ytok, top_w, x2, g2, final_g.reshape(1, d), seq, final=(l == depth - 1))

    return x2.reshape(batch, seq, d)
```

```python
import functools
import math

import jax
import jax.numpy as jnp
from jax import lax
from jax.experimental import pallas as pl
from jax.experimental.pallas import tpu as pltpu
from jax.experimental.pallas import tpu_sc as plsc

F32 = jnp.float32
BF16 = jnp.bfloat16
I32 = jnp.int32
U32 = jnp.uint32

D_MODEL = 1024
HEAD_DIM = 64
BRANCH_WIDTH = D_MODEL // 2
N_BRANCHES = 3
FOX_HEADS = BRANCH_WIDTH // HEAD_DIM
RET_HEADS = 4
RET_HEAD_DIM = 128
ROPE_BASE = 10000.0
RET_GAMMA_EXP0 = 5.0
N_EXPERTS = 32
TOP_K = 4
D_FF = D_MODEL
SWIGLU_LIMIT = 7.0
SWIGLU_ALPHA = 1.702
NORM_EPS = 1e-5
NEG_INF = -1e30

LANES = 128
MAIN_COLS = 8192
GATE_COL0 = 10 * BRANCH_WIDTH
HALF = D_MODEL // 2
QUART = HALF // 2
SC_WINDOW = 128
SC_WORKERS = 32
VMEM_LIMIT = 48 * 1024 * 1024

ROW_TILE_IN = 512
COL_TILE_IN = 1024
ATTN_TILE = 256
RET_CHUNK = 256
ROW_TILE_OUT = 256
MERGE_TILE = 512
MERGE_SUB = 256
EXPERT_TILE = 256
EXPERT_GROUP = 2
CUM_CHUNK = 256


def _cparams(sem):
    return pltpu.CompilerParams(dimension_semantics=sem, vmem_limit_bytes=VMEM_LIMIT)


def _sigmoid(x):
    return 0.5 * jnp.tanh(0.5 * x) + 0.5


def _pack_pairs(v):
    u = lax.bitcast_convert_type(v.astype(BF16).astype(F32), U32)
    half = v.shape[-1] // 2
    w = (u[:, half:] & jnp.uint32(0xFFFF0000)) | (u[:, :half] >> 16)
    return lax.bitcast_convert_type(w, I32)


def _unpack_pairs(w):
    u = lax.bitcast_convert_type(w, U32)
    lo = lax.bitcast_convert_type(u << 16, F32)
    hi = lax.bitcast_convert_type(u & jnp.uint32(0xFFFF0000), F32)
    return lo, hi


def _ada_kernel(c_ref, w_ref, b_ref, o_ref):
    c = c_ref[...]
    ca = c * _sigmoid(c)
    o_ref[0] = jnp.dot(ca, w_ref[0], preferred_element_type=F32) + b_ref[0]


def _ada_mod(c, ada_w, ada_b):
    depth, d, cols = ada_w.shape
    b = c.shape[0]
    tn = 1024
    return pl.pallas_call(
        _ada_kernel,
        grid=(depth, cols // tn),
        in_specs=[
            pl.BlockSpec((b, d), lambda l, j: (0, 0)),
            pl.BlockSpec((1, d, tn), lambda l, j: (l, 0, j)),
            pl.BlockSpec((1, 1, tn), lambda l, j: (l, 0, j)),
        ],
        out_specs=pl.BlockSpec((1, b, tn), lambda l, j: (l, 0, j)),
        out_shape=jax.ShapeDtypeStruct((depth, b, cols), F32),
        compiler_params=_cparams(("arbitrary", "arbitrary")),
        name="ada_mod",
    )(c, ada_w, ada_b.reshape(depth, 1, cols))


def _prep_win_kernel(w_ref, wm_ref, wf_ref):
    x = w_ref[0]
    wm_ref[:, :GATE_COL0] = x[:, :GATE_COL0].astype(BF16)
    wm_ref[:, GATE_COL0:] = x[:, GATE_COL0 + FOX_HEADS:].astype(BF16)
    lane = lax.broadcasted_iota(I32, (x.shape[0], LANES), 1)
    wf_ref[...] = jnp.where(lane < FOX_HEADS, x[:, GATE_COL0:GATE_COL0 + LANES], 0.0).astype(BF16)


def _prep_win(w, layer):
    _, d, cols = w.shape
    rt = 128
    return pl.pallas_call(
        _prep_win_kernel,
        grid=(d // rt,),
        in_specs=[pl.BlockSpec((1, rt, cols), lambda i: (layer, i, 0))],
        out_specs=[pl.BlockSpec((rt, MAIN_COLS), lambda i: (i, 0)), pl.BlockSpec((rt, LANES), lambda i: (i, 0))],
        out_shape=[jax.ShapeDtypeStruct((d, MAIN_COLS), BF16), jax.ShapeDtypeStruct((d, LANES), BF16)],
        compiler_params=_cparams(("arbitrary",)),
        name="inproj_weight_prep",
    )(w)


def _inproj_kernel(x_ref, g_ref, sh_ref, sc_ref, w_ref, wf_ref, proj_ref, fl_ref):
    x = x_ref[...]
    r = lax.rsqrt(jnp.mean(x * x, axis=-1, keepdims=True) + NORM_EPS)
    h = x * r * g_ref[...]
    h = h * (1.0 + sc_ref[0]) + sh_ref[0]
    hb = h.astype(BF16)
    fl_ref[...] = jnp.dot(hb, wf_ref[...], preferred_element_type=F32)
    for j in range(MAIN_COLS // COL_TILE_IN):
        cs = slice(j * COL_TILE_IN, (j + 1) * COL_TILE_IN)
        proj_ref[:, cs] = jnp.dot(hb, w_ref[:, cs], preferred_element_type=F32).astype(BF16)


def _inproj(x2, g, sh, sc, w_main, w_f, seq):
    n, d = x2.shape
    tm = min(ROW_TILE_IN, seq)
    per_seq = seq // tm
    resident = pl.Buffered(1)
    return pl.pallas_call(
        _inproj_kernel,
        grid=(n // tm,),
        in_specs=[
            pl.BlockSpec((tm, d), lambda i: (i, 0)),
            pl.BlockSpec((1, d), lambda i: (0, 0)),
            pl.BlockSpec((1, 1, d), lambda i: (i // per_seq, 0, 0)),
            pl.BlockSpec((1, 1, d), lambda i: (i // per_seq, 0, 0)),
            pl.BlockSpec((d, MAIN_COLS), lambda i: (0, 0), pipeline_mode=resident),
            pl.BlockSpec((d, LANES), lambda i: (0, 0), pipeline_mode=resident),
        ],
        out_specs=[
            pl.BlockSpec((tm, MAIN_COLS), lambda i: (i, 0)),
            pl.BlockSpec((tm, LANES), lambda i: (i, 0)),
        ],
        out_shape=[
            jax.ShapeDtypeStruct((n, MAIN_COLS), BF16),
            jax.ShapeDtypeStruct((n, LANES), F32),
        ],
        compiler_params=_cparams(("arbitrary",)),
        name="inproj",
    )(x2, g, sh, sc, w_main, w_f)


def _split3(v):
    hi = v.astype(BF16)
    r1 = v - hi.astype(F32)
    mid = r1.astype(BF16)
    lo = (r1 - mid.astype(F32)).astype(BF16)
    return hi, mid, lo


def _fcum_kernel(fl_ref, fb_ref, o_ref, *, seq):
    c = min(CUM_CHUNK, seq)
    row = lax.broadcasted_iota(I32, (c, c), 0)
    col = lax.broadcasted_iota(I32, (c, c), 1)
    lower = (col <= row).astype(BF16)
    carry = jnp.zeros((1, LANES), F32)
    for ci in range(seq // c):
        x = fl_ref[pl.ds(ci * c, c), :] + fb_ref[...]
        lf = jnp.minimum(x, 0.0) - jnp.log(1.0 + jnp.exp(-jnp.abs(x)))
        hi, mid, lo = _split3(lf)
        cs = (jnp.dot(lower, hi, preferred_element_type=F32)
              + jnp.dot(lower, mid, preferred_element_type=F32)
              + jnp.dot(lower, lo, preferred_element_type=F32)) + carry
        o_ref[0, :, pl.ds(ci * c, c)] = cs.T[:FOX_HEADS, :]
        carry = cs[c - 1:c, :]


def _fcum(flog, fb_pad, batch, seq):
    return pl.pallas_call(
        functools.partial(_fcum_kernel, seq=seq),
        grid=(batch,),
        in_specs=[
            pl.BlockSpec((seq, LANES), lambda b: (b, 0)),
            pl.BlockSpec((1, LANES), lambda b: (0, 0)),
        ],
        out_specs=pl.BlockSpec((1, FOX_HEADS, seq), lambda b: (b, 0, 0)),
        out_shape=jax.ShapeDtypeStruct((batch, FOX_HEADS, seq), F32),
        compiler_params=_cparams(("arbitrary",)),
        name="fox_decay_cumsum",
    )(flog, fb_pad)


def _attn_kernel(*refs, mode, t, seq, lam_init):
    if mode == "diff":
        q_ref, k_ref, v_ref, lam_ref, g_ref, o_ref = refs
        f_ref = None
        lv = lam_ref[...]
        lam = (jnp.exp(jnp.sum(lv[0:1, :] * lv[1:2, :])) - jnp.exp(jnp.sum(lv[2:3, :] * lv[3:4, :]))
               + lam_init)
    else:
        q_ref, k_ref, v_ref, f_ref, o_ref = refs
    lane = lax.broadcasted_iota(I32, (t, LANES), 1)
    row = lax.broadcasted_iota(I32, (t, t), 0)
    col = lax.broadcasted_iota(I32, (t, t), 1)
    nt = (((1,), (1,)), ((), ()))

    for qi in range(seq // t):
        lo = qi * t
        q = q_ref[lo:lo + t, :] * jnp.asarray(HEAD_DIM ** -0.5, BF16)
        zero = jnp.zeros_like(q)
        outs = []
        for idx in range(2):
            qm = jnp.where(lane < HEAD_DIM, q, zero) if idx == 0 else jnp.where(lane >= HEAD_DIM, q, zero)
            sd = lax.dot_general(qm, k_ref[lo:lo + t, :], nt, preferred_element_type=F32)
            if f_ref is not None:
                sd = sd - f_ref[0, idx:idx + 1, lo:lo + t]
            sd = jnp.where(col <= row, sd, NEG_INF)
            m = jnp.max(sd, axis=-1, keepdims=True)
            if qi > 0:
                sm = lax.dot_general(qm, k_ref[0:lo, :], nt, preferred_element_type=F32)
                if f_ref is not None:
                    sm = sm - f_ref[0, idx:idx + 1, 0:lo]
                m = jnp.maximum(m, jnp.max(sm, axis=-1, keepdims=True))
            pd = jnp.exp(sd - m)
            l = jnp.sum(pd, axis=-1, keepdims=True)
            acc = jnp.dot(pd.astype(BF16), v_ref[lo:lo + t, :], preferred_element_type=F32)
            if qi > 0:
                pm = jnp.exp(sm - m)
                l = l + jnp.sum(pm, axis=-1, keepdims=True)
                acc = acc + jnp.dot(pm.astype(BF16), v_ref[0:lo, :], preferred_element_type=F32)
            outs.append(acc / l)
        if mode == "diff":
            o = outs[0] - lam * outs[1]
            r = lax.rsqrt(jnp.mean(o * o, axis=-1, keepdims=True) + NORM_EPS)
            o = o * r * g_ref[...] * (1.0 - lam_init)
        else:
            o = jnp.where(lane < HEAD_DIM, outs[0], outs[1])
        o_ref[lo:lo + t, :] = o.astype(BF16)


def _attention(proj, mode, col_q, col_k, col_v, batch, seq, extra, lam_init=0.0):
    n = proj.shape[0]
    t = min(ATTN_TILE, seq)
    n_blk = BRANCH_WIDTH // LANES
    in_specs = [
        pl.BlockSpec((seq, LANES), lambda b, h: (b, col_q + h)),
        pl.BlockSpec((seq, LANES), lambda b, h: (b, col_k + h)),
        pl.BlockSpec((seq, LANES), lambda b, h: (b, col_v + h)),
    ]
    if mode == "diff":
        lamv, g = extra
        in_specs += [pl.BlockSpec((4, HEAD_DIM), lambda b, h: (0, 0)),
                     pl.BlockSpec((1, LANES), lambda b, h: (0, 0))]
        args = (proj, proj, proj, lamv, g)
    else:
        (frow,) = extra
        in_specs += [pl.BlockSpec((1, 2, seq), lambda b, h: (b * n_blk + h, 0, 0))]
        args = (proj, proj, proj, frow)
    return pl.pallas_call(
        functools.partial(_attn_kernel, mode=mode, t=t, seq=seq, lam_init=lam_init),
        grid=(batch, n_blk),
        in_specs=in_specs,
        out_specs=pl.BlockSpec((seq, LANES), lambda b, h: (b, h)),
        out_shape=jax.ShapeDtypeStruct((n, BRANCH_WIDTH), BF16),
        compiler_params=_cparams(("arbitrary", "arbitrary")),
        name="attn_" + mode,
    )(*args)


def _ret_kernel(q_ref, k_ref, v_ref, g_ref, cos_ref, sin_ref, dm_ref, qd_ref, kd_ref, cd_ref,
                o_ref, *, seq, c):
    st = jnp.zeros((RET_HEAD_DIM, RET_HEAD_DIM), F32)
    half = RET_HEAD_DIM // 2

    for ci in range(seq // c):
        sl = pl.ds(ci * c, c)
        cos = cos_ref[sl, :]
        sin = sin_ref[sl, :]
        q = q_ref[sl, :].astype(F32)
        k = k_ref[sl, :].astype(F32)
        qr = q * cos + pltpu.roll(q, half, 1) * sin
        kr = (k * cos + pltpu.roll(k, half, 1) * sin) * (RET_HEAD_DIM ** -0.5)
        v = v_ref[sl, :]
        sc = lax.dot_general(qr.astype(BF16), kr.astype(BF16), (((1,), (1,)), ((), ())),
                             preferred_element_type=F32) * dm_ref[0]
        inner = jnp.dot(sc.astype(BF16), v, preferred_element_type=F32)
        cross = jnp.dot((qr * qd_ref[0]).astype(BF16), st.astype(BF16), preferred_element_type=F32)
        o = inner + cross
        kt = (kr * kd_ref[0]).T.astype(BF16)
        st = st * cd_ref[0] + jnp.dot(kt, v, preferred_element_type=F32)
        mu = jnp.mean(o, axis=-1, keepdims=True)
        oc = o - mu
        var = jnp.mean(oc * oc, axis=-1, keepdims=True)
        on = oc * lax.rsqrt(var + NORM_EPS)
        g = g_ref[sl, :].astype(F32)
        o_ref[sl, :] = (on * (g * _sigmoid(g))).astype(BF16)


def _retention(proj, tables, batch, seq):
    n = proj.shape[0]
    c = min(RET_CHUNK, seq)
    cos, sin, dm, qd, kd, cd = tables
    col = BRANCH_WIDTH * 3 // LANES
    step = BRANCH_WIDTH // LANES

    def pspec(k):
        return pl.BlockSpec((seq, LANES), lambda b, h: (b, col + k * step + h))

    return pl.pallas_call(
        functools.partial(_ret_kernel, seq=seq, c=c),
        grid=(batch, RET_HEADS),
        in_specs=[
            pspec(0), pspec(1), pspec(2), pspec(3),
            pl.BlockSpec((seq, LANES), lambda b, h: (0, 0)),
            pl.BlockSpec((seq, LANES), lambda b, h: (0, 0)),
            pl.BlockSpec((1, c, c), lambda b, h: (h, 0, 0)),
            pl.BlockSpec((1, c, LANES), lambda b, h: (h, 0, 0)),
            pl.BlockSpec((1, c, LANES), lambda b, h: (h, 0, 0)),
            pl.BlockSpec((1, LANES, LANES), lambda b, h: (h, 0, 0)),
        ],
        out_specs=pl.BlockSpec((seq, LANES), lambda b, h: (b, h)),
        out_shape=jax.ShapeDtypeStruct((n, BRANCH_WIDTH), BF16),
        compiler_params=_cparams(("arbitrary", "arbitrary")),
        name="retention",
    )(proj, proj, proj, proj, cos, sin, dm, qd, kd, cd)


def _retention_tables(seq):
    c = min(RET_CHUNK, seq)
    d = RET_HEAD_DIM
    inv = ROPE_BASE ** (-jnp.arange(0, d, 2, dtype=F32) / d)
    ang = jnp.arange(seq).astype(F32)[:, None] * inv[None, :]
    cos = jnp.cos(ang)
    sin = jnp.sin(ang)
    cos_full = jnp.concatenate([cos, cos], axis=-1)
    sin_signed = jnp.concatenate([-sin, sin], axis=-1)
    log_gamma = jnp.log1p(-jnp.exp2(-RET_GAMMA_EXP0 - jnp.arange(RET_HEADS, dtype=F32)))
    idx = jnp.arange(c, dtype=F32)
    dist = idx[:, None] - idx[None, :]
    dm = jnp.where(dist >= 0, jnp.exp(log_gamma[:, None, None] * jnp.maximum(dist, 0.0)), 0.0)
    qd = jnp.exp(log_gamma[:, None] * (idx + 1.0))
    kd = jnp.exp(log_gamma[:, None] * (c - 1.0 - idx))
    cd = jnp.exp(log_gamma * c)
    qd = jnp.broadcast_to(qd[:, :, None], (RET_HEADS, c, LANES))
    kd = jnp.broadcast_to(kd[:, :, None], (RET_HEADS, c, LANES))
    cd = jnp.broadcast_to(cd[:, None, None], (RET_HEADS, LANES, LANES))
    return cos_full, sin_signed, dm, qd, kd, cd


def _merge_kernel(od_ref, or_ref, of_ref, gd_ref, gr_ref, gf_ref, gb_ref, wb_ref, wo_ref, x_ref,
                  g1_ref, n2_ref, sh_ref, sc_ref, rw_ref, rb_ref,
                  xo_ref, hp_ref, ti_ref, tw_ref, cnt_ref, cnt_scr, *, tm, sub):
    @pl.when(pl.program_id(0) == 0)
    def _():
        cnt_scr[...] = jnp.zeros(cnt_scr.shape, F32)

    lane = lax.broadcasted_iota(I32, (sub, LANES), 1)
    r_i = lax.broadcasted_iota(I32, (sub, sub), 0)
    c_i = lax.broadcasted_iota(I32, (sub, sub), 1)
    earlier = (c_i < r_i).astype(BF16)
    counts = cnt_scr[...]

    for s0 in range(0, tm, sub):
        rs = slice(s0, s0 + sub)
        merged = None
        for i, (o_ref, gate_ref) in enumerate(((od_ref, gd_ref), (or_ref, gr_ref), (of_ref, gf_ref))):
            gate = _sigmoid(gate_ref[rs, :] + gb_ref[i].astype(BF16))
            term = gate.astype(F32) * jnp.dot(o_ref[rs, :], wb_ref[i], preferred_element_type=F32)
            merged = term if merged is None else merged + term
        mix = jnp.dot(merged.astype(BF16), wo_ref[...], preferred_element_type=F32)
        xn = x_ref[rs, :] + g1_ref[0] * mix
        xo_ref[rs, :] = xn

        r = lax.rsqrt(jnp.mean(xn * xn, axis=-1, keepdims=True) + NORM_EPS)
        h = xn * r * n2_ref[...]
        h = h * (1.0 + sc_ref[0]) + sh_ref[0]
        hp = _pack_pairs(h)
        hp_ref[0, rs, :] = hp[:, :QUART]
        hp_ref[1, rs, :] = hp[:, QUART:]

        hh, hm, _ = _split3(h)
        logits = (jnp.dot(hh, rw_ref[0], preferred_element_type=F32)
                  + jnp.dot(hh, rw_ref[1], preferred_element_type=F32)
                  + jnp.dot(hm, rw_ref[0], preferred_element_type=F32)) + rb_ref[...]

        vals, idxs = [], []
        for _ in range(TOP_K):
            m = jnp.max(logits, axis=-1, keepdims=True)
            idx = jnp.min(jnp.where(logits == m, lane, LANES), axis=-1, keepdims=True)
            vals.append(m)
            idxs.append(idx)
            logits = jnp.where(lane == idx, -3.0e38, logits)
        es = [jnp.exp(v - vals[0]) for v in vals]
        den = es[0] + es[1] + es[2] + es[3]

        hits = [lane == idxs[k] for k in range(TOP_K)]
        picked = (hits[0] | hits[1] | hits[2] | hits[3])
        base = jnp.dot(earlier, picked.astype(BF16), preferred_element_type=F32) + counts
        counts = counts + jnp.sum(picked.astype(F32), axis=0, keepdims=True)

        ti = jnp.zeros((sub, LANES), I32)
        tw = jnp.zeros((sub, LANES), F32)
        for k in range(TOP_K):
            rank = jnp.sum(jnp.where(hits[k], base, 0.0), axis=-1, keepdims=True).astype(I32)
            ti = jnp.where(lane == k, idxs[k], ti)
            ti = jnp.where(lane == TOP_K + k, rank, ti)
            tw = jnp.where(lane == k, es[k] / den, tw)
        ti_ref[rs, :] = ti
        tw_ref[rs, :] = tw

    cnt_scr[...] = counts
    cnt_ref[...] = counts.astype(I32)


def _merge(o_diff, o_ret, o_fox, proj, gate_b, w_branch, w_out, x2, g1, n2g, sh2, sc2, rw, rb, seq):
    n, d = x2.shape
    tm = min(MERGE_TILE, seq)
    sub = min(MERGE_SUB, tm)
    per_seq = seq // tm
    gcol = GATE_COL0 // d

    def rows(w):
        return pl.BlockSpec((tm, w), lambda i: (i, 0))

    def gspec(k):
        return pl.BlockSpec((tm, d), lambda i: (i, gcol + k))

    def whole(shape):
        return pl.BlockSpec(shape, lambda i: (0,) * len(shape))

    def per_b():
        return pl.BlockSpec((1, 1, d), lambda i: (i // per_seq, 0, 0))

    return pl.pallas_call(
        functools.partial(_merge_kernel, tm=tm, sub=sub),
        grid=(n // tm,),
        in_specs=[
            rows(BRANCH_WIDTH), rows(BRANCH_WIDTH), rows(BRANCH_WIDTH),
            gspec(0), gspec(1), gspec(2),
            whole((N_BRANCHES, 1, d)), whole((N_BRANCHES, BRANCH_WIDTH, d)), whole((d, d)),
            rows(d), per_b(), whole((1, d)), per_b(), per_b(),
            whole((2, d, LANES)), whole((1, LANES)),
        ],
        out_specs=[rows(d), pl.BlockSpec((2, tm, QUART), lambda i: (0, i, 0)), rows(LANES), rows(LANES),
                   whole((1, LANES))],
        out_shape=[
            jax.ShapeDtypeStruct((n, d), F32),
            jax.ShapeDtypeStruct((2, n, QUART), I32),
            jax.ShapeDtypeStruct((n, LANES), I32),
            jax.ShapeDtypeStruct((n, LANES), F32),
            jax.ShapeDtypeStruct((1, LANES), I32),
        ],
        scratch_shapes=[pltpu.VMEM((1, LANES), F32)],
        compiler_params=_cparams(("arbitrary",)),
        name="merge_outproj_router",
    )(o_diff, o_ret, o_fox, proj, proj, proj, gate_b, w_branch, w_out, x2, g1, n2g, sh2, sc2, rw, rb)


def _prep_up_kernel(w_ref, o_ref, t_scr):
    f = t_scr.shape[1] // 2
    for g in range(t_scr.shape[0]):
        cols = slice(g * LANES, (g + 1) * LANES)
        t_scr[g] = w_ref[0, 0, cols, :].T
        o_ref[0, 0:f, cols] = t_scr[g, pl.ds(0, f, stride=2), :].astype(BF16)
        o_ref[0, f:, cols] = t_scr[g, pl.ds(1, f, stride=2), :].astype(BF16)


def _prep_up(w_up, layer):
    _, e, d, f2 = w_up.shape
    groups = 4
    kt = groups * LANES
    return pl.pallas_call(
        _prep_up_kernel,
        grid=(e, d // kt),
        in_specs=[pl.BlockSpec((1, 1, kt, f2), lambda i, j: (layer, i, j, 0))],
        out_specs=pl.BlockSpec((1, f2, kt), lambda i, j: (i, 0, j)),
        out_shape=jax.ShapeDtypeStruct((e, f2, d), BF16),
        scratch_shapes=[pltpu.VMEM((groups, f2, LANES), F32)],
        compiler_params=_cparams(("arbitrary", "arbitrary")),
        name="expert_up_weight_prep",
    )(w_up)


def _cast_kernel(w_ref, o_ref):
    o_ref[0] = w_ref[0, 0].astype(BF16)


def _prep_down(w_down, layer):
    _, e, f, d = w_down.shape
    return pl.pallas_call(
        _cast_kernel,
        grid=(e,),
        in_specs=[pl.BlockSpec((1, 1, f, d), lambda i: (layer, i, 0, 0))],
        out_specs=pl.BlockSpec((1, f, d), lambda i: (i, 0, 0)),
        out_shape=jax.ShapeDtypeStruct((e, f, d), BF16),
        compiler_params=_cparams(("arbitrary",)),
        name="expert_down_weight_cast",
    )(w_down)


def _expert_kernel(te_ref, nu_ref, xs_ref, *refs):
    y_ref = refs[-1]
    nt = (((1,), (1,)), ((), ()))
    first_tile = pl.program_id(0) * EXPERT_GROUP

    @pl.when(first_tile < nu_ref[0])
    def _():
        for s in range(EXPERT_GROUP):
            wu_ref, bu_ref, wd_ref, bd_ref = refs[4 * s:4 * s + 4]
            rs = slice(s * EXPERT_TILE, (s + 1) * EXPERT_TILE)
            lo, hi = _unpack_pairs(jnp.concatenate([xs_ref[0, rs, :], xs_ref[1, rs, :]], axis=1))
            lo = lo.astype(BF16)
            hi = hi.astype(BF16)

            def up_half(r0):
                return (lax.dot_general(lo, wu_ref[0, r0:r0 + D_FF, :HALF], nt, preferred_element_type=F32)
                        + lax.dot_general(hi, wu_ref[0, r0:r0 + D_FF, HALF:], nt, preferred_element_type=F32))

            glu = jnp.minimum(up_half(0) + bu_ref[0, :, :D_FF], SWIGLU_LIMIT)
            lin = jnp.clip(up_half(D_FF) + bu_ref[0, :, D_FF:], -SWIGLU_LIMIT, SWIGLU_LIMIT)
            act = glu * _sigmoid(SWIGLU_ALPHA * glu) * (lin + 1.0)
            down = jnp.dot(act.astype(BF16), wd_ref[0], preferred_element_type=F32) + bd_ref[0]
            yp = _pack_pairs(down)
            y_ref[0, rs, :] = yp[:, :QUART]
            y_ref[1, rs, :] = yp[:, QUART:]

    @pl.when(first_tile >= nu_ref[0])
    def _():
        y_ref[...] = jnp.zeros(y_ref.shape, I32)


def _experts(tile_e, n_used, xs, w_up, b_up, w_down, b_down):
    r = xs.shape[1]
    tm = EXPERT_TILE * EXPERT_GROUP
    d = D_MODEL
    in_specs = [pl.BlockSpec((2, tm, QUART), lambda i, te, nu: (0, i, 0))]
    args = [xs]
    for s in range(EXPERT_GROUP):
        def pick(i, te, nu, s=s):
            return (te[i * EXPERT_GROUP + s], 0, 0)
        in_specs += [pl.BlockSpec((1, 2 * D_FF, d), pick), pl.BlockSpec((1, 1, 2 * D_FF), pick),
                     pl.BlockSpec((1, D_FF, d), pick), pl.BlockSpec((1, 1, d), pick)]
        args += [w_up, b_up, w_down, b_down]
    grid_spec = pltpu.PrefetchScalarGridSpec(
        num_scalar_prefetch=2,
        grid=(r // tm,),
        in_specs=in_specs,
        out_specs=pl.BlockSpec((2, tm, QUART), lambda i, te, nu: (0, i, 0)),
    )
    return pl.pallas_call(
        _expert_kernel,
        grid_spec=grid_spec,
        out_shape=jax.ShapeDtypeStruct((2, r, QUART), I32),
        compiler_params=_cparams(("arbitrary",)),
        name="expert_mlp",
    )(tile_e, n_used, *args)


def _combine_kernel(y_ref, tw_ref, x_ref, g2_ref, fg_ref, o_ref, *, final):
    tw = tw_ref[...]
    parts = [None] * 4
    for k in range(TOP_K):
        w = tw[:, k:k + 1]
        for piece in range(2):
            lo, hi = _unpack_pairs(y_ref[piece, k])
            for slot, v in ((piece, lo), (2 + piece, hi)):
                parts[slot] = w * v if parts[slot] is None else parts[slot] + w * v
    ffn = jnp.concatenate(parts, axis=-1)
    xn = x_ref[...] + g2_ref[0] * ffn
    if final:
        r = lax.rsqrt(jnp.mean(xn * xn, axis=-1, keepdims=True) + NORM_EPS)
        xn = xn * r * fg_ref[...]
    o_ref[...] = xn


def _combine(ytok, top_w, x2, g2, final_g, seq, final):
    n, d = x2.shape
    tm = min(ROW_TILE_OUT, seq)
    per_seq = seq // tm
    return pl.pallas_call(
        functools.partial(_combine_kernel, final=final),
        grid=(n // tm,),
        in_specs=[
            pl.BlockSpec((2, TOP_K, tm, QUART), lambda i: (0, 0, i, 0)),
            pl.BlockSpec((tm, LANES), lambda i: (i, 0)),
            pl.BlockSpec((tm, d), lambda i: (i, 0)),
            pl.BlockSpec((1, 1, d), lambda i: (i // per_seq, 0, 0)),
            pl.BlockSpec((1, d), lambda i: (0, 0)),
        ],
        out_specs=pl.BlockSpec((tm, d), lambda i: (i, 0)),
        out_shape=jax.ShapeDtypeStruct((n, d), F32),
        compiler_params=_cparams(("arbitrary",)),
        name="moe_combine",
    )(ytok, top_w, x2, g2, final_g)


def _route(top_idx, counts, n_rows):
    tm = EXPERT_TILE
    e = top_idx[:, :TOP_K]
    rank = top_idx[:, TOP_K:2 * TOP_K]
    sizes = counts[0, :N_EXPERTS]
    padded = ((sizes + tm - 1) // tm) * tm
    pend = jnp.cumsum(padded)
    pstart = pend - padded
    onehot = e[:, :, None] == jnp.arange(N_EXPERTS, dtype=I32)[None, None, :]
    pos = jnp.sum(jnp.where(onehot, pstart[None, None, :], 0), axis=-1) + rank
    tiles = jnp.arange(n_rows // tm, dtype=I32) * tm
    tile_e = jnp.minimum(jnp.sum(tiles[:, None] >= pend[None, :], axis=-1), N_EXPERTS - 1).astype(I32)
    n_used = (pend[-1] // tm).astype(I32).reshape(1)
    return pos.T.astype(I32), tile_e, n_used


def _sc_gather(table, idx):
    m = idx.shape[0]
    width = table.shape[1]
    assert m % (SC_WINDOW * SC_WORKERS) == 0
    mesh = plsc.VectorSubcoreMesh(core_axis_name="c", subcore_axis_name="s")

    @pl.kernel(out_type=jax.ShapeDtypeStruct((m, width), table.dtype), mesh=mesh)
    def gather_rows(t_hbm, i_hbm, o_hbm):
        def body(i_vmem, o_vmem):
            pltpu.sync_copy(t_hbm.at[i_vmem.at[0]], o_vmem)

        pltpu.emit_pipeline(
            body,
            grid=(m // SC_WINDOW,),
            in_specs=[pl.BlockSpec((1, SC_WINDOW), lambda i: (0, i))],
            out_specs=[pl.BlockSpec((SC_WINDOW, width), lambda i: (i, 0))],
            core_axis_name=("c", "s"),
            dimension_semantics=(pltpu.PARALLEL,),
        )(i_hbm, o_hbm)

    return gather_rows(table, idx.reshape(1, m))


def _sc_scatter(rows, idx, n_out):
    t, width = rows.shape
    picks = idx.shape[0]
    assert t % SC_WINDOW == 0
    mesh = plsc.VectorSubcoreMesh(core_axis_name="c", subcore_axis_name="s")

    @pl.kernel(out_type=jax.ShapeDtypeStruct((n_out, width), rows.dtype), mesh=mesh)
    def scatter_rows(x_hbm, i_hbm, o_hbm):
        def body(x_vmem, i_vmem):
            for k in range(picks):
                pltpu.sync_copy(x_vmem, o_hbm.at[i_vmem.at[k]])

        pltpu.emit_pipeline(
            body,
            grid=(t // SC_WINDOW,),
            in_specs=[pl.BlockSpec((SC_WINDOW, width), lambda i: (i, 0)),
                      pl.BlockSpec((picks, SC_WINDOW), lambda i: (0, i))],
            out_specs=[],
            core_axis_name=("c", "s"),
            dimension_semantics=(pltpu.PARALLEL,),
        )(x_hbm, i_hbm)

    return scatter_rows(rows, idx)


def _dispatch(h2p, pos_t, n_rows):
    n = h2p.shape[1]
    idx = jnp.concatenate([pos_t, pos_t + n_rows], axis=1)
    return _sc_scatter(h2p.reshape(2 * n, QUART), idx, 2 * n_rows).reshape(2, n_rows, QUART)


def _collect(y, pos_t):
    n_rows = y.shape[1]
    flat = pos_t.reshape(-1)
    out = _sc_gather(y.reshape(2 * n_rows, QUART), jnp.concatenate([flat, flat + n_rows]))
    return out.reshape(2, TOP_K, pos_t.shape[1], QUART)


def kernel(x, c, norm1_g, norm2_g, ada_w, ada_b, w_in, gate_b, fox_fb, lam_q1, lam_k1, lam_q2, lam_k2,
           diff_subln_g, w_branch, w_out, router_w, router_b, exp_w_up, exp_b_up, exp_w_down,
           exp_b_down, final_g):
    batch, seq, d = x.shape
    depth = ada_w.shape[0]
    n = batch * seq
    x2 = x.reshape(n, d)

    mod = _ada_mod(c, ada_w, ada_b)
    tables = _retention_tables(seq)
    n_rows = n * TOP_K + N_EXPERTS * EXPERT_TILE
    blk = BRANCH_WIDTH // LANES

    for l in range(depth):
        lam_init = 0.8 - 0.6 * math.exp(-0.3 * l)
        sh1, sc1, g1, sh2, sc2, g2 = [m.reshape(batch, 1, d) for m in jnp.split(mod[l], 6, axis=-1)]

        w_main, w_f = _prep_win(w_in, l)
        proj, flog = _inproj(x2, norm1_g[l].reshape(1, d), sh1, sc1, w_main, w_f, seq)

        fb_pad = jnp.pad(fox_fb[l], (0, LANES - FOX_HEADS)).reshape(1, LANES)
        frow = _fcum(flog, fb_pad, batch, seq).reshape(batch * blk, 2, seq)

        lamv = jnp.stack([lam_q1[l], lam_k1[l], lam_q2[l], lam_k2[l]])
        o_diff = _attention(proj, "diff", 0, blk, 2 * blk, batch, seq,
                            (lamv, diff_subln_g[l].reshape(1, LANES)), lam_init)
        o_ret = _retention(proj, tables, batch, seq)
        o_fox = _attention(proj, "fox", 7 * blk, 8 * blk, 9 * blk, batch, seq, (frow,))

        rw = jnp.pad(router_w[l], ((0, 0), (0, LANES - N_EXPERTS)))
        rw_hi = rw.astype(BF16)
        rw = jnp.stack([rw_hi, (rw - rw_hi.astype(F32)).astype(BF16)])
        rb = jnp.pad(router_b[l], (0, LANES - N_EXPERTS), constant_values=NEG_INF).reshape(1, LANES)
        x2, h2p, top_idx, top_w, counts = _merge(
            o_diff, o_ret, o_fox, proj, gate_b[l].reshape(N_BRANCHES, 1, d), w_branch[l].astype(BF16),
            w_out[l].astype(BF16), x2, g1, norm2_g[l].reshape(1, d), sh2, sc2, rw, rb, seq)

        pos_t, tile_e, n_used = _route(top_idx, counts, n_rows)
        xs = _dispatch(h2p, pos_t, n_rows)

        w_up = _prep_up(exp_w_up, l)
        bu = exp_b_up[l]
        b_up = jnp.concatenate([bu[:, 0::2], bu[:, 1::2]], axis=-1).reshape(N_EXPERTS, 1, 2 * D_FF)
        y = _experts(tile_e, n_used, xs, w_up, b_up, _prep_down(exp_w_down, l),
                     exp_b_down[l].reshape(N_EXPERTS, 1, d))
        ytok = _collect(y, pos_t)
        x2 = _combine(ytok, top_w, x2, g2, final_g.reshape(1, d), seq, final=(l == depth - 1))

    return x2.reshape(batch, seq, d)
```

```python
import functools
import math

import jax
import jax.numpy as jnp
from jax import lax
from jax.experimental import pallas as pl
from jax.experimental.pallas import tpu as pltpu
from jax.experimental.pallas import tpu_sc as plsc

F32 = jnp.float32
BF16 = jnp.bfloat16
I32 = jnp.int32
U32 = jnp.uint32

D_MODEL = 1024
HEAD_DIM = 64
BRANCH_WIDTH = D_MODEL // 2
N_BRANCHES = 3
FOX_HEADS = BRANCH_WIDTH // HEAD_DIM
RET_HEADS = 4
RET_HEAD_DIM = 128
ROPE_BASE = 10000.0
RET_GAMMA_EXP0 = 5.0
N_EXPERTS = 32
TOP_K = 4
D_FF = D_MODEL
SWIGLU_LIMIT = 7.0
SWIGLU_ALPHA = 1.702
NORM_EPS = 1e-5
NEG_INF = -1e30

LANES = 128
MAIN_COLS = 8192
GATE_COL0 = 10 * BRANCH_WIDTH
HALF = D_MODEL // 2
QUART = HALF // 2
SC_WINDOW = 128
SC_WORKERS = 32
VMEM_LIMIT = 48 * 1024 * 1024

ROW_TILE_IN = 512
COL_TILE_IN = 1024
ATTN_TILE = 256
RET_CHUNK = 256
ROW_TILE_OUT = 256
MERGE_TILE = 512
MERGE_SUB = 256
EXPERT_TILE = 256
EXPERT_GROUP = 2
CUM_CHUNK = 256


def _cparams(sem):
    return pltpu.CompilerParams(dimension_semantics=sem, vmem_limit_bytes=VMEM_LIMIT)


def _sigmoid(x):
    return 0.5 * jnp.tanh(0.5 * x) + 0.5


def _pack_pairs(v):
    u = lax.bitcast_convert_type(v.astype(BF16).astype(F32), U32)
    half = v.shape[-1] // 2
    w = (u[:, half:] & jnp.uint32(0xFFFF0000)) | (u[:, :half] >> 16)
    return lax.bitcast_convert_type(w, I32)


def _unpack_pairs(w):
    u = lax.bitcast_convert_type(w, U32)
    lo = lax.bitcast_convert_type(u << 16, F32)
    hi = lax.bitcast_convert_type(u & jnp.uint32(0xFFFF0000), F32)
    return lo, hi


def _ada_kernel(c_ref, w_ref, b_ref, o_ref):
    c = c_ref[...]
    ca = c * _sigmoid(c)
    o_ref[0] = jnp.dot(ca, w_ref[0], preferred_element_type=F32) + b_ref[0]


def _ada_mod(c, ada_w, ada_b):
    depth, d, cols = ada_w.shape
    b = c.shape[0]
    tn = 1024
    return pl.pallas_call(
        _ada_kernel,
        grid=(depth, cols // tn),
        in_specs=[
            pl.BlockSpec((b, d), lambda l, j: (0, 0)),
            pl.BlockSpec((1, d, tn), lambda l, j: (l, 0, j)),
            pl.BlockSpec((1, 1, tn), lambda l, j: (l, 0, j)),
        ],
        out_specs=pl.BlockSpec((1, b, tn), lambda l, j: (l, 0, j)),
        out_shape=jax.ShapeDtypeStruct((depth, b, cols), F32),
        compiler_params=_cparams(("arbitrary", "arbitrary")),
        name="ada_mod",
    )(c, ada_w, ada_b.reshape(depth, 1, cols))


def _prep_win_kernel(w_ref, wm_ref, wf_ref):
    x = w_ref[0]
    wm_ref[:, :GATE_COL0] = x[:, :GATE_COL0].astype(BF16)
    wm_ref[:, GATE_COL0:] = x[:, GATE_COL0 + FOX_HEADS:].astype(BF16)
    lane = lax.broadcasted_iota(I32, (x.shape[0], LANES), 1)
    wf_ref[...] = jnp.where(lane < FOX_HEADS, x[:, GATE_COL0:GATE_COL0 + LANES], 0.0).astype(BF16)


def _prep_win(w, layer):
    _, d, cols = w.shape
    rt = 128
    return pl.pallas_call(
        _prep_win_kernel,
        grid=(d // rt,),
        in_specs=[pl.BlockSpec((1, rt, cols), lambda i: (layer, i, 0))],
        out_specs=[pl.BlockSpec((rt, MAIN_COLS), lambda i: (i, 0)), pl.BlockSpec((rt, LANES), lambda i: (i, 0))],
        out_shape=[jax.ShapeDtypeStruct((d, MAIN_COLS), BF16), jax.ShapeDtypeStruct((d, LANES), BF16)],
        compiler_params=_cparams(("arbitrary",)),
        name="inproj_weight_prep",
    )(w)


def _inproj_kernel(x_ref, g_ref, sh_ref, sc_ref, w_ref, wf_ref, proj_ref, fl_ref):
    x = x_ref[...]
    r = lax.rsqrt(jnp.mean(x * x, axis=-1, keepdims=True) + NORM_EPS)
    h = x * r * g_ref[...]
    h = h * (1.0 + sc_ref[0]) + sh_ref[0]
    hb = h.astype(BF16)
    fl_ref[...] = jnp.dot(hb, wf_ref[...], preferred_element_type=F32)
    for j in range(MAIN_COLS // COL_TILE_IN):
        cs = slice(j * COL_TILE_IN, (j + 1) * COL_TILE_IN)
        proj_ref[:, cs] = jnp.dot(hb, w_ref[:, cs], preferred_element_type=F32).astype(BF16)


def _inproj(x2, g, sh, sc, w_main, w_f, seq):
    n, d = x2.shape
    tm = min(ROW_TILE_IN, seq)
    per_seq = seq // tm
    resident = pl.Buffered(1)
    return pl.pallas_call(
        _inproj_kernel,
        grid=(n // tm,),
        in_specs=[
            pl.BlockSpec((tm, d), lambda i: (i, 0)),
            pl.BlockSpec((1, d), lambda i: (0, 0)),
            pl.BlockSpec((1, 1, d), lambda i: (i // per_seq, 0, 0)),
            pl.BlockSpec((1, 1, d), lambda i: (i // per_seq, 0, 0)),
            pl.BlockSpec((d, MAIN_COLS), lambda i: (0, 0), pipeline_mode=resident),
            pl.BlockSpec((d, LANES), lambda i: (0, 0), pipeline_mode=resident),
        ],
        out_specs=[
            pl.BlockSpec((tm, MAIN_COLS), lambda i: (i, 0)),
            pl.BlockSpec((tm, LANES), lambda i: (i, 0)),
        ],
        out_shape=[
            jax.ShapeDtypeStruct((n, MAIN_COLS), BF16),
            jax.ShapeDtypeStruct((n, LANES), F32),
        ],
        compiler_params=_cparams(("arbitrary",)),
        name="inproj",
    )(x2, g, sh, sc, w_main, w_f)


def _split3(v):
    hi = v.astype(BF16)
    r1 = v - hi.astype(F32)
    mid = r1.astype(BF16)
    lo = (r1 - mid.astype(F32)).astype(BF16)
    return hi, mid, lo


def _fcum_kernel(fl_ref, fb_ref, o_ref, *, seq):
    c = min(CUM_CHUNK, seq)
    row = lax.broadcasted_iota(I32, (c, c), 0)
    col = lax.broadcasted_iota(I32, (c, c), 1)
    lower = (col <= row).astype(BF16)
    carry = jnp.zeros((1, LANES), F32)
    for ci in range(seq // c):
        x = fl_ref[pl.ds(ci * c, c), :] + fb_ref[...]
        lf = jnp.minimum(x, 0.0) - jnp.log(1.0 + jnp.exp(-jnp.abs(x)))
        hi, mid, lo = _split3(lf)
        cs = (jnp.dot(lower, hi, preferred_element_type=F32)
              + jnp.dot(lower, mid, preferred_element_type=F32)
              + jnp.dot(lower, lo, preferred_element_type=F32)) + carry
        o_ref[0, :, pl.ds(ci * c, c)] = cs.T[:FOX_HEADS, :]
        carry = cs[c - 1:c, :]


def _fcum(flog, fb_pad, batch, seq):
    return pl.pallas_call(
        functools.partial(_fcum_kernel, seq=seq),
        grid=(batch,),
        in_specs=[
            pl.BlockSpec((seq, LANES), lambda b: (b, 0)),
            pl.BlockSpec((1, LANES), lambda b: (0, 0)),
        ],
        out_specs=pl.BlockSpec((1, FOX_HEADS, seq), lambda b: (b, 0, 0)),
        out_shape=jax.ShapeDtypeStruct((batch, FOX_HEADS, seq), F32),
        compiler_params=_cparams(("arbitrary",)),
        name="fox_decay_cumsum",
    )(flog, fb_pad)


def _attn_kernel(*refs, mode, t, seq, lam_init):
    if mode == "diff":
        q_ref, k_ref, v_ref, lam_ref, g_ref, o_ref, vx_scr = refs
        f_ref = None
        lv = lam_ref[...]
        lam = (jnp.exp(jnp.sum(lv[0:1, :] * lv[1:2, :])) - jnp.exp(jnp.sum(lv[2:3, :] * lv[3:4, :]))
               + lam_init)
    else:
        q_ref, k_ref, v_ref, f_ref, o_ref, vx_scr = refs
    lane = lax.broadcasted_iota(I32, (t, LANES), 1)
    row = lax.broadcasted_iota(I32, (t, t), 0)
    col = lax.broadcasted_iota(I32, (t, t), 1)
    nt = (((1,), (1,)), ((), ()))

    v_all = v_ref[...]
    one = jnp.ones_like(v_all)
    if mode == "diff":
        vx_scr[0] = jnp.concatenate([v_all, one], axis=1)
    else:
        lane_s = lax.broadcasted_iota(I32, v_all.shape, 1)
        vx_scr[0] = jnp.where(lane_s < HEAD_DIM, v_all, one)
        vx_scr[1] = jnp.where(lane_s >= HEAD_DIM, v_all, one)

    for qi in range(seq // t):
        lo = qi * t
        q = q_ref[lo:lo + t, :] * jnp.asarray(HEAD_DIM ** -0.5, BF16)
        zero = jnp.zeros_like(q)
        outs = []
        for idx in range(2):
            vx = vx_scr.at[0 if mode == "diff" else idx]
            qm = jnp.where(lane < HEAD_DIM, q, zero) if idx == 0 else jnp.where(lane >= HEAD_DIM, q, zero)
            sd = lax.dot_general(qm, k_ref[lo:lo + t, :], nt, preferred_element_type=F32)
            if f_ref is not None:
                sd = sd - f_ref[0, idx:idx + 1, lo:lo + t]
            sd = jnp.where(col <= row, sd, NEG_INF)
            m = jnp.max(sd, axis=-1, keepdims=True)
            if qi > 0:
                sm = lax.dot_general(qm, k_ref[0:lo, :], nt, preferred_element_type=F32)
                if f_ref is not None:
                    sm = sm - f_ref[0, idx:idx + 1, 0:lo]
                m = jnp.maximum(m, jnp.max(sm, axis=-1, keepdims=True))
            pd = jnp.exp((sd - m).astype(BF16))
            acc = jnp.dot(pd, vx[lo:lo + t, :], preferred_element_type=F32)
            if qi > 0:
                pm = jnp.exp((sm - m).astype(BF16))
                acc = acc + jnp.dot(pm, vx[0:lo, :], preferred_element_type=F32)
            outs.append(acc)
        if mode == "diff":
            a1 = outs[0][:, :LANES] / outs[0][:, LANES:]
            a2 = outs[1][:, :LANES] / outs[1][:, LANES:]
            o = a1 - lam * a2
            r = lax.rsqrt(jnp.mean(o * o, axis=-1, keepdims=True) + NORM_EPS)
            o = o * r * g_ref[...] * (1.0 - lam_init)
        else:
            num = jnp.where(lane < HEAD_DIM, outs[0], outs[1])
            den = jnp.where(lane < HEAD_DIM, pltpu.roll(outs[0], HEAD_DIM, 1), pltpu.roll(outs[1], HEAD_DIM, 1))
            o = num / den
        o_ref[lo:lo + t, :] = o.astype(BF16)


def _attention(proj, mode, col_q, col_k, col_v, batch, seq, extra, lam_init=0.0):
    n = proj.shape[0]
    t = min(ATTN_TILE, seq)
    n_blk = BRANCH_WIDTH // LANES
    in_specs = [
        pl.BlockSpec((seq, LANES), lambda b, h: (b, col_q + h)),
        pl.BlockSpec((seq, LANES), lambda b, h: (b, col_k + h)),
        pl.BlockSpec((seq, LANES), lambda b, h: (b, col_v + h)),
    ]
    if mode == "diff":
        lamv, g = extra
        in_specs += [pl.BlockSpec((4, HEAD_DIM), lambda b, h: (0, 0)),
                     pl.BlockSpec((1, LANES), lambda b, h: (0, 0))]
        args = (proj, proj, proj, lamv, g)
    else:
        (frow,) = extra
        in_specs += [pl.BlockSpec((1, 2, seq), lambda b, h: (b * n_blk + h, 0, 0))]
        args = (proj, proj, proj, frow)
    return pl.pallas_call(
        functools.partial(_attn_kernel, mode=mode, t=t, seq=seq, lam_init=lam_init),
        grid=(batch, n_blk),
        in_specs=in_specs,
        out_specs=pl.BlockSpec((seq, LANES), lambda b, h: (b, h)),
        out_shape=jax.ShapeDtypeStruct((n, BRANCH_WIDTH), BF16),
        scratch_shapes=[pltpu.VMEM((1, seq, 2 * LANES) if mode == "diff" else (2, seq, LANES), BF16)],
        compiler_params=_cparams(("arbitrary", "arbitrary")),
        name="attn_" + mode,
    )(*args)


def _ret_kernel(q_ref, k_ref, v_ref, g_ref, cos_ref, sin_ref, dm_ref, qd_ref, kd_ref, cd_ref,
                o_ref, *, seq, c):
    st = jnp.zeros((RET_HEAD_DIM, RET_HEAD_DIM), F32)
    half = RET_HEAD_DIM // 2

    for ci in range(seq // c):
        sl = pl.ds(ci * c, c)
        cos = cos_ref[sl, :]
        sin = sin_ref[sl, :]
        q = q_ref[sl, :].astype(F32)
        k = k_ref[sl, :].astype(F32)
        qr = q * cos + pltpu.roll(q, half, 1) * sin
        kr = (k * cos + pltpu.roll(k, half, 1) * sin) * (RET_HEAD_DIM ** -0.5)
        v = v_ref[sl, :]
        sc = lax.dot_general(qr.astype(BF16), kr.astype(BF16), (((1,), (1,)), ((), ())),
                             preferred_element_type=F32) * dm_ref[0]
        inner = jnp.dot(sc.astype(BF16), v, preferred_element_type=F32)
        cross = jnp.dot((qr * qd_ref[0]).astype(BF16), st.astype(BF16), preferred_element_type=F32)
        o = inner + cross
        kt = (kr * kd_ref[0]).T.astype(BF16)
        st = st * cd_ref[0] + jnp.dot(kt, v, preferred_element_type=F32)
        mu = jnp.mean(o, axis=-1, keepdims=True)
        oc = o - mu
        var = jnp.mean(oc * oc, axis=-1, keepdims=True)
        on = oc * lax.rsqrt(var + NORM_EPS)
        g = g_ref[sl, :].astype(F32)
        o_ref[sl, :] = (on * (g * _sigmoid(g))).astype(BF16)


def _retention(proj, tables, batch, seq):
    n = proj.shape[0]
    c = min(RET_CHUNK, seq)
    cos, sin, dm, qd, kd, cd = tables
    col = BRANCH_WIDTH * 3 // LANES
    step = BRANCH_WIDTH // LANES

    def pspec(k):
        return pl.BlockSpec((seq, LANES), lambda b, h: (b, col + k * step + h))

    return pl.pallas_call(
        functools.partial(_ret_kernel, seq=seq, c=c),
        grid=(batch, RET_HEADS),
        in_specs=[
            pspec(0), pspec(1), pspec(2), pspec(3),
            pl.BlockSpec((seq, LANES), lambda b, h: (0, 0)),
            pl.BlockSpec((seq, LANES), lambda b, h: (0, 0)),
            pl.BlockSpec((1, c, c), lambda b, h: (h, 0, 0)),
            pl.BlockSpec((1, c, LANES), lambda b, h: (h, 0, 0)),
            pl.BlockSpec((1, c, LANES), lambda b, h: (h, 0, 0)),
            pl.BlockSpec((1, LANES, LANES), lambda b, h: (h, 0, 0)),
        ],
        out_specs=pl.BlockSpec((seq, LANES), lambda b, h: (b, h)),
        out_shape=jax.ShapeDtypeStruct((n, BRANCH_WIDTH), BF16),
        compiler_params=_cparams(("arbitrary", "arbitrary")),
        name="retention",
    )(proj, proj, proj, proj, cos, sin, dm, qd, kd, cd)


def _retention_tables(seq):
    c = min(RET_CHUNK, seq)
    d = RET_HEAD_DIM
    inv = ROPE_BASE ** (-jnp.arange(0, d, 2, dtype=F32) / d)
    ang = jnp.arange(seq).astype(F32)[:, None] * inv[None, :]
    cos = jnp.cos(ang)
    sin = jnp.sin(ang)
    cos_full = jnp.concatenate([cos, cos], axis=-1)
    sin_signed = jnp.concatenate([-sin, sin], axis=-1)
    log_gamma = jnp.log1p(-jnp.exp2(-RET_GAMMA_EXP0 - jnp.arange(RET_HEADS, dtype=F32)))
    idx = jnp.arange(c, dtype=F32)
    dist = idx[:, None] - idx[None, :]
    dm = jnp.where(dist >= 0, jnp.exp(log_gamma[:, None, None] * jnp.maximum(dist, 0.0)), 0.0)
    qd = jnp.exp(log_gamma[:, None] * (idx + 1.0))
    kd = jnp.exp(log_gamma[:, None] * (c - 1.0 - idx))
    cd = jnp.exp(log_gamma * c)
    qd = jnp.broadcast_to(qd[:, :, None], (RET_HEADS, c, LANES))
    kd = jnp.broadcast_to(kd[:, :, None], (RET_HEADS, c, LANES))
    cd = jnp.broadcast_to(cd[:, None, None], (RET_HEADS, LANES, LANES))
    return cos_full, sin_signed, dm, qd, kd, cd


def _merge_kernel(od_ref, or_ref, of_ref, gd_ref, gr_ref, gf_ref, gb_ref, wb_ref, wo_ref, x_ref,
                  g1_ref, n2_ref, sh_ref, sc_ref, rw_ref, rb_ref,
                  xo_ref, hp_ref, ti_ref, tw_ref, cnt_ref, cnt_scr, *, tm, sub):
    @pl.when(pl.program_id(0) == 0)
    def _():
        cnt_scr[...] = jnp.zeros(cnt_scr.shape, F32)

    lane = lax.broadcasted_iota(I32, (sub, LANES), 1)
    r_i = lax.broadcasted_iota(I32, (sub, sub), 0)
    c_i = lax.broadcasted_iota(I32, (sub, sub), 1)
    earlier = (c_i < r_i).astype(BF16)
    counts = cnt_scr[...]

    for s0 in range(0, tm, sub):
        rs = slice(s0, s0 + sub)
        merged = None
        for i, (o_ref, gate_ref) in enumerate(((od_ref, gd_ref), (or_ref, gr_ref), (of_ref, gf_ref))):
            gate = _sigmoid(gate_ref[rs, :] + gb_ref[i].astype(BF16))
            term = gate.astype(F32) * jnp.dot(o_ref[rs, :], wb_ref[i], preferred_element_type=F32)
            merged = term if merged is None else merged + term
        mix = jnp.dot(merged.astype(BF16), wo_ref[...], preferred_element_type=F32)
        xn = x_ref[rs, :] + g1_ref[0] * mix
        xo_ref[rs, :] = xn

        r = lax.rsqrt(jnp.mean(xn * xn, axis=-1, keepdims=True) + NORM_EPS)
        h = xn * r * n2_ref[...]
        h = h * (1.0 + sc_ref[0]) + sh_ref[0]
        hp = _pack_pairs(h)
        hp_ref[0, rs, :] = hp[:, :QUART]
        hp_ref[1, rs, :] = hp[:, QUART:]

        hh, hm, _ = _split3(h)
        logits = (jnp.dot(hh, rw_ref[0], preferred_element_type=F32)
                  + jnp.dot(hh, rw_ref[1], preferred_element_type=F32)
                  + jnp.dot(hm, rw_ref[0], preferred_element_type=F32)) + rb_ref[...]

        vals, idxs = [], []
        for _ in range(TOP_K):
            m = jnp.max(logits, axis=-1, keepdims=True)
            idx = jnp.min(jnp.where(logits == m, lane, LANES), axis=-1, keepdims=True)
            vals.append(m)
            idxs.append(idx)
            logits = jnp.where(lane == idx, -3.0e38, logits)
        es = [jnp.exp(v - vals[0]) for v in vals]
        den = es[0] + es[1] + es[2] + es[3]

        hits = [lane == idxs[k] for k in range(TOP_K)]
        picked = (hits[0] | hits[1] | hits[2] | hits[3])
        base = jnp.dot(earlier, picked.astype(BF16), preferred_element_type=F32) + counts
        counts = counts + jnp.sum(picked.astype(F32), axis=0, keepdims=True)

        ti = jnp.zeros((sub, LANES), I32)
        tw = jnp.zeros((sub, LANES), F32)
        for k in range(TOP_K):
            rank = jnp.sum(jnp.where(hits[k], base, 0.0), axis=-1, keepdims=True).astype(I32)
            ti = jnp.where(lane == k, idxs[k], ti)
            ti = jnp.where(lane == TOP_K + k, rank, ti)
            tw = jnp.where(lane == k, es[k] / den, tw)
        ti_ref[rs, :] = ti
        tw_ref[rs, :] = tw

    cnt_scr[...] = counts
    cnt_ref[...] = counts.astype(I32)


def _merge(o_diff, o_ret, o_fox, proj, gate_b, w_branch, w_out, x2, g1, n2g, sh2, sc2, rw, rb, seq):
    n, d = x2.shape
    tm = min(MERGE_TILE, seq)
    sub = min(MERGE_SUB, tm)
    per_seq = seq // tm
    gcol = GATE_COL0 // d

    def rows(w):
        return pl.BlockSpec((tm, w), lambda i: (i, 0))

    def gspec(k):
        return pl.BlockSpec((tm, d), lambda i: (i, gcol + k))

    def whole(shape):
        return pl.BlockSpec(shape, lambda i: (0,) * len(shape))

    def per_b():
        return pl.BlockSpec((1, 1, d), lambda i: (i // per_seq, 0, 0))

    return pl.pallas_call(
        functools.partial(_merge_kernel, tm=tm, sub=sub),
        grid=(n // tm,),
        in_specs=[
            rows(BRANCH_WIDTH), rows(BRANCH_WIDTH), rows(BRANCH_WIDTH),
            gspec(0), gspec(1), gspec(2),
            whole((N_BRANCHES, 1, d)), whole((N_BRANCHES, BRANCH_WIDTH, d)), whole((d, d)),
            rows(d), per_b(), whole((1, d)), per_b(), per_b(),
            whole((2, d, LANES)), whole((1, LANES)),
        ],
        out_specs=[rows(d), pl.BlockSpec((2, tm, QUART), lambda i: (0, i, 0)), rows(LANES), rows(LANES),
                   whole((1, LANES))],
        out_shape=[
            jax.ShapeDtypeStruct((n, d), F32),
            jax.ShapeDtypeStruct((2, n, QUART), I32),
            jax.ShapeDtypeStruct((n, LANES), I32),
            jax.ShapeDtypeStruct((n, LANES), F32),
            jax.ShapeDtypeStruct((1, LANES), I32),
        ],
        scratch_shapes=[pltpu.VMEM((1, LANES), F32)],
        compiler_params=_cparams(("arbitrary",)),
        name="merge_outproj_router",
    )(o_diff, o_ret, o_fox, proj, proj, proj, gate_b, w_branch, w_out, x2, g1, n2g, sh2, sc2, rw, rb)


def _prep_up_kernel(w_ref, o_ref, t_scr):
    f = t_scr.shape[1] // 2
    for g in range(t_scr.shape[0]):
        cols = slice(g * LANES, (g + 1) * LANES)
        t_scr[g] = w_ref[0, 0, cols, :].T
        o_ref[0, 0:f, cols] = t_scr[g, pl.ds(0, f, stride=2), :].astype(BF16)
        o_ref[0, f:, cols] = t_scr[g, pl.ds(1, f, stride=2), :].astype(BF16)


def _prep_up(w_up, layer):
    _, e, d, f2 = w_up.shape
    groups = 4
    kt = groups * LANES
    return pl.pallas_call(
        _prep_up_kernel,
        grid=(e, d // kt),
        in_specs=[pl.BlockSpec((1, 1, kt, f2), lambda i, j: (layer, i, j, 0))],
        out_specs=pl.BlockSpec((1, f2, kt), lambda i, j: (i, 0, j)),
        out_shape=jax.ShapeDtypeStruct((e, f2, d), BF16),
        scratch_shapes=[pltpu.VMEM((groups, f2, LANES), F32)],
        compiler_params=_cparams(("arbitrary", "arbitrary")),
        name="expert_up_weight_prep",
    )(w_up)


def _cast_kernel(w_ref, o_ref):
    o_ref[0] = w_ref[0, 0].astype(BF16)


def _prep_down(w_down, layer):
    _, e, f, d = w_down.shape
    return pl.pallas_call(
        _cast_kernel,
        grid=(e,),
        in_specs=[pl.BlockSpec((1, 1, f, d), lambda i: (layer, i, 0, 0))],
        out_specs=pl.BlockSpec((1, f, d), lambda i: (i, 0, 0)),
        out_shape=jax.ShapeDtypeStruct((e, f, d), BF16),
        compiler_params=_cparams(("arbitrary",)),
        name="expert_down_weight_cast",
    )(w_down)


def _expert_kernel(te_ref, nu_ref, xs_ref, *refs):
    y_ref = refs[-1]
    nt = (((1,), (1,)), ((), ()))
    first_tile = pl.program_id(0) * EXPERT_GROUP

    @pl.when(first_tile < nu_ref[0])
    def _():
        for s in range(EXPERT_GROUP):
            wu_ref, bu_ref, wd_ref, bd_ref = refs[4 * s:4 * s + 4]
            rs = slice(s * EXPERT_TILE, (s + 1) * EXPERT_TILE)
            lo, hi = _unpack_pairs(jnp.concatenate([xs_ref[0, rs, :], xs_ref[1, rs, :]], axis=1))
            lo = lo.astype(BF16)
            hi = hi.astype(BF16)

            def up_half(r0):
                return (lax.dot_general(lo, wu_ref[0, r0:r0 + D_FF, :HALF], nt, preferred_element_type=F32)
                        + lax.dot_general(hi, wu_ref[0, r0:r0 + D_FF, HALF:], nt, preferred_element_type=F32))

            glu = jnp.minimum(up_half(0) + bu_ref[0, :, :D_FF], SWIGLU_LIMIT)
            lin = jnp.clip(up_half(D_FF) + bu_ref[0, :, D_FF:], -SWIGLU_LIMIT, SWIGLU_LIMIT)
            act = glu * _sigmoid(SWIGLU_ALPHA * glu) * (lin + 1.0)
            down = jnp.dot(act.astype(BF16), wd_ref[0], preferred_element_type=F32) + bd_ref[0]
            yp = _pack_pairs(down)
            y_ref[0, rs, :] = yp[:, :QUART]
            y_ref[1, rs, :] = yp[:, QUART:]

    @pl.when(first_tile >= nu_ref[0])
    def _():
        y_ref[...] = jnp.zeros(y_ref.shape, I32)


def _experts(tile_e, n_used, xs, w_up, b_up, w_down, b_down):
    r = xs.shape[1]
    tm = EXPERT_TILE * EXPERT_GROUP
    d = D_MODEL
    in_specs = [pl.BlockSpec((2, tm, QUART), lambda i, te, nu: (0, i, 0))]
    args = [xs]
    for s in range(EXPERT_GROUP):
        def pick(i, te, nu, s=s):
            return (te[i * EXPERT_GROUP + s], 0, 0)
        in_specs += [pl.BlockSpec((1, 2 * D_FF, d), pick), pl.BlockSpec((1, 1, 2 * D_FF), pick),
                     pl.BlockSpec((1, D_FF, d), pick), pl.BlockSpec((1, 1, d), pick)]
        args += [w_up, b_up, w_down, b_down]
    grid_spec = pltpu.PrefetchScalarGridSpec(
        num_scalar_prefetch=2,
        grid=(r // tm,),
        in_specs=in_specs,
        out_specs=pl.BlockSpec((2, tm, QUART), lambda i, te, nu: (0, i, 0)),
    )
    return pl.pallas_call(
        _expert_kernel,
        grid_spec=grid_spec,
        out_shape=jax.ShapeDtypeStruct((2, r, QUART), I32),
        compiler_params=_cparams(("arbitrary",)),
        name="expert_mlp",
    )(tile_e, n_used, *args)


def _combine_kernel(y_ref, tw_ref, x_ref, g2_ref, fg_ref, o_ref, *, final):
    tw = tw_ref[...]
    parts = [None] * 4
    for k in range(TOP_K):
        w = tw[:, k:k + 1]
        for piece in range(2):
            lo, hi = _unpack_pairs(y_ref[piece, k])
            for slot, v in ((piece, lo), (2 + piece, hi)):
                parts[slot] = w * v if parts[slot] is None else parts[slot] + w * v
    ffn = jnp.concatenate(parts, axis=-1)
    xn = x_ref[...] + g2_ref[0] * ffn
    if final:
        r = lax.rsqrt(jnp.mean(xn * xn, axis=-1, keepdims=True) + NORM_EPS)
        xn = xn * r * fg_ref[...]
    o_ref[...] = xn


def _combine(ytok, top_w, x2, g2, final_g, seq, final):
    n, d = x2.shape
    tm = min(ROW_TILE_OUT, seq)
    per_seq = seq // tm
    return pl.pallas_call(
        functools.partial(_combine_kernel, final=final),
        grid=(n // tm,),
        in_specs=[
            pl.BlockSpec((2, TOP_K, tm, QUART), lambda i: (0, 0, i, 0)),
            pl.BlockSpec((tm, LANES), lambda i: (i, 0)),
            pl.BlockSpec((tm, d), lambda i: (i, 0)),
            pl.BlockSpec((1, 1, d), lambda i: (i // per_seq, 0, 0)),
            pl.BlockSpec((1, d), lambda i: (0, 0)),
        ],
        out_specs=pl.BlockSpec((tm, d), lambda i: (i, 0)),
        out_shape=jax.ShapeDtypeStruct((n, d), F32),
        compiler_params=_cparams(("arbitrary",)),
        name="moe_combine",
    )(ytok, top_w, x2, g2, final_g)


def _route(top_idx, counts, n_rows):
    tm = EXPERT_TILE
    e = top_idx[:, :TOP_K]
    rank = top_idx[:, TOP_K:2 * TOP_K]
    sizes = counts[0, :N_EXPERTS]
    padded = ((sizes + tm - 1) // tm) * tm
    pend = jnp.cumsum(padded)
    pstart = pend - padded
    onehot = e[:, :, None] == jnp.arange(N_EXPERTS, dtype=I32)[None, None, :]
    pos = jnp.sum(jnp.where(onehot, pstart[None, None, :], 0), axis=-1) + rank
    tiles = jnp.arange(n_rows // tm, dtype=I32) * tm
    tile_e = jnp.minimum(jnp.sum(tiles[:, None] >= pend[None, :], axis=-1), N_EXPERTS - 1).astype(I32)
    n_used = (pend[-1] // tm).astype(I32).reshape(1)
    return pos.T.astype(I32), tile_e, n_used


def _sc_gather(table, idx):
    m = idx.shape[0]
    width = table.shape[1]
    assert m % (SC_WINDOW * SC_WORKERS) == 0
    mesh = plsc.VectorSubcoreMesh(core_axis_name="c", subcore_axis_name="s")

    @pl.kernel(out_type=jax.ShapeDtypeStruct((m, width), table.dtype), mesh=mesh)
    def gather_rows(t_hbm, i_hbm, o_hbm):
        def body(i_vmem, o_vmem):
            pltpu.sync_copy(t_hbm.at[i_vmem.at[0]], o_vmem)

        pltpu.emit_pipeline(
            body,
            grid=(m // SC_WINDOW,),
            in_specs=[pl.BlockSpec((1, SC_WINDOW), lambda i: (0, i))],
            out_specs=[pl.BlockSpec((SC_WINDOW, width), lambda i: (i, 0))],
            core_axis_name=("c", "s"),
            dimension_semantics=(pltpu.PARALLEL,),
        )(i_hbm, o_hbm)

    return gather_rows(table, idx.reshape(1, m))


def _sc_scatter(rows, idx, n_out):
    t, width = rows.shape
    picks = idx.shape[0]
    assert t % SC_WINDOW == 0
    mesh = plsc.VectorSubcoreMesh(core_axis_name="c", subcore_axis_name="s")

    @pl.kernel(out_type=jax.ShapeDtypeStruct((n_out, width), rows.dtype), mesh=mesh)
    def scatter_rows(x_hbm, i_hbm, o_hbm):
        def body(x_vmem, i_vmem):
            for k in range(picks):
                pltpu.sync_copy(x_vmem, o_hbm.at[i_vmem.at[k]])

        pltpu.emit_pipeline(
            body,
            grid=(t // SC_WINDOW,),
            in_specs=[pl.BlockSpec((SC_WINDOW, width), lambda i: (i, 0)),
                      pl.BlockSpec((picks, SC_WINDOW), lambda i: (0, i))],
            out_specs=[],
            core_axis_name=("c", "s"),
            dimension_semantics=(pltpu.PARALLEL,),
        )(x_hbm, i_hbm)

    return scatter_rows(rows, idx)


def _dispatch(h2p, pos_t, n_rows):
    n = h2p.shape[1]
    idx = jnp.concatenate([pos_t, pos_t + n_rows], axis=1)
    return _sc_scatter(h2p.reshape(2 * n, QUART), idx, 2 * n_rows).reshape(2, n_rows, QUART)


def _collect(y, pos_t):
    n_rows = y.shape[1]
    flat = pos_t.reshape(-1)
    out = _sc_gather(y.reshape(2 * n_rows, QUART), jnp.concatenate([flat, flat + n_rows]))
    return out.reshape(2, TOP_K, pos_t.shape[1], QUART)


def kernel(x, c, norm1_g, norm2_g, ada_w, ada_b, w_in, gate_b, fox_fb, lam_q1, lam_k1, lam_q2, lam_k2,
           diff_subln_g, w_branch, w_out, router_w, router_b, exp_w_up, exp_b_up, exp_w_down,
           exp_b_down, final_g):
    batch, seq, d = x.shape
    depth = ada_w.shape[0]
    n = batch * seq
    x2 = x.reshape(n, d)

    mod = _ada_mod(c, ada_w, ada_b)
    tables = _retention_tables(seq)
    n_rows = n * TOP_K + N_EXPERTS * EXPERT_TILE
    blk = BRANCH_WIDTH // LANES

    for l in range(depth):
        lam_init = 0.8 - 0.6 * math.exp(-0.3 * l)
        sh1, sc1, g1, sh2, sc2, g2 = [m.reshape(batch, 1, d) for m in jnp.split(mod[l], 6, axis=-1)]

        w_main, w_f = _prep_win(w_in, l)
        proj, flog = _inproj(x2, norm1_g[l].reshape(1, d), sh1, sc1, w_main, w_f, seq)

        fb_pad = jnp.pad(fox_fb[l], (0, LANES - FOX_HEADS)).reshape(1, LANES)
        frow = _fcum(flog, fb_pad, batch, seq).reshape(batch * blk, 2, seq)

        lamv = jnp.stack([lam_q1[l], lam_k1[l], lam_q2[l], lam_k2[l]])
        o_diff = _attention(proj, "diff", 0, blk, 2 * blk, batch, seq,
                            (lamv, diff_subln_g[l].reshape(1, LANES)), lam_init)
        o_ret = _retention(proj, tables, batch, seq)
        o_fox = _attention(proj, "fox", 7 * blk, 8 * blk, 9 * blk, batch, seq, (frow,))

        rw = jnp.pad(router_w[l], ((0, 0), (0, LANES - N_EXPERTS)))
        rw_hi = rw.astype(BF16)
        rw = jnp.stack([rw_hi, (rw - rw_hi.astype(F32)).astype(BF16)])
        rb = jnp.pad(router_b[l], (0, LANES - N_EXPERTS), constant_values=NEG_INF).reshape(1, LANES)
        x2, h2p, top_idx, top_w, counts = _merge(
            o_diff, o_ret, o_fox, proj, gate_b[l].reshape(N_BRANCHES, 1, d), w_branch[l].astype(BF16),
            w_out[l].astype(BF16), x2, g1, norm2_g[l].reshape(1, d), sh2, sc2, rw, rb, seq)

        pos_t, tile_e, n_used = _route(top_idx, counts, n_rows)
        xs = _dispatch(h2p, pos_t, n_rows)

        w_up = _prep_up(exp_w_up, l)
        bu = exp_b_up[l]
        b_up = jnp.concatenate([bu[:, 0::2], bu[:, 1::2]], axis=-1).reshape(N_EXPERTS, 1, 2 * D_FF)
        y = _experts(tile_e, n_used, xs, w_up, b_up, _prep_down(exp_w_down, l),
                     exp_b_down[l].reshape(N_EXPERTS, 1, d))
        ytok = _collect(y, pos_t)
        x2 = _combine(ytok, top_w, x2, g2, final_g.reshape(1, d), seq, final=(l == depth - 1))

    return x2.reshape(batch, seq, d)
```

```python
import functools
import math

import jax
import jax.numpy as jnp
from jax import lax
from jax.experimental import pallas as pl
from jax.experimental.pallas import tpu as pltpu
from jax.experimental.pallas import tpu_sc as plsc

F32 = jnp.float32
BF16 = jnp.bfloat16
I32 = jnp.int32
U32 = jnp.uint32

D_MODEL = 1024
HEAD_DIM = 64
BRANCH_WIDTH = D_MODEL // 2
N_BRANCHES = 3
FOX_HEADS = BRANCH_WIDTH // HEAD_DIM
RET_HEADS = 4
RET_HEAD_DIM = 128
ROPE_BASE = 10000.0
RET_GAMMA_EXP0 = 5.0
N_EXPERTS = 32
TOP_K = 4
D_FF = D_MODEL
SWIGLU_LIMIT = 7.0
SWIGLU_ALPHA = 1.702
NORM_EPS = 1e-5
NEG_INF = -1e30

LANES = 128
MAIN_COLS = 8192
GATE_COL0 = 10 * BRANCH_WIDTH
HALF = D_MODEL // 2
QUART = HALF // 2
SC_WINDOW = 128
SC_WORKERS = 32
VMEM_LIMIT = 48 * 1024 * 1024

ROW_TILE_IN = 512
COL_TILE_IN = 1024
ATTN_TILE = 256
RET_CHUNK = 256
ROW_TILE_OUT = 256
MERGE_TILE = 512
MERGE_SUB = 256
COMBINE_PARTS = 2
EXPERT_TILE = 256
EXPERT_GROUP = 2
CUM_CHUNK = 256


def _cparams(sem):
    return pltpu.CompilerParams(dimension_semantics=sem, vmem_limit_bytes=VMEM_LIMIT)


def _sigmoid(x):
    return 0.5 * jnp.tanh(0.5 * x) + 0.5


def _pack_pairs(v):
    u = lax.bitcast_convert_type(v.astype(BF16).astype(F32), U32)
    half = v.shape[-1] // 2
    w = (u[:, half:] & jnp.uint32(0xFFFF0000)) | (u[:, :half] >> 16)
    return lax.bitcast_convert_type(w, I32)


def _unpack_pairs(w):
    u = lax.bitcast_convert_type(w, U32)
    lo = lax.bitcast_convert_type(u << 16, F32)
    hi = lax.bitcast_convert_type(u & jnp.uint32(0xFFFF0000), F32)
    return lo, hi


def _ada_kernel(c_ref, w_ref, b_ref, o_ref):
    c = c_ref[...]
    ca = c * _sigmoid(c)
    o_ref[0] = jnp.dot(ca, w_ref[0], preferred_element_type=F32) + b_ref[0]


def _ada_mod(c, ada_w, ada_b):
    depth, d, cols = ada_w.shape
    b = c.shape[0]
    tn = 1024
    return pl.pallas_call(
        _ada_kernel,
        grid=(depth, cols // tn),
        in_specs=[
            pl.BlockSpec((b, d), lambda l, j: (0, 0)),
            pl.BlockSpec((1, d, tn), lambda l, j: (l, 0, j)),
            pl.BlockSpec((1, 1, tn), lambda l, j: (l, 0, j)),
        ],
        out_specs=pl.BlockSpec((1, b, tn), lambda l, j: (l, 0, j)),
        out_shape=jax.ShapeDtypeStruct((depth, b, cols), F32),
        compiler_params=_cparams(("arbitrary", "arbitrary")),
        name="ada_mod",
    )(c, ada_w, ada_b.reshape(depth, 1, cols))


def _prep_win_kernel(w_ref, wm_ref, wf_ref):
    x = w_ref[0]
    wm_ref[:, :GATE_COL0] = x[:, :GATE_COL0].astype(BF16)
    wm_ref[:, GATE_COL0:] = x[:, GATE_COL0 + FOX_HEADS:].astype(BF16)
    lane = lax.broadcasted_iota(I32, (x.shape[0], LANES), 1)
    wf_ref[...] = jnp.where(lane < FOX_HEADS, x[:, GATE_COL0:GATE_COL0 + LANES], 0.0).astype(BF16)


def _prep_win(w, layer):
    _, d, cols = w.shape
    rt = 128
    return pl.pallas_call(
        _prep_win_kernel,
        grid=(d // rt,),
        in_specs=[pl.BlockSpec((1, rt, cols), lambda i: (layer, i, 0))],
        out_specs=[pl.BlockSpec((rt, MAIN_COLS), lambda i: (i, 0)), pl.BlockSpec((rt, LANES), lambda i: (i, 0))],
        out_shape=[jax.ShapeDtypeStruct((d, MAIN_COLS), BF16), jax.ShapeDtypeStruct((d, LANES), BF16)],
        compiler_params=_cparams(("arbitrary",)),
        name="inproj_weight_prep",
    )(w)


def _inproj_kernel(x_ref, g_ref, sh_ref, sc_ref, w_ref, wf_ref, proj_ref, fl_ref):
    x = x_ref[...]
    r = lax.rsqrt(jnp.mean(x * x, axis=-1, keepdims=True) + NORM_EPS)
    h = x * r * g_ref[...]
    h = h * (1.0 + sc_ref[0]) + sh_ref[0]
    hb = h.astype(BF16)
    fl_ref[...] = jnp.dot(hb, wf_ref[...], preferred_element_type=F32)
    for j in range(MAIN_COLS // COL_TILE_IN):
        cs = slice(j * COL_TILE_IN, (j + 1) * COL_TILE_IN)
        proj_ref[:, cs] = jnp.dot(hb, w_ref[:, cs], preferred_element_type=F32).astype(BF16)


def _inproj(x2, g, sh, sc, w_main, w_f, seq):
    n, d = x2.shape
    tm = min(ROW_TILE_IN, seq)
    per_seq = seq // tm
    resident = pl.Buffered(1)
    return pl.pallas_call(
        _inproj_kernel,
        grid=(n // tm,),
        in_specs=[
            pl.BlockSpec((tm, d), lambda i: (i, 0)),
            pl.BlockSpec((1, d), lambda i: (0, 0)),
            pl.BlockSpec((1, 1, d), lambda i: (i // per_seq, 0, 0)),
            pl.BlockSpec((1, 1, d), lambda i: (i // per_seq, 0, 0)),
            pl.BlockSpec((d, MAIN_COLS), lambda i: (0, 0), pipeline_mode=resident),
            pl.BlockSpec((d, LANES), lambda i: (0, 0), pipeline_mode=resident),
        ],
        out_specs=[
            pl.BlockSpec((tm, MAIN_COLS), lambda i: (i, 0)),
            pl.BlockSpec((tm, LANES), lambda i: (i, 0)),
        ],
        out_shape=[
            jax.ShapeDtypeStruct((n, MAIN_COLS), BF16),
            jax.ShapeDtypeStruct((n, LANES), F32),
        ],
        compiler_params=_cparams(("arbitrary",)),
        name="inproj",
    )(x2, g, sh, sc, w_main, w_f)


def _split3(v):
    hi = v.astype(BF16)
    r1 = v - hi.astype(F32)
    mid = r1.astype(BF16)
    lo = (r1 - mid.astype(F32)).astype(BF16)
    return hi, mid, lo


def _fcum_kernel(fl_ref, fb_ref, o_ref, *, seq):
    c = min(CUM_CHUNK, seq)
    row = lax.broadcasted_iota(I32, (c, c), 0)
    col = lax.broadcasted_iota(I32, (c, c), 1)
    lower = (col <= row).astype(BF16)
    carry = jnp.zeros((1, LANES), F32)
    for ci in range(seq // c):
        x = fl_ref[pl.ds(ci * c, c), :] + fb_ref[...]
        lf = jnp.minimum(x, 0.0) - jnp.log(1.0 + jnp.exp(-jnp.abs(x)))
        hi, mid, lo = _split3(lf)
        cs = (jnp.dot(lower, hi, preferred_element_type=F32)
              + jnp.dot(lower, mid, preferred_element_type=F32)
              + jnp.dot(lower, lo, preferred_element_type=F32)) + carry
        o_ref[0, :, pl.ds(ci * c, c)] = cs.T[:FOX_HEADS, :]
        carry = cs[c - 1:c, :]


def _fcum(flog, fb_pad, batch, seq):
    return pl.pallas_call(
        functools.partial(_fcum_kernel, seq=seq),
        grid=(batch,),
        in_specs=[
            pl.BlockSpec((seq, LANES), lambda b: (b, 0)),
            pl.BlockSpec((1, LANES), lambda b: (0, 0)),
        ],
        out_specs=pl.BlockSpec((1, FOX_HEADS, seq), lambda b: (b, 0, 0)),
        out_shape=jax.ShapeDtypeStruct((batch, FOX_HEADS, seq), F32),
        compiler_params=_cparams(("arbitrary",)),
        name="fox_decay_cumsum",
    )(flog, fb_pad)


def _attn_kernel(*refs, mode, t, seq, lam_init):
    if mode == "diff":
        q_ref, k_ref, v_ref, lam_ref, g_ref, o_ref, vx_scr = refs
        f_ref = None
        lv = lam_ref[...]
        lam = (jnp.exp(jnp.sum(lv[0:1, :] * lv[1:2, :])) - jnp.exp(jnp.sum(lv[2:3, :] * lv[3:4, :]))
               + lam_init)
    else:
        q_ref, k_ref, v_ref, f_ref, o_ref, vx_scr = refs
    lane = lax.broadcasted_iota(I32, (t, LANES), 1)
    row = lax.broadcasted_iota(I32, (t, t), 0)
    col = lax.broadcasted_iota(I32, (t, t), 1)
    nt = (((1,), (1,)), ((), ()))

    v_all = v_ref[...]
    one = jnp.ones_like(v_all)
    if mode == "diff":
        vx_scr[0] = jnp.concatenate([v_all, one], axis=1)
    else:
        lane_s = lax.broadcasted_iota(I32, v_all.shape, 1)
        vx_scr[0] = jnp.where(lane_s < HEAD_DIM, v_all, one)
        vx_scr[1] = jnp.where(lane_s >= HEAD_DIM, v_all, one)

    for qi in range(seq // t):
        lo = qi * t
        q = q_ref[lo:lo + t, :] * jnp.asarray(HEAD_DIM ** -0.5, BF16)
        zero = jnp.zeros_like(q)
        outs = []
        for idx in range(2):
            vx = vx_scr.at[0 if mode == "diff" else idx]
            qm = jnp.where(lane < HEAD_DIM, q, zero) if idx == 0 else jnp.where(lane >= HEAD_DIM, q, zero)
            sd = lax.dot_general(qm, k_ref[lo:lo + t, :], nt, preferred_element_type=F32)
            if f_ref is not None:
                sd = sd - f_ref[0, idx:idx + 1, lo:lo + t]
            sd = jnp.where(col <= row, sd, NEG_INF)
            m = jnp.max(sd, axis=-1, keepdims=True)
            if qi > 0:
                sm = lax.dot_general(qm, k_ref[0:lo, :], nt, preferred_element_type=F32)
                if f_ref is not None:
                    sm = sm - f_ref[0, idx:idx + 1, 0:lo]
                m = jnp.maximum(m, jnp.max(sm, axis=-1, keepdims=True))
            pd = jnp.exp((sd - m).astype(BF16))
            acc = jnp.dot(pd, vx[lo:lo + t, :], preferred_element_type=F32)
            if qi > 0:
                pm = jnp.exp((sm - m).astype(BF16))
                acc = acc + jnp.dot(pm, vx[0:lo, :], preferred_element_type=F32)
            outs.append(acc)
        if mode == "diff":
            a1 = outs[0][:, :LANES] / outs[0][:, LANES:]
            a2 = outs[1][:, :LANES] / outs[1][:, LANES:]
            o = a1 - lam * a2
            r = lax.rsqrt(jnp.mean(o * o, axis=-1, keepdims=True) + NORM_EPS)
            o = o * r * g_ref[...] * (1.0 - lam_init)
        else:
            num = jnp.where(lane < HEAD_DIM, outs[0], outs[1])
            den = jnp.where(lane < HEAD_DIM, pltpu.roll(outs[0], HEAD_DIM, 1), pltpu.roll(outs[1], HEAD_DIM, 1))
            o = num / den
        o_ref[lo:lo + t, :] = o.astype(BF16)


def _attention(proj, mode, col_q, col_k, col_v, batch, seq, extra, lam_init=0.0):
    n = proj.shape[0]
    t = min(ATTN_TILE, seq)
    n_blk = BRANCH_WIDTH // LANES
    in_specs = [
        pl.BlockSpec((seq, LANES), lambda b, h: (b, col_q + h)),
        pl.BlockSpec((seq, LANES), lambda b, h: (b, col_k + h)),
        pl.BlockSpec((seq, LANES), lambda b, h: (b, col_v + h)),
    ]
    if mode == "diff":
        lamv, g = extra
        in_specs += [pl.BlockSpec((4, HEAD_DIM), lambda b, h: (0, 0)),
                     pl.BlockSpec((1, LANES), lambda b, h: (0, 0))]
        args = (proj, proj, proj, lamv, g)
    else:
        (frow,) = extra
        in_specs += [pl.BlockSpec((1, 2, seq), lambda b, h: (b * n_blk + h, 0, 0))]
        args = (proj, proj, proj, frow)
    return pl.pallas_call(
        functools.partial(_attn_kernel, mode=mode, t=t, seq=seq, lam_init=lam_init),
        grid=(batch, n_blk),
        in_specs=in_specs,
        out_specs=pl.BlockSpec((seq, LANES), lambda b, h: (b, h)),
        out_shape=jax.ShapeDtypeStruct((n, BRANCH_WIDTH), BF16),
        scratch_shapes=[pltpu.VMEM((1, seq, 2 * LANES) if mode == "diff" else (2, seq, LANES), BF16)],
        compiler_params=_cparams(("arbitrary", "arbitrary")),
        name="attn_" + mode,
    )(*args)


def _ret_kernel(q_ref, k_ref, v_ref, g_ref, cos_ref, sin_ref, dm_ref, qd_ref, kd_ref, cd_ref,
                o_ref, *, seq, c):
    st = jnp.zeros((RET_HEAD_DIM, RET_HEAD_DIM), F32)
    half = RET_HEAD_DIM // 2

    for ci in range(seq // c):
        sl = pl.ds(ci * c, c)
        cos = cos_ref[sl, :]
        sin = sin_ref[sl, :]
        q = q_ref[sl, :].astype(F32)
        k = k_ref[sl, :].astype(F32)
        qr = q * cos + pltpu.roll(q, half, 1) * sin
        kr = (k * cos + pltpu.roll(k, half, 1) * sin) * (RET_HEAD_DIM ** -0.5)
        v = v_ref[sl, :]
        sc = lax.dot_general(qr.astype(BF16), kr.astype(BF16), (((1,), (1,)), ((), ())),
                             preferred_element_type=F32) * dm_ref[0]
        inner = jnp.dot(sc.astype(BF16), v, preferred_element_type=F32)
        cross = jnp.dot((qr * qd_ref[0]).astype(BF16), st.astype(BF16), preferred_element_type=F32)
        o = inner + cross
        kt = (kr * kd_ref[0]).T.astype(BF16)
        st = st * cd_ref[0] + jnp.dot(kt, v, preferred_element_type=F32)
        mu = jnp.mean(o, axis=-1, keepdims=True)
        oc = o - mu
        var = jnp.mean(oc * oc, axis=-1, keepdims=True)
        on = oc * lax.rsqrt(var + NORM_EPS)
        g = g_ref[sl, :].astype(F32)
        o_ref[sl, :] = (on * (g * _sigmoid(g))).astype(BF16)


def _retention(proj, tables, batch, seq):
    n = proj.shape[0]
    c = min(RET_CHUNK, seq)
    cos, sin, dm, qd, kd, cd = tables
    col = BRANCH_WIDTH * 3 // LANES
    step = BRANCH_WIDTH // LANES

    def pspec(k):
        return pl.BlockSpec((seq, LANES), lambda b, h: (b, col + k * step + h))

    return pl.pallas_call(
        functools.partial(_ret_kernel, seq=seq, c=c),
        grid=(batch, RET_HEADS),
        in_specs=[
            pspec(0), pspec(1), pspec(2), pspec(3),
            pl.BlockSpec((seq, LANES), lambda b, h: (0, 0)),
            pl.BlockSpec((seq, LANES), lambda b, h: (0, 0)),
            pl.BlockSpec((1, c, c), lambda b, h: (h, 0, 0)),
            pl.BlockSpec((1, c, LANES), lambda b, h: (h, 0, 0)),
            pl.BlockSpec((1, c, LANES), lambda b, h: (h, 0, 0)),
            pl.BlockSpec((1, LANES, LANES), lambda b, h: (h, 0, 0)),
        ],
        out_specs=pl.BlockSpec((seq, LANES), lambda b, h: (b, h)),
        out_shape=jax.ShapeDtypeStruct((n, BRANCH_WIDTH), BF16),
        compiler_params=_cparams(("arbitrary", "arbitrary")),
        name="retention",
    )(proj, proj, proj, proj, cos, sin, dm, qd, kd, cd)


def _retention_tables(seq):
    c = min(RET_CHUNK, seq)
    d = RET_HEAD_DIM
    inv = ROPE_BASE ** (-jnp.arange(0, d, 2, dtype=F32) / d)
    ang = jnp.arange(seq).astype(F32)[:, None] * inv[None, :]
    cos = jnp.cos(ang)
    sin = jnp.sin(ang)
    cos_full = jnp.concatenate([cos, cos], axis=-1)
    sin_signed = jnp.concatenate([-sin, sin], axis=-1)
    log_gamma = jnp.log1p(-jnp.exp2(-RET_GAMMA_EXP0 - jnp.arange(RET_HEADS, dtype=F32)))
    idx = jnp.arange(c, dtype=F32)
    dist = idx[:, None] - idx[None, :]
    dm = jnp.where(dist >= 0, jnp.exp(log_gamma[:, None, None] * jnp.maximum(dist, 0.0)), 0.0)
    qd = jnp.exp(log_gamma[:, None] * (idx + 1.0))
    kd = jnp.exp(log_gamma[:, None] * (c - 1.0 - idx))
    cd = jnp.exp(log_gamma * c)
    qd = jnp.broadcast_to(qd[:, :, None], (RET_HEADS, c, LANES))
    kd = jnp.broadcast_to(kd[:, :, None], (RET_HEADS, c, LANES))
    cd = jnp.broadcast_to(cd[:, None, None], (RET_HEADS, LANES, LANES))
    return cos_full, sin_signed, dm, qd, kd, cd


def _merge_kernel(od_ref, or_ref, of_ref, gd_ref, gr_ref, gf_ref, gb_ref, wb_ref, wo_ref, x_ref,
                  g1_ref, n2_ref, sh_ref, sc_ref, rw_ref, rb_ref,
                  xo_ref, hp_ref, ti_ref, tw_ref, cnt_ref, cnt_scr, *, tm, sub):
    @pl.when(pl.program_id(0) == 0)
    def _():
        cnt_scr[...] = jnp.zeros(cnt_scr.shape, F32)

    lane = lax.broadcasted_iota(I32, (sub, LANES), 1)
    r_i = lax.broadcasted_iota(I32, (sub, sub), 0)
    c_i = lax.broadcasted_iota(I32, (sub, sub), 1)
    earlier = (c_i < r_i).astype(BF16)
    counts = cnt_scr[...]

    for s0 in range(0, tm, sub):
        rs = slice(s0, s0 + sub)
        merged = None
        for i, (o_ref, gate_ref) in enumerate(((od_ref, gd_ref), (or_ref, gr_ref), (of_ref, gf_ref))):
            gate = _sigmoid(gate_ref[rs, :] + gb_ref[i].astype(BF16))
            term = gate * jnp.dot(o_ref[rs, :], wb_ref[i], preferred_element_type=F32).astype(BF16)
            merged = term if merged is None else merged + term
        mix = jnp.dot(merged, wo_ref[...], preferred_element_type=F32)
        xn = x_ref[rs, :] + g1_ref[0] * mix
        xo_ref[rs, :] = xn

        r = lax.rsqrt(jnp.mean(xn * xn, axis=-1, keepdims=True) + NORM_EPS)
        h = xn * r * n2_ref[...]
        h = h * (1.0 + sc_ref[0]) + sh_ref[0]
        hp = _pack_pairs(h)
        hp_ref[0, rs, :] = hp[:, :QUART]
        hp_ref[1, rs, :] = hp[:, QUART:]

        hh, hm, _ = _split3(h)
        logits = (jnp.dot(hh, rw_ref[0], preferred_element_type=F32)
                  + jnp.dot(hh, rw_ref[1], preferred_element_type=F32)
                  + jnp.dot(hm, rw_ref[0], preferred_element_type=F32)) + rb_ref[...]

        vals, idxs = [], []
        for _ in range(TOP_K):
            m = jnp.max(logits, axis=-1, keepdims=True)
            idx = jnp.min(jnp.where(logits == m, lane, LANES), axis=-1, keepdims=True)
            vals.append(m)
            idxs.append(idx)
            logits = jnp.where(lane == idx, -3.0e38, logits)
        es = [jnp.exp(v - vals[0]) for v in vals]
        den = es[0] + es[1] + es[2] + es[3]

        hits = [lane == idxs[k] for k in range(TOP_K)]
        picked = (hits[0] | hits[1] | hits[2] | hits[3])
        base = jnp.dot(earlier, picked.astype(BF16), preferred_element_type=F32) + counts
        counts = counts + jnp.sum(picked.astype(F32), axis=0, keepdims=True)

        ti = jnp.zeros((sub, LANES), I32)
        tw = jnp.zeros((sub, LANES), F32)
        for k in range(TOP_K):
            rank = jnp.sum(jnp.where(hits[k], base, 0.0), axis=-1, keepdims=True).astype(I32)
            ti = jnp.where(lane == k, idxs[k], ti)
            ti = jnp.where(lane == TOP_K + k, rank, ti)
            tw = jnp.where(lane == k, es[k] / den, tw)
        ti_ref[rs, :] = ti
        tw_ref[rs, :] = tw

    cnt_scr[...] = counts
    cnt_ref[...] = counts.astype(I32)


def _merge(o_diff, o_ret, o_fox, proj, gate_b, w_branch, w_out, x2, g1, n2g, sh2, sc2, rw, rb, seq):
    n, d = x2.shape
    tm = min(MERGE_TILE, seq)
    sub = min(MERGE_SUB, tm)
    per_seq = seq // tm
    gcol = GATE_COL0 // d

    def rows(w):
        return pl.BlockSpec((tm, w), lambda i: (i, 0))

    def gspec(k):
        return pl.BlockSpec((tm, d), lambda i: (i, gcol + k))

    def whole(shape):
        return pl.BlockSpec(shape, lambda i: (0,) * len(shape))

    def per_b():
        return pl.BlockSpec((1, 1, d), lambda i: (i // per_seq, 0, 0))

    return pl.pallas_call(
        functools.partial(_merge_kernel, tm=tm, sub=sub),
        grid=(n // tm,),
        in_specs=[
            rows(BRANCH_WIDTH), rows(BRANCH_WIDTH), rows(BRANCH_WIDTH),
            gspec(0), gspec(1), gspec(2),
            whole((N_BRANCHES, 1, d)), whole((N_BRANCHES, BRANCH_WIDTH, d)), whole((d, d)),
            rows(d), per_b(), whole((1, d)), per_b(), per_b(),
            whole((2, d, LANES)), whole((1, LANES)),
        ],
        out_specs=[rows(d), pl.BlockSpec((2, tm, QUART), lambda i: (0, i, 0)), rows(LANES), rows(LANES),
                   whole((1, LANES))],
        out_shape=[
            jax.ShapeDtypeStruct((n, d), F32),
            jax.ShapeDtypeStruct((2, n, QUART), I32),
            jax.ShapeDtypeStruct((n, LANES), I32),
            jax.ShapeDtypeStruct((n, LANES), F32),
            jax.ShapeDtypeStruct((1, LANES), I32),
        ],
        scratch_shapes=[pltpu.VMEM((1, LANES), F32)],
        compiler_params=_cparams(("arbitrary",)),
        name="merge_outproj_router",
    )(o_diff, o_ret, o_fox, proj, proj, proj, gate_b, w_branch, w_out, x2, g1, n2g, sh2, sc2, rw, rb)


def _prep_up_kernel(w_ref, o_ref, t_scr):
    f = t_scr.shape[1] // 2
    for g in range(t_scr.shape[0]):
        cols = slice(g * LANES, (g + 1) * LANES)
        t_scr[g] = w_ref[0, 0, cols, :].T
        o_ref[0, 0:f, cols] = t_scr[g, pl.ds(0, f, stride=2), :].astype(BF16)
        o_ref[0, f:, cols] = t_scr[g, pl.ds(1, f, stride=2), :].astype(BF16)


def _prep_up(w_up, layer):
    _, e, d, f2 = w_up.shape
    groups = d // LANES
    kt = groups * LANES
    return pl.pallas_call(
        _prep_up_kernel,
        grid=(e, d // kt),
        in_specs=[pl.BlockSpec((1, 1, kt, f2), lambda i, j: (layer, i, j, 0))],
        out_specs=pl.BlockSpec((1, f2, kt), lambda i, j: (i, 0, j)),
        out_shape=jax.ShapeDtypeStruct((e, f2, d), BF16),
        scratch_shapes=[pltpu.VMEM((groups, f2, LANES), F32)],
        compiler_params=_cparams(("arbitrary", "arbitrary")),
        name="expert_up_weight_prep",
    )(w_up)


def _cast_kernel(w_ref, o_ref):
    o_ref[0] = w_ref[0, 0].astype(BF16)


def _prep_down(w_down, layer):
    _, e, f, d = w_down.shape
    return pl.pallas_call(
        _cast_kernel,
        grid=(e,),
        in_specs=[pl.BlockSpec((1, 1, f, d), lambda i: (layer, i, 0, 0))],
        out_specs=pl.BlockSpec((1, f, d), lambda i: (i, 0, 0)),
        out_shape=jax.ShapeDtypeStruct((e, f, d), BF16),
        compiler_params=_cparams(("arbitrary",)),
        name="expert_down_weight_cast",
    )(w_down)


def _expert_kernel(te_ref, nu_ref, xs_ref, *refs):
    y_ref = refs[-1]
    nt = (((1,), (1,)), ((), ()))
    first_tile = pl.program_id(0) * EXPERT_GROUP

    @pl.when(first_tile < nu_ref[0])
    def _():
        for s in range(EXPERT_GROUP):
            wu_ref, bu_ref, wd_ref, bd_ref = refs[4 * s:4 * s + 4]
            rs = slice(s * EXPERT_TILE, (s + 1) * EXPERT_TILE)
            lo, hi = _unpack_pairs(jnp.concatenate([xs_ref[0, rs, :], xs_ref[1, rs, :]], axis=1))
            lo = lo.astype(BF16)
            hi = hi.astype(BF16)

            def up_half(r0):
                return (lax.dot_general(lo, wu_ref[0, r0:r0 + D_FF, :HALF], nt, preferred_element_type=F32)
                        + lax.dot_general(hi, wu_ref[0, r0:r0 + D_FF, HALF:], nt, preferred_element_type=F32))

            glu = jnp.minimum(up_half(0) + bu_ref[0, :, :D_FF], SWIGLU_LIMIT)
            lin = jnp.clip(up_half(D_FF) + bu_ref[0, :, D_FF:], -SWIGLU_LIMIT, SWIGLU_LIMIT)
            act = glu * _sigmoid(SWIGLU_ALPHA * glu) * (lin + 1.0)
            down = jnp.dot(act.astype(BF16), wd_ref[0], preferred_element_type=F32) + bd_ref[0]
            yp = _pack_pairs(down)
            y_ref[0, rs, :] = yp[:, :QUART]
            y_ref[1, rs, :] = yp[:, QUART:]

    @pl.when(first_tile >= nu_ref[0])
    def _():
        y_ref[...] = jnp.zeros(y_ref.shape, I32)


def _experts(tile_e, n_used, xs, w_up, b_up, w_down, b_down):
    r = xs.shape[1]
    tm = EXPERT_TILE * EXPERT_GROUP
    d = D_MODEL
    in_specs = [pl.BlockSpec((2, tm, QUART), lambda i, te, nu: (0, i, 0))]
    args = [xs]
    for s in range(EXPERT_GROUP):
        def pick(i, te, nu, s=s):
            return (te[i * EXPERT_GROUP + s], 0, 0)
        in_specs += [pl.BlockSpec((1, 2 * D_FF, d), pick), pl.BlockSpec((1, 1, 2 * D_FF), pick),
                     pl.BlockSpec((1, D_FF, d), pick), pl.BlockSpec((1, 1, d), pick)]
        args += [w_up, b_up, w_down, b_down]
    grid_spec = pltpu.PrefetchScalarGridSpec(
        num_scalar_prefetch=2,
        grid=(r // tm,),
        in_specs=in_specs,
        out_specs=pl.BlockSpec((2, tm, QUART), lambda i, te, nu: (0, i, 0)),
    )
    return pl.pallas_call(
        _expert_kernel,
        grid_spec=grid_spec,
        out_shape=jax.ShapeDtypeStruct((2, r, QUART), I32),
        compiler_params=_cparams(("arbitrary",)),
        name="expert_mlp",
    )(tile_e, n_used, *args)


def _combine_kernel(y_ref, tw_ref, x_ref, g2_ref, fg_ref, *rest, final):
    o_ref = rest[-1]
    tw = tw_ref[...]
    parts = [None] * 4
    for k in range(TOP_K):
        w = tw[:, k:k + 1]
        for piece in range(2):
            lo, hi = _unpack_pairs(y_ref[piece, k])
            for slot, v in ((piece, lo), (2 + piece, hi)):
                parts[slot] = w * v if parts[slot] is None else parts[slot] + w * v
    ffn = jnp.concatenate(parts, axis=-1)
    xn = x_ref[...] + g2_ref[0] * ffn
    if final:
        r = lax.rsqrt(jnp.mean(xn * xn, axis=-1, keepdims=True) + NORM_EPS)
        xn = xn * r * fg_ref[...]
    o_ref[...] = xn


def _combine(ytok, top_w, x2, g2, final_g, seq, final, part, parts, prev):
    n, d = x2.shape
    tm = min(ROW_TILE_OUT, seq)
    per_seq = seq // tm
    steps = n // tm // parts
    off = part * steps
    in_specs = [
        pl.BlockSpec((2, TOP_K, tm, QUART), lambda i: (0, 0, i, 0)),
        pl.BlockSpec((tm, LANES), lambda i: (i + off, 0)),
        pl.BlockSpec((tm, d), lambda i: (i + off, 0)),
        pl.BlockSpec((1, 1, d), lambda i: ((i + off) // per_seq, 0, 0)),
        pl.BlockSpec((1, d), lambda i: (0, 0)),
    ]
    args = [ytok, top_w, x2, g2, final_g]
    aliases = {}
    if prev is not None:
        in_specs.append(pl.BlockSpec(memory_space=pl.ANY))
        args.append(prev)
        aliases = {len(args) - 1: 0}
    return pl.pallas_call(
        functools.partial(_combine_kernel, final=final),
        grid=(steps,),
        in_specs=in_specs,
        out_specs=pl.BlockSpec((tm, d), lambda i: (i + off, 0)),
        out_shape=jax.ShapeDtypeStruct((n, d), F32),
        input_output_aliases=aliases,
        compiler_params=_cparams(("arbitrary",)),
        name="moe_combine",
    )(*args)


def _route(top_idx, counts, n_rows):
    tm = EXPERT_TILE
    e = top_idx[:, :TOP_K]
    rank = top_idx[:, TOP_K:2 * TOP_K]
    sizes = counts[0, :N_EXPERTS]
    padded = ((sizes + tm - 1) // tm) * tm
    pend = jnp.cumsum(padded)
    pstart = pend - padded
    onehot = e[:, :, None] == jnp.arange(N_EXPERTS, dtype=I32)[None, None, :]
    pos = jnp.sum(jnp.where(onehot, pstart[None, None, :], 0), axis=-1) + rank
    tiles = jnp.arange(n_rows // tm, dtype=I32) * tm
    tile_e = jnp.minimum(jnp.sum(tiles[:, None] >= pend[None, :], axis=-1), N_EXPERTS - 1).astype(I32)
    n_used = (pend[-1] // tm).astype(I32).reshape(1)
    return pos.T.astype(I32), tile_e, n_used


def _sc_gather(table, idx):
    m = idx.shape[0]
    width = table.shape[1]
    assert m % (SC_WINDOW * SC_WORKERS) == 0
    mesh = plsc.VectorSubcoreMesh(core_axis_name="c", subcore_axis_name="s")

    @pl.kernel(out_type=jax.ShapeDtypeStruct((m, width), table.dtype), mesh=mesh)
    def gather_rows(t_hbm, i_hbm, o_hbm):
        def body(i_vmem, o_vmem):
            pltpu.sync_copy(t_hbm.at[i_vmem.at[0]], o_vmem)

        pltpu.emit_pipeline(
            body,
            grid=(m // SC_WINDOW,),
            in_specs=[pl.BlockSpec((1, SC_WINDOW), lambda i: (0, i))],
            out_specs=[pl.BlockSpec((SC_WINDOW, width), lambda i: (i, 0))],
            core_axis_name=("c", "s"),
            dimension_semantics=(pltpu.PARALLEL,),
        )(i_hbm, o_hbm)

    return gather_rows(table, idx.reshape(1, m))


def _sc_scatter(rows, idx, n_out):
    t, width = rows.shape
    picks = idx.shape[0]
    assert t % SC_WINDOW == 0
    mesh = plsc.VectorSubcoreMesh(core_axis_name="c", subcore_axis_name="s")

    @pl.kernel(out_type=jax.ShapeDtypeStruct((n_out, width), rows.dtype), mesh=mesh)
    def scatter_rows(x_hbm, i_hbm, o_hbm):
        def body(x_vmem, i_vmem):
            for k in range(picks):
                pltpu.sync_copy(x_vmem, o_hbm.at[i_vmem.at[k]])

        pltpu.emit_pipeline(
            body,
            grid=(t // SC_WINDOW,),
            in_specs=[pl.BlockSpec((SC_WINDOW, width), lambda i: (i, 0)),
                      pl.BlockSpec((picks, SC_WINDOW), lambda i: (0, i))],
            out_specs=[],
            core_axis_name=("c", "s"),
            dimension_semantics=(pltpu.PARALLEL,),
        )(x_hbm, i_hbm)

    return scatter_rows(rows, idx)


def _dispatch(h2p, pos_t, n_rows):
    n = h2p.shape[1]
    idx = jnp.concatenate([pos_t, pos_t + n_rows], axis=1)
    return _sc_scatter(h2p.reshape(2 * n, QUART), idx, 2 * n_rows).reshape(2, n_rows, QUART)


def _collect(y, pos_t):
    n_rows = y.shape[1]
    flat = pos_t.reshape(-1)
    out = _sc_gather(y.reshape(2 * n_rows, QUART), jnp.concatenate([flat, flat + n_rows]))
    return out.reshape(2, TOP_K, pos_t.shape[1], QUART)


def kernel(x, c, norm1_g, norm2_g, ada_w, ada_b, w_in, gate_b, fox_fb, lam_q1, lam_k1, lam_q2, lam_k2,
           diff_subln_g, w_branch, w_out, router_w, router_b, exp_w_up, exp_b_up, exp_w_down,
           exp_b_down, final_g):
    batch, seq, d = x.shape
    depth = ada_w.shape[0]
    n = batch * seq
    x2 = x.reshape(n, d)

    mod = _ada_mod(c, ada_w, ada_b)
    tables = _retention_tables(seq)
    n_rows = n * TOP_K + N_EXPERTS * EXPERT_TILE
    blk = BRANCH_WIDTH // LANES

    for l in range(depth):
        lam_init = 0.8 - 0.6 * math.exp(-0.3 * l)
        sh1, sc1, g1, sh2, sc2, g2 = [m.reshape(batch, 1, d) for m in jnp.split(mod[l], 6, axis=-1)]

        w_main, w_f = _prep_win(w_in, l)
        proj, flog = _inproj(x2, norm1_g[l].reshape(1, d), sh1, sc1, w_main, w_f, seq)

        fb_pad = jnp.pad(fox_fb[l], (0, LANES - FOX_HEADS)).reshape(1, LANES)
        frow = _fcum(flog, fb_pad, batch, seq).reshape(batch * blk, 2, seq)

        lamv = jnp.stack([lam_q1[l], lam_k1[l], lam_q2[l], lam_k2[l]])
        o_diff = _attention(proj, "diff", 0, blk, 2 * blk, batch, seq,
                            (lamv, diff_subln_g[l].reshape(1, LANES)), lam_init)
        o_ret = _retention(proj, tables, batch, seq)
        o_fox = _attention(proj, "fox", 7 * blk, 8 * blk, 9 * blk, batch, seq, (frow,))

        rw = jnp.pad(router_w[l], ((0, 0), (0, LANES - N_EXPERTS)))
        rw_hi = rw.astype(BF16)
        rw = jnp.stack([rw_hi, (rw - rw_hi.astype(F32)).astype(BF16)])
        rb = jnp.pad(router_b[l], (0, LANES - N_EXPERTS), constant_values=NEG_INF).reshape(1, LANES)
        x2, h2p, top_idx, top_w, counts = _merge(
            o_diff, o_ret, o_fox, proj, gate_b[l].reshape(N_BRANCHES, 1, d), w_branch[l].astype(BF16),
            w_out[l].astype(BF16), x2, g1, norm2_g[l].reshape(1, d), sh2, sc2, rw, rb, seq)

        pos_t, tile_e, n_used = _route(top_idx, counts, n_rows)
        xs = _dispatch(h2p, pos_t, n_rows)

        w_up = _prep_up(exp_w_up, l)
        bu = exp_b_up[l]
        b_up = jnp.concatenate([bu[:, 0::2], bu[:, 1::2]], axis=-1).reshape(N_EXPERTS, 1, 2 * D_FF)
        y = _experts(tile_e, n_used, xs, w_up, b_up, _prep_down(exp_w_down, l),
                     exp_b_down[l].reshape(N_EXPERTS, 1, d))
        x_in, x2 = x2, None
        for part in range(COMBINE_PARTS):
            span = n // COMBINE_PARTS
            ytok = _collect(y, pos_t[:, part * span:(part + 1) * span])
            x2 = _combine(ytok, top_w, x_in, g2, final_g.reshape(1, d), seq, l == depth - 1,
                          part, COMBINE_PARTS, x2)

    return x2.reshape(batch, seq, d)
```

```python
import functools
import math

import jax
import jax.numpy as jnp
from jax import lax
from jax.experimental import pallas as pl
from jax.experimental.pallas import tpu as pltpu
from jax.experimental.pallas import tpu_sc as plsc

F32 = jnp.float32
BF16 = jnp.bfloat16
I32 = jnp.int32
U32 = jnp.uint32

D_MODEL = 1024
HEAD_DIM = 64
BRANCH_WIDTH = D_MODEL // 2
N_BRANCHES = 3
FOX_HEADS = BRANCH_WIDTH // HEAD_DIM
RET_HEADS = 4
RET_HEAD_DIM = 128
ROPE_BASE = 10000.0
RET_GAMMA_EXP0 = 5.0
N_EXPERTS = 32
TOP_K = 4
D_FF = D_MODEL
SWIGLU_LIMIT = 7.0
SWIGLU_ALPHA = 1.702
NORM_EPS = 1e-5
NEG_INF = -1e30

LANES = 128
MAIN_COLS = 8192
GATE_COL0 = 10 * BRANCH_WIDTH
HALF = D_MODEL // 2
QUART = HALF // 2
SC_WINDOW = 128
SC_WORKERS = 32
VMEM_LIMIT = 48 * 1024 * 1024

ROW_TILE_IN = 512
COL_TILE_IN = 1024
ATTN_TILE = 256
RET_CHUNK = 256
ROW_TILE_OUT = 256
MERGE_TILE = 1024
MERGE_SUB = 256
EXPERT_TILE = 256
EXPERT_GROUP = 2
CUM_CHUNK = 256


def _cparams(sem):
    return pltpu.CompilerParams(dimension_semantics=sem, vmem_limit_bytes=VMEM_LIMIT)


def _sigmoid(x):
    return 0.5 * jnp.tanh(0.5 * x) + 0.5


def _pack_pairs(v):
    u = lax.bitcast_convert_type(v.astype(BF16).astype(F32), U32)
    half = v.shape[-1] // 2
    w = (u[:, half:] & jnp.uint32(0xFFFF0000)) | (u[:, :half] >> 16)
    return lax.bitcast_convert_type(w, I32)


def _unpack_pairs(w):
    u = lax.bitcast_convert_type(w, U32)
    lo = lax.bitcast_convert_type(u << 16, F32)
    hi = lax.bitcast_convert_type(u & jnp.uint32(0xFFFF0000), F32)
    return lo, hi


def _ada_kernel(c_ref, w_ref, b_ref, o_ref):
    c = c_ref[...]
    ca = c * _sigmoid(c)
    o_ref[0] = jnp.dot(ca, w_ref[0], preferred_element_type=F32) + b_ref[0]


def _ada_mod(c, ada_w, ada_b):
    depth, d, cols = ada_w.shape
    b = c.shape[0]
    tn = 1024
    return pl.pallas_call(
        _ada_kernel,
        grid=(depth, cols // tn),
        in_specs=[
            pl.BlockSpec((b, d), lambda l, j: (0, 0)),
            pl.BlockSpec((1, d, tn), lambda l, j: (l, 0, j)),
            pl.BlockSpec((1, 1, tn), lambda l, j: (l, 0, j)),
        ],
        out_specs=pl.BlockSpec((1, b, tn), lambda l, j: (l, 0, j)),
        out_shape=jax.ShapeDtypeStruct((depth, b, cols), F32),
        compiler_params=_cparams(("arbitrary", "arbitrary")),
        name="ada_mod",
    )(c, ada_w, ada_b.reshape(depth, 1, cols))


def _prep_win_kernel(w_ref, wm_ref, wf_ref):
    x = w_ref[0]
    wm_ref[:, :GATE_COL0] = x[:, :GATE_COL0].astype(BF16)
    wm_ref[:, GATE_COL0:] = x[:, GATE_COL0 + FOX_HEADS:].astype(BF16)
    lane = lax.broadcasted_iota(I32, (x.shape[0], LANES), 1)
    wf_ref[...] = jnp.where(lane < FOX_HEADS, x[:, GATE_COL0:GATE_COL0 + LANES], 0.0).astype(BF16)


def _prep_win(w, layer):
    _, d, cols = w.shape
    rt = 128
    return pl.pallas_call(
        _prep_win_kernel,
        grid=(d // rt,),
        in_specs=[pl.BlockSpec((1, rt, cols), lambda i: (layer, i, 0))],
        out_specs=[pl.BlockSpec((rt, MAIN_COLS), lambda i: (i, 0)), pl.BlockSpec((rt, LANES), lambda i: (i, 0))],
        out_shape=[jax.ShapeDtypeStruct((d, MAIN_COLS), BF16), jax.ShapeDtypeStruct((d, LANES), BF16)],
        compiler_params=_cparams(("arbitrary",)),
        name="inproj_weight_prep",
    )(w)


def _inproj_kernel(x_ref, g_ref, sh_ref, sc_ref, w_ref, wf_ref, proj_ref, fl_ref):
    x = x_ref[...]
    r = lax.rsqrt(jnp.mean(x * x, axis=-1, keepdims=True) + NORM_EPS)
    h = x * r * g_ref[...]
    h = h * (1.0 + sc_ref[0]) + sh_ref[0]
    hb = h.astype(BF16)
    fl_ref[...] = jnp.dot(hb, wf_ref[...], preferred_element_type=F32)
    for j in range(MAIN_COLS // COL_TILE_IN):
        cs = slice(j * COL_TILE_IN, (j + 1) * COL_TILE_IN)
        proj_ref[:, cs] = jnp.dot(hb, w_ref[:, cs], preferred_element_type=F32).astype(BF16)


def _inproj(x2, g, sh, sc, w_main, w_f, seq):
    n, d = x2.shape
    tm = min(ROW_TILE_IN, seq)
    per_seq = seq // tm
    resident = pl.Buffered(1)
    return pl.pallas_call(
        _inproj_kernel,
        grid=(n // tm,),
        in_specs=[
            pl.BlockSpec((tm, d), lambda i: (i, 0)),
            pl.BlockSpec((1, d), lambda i: (0, 0)),
            pl.BlockSpec((1, 1, d), lambda i: (i // per_seq, 0, 0)),
            pl.BlockSpec((1, 1, d), lambda i: (i // per_seq, 0, 0)),
            pl.BlockSpec((d, MAIN_COLS), lambda i: (0, 0), pipeline_mode=resident),
            pl.BlockSpec((d, LANES), lambda i: (0, 0), pipeline_mode=resident),
        ],
        out_specs=[
            pl.BlockSpec((tm, MAIN_COLS), lambda i: (i, 0)),
            pl.BlockSpec((tm, LANES), lambda i: (i, 0)),
        ],
        out_shape=[
            jax.ShapeDtypeStruct((n, MAIN_COLS), BF16),
            jax.ShapeDtypeStruct((n, LANES), F32),
        ],
        compiler_params=_cparams(("arbitrary",)),
        name="inproj",
    )(x2, g, sh, sc, w_main, w_f)


def _split3(v):
    hi = v.astype(BF16)
    r1 = v - hi.astype(F32)
    mid = r1.astype(BF16)
    lo = (r1 - mid.astype(F32)).astype(BF16)
    return hi, mid, lo


def _fcum_kernel(fl_ref, fb_ref, o_ref, *, seq):
    c = min(CUM_CHUNK, seq)
    row = lax.broadcasted_iota(I32, (c, c), 0)
    col = lax.broadcasted_iota(I32, (c, c), 1)
    lower = (col <= row).astype(BF16)
    carry = jnp.zeros((1, LANES), F32)
    for ci in range(seq // c):
        x = fl_ref[pl.ds(ci * c, c), :] + fb_ref[...]
        lf = jnp.minimum(x, 0.0) - jnp.log(1.0 + jnp.exp(-jnp.abs(x)))
        hi, mid, lo = _split3(lf)
        cs = (jnp.dot(lower, hi, preferred_element_type=F32)
              + jnp.dot(lower, mid, preferred_element_type=F32)
              + jnp.dot(lower, lo, preferred_element_type=F32)) + carry
        o_ref[0, :, pl.ds(ci * c, c)] = cs.T[:FOX_HEADS, :]
        carry = cs[c - 1:c, :]


def _fcum(flog, fb_pad, batch, seq):
    return pl.pallas_call(
        functools.partial(_fcum_kernel, seq=seq),
        grid=(batch,),
        in_specs=[
            pl.BlockSpec((seq, LANES), lambda b: (b, 0)),
            pl.BlockSpec((1, LANES), lambda b: (0, 0)),
        ],
        out_specs=pl.BlockSpec((1, FOX_HEADS, seq), lambda b: (b, 0, 0)),
        out_shape=jax.ShapeDtypeStruct((batch, FOX_HEADS, seq), F32),
        compiler_params=_cparams(("arbitrary",)),
        name="fox_decay_cumsum",
    )(flog, fb_pad)


def _attn_kernel(*refs, mode, t, seq, lam_init):
    if mode == "diff":
        q_ref, k_ref, v_ref, lam_ref, g_ref, o_ref, vx_scr = refs
        f_ref = None
        lv = lam_ref[...]
        lam = (jnp.exp(jnp.sum(lv[0:1, :] * lv[1:2, :])) - jnp.exp(jnp.sum(lv[2:3, :] * lv[3:4, :]))
               + lam_init)
    else:
        q_ref, k_ref, v_ref, f_ref, o_ref, vx_scr = refs
    lane = lax.broadcasted_iota(I32, (t, LANES), 1)
    row = lax.broadcasted_iota(I32, (t, t), 0)
    col = lax.broadcasted_iota(I32, (t, t), 1)
    nt = (((1,), (1,)), ((), ()))

    v_all = v_ref[...]
    one = jnp.ones_like(v_all)
    if mode == "diff":
        vx_scr[0] = jnp.concatenate([v_all, one], axis=1)
    else:
        lane_s = lax.broadcasted_iota(I32, v_all.shape, 1)
        vx_scr[0] = jnp.where(lane_s < HEAD_DIM, v_all, one)
        vx_scr[1] = jnp.where(lane_s >= HEAD_DIM, v_all, one)

    n_strips = seq // t
    pairs = [(i, n_strips - 1 - i) for i in range(n_strips // 2)] if n_strips % 2 == 0 else [(i,) for i in range(n_strips)]
    for group in pairs:
        units = [(qi, idx) for qi in group for idx in range(2)]
        qms = {}
        for qi in group:
            q = q_ref[qi * t:(qi + 1) * t, :] * jnp.asarray(HEAD_DIM ** -0.5, BF16)
            zero = jnp.zeros_like(q)
            qms[qi, 0] = jnp.where(lane < HEAD_DIM, q, zero)
            qms[qi, 1] = jnp.where(lane >= HEAD_DIM, q, zero)

        def scores(u, k0, k1):
            s_ = lax.dot_general(qms[u], k_ref[k0:k1, :], nt, preferred_element_type=F32)
            if f_ref is not None:
                s_ = s_ - f_ref[0, u[1]:u[1] + 1, k0:k1]
            return s_

        sds = {u: jnp.where(col <= row, scores(u, u[0] * t, (u[0] + 1) * t), NEG_INF) for u in units}
        sms = {u: scores(u, 0, u[0] * t) for u in units if u[0] > 0}
        ms = {}
        for u in units:
            m = jnp.max(sds[u], axis=-1, keepdims=True)
            if u in sms:
                m = jnp.maximum(m, jnp.max(sms[u], axis=-1, keepdims=True))
            ms[u] = m
        pds = {u: jnp.exp((sds[u] - ms[u]).astype(BF16)) for u in units}
        pms = {u: jnp.exp((sms[u] - ms[u]).astype(BF16)) for u in sms}
        outs = {}
        for u in units:
            vx = vx_scr.at[0 if mode == "diff" else u[1]]
            lo = u[0] * t
            acc = jnp.dot(pds[u], vx[lo:lo + t, :], preferred_element_type=F32)
            if u in pms:
                acc = acc + jnp.dot(pms[u], vx[0:lo, :], preferred_element_type=F32)
            outs[u] = acc
        for qi in group:
            o0, o1 = outs[qi, 0], outs[qi, 1]
            if mode == "diff":
                a1 = o0[:, :LANES] / o0[:, LANES:]
                a2 = o1[:, :LANES] / o1[:, LANES:]
                o = a1 - lam * a2
                r = lax.rsqrt(jnp.mean(o * o, axis=-1, keepdims=True) + NORM_EPS)
                o = o * r * g_ref[...] * (1.0 - lam_init)
            else:
                num = jnp.where(lane < HEAD_DIM, o0, o1)
                den = jnp.where(lane < HEAD_DIM, pltpu.roll(o0, HEAD_DIM, 1), pltpu.roll(o1, HEAD_DIM, 1))
                o = num / den
            o_ref[qi * t:(qi + 1) * t, :] = o.astype(BF16)


def _attention(proj, mode, col_q, col_k, col_v, batch, seq, extra, lam_init=0.0):
    n = proj.shape[0]
    t = min(ATTN_TILE, seq)
    n_blk = BRANCH_WIDTH // LANES
    in_specs = [
        pl.BlockSpec((seq, LANES), lambda b, h: (b, col_q + h)),
        pl.BlockSpec((seq, LANES), lambda b, h: (b, col_k + h)),
        pl.BlockSpec((seq, LANES), lambda b, h: (b, col_v + h)),
    ]
    if mode == "diff":
        lamv, g = extra
        in_specs += [pl.BlockSpec((4, HEAD_DIM), lambda b, h: (0, 0)),
                     pl.BlockSpec((1, LANES), lambda b, h: (0, 0))]
        args = (proj, proj, proj, lamv, g)
    else:
        (frow,) = extra
        in_specs += [pl.BlockSpec((1, 2, seq), lambda b, h: (b * n_blk + h, 0, 0))]
        args = (proj, proj, proj, frow)
    return pl.pallas_call(
        functools.partial(_attn_kernel, mode=mode, t=t, seq=seq, lam_init=lam_init),
        grid=(batch, n_blk),
        in_specs=in_specs,
        out_specs=pl.BlockSpec((seq, LANES), lambda b, h: (b, h)),
        out_shape=jax.ShapeDtypeStruct((n, BRANCH_WIDTH), BF16),
        scratch_shapes=[pltpu.VMEM((1, seq, 2 * LANES) if mode == "diff" else (2, seq, LANES), BF16)],
        compiler_params=_cparams(("arbitrary", "arbitrary")),
        name="attn_" + mode,
    )(*args)


def _ret_kernel(q_ref, k_ref, v_ref, g_ref, cos_ref, sin_ref, dm_ref, qd_ref, kd_ref, cd_ref,
                o_ref, *, seq, c):
    st = jnp.zeros((RET_HEAD_DIM, RET_HEAD_DIM), F32)
    half = RET_HEAD_DIM // 2

    for ci in range(seq // c):
        sl = pl.ds(ci * c, c)
        cos = cos_ref[sl, :]
        sin = sin_ref[sl, :]
        q = q_ref[sl, :].astype(F32)
        k = k_ref[sl, :].astype(F32)
        qr = q * cos + pltpu.roll(q, half, 1) * sin
        kr = (k * cos + pltpu.roll(k, half, 1) * sin) * (RET_HEAD_DIM ** -0.5)
        v = v_ref[sl, :]
        sc = lax.dot_general(qr.astype(BF16), kr.astype(BF16), (((1,), (1,)), ((), ())),
                             preferred_element_type=F32) * dm_ref[0]
        inner = jnp.dot(sc.astype(BF16), v, preferred_element_type=F32)
        cross = jnp.dot((qr * qd_ref[0]).astype(BF16), st.astype(BF16), preferred_element_type=F32)
        o = inner + cross
        kt = (kr * kd_ref[0]).T.astype(BF16)
        st = st * cd_ref[0] + jnp.dot(kt, v, preferred_element_type=F32)
        mu = jnp.mean(o, axis=-1, keepdims=True)
        oc = o - mu
        var = jnp.mean(oc * oc, axis=-1, keepdims=True)
        on = oc * lax.rsqrt(var + NORM_EPS)
        g = g_ref[sl, :].astype(F32)
        o_ref[sl, :] = (on * (g * _sigmoid(g))).astype(BF16)


def _retention(proj, tables, batch, seq):
    n = proj.shape[0]
    c = min(RET_CHUNK, seq)
    cos, sin, dm, qd, kd, cd = tables
    col = BRANCH_WIDTH * 3 // LANES
    step = BRANCH_WIDTH // LANES

    def pspec(k):
        return pl.BlockSpec((seq, LANES), lambda b, h: (b, col + k * step + h))

    return pl.pallas_call(
        functools.partial(_ret_kernel, seq=seq, c=c),
        grid=(batch, RET_HEADS),
        in_specs=[
            pspec(0), pspec(1), pspec(2), pspec(3),
            pl.BlockSpec((seq, LANES), lambda b, h: (0, 0)),
            pl.BlockSpec((seq, LANES), lambda b, h: (0, 0)),
            pl.BlockSpec((1, c, c), lambda b, h: (h, 0, 0)),
            pl.BlockSpec((1, c, LANES), lambda b, h: (h, 0, 0)),
            pl.BlockSpec((1, c, LANES), lambda b, h: (h, 0, 0)),
            pl.BlockSpec((1, LANES, LANES), lambda b, h: (h, 0, 0)),
        ],
        out_specs=pl.BlockSpec((seq, LANES), lambda b, h: (b, h)),
        out_shape=jax.ShapeDtypeStruct((n, BRANCH_WIDTH), BF16),
        compiler_params=_cparams(("arbitrary", "arbitrary")),
        name="retention",
    )(proj, proj, proj, proj, cos, sin, dm, qd, kd, cd)


def _retention_tables(seq):
    c = min(RET_CHUNK, seq)
    d = RET_HEAD_DIM
    inv = ROPE_BASE ** (-jnp.arange(0, d, 2, dtype=F32) / d)
    ang = jnp.arange(seq).astype(F32)[:, None] * inv[None, :]
    cos = jnp.cos(ang)
    sin = jnp.sin(ang)
    cos_full = jnp.concatenate([cos, cos], axis=-1)
    sin_signed = jnp.concatenate([-sin, sin], axis=-1)
    log_gamma = jnp.log1p(-jnp.exp2(-RET_GAMMA_EXP0 - jnp.arange(RET_HEADS, dtype=F32)))
    idx = jnp.arange(c, dtype=F32)
    dist = idx[:, None] - idx[None, :]
    dm = jnp.where(dist >= 0, jnp.exp(log_gamma[:, None, None] * jnp.maximum(dist, 0.0)), 0.0)
    qd = jnp.exp(log_gamma[:, None] * (idx + 1.0))
    kd = jnp.exp(log_gamma[:, None] * (c - 1.0 - idx))
    cd = jnp.exp(log_gamma * c)
    qd = jnp.broadcast_to(qd[:, :, None], (RET_HEADS, c, LANES))
    kd = jnp.broadcast_to(kd[:, :, None], (RET_HEADS, c, LANES))
    cd = jnp.broadcast_to(cd[:, None, None], (RET_HEADS, LANES, LANES))
    return cos_full, sin_signed, dm, qd, kd, cd


def _merge_kernel(od_ref, or_ref, of_ref, gd_ref, gr_ref, gf_ref, gb_ref, wb_ref, wo_ref, x_ref,
                  g1_ref, n2_ref, sh_ref, sc_ref, rw_ref, rb_ref,
                  xo_ref, hp_ref, ti_ref, tw_ref, cnt_ref, cnt_scr, *, tm, sub):
    @pl.when(pl.program_id(0) == 0)
    def _():
        cnt_scr[...] = jnp.zeros(cnt_scr.shape, F32)

    lane = lax.broadcasted_iota(I32, (sub, LANES), 1)
    r_i = lax.broadcasted_iota(I32, (sub, sub), 0)
    c_i = lax.broadcasted_iota(I32, (sub, sub), 1)
    earlier = (c_i < r_i).astype(BF16)
    counts = cnt_scr[...]

    blocks = [slice(s0, s0 + sub) for s0 in range(0, tm, sub)]

    def phase_branches(rs):
        merged = None
        for i, (o_ref, gate_ref) in enumerate(((od_ref, gd_ref), (or_ref, gr_ref), (of_ref, gf_ref))):
            gate = _sigmoid(gate_ref[rs, :] + gb_ref[i].astype(BF16))
            term = gate.astype(F32) * jnp.dot(o_ref[rs, :], wb_ref[i], preferred_element_type=F32)
            merged = term if merged is None else merged + term
        return merged.astype(BF16)

    def phase_outproj(rs, merged):
        mix = jnp.dot(merged, wo_ref[...], preferred_element_type=F32)
        xn = x_ref[rs, :] + g1_ref[0] * mix
        xo_ref[rs, :] = xn
        r = lax.rsqrt(jnp.mean(xn * xn, axis=-1, keepdims=True) + NORM_EPS)
        h = xn * r * n2_ref[...]
        h = h * (1.0 + sc_ref[0]) + sh_ref[0]
        hp = _pack_pairs(h)
        hp_ref[0, rs, :] = hp[:, :QUART]
        hp_ref[1, rs, :] = hp[:, QUART:]
        hh, hm, _ = _split3(h)
        return (jnp.dot(hh, rw_ref[0], preferred_element_type=F32)
                + jnp.dot(hh, rw_ref[1], preferred_element_type=F32)
                + jnp.dot(hm, rw_ref[0], preferred_element_type=F32)) + rb_ref[...]

    def phase_topk(rs, logits, counts):
        vals, idxs = [], []
        for _ in range(TOP_K):
            m = jnp.max(logits, axis=-1, keepdims=True)
            idx = jnp.min(jnp.where(logits == m, lane, LANES), axis=-1, keepdims=True)
            vals.append(m)
            idxs.append(idx)
            logits = jnp.where(lane == idx, -3.0e38, logits)
        es = [jnp.exp(v - vals[0]) for v in vals]
        den = es[0] + es[1] + es[2] + es[3]
        hits = [lane == idxs[k] for k in range(TOP_K)]
        picked = (hits[0] | hits[1] | hits[2] | hits[3])
        base = jnp.dot(earlier, picked.astype(BF16), preferred_element_type=F32) + counts
        counts = counts + jnp.sum(picked.astype(F32), axis=0, keepdims=True)
        ti = jnp.zeros((sub, LANES), I32)
        tw = jnp.zeros((sub, LANES), F32)
        for k in range(TOP_K):
            rank = jnp.sum(jnp.where(hits[k], base, 0.0), axis=-1, keepdims=True).astype(I32)
            ti = jnp.where(lane == k, idxs[k], ti)
            ti = jnp.where(lane == TOP_K + k, rank, ti)
            tw = jnp.where(lane == k, es[k] / den, tw)
        ti_ref[rs, :] = ti
        tw_ref[rs, :] = tw
        return counts

    merged = [phase_branches(rs) for rs in blocks]
    logits = [phase_outproj(rs, m) for rs, m in zip(blocks, merged)]
    for rs, lg in zip(blocks, logits):
        counts = phase_topk(rs, lg, counts)

    cnt_scr[...] = counts
    cnt_ref[...] = counts.astype(I32)


def _merge(o_diff, o_ret, o_fox, proj, gate_b, w_branch, w_out, x2, g1, n2g, sh2, sc2, rw, rb, seq):
    n, d = x2.shape
    tm = min(MERGE_TILE, seq)
    sub = min(MERGE_SUB, tm)
    per_seq = seq // tm
    gcol = GATE_COL0 // d

    def rows(w):
        return pl.BlockSpec((tm, w), lambda i: (i, 0))

    def gspec(k):
        return pl.BlockSpec((tm, d), lambda i: (i, gcol + k))

    def whole(shape):
        return pl.BlockSpec(shape, lambda i: (0,) * len(shape))

    def per_b():
        return pl.BlockSpec((1, 1, d), lambda i: (i // per_seq, 0, 0))

    return pl.pallas_call(
        functools.partial(_merge_kernel, tm=tm, sub=sub),
        grid=(n // tm,),
        in_specs=[
            rows(BRANCH_WIDTH), rows(BRANCH_WIDTH), rows(BRANCH_WIDTH),
            gspec(0), gspec(1), gspec(2),
            whole((N_BRANCHES, 1, d)), whole((N_BRANCHES, BRANCH_WIDTH, d)), whole((d, d)),
            rows(d), per_b(), whole((1, d)), per_b(), per_b(),
            whole((2, d, LANES)), whole((1, LANES)),
        ],
        out_specs=[rows(d), pl.BlockSpec((2, tm, QUART), lambda i: (0, i, 0)), rows(LANES), rows(LANES),
                   whole((1, LANES))],
        out_shape=[
            jax.ShapeDtypeStruct((n, d), F32),
            jax.ShapeDtypeStruct((2, n, QUART), I32),
            jax.ShapeDtypeStruct((n, LANES), I32),
            jax.ShapeDtypeStruct((n, LANES), F32),
            jax.ShapeDtypeStruct((1, LANES), I32),
        ],
        scratch_shapes=[pltpu.VMEM((1, LANES), F32)],
        compiler_params=_cparams(("arbitrary",)),
        name="merge_outproj_router",
    )(o_diff, o_ret, o_fox, proj, proj, proj, gate_b, w_branch, w_out, x2, g1, n2g, sh2, sc2, rw, rb)


def _prep_up_kernel(w_ref, o_ref, t_scr):
    f = t_scr.shape[1] // 2
    for g in range(t_scr.shape[0]):
        cols = slice(g * LANES, (g + 1) * LANES)
        t_scr[g] = w_ref[0, 0, cols, :].T
        o_ref[0, 0:f, cols] = t_scr[g, pl.ds(0, f, stride=2), :].astype(BF16)
        o_ref[0, f:, cols] = t_scr[g, pl.ds(1, f, stride=2), :].astype(BF16)


def _prep_up(w_up, layer):
    _, e, d, f2 = w_up.shape
    groups = d // LANES
    kt = groups * LANES
    return pl.pallas_call(
        _prep_up_kernel,
        grid=(e, d // kt),
        in_specs=[pl.BlockSpec((1, 1, kt, f2), lambda i, j: (layer, i, j, 0))],
        out_specs=pl.BlockSpec((1, f2, kt), lambda i, j: (i, 0, j)),
        out_shape=jax.ShapeDtypeStruct((e, f2, d), BF16),
        scratch_shapes=[pltpu.VMEM((groups, f2, LANES), F32)],
        compiler_params=_cparams(("arbitrary", "arbitrary")),
        name="expert_up_weight_prep",
    )(w_up)


def _cast_kernel(w_ref, o_ref):
    o_ref[0] = w_ref[0, 0].astype(BF16)


def _prep_down(w_down, layer):
    _, e, f, d = w_down.shape
    return pl.pallas_call(
        _cast_kernel,
        grid=(e,),
        in_specs=[pl.BlockSpec((1, 1, f, d), lambda i: (layer, i, 0, 0))],
        out_specs=pl.BlockSpec((1, f, d), lambda i: (i, 0, 0)),
        out_shape=jax.ShapeDtypeStruct((e, f, d), BF16),
        compiler_params=_cparams(("arbitrary",)),
        name="expert_down_weight_cast",
    )(w_down)


def _expert_kernel(te_ref, nu_ref, xs_ref, *refs):
    y_ref = refs[-1]
    nt = (((1,), (1,)), ((), ()))
    first_tile = pl.program_id(0) * EXPERT_GROUP

    @pl.when(first_tile < nu_ref[0])
    def _():
        for s in range(EXPERT_GROUP):
            wu_ref, bu_ref, wd_ref, bd_ref = refs[4 * s:4 * s + 4]
            rs = slice(s * EXPERT_TILE, (s + 1) * EXPERT_TILE)
            lo, hi = _unpack_pairs(jnp.concatenate([xs_ref[0, rs, :], xs_ref[1, rs, :]], axis=1))
            lo = lo.astype(BF16)
            hi = hi.astype(BF16)

            def up_half(r0):
                return (lax.dot_general(lo, wu_ref[0, r0:r0 + D_FF, :HALF], nt, preferred_element_type=F32)
                        + lax.dot_general(hi, wu_ref[0, r0:r0 + D_FF, HALF:], nt, preferred_element_type=F32))

            glu = jnp.minimum(up_half(0) + bu_ref[0, :, :D_FF], SWIGLU_LIMIT)
            lin = jnp.clip(up_half(D_FF) + bu_ref[0, :, D_FF:], -SWIGLU_LIMIT, SWIGLU_LIMIT)
            act = glu * _sigmoid(SWIGLU_ALPHA * glu) * (lin + 1.0)
            down = jnp.dot(act.astype(BF16), wd_ref[0], preferred_element_type=F32) + bd_ref[0]
            yp = _pack_pairs(down)
            y_ref[0, rs, :] = yp[:, :QUART]
            y_ref[1, rs, :] = yp[:, QUART:]

    @pl.when(first_tile >= nu_ref[0])
    def _():
        y_ref[...] = jnp.zeros(y_ref.shape, I32)


def _experts(tile_e, n_used, xs, w_up, b_up, w_down, b_down):
    r = xs.shape[1]
    tm = EXPERT_TILE * EXPERT_GROUP
    d = D_MODEL
    in_specs = [pl.BlockSpec((2, tm, QUART), lambda i, te, nu: (0, i, 0))]
    args = [xs]
    for s in range(EXPERT_GROUP):
        def pick(i, te, nu, s=s):
            return (te[i * EXPERT_GROUP + s], 0, 0)
        in_specs += [pl.BlockSpec((1, 2 * D_FF, d), pick), pl.BlockSpec((1, 1, 2 * D_FF), pick),
                     pl.BlockSpec((1, D_FF, d), pick), pl.BlockSpec((1, 1, d), pick)]
        args += [w_up, b_up, w_down, b_down]
    grid_spec = pltpu.PrefetchScalarGridSpec(
        num_scalar_prefetch=2,
        grid=(r // tm,),
        in_specs=in_specs,
        out_specs=pl.BlockSpec((2, tm, QUART), lambda i, te, nu: (0, i, 0)),
    )
    return pl.pallas_call(
        _expert_kernel,
        grid_spec=grid_spec,
        out_shape=jax.ShapeDtypeStruct((2, r, QUART), I32),
        compiler_params=_cparams(("arbitrary",)),
        name="expert_mlp",
    )(tile_e, n_used, *args)


def _combine_kernel(y_ref, tw_ref, x_ref, g2_ref, fg_ref, o_ref, *, final):
    tw = tw_ref[...]
    parts = [None] * 4
    for k in range(TOP_K):
        w = tw[:, k:k + 1]
        for piece in range(2):
            lo, hi = _unpack_pairs(y_ref[piece, k])
            for slot, v in ((piece, lo), (2 + piece, hi)):
                parts[slot] = w * v if parts[slot] is None else parts[slot] + w * v
    ffn = jnp.concatenate(parts, axis=-1)
    xn = x_ref[...] + g2_ref[0] * ffn
    if final:
        r = lax.rsqrt(jnp.mean(xn * xn, axis=-1, keepdims=True) + NORM_EPS)
        xn = xn * r * fg_ref[...]
    o_ref[...] = xn


def _combine(ytok, top_w, x2, g2, final_g, seq, final):
    n, d = x2.shape
    tm = min(ROW_TILE_OUT, seq)
    per_seq = seq // tm
    return pl.pallas_call(
        functools.partial(_combine_kernel, final=final),
        grid=(n // tm,),
        in_specs=[
            pl.BlockSpec((2, TOP_K, tm, QUART), lambda i: (0, 0, i, 0)),
            pl.BlockSpec((tm, LANES), lambda i: (i, 0)),
            pl.BlockSpec((tm, d), lambda i: (i, 0)),
            pl.BlockSpec((1, 1, d), lambda i: (i // per_seq, 0, 0)),
            pl.BlockSpec((1, d), lambda i: (0, 0)),
        ],
        out_specs=pl.BlockSpec((tm, d), lambda i: (i, 0)),
        out_shape=jax.ShapeDtypeStruct((n, d), F32),
        compiler_params=_cparams(("arbitrary",)),
        name="moe_combine",
    )(ytok, top_w, x2, g2, final_g)


def _route(top_idx, counts, n_rows):
    tm = EXPERT_TILE
    e = top_idx[:, :TOP_K]
    rank = top_idx[:, TOP_K:2 * TOP_K]
    sizes = counts[0, :N_EXPERTS]
    padded = ((sizes + tm - 1) // tm) * tm
    pend = jnp.cumsum(padded)
    pstart = pend - padded
    onehot = e[:, :, None] == jnp.arange(N_EXPERTS, dtype=I32)[None, None, :]
    pos = jnp.sum(jnp.where(onehot, pstart[None, None, :], 0), axis=-1) + rank
    tiles = jnp.arange(n_rows // tm, dtype=I32) * tm
    tile_e = jnp.minimum(jnp.sum(tiles[:, None] >= pend[None, :], axis=-1), N_EXPERTS - 1).astype(I32)
    n_used = (pend[-1] // tm).astype(I32).reshape(1)
    return pos.T.astype(I32), tile_e, n_used


def _sc_gather(table, idx):
    m = idx.shape[0]
    width = table.shape[1]
    assert m % (SC_WINDOW * SC_WORKERS) == 0
    mesh = plsc.VectorSubcoreMesh(core_axis_name="c", subcore_axis_name="s")

    @pl.kernel(out_type=jax.ShapeDtypeStruct((m, width), table.dtype), mesh=mesh)
    def gather_rows(t_hbm, i_hbm, o_hbm):
        def body(i_vmem, o_vmem):
            pltpu.sync_copy(t_hbm.at[i_vmem.at[0]], o_vmem)

        pltpu.emit_pipeline(
            body,
            grid=(m // SC_WINDOW,),
            in_specs=[pl.BlockSpec((1, SC_WINDOW), lambda i: (0, i))],
            out_specs=[pl.BlockSpec((SC_WINDOW, width), lambda i: (i, 0))],
            core_axis_name=("c", "s"),
            dimension_semantics=(pltpu.PARALLEL,),
        )(i_hbm, o_hbm)

    return gather_rows(table, idx.reshape(1, m))


def _sc_scatter(rows, idx, n_out):
    t, width = rows.shape
    picks = idx.shape[0]
    assert t % SC_WINDOW == 0
    mesh = plsc.VectorSubcoreMesh(core_axis_name="c", subcore_axis_name="s")

    @pl.kernel(out_type=jax.ShapeDtypeStruct((n_out, width), rows.dtype), mesh=mesh)
    def scatter_rows(x_hbm, i_hbm, o_hbm):
        def body(x_vmem, i_vmem):
            for k in range(picks):
                pltpu.sync_copy(x_vmem, o_hbm.at[i_vmem.at[k]])

        pltpu.emit_pipeline(
            body,
            grid=(t // SC_WINDOW,),
            in_specs=[pl.BlockSpec((SC_WINDOW, width), lambda i: (i, 0)),
                      pl.BlockSpec((picks, SC_WINDOW), lambda i: (0, i))],
            out_specs=[],
            core_axis_name=("c", "s"),
            dimension_semantics=(pltpu.PARALLEL,),
        )(x_hbm, i_hbm)

    return scatter_rows(rows, idx)


def _dispatch(h2p, pos_t, n_rows):
    n = h2p.shape[1]
    idx = jnp.concatenate([pos_t, pos_t + n_rows], axis=1)
    return _sc_scatter(h2p.reshape(2 * n, QUART), idx, 2 * n_rows).reshape(2, n_rows, QUART)


def _collect(y, pos_t):
    n_rows = y.shape[1]
    flat = pos_t.reshape(-1)
    out = _sc_gather(y.reshape(2 * n_rows, QUART), jnp.concatenate([flat, flat + n_rows]))
    return out.reshape(2, TOP_K, pos_t.shape[1], QUART)


def kernel(x, c, norm1_g, norm2_g, ada_w, ada_b, w_in, gate_b, fox_fb, lam_q1, lam_k1, lam_q2, lam_k2,
           diff_subln_g, w_branch, w_out, router_w, router_b, exp_w_up, exp_b_up, exp_w_down,
           exp_b_down, final_g):
    batch, seq, d = x.shape
    depth = ada_w.shape[0]
    n = batch * seq
    x2 = x.reshape(n, d)

    mod = _ada_mod(c, ada_w, ada_b)
    tables = _retention_tables(seq)
    n_rows = n * TOP_K + N_EXPERTS * EXPERT_TILE
    blk = BRANCH_WIDTH // LANES

    for l in range(depth):
        lam_init = 0.8 - 0.6 * math.exp(-0.3 * l)
        sh1, sc1, g1, sh2, sc2, g2 = [m.reshape(batch, 1, d) for m in jnp.split(mod[l], 6, axis=-1)]

        w_main, w_f = _prep_win(w_in, l)
        proj, flog = _inproj(x2, norm1_g[l].reshape(1, d), sh1, sc1, w_main, w_f, seq)

        fb_pad = jnp.pad(fox_fb[l], (0, LANES - FOX_HEADS)).reshape(1, LANES)
        frow = _fcum(flog, fb_pad, batch, seq).reshape(batch * blk, 2, seq)

        lamv = jnp.stack([lam_q1[l], lam_k1[l], lam_q2[l], lam_k2[l]])
        o_diff = _attention(proj, "diff", 0, blk, 2 * blk, batch, seq,
                            (lamv, diff_subln_g[l].reshape(1, LANES)), lam_init)
        o_ret = _retention(proj, tables, batch, seq)
        o_fox = _attention(proj, "fox", 7 * blk, 8 * blk, 9 * blk, batch, seq, (frow,))

        rw = jnp.pad(router_w[l], ((0, 0), (0, LANES - N_EXPERTS)))
        rw_hi = rw.astype(BF16)
        rw = jnp.stack([rw_hi, (rw - rw_hi.astype(F32)).astype(BF16)])
        rb = jnp.pad(router_b[l], (0, LANES - N_EXPERTS), constant_values=NEG_INF).reshape(1, LANES)
        x2, h2p, top_idx, top_w, counts = _merge(
            o_diff, o_ret, o_fox, proj, gate_b[l].reshape(N_BRANCHES, 1, d), w_branch[l].astype(BF16),
            w_out[l].astype(BF16), x2, g1, norm2_g[l].reshape(1, d), sh2, sc2, rw, rb, seq)

        pos_t, tile_e, n_used = _route(top_idx, counts, n_rows)
        xs = _dispatch(h2p, pos_t, n_rows)

        w_up = _prep_up(exp_w_up, l)
        bu = exp_b_up[l]
        b_up = jnp.concatenate([bu[:, 0::2], bu[:, 1::2]], axis=-1).reshape(N_EXPERTS, 1, 2 * D_FF)
        y = _experts(tile_e, n_used, xs, w_up, b_up, _prep_down(exp_w_down, l),
                     exp_b_down[l].reshape(N_EXPERTS, 1, d))
        ytok = _collect(y, pos_t)
        x2 = _combine(ytok, top_w, x2, g2, final_g.reshape(1, d), seq, final=(l == depth - 1))

    return x2.reshape(batch, seq, d)
```

```python
import functools
import math

import jax
import jax.numpy as jnp
from jax import lax
from jax.experimental import pallas as pl
from jax.experimental.pallas import tpu as pltpu
from jax.experimental.pallas import tpu_sc as plsc

F32 = jnp.float32
BF16 = jnp.bfloat16
I32 = jnp.int32
U32 = jnp.uint32

D_MODEL = 1024
HEAD_DIM = 64
BRANCH_WIDTH = D_MODEL // 2
N_BRANCHES = 3
FOX_HEADS = BRANCH_WIDTH // HEAD_DIM
RET_HEADS = 4
RET_HEAD_DIM = 128
ROPE_BASE = 10000.0
RET_GAMMA_EXP0 = 5.0
N_EXPERTS = 32
TOP_K = 4
D_FF = D_MODEL
SWIGLU_LIMIT = 7.0
SWIGLU_ALPHA = 1.702
NORM_EPS = 1e-5
NEG_INF = -1e30

LANES = 128
MAIN_COLS = 8192
GATE_COL0 = 10 * BRANCH_WIDTH
HALF = D_MODEL // 2
QUART = HALF // 2
SC_WINDOW = 128
SC_WORKERS = 32
VMEM_LIMIT = 48 * 1024 * 1024

ROW_TILE_IN = 512
COL_TILE_IN = 1024
ATTN_TILE = 256
RET_CHUNK = 256
ROW_TILE_OUT = 256
MERGE_TILE = 1024
MERGE_SUB = 256
EXPERT_TILE = 256
EXPERT_GROUP = 2
CUM_CHUNK = 256


def _cparams(sem):
    return pltpu.CompilerParams(dimension_semantics=sem, vmem_limit_bytes=VMEM_LIMIT)


def _sigmoid(x):
    return 0.5 * jnp.tanh(0.5 * x) + 0.5


def _pack_pairs(v):
    u = lax.bitcast_convert_type(v.astype(BF16).astype(F32), U32)
    half = v.shape[-1] // 2
    w = (u[:, half:] & jnp.uint32(0xFFFF0000)) | (u[:, :half] >> 16)
    return lax.bitcast_convert_type(w, I32)


def _unpack_pairs(w):
    u = lax.bitcast_convert_type(w, U32)
    lo = lax.bitcast_convert_type(u << 16, F32)
    hi = lax.bitcast_convert_type(u & jnp.uint32(0xFFFF0000), F32)
    return lo, hi


def _ada_kernel(c_ref, w_ref, b_ref, o_ref):
    c = c_ref[...]
    ca = c * _sigmoid(c)
    o_ref[0] = jnp.dot(ca, w_ref[0], preferred_element_type=F32) + b_ref[0]


def _ada_mod(c, ada_w, ada_b):
    depth, d, cols = ada_w.shape
    b = c.shape[0]
    tn = 1024
    return pl.pallas_call(
        _ada_kernel,
        grid=(depth, cols // tn),
        in_specs=[
            pl.BlockSpec((b, d), lambda l, j: (0, 0)),
            pl.BlockSpec((1, d, tn), lambda l, j: (l, 0, j)),
            pl.BlockSpec((1, 1, tn), lambda l, j: (l, 0, j)),
        ],
        out_specs=pl.BlockSpec((1, b, tn), lambda l, j: (l, 0, j)),
        out_shape=jax.ShapeDtypeStruct((depth, b, cols), F32),
        compiler_params=_cparams(("arbitrary", "arbitrary")),
        name="ada_mod",
    )(c, ada_w, ada_b.reshape(depth, 1, cols))


def _prep_win_kernel(w_ref, wm_ref, wf_ref):
    x = w_ref[0]
    wm_ref[:, :GATE_COL0] = x[:, :GATE_COL0].astype(BF16)
    wm_ref[:, GATE_COL0:] = x[:, GATE_COL0 + FOX_HEADS:].astype(BF16)
    lane = lax.broadcasted_iota(I32, (x.shape[0], LANES), 1)
    wf_ref[...] = jnp.where(lane < FOX_HEADS, x[:, GATE_COL0:GATE_COL0 + LANES], 0.0).astype(BF16)


def _prep_win(w, layer):
    _, d, cols = w.shape
    rt = 128
    return pl.pallas_call(
        _prep_win_kernel,
        grid=(d // rt,),
        in_specs=[pl.BlockSpec((1, rt, cols), lambda i: (layer, i, 0))],
        out_specs=[pl.BlockSpec((rt, MAIN_COLS), lambda i: (i, 0)), pl.BlockSpec((rt, LANES), lambda i: (i, 0))],
        out_shape=[jax.ShapeDtypeStruct((d, MAIN_COLS), BF16), jax.ShapeDtypeStruct((d, LANES), BF16)],
        compiler_params=_cparams(("arbitrary",)),
        name="inproj_weight_prep",
    )(w)


def _inproj_kernel(x_ref, g_ref, sh_ref, sc_ref, w_ref, wf_ref, proj_ref, fl_ref):
    x = x_ref[...]
    r = lax.rsqrt(jnp.mean(x * x, axis=-1, keepdims=True) + NORM_EPS)
    h = x * r * g_ref[...]
    h = h * (1.0 + sc_ref[0]) + sh_ref[0]
    hb = h.astype(BF16)
    fl_ref[...] = jnp.dot(hb, wf_ref[...], preferred_element_type=F32)
    for j in range(MAIN_COLS // COL_TILE_IN):
        cs = slice(j * COL_TILE_IN, (j + 1) * COL_TILE_IN)
        proj_ref[:, cs] = jnp.dot(hb, w_ref[:, cs], preferred_element_type=F32).astype(BF16)


def _inproj(x2, g, sh, sc, w_main, w_f, seq):
    n, d = x2.shape
    tm = min(ROW_TILE_IN, seq)
    per_seq = seq // tm
    resident = pl.Buffered(1)
    return pl.pallas_call(
        _inproj_kernel,
        grid=(n // tm,),
        in_specs=[
            pl.BlockSpec((tm, d), lambda i: (i, 0)),
            pl.BlockSpec((1, d), lambda i: (0, 0)),
            pl.BlockSpec((1, 1, d), lambda i: (i // per_seq, 0, 0)),
            pl.BlockSpec((1, 1, d), lambda i: (i // per_seq, 0, 0)),
            pl.BlockSpec((d, MAIN_COLS), lambda i: (0, 0), pipeline_mode=resident),
            pl.BlockSpec((d, LANES), lambda i: (0, 0), pipeline_mode=resident),
        ],
        out_specs=[
            pl.BlockSpec((tm, MAIN_COLS), lambda i: (i, 0)),
            pl.BlockSpec((tm, LANES), lambda i: (i, 0)),
        ],
        out_shape=[
            jax.ShapeDtypeStruct((n, MAIN_COLS), BF16),
            jax.ShapeDtypeStruct((n, LANES), F32),
        ],
        compiler_params=_cparams(("arbitrary",)),
        name="inproj",
    )(x2, g, sh, sc, w_main, w_f)


def _split3(v):
    hi = v.astype(BF16)
    r1 = v - hi.astype(F32)
    mid = r1.astype(BF16)
    lo = (r1 - mid.astype(F32)).astype(BF16)
    return hi, mid, lo


def _fcum_kernel(fl_ref, fb_ref, o_ref, *, seq):
    c = min(CUM_CHUNK, seq)
    row = lax.broadcasted_iota(I32, (c, c), 0)
    col = lax.broadcasted_iota(I32, (c, c), 1)
    lower = (col <= row).astype(BF16)
    carry = jnp.zeros((1, LANES), F32)
    for ci in range(seq // c):
        x = fl_ref[pl.ds(ci * c, c), :] + fb_ref[...]
        lf = jnp.minimum(x, 0.0) - jnp.log(1.0 + jnp.exp(-jnp.abs(x)))
        hi, mid, lo = _split3(lf)
        cs = (jnp.dot(lower, hi, preferred_element_type=F32)
              + jnp.dot(lower, mid, preferred_element_type=F32)
              + jnp.dot(lower, lo, preferred_element_type=F32)) + carry
        o_ref[0, :, pl.ds(ci * c, c)] = cs.T[:FOX_HEADS, :]
        carry = cs[c - 1:c, :]


def _fcum(flog, fb_pad, batch, seq):
    return pl.pallas_call(
        functools.partial(_fcum_kernel, seq=seq),
        grid=(batch,),
        in_specs=[
            pl.BlockSpec((seq, LANES), lambda b: (b, 0)),
            pl.BlockSpec((1, LANES), lambda b: (0, 0)),
        ],
        out_specs=pl.BlockSpec((1, FOX_HEADS, seq), lambda b: (b, 0, 0)),
        out_shape=jax.ShapeDtypeStruct((batch, FOX_HEADS, seq), F32),
        compiler_params=_cparams(("arbitrary",)),
        name="fox_decay_cumsum",
    )(flog, fb_pad)


def _attn_kernel(*refs, mode, t, seq, lam_init):
    if mode == "diff":
        q_ref, k_ref, v_ref, lam_ref, g_ref, o_ref, vx_scr = refs
        f_ref = None
        lv = lam_ref[...]
        lam = (jnp.exp(jnp.sum(lv[0:1, :] * lv[1:2, :])) - jnp.exp(jnp.sum(lv[2:3, :] * lv[3:4, :]))
               + lam_init)
    else:
        q_ref, k_ref, v_ref, f_ref, o_ref, vx_scr = refs
    lane = lax.broadcasted_iota(I32, (t, LANES), 1)
    row = lax.broadcasted_iota(I32, (t, t), 0)
    col = lax.broadcasted_iota(I32, (t, t), 1)
    nt = (((1,), (1,)), ((), ()))

    v_all = v_ref[...]
    one = jnp.ones_like(v_all)
    if mode == "diff":
        vx_scr[0] = jnp.concatenate([v_all, one], axis=1)
    else:
        lane_s = lax.broadcasted_iota(I32, v_all.shape, 1)
        vx_scr[0] = jnp.where(lane_s < HEAD_DIM, v_all, one)
        vx_scr[1] = jnp.where(lane_s >= HEAD_DIM, v_all, one)

    n_strips = seq // t
    pairs = [(i, n_strips - 1 - i) for i in range(n_strips // 2)] if n_strips % 2 == 0 else [(i,) for i in range(n_strips)]
    for group in pairs:
        units = [(qi, idx) for qi in group for idx in range(2)]
        qms = {}
        for qi in group:
            q = q_ref[qi * t:(qi + 1) * t, :] * jnp.asarray(HEAD_DIM ** -0.5, BF16)
            zero = jnp.zeros_like(q)
            qms[qi, 0] = jnp.where(lane < HEAD_DIM, q, zero)
            qms[qi, 1] = jnp.where(lane >= HEAD_DIM, q, zero)

        def scores(u, k0, k1):
            s_ = lax.dot_general(qms[u], k_ref[k0:k1, :], nt, preferred_element_type=F32)
            if f_ref is not None:
                s_ = s_ - f_ref[0, u[1]:u[1] + 1, k0:k1]
            return s_

        sds = {u: jnp.where(col <= row, scores(u, u[0] * t, (u[0] + 1) * t), NEG_INF) for u in units}
        sms = {u: scores(u, 0, u[0] * t) for u in units if u[0] > 0}
        ms = {}
        for u in units:
            m = jnp.max(sds[u], axis=-1, keepdims=True)
            if u in sms:
                m = jnp.maximum(m, jnp.max(sms[u], axis=-1, keepdims=True))
            ms[u] = m
        pds = {u: jnp.exp((sds[u] - ms[u]).astype(BF16)) for u in units}
        pms = {u: jnp.exp((sms[u] - ms[u]).astype(BF16)) for u in sms}
        outs = {}
        for u in units:
            vx = vx_scr.at[0 if mode == "diff" else u[1]]
            lo = u[0] * t
            acc = jnp.dot(pds[u], vx[lo:lo + t, :], preferred_element_type=F32)
            if u in pms:
                acc = acc + jnp.dot(pms[u], vx[0:lo, :], preferred_element_type=F32)
            outs[u] = acc
        for qi in group:
            o0, o1 = outs[qi, 0], outs[qi, 1]
            if mode == "diff":
                a1 = o0[:, :LANES] / o0[:, LANES:]
                a2 = o1[:, :LANES] / o1[:, LANES:]
                o = a1 - lam * a2
                r = lax.rsqrt(jnp.mean(o * o, axis=-1, keepdims=True) + NORM_EPS)
                o = o * r * g_ref[...] * (1.0 - lam_init)
            else:
                num = jnp.where(lane < HEAD_DIM, o0, o1)
                den = jnp.where(lane < HEAD_DIM, pltpu.roll(o0, HEAD_DIM, 1), pltpu.roll(o1, HEAD_DIM, 1))
                o = num / den
            o_ref[qi * t:(qi + 1) * t, :] = o.astype(BF16)


def _attention(proj, mode, col_q, col_k, col_v, batch, seq, extra, lam_init=0.0):
    n = proj.shape[0]
    t = min(ATTN_TILE, seq)
    n_blk = BRANCH_WIDTH // LANES
    in_specs = [
        pl.BlockSpec((seq, LANES), lambda b, h: (b, col_q + h)),
        pl.BlockSpec((seq, LANES), lambda b, h: (b, col_k + h)),
        pl.BlockSpec((seq, LANES), lambda b, h: (b, col_v + h)),
    ]
    if mode == "diff":
        lamv, g = extra
        in_specs += [pl.BlockSpec((4, HEAD_DIM), lambda b, h: (0, 0)),
                     pl.BlockSpec((1, LANES), lambda b, h: (0, 0))]
        args = (proj, proj, proj, lamv, g)
    else:
        (frow,) = extra
        in_specs += [pl.BlockSpec((1, 2, seq), lambda b, h: (b * n_blk + h, 0, 0))]
        args = (proj, proj, proj, frow)
    return pl.pallas_call(
        functools.partial(_attn_kernel, mode=mode, t=t, seq=seq, lam_init=lam_init),
        grid=(batch, n_blk),
        in_specs=in_specs,
        out_specs=pl.BlockSpec((seq, LANES), lambda b, h: (b, h)),
        out_shape=jax.ShapeDtypeStruct((n, BRANCH_WIDTH), BF16),
        scratch_shapes=[pltpu.VMEM((1, seq, 2 * LANES) if mode == "diff" else (2, seq, LANES), BF16)],
        compiler_params=_cparams(("arbitrary", "arbitrary")),
        name="attn_" + mode,
    )(*args)


def _ret_kernel(q_ref, k_ref, v_ref, g_ref, cos_ref, sin_ref, dm_ref, qd_ref, kd_ref, cd_ref,
                o_ref, *, seq, c):
    st = jnp.zeros((RET_HEAD_DIM, RET_HEAD_DIM), F32)
    half = RET_HEAD_DIM // 2

    for ci in range(seq // c):
        sl = pl.ds(ci * c, c)
        cos = cos_ref[sl, :]
        sin = sin_ref[sl, :]
        q = q_ref[sl, :].astype(F32)
        k = k_ref[sl, :].astype(F32)
        qr = q * cos + pltpu.roll(q, half, 1) * sin
        kr = (k * cos + pltpu.roll(k, half, 1) * sin) * (RET_HEAD_DIM ** -0.5)
        v = v_ref[sl, :]
        sc = lax.dot_general(qr.astype(BF16), kr.astype(BF16), (((1,), (1,)), ((), ())),
                             preferred_element_type=F32) * dm_ref[0]
        inner = jnp.dot(sc.astype(BF16), v, preferred_element_type=F32)
        cross = jnp.dot((qr * qd_ref[0]).astype(BF16), st.astype(BF16), preferred_element_type=F32)
        o = inner + cross
        kt = (kr * kd_ref[0]).T.astype(BF16)
        st = st * cd_ref[0] + jnp.dot(kt, v, preferred_element_type=F32)
        mu = jnp.mean(o, axis=-1, keepdims=True)
        oc = o - mu
        var = jnp.mean(oc * oc, axis=-1, keepdims=True)
        on = oc * lax.rsqrt(var + NORM_EPS)
        g = g_ref[sl, :].astype(F32)
        o_ref[sl, :] = (on * (g * _sigmoid(g))).astype(BF16)


def _retention(proj, tables, batch, seq):
    n = proj.shape[0]
    c = min(RET_CHUNK, seq)
    cos, sin, dm, qd, kd, cd = tables
    col = BRANCH_WIDTH * 3 // LANES
    step = BRANCH_WIDTH // LANES

    def pspec(k):
        return pl.BlockSpec((seq, LANES), lambda b, h: (b, col + k * step + h))

    return pl.pallas_call(
        functools.partial(_ret_kernel, seq=seq, c=c),
        grid=(batch, RET_HEADS),
        in_specs=[
            pspec(0), pspec(1), pspec(2), pspec(3),
            pl.BlockSpec((seq, LANES), lambda b, h: (0, 0)),
            pl.BlockSpec((seq, LANES), lambda b, h: (0, 0)),
            pl.BlockSpec((1, c, c), lambda b, h: (h, 0, 0)),
            pl.BlockSpec((1, c, LANES), lambda b, h: (h, 0, 0)),
            pl.BlockSpec((1, c, LANES), lambda b, h: (h, 0, 0)),
            pl.BlockSpec((1, LANES, LANES), lambda b, h: (h, 0, 0)),
        ],
        out_specs=pl.BlockSpec((seq, LANES), lambda b, h: (b, h)),
        out_shape=jax.ShapeDtypeStruct((n, BRANCH_WIDTH), BF16),
        compiler_params=_cparams(("arbitrary", "arbitrary")),
        name="retention",
    )(proj, proj, proj, proj, cos, sin, dm, qd, kd, cd)


def _retention_tables(seq):
    c = min(RET_CHUNK, seq)
    d = RET_HEAD_DIM
    inv = ROPE_BASE ** (-jnp.arange(0, d, 2, dtype=F32) / d)
    ang = jnp.arange(seq).astype(F32)[:, None] * inv[None, :]
    cos = jnp.cos(ang)
    sin = jnp.sin(ang)
    cos_full = jnp.concatenate([cos, cos], axis=-1)
    sin_signed = jnp.concatenate([-sin, sin], axis=-1)
    log_gamma = jnp.log1p(-jnp.exp2(-RET_GAMMA_EXP0 - jnp.arange(RET_HEADS, dtype=F32)))
    idx = jnp.arange(c, dtype=F32)
    dist = idx[:, None] - idx[None, :]
    dm = jnp.where(dist >= 0, jnp.exp(log_gamma[:, None, None] * jnp.maximum(dist, 0.0)), 0.0)
    qd = jnp.exp(log_gamma[:, None] * (idx + 1.0))
    kd = jnp.exp(log_gamma[:, None] * (c - 1.0 - idx))
    cd = jnp.exp(log_gamma * c)
    qd = jnp.broadcast_to(qd[:, :, None], (RET_HEADS, c, LANES))
    kd = jnp.broadcast_to(kd[:, :, None], (RET_HEADS, c, LANES))
    cd = jnp.broadcast_to(cd[:, None, None], (RET_HEADS, LANES, LANES))
    return cos_full, sin_signed, dm, qd, kd, cd


def _merge_kernel(od_ref, or_ref, of_ref, gd_ref, gr_ref, gf_ref, gb_ref, wb_ref, wo_ref, x_ref,
                  g1_ref, n2_ref, sh_ref, sc_ref, rw_ref, rb_ref,
                  xo_ref, hp_ref, ti_ref, tw_ref, cnt_ref, cnt_scr, *, tm, sub):
    @pl.when(pl.program_id(0) == 0)
    def _():
        cnt_scr[...] = jnp.zeros(cnt_scr.shape, F32)

    lane = lax.broadcasted_iota(I32, (sub, LANES), 1)
    r_i = lax.broadcasted_iota(I32, (sub, sub), 0)
    c_i = lax.broadcasted_iota(I32, (sub, sub), 1)
    earlier = (c_i < r_i).astype(BF16)
    counts = cnt_scr[...]

    blocks = [slice(s0, s0 + sub) for s0 in range(0, tm, sub)]

    def phase_branches(rs):
        merged = None
        for i, (o_ref, gate_ref) in enumerate(((od_ref, gd_ref), (or_ref, gr_ref), (of_ref, gf_ref))):
            gate = _sigmoid(gate_ref[rs, :] + gb_ref[i].astype(BF16))
            term = gate.astype(F32) * jnp.dot(o_ref[rs, :], wb_ref[i], preferred_element_type=F32)
            merged = term if merged is None else merged + term
        return merged.astype(BF16)

    def phase_outproj(rs, merged):
        mix = jnp.dot(merged, wo_ref[...], preferred_element_type=F32)
        xn = x_ref[rs, :] + g1_ref[0] * mix
        xo_ref[rs, :] = xn
        r = lax.rsqrt(jnp.mean(xn * xn, axis=-1, keepdims=True) + NORM_EPS)
        h = xn * r * n2_ref[...]
        h = h * (1.0 + sc_ref[0]) + sh_ref[0]
        hp = _pack_pairs(h)
        hp_ref[0, rs, :] = hp[:, :QUART]
        hp_ref[1, rs, :] = hp[:, QUART:]
        hh, hm, _ = _split3(h)
        return (jnp.dot(hh, rw_ref[0], preferred_element_type=F32)
                + jnp.dot(hh, rw_ref[1], preferred_element_type=F32)
                + jnp.dot(hm, rw_ref[0], preferred_element_type=F32)) + rb_ref[...]

    def phase_topk(rs, logits, counts):
        vals, idxs = [], []
        for _ in range(TOP_K):
            m = jnp.max(logits, axis=-1, keepdims=True)
            idx = jnp.min(jnp.where(logits == m, lane, LANES), axis=-1, keepdims=True)
            vals.append(m)
            idxs.append(idx)
            logits = jnp.where(lane == idx, -3.0e38, logits)
        es = [jnp.exp(v - vals[0]) for v in vals]
        den = es[0] + es[1] + es[2] + es[3]
        hits = [lane == idxs[k] for k in range(TOP_K)]
        picked = (hits[0] | hits[1] | hits[2] | hits[3])
        base = jnp.dot(earlier, picked.astype(BF16), preferred_element_type=F32) + counts
        counts = counts + jnp.sum(picked.astype(F32), axis=0, keepdims=True)
        ti = jnp.zeros((sub, LANES), I32)
        tw = jnp.zeros((sub, LANES), F32)
        for k in range(TOP_K):
            rank = jnp.sum(jnp.where(hits[k], base, 0.0), axis=-1, keepdims=True).astype(I32)
            ti = jnp.where(lane == k, idxs[k], ti)
            ti = jnp.where(lane == TOP_K + k, rank, ti)
            tw = jnp.where(lane == k, es[k] / den, tw)
        ti_ref[rs, :] = ti
        tw_ref[rs, :] = tw
        return counts

    merged = [phase_branches(rs) for rs in blocks]
    logits = [phase_outproj(rs, m) for rs, m in zip(blocks, merged)]
    for rs, lg in zip(blocks, logits):
        counts = phase_topk(rs, lg, counts)

    cnt_scr[...] = counts
    cnt_ref[...] = counts.astype(I32)


def _merge(o_diff, o_ret, o_fox, proj, gate_b, w_branch, w_out, x2, g1, n2g, sh2, sc2, rw, rb, seq):
    n, d = x2.shape
    tm = min(MERGE_TILE, seq)
    sub = min(MERGE_SUB, tm)
    per_seq = seq // tm
    gcol = GATE_COL0 // d

    def rows(w):
        return pl.BlockSpec((tm, w), lambda i: (i, 0))

    def gspec(k):
        return pl.BlockSpec((tm, d), lambda i: (i, gcol + k))

    def whole(shape):
        return pl.BlockSpec(shape, lambda i: (0,) * len(shape))

    def per_b():
        return pl.BlockSpec((1, 1, d), lambda i: (i // per_seq, 0, 0))

    return pl.pallas_call(
        functools.partial(_merge_kernel, tm=tm, sub=sub),
        grid=(n // tm,),
        in_specs=[
            rows(BRANCH_WIDTH), rows(BRANCH_WIDTH), rows(BRANCH_WIDTH),
            gspec(0), gspec(1), gspec(2),
            whole((N_BRANCHES, 1, d)), whole((N_BRANCHES, BRANCH_WIDTH, d)), whole((d, d)),
            rows(d), per_b(), whole((1, d)), per_b(), per_b(),
            whole((2, d, LANES)), whole((1, LANES)),
        ],
        out_specs=[rows(d), pl.BlockSpec((2, tm, QUART), lambda i: (0, i, 0)), rows(LANES), rows(LANES),
                   whole((1, LANES))],
        out_shape=[
            jax.ShapeDtypeStruct((n, d), F32),
            jax.ShapeDtypeStruct((2, n, QUART), I32),
            jax.ShapeDtypeStruct((n, LANES), I32),
            jax.ShapeDtypeStruct((n, LANES), F32),
            jax.ShapeDtypeStruct((1, LANES), I32),
        ],
        scratch_shapes=[pltpu.VMEM((1, LANES), F32)],
        compiler_params=_cparams(("arbitrary",)),
        name="merge_outproj_router",
    )(o_diff, o_ret, o_fox, proj, proj, proj, gate_b, w_branch, w_out, x2, g1, n2g, sh2, sc2, rw, rb)


def _prep_up_kernel(w_ref, o_ref, t_scr):
    f = t_scr.shape[1] // 2
    for g in range(t_scr.shape[0]):
        cols = slice(g * LANES, (g + 1) * LANES)
        t_scr[g] = w_ref[0, 0, cols, :].T
        o_ref[0, 0:f, cols] = t_scr[g, pl.ds(0, f, stride=2), :].astype(BF16)
        o_ref[0, f:, cols] = t_scr[g, pl.ds(1, f, stride=2), :].astype(BF16)


def _prep_up(w_up, layer):
    _, e, d, f2 = w_up.shape
    groups = d // LANES
    kt = groups * LANES
    return pl.pallas_call(
        _prep_up_kernel,
        grid=(e, d // kt),
        in_specs=[pl.BlockSpec((1, 1, kt, f2), lambda i, j: (layer, i, j, 0))],
        out_specs=pl.BlockSpec((1, f2, kt), lambda i, j: (i, 0, j)),
        out_shape=jax.ShapeDtypeStruct((e, f2, d), BF16),
        scratch_shapes=[pltpu.VMEM((groups, f2, LANES), F32)],
        compiler_params=_cparams(("arbitrary", "arbitrary")),
        name="expert_up_weight_prep",
    )(w_up)


def _expert_kernel(te_ref, nu_ref, xs_ref, *refs):
    y_ref = refs[-1]
    nt = (((1,), (1,)), ((), ()))
    first_tile = pl.program_id(0) * EXPERT_GROUP

    @pl.when(first_tile < nu_ref[0])
    def _():
        for s in range(EXPERT_GROUP):
            wu_ref, bu_ref, wd_ref, bd_ref = refs[4 * s:4 * s + 4]
            rs = slice(s * EXPERT_TILE, (s + 1) * EXPERT_TILE)
            lo, hi = _unpack_pairs(jnp.concatenate([xs_ref[0, rs, :], xs_ref[1, rs, :]], axis=1))
            lo = lo.astype(BF16)
            hi = hi.astype(BF16)

            def up_half(r0):
                return (lax.dot_general(lo, wu_ref[0, r0:r0 + D_FF, :HALF], nt, preferred_element_type=F32)
                        + lax.dot_general(hi, wu_ref[0, r0:r0 + D_FF, HALF:], nt, preferred_element_type=F32))

            glu = jnp.minimum(up_half(0) + bu_ref[0, :, :D_FF], SWIGLU_LIMIT)
            lin = jnp.clip(up_half(D_FF) + bu_ref[0, :, D_FF:], -SWIGLU_LIMIT, SWIGLU_LIMIT)
            act = glu * _sigmoid(SWIGLU_ALPHA * glu) * (lin + 1.0)
            down = jnp.dot(act.astype(BF16), wd_ref[0, 0].astype(BF16), preferred_element_type=F32) + bd_ref[0]
            yp = _pack_pairs(down)
            y_ref[0, rs, :] = yp[:, :QUART]
            y_ref[1, rs, :] = yp[:, QUART:]

    @pl.when(first_tile >= nu_ref[0])
    def _():
        y_ref[...] = jnp.zeros(y_ref.shape, I32)


def _experts(tile_e, n_used, xs, w_up, b_up, w_down, layer, b_down):
    r = xs.shape[1]
    tm = EXPERT_TILE * EXPERT_GROUP
    d = D_MODEL
    in_specs = [pl.BlockSpec((2, tm, QUART), lambda i, te, nu: (0, i, 0))]
    args = [xs]
    for s in range(EXPERT_GROUP):
        def pick(i, te, nu, s=s):
            return (te[i * EXPERT_GROUP + s], 0, 0)

        def pick_down(i, te, nu, s=s):
            return (layer, te[i * EXPERT_GROUP + s], 0, 0)
        in_specs += [pl.BlockSpec((1, 2 * D_FF, d), pick), pl.BlockSpec((1, 1, 2 * D_FF), pick),
                     pl.BlockSpec((1, 1, D_FF, d), pick_down), pl.BlockSpec((1, 1, d), pick)]
        args += [w_up, b_up, w_down, b_down]
    grid_spec = pltpu.PrefetchScalarGridSpec(
        num_scalar_prefetch=2,
        grid=(r // tm,),
        in_specs=in_specs,
        out_specs=pl.BlockSpec((2, tm, QUART), lambda i, te, nu: (0, i, 0)),
    )
    return pl.pallas_call(
        _expert_kernel,
        grid_spec=grid_spec,
        out_shape=jax.ShapeDtypeStruct((2, r, QUART), I32),
        compiler_params=_cparams(("arbitrary",)),
        name="expert_mlp",
    )(tile_e, n_used, *args)


def _combine_kernel(y_ref, tw_ref, x_ref, g2_ref, fg_ref, o_ref, *, final):
    tw = tw_ref[...]
    parts = [None] * 4
    for k in range(TOP_K):
        w = tw[:, k:k + 1]
        for piece in range(2):
            lo, hi = _unpack_pairs(y_ref[piece, k])
            for slot, v in ((piece, lo), (2 + piece, hi)):
                parts[slot] = w * v if parts[slot] is None else parts[slot] + w * v
    ffn = jnp.concatenate(parts, axis=-1)
    xn = x_ref[...] + g2_ref[0] * ffn
    if final:
        r = lax.rsqrt(jnp.mean(xn * xn, axis=-1, keepdims=True) + NORM_EPS)
        xn = xn * r * fg_ref[...]
    o_ref[...] = xn


def _combine(ytok, top_w, x2, g2, final_g, seq, final):
    n, d = x2.shape
    tm = min(ROW_TILE_OUT, seq)
    per_seq = seq // tm
    return pl.pallas_call(
        functools.partial(_combine_kernel, final=final),
        grid=(n // tm,),
        in_specs=[
            pl.BlockSpec((2, TOP_K, tm, QUART), lambda i: (0, 0, i, 0)),
            pl.BlockSpec((tm, LANES), lambda i: (i, 0)),
            pl.BlockSpec((tm, d), lambda i: (i, 0)),
            pl.BlockSpec((1, 1, d), lambda i: (i // per_seq, 0, 0)),
            pl.BlockSpec((1, d), lambda i: (0, 0)),
        ],
        out_specs=pl.BlockSpec((tm, d), lambda i: (i, 0)),
        out_shape=jax.ShapeDtypeStruct((n, d), F32),
        compiler_params=_cparams(("arbitrary",)),
        name="moe_combine",
    )(ytok, top_w, x2, g2, final_g)


def _route(top_idx, counts, n_rows):
    tm = EXPERT_TILE
    e = top_idx[:, :TOP_K]
    rank = top_idx[:, TOP_K:2 * TOP_K]
    sizes = counts[0, :N_EXPERTS]
    padded = ((sizes + tm - 1) // tm) * tm
    pend = jnp.cumsum(padded)
    pstart = pend - padded
    onehot = e[:, :, None] == jnp.arange(N_EXPERTS, dtype=I32)[None, None, :]
    pos = jnp.sum(jnp.where(onehot, pstart[None, None, :], 0), axis=-1) + rank
    tiles = jnp.arange(n_rows // tm, dtype=I32) * tm
    tile_e = jnp.minimum(jnp.sum(tiles[:, None] >= pend[None, :], axis=-1), N_EXPERTS - 1).astype(I32)
    n_used = (pend[-1] // tm).astype(I32).reshape(1)
    return pos.T.astype(I32), tile_e, n_used


def _sc_gather(table, idx):
    m = idx.shape[0]
    width = table.shape[1]
    assert m % (SC_WINDOW * SC_WORKERS) == 0
    mesh = plsc.VectorSubcoreMesh(core_axis_name="c", subcore_axis_name="s")

    @pl.kernel(out_type=jax.ShapeDtypeStruct((m, width), table.dtype), mesh=mesh)
    def gather_rows(t_hbm, i_hbm, o_hbm):
        def body(i_vmem, o_vmem):
            pltpu.sync_copy(t_hbm.at[i_vmem.at[0]], o_vmem)

        pltpu.emit_pipeline(
            body,
            grid=(m // SC_WINDOW,),
            in_specs=[pl.BlockSpec((1, SC_WINDOW), lambda i: (0, i))],
            out_specs=[pl.BlockSpec((SC_WINDOW, width), lambda i: (i, 0))],
            core_axis_name=("c", "s"),
            dimension_semantics=(pltpu.PARALLEL,),
        )(i_hbm, o_hbm)

    return gather_rows(table, idx.reshape(1, m))


def _sc_scatter(rows, idx, n_out):
    t, width = rows.shape
    picks = idx.shape[0]
    assert t % SC_WINDOW == 0
    mesh = plsc.VectorSubcoreMesh(core_axis_name="c", subcore_axis_name="s")

    @pl.kernel(out_type=jax.ShapeDtypeStruct((n_out, width), rows.dtype), mesh=mesh)
    def scatter_rows(x_hbm, i_hbm, o_hbm):
        def body(x_vmem, i_vmem):
            for k in range(picks):
                pltpu.sync_copy(x_vmem, o_hbm.at[i_vmem.at[k]])

        pltpu.emit_pipeline(
            body,
            grid=(t // SC_WINDOW,),
            in_specs=[pl.BlockSpec((SC_WINDOW, width), lambda i: (i, 0)),
                      pl.BlockSpec((picks, SC_WINDOW), lambda i: (0, i))],
            out_specs=[],
            core_axis_name=("c", "s"),
            dimension_semantics=(pltpu.PARALLEL,),
        )(x_hbm, i_hbm)

    return scatter_rows(rows, idx)


def _dispatch(h2p, pos_t, n_rows):
    n = h2p.shape[1]
    idx = jnp.concatenate([pos_t, pos_t + n_rows], axis=1)
    return _sc_scatter(h2p.reshape(2 * n, QUART), idx, 2 * n_rows).reshape(2, n_rows, QUART)


def _collect(y, pos_t):
    n_rows = y.shape[1]
    flat = pos_t.reshape(-1)
    out = _sc_gather(y.reshape(2 * n_rows, QUART), jnp.concatenate([flat, flat + n_rows]))
    return out.reshape(2, TOP_K, pos_t.shape[1], QUART)


def kernel(x, c, norm1_g, norm2_g, ada_w, ada_b, w_in, gate_b, fox_fb, lam_q1, lam_k1, lam_q2, lam_k2,
           diff_subln_g, w_branch, w_out, router_w, router_b, exp_w_up, exp_b_up, exp_w_down,
           exp_b_down, final_g):
    batch, seq, d = x.shape
    depth = ada_w.shape[0]
    n = batch * seq
    x2 = x.reshape(n, d)

    mod = _ada_mod(c, ada_w, ada_b)
    tables = _retention_tables(seq)
    n_rows = n * TOP_K + N_EXPERTS * EXPERT_TILE
    blk = BRANCH_WIDTH // LANES

    for l in range(depth):
        lam_init = 0.8 - 0.6 * math.exp(-0.3 * l)
        sh1, sc1, g1, sh2, sc2, g2 = [m.reshape(batch, 1, d) for m in jnp.split(mod[l], 6, axis=-1)]

        w_main, w_f = _prep_win(w_in, l)
        proj, flog = _inproj(x2, norm1_g[l].reshape(1, d), sh1, sc1, w_main, w_f, seq)

        fb_pad = jnp.pad(fox_fb[l], (0, LANES - FOX_HEADS)).reshape(1, LANES)
        frow = _fcum(flog, fb_pad, batch, seq).reshape(batch * blk, 2, seq)

        lamv = jnp.stack([lam_q1[l], lam_k1[l], lam_q2[l], lam_k2[l]])
        o_diff = _attention(proj, "diff", 0, blk, 2 * blk, batch, seq,
                            (lamv, diff_subln_g[l].reshape(1, LANES)), lam_init)
        o_ret = _retention(proj, tables, batch, seq)
        o_fox = _attention(proj, "fox", 7 * blk, 8 * blk, 9 * blk, batch, seq, (frow,))

        rw = jnp.pad(router_w[l], ((0, 0), (0, LANES - N_EXPERTS)))
        rw_hi = rw.astype(BF16)
        rw = jnp.stack([rw_hi, (rw - rw_hi.astype(F32)).astype(BF16)])
        rb = jnp.pad(router_b[l], (0, LANES - N_EXPERTS), constant_values=NEG_INF).reshape(1, LANES)
        x2, h2p, top_idx, top_w, counts = _merge(
            o_diff, o_ret, o_fox, proj, gate_b[l].reshape(N_BRANCHES, 1, d), w_branch[l].astype(BF16),
            w_out[l].astype(BF16), x2, g1, norm2_g[l].reshape(1, d), sh2, sc2, rw, rb, seq)

        pos_t, tile_e, n_used = _route(top_idx, counts, n_rows)
        xs = _dispatch(h2p, pos_t, n_rows)

        w_up = _prep_up(exp_w_up, l)
        bu = exp_b_up[l]
        b_up = jnp.concatenate([bu[:, 0::2], bu[:, 1::2]], axis=-1).reshape(N_EXPERTS, 1, 2 * D_FF)
        y = _experts(tile_e, n_used, xs, w_up, b_up, exp_w_down, l,
                     exp_b_down[l].reshape(N_EXPERTS, 1, d))
        ytok = _collect(y, pos_t)
        x2 = _combine(ytok, top_w, x2, g2, final_g.reshape(1, d), seq, final=(l == depth - 1))

    return x2.reshape(batch, seq, d)
```

```python
import functools
import math

import jax
import jax.numpy as jnp
from jax import lax
from jax.experimental import pallas as pl
from jax.experimental.pallas import tpu as pltpu
from jax.experimental.pallas import tpu_sc as plsc

F32 = jnp.float32
BF16 = jnp.bfloat16
I32 = jnp.int32
U32 = jnp.uint32

D_MODEL = 1024
HEAD_DIM = 64
BRANCH_WIDTH = D_MODEL // 2
N_BRANCHES = 3
FOX_HEADS = BRANCH_WIDTH // HEAD_DIM
RET_HEADS = 4
RET_HEAD_DIM = 128
ROPE_BASE = 10000.0
RET_GAMMA_EXP0 = 5.0
N_EXPERTS = 32
TOP_K = 4
D_FF = D_MODEL
SWIGLU_LIMIT = 7.0
SWIGLU_ALPHA = 1.702
NORM_EPS = 1e-5
NEG_INF = -1e30

LANES = 128
MAIN_COLS = 8192
GATE_COL0 = 10 * BRANCH_WIDTH
HALF = D_MODEL // 2
QUART = HALF // 2
SC_WINDOW = 128
SC_WORKERS = 32
VMEM_LIMIT = 48 * 1024 * 1024
VMEM_LIMIT_INPROJ = 56 * 1024 * 1024

ROW_TILE_IN = 512
COL_TILE_IN = 1024
ATTN_TILE = 256
RET_CHUNK = 256
ROW_TILE_OUT = 256
MERGE_TILE = 1024
MERGE_SUB = 256
EXPERT_TILE = 256
EXPERT_GROUP = 2
CUM_CHUNK = 256


def _cparams(sem):
    return pltpu.CompilerParams(dimension_semantics=sem, vmem_limit_bytes=VMEM_LIMIT)


def _sigmoid(x):
    return 0.5 * jnp.tanh(0.5 * x) + 0.5


def _pack_pairs(v):
    u = lax.bitcast_convert_type(v.astype(BF16).astype(F32), U32)
    half = v.shape[-1] // 2
    w = (u[:, half:] & jnp.uint32(0xFFFF0000)) | (u[:, :half] >> 16)
    return lax.bitcast_convert_type(w, I32)


def _unpack_pairs(w):
    u = lax.bitcast_convert_type(w, U32)
    lo = lax.bitcast_convert_type(u << 16, F32)
    hi = lax.bitcast_convert_type(u & jnp.uint32(0xFFFF0000), F32)
    return lo, hi


def _moe_residual(y_ref, tw_ref, x_ref, g2_ref):
    tw = tw_ref[...]
    parts = [None] * 4
    for k in range(TOP_K):
        w = tw[:, k:k + 1]
        for piece in range(2):
            lo, hi = _unpack_pairs(y_ref[piece, k])
            for slot, v in ((piece, lo), (2 + piece, hi)):
                parts[slot] = w * v if parts[slot] is None else parts[slot] + w * v
    return x_ref[...] + g2_ref[0] * jnp.concatenate(parts, axis=-1)


def _ada_kernel(c_ref, w_ref, b_ref, o_ref):
    c = c_ref[...]
    ca = c * _sigmoid(c)
    o_ref[0] = jnp.dot(ca, w_ref[0], preferred_element_type=F32) + b_ref[0]


def _ada_mod(c, ada_w, ada_b):
    depth, d, cols = ada_w.shape
    b = c.shape[0]
    tn = 1024
    return pl.pallas_call(
        _ada_kernel,
        grid=(depth, cols // tn),
        in_specs=[
            pl.BlockSpec((b, d), lambda l, j: (0, 0)),
            pl.BlockSpec((1, d, tn), lambda l, j: (l, 0, j)),
            pl.BlockSpec((1, 1, tn), lambda l, j: (l, 0, j)),
        ],
        out_specs=pl.BlockSpec((1, b, tn), lambda l, j: (l, 0, j)),
        out_shape=jax.ShapeDtypeStruct((depth, b, cols), F32),
        compiler_params=_cparams(("arbitrary", "arbitrary")),
        name="ada_mod",
    )(c, ada_w, ada_b.reshape(depth, 1, cols))


def _prep_win_kernel(w_ref, wm_ref, wf_ref):
    x = w_ref[0]
    wm_ref[:, :GATE_COL0] = x[:, :GATE_COL0].astype(BF16)
    wm_ref[:, GATE_COL0:] = x[:, GATE_COL0 + FOX_HEADS:].astype(BF16)
    lane = lax.broadcasted_iota(I32, (x.shape[0], LANES), 1)
    wf_ref[...] = jnp.where(lane < FOX_HEADS, x[:, GATE_COL0:GATE_COL0 + LANES], 0.0).astype(BF16)


def _prep_win(w, layer):
    _, d, cols = w.shape
    rt = 128
    return pl.pallas_call(
        _prep_win_kernel,
        grid=(d // rt,),
        in_specs=[pl.BlockSpec((1, rt, cols), lambda i: (layer, i, 0))],
        out_specs=[pl.BlockSpec((rt, MAIN_COLS), lambda i: (i, 0)), pl.BlockSpec((rt, LANES), lambda i: (i, 0))],
        out_shape=[jax.ShapeDtypeStruct((d, MAIN_COLS), BF16), jax.ShapeDtypeStruct((d, LANES), BF16)],
        compiler_params=_cparams(("arbitrary",)),
        name="inproj_weight_prep",
    )(w)


def _inproj_kernel(*refs, moe):
    if moe:
        y_ref, tw_ref, x_ref, g2_ref, g_ref, sh_ref, sc_ref, w_ref, wf_ref, proj_ref, fl_ref, xo_ref = refs
        x = _moe_residual(y_ref, tw_ref, x_ref, g2_ref)
        xo_ref[...] = x
    else:
        x_ref, g_ref, sh_ref, sc_ref, w_ref, wf_ref, proj_ref, fl_ref = refs
        x = x_ref[...]
    r = lax.rsqrt(jnp.mean(x * x, axis=-1, keepdims=True) + NORM_EPS)
    h = x * r * g_ref[...]
    h = h * (1.0 + sc_ref[0]) + sh_ref[0]
    hb = h.astype(BF16)
    fl_ref[...] = jnp.dot(hb, wf_ref[...], preferred_element_type=F32)
    for j in range(MAIN_COLS // COL_TILE_IN):
        cs = slice(j * COL_TILE_IN, (j + 1) * COL_TILE_IN)
        proj_ref[:, cs] = jnp.dot(hb, w_ref[:, cs], preferred_element_type=F32).astype(BF16)


def _inproj(x2, g, sh, sc, w_main, w_f, seq, moe=None):
    n, d = x2.shape
    tm = min(ROW_TILE_IN, seq)
    per_seq = seq // tm
    resident = pl.Buffered(1)

    def per_b():
        return pl.BlockSpec((1, 1, d), lambda i: (i // per_seq, 0, 0))

    in_specs = [
        pl.BlockSpec((tm, d), lambda i: (i, 0)),
        pl.BlockSpec((1, d), lambda i: (0, 0)),
        per_b(), per_b(),
        pl.BlockSpec((d, MAIN_COLS), lambda i: (0, 0), pipeline_mode=resident),
        pl.BlockSpec((d, LANES), lambda i: (0, 0), pipeline_mode=resident),
    ]
    out_specs = [pl.BlockSpec((tm, MAIN_COLS), lambda i: (i, 0)), pl.BlockSpec((tm, LANES), lambda i: (i, 0))]
    out_shape = [jax.ShapeDtypeStruct((n, MAIN_COLS), BF16), jax.ShapeDtypeStruct((n, LANES), F32)]
    args = [x2, g, sh, sc, w_main, w_f]
    if moe is not None:
        ytok, top_w, g2 = moe
        in_specs = [pl.BlockSpec((2, TOP_K, tm, QUART), lambda i: (0, 0, i, 0)),
                    pl.BlockSpec((tm, LANES), lambda i: (i, 0)), in_specs[0], per_b()] + in_specs[1:]
        args = [ytok, top_w, x2, g2] + args[1:]
        out_specs.append(pl.BlockSpec((tm, d), lambda i: (i, 0)))
        out_shape.append(jax.ShapeDtypeStruct((n, d), F32))
    return pl.pallas_call(
        functools.partial(_inproj_kernel, moe=moe is not None),
        grid=(n // tm,),
        in_specs=in_specs,
        out_specs=out_specs,
        out_shape=out_shape,
        compiler_params=pltpu.CompilerParams(dimension_semantics=("arbitrary",), vmem_limit_bytes=VMEM_LIMIT_INPROJ),
        name="inproj",
    )(*args)


def _split3(v):
    hi = v.astype(BF16)
    r1 = v - hi.astype(F32)
    mid = r1.astype(BF16)
    lo = (r1 - mid.astype(F32)).astype(BF16)
    return hi, mid, lo


def _fcum_kernel(fl_ref, fb_ref, o_ref, *, seq):
    c = min(CUM_CHUNK, seq)
    row = lax.broadcasted_iota(I32, (c, c), 0)
    col = lax.broadcasted_iota(I32, (c, c), 1)
    lower = (col <= row).astype(BF16)
    carry = jnp.zeros((1, LANES), F32)
    for ci in range(seq // c):
        x = fl_ref[pl.ds(ci * c, c), :] + fb_ref[...]
        lf = jnp.minimum(x, 0.0) - jnp.log(1.0 + jnp.exp(-jnp.abs(x)))
        hi, mid, lo = _split3(lf)
        cs = (jnp.dot(lower, hi, preferred_element_type=F32)
              + jnp.dot(lower, mid, preferred_element_type=F32)
              + jnp.dot(lower, lo, preferred_element_type=F32)) + carry
        o_ref[0, :, pl.ds(ci * c, c)] = cs.T[:FOX_HEADS, :]
        carry = cs[c - 1:c, :]


def _fcum(flog, fb_pad, batch, seq):
    return pl.pallas_call(
        functools.partial(_fcum_kernel, seq=seq),
        grid=(batch,),
        in_specs=[
            pl.BlockSpec((seq, LANES), lambda b: (b, 0)),
            pl.BlockSpec((1, LANES), lambda b: (0, 0)),
        ],
        out_specs=pl.BlockSpec((1, FOX_HEADS, seq), lambda b: (b, 0, 0)),
        out_shape=jax.ShapeDtypeStruct((batch, FOX_HEADS, seq), F32),
        compiler_params=_cparams(("arbitrary",)),
        name="fox_decay_cumsum",
    )(flog, fb_pad)


def _attn_kernel(*refs, mode, t, seq, lam_init):
    if mode == "diff":
        q_ref, k_ref, v_ref, lam_ref, g_ref, o_ref, vx_scr = refs
        f_ref = None
        lv = lam_ref[...]
        lam = (jnp.exp(jnp.sum(lv[0:1, :] * lv[1:2, :])) - jnp.exp(jnp.sum(lv[2:3, :] * lv[3:4, :]))
               + lam_init)
    else:
        q_ref, k_ref, v_ref, f_ref, o_ref, vx_scr = refs
    lane = lax.broadcasted_iota(I32, (t, LANES), 1)
    row = lax.broadcasted_iota(I32, (t, t), 0)
    col = lax.broadcasted_iota(I32, (t, t), 1)
    nt = (((1,), (1,)), ((), ()))

    v_all = v_ref[...]
    one = jnp.ones_like(v_all)
    if mode == "diff":
        vx_scr[0] = jnp.concatenate([v_all, one], axis=1)
    else:
        lane_s = lax.broadcasted_iota(I32, v_all.shape, 1)
        vx_scr[0] = jnp.where(lane_s < HEAD_DIM, v_all, one)
        vx_scr[1] = jnp.where(lane_s >= HEAD_DIM, v_all, one)

    n_strips = seq // t
    pairs = [(i, n_strips - 1 - i) for i in range(n_strips // 2)] if n_strips % 2 == 0 else [(i,) for i in range(n_strips)]
    for group in pairs:
        units = [(qi, idx) for qi in group for idx in range(2)]
        qms = {}
        for qi in group:
            q = q_ref[qi * t:(qi + 1) * t, :] * jnp.asarray(HEAD_DIM ** -0.5, BF16)
            zero = jnp.zeros_like(q)
            qms[qi, 0] = jnp.where(lane < HEAD_DIM, q, zero)
            qms[qi, 1] = jnp.where(lane >= HEAD_DIM, q, zero)

        def scores(u, k0, k1):
            s_ = lax.dot_general(qms[u], k_ref[k0:k1, :], nt, preferred_element_type=F32)
            if f_ref is not None:
                s_ = s_ - f_ref[0, u[1]:u[1] + 1, k0:k1]
            return s_

        sds = {u: jnp.where(col <= row, scores(u, u[0] * t, (u[0] + 1) * t), NEG_INF) for u in units}
        sms = {u: scores(u, 0, u[0] * t) for u in units if u[0] > 0}
        ms = {}
        for u in units:
            m = jnp.max(sds[u], axis=-1, keepdims=True)
            if u in sms:
                m = jnp.maximum(m, jnp.max(sms[u], axis=-1, keepdims=True))
            ms[u] = m
        pds = {u: jnp.exp((sds[u] - ms[u]).astype(BF16)) for u in units}
        pms = {u: jnp.exp((sms[u] - ms[u]).astype(BF16)) for u in sms}
        outs = {}
        for u in units:
            vx = vx_scr.at[0 if mode == "diff" else u[1]]
            lo = u[0] * t
            acc = jnp.dot(pds[u], vx[lo:lo + t, :], preferred_element_type=F32)
            if u in pms:
                acc = acc + jnp.dot(pms[u], vx[0:lo, :], preferred_element_type=F32)
            outs[u] = acc
        for qi in group:
            o0, o1 = outs[qi, 0], outs[qi, 1]
            if mode == "diff":
                a1 = o0[:, :LANES] / o0[:, LANES:]
                a2 = o1[:, :LANES] / o1[:, LANES:]
                o = a1 - lam * a2
                r = lax.rsqrt(jnp.mean(o * o, axis=-1, keepdims=True) + NORM_EPS)
                o = o * r * g_ref[...] * (1.0 - lam_init)
            else:
                num = jnp.where(lane < HEAD_DIM, o0, o1)
                den = jnp.where(lane < HEAD_DIM, pltpu.roll(o0, HEAD_DIM, 1), pltpu.roll(o1, HEAD_DIM, 1))
                o = num / den
            o_ref[qi * t:(qi + 1) * t, :] = o.astype(BF16)


def _attention(proj, mode, col_q, col_k, col_v, batch, seq, extra, lam_init=0.0):
    n = proj.shape[0]
    t = min(ATTN_TILE, seq)
    n_blk = BRANCH_WIDTH // LANES
    in_specs = [
        pl.BlockSpec((seq, LANES), lambda b, h: (b, col_q + h)),
        pl.BlockSpec((seq, LANES), lambda b, h: (b, col_k + h)),
        pl.BlockSpec((seq, LANES), lambda b, h: (b, col_v + h)),
    ]
    if mode == "diff":
        lamv, g = extra
        in_specs += [pl.BlockSpec((4, HEAD_DIM), lambda b, h: (0, 0)),
                     pl.BlockSpec((1, LANES), lambda b, h: (0, 0))]
        args = (proj, proj, proj, lamv, g)
    else:
        (frow,) = extra
        in_specs += [pl.BlockSpec((1, 2, seq), lambda b, h: (b * n_blk + h, 0, 0))]
        args = (proj, proj, proj, frow)
    return pl.pallas_call(
        functools.partial(_attn_kernel, mode=mode, t=t, seq=seq, lam_init=lam_init),
        grid=(batch, n_blk),
        in_specs=in_specs,
        out_specs=pl.BlockSpec((seq, LANES), lambda b, h: (b, h)),
        out_shape=jax.ShapeDtypeStruct((n, BRANCH_WIDTH), BF16),
        scratch_shapes=[pltpu.VMEM((1, seq, 2 * LANES) if mode == "diff" else (2, seq, LANES), BF16)],
        compiler_params=_cparams(("arbitrary", "arbitrary")),
        name="attn_" + mode,
    )(*args)


def _ret_kernel(q_ref, k_ref, v_ref, g_ref, cos_ref, sin_ref, dm_ref, qd_ref, kd_ref, cd_ref,
                o_ref, *, seq, c):
    st = jnp.zeros((RET_HEAD_DIM, RET_HEAD_DIM), F32)
    half = RET_HEAD_DIM // 2

    for ci in range(seq // c):
        sl = pl.ds(ci * c, c)
        cos = cos_ref[sl, :]
        sin = sin_ref[sl, :]
        q = q_ref[sl, :].astype(F32)
        k = k_ref[sl, :].astype(F32)
        qr = q * cos + pltpu.roll(q, half, 1) * sin
        kr = (k * cos + pltpu.roll(k, half, 1) * sin) * (RET_HEAD_DIM ** -0.5)
        v = v_ref[sl, :]
        sc = lax.dot_general(qr.astype(BF16), kr.astype(BF16), (((1,), (1,)), ((), ())),
                             preferred_element_type=F32) * dm_ref[0]
        inner = jnp.dot(sc.astype(BF16), v, preferred_element_type=F32)
        cross = jnp.dot((qr * qd_ref[0]).astype(BF16), st.astype(BF16), preferred_element_type=F32)
        o = inner + cross
        kt = (kr * kd_ref[0]).T.astype(BF16)
        st = st * cd_ref[0] + jnp.dot(kt, v, preferred_element_type=F32)
        mu = jnp.mean(o, axis=-1, keepdims=True)
        oc = o - mu
        var = jnp.mean(oc * oc, axis=-1, keepdims=True)
        on = oc * lax.rsqrt(var + NORM_EPS)
        g = g_ref[sl, :].astype(F32)
        o_ref[sl, :] = (on * (g * _sigmoid(g))).astype(BF16)


def _retention(proj, tables, batch, seq):
    n = proj.shape[0]
    c = min(RET_CHUNK, seq)
    cos, sin, dm, qd, kd, cd = tables
    col = BRANCH_WIDTH * 3 // LANES
    step = BRANCH_WIDTH // LANES

    def pspec(k):
        return pl.BlockSpec((seq, LANES), lambda b, h: (b, col + k * step + h))

    return pl.pallas_call(
        functools.partial(_ret_kernel, seq=seq, c=c),
        grid=(batch, RET_HEADS),
        in_specs=[
            pspec(0), pspec(1), pspec(2), pspec(3),
            pl.BlockSpec((seq, LANES), lambda b, h: (0, 0)),
            pl.BlockSpec((seq, LANES), lambda b, h: (0, 0)),
            pl.BlockSpec((1, c, c), lambda b, h: (h, 0, 0)),
            pl.BlockSpec((1, c, LANES), lambda b, h: (h, 0, 0)),
            pl.BlockSpec((1, c, LANES), lambda b, h: (h, 0, 0)),
            pl.BlockSpec((1, LANES, LANES), lambda b, h: (h, 0, 0)),
        ],
        out_specs=pl.BlockSpec((seq, LANES), lambda b, h: (b, h)),
        out_shape=jax.ShapeDtypeStruct((n, BRANCH_WIDTH), BF16),
        compiler_params=_cparams(("arbitrary", "arbitrary")),
        name="retention",
    )(proj, proj, proj, proj, cos, sin, dm, qd, kd, cd)


def _retention_tables(seq):
    c = min(RET_CHUNK, seq)
    d = RET_HEAD_DIM
    inv = ROPE_BASE ** (-jnp.arange(0, d, 2, dtype=F32) / d)
    ang = jnp.arange(seq).astype(F32)[:, None] * inv[None, :]
    cos = jnp.cos(ang)
    sin = jnp.sin(ang)
    cos_full = jnp.concatenate([cos, cos], axis=-1)
    sin_signed = jnp.concatenate([-sin, sin], axis=-1)
    log_gamma = jnp.log1p(-jnp.exp2(-RET_GAMMA_EXP0 - jnp.arange(RET_HEADS, dtype=F32)))
    idx = jnp.arange(c, dtype=F32)
    dist = idx[:, None] - idx[None, :]
    dm = jnp.where(dist >= 0, jnp.exp(log_gamma[:, None, None] * jnp.maximum(dist, 0.0)), 0.0)
    qd = jnp.exp(log_gamma[:, None] * (idx + 1.0))
    kd = jnp.exp(log_gamma[:, None] * (c - 1.0 - idx))
    cd = jnp.exp(log_gamma * c)
    qd = jnp.broadcast_to(qd[:, :, None], (RET_HEADS, c, LANES))
    kd = jnp.broadcast_to(kd[:, :, None], (RET_HEADS, c, LANES))
    cd = jnp.broadcast_to(cd[:, None, None], (RET_HEADS, LANES, LANES))
    return cos_full, sin_signed, dm, qd, kd, cd


def _merge_kernel(od_ref, or_ref, of_ref, gd_ref, gr_ref, gf_ref, gb_ref, wb_ref, wo_ref, x_ref,
                  g1_ref, n2_ref, sh_ref, sc_ref, rw_ref, rb_ref,
                  xo_ref, hp_ref, ti_ref, tw_ref, cnt_ref, cnt_scr, *, tm, sub):
    @pl.when(pl.program_id(0) == 0)
    def _():
        cnt_scr[...] = jnp.zeros(cnt_scr.shape, F32)

    lane = lax.broadcasted_iota(I32, (sub, LANES), 1)
    r_i = lax.broadcasted_iota(I32, (sub, sub), 0)
    c_i = lax.broadcasted_iota(I32, (sub, sub), 1)
    earlier = (c_i < r_i).astype(BF16)
    counts = cnt_scr[...]

    blocks = [slice(s0, s0 + sub) for s0 in range(0, tm, sub)]

    def phase_branches(rs):
        merged = None
        for i, (o_ref, gate_ref) in enumerate(((od_ref, gd_ref), (or_ref, gr_ref), (of_ref, gf_ref))):
            gate = _sigmoid(gate_ref[rs, :] + gb_ref[i].astype(BF16))
            term = gate.astype(F32) * jnp.dot(o_ref[rs, :], wb_ref[i], preferred_element_type=F32)
            merged = term if merged is None else merged + term
        return merged.astype(BF16)

    def phase_outproj(rs, merged):
        mix = jnp.dot(merged, wo_ref[...], preferred_element_type=F32)
        xn = x_ref[rs, :] + g1_ref[0] * mix
        xo_ref[rs, :] = xn
        r = lax.rsqrt(jnp.mean(xn * xn, axis=-1, keepdims=True) + NORM_EPS)
        h = xn * r * n2_ref[...]
        h = h * (1.0 + sc_ref[0]) + sh_ref[0]
        hp = _pack_pairs(h)
        hp_ref[0, rs, :] = hp[:, :QUART]
        hp_ref[1, rs, :] = hp[:, QUART:]
        hh, hm, _ = _split3(h)
        return (jnp.dot(hh, rw_ref[0], preferred_element_type=F32)
                + jnp.dot(hh, rw_ref[1], preferred_element_type=F32)
                + jnp.dot(hm, rw_ref[0], preferred_element_type=F32)) + rb_ref[...]

    def phase_topk(rs, logits, counts):
        vals, idxs = [], []
        for _ in range(TOP_K):
            m = jnp.max(logits, axis=-1, keepdims=True)
            idx = jnp.min(jnp.where(logits == m, lane, LANES), axis=-1, keepdims=True)
            vals.append(m)
            idxs.append(idx)
            logits = jnp.where(lane == idx, -3.0e38, logits)
        es = [jnp.exp(v - vals[0]) for v in vals]
        den = es[0] + es[1] + es[2] + es[3]
        hits = [lane == idxs[k] for k in range(TOP_K)]
        picked = (hits[0] | hits[1] | hits[2] | hits[3])
        base = jnp.dot(earlier, picked.astype(BF16), preferred_element_type=F32) + counts
        counts = counts + jnp.sum(picked.astype(F32), axis=0, keepdims=True)
        ti = jnp.zeros((sub, LANES), I32)
        tw = jnp.zeros((sub, LANES), F32)
        for k in range(TOP_K):
            rank = jnp.sum(jnp.where(hits[k], base, 0.0), axis=-1, keepdims=True).astype(I32)
            ti = jnp.where(lane == k, idxs[k], ti)
            ti = jnp.where(lane == TOP_K + k, rank, ti)
            tw = jnp.where(lane == k, es[k] / den, tw)
        ti_ref[rs, :] = ti
        tw_ref[rs, :] = tw
        return counts

    merged = [phase_branches(rs) for rs in blocks]
    logits = [phase_outproj(rs, m) for rs, m in zip(blocks, merged)]
    for rs, lg in zip(blocks, logits):
        counts = phase_topk(rs, lg, counts)

    cnt_scr[...] = counts
    cnt_ref[...] = counts.astype(I32)


def _merge(o_diff, o_ret, o_fox, proj, gate_b, w_branch, w_out, x2, g1, n2g, sh2, sc2, rw, rb, seq):
    n, d = x2.shape
    tm = min(MERGE_TILE, seq)
    sub = min(MERGE_SUB, tm)
    per_seq = seq // tm
    gcol = GATE_COL0 // d

    def rows(w):
        return pl.BlockSpec((tm, w), lambda i: (i, 0))

    def gspec(k):
        return pl.BlockSpec((tm, d), lambda i: (i, gcol + k))

    def whole(shape):
        return pl.BlockSpec(shape, lambda i: (0,) * len(shape))

    def per_b():
        return pl.BlockSpec((1, 1, d), lambda i: (i // per_seq, 0, 0))

    return pl.pallas_call(
        functools.partial(_merge_kernel, tm=tm, sub=sub),
        grid=(n // tm,),
        in_specs=[
            rows(BRANCH_WIDTH), rows(BRANCH_WIDTH), rows(BRANCH_WIDTH),
            gspec(0), gspec(1), gspec(2),
            whole((N_BRANCHES, 1, d)), whole((N_BRANCHES, BRANCH_WIDTH, d)), whole((d, d)),
            rows(d), per_b(), whole((1, d)), per_b(), per_b(),
            whole((2, d, LANES)), whole((1, LANES)),
        ],
        out_specs=[rows(d), pl.BlockSpec((2, tm, QUART), lambda i: (0, i, 0)), rows(LANES), rows(LANES),
                   whole((1, LANES))],
        out_shape=[
            jax.ShapeDtypeStruct((n, d), F32),
            jax.ShapeDtypeStruct((2, n, QUART), I32),
            jax.ShapeDtypeStruct((n, LANES), I32),
            jax.ShapeDtypeStruct((n, LANES), F32),
            jax.ShapeDtypeStruct((1, LANES), I32),
        ],
        scratch_shapes=[pltpu.VMEM((1, LANES), F32)],
        compiler_params=_cparams(("arbitrary",)),
        name="merge_outproj_router",
    )(o_diff, o_ret, o_fox, proj, proj, proj, gate_b, w_branch, w_out, x2, g1, n2g, sh2, sc2, rw, rb)


def _prep_up_kernel(w_ref, o_ref, t_scr):
    f = t_scr.shape[1] // 2
    for g in range(t_scr.shape[0]):
        cols = slice(g * LANES, (g + 1) * LANES)
        t_scr[g] = w_ref[0, 0, cols, :].T
        o_ref[0, 0:f, cols] = t_scr[g, pl.ds(0, f, stride=2), :].astype(BF16)
        o_ref[0, f:, cols] = t_scr[g, pl.ds(1, f, stride=2), :].astype(BF16)


def _prep_up(w_up, layer):
    _, e, d, f2 = w_up.shape
    groups = d // LANES
    kt = groups * LANES
    return pl.pallas_call(
        _prep_up_kernel,
        grid=(e, d // kt),
        in_specs=[pl.BlockSpec((1, 1, kt, f2), lambda i, j: (layer, i, j, 0))],
        out_specs=pl.BlockSpec((1, f2, kt), lambda i, j: (i, 0, j)),
        out_shape=jax.ShapeDtypeStruct((e, f2, d), BF16),
        scratch_shapes=[pltpu.VMEM((groups, f2, LANES), F32)],
        compiler_params=_cparams(("arbitrary", "arbitrary")),
        name="expert_up_weight_prep",
    )(w_up)


def _expert_kernel(te_ref, nu_ref, xs_ref, *refs):
    y_ref = refs[-1]
    nt = (((1,), (1,)), ((), ()))
    first_tile = pl.program_id(0) * EXPERT_GROUP

    @pl.when(first_tile < nu_ref[0])
    def _():
        for s in range(EXPERT_GROUP):
            wu_ref, bu_ref, wd_ref, bd_ref = refs[4 * s:4 * s + 4]
            rs = slice(s * EXPERT_TILE, (s + 1) * EXPERT_TILE)
            lo, hi = _unpack_pairs(jnp.concatenate([xs_ref[0, rs, :], xs_ref[1, rs, :]], axis=1))
            lo = lo.astype(BF16)
            hi = hi.astype(BF16)

            def up_half(r0):
                return (lax.dot_general(lo, wu_ref[0, r0:r0 + D_FF, :HALF], nt, preferred_element_type=F32)
                        + lax.dot_general(hi, wu_ref[0, r0:r0 + D_FF, HALF:], nt, preferred_element_type=F32))

            glu = jnp.minimum(up_half(0) + bu_ref[0, :, :D_FF], SWIGLU_LIMIT)
            lin = jnp.clip(up_half(D_FF) + bu_ref[0, :, D_FF:], -SWIGLU_LIMIT, SWIGLU_LIMIT)
            act = glu * _sigmoid(SWIGLU_ALPHA * glu) * (lin + 1.0)
            down = jnp.dot(act.astype(BF16), wd_ref[0, 0].astype(BF16), preferred_element_type=F32) + bd_ref[0]
            yp = _pack_pairs(down)
            y_ref[0, rs, :] = yp[:, :QUART]
            y_ref[1, rs, :] = yp[:, QUART:]

    @pl.when(first_tile >= nu_ref[0])
    def _():
        y_ref[...] = jnp.zeros(y_ref.shape, I32)


def _experts(tile_e, n_used, xs, w_up, b_up, w_down, layer, b_down):
    r = xs.shape[1]
    tm = EXPERT_TILE * EXPERT_GROUP
    d = D_MODEL
    in_specs = [pl.BlockSpec((2, tm, QUART), lambda i, te, nu: (0, i, 0))]
    args = [xs]
    for s in range(EXPERT_GROUP):
        def pick(i, te, nu, s=s):
            return (te[i * EXPERT_GROUP + s], 0, 0)

        def pick_down(i, te, nu, s=s):
            return (layer, te[i * EXPERT_GROUP + s], 0, 0)
        in_specs += [pl.BlockSpec((1, 2 * D_FF, d), pick), pl.BlockSpec((1, 1, 2 * D_FF), pick),
                     pl.BlockSpec((1, 1, D_FF, d), pick_down), pl.BlockSpec((1, 1, d), pick)]
        args += [w_up, b_up, w_down, b_down]
    grid_spec = pltpu.PrefetchScalarGridSpec(
        num_scalar_prefetch=2,
        grid=(r // tm,),
        in_specs=in_specs,
        out_specs=pl.BlockSpec((2, tm, QUART), lambda i, te, nu: (0, i, 0)),
    )
    return pl.pallas_call(
        _expert_kernel,
        grid_spec=grid_spec,
        out_shape=jax.ShapeDtypeStruct((2, r, QUART), I32),
        compiler_params=_cparams(("arbitrary",)),
        name="expert_mlp",
    )(tile_e, n_used, *args)


def _combine_kernel(y_ref, tw_ref, x_ref, g2_ref, fg_ref, o_ref):
    xn = _moe_residual(y_ref, tw_ref, x_ref, g2_ref)
    r = lax.rsqrt(jnp.mean(xn * xn, axis=-1, keepdims=True) + NORM_EPS)
    o_ref[...] = xn * r * fg_ref[...]


def _combine(ytok, top_w, x2, g2, final_g, seq):
    n, d = x2.shape
    tm = min(ROW_TILE_OUT, seq)
    per_seq = seq // tm
    return pl.pallas_call(
        _combine_kernel,
        grid=(n // tm,),
        in_specs=[
            pl.BlockSpec((2, TOP_K, tm, QUART), lambda i: (0, 0, i, 0)),
            pl.BlockSpec((tm, LANES), lambda i: (i, 0)),
            pl.BlockSpec((tm, d), lambda i: (i, 0)),
            pl.BlockSpec((1, 1, d), lambda i: (i // per_seq, 0, 0)),
            pl.BlockSpec((1, d), lambda i: (0, 0)),
        ],
        out_specs=pl.BlockSpec((tm, d), lambda i: (i, 0)),
        out_shape=jax.ShapeDtypeStruct((n, d), F32),
        compiler_params=_cparams(("arbitrary",)),
        name="moe_combine",
    )(ytok, top_w, x2, g2, final_g)


def _route(top_idx, counts, n_rows):
    tm = EXPERT_TILE
    e = top_idx[:, :TOP_K]
    rank = top_idx[:, TOP_K:2 * TOP_K]
    sizes = counts[0, :N_EXPERTS]
    padded = ((sizes + tm - 1) // tm) * tm
    pend = jnp.cumsum(padded)
    pstart = pend - padded
    onehot = e[:, :, None] == jnp.arange(N_EXPERTS, dtype=I32)[None, None, :]
    pos = jnp.sum(jnp.where(onehot, pstart[None, None, :], 0), axis=-1) + rank
    tiles = jnp.arange(n_rows // tm, dtype=I32) * tm
    tile_e = jnp.minimum(jnp.sum(tiles[:, None] >= pend[None, :], axis=-1), N_EXPERTS - 1).astype(I32)
    n_used = (pend[-1] // tm).astype(I32).reshape(1)
    return pos.T.astype(I32), tile_e, n_used


def _sc_gather(table, idx):
    m = idx.shape[0]
    width = table.shape[1]
    assert m % (SC_WINDOW * SC_WORKERS) == 0
    mesh = plsc.VectorSubcoreMesh(core_axis_name="c", subcore_axis_name="s")

    @pl.kernel(out_type=jax.ShapeDtypeStruct((m, width), table.dtype), mesh=mesh)
    def gather_rows(t_hbm, i_hbm, o_hbm):
        def body(i_vmem, o_vmem):
            pltpu.sync_copy(t_hbm.at[i_vmem.at[0]], o_vmem)

        pltpu.emit_pipeline(
            body,
            grid=(m // SC_WINDOW,),
            in_specs=[pl.BlockSpec((1, SC_WINDOW), lambda i: (0, i))],
            out_specs=[pl.BlockSpec((SC_WINDOW, width), lambda i: (i, 0))],
            core_axis_name=("c", "s"),
            dimension_semantics=(pltpu.PARALLEL,),
        )(i_hbm, o_hbm)

    return gather_rows(table, idx.reshape(1, m))


def _sc_scatter(rows, idx, n_out):
    t, width = rows.shape
    picks = idx.shape[0]
    assert t % SC_WINDOW == 0
    mesh = plsc.VectorSubcoreMesh(core_axis_name="c", subcore_axis_name="s")

    @pl.kernel(out_type=jax.ShapeDtypeStruct((n_out, width), rows.dtype), mesh=mesh)
    def scatter_rows(x_hbm, i_hbm, o_hbm):
        def body(x_vmem, i_vmem):
            for k in range(picks):
                pltpu.sync_copy(x_vmem, o_hbm.at[i_vmem.at[k]])

        pltpu.emit_pipeline(
            body,
            grid=(t // SC_WINDOW,),
            in_specs=[pl.BlockSpec((SC_WINDOW, width), lambda i: (i, 0)),
                      pl.BlockSpec((picks, SC_WINDOW), lambda i: (0, i))],
            out_specs=[],
            core_axis_name=("c", "s"),
            dimension_semantics=(pltpu.PARALLEL,),
        )(x_hbm, i_hbm)

    return scatter_rows(rows, idx)


def _dispatch(h2p, pos_t, n_rows):
    n = h2p.shape[1]
    idx = jnp.concatenate([pos_t, pos_t + n_rows], axis=1)
    return _sc_scatter(h2p.reshape(2 * n, QUART), idx, 2 * n_rows).reshape(2, n_rows, QUART)


def _collect(y, pos_t):
    n_rows = y.shape[1]
    flat = pos_t.reshape(-1)
    out = _sc_gather(y.reshape(2 * n_rows, QUART), jnp.concatenate([flat, flat + n_rows]))
    return out.reshape(2, TOP_K, pos_t.shape[1], QUART)


def kernel(x, c, norm1_g, norm2_g, ada_w, ada_b, w_in, gate_b, fox_fb, lam_q1, lam_k1, lam_q2, lam_k2,
           diff_subln_g, w_branch, w_out, router_w, router_b, exp_w_up, exp_b_up, exp_w_down,
           exp_b_down, final_g):
    batch, seq, d = x.shape
    depth = ada_w.shape[0]
    n = batch * seq
    x2 = x.reshape(n, d)

    mod = _ada_mod(c, ada_w, ada_b)
    tables = _retention_tables(seq)
    n_rows = n * TOP_K + N_EXPERTS * EXPERT_TILE
    blk = BRANCH_WIDTH // LANES

    pending = None
    for l in range(depth):
        lam_init = 0.8 - 0.6 * math.exp(-0.3 * l)
        sh1, sc1, g1, sh2, sc2, g2 = [m.reshape(batch, 1, d) for m in jnp.split(mod[l], 6, axis=-1)]

        w_main, w_f = _prep_win(w_in, l)
        outs = _inproj(x2, norm1_g[l].reshape(1, d), sh1, sc1, w_main, w_f, seq, pending)
        proj, flog = outs[0], outs[1]
        if pending is not None:
            x2 = outs[2]

        fb_pad = jnp.pad(fox_fb[l], (0, LANES - FOX_HEADS)).reshape(1, LANES)
        frow = _fcum(flog, fb_pad, batch, seq).reshape(batch * blk, 2, seq)

        lamv = jnp.stack([lam_q1[l], lam_k1[l], lam_q2[l], lam_k2[l]])
        o_diff = _attention(proj, "diff", 0, blk, 2 * blk, batch, seq,
                            (lamv, diff_subln_g[l].reshape(1, LANES)), lam_init)
        o_ret = _retention(proj, tables, batch, seq)
        o_fox = _attention(proj, "fox", 7 * blk, 8 * blk, 9 * blk, batch, seq, (frow,))

        rw = jnp.pad(router_w[l], ((0, 0), (0, LANES - N_EXPERTS)))
        rw_hi = rw.astype(BF16)
        rw = jnp.stack([rw_hi, (rw - rw_hi.astype(F32)).astype(BF16)])
        rb = jnp.pad(router_b[l], (0, LANES - N_EXPERTS), constant_values=NEG_INF).reshape(1, LANES)
        x2, h2p, top_idx, top_w, counts = _merge(
            o_diff, o_ret, o_fox, proj, gate_b[l].reshape(N_BRANCHES, 1, d), w_branch[l].astype(BF16),
            w_out[l].astype(BF16), x2, g1, norm2_g[l].reshape(1, d), sh2, sc2, rw, rb, seq)

        pos_t, tile_e, n_used = _route(top_idx, counts, n_rows)
        xs = _dispatch(h2p, pos_t, n_rows)

        w_up = _prep_up(exp_w_up, l)
        bu = exp_b_up[l]
        b_up = jnp.concatenate([bu[:, 0::2], bu[:, 1::2]], axis=-1).reshape(N_EXPERTS, 1, 2 * D_FF)
        y = _experts(tile_e, n_used, xs, w_up, b_up, exp_w_down, l,
                     exp_b_down[l].reshape(N_EXPERTS, 1, d))
        pending = (_collect(y, pos_t), top_w, g2)

    ytok, top_w, g2 = pending
    x2 = _combine(ytok, top_w, x2, g2, final_g.reshape(1, d), seq)
    return x2.reshape(batch, seq, d)
```

```python
import functools
import math

import jax
import jax.numpy as jnp
from jax import lax
from jax.experimental import pallas as pl
from jax.experimental.pallas import tpu as pltpu
from jax.experimental.pallas import tpu_sc as plsc

F32 = jnp.float32
BF16 = jnp.bfloat16
I32 = jnp.int32
U32 = jnp.uint32

D_MODEL = 1024
HEAD_DIM = 64
BRANCH_WIDTH = D_MODEL // 2
N_BRANCHES = 3
FOX_HEADS = BRANCH_WIDTH // HEAD_DIM
RET_HEADS = 4
RET_HEAD_DIM = 128
ROPE_BASE = 10000.0
RET_GAMMA_EXP0 = 5.0
N_EXPERTS = 32
TOP_K = 4
D_FF = D_MODEL
SWIGLU_LIMIT = 7.0
SWIGLU_ALPHA = 1.702
NORM_EPS = 1e-5
NEG_INF = -1e30

LANES = 128
MAIN_COLS = 8192
GATE_COL0 = 10 * BRANCH_WIDTH
HALF = D_MODEL // 2
QUART = HALF // 2
SC_WINDOW = 128
SC_WORKERS = 32
VMEM_LIMIT = 48 * 1024 * 1024
VMEM_LIMIT_INPROJ = 56 * 1024 * 1024

ROW_TILE_IN = 512
COL_TILE_IN = 1024
ATTN_TILE = 256
RET_CHUNK = 256
ROW_TILE_OUT = 256
MERGE_TILE = 1024
MERGE_SUB = 256
EXPERT_TILE = 256
EXPERT_GROUP = 2
CUM_CHUNK = 256


def _cparams(sem):
    return pltpu.CompilerParams(dimension_semantics=sem, vmem_limit_bytes=VMEM_LIMIT)


def _sigmoid(x):
    return 0.5 * jnp.tanh(0.5 * x) + 0.5


def _pack_pairs(v):
    u = lax.bitcast_convert_type(v.astype(BF16).astype(F32), U32)
    half = v.shape[-1] // 2
    w = (u[:, half:] & jnp.uint32(0xFFFF0000)) | (u[:, :half] >> 16)
    return lax.bitcast_convert_type(w, I32)


def _unpack_pairs(w):
    u = lax.bitcast_convert_type(w, U32)
    lo = lax.bitcast_convert_type(u << 16, F32)
    hi = lax.bitcast_convert_type(u & jnp.uint32(0xFFFF0000), F32)
    return lo, hi


def _moe_residual(y_ref, tw_ref, x_ref, g2_ref):
    tw = tw_ref[...]
    parts = [None] * 4
    for k in range(TOP_K):
        w = tw[:, k:k + 1]
        for piece in range(2):
            lo, hi = _unpack_pairs(y_ref[piece, k])
            for slot, v in ((piece, lo), (2 + piece, hi)):
                parts[slot] = w * v if parts[slot] is None else parts[slot] + w * v
    return x_ref[...] + g2_ref[0] * jnp.concatenate(parts, axis=-1)


def _ada_kernel(c_ref, w_ref, b_ref, o_ref):
    c = c_ref[...]
    ca = c * _sigmoid(c)
    o_ref[0] = jnp.dot(ca, w_ref[0], preferred_element_type=F32) + b_ref[0]


def _ada_mod(c, ada_w, ada_b):
    depth, d, cols = ada_w.shape
    b = c.shape[0]
    tn = 1024
    return pl.pallas_call(
        _ada_kernel,
        grid=(depth, cols // tn),
        in_specs=[
            pl.BlockSpec((b, d), lambda l, j: (0, 0)),
            pl.BlockSpec((1, d, tn), lambda l, j: (l, 0, j)),
            pl.BlockSpec((1, 1, tn), lambda l, j: (l, 0, j)),
        ],
        out_specs=pl.BlockSpec((1, b, tn), lambda l, j: (l, 0, j)),
        out_shape=jax.ShapeDtypeStruct((depth, b, cols), F32),
        compiler_params=_cparams(("arbitrary", "arbitrary")),
        name="ada_mod",
    )(c, ada_w, ada_b.reshape(depth, 1, cols))


def _prep_win_kernel(w_ref, wm_ref, wf_ref):
    wm_ref[:GATE_COL0, :] = w_ref[0, :GATE_COL0, :].astype(BF16)
    wm_ref[GATE_COL0:, :] = w_ref[0, GATE_COL0 + FOX_HEADS:, :].astype(BF16)
    wf_ref[...] = jnp.zeros(wf_ref.shape, BF16)
    wf_ref[:FOX_HEADS, :] = w_ref[0, GATE_COL0:GATE_COL0 + FOX_HEADS, :].astype(BF16)


def _prep_win(w_t, layer):
    _, cols, d = w_t.shape
    ct = 256
    return pl.pallas_call(
        _prep_win_kernel,
        grid=(d // ct,),
        in_specs=[pl.BlockSpec((1, cols, ct), lambda i: (layer, 0, i))],
        out_specs=[pl.BlockSpec((MAIN_COLS, ct), lambda i: (0, i)), pl.BlockSpec((LANES, ct), lambda i: (0, i))],
        out_shape=[jax.ShapeDtypeStruct((MAIN_COLS, d), BF16), jax.ShapeDtypeStruct((LANES, d), BF16)],
        compiler_params=_cparams(("arbitrary",)),
        name="inproj_weight_prep",
    )(w_t)


def _inproj_kernel(*refs, moe):
    if moe:
        y_ref, tw_ref, x_ref, g2_ref, g_ref, sh_ref, sc_ref, w_ref, wf_ref, proj_ref, fl_ref, xo_ref = refs
        x = _moe_residual(y_ref, tw_ref, x_ref, g2_ref)
        xo_ref[...] = x
    else:
        x_ref, g_ref, sh_ref, sc_ref, w_ref, wf_ref, proj_ref, fl_ref = refs
        x = x_ref[...]
    r = lax.rsqrt(jnp.mean(x * x, axis=-1, keepdims=True) + NORM_EPS)
    h = x * r * g_ref[...]
    h = h * (1.0 + sc_ref[0]) + sh_ref[0]
    hb = h.astype(BF16)
    nt = (((1,), (1,)), ((), ()))
    fl_ref[...] = lax.dot_general(hb, wf_ref[...], nt, preferred_element_type=F32)
    for j in range(MAIN_COLS // COL_TILE_IN):
        cs = slice(j * COL_TILE_IN, (j + 1) * COL_TILE_IN)
        proj_ref[:, cs] = lax.dot_general(hb, w_ref[cs, :], nt, preferred_element_type=F32).astype(BF16)


def _inproj(x2, g, sh, sc, w_main, w_f, seq, moe=None):
    n, d = x2.shape
    tm = min(ROW_TILE_IN, seq)
    per_seq = seq // tm
    resident = pl.Buffered(1)

    def per_b():
        return pl.BlockSpec((1, 1, d), lambda i: (i // per_seq, 0, 0))

    in_specs = [
        pl.BlockSpec((tm, d), lambda i: (i, 0)),
        pl.BlockSpec((1, d), lambda i: (0, 0)),
        per_b(), per_b(),
        pl.BlockSpec((MAIN_COLS, d), lambda i: (0, 0), pipeline_mode=resident),
        pl.BlockSpec((LANES, d), lambda i: (0, 0), pipeline_mode=resident),
    ]
    out_specs = [pl.BlockSpec((tm, MAIN_COLS), lambda i: (i, 0)), pl.BlockSpec((tm, LANES), lambda i: (i, 0))]
    out_shape = [jax.ShapeDtypeStruct((n, MAIN_COLS), BF16), jax.ShapeDtypeStruct((n, LANES), F32)]
    args = [x2, g, sh, sc, w_main, w_f]
    if moe is not None:
        ytok, top_w, g2 = moe
        in_specs = [pl.BlockSpec((2, TOP_K, tm, QUART), lambda i: (0, 0, i, 0)),
                    pl.BlockSpec((tm, LANES), lambda i: (i, 0)), in_specs[0], per_b()] + in_specs[1:]
        args = [ytok, top_w, x2, g2] + args[1:]
        out_specs.append(pl.BlockSpec((tm, d), lambda i: (i, 0)))
        out_shape.append(jax.ShapeDtypeStruct((n, d), F32))
    return pl.pallas_call(
        functools.partial(_inproj_kernel, moe=moe is not None),
        grid=(n // tm,),
        in_specs=in_specs,
        out_specs=out_specs,
        out_shape=out_shape,
        compiler_params=pltpu.CompilerParams(dimension_semantics=("arbitrary",), vmem_limit_bytes=VMEM_LIMIT_INPROJ),
        name="inproj",
    )(*args)


def _split3(v):
    hi = v.astype(BF16)
    r1 = v - hi.astype(F32)
    mid = r1.astype(BF16)
    lo = (r1 - mid.astype(F32)).astype(BF16)
    return hi, mid, lo


def _fcum_kernel(fl_ref, fb_ref, o_ref, *, seq):
    c = min(CUM_CHUNK, seq)
    row = lax.broadcasted_iota(I32, (c, c), 0)
    col = lax.broadcasted_iota(I32, (c, c), 1)
    lower = (col <= row).astype(BF16)
    carry = jnp.zeros((1, LANES), F32)
    for ci in range(seq // c):
        x = fl_ref[pl.ds(ci * c, c), :] + fb_ref[...]
        lf = jnp.minimum(x, 0.0) - jnp.log(1.0 + jnp.exp(-jnp.abs(x)))
        hi, mid, lo = _split3(lf)
        cs = (jnp.dot(lower, hi, preferred_element_type=F32)
              + jnp.dot(lower, mid, preferred_element_type=F32)
              + jnp.dot(lower, lo, preferred_element_type=F32)) + carry
        o_ref[0, :, pl.ds(ci * c, c)] = cs.T[:FOX_HEADS, :]
        carry = cs[c - 1:c, :]


def _fcum(flog, fb_pad, batch, seq):
    return pl.pallas_call(
        functools.partial(_fcum_kernel, seq=seq),
        grid=(batch,),
        in_specs=[
            pl.BlockSpec((seq, LANES), lambda b: (b, 0)),
            pl.BlockSpec((1, LANES), lambda b: (0, 0)),
        ],
        out_specs=pl.BlockSpec((1, FOX_HEADS, seq), lambda b: (b, 0, 0)),
        out_shape=jax.ShapeDtypeStruct((batch, FOX_HEADS, seq), F32),
        compiler_params=_cparams(("arbitrary",)),
        name="fox_decay_cumsum",
    )(flog, fb_pad)


def _attn_kernel(*refs, mode, t, seq, lam_init):
    if mode == "diff":
        q_ref, k_ref, v_ref, lam_ref, g_ref, o_ref, vx_scr = refs
        f_ref = None
        lv = lam_ref[...]
        lam = (jnp.exp(jnp.sum(lv[0:1, :] * lv[1:2, :])) - jnp.exp(jnp.sum(lv[2:3, :] * lv[3:4, :]))
               + lam_init)
    else:
        q_ref, k_ref, v_ref, f_ref, o_ref, vx_scr = refs
    lane = lax.broadcasted_iota(I32, (t, LANES), 1)
    row = lax.broadcasted_iota(I32, (t, t), 0)
    col = lax.broadcasted_iota(I32, (t, t), 1)
    nt = (((1,), (1,)), ((), ()))

    v_all = v_ref[...]
    one = jnp.ones_like(v_all)
    if mode == "diff":
        vx_scr[0] = jnp.concatenate([v_all, one], axis=1)
    else:
        lane_s = lax.broadcasted_iota(I32, v_all.shape, 1)
        vx_scr[0] = jnp.where(lane_s < HEAD_DIM, v_all, one)
        vx_scr[1] = jnp.where(lane_s >= HEAD_DIM, v_all, one)

    n_strips = seq // t
    pairs = [(i, n_strips - 1 - i) for i in range(n_strips // 2)] if n_strips % 2 == 0 else [(i,) for i in range(n_strips)]
    for group in pairs:
        units = [(qi, idx) for qi in group for idx in range(2)]
        qms = {}
        for qi in group:
            q = q_ref[qi * t:(qi + 1) * t, :] * jnp.asarray(HEAD_DIM ** -0.5, BF16)
            zero = jnp.zeros_like(q)
            qms[qi, 0] = jnp.where(lane < HEAD_DIM, q, zero)
            qms[qi, 1] = jnp.where(lane >= HEAD_DIM, q, zero)

        def scores(u, k0, k1):
            s_ = lax.dot_general(qms[u], k_ref[k0:k1, :], nt, preferred_element_type=F32)
            if f_ref is not None:
                s_ = s_ - f_ref[0, u[1]:u[1] + 1, k0:k1]
            return s_

        sds = {u: jnp.where(col <= row, scores(u, u[0] * t, (u[0] + 1) * t), NEG_INF) for u in units}
        sms = {u: scores(u, 0, u[0] * t) for u in units if u[0] > 0}
        ms = {}
        for u in units:
            m = jnp.max(sds[u], axis=-1, keepdims=True)
            if u in sms:
                m = jnp.maximum(m, jnp.max(sms[u], axis=-1, keepdims=True))
            ms[u] = m
        pds = {u: jnp.exp((sds[u] - ms[u]).astype(BF16)) for u in units}
        pms = {u: jnp.exp((sms[u] - ms[u]).astype(BF16)) for u in sms}
        outs = {}
        for u in units:
            vx = vx_scr.at[0 if mode == "diff" else u[1]]
            lo = u[0] * t
            acc = jnp.dot(pds[u], vx[lo:lo + t, :], preferred_element_type=F32)
            if u in pms:
                acc = acc + jnp.dot(pms[u], vx[0:lo, :], preferred_element_type=F32)
            outs[u] = acc
        for qi in group:
            o0, o1 = outs[qi, 0], outs[qi, 1]
            if mode == "diff":
                a1 = o0[:, :LANES] / o0[:, LANES:]
                a2 = o1[:, :LANES] / o1[:, LANES:]
                o = a1 - lam * a2
                r = lax.rsqrt(jnp.mean(o * o, axis=-1, keepdims=True) + NORM_EPS)
                o = o * r * g_ref[...] * (1.0 - lam_init)
            else:
                num = jnp.where(lane < HEAD_DIM, o0, o1)
                den = jnp.where(lane < HEAD_DIM, pltpu.roll(o0, HEAD_DIM, 1), pltpu.roll(o1, HEAD_DIM, 1))
                o = num / den
            o_ref[qi * t:(qi + 1) * t, :] = o.astype(BF16)


def _attention(proj, mode, col_q, col_k, col_v, batch, seq, extra, lam_init=0.0):
    n = proj.shape[0]
    t = min(ATTN_TILE, seq)
    n_blk = BRANCH_WIDTH // LANES
    in_specs = [
        pl.BlockSpec((seq, LANES), lambda b, h: (b, col_q + h)),
        pl.BlockSpec((seq, LANES), lambda b, h: (b, col_k + h)),
        pl.BlockSpec((seq, LANES), lambda b, h: (b, col_v + h)),
    ]
    if mode == "diff":
        lamv, g = extra
        in_specs += [pl.BlockSpec((4, HEAD_DIM), lambda b, h: (0, 0)),
                     pl.BlockSpec((1, LANES), lambda b, h: (0, 0))]
        args = (proj, proj, proj, lamv, g)
    else:
        (frow,) = extra
        in_specs += [pl.BlockSpec((1, 2, seq), lambda b, h: (b * n_blk + h, 0, 0))]
        args = (proj, proj, proj, frow)
    return pl.pallas_call(
        functools.partial(_attn_kernel, mode=mode, t=t, seq=seq, lam_init=lam_init),
        grid=(batch, n_blk),
        in_specs=in_specs,
        out_specs=pl.BlockSpec((seq, LANES), lambda b, h: (b, h)),
        out_shape=jax.ShapeDtypeStruct((n, BRANCH_WIDTH), BF16),
        scratch_shapes=[pltpu.VMEM((1, seq, 2 * LANES) if mode == "diff" else (2, seq, LANES), BF16)],
        compiler_params=_cparams(("arbitrary", "arbitrary")),
        name="attn_" + mode,
    )(*args)


def _ret_kernel(q_ref, k_ref, v_ref, g_ref, cos_ref, sin_ref, dm_ref, qd_ref, kd_ref, cd_ref,
                o_ref, *, seq, c):
    st = jnp.zeros((RET_HEAD_DIM, RET_HEAD_DIM), F32)
    half = RET_HEAD_DIM // 2

    for ci in range(seq // c):
        sl = pl.ds(ci * c, c)
        cos = cos_ref[sl, :]
        sin = sin_ref[sl, :]
        q = q_ref[sl, :].astype(F32)
        k = k_ref[sl, :].astype(F32)
        qr = q * cos + pltpu.roll(q, half, 1) * sin
        kr = (k * cos + pltpu.roll(k, half, 1) * sin) * (RET_HEAD_DIM ** -0.5)
        v = v_ref[sl, :]
        sc = lax.dot_general(qr.astype(BF16), kr.astype(BF16), (((1,), (1,)), ((), ())),
                             preferred_element_type=F32) * dm_ref[0]
        inner = jnp.dot(sc.astype(BF16), v, preferred_element_type=F32)
        cross = jnp.dot((qr * qd_ref[0]).astype(BF16), st.astype(BF16), preferred_element_type=F32)
        o = inner + cross
        kt = (kr * kd_ref[0]).T.astype(BF16)
        st = st * cd_ref[0] + jnp.dot(kt, v, preferred_element_type=F32)
        mu = jnp.mean(o, axis=-1, keepdims=True)
        oc = o - mu
        var = jnp.mean(oc * oc, axis=-1, keepdims=True)
        on = oc * lax.rsqrt(var + NORM_EPS)
        g = g_ref[sl, :].astype(F32)
        o_ref[sl, :] = (on * (g * _sigmoid(g))).astype(BF16)


def _retention(proj, tables, batch, seq):
    n = proj.shape[0]
    c = min(RET_CHUNK, seq)
    cos, sin, dm, qd, kd, cd = tables
    col = BRANCH_WIDTH * 3 // LANES
    step = BRANCH_WIDTH // LANES

    def pspec(k):
        return pl.BlockSpec((seq, LANES), lambda b, h: (b, col + k * step + h))

    return pl.pallas_call(
        functools.partial(_ret_kernel, seq=seq, c=c),
        grid=(batch, RET_HEADS),
        in_specs=[
            pspec(0), pspec(1), pspec(2), pspec(3),
            pl.BlockSpec((seq, LANES), lambda b, h: (0, 0)),
            pl.BlockSpec((seq, LANES), lambda b, h: (0, 0)),
            pl.BlockSpec((1, c, c), lambda b, h: (h, 0, 0)),
            pl.BlockSpec((1, c, LANES), lambda b, h: (h, 0, 0)),
            pl.BlockSpec((1, c, LANES), lambda b, h: (h, 0, 0)),
            pl.BlockSpec((1, LANES, LANES), lambda b, h: (h, 0, 0)),
        ],
        out_specs=pl.BlockSpec((seq, LANES), lambda b, h: (b, h)),
        out_shape=jax.ShapeDtypeStruct((n, BRANCH_WIDTH), BF16),
        compiler_params=_cparams(("arbitrary", "arbitrary")),
        name="retention",
    )(proj, proj, proj, proj, cos, sin, dm, qd, kd, cd)


def _retention_tables(seq):
    c = min(RET_CHUNK, seq)
    d = RET_HEAD_DIM
    inv = ROPE_BASE ** (-jnp.arange(0, d, 2, dtype=F32) / d)
    ang = jnp.arange(seq).astype(F32)[:, None] * inv[None, :]
    cos = jnp.cos(ang)
    sin = jnp.sin(ang)
    cos_full = jnp.concatenate([cos, cos], axis=-1)
    sin_signed = jnp.concatenate([-sin, sin], axis=-1)
    log_gamma = jnp.log1p(-jnp.exp2(-RET_GAMMA_EXP0 - jnp.arange(RET_HEADS, dtype=F32)))
    idx = jnp.arange(c, dtype=F32)
    dist = idx[:, None] - idx[None, :]
    dm = jnp.where(dist >= 0, jnp.exp(log_gamma[:, None, None] * jnp.maximum(dist, 0.0)), 0.0)
    qd = jnp.exp(log_gamma[:, None] * (idx + 1.0))
    kd = jnp.exp(log_gamma[:, None] * (c - 1.0 - idx))
    cd = jnp.exp(log_gamma * c)
    qd = jnp.broadcast_to(qd[:, :, None], (RET_HEADS, c, LANES))
    kd = jnp.broadcast_to(kd[:, :, None], (RET_HEADS, c, LANES))
    cd = jnp.broadcast_to(cd[:, None, None], (RET_HEADS, LANES, LANES))
    return cos_full, sin_signed, dm, qd, kd, cd


def _merge_kernel(od_ref, or_ref, of_ref, gd_ref, gr_ref, gf_ref, gb_ref, wb_ref, wo_ref, x_ref,
                  g1_ref, n2_ref, sh_ref, sc_ref, rw_ref, rb_ref,
                  xo_ref, hp_ref, ti_ref, tw_ref, cnt_ref, cnt_scr, *, tm, sub):
    @pl.when(pl.program_id(0) == 0)
    def _():
        cnt_scr[...] = jnp.zeros(cnt_scr.shape, F32)

    lane = lax.broadcasted_iota(I32, (sub, LANES), 1)
    r_i = lax.broadcasted_iota(I32, (sub, sub), 0)
    c_i = lax.broadcasted_iota(I32, (sub, sub), 1)
    earlier = (c_i < r_i).astype(BF16)
    counts = cnt_scr[...]

    blocks = [slice(s0, s0 + sub) for s0 in range(0, tm, sub)]

    def phase_branches(rs):
        merged = None
        for i, (o_ref, gate_ref) in enumerate(((od_ref, gd_ref), (or_ref, gr_ref), (of_ref, gf_ref))):
            gate = _sigmoid(gate_ref[rs, :] + gb_ref[i].astype(BF16))
            term = gate.astype(F32) * jnp.dot(o_ref[rs, :], wb_ref[i], preferred_element_type=F32)
            merged = term if merged is None else merged + term
        return merged.astype(BF16)

    def phase_outproj(rs, merged):
        mix = jnp.dot(merged, wo_ref[...], preferred_element_type=F32)
        xn = x_ref[rs, :] + g1_ref[0] * mix
        xo_ref[rs, :] = xn
        r = lax.rsqrt(jnp.mean(xn * xn, axis=-1, keepdims=True) + NORM_EPS)
        h = xn * r * n2_ref[...]
        h = h * (1.0 + sc_ref[0]) + sh_ref[0]
        hp = _pack_pairs(h)
        hp_ref[0, rs, :] = hp[:, :QUART]
        hp_ref[1, rs, :] = hp[:, QUART:]
        hh, hm, _ = _split3(h)
        return (jnp.dot(hh, rw_ref[0], preferred_element_type=F32)
                + jnp.dot(hh, rw_ref[1], preferred_element_type=F32)
                + jnp.dot(hm, rw_ref[0], preferred_element_type=F32)) + rb_ref[...]

    def phase_topk(rs, logits, counts):
        vals, idxs = [], []
        for _ in range(TOP_K):
            m = jnp.max(logits, axis=-1, keepdims=True)
            idx = jnp.min(jnp.where(logits == m, lane, LANES), axis=-1, keepdims=True)
            vals.append(m)
            idxs.append(idx)
            logits = jnp.where(lane == idx, -3.0e38, logits)
        es = [jnp.exp(v - vals[0]) for v in vals]
        den = es[0] + es[1] + es[2] + es[3]
        hits = [lane == idxs[k] for k in range(TOP_K)]
        picked = (hits[0] | hits[1] | hits[2] | hits[3])
        base = jnp.dot(earlier, picked.astype(BF16), preferred_element_type=F32) + counts
        counts = counts + jnp.sum(picked.astype(F32), axis=0, keepdims=True)
        ti = jnp.zeros((sub, LANES), I32)
        tw = jnp.zeros((sub, LANES), F32)
        for k in range(TOP_K):
            rank = jnp.sum(jnp.where(hits[k], base, 0.0), axis=-1, keepdims=True).astype(I32)
            ti = jnp.where(lane == k, idxs[k], ti)
            ti = jnp.where(lane == TOP_K + k, rank, ti)
            tw = jnp.where(lane == k, es[k] / den, tw)
        ti_ref[rs, :] = ti
        tw_ref[rs, :] = tw
        return counts

    merged = [phase_branches(rs) for rs in blocks]
    logits = [phase_outproj(rs, m) for rs, m in zip(blocks, merged)]
    for rs, lg in zip(blocks, logits):
        counts = phase_topk(rs, lg, counts)

    cnt_scr[...] = counts
    cnt_ref[...] = counts.astype(I32)


def _merge(o_diff, o_ret, o_fox, proj, gate_b, w_branch, w_out, x2, g1, n2g, sh2, sc2, rw, rb, seq):
    n, d = x2.shape
    tm = min(MERGE_TILE, seq)
    sub = min(MERGE_SUB, tm)
    per_seq = seq // tm
    gcol = GATE_COL0 // d

    def rows(w):
        return pl.BlockSpec((tm, w), lambda i: (i, 0))

    def gspec(k):
        return pl.BlockSpec((tm, d), lambda i: (i, gcol + k))

    def whole(shape):
        return pl.BlockSpec(shape, lambda i: (0,) * len(shape))

    def per_b():
        return pl.BlockSpec((1, 1, d), lambda i: (i // per_seq, 0, 0))

    return pl.pallas_call(
        functools.partial(_merge_kernel, tm=tm, sub=sub),
        grid=(n // tm,),
        in_specs=[
            rows(BRANCH_WIDTH), rows(BRANCH_WIDTH), rows(BRANCH_WIDTH),
            gspec(0), gspec(1), gspec(2),
            whole((N_BRANCHES, 1, d)), whole((N_BRANCHES, BRANCH_WIDTH, d)), whole((d, d)),
            rows(d), per_b(), whole((1, d)), per_b(), per_b(),
            whole((2, d, LANES)), whole((1, LANES)),
        ],
        out_specs=[rows(d), pl.BlockSpec((2, tm, QUART), lambda i: (0, i, 0)), rows(LANES), rows(LANES),
                   whole((1, LANES))],
        out_shape=[
            jax.ShapeDtypeStruct((n, d), F32),
            jax.ShapeDtypeStruct((2, n, QUART), I32),
            jax.ShapeDtypeStruct((n, LANES), I32),
            jax.ShapeDtypeStruct((n, LANES), F32),
            jax.ShapeDtypeStruct((1, LANES), I32),
        ],
        scratch_shapes=[pltpu.VMEM((1, LANES), F32)],
        compiler_params=_cparams(("arbitrary",)),
        name="merge_outproj_router",
    )(o_diff, o_ret, o_fox, proj, proj, proj, gate_b, w_branch, w_out, x2, g1, n2g, sh2, sc2, rw, rb)


def _prep_up_kernel(w_ref, o_ref, t_scr):
    f = t_scr.shape[1] // 2
    for g in range(t_scr.shape[0]):
        cols = slice(g * LANES, (g + 1) * LANES)
        t_scr[g] = w_ref[0, 0, cols, :].T
        o_ref[0, 0:f, cols] = t_scr[g, pl.ds(0, f, stride=2), :].astype(BF16)
        o_ref[0, f:, cols] = t_scr[g, pl.ds(1, f, stride=2), :].astype(BF16)


def _prep_up(w_up, layer):
    _, e, d, f2 = w_up.shape
    groups = d // LANES
    kt = groups * LANES
    return pl.pallas_call(
        _prep_up_kernel,
        grid=(e, d // kt),
        in_specs=[pl.BlockSpec((1, 1, kt, f2), lambda i, j: (layer, i, j, 0))],
        out_specs=pl.BlockSpec((1, f2, kt), lambda i, j: (i, 0, j)),
        out_shape=jax.ShapeDtypeStruct((e, f2, d), BF16),
        scratch_shapes=[pltpu.VMEM((groups, f2, LANES), F32)],
        compiler_params=_cparams(("arbitrary", "arbitrary")),
        name="expert_up_weight_prep",
    )(w_up)


def _expert_kernel(te_ref, nu_ref, xs_ref, *refs):
    y_ref = refs[-1]
    nt = (((1,), (1,)), ((), ()))
    first_tile = pl.program_id(0) * EXPERT_GROUP

    @pl.when(first_tile < nu_ref[0])
    def _():
        for s in range(EXPERT_GROUP):
            wu_ref, bu_ref, wd_ref, bd_ref = refs[4 * s:4 * s + 4]
            rs = slice(s * EXPERT_TILE, (s + 1) * EXPERT_TILE)
            lo, hi = _unpack_pairs(jnp.concatenate([xs_ref[0, rs, :], xs_ref[1, rs, :]], axis=1))
            lo = lo.astype(BF16)
            hi = hi.astype(BF16)

            def up_half(r0):
                return (lax.dot_general(lo, wu_ref[0, r0:r0 + D_FF, :HALF], nt, preferred_element_type=F32)
                        + lax.dot_general(hi, wu_ref[0, r0:r0 + D_FF, HALF:], nt, preferred_element_type=F32))

            glu = jnp.minimum(up_half(0) + bu_ref[0, :, :D_FF], SWIGLU_LIMIT)
            lin = jnp.clip(up_half(D_FF) + bu_ref[0, :, D_FF:], -SWIGLU_LIMIT, SWIGLU_LIMIT)
            act = glu * _sigmoid(SWIGLU_ALPHA * glu) * (lin + 1.0)
            down = jnp.dot(act.astype(BF16), wd_ref[0, 0].astype(BF16), preferred_element_type=F32) + bd_ref[0]
            yp = _pack_pairs(down)
            y_ref[0, rs, :] = yp[:, :QUART]
            y_ref[1, rs, :] = yp[:, QUART:]

    @pl.when(first_tile >= nu_ref[0])
    def _():
        y_ref[...] = jnp.zeros(y_ref.shape, I32)


def _experts(tile_e, n_used, xs, w_up, b_up, w_down, layer, b_down):
    r = xs.shape[1]
    tm = EXPERT_TILE * EXPERT_GROUP
    d = D_MODEL
    in_specs = [pl.BlockSpec((2, tm, QUART), lambda i, te, nu: (0, i, 0))]
    args = [xs]
    for s in range(EXPERT_GROUP):
        def pick(i, te, nu, s=s):
            return (te[i * EXPERT_GROUP + s], 0, 0)

        def pick_down(i, te, nu, s=s):
            return (layer, te[i * EXPERT_GROUP + s], 0, 0)
        in_specs += [pl.BlockSpec((1, 2 * D_FF, d), pick), pl.BlockSpec((1, 1, 2 * D_FF), pick),
                     pl.BlockSpec((1, 1, D_FF, d), pick_down), pl.BlockSpec((1, 1, d), pick)]
        args += [w_up, b_up, w_down, b_down]
    grid_spec = pltpu.PrefetchScalarGridSpec(
        num_scalar_prefetch=2,
        grid=(r // tm,),
        in_specs=in_specs,
        out_specs=pl.BlockSpec((2, tm, QUART), lambda i, te, nu: (0, i, 0)),
    )
    return pl.pallas_call(
        _expert_kernel,
        grid_spec=grid_spec,
        out_shape=jax.ShapeDtypeStruct((2, r, QUART), I32),
        compiler_params=_cparams(("arbitrary",)),
        name="expert_mlp",
    )(tile_e, n_used, *args)


def _combine_kernel(y_ref, tw_ref, x_ref, g2_ref, fg_ref, o_ref):
    xn = _moe_residual(y_ref, tw_ref, x_ref, g2_ref)
    r = lax.rsqrt(jnp.mean(xn * xn, axis=-1, keepdims=True) + NORM_EPS)
    o_ref[...] = xn * r * fg_ref[...]


def _combine(ytok, top_w, x2, g2, final_g, seq):
    n, d = x2.shape
    tm = min(ROW_TILE_OUT, seq)
    per_seq = seq // tm
    return pl.pallas_call(
        _combine_kernel,
        grid=(n // tm,),
        in_specs=[
            pl.BlockSpec((2, TOP_K, tm, QUART), lambda i: (0, 0, i, 0)),
            pl.BlockSpec((tm, LANES), lambda i: (i, 0)),
            pl.BlockSpec((tm, d), lambda i: (i, 0)),
            pl.BlockSpec((1, 1, d), lambda i: (i // per_seq, 0, 0)),
            pl.BlockSpec((1, d), lambda i: (0, 0)),
        ],
        out_specs=pl.BlockSpec((tm, d), lambda i: (i, 0)),
        out_shape=jax.ShapeDtypeStruct((n, d), F32),
        compiler_params=_cparams(("arbitrary",)),
        name="moe_combine",
    )(ytok, top_w, x2, g2, final_g)


def _route(top_idx, counts, n_rows):
    tm = EXPERT_TILE
    e = top_idx[:, :TOP_K]
    rank = top_idx[:, TOP_K:2 * TOP_K]
    sizes = counts[0, :N_EXPERTS]
    padded = ((sizes + tm - 1) // tm) * tm
    pend = jnp.cumsum(padded)
    pstart = pend - padded
    onehot = e[:, :, None] == jnp.arange(N_EXPERTS, dtype=I32)[None, None, :]
    pos = jnp.sum(jnp.where(onehot, pstart[None, None, :], 0), axis=-1) + rank
    tiles = jnp.arange(n_rows // tm, dtype=I32) * tm
    tile_e = jnp.minimum(jnp.sum(tiles[:, None] >= pend[None, :], axis=-1), N_EXPERTS - 1).astype(I32)
    n_used = (pend[-1] // tm).astype(I32).reshape(1)
    return pos.T.astype(I32), tile_e, n_used


def _sc_gather(table, idx):
    m = idx.shape[0]
    width = table.shape[1]
    assert m % (SC_WINDOW * SC_WORKERS) == 0
    mesh = plsc.VectorSubcoreMesh(core_axis_name="c", subcore_axis_name="s")

    @pl.kernel(out_type=jax.ShapeDtypeStruct((m, width), table.dtype), mesh=mesh)
    def gather_rows(t_hbm, i_hbm, o_hbm):
        def body(i_vmem, o_vmem):
            pltpu.sync_copy(t_hbm.at[i_vmem.at[0]], o_vmem)

        pltpu.emit_pipeline(
            body,
            grid=(m // SC_WINDOW,),
            in_specs=[pl.BlockSpec((1, SC_WINDOW), lambda i: (0, i))],
            out_specs=[pl.BlockSpec((SC_WINDOW, width), lambda i: (i, 0))],
            core_axis_name=("c", "s"),
            dimension_semantics=(pltpu.PARALLEL,),
        )(i_hbm, o_hbm)

    return gather_rows(table, idx.reshape(1, m))


def _sc_scatter(rows, idx, n_out):
    t, width = rows.shape
    picks = idx.shape[0]
    assert t % SC_WINDOW == 0
    mesh = plsc.VectorSubcoreMesh(core_axis_name="c", subcore_axis_name="s")

    @pl.kernel(out_type=jax.ShapeDtypeStruct((n_out, width), rows.dtype), mesh=mesh)
    def scatter_rows(x_hbm, i_hbm, o_hbm):
        def body(x_vmem, i_vmem):
            for k in range(picks):
                pltpu.sync_copy(x_vmem, o_hbm.at[i_vmem.at[k]])

        pltpu.emit_pipeline(
            body,
            grid=(t // SC_WINDOW,),
            in_specs=[pl.BlockSpec((SC_WINDOW, width), lambda i: (i, 0)),
                      pl.BlockSpec((picks, SC_WINDOW), lambda i: (0, i))],
            out_specs=[],
            core_axis_name=("c", "s"),
            dimension_semantics=(pltpu.PARALLEL,),
        )(x_hbm, i_hbm)

    return scatter_rows(rows, idx)


def _dispatch(h2p, pos_t, n_rows):
    n = h2p.shape[1]
    idx = jnp.concatenate([pos_t, pos_t + n_rows], axis=1)
    return _sc_scatter(h2p.reshape(2 * n, QUART), idx, 2 * n_rows).reshape(2, n_rows, QUART)


def _collect(y, pos_t):
    n_rows = y.shape[1]
    flat = pos_t.reshape(-1)
    out = _sc_gather(y.reshape(2 * n_rows, QUART), jnp.concatenate([flat, flat + n_rows]))
    return out.reshape(2, TOP_K, pos_t.shape[1], QUART)


def kernel(x, c, norm1_g, norm2_g, ada_w, ada_b, w_in, gate_b, fox_fb, lam_q1, lam_k1, lam_q2, lam_k2,
           diff_subln_g, w_branch, w_out, router_w, router_b, exp_w_up, exp_b_up, exp_w_down,
           exp_b_down, final_g):
    batch, seq, d = x.shape
    depth = ada_w.shape[0]
    n = batch * seq
    x2 = x.reshape(n, d)

    mod = _ada_mod(c, ada_w, ada_b)
    tables = _retention_tables(seq)
    n_rows = n * TOP_K + N_EXPERTS * EXPERT_TILE
    blk = BRANCH_WIDTH // LANES

    w_in_t = jnp.swapaxes(w_in, 1, 2)
    pending = None
    for l in range(depth):
        lam_init = 0.8 - 0.6 * math.exp(-0.3 * l)
        sh1, sc1, g1, sh2, sc2, g2 = [m.reshape(batch, 1, d) for m in jnp.split(mod[l], 6, axis=-1)]

        w_main, w_f = _prep_win(w_in_t, l)
        outs = _inproj(x2, norm1_g[l].reshape(1, d), sh1, sc1, w_main, w_f, seq, pending)
        proj, flog = outs[0], outs[1]
        if pending is not None:
            x2 = outs[2]

        fb_pad = jnp.pad(fox_fb[l], (0, LANES - FOX_HEADS)).reshape(1, LANES)
        frow = _fcum(flog, fb_pad, batch, seq).reshape(batch * blk, 2, seq)

        lamv = jnp.stack([lam_q1[l], lam_k1[l], lam_q2[l], lam_k2[l]])
        o_diff = _attention(proj, "diff", 0, blk, 2 * blk, batch, seq,
                            (lamv, diff_subln_g[l].reshape(1, LANES)), lam_init)
        o_ret = _retention(proj, tables, batch, seq)
        o_fox = _attention(proj, "fox", 7 * blk, 8 * blk, 9 * blk, batch, seq, (frow,))

        rw = jnp.pad(router_w[l], ((0, 0), (0, LANES - N_EXPERTS)))
        rw_hi = rw.astype(BF16)
        rw = jnp.stack([rw_hi, (rw - rw_hi.astype(F32)).astype(BF16)])
        rb = jnp.pad(router_b[l], (0, LANES - N_EXPERTS), constant_values=NEG_INF).reshape(1, LANES)
        x2, h2p, top_idx, top_w, counts = _merge(
            o_diff, o_ret, o_fox, proj, gate_b[l].reshape(N_BRANCHES, 1, d), w_branch[l].astype(BF16),
            w_out[l].astype(BF16), x2, g1, norm2_g[l].reshape(1, d), sh2, sc2, rw, rb, seq)

        pos_t, tile_e, n_used = _route(top_idx, counts, n_rows)
        xs = _dispatch(h2p, pos_t, n_rows)

        w_up = _prep_up(exp_w_up, l)
        bu = exp_b_up[l]
        b_up = jnp.concatenate([bu[:, 0::2], bu[:, 1::2]], axis=-1).reshape(N_EXPERTS, 1, 2 * D_FF)
        y = _experts(tile_e, n_used, xs, w_up, b_up, exp_w_down, l,
                     exp_b_down[l].reshape(N_EXPERTS, 1, d))
        pending = (_collect(y, pos_t), top_w, g2)

    ytok, top_w, g2 = pending
    x2 = _combine(ytok, top_w, x2, g2, final_g.reshape(1, d), seq)
    return x2.reshape(batch, seq, d)
```

```python
import functools
import math

import jax
import jax.numpy as jnp
from jax import lax
from jax.experimental import pallas as pl
from jax.experimental.pallas import tpu as pltpu
from jax.experimental.pallas import tpu_sc as plsc

F32 = jnp.float32
BF16 = jnp.bfloat16
I32 = jnp.int32
U32 = jnp.uint32

D_MODEL = 1024
HEAD_DIM = 64
BRANCH_WIDTH = D_MODEL // 2
N_BRANCHES = 3
FOX_HEADS = BRANCH_WIDTH // HEAD_DIM
RET_HEADS = 4
RET_HEAD_DIM = 128
ROPE_BASE = 10000.0
RET_GAMMA_EXP0 = 5.0
N_EXPERTS = 32
TOP_K = 4
D_FF = D_MODEL
SWIGLU_LIMIT = 7.0
SWIGLU_ALPHA = 1.702
NORM_EPS = 1e-5
NEG_INF = -1e30

LANES = 128
MAIN_COLS = 8192
GATE_COL0 = 10 * BRANCH_WIDTH
HALF = D_MODEL // 2
QUART = HALF // 2
SC_WINDOW = 128
SC_WORKERS = 32
VMEM_LIMIT = 48 * 1024 * 1024
VMEM_LIMIT_INPROJ = 56 * 1024 * 1024

ROW_TILE_IN = 512
COL_TILE_IN = 1024
ATTN_TILE = 256
RET_CHUNK = 256
ROW_TILE_OUT = 256
MERGE_TILE = 1024
MERGE_SUB = 256
EXPERT_TILE = 256
EXPERT_GROUP = 2
CUM_CHUNK = 256


def _cparams(sem):
    return pltpu.CompilerParams(dimension_semantics=sem, vmem_limit_bytes=VMEM_LIMIT)


def _sigmoid(x):
    return 0.5 * jnp.tanh(0.5 * x) + 0.5


def _pack_pairs(v):
    u = lax.bitcast_convert_type(v.astype(BF16).astype(F32), U32)
    half = v.shape[-1] // 2
    w = (u[:, half:] & jnp.uint32(0xFFFF0000)) | (u[:, :half] >> 16)
    return lax.bitcast_convert_type(w, I32)


def _unpack_pairs(w):
    u = lax.bitcast_convert_type(w, U32)
    lo = lax.bitcast_convert_type(u << 16, F32)
    hi = lax.bitcast_convert_type(u & jnp.uint32(0xFFFF0000), F32)
    return lo, hi


def _moe_residual(y_ref, tw_ref, x_ref, g2_ref):
    tw = tw_ref[...]
    parts = [None] * 4
    for k in range(TOP_K):
        w = tw[:, k:k + 1]
        for piece in range(2):
            lo, hi = _unpack_pairs(y_ref[piece, k])
            for slot, v in ((piece, lo), (2 + piece, hi)):
                parts[slot] = w * v if parts[slot] is None else parts[slot] + w * v
    return x_ref[...] + g2_ref[0] * jnp.concatenate(parts, axis=-1)


def _ada_kernel(c_ref, w_ref, b_ref, o_ref):
    c = c_ref[...]
    ca = c * _sigmoid(c)
    o_ref[0] = jnp.dot(ca, w_ref[0], preferred_element_type=F32) + b_ref[0]


def _ada_mod(c, ada_w, ada_b):
    depth, d, cols = ada_w.shape
    b = c.shape[0]
    tn = 1024
    return pl.pallas_call(
        _ada_kernel,
        grid=(depth, cols // tn),
        in_specs=[
            pl.BlockSpec((b, d), lambda l, j: (0, 0)),
            pl.BlockSpec((1, d, tn), lambda l, j: (l, 0, j)),
            pl.BlockSpec((1, 1, tn), lambda l, j: (l, 0, j)),
        ],
        out_specs=pl.BlockSpec((1, b, tn), lambda l, j: (l, 0, j)),
        out_shape=jax.ShapeDtypeStruct((depth, b, cols), F32),
        compiler_params=_cparams(("arbitrary", "arbitrary")),
        name="ada_mod",
    )(c, ada_w, ada_b.reshape(depth, 1, cols))


def _prep_win_kernel(w_ref, wm_ref, wf_ref):
    wm_ref[:GATE_COL0, :] = w_ref[0, :GATE_COL0, :].astype(BF16)
    wm_ref[GATE_COL0:, :] = w_ref[0, GATE_COL0 + FOX_HEADS:, :].astype(BF16)
    wf_ref[...] = jnp.zeros(wf_ref.shape, BF16)
    wf_ref[:FOX_HEADS, :] = w_ref[0, GATE_COL0:GATE_COL0 + FOX_HEADS, :].astype(BF16)


def _prep_win(w_t, layer):
    _, cols, d = w_t.shape
    ct = 256
    return pl.pallas_call(
        _prep_win_kernel,
        grid=(d // ct,),
        in_specs=[pl.BlockSpec((1, cols, ct), lambda i: (layer, 0, i))],
        out_specs=[pl.BlockSpec((MAIN_COLS, ct), lambda i: (0, i)), pl.BlockSpec((LANES, ct), lambda i: (0, i))],
        out_shape=[jax.ShapeDtypeStruct((MAIN_COLS, d), BF16), jax.ShapeDtypeStruct((LANES, d), BF16)],
        compiler_params=_cparams(("arbitrary",)),
        name="inproj_weight_prep",
    )(w_t)


def _inproj_kernel(*refs, moe):
    if moe:
        y_ref, tw_ref, x_ref, g2_ref, g_ref, sh_ref, sc_ref, w_ref, wf_ref, proj_ref, fl_ref, xo_ref = refs
        x = _moe_residual(y_ref, tw_ref, x_ref, g2_ref)
        xo_ref[...] = x
    else:
        x_ref, g_ref, sh_ref, sc_ref, w_ref, wf_ref, proj_ref, fl_ref = refs
        x = x_ref[...]
    r = lax.rsqrt(jnp.mean(x * x, axis=-1, keepdims=True) + NORM_EPS)
    h = x * r * g_ref[...]
    h = h * (1.0 + sc_ref[0]) + sh_ref[0]
    hb = h.astype(BF16)
    nt = (((1,), (1,)), ((), ()))
    fl_ref[...] = lax.dot_general(hb, wf_ref[...], nt, preferred_element_type=F32)
    for j in range(MAIN_COLS // COL_TILE_IN):
        cs = slice(j * COL_TILE_IN, (j + 1) * COL_TILE_IN)
        proj_ref[:, cs] = lax.dot_general(hb, w_ref[cs, :], nt, preferred_element_type=F32).astype(BF16)


def _inproj(x2, g, sh, sc, w_main, w_f, seq, moe=None):
    n, d = x2.shape
    tm = min(ROW_TILE_IN, seq)
    per_seq = seq // tm
    resident = pl.Buffered(1)

    def per_b():
        return pl.BlockSpec((1, 1, d), lambda i: (i // per_seq, 0, 0))

    in_specs = [
        pl.BlockSpec((tm, d), lambda i: (i, 0)),
        pl.BlockSpec((1, d), lambda i: (0, 0)),
        per_b(), per_b(),
        pl.BlockSpec((MAIN_COLS, d), lambda i: (0, 0), pipeline_mode=resident),
        pl.BlockSpec((LANES, d), lambda i: (0, 0), pipeline_mode=resident),
    ]
    out_specs = [pl.BlockSpec((tm, MAIN_COLS), lambda i: (i, 0)), pl.BlockSpec((tm, LANES), lambda i: (i, 0))]
    out_shape = [jax.ShapeDtypeStruct((n, MAIN_COLS), BF16), jax.ShapeDtypeStruct((n, LANES), F32)]
    args = [x2, g, sh, sc, w_main, w_f]
    if moe is not None:
        ytok, top_w, g2 = moe
        in_specs = [pl.BlockSpec((2, TOP_K, tm, QUART), lambda i: (0, 0, i, 0)),
                    pl.BlockSpec((tm, LANES), lambda i: (i, 0)), in_specs[0], per_b()] + in_specs[1:]
        args = [ytok, top_w, x2, g2] + args[1:]
        out_specs.append(pl.BlockSpec((tm, d), lambda i: (i, 0)))
        out_shape.append(jax.ShapeDtypeStruct((n, d), F32))
    return pl.pallas_call(
        functools.partial(_inproj_kernel, moe=moe is not None),
        grid=(n // tm,),
        in_specs=in_specs,
        out_specs=out_specs,
        out_shape=out_shape,
        compiler_params=pltpu.CompilerParams(dimension_semantics=("arbitrary",), vmem_limit_bytes=VMEM_LIMIT_INPROJ),
        name="inproj",
    )(*args)


def _split3(v):
    hi = v.astype(BF16)
    r1 = v - hi.astype(F32)
    mid = r1.astype(BF16)
    lo = (r1 - mid.astype(F32)).astype(BF16)
    return hi, mid, lo


def _fcum_kernel(fl_ref, fb_ref, o_ref, *, seq):
    c = min(CUM_CHUNK, seq)
    row = lax.broadcasted_iota(I32, (c, c), 0)
    col = lax.broadcasted_iota(I32, (c, c), 1)
    lower = (col <= row).astype(BF16)
    carry = jnp.zeros((1, LANES), F32)
    for ci in range(seq // c):
        x = fl_ref[pl.ds(ci * c, c), :] + fb_ref[...]
        lf = jnp.minimum(x, 0.0) - jnp.log(1.0 + jnp.exp(-jnp.abs(x)))
        hi, mid, lo = _split3(lf)
        cs = (jnp.dot(lower, hi, preferred_element_type=F32)
              + jnp.dot(lower, mid, preferred_element_type=F32)
              + jnp.dot(lower, lo, preferred_element_type=F32)) + carry
        o_ref[0, :, pl.ds(ci * c, c)] = cs.T[:FOX_HEADS, :]
        carry = cs[c - 1:c, :]


def _fcum(flog, fb_pad, batch, seq):
    return pl.pallas_call(
        functools.partial(_fcum_kernel, seq=seq),
        grid=(batch,),
        in_specs=[
            pl.BlockSpec((seq, LANES), lambda b: (b, 0)),
            pl.BlockSpec((1, LANES), lambda b: (0, 0)),
        ],
        out_specs=pl.BlockSpec((1, FOX_HEADS, seq), lambda b: (b, 0, 0)),
        out_shape=jax.ShapeDtypeStruct((batch, FOX_HEADS, seq), F32),
        compiler_params=_cparams(("arbitrary",)),
        name="fox_decay_cumsum",
    )(flog, fb_pad)


def _attn_kernel(*refs, mode, t, seq, lam_init, rider):
    if rider:
        *refs, wt_scr = refs
        wraw_ref = refs.pop(5)
        wprep_ref = refs.pop(6)
    if mode == "diff":
        q_ref, k_ref, v_ref, lam_ref, g_ref, o_ref, vx_scr = refs
        f_ref = None
        lv = lam_ref[...]
        lam = (jnp.exp(jnp.sum(lv[0:1, :] * lv[1:2, :])) - jnp.exp(jnp.sum(lv[2:3, :] * lv[3:4, :]))
               + lam_init)
    else:
        q_ref, k_ref, v_ref, f_ref, o_ref, vx_scr = refs
    lane = lax.broadcasted_iota(I32, (t, LANES), 1)
    row = lax.broadcasted_iota(I32, (t, t), 0)
    col = lax.broadcasted_iota(I32, (t, t), 1)
    nt = (((1,), (1,)), ((), ()))

    v_all = v_ref[...]
    one = jnp.ones_like(v_all)
    if mode == "diff":
        vx_scr[0] = jnp.concatenate([v_all, one], axis=1)
    else:
        lane_s = lax.broadcasted_iota(I32, v_all.shape, 1)
        vx_scr[0] = jnp.where(lane_s < HEAD_DIM, v_all, one)
        vx_scr[1] = jnp.where(lane_s >= HEAD_DIM, v_all, one)

    n_strips = seq // t
    pairs = [(i, n_strips - 1 - i) for i in range(n_strips // 2)] if n_strips % 2 == 0 else [(i,) for i in range(n_strips)]
    for gi, group in enumerate(pairs):
        if rider:
            for g in range(gi, wt_scr.shape[0], len(pairs)):
                _prep_up_group(wraw_ref, wprep_ref, wt_scr, g)
        units = [(qi, idx) for qi in group for idx in range(2)]
        qms = {}
        for qi in group:
            q = q_ref[qi * t:(qi + 1) * t, :] * jnp.asarray(HEAD_DIM ** -0.5, BF16)
            zero = jnp.zeros_like(q)
            qms[qi, 0] = jnp.where(lane < HEAD_DIM, q, zero)
            qms[qi, 1] = jnp.where(lane >= HEAD_DIM, q, zero)

        def scores(u, k0, k1):
            s_ = lax.dot_general(qms[u], k_ref[k0:k1, :], nt, preferred_element_type=F32)
            if f_ref is not None:
                s_ = s_ - f_ref[0, u[1]:u[1] + 1, k0:k1]
            return s_

        sds = {u: jnp.where(col <= row, scores(u, u[0] * t, (u[0] + 1) * t), NEG_INF) for u in units}
        sms = {u: scores(u, 0, u[0] * t) for u in units if u[0] > 0}
        ms = {}
        for u in units:
            m = jnp.max(sds[u], axis=-1, keepdims=True)
            if u in sms:
                m = jnp.maximum(m, jnp.max(sms[u], axis=-1, keepdims=True))
            ms[u] = m
        pds = {u: jnp.exp((sds[u] - ms[u]).astype(BF16)) for u in units}
        pms = {u: jnp.exp((sms[u] - ms[u]).astype(BF16)) for u in sms}
        outs = {}
        for u in units:
            vx = vx_scr.at[0 if mode == "diff" else u[1]]
            lo = u[0] * t
            acc = jnp.dot(pds[u], vx[lo:lo + t, :], preferred_element_type=F32)
            if u in pms:
                acc = acc + jnp.dot(pms[u], vx[0:lo, :], preferred_element_type=F32)
            outs[u] = acc
        for qi in group:
            o0, o1 = outs[qi, 0], outs[qi, 1]
            if mode == "diff":
                a1 = o0[:, :LANES] / o0[:, LANES:]
                a2 = o1[:, :LANES] / o1[:, LANES:]
                o = a1 - lam * a2
                r = lax.rsqrt(jnp.mean(o * o, axis=-1, keepdims=True) + NORM_EPS)
                o = o * r * g_ref[...] * (1.0 - lam_init)
            else:
                num = jnp.where(lane < HEAD_DIM, o0, o1)
                den = jnp.where(lane < HEAD_DIM, pltpu.roll(o0, HEAD_DIM, 1), pltpu.roll(o1, HEAD_DIM, 1))
                o = num / den
            o_ref[qi * t:(qi + 1) * t, :] = o.astype(BF16)


def _attention(proj, mode, col_q, col_k, col_v, batch, seq, extra, lam_init=0.0, rider=None):
    n = proj.shape[0]
    t = min(ATTN_TILE, seq)
    n_blk = BRANCH_WIDTH // LANES
    in_specs = [
        pl.BlockSpec((seq, LANES), lambda b, h: (b, col_q + h)),
        pl.BlockSpec((seq, LANES), lambda b, h: (b, col_k + h)),
        pl.BlockSpec((seq, LANES), lambda b, h: (b, col_v + h)),
    ]
    if mode == "diff":
        lamv, g = extra
        in_specs += [pl.BlockSpec((4, HEAD_DIM), lambda b, h: (0, 0)),
                     pl.BlockSpec((1, LANES), lambda b, h: (0, 0))]
        args = (proj, proj, proj, lamv, g)
    else:
        (frow,) = extra
        in_specs += [pl.BlockSpec((1, 2, seq), lambda b, h: (b * n_blk + h, 0, 0))]
        args = (proj, proj, proj, frow)
    out_specs = [pl.BlockSpec((seq, LANES), lambda b, h: (b, h))]
    out_shape = [jax.ShapeDtypeStruct((n, BRANCH_WIDTH), BF16)]
    scratch = [pltpu.VMEM((1, seq, 2 * LANES) if mode == "diff" else (2, seq, LANES), BF16)]
    if rider is not None:
        w_up, layer = rider
        _, e, d, f2 = w_up.shape
        parts = batch * n_blk // e
        kt = d // parts
        in_specs.append(pl.BlockSpec((1, 1, kt, f2), lambda b, h: (layer, (b * n_blk + h) // parts, (b * n_blk + h) % parts, 0)))
        args = args + (w_up,)
        out_specs.append(pl.BlockSpec((1, f2, kt), lambda b, h: ((b * n_blk + h) // parts, 0, (b * n_blk + h) % parts)))
        out_shape.append(jax.ShapeDtypeStruct((e, f2, d), BF16))
        scratch.append(pltpu.VMEM((kt // LANES, f2, LANES), F32))
    outs = pl.pallas_call(
        functools.partial(_attn_kernel, mode=mode, t=t, seq=seq, lam_init=lam_init, rider=rider is not None),
        grid=(batch, n_blk),
        in_specs=in_specs,
        out_specs=out_specs,
        out_shape=out_shape,
        scratch_shapes=scratch,
        compiler_params=_cparams(("arbitrary", "arbitrary")),
        name="attn_" + mode,
    )(*args)
    return outs if rider is not None else outs[0]


def _rider_fits(batch, w_up):
    _, e, d, _ = w_up.shape
    steps = batch * (BRANCH_WIDTH // LANES)
    return steps % e == 0 and (d // LANES) % (steps // e) == 0


def _ret_kernel(q_ref, k_ref, v_ref, g_ref, cos_ref, sin_ref, dm_ref, qd_ref, kd_ref, cd_ref,
                o_ref, *, seq, c):
    st = jnp.zeros((RET_HEAD_DIM, RET_HEAD_DIM), F32)
    half = RET_HEAD_DIM // 2

    for ci in range(seq // c):
        sl = pl.ds(ci * c, c)
        cos = cos_ref[sl, :]
        sin = sin_ref[sl, :]
        q = q_ref[sl, :].astype(F32)
        k = k_ref[sl, :].astype(F32)
        qr = q * cos + pltpu.roll(q, half, 1) * sin
        kr = (k * cos + pltpu.roll(k, half, 1) * sin) * (RET_HEAD_DIM ** -0.5)
        v = v_ref[sl, :]
        sc = lax.dot_general(qr.astype(BF16), kr.astype(BF16), (((1,), (1,)), ((), ())),
                             preferred_element_type=F32) * dm_ref[0]
        inner = jnp.dot(sc.astype(BF16), v, preferred_element_type=F32)
        cross = jnp.dot((qr * qd_ref[0]).astype(BF16), st.astype(BF16), preferred_element_type=F32)
        o = inner + cross
        kt = (kr * kd_ref[0]).T.astype(BF16)
        st = st * cd_ref[0] + jnp.dot(kt, v, preferred_element_type=F32)
        mu = jnp.mean(o, axis=-1, keepdims=True)
        oc = o - mu
        var = jnp.mean(oc * oc, axis=-1, keepdims=True)
        on = oc * lax.rsqrt(var + NORM_EPS)
        g = g_ref[sl, :].astype(F32)
        o_ref[sl, :] = (on * (g * _sigmoid(g))).astype(BF16)


def _retention(proj, tables, batch, seq):
    n = proj.shape[0]
    c = min(RET_CHUNK, seq)
    cos, sin, dm, qd, kd, cd = tables
    col = BRANCH_WIDTH * 3 // LANES
    step = BRANCH_WIDTH // LANES

    def pspec(k):
        return pl.BlockSpec((seq, LANES), lambda b, h: (b, col + k * step + h))

    return pl.pallas_call(
        functools.partial(_ret_kernel, seq=seq, c=c),
        grid=(batch, RET_HEADS),
        in_specs=[
            pspec(0), pspec(1), pspec(2), pspec(3),
            pl.BlockSpec((seq, LANES), lambda b, h: (0, 0)),
            pl.BlockSpec((seq, LANES), lambda b, h: (0, 0)),
            pl.BlockSpec((1, c, c), lambda b, h: (h, 0, 0)),
            pl.BlockSpec((1, c, LANES), lambda b, h: (h, 0, 0)),
            pl.BlockSpec((1, c, LANES), lambda b, h: (h, 0, 0)),
            pl.BlockSpec((1, LANES, LANES), lambda b, h: (h, 0, 0)),
        ],
        out_specs=pl.BlockSpec((seq, LANES), lambda b, h: (b, h)),
        out_shape=jax.ShapeDtypeStruct((n, BRANCH_WIDTH), BF16),
        compiler_params=_cparams(("arbitrary", "arbitrary")),
        name="retention",
    )(proj, proj, proj, proj, cos, sin, dm, qd, kd, cd)


def _retention_tables(seq):
    c = min(RET_CHUNK, seq)
    d = RET_HEAD_DIM
    inv = ROPE_BASE ** (-jnp.arange(0, d, 2, dtype=F32) / d)
    ang = jnp.arange(seq).astype(F32)[:, None] * inv[None, :]
    cos = jnp.cos(ang)
    sin = jnp.sin(ang)
    cos_full = jnp.concatenate([cos, cos], axis=-1)
    sin_signed = jnp.concatenate([-sin, sin], axis=-1)
    log_gamma = jnp.log1p(-jnp.exp2(-RET_GAMMA_EXP0 - jnp.arange(RET_HEADS, dtype=F32)))
    idx = jnp.arange(c, dtype=F32)
    dist = idx[:, None] - idx[None, :]
    dm = jnp.where(dist >= 0, jnp.exp(log_gamma[:, None, None] * jnp.maximum(dist, 0.0)), 0.0)
    qd = jnp.exp(log_gamma[:, None] * (idx + 1.0))
    kd = jnp.exp(log_gamma[:, None] * (c - 1.0 - idx))
    cd = jnp.exp(log_gamma * c)
    qd = jnp.broadcast_to(qd[:, :, None], (RET_HEADS, c, LANES))
    kd = jnp.broadcast_to(kd[:, :, None], (RET_HEADS, c, LANES))
    cd = jnp.broadcast_to(cd[:, None, None], (RET_HEADS, LANES, LANES))
    return cos_full, sin_signed, dm, qd, kd, cd


def _merge_kernel(od_ref, or_ref, of_ref, gd_ref, gr_ref, gf_ref, gb_ref, wb_ref, wo_ref, x_ref,
                  g1_ref, n2_ref, sh_ref, sc_ref, rw_ref, rb_ref,
                  xo_ref, hp_ref, ti_ref, tw_ref, cnt_ref, cnt_scr, *, tm, sub):
    @pl.when(pl.program_id(0) == 0)
    def _():
        cnt_scr[...] = jnp.zeros(cnt_scr.shape, F32)

    lane = lax.broadcasted_iota(I32, (sub, LANES), 1)
    r_i = lax.broadcasted_iota(I32, (sub, sub), 0)
    c_i = lax.broadcasted_iota(I32, (sub, sub), 1)
    earlier = (c_i < r_i).astype(BF16)
    counts = cnt_scr[...]

    blocks = [slice(s0, s0 + sub) for s0 in range(0, tm, sub)]

    def phase_branches(rs):
        merged = None
        for i, (o_ref, gate_ref) in enumerate(((od_ref, gd_ref), (or_ref, gr_ref), (of_ref, gf_ref))):
            gate = _sigmoid(gate_ref[rs, :] + gb_ref[i].astype(BF16))
            term = gate.astype(F32) * jnp.dot(o_ref[rs, :], wb_ref[i], preferred_element_type=F32)
            merged = term if merged is None else merged + term
        return merged.astype(BF16)

    def phase_outproj(rs, merged):
        mix = jnp.dot(merged, wo_ref[...], preferred_element_type=F32)
        xn = x_ref[rs, :] + g1_ref[0] * mix
        xo_ref[rs, :] = xn
        r = lax.rsqrt(jnp.mean(xn * xn, axis=-1, keepdims=True) + NORM_EPS)
        h = xn * r * n2_ref[...]
        h = h * (1.0 + sc_ref[0]) + sh_ref[0]
        hp = _pack_pairs(h)
        hp_ref[0, rs, :] = hp[:, :QUART]
        hp_ref[1, rs, :] = hp[:, QUART:]
        hh, hm, _ = _split3(h)
        return (jnp.dot(hh, rw_ref[0], preferred_element_type=F32)
                + jnp.dot(hh, rw_ref[1], preferred_element_type=F32)
                + jnp.dot(hm, rw_ref[0], preferred_element_type=F32)) + rb_ref[...]

    def phase_topk(rs, logits, counts):
        vals, idxs = [], []
        for _ in range(TOP_K):
            m = jnp.max(logits, axis=-1, keepdims=True)
            idx = jnp.min(jnp.where(logits == m, lane, LANES), axis=-1, keepdims=True)
            vals.append(m)
            idxs.append(idx)
            logits = jnp.where(lane == idx, -3.0e38, logits)
        es = [jnp.exp(v - vals[0]) for v in vals]
        den = es[0] + es[1] + es[2] + es[3]
        hits = [lane == idxs[k] for k in range(TOP_K)]
        picked = (hits[0] | hits[1] | hits[2] | hits[3])
        base = jnp.dot(earlier, picked.astype(BF16), preferred_element_type=F32) + counts
        counts = counts + jnp.sum(picked.astype(F32), axis=0, keepdims=True)
        ti = jnp.zeros((sub, LANES), I32)
        tw = jnp.zeros((sub, LANES), F32)
        for k in range(TOP_K):
            rank = jnp.sum(jnp.where(hits[k], base, 0.0), axis=-1, keepdims=True).astype(I32)
            ti = jnp.where(lane == k, idxs[k], ti)
            ti = jnp.where(lane == TOP_K + k, rank, ti)
            tw = jnp.where(lane == k, es[k] / den, tw)
        ti_ref[rs, :] = ti
        tw_ref[rs, :] = tw
        return counts

    merged = [phase_branches(rs) for rs in blocks]
    logits = [phase_outproj(rs, m) for rs, m in zip(blocks, merged)]
    for rs, lg in zip(blocks, logits):
        counts = phase_topk(rs, lg, counts)

    cnt_scr[...] = counts
    cnt_ref[...] = counts.astype(I32)


def _merge(o_diff, o_ret, o_fox, proj, gate_b, w_branch, w_out, x2, g1, n2g, sh2, sc2, rw, rb, seq):
    n, d = x2.shape
    tm = min(MERGE_TILE, seq)
    sub = min(MERGE_SUB, tm)
    per_seq = seq // tm
    gcol = GATE_COL0 // d

    def rows(w):
        return pl.BlockSpec((tm, w), lambda i: (i, 0))

    def gspec(k):
        return pl.BlockSpec((tm, d), lambda i: (i, gcol + k))

    def whole(shape):
        return pl.BlockSpec(shape, lambda i: (0,) * len(shape))

    def per_b():
        return pl.BlockSpec((1, 1, d), lambda i: (i // per_seq, 0, 0))

    return pl.pallas_call(
        functools.partial(_merge_kernel, tm=tm, sub=sub),
        grid=(n // tm,),
        in_specs=[
            rows(BRANCH_WIDTH), rows(BRANCH_WIDTH), rows(BRANCH_WIDTH),
            gspec(0), gspec(1), gspec(2),
            whole((N_BRANCHES, 1, d)), whole((N_BRANCHES, BRANCH_WIDTH, d)), whole((d, d)),
            rows(d), per_b(), whole((1, d)), per_b(), per_b(),
            whole((2, d, LANES)), whole((1, LANES)),
        ],
        out_specs=[rows(d), pl.BlockSpec((2, tm, QUART), lambda i: (0, i, 0)), rows(LANES), rows(LANES),
                   whole((1, LANES))],
        out_shape=[
            jax.ShapeDtypeStruct((n, d), F32),
            jax.ShapeDtypeStruct((2, n, QUART), I32),
            jax.ShapeDtypeStruct((n, LANES), I32),
            jax.ShapeDtypeStruct((n, LANES), F32),
            jax.ShapeDtypeStruct((1, LANES), I32),
        ],
        scratch_shapes=[pltpu.VMEM((1, LANES), F32)],
        compiler_params=_cparams(("arbitrary",)),
        name="merge_outproj_router",
    )(o_diff, o_ret, o_fox, proj, proj, proj, gate_b, w_branch, w_out, x2, g1, n2g, sh2, sc2, rw, rb)


def _prep_up_group(w_ref, o_ref, t_scr, g):
    f = t_scr.shape[1] // 2
    cols = slice(g * LANES, (g + 1) * LANES)
    t_scr[g] = w_ref[0, 0, cols, :].T
    o_ref[0, 0:f, cols] = t_scr[g, pl.ds(0, f, stride=2), :].astype(BF16)
    o_ref[0, f:, cols] = t_scr[g, pl.ds(1, f, stride=2), :].astype(BF16)


def _prep_up_kernel(w_ref, o_ref, t_scr):
    for g in range(t_scr.shape[0]):
        _prep_up_group(w_ref, o_ref, t_scr, g)


def _prep_up(w_up, layer):
    _, e, d, f2 = w_up.shape
    groups = d // LANES
    kt = groups * LANES
    return pl.pallas_call(
        _prep_up_kernel,
        grid=(e, d // kt),
        in_specs=[pl.BlockSpec((1, 1, kt, f2), lambda i, j: (layer, i, j, 0))],
        out_specs=pl.BlockSpec((1, f2, kt), lambda i, j: (i, 0, j)),
        out_shape=jax.ShapeDtypeStruct((e, f2, d), BF16),
        scratch_shapes=[pltpu.VMEM((groups, f2, LANES), F32)],
        compiler_params=_cparams(("arbitrary", "arbitrary")),
        name="expert_up_weight_prep",
    )(w_up)


def _expert_kernel(te_ref, nu_ref, xs_ref, *refs):
    y_ref = refs[-1]
    nt = (((1,), (1,)), ((), ()))
    first_tile = pl.program_id(0) * EXPERT_GROUP

    @pl.when(first_tile < nu_ref[0])
    def _():
        for s in range(EXPERT_GROUP):
            wu_ref, bu_ref, wd_ref, bd_ref = refs[4 * s:4 * s + 4]
            rs = slice(s * EXPERT_TILE, (s + 1) * EXPERT_TILE)
            lo, hi = _unpack_pairs(jnp.concatenate([xs_ref[0, rs, :], xs_ref[1, rs, :]], axis=1))
            lo = lo.astype(BF16)
            hi = hi.astype(BF16)

            def up_half(r0):
                return (lax.dot_general(lo, wu_ref[0, r0:r0 + D_FF, :HALF], nt, preferred_element_type=F32)
                        + lax.dot_general(hi, wu_ref[0, r0:r0 + D_FF, HALF:], nt, preferred_element_type=F32))

            glu = jnp.minimum(up_half(0) + bu_ref[0, :, :D_FF], SWIGLU_LIMIT)
            lin = jnp.clip(up_half(D_FF) + bu_ref[0, :, D_FF:], -SWIGLU_LIMIT, SWIGLU_LIMIT)
            act = glu * _sigmoid(SWIGLU_ALPHA * glu) * (lin + 1.0)
            down = jnp.dot(act.astype(BF16), wd_ref[0, 0].astype(BF16), preferred_element_type=F32) + bd_ref[0]
            yp = _pack_pairs(down)
            y_ref[0, rs, :] = yp[:, :QUART]
            y_ref[1, rs, :] = yp[:, QUART:]

    @pl.when(first_tile >= nu_ref[0])
    def _():
        y_ref[...] = jnp.zeros(y_ref.shape, I32)


def _experts(tile_e, n_used, xs, w_up, b_up, w_down, layer, b_down):
    r = xs.shape[1]
    tm = EXPERT_TILE * EXPERT_GROUP
    d = D_MODEL
    in_specs = [pl.BlockSpec((2, tm, QUART), lambda i, te, nu: (0, i, 0))]
    args = [xs]
    for s in range(EXPERT_GROUP):
        def pick(i, te, nu, s=s):
            return (te[i * EXPERT_GROUP + s], 0, 0)

        def pick_down(i, te, nu, s=s):
            return (layer, te[i * EXPERT_GROUP + s], 0, 0)
        in_specs += [pl.BlockSpec((1, 2 * D_FF, d), pick), pl.BlockSpec((1, 1, 2 * D_FF), pick),
                     pl.BlockSpec((1, 1, D_FF, d), pick_down), pl.BlockSpec((1, 1, d), pick)]
        args += [w_up, b_up, w_down, b_down]
    grid_spec = pltpu.PrefetchScalarGridSpec(
        num_scalar_prefetch=2,
        grid=(r // tm,),
        in_specs=in_specs,
        out_specs=pl.BlockSpec((2, tm, QUART), lambda i, te, nu: (0, i, 0)),
    )
    return pl.pallas_call(
        _expert_kernel,
        grid_spec=grid_spec,
        out_shape=jax.ShapeDtypeStruct((2, r, QUART), I32),
        compiler_params=_cparams(("arbitrary",)),
        name="expert_mlp",
    )(tile_e, n_used, *args)


def _combine_kernel(y_ref, tw_ref, x_ref, g2_ref, fg_ref, o_ref):
    xn = _moe_residual(y_ref, tw_ref, x_ref, g2_ref)
    r = lax.rsqrt(jnp.mean(xn * xn, axis=-1, keepdims=True) + NORM_EPS)
    o_ref[...] = xn * r * fg_ref[...]


def _combine(ytok, top_w, x2, g2, final_g, seq):
    n, d = x2.shape
    tm = min(ROW_TILE_OUT, seq)
    per_seq = seq // tm
    return pl.pallas_call(
        _combine_kernel,
        grid=(n // tm,),
        in_specs=[
            pl.BlockSpec((2, TOP_K, tm, QUART), lambda i: (0, 0, i, 0)),
            pl.BlockSpec((tm, LANES), lambda i: (i, 0)),
            pl.BlockSpec((tm, d), lambda i: (i, 0)),
            pl.BlockSpec((1, 1, d), lambda i: (i // per_seq, 0, 0)),
            pl.BlockSpec((1, d), lambda i: (0, 0)),
        ],
        out_specs=pl.BlockSpec((tm, d), lambda i: (i, 0)),
        out_shape=jax.ShapeDtypeStruct((n, d), F32),
        compiler_params=_cparams(("arbitrary",)),
        name="moe_combine",
    )(ytok, top_w, x2, g2, final_g)


def _route(top_idx, counts, n_rows):
    tm = EXPERT_TILE
    e = top_idx[:, :TOP_K]
    rank = top_idx[:, TOP_K:2 * TOP_K]
    sizes = counts[0, :N_EXPERTS]
    padded = ((sizes + tm - 1) // tm) * tm
    pend = jnp.cumsum(padded)
    pstart = pend - padded
    onehot = e[:, :, None] == jnp.arange(N_EXPERTS, dtype=I32)[None, None, :]
    pos = jnp.sum(jnp.where(onehot, pstart[None, None, :], 0), axis=-1) + rank
    tiles = jnp.arange(n_rows // tm, dtype=I32) * tm
    tile_e = jnp.minimum(jnp.sum(tiles[:, None] >= pend[None, :], axis=-1), N_EXPERTS - 1).astype(I32)
    n_used = (pend[-1] // tm).astype(I32).reshape(1)
    return pos.T.astype(I32), tile_e, n_used


def _sc_gather(table, idx):
    m = idx.shape[0]
    width = table.shape[1]
    assert m % (SC_WINDOW * SC_WORKERS) == 0
    mesh = plsc.VectorSubcoreMesh(core_axis_name="c", subcore_axis_name="s")

    @pl.kernel(out_type=jax.ShapeDtypeStruct((m, width), table.dtype), mesh=mesh)
    def gather_rows(t_hbm, i_hbm, o_hbm):
        def body(i_vmem, o_vmem):
            pltpu.sync_copy(t_hbm.at[i_vmem.at[0]], o_vmem)

        pltpu.emit_pipeline(
            body,
            grid=(m // SC_WINDOW,),
            in_specs=[pl.BlockSpec((1, SC_WINDOW), lambda i: (0, i))],
            out_specs=[pl.BlockSpec((SC_WINDOW, width), lambda i: (i, 0))],
            core_axis_name=("c", "s"),
            dimension_semantics=(pltpu.PARALLEL,),
        )(i_hbm, o_hbm)

    return gather_rows(table, idx.reshape(1, m))


def _sc_scatter(rows, idx, n_out):
    t, width = rows.shape
    picks = idx.shape[0]
    assert t % SC_WINDOW == 0
    mesh = plsc.VectorSubcoreMesh(core_axis_name="c", subcore_axis_name="s")

    @pl.kernel(out_type=jax.ShapeDtypeStruct((n_out, width), rows.dtype), mesh=mesh)
    def scatter_rows(x_hbm, i_hbm, o_hbm):
        def body(x_vmem, i_vmem):
            for k in range(picks):
                pltpu.sync_copy(x_vmem, o_hbm.at[i_vmem.at[k]])

        pltpu.emit_pipeline(
            body,
            grid=(t // SC_WINDOW,),
            in_specs=[pl.BlockSpec((SC_WINDOW, width), lambda i: (i, 0)),
                      pl.BlockSpec((picks, SC_WINDOW), lambda i: (0, i))],
            out_specs=[],
            core_axis_name=("c", "s"),
            dimension_semantics=(pltpu.PARALLEL,),
        )(x_hbm, i_hbm)

    return scatter_rows(rows, idx)


def _dispatch(h2p, pos_t, n_rows):
    n = h2p.shape[1]
    idx = jnp.concatenate([pos_t, pos_t + n_rows], axis=1)
    return _sc_scatter(h2p.reshape(2 * n, QUART), idx, 2 * n_rows).reshape(2, n_rows, QUART)


def _collect(y, pos_t):
    n_rows = y.shape[1]
    flat = pos_t.reshape(-1)
    out = _sc_gather(y.reshape(2 * n_rows, QUART), jnp.concatenate([flat, flat + n_rows]))
    return out.reshape(2, TOP_K, pos_t.shape[1], QUART)


def kernel(x, c, norm1_g, norm2_g, ada_w, ada_b, w_in, gate_b, fox_fb, lam_q1, lam_k1, lam_q2, lam_k2,
           diff_subln_g, w_branch, w_out, router_w, router_b, exp_w_up, exp_b_up, exp_w_down,
           exp_b_down, final_g):
    batch, seq, d = x.shape
    depth = ada_w.shape[0]
    n = batch * seq
    x2 = x.reshape(n, d)

    mod = _ada_mod(c, ada_w, ada_b)
    tables = _retention_tables(seq)
    n_rows = n * TOP_K + N_EXPERTS * EXPERT_TILE
    blk = BRANCH_WIDTH // LANES

    w_in_t = jnp.swapaxes(w_in, 1, 2)
    pending = None
    for l in range(depth):
        lam_init = 0.8 - 0.6 * math.exp(-0.3 * l)
        sh1, sc1, g1, sh2, sc2, g2 = [m.reshape(batch, 1, d) for m in jnp.split(mod[l], 6, axis=-1)]

        w_main, w_f = _prep_win(w_in_t, l)
        outs = _inproj(x2, norm1_g[l].reshape(1, d), sh1, sc1, w_main, w_f, seq, pending)
        proj, flog = outs[0], outs[1]
        if pending is not None:
            x2 = outs[2]

        fb_pad = jnp.pad(fox_fb[l], (0, LANES - FOX_HEADS)).reshape(1, LANES)
        frow = _fcum(flog, fb_pad, batch, seq).reshape(batch * blk, 2, seq)

        lamv = jnp.stack([lam_q1[l], lam_k1[l], lam_q2[l], lam_k2[l]])
        diff_extra = (lamv, diff_subln_g[l].reshape(1, LANES))
        if _rider_fits(batch, exp_w_up):
            o_diff, w_up = _attention(proj, "diff", 0, blk, 2 * blk, batch, seq, diff_extra, lam_init,
                                      rider=(exp_w_up, l))
        else:
            o_diff = _attention(proj, "diff", 0, blk, 2 * blk, batch, seq, diff_extra, lam_init)
            w_up = _prep_up(exp_w_up, l)
        o_ret = _retention(proj, tables, batch, seq)
        o_fox = _attention(proj, "fox", 7 * blk, 8 * blk, 9 * blk, batch, seq, (frow,))

        rw = jnp.pad(router_w[l], ((0, 0), (0, LANES - N_EXPERTS)))
        rw_hi = rw.astype(BF16)
        rw = jnp.stack([rw_hi, (rw - rw_hi.astype(F32)).astype(BF16)])
        rb = jnp.pad(router_b[l], (0, LANES - N_EXPERTS), constant_values=NEG_INF).reshape(1, LANES)
        x2, h2p, top_idx, top_w, counts = _merge(
            o_diff, o_ret, o_fox, proj, gate_b[l].reshape(N_BRANCHES, 1, d), w_branch[l].astype(BF16),
            w_out[l].astype(BF16), x2, g1, norm2_g[l].reshape(1, d), sh2, sc2, rw, rb, seq)

        pos_t, tile_e, n_used = _route(top_idx, counts, n_rows)
        xs = _dispatch(h2p, pos_t, n_rows)

        bu = exp_b_up[l]
        b_up = jnp.concatenate([bu[:, 0::2], bu[:, 1::2]], axis=-1).reshape(N_EXPERTS, 1, 2 * D_FF)
        y = _experts(tile_e, n_used, xs, w_up, b_up, exp_w_down, l,
                     exp_b_down[l].reshape(N_EXPERTS, 1, d))
        pending = (_collect(y, pos_t), top_w, g2)

    ytok, top_w, g2 = pending
    x2 = _combine(ytok, top_w, x2, g2, final_g.reshape(1, d), seq)
    return x2.reshape(batch, seq, d)
```

```python
import functools
import math

import jax
import jax.numpy as jnp
from jax import lax
from jax.experimental import pallas as pl
from jax.experimental.pallas import tpu as pltpu
from jax.experimental.pallas import tpu_sc as plsc

F32 = jnp.float32
BF16 = jnp.bfloat16
I32 = jnp.int32
U32 = jnp.uint32

D_MODEL = 1024
HEAD_DIM = 64
BRANCH_WIDTH = D_MODEL // 2
N_BRANCHES = 3
FOX_HEADS = BRANCH_WIDTH // HEAD_DIM
RET_HEADS = 4
RET_HEAD_DIM = 128
ROPE_BASE = 10000.0
RET_GAMMA_EXP0 = 5.0
N_EXPERTS = 32
TOP_K = 4
D_FF = D_MODEL
SWIGLU_LIMIT = 7.0
SWIGLU_ALPHA = 1.702
NORM_EPS = 1e-5
NEG_INF = -1e30

LANES = 128
MAIN_COLS = 8192
GATE_COL0 = 10 * BRANCH_WIDTH
HALF = D_MODEL // 2
QUART = HALF // 2
SC_WINDOW = 128
SC_WORKERS = 32
VMEM_LIMIT = 48 * 1024 * 1024
VMEM_LIMIT_INPROJ = 56 * 1024 * 1024

ROW_TILE_IN = 512
COL_TILE_IN = 1024
ATTN_TILE = 256
RET_CHUNK = 256
ROW_TILE_OUT = 256
MERGE_TILE = 1024
MERGE_SUB = 256
EXPERT_TILE = 256
EXPERT_GROUP = 2
CUM_CHUNK = 256


def _cparams(sem):
    return pltpu.CompilerParams(dimension_semantics=sem, vmem_limit_bytes=VMEM_LIMIT)


def _sigmoid(x):
    return 0.5 * jnp.tanh(0.5 * x) + 0.5


def _pack_pairs(v):
    u = lax.bitcast_convert_type(v.astype(BF16).astype(F32), U32)
    half = v.shape[-1] // 2
    w = (u[:, half:] & jnp.uint32(0xFFFF0000)) | (u[:, :half] >> 16)
    return lax.bitcast_convert_type(w, I32)


def _unpack_pairs(w):
    u = lax.bitcast_convert_type(w, U32)
    lo = lax.bitcast_convert_type(u << 16, F32)
    hi = lax.bitcast_convert_type(u & jnp.uint32(0xFFFF0000), F32)
    return lo, hi


def _moe_residual(y_ref, tw_ref, x_ref, g2_ref):
    tw = tw_ref[...]
    parts = [None] * 4
    for k in range(TOP_K):
        w = tw[:, k:k + 1]
        for piece in range(2):
            lo, hi = _unpack_pairs(y_ref[piece, k])
            for slot, v in ((piece, lo), (2 + piece, hi)):
                parts[slot] = w * v if parts[slot] is None else parts[slot] + w * v
    return x_ref[...] + g2_ref[0] * jnp.concatenate(parts, axis=-1)


def _ada_kernel(c_ref, w_ref, b_ref, o_ref):
    c = c_ref[...]
    ca = c * _sigmoid(c)
    o_ref[0] = jnp.dot(ca, w_ref[0], preferred_element_type=F32) + b_ref[0]


def _ada_mod(c, ada_w, ada_b):
    depth, d, cols = ada_w.shape
    b = c.shape[0]
    tn = 1024
    return pl.pallas_call(
        _ada_kernel,
        grid=(depth, cols // tn),
        in_specs=[
            pl.BlockSpec((b, d), lambda l, j: (0, 0)),
            pl.BlockSpec((1, d, tn), lambda l, j: (l, 0, j)),
            pl.BlockSpec((1, 1, tn), lambda l, j: (l, 0, j)),
        ],
        out_specs=pl.BlockSpec((1, b, tn), lambda l, j: (l, 0, j)),
        out_shape=jax.ShapeDtypeStruct((depth, b, cols), F32),
        compiler_params=_cparams(("arbitrary", "arbitrary")),
        name="ada_mod",
    )(c, ada_w, ada_b.reshape(depth, 1, cols))


def _prep_win_kernel(w_ref, wm_ref, wf_ref):
    wm_ref[:GATE_COL0, :] = w_ref[0, :GATE_COL0, :].astype(BF16)
    wm_ref[GATE_COL0:, :] = w_ref[0, GATE_COL0 + FOX_HEADS:, :].astype(BF16)
    wf_ref[...] = jnp.zeros(wf_ref.shape, BF16)
    wf_ref[:FOX_HEADS, :] = w_ref[0, GATE_COL0:GATE_COL0 + FOX_HEADS, :].astype(BF16)


def _prep_win(w_t, layer):
    _, cols, d = w_t.shape
    ct = 256
    return pl.pallas_call(
        _prep_win_kernel,
        grid=(d // ct,),
        in_specs=[pl.BlockSpec((1, cols, ct), lambda i: (layer, 0, i))],
        out_specs=[pl.BlockSpec((MAIN_COLS, ct), lambda i: (0, i)), pl.BlockSpec((LANES, ct), lambda i: (0, i))],
        out_shape=[jax.ShapeDtypeStruct((MAIN_COLS, d), BF16), jax.ShapeDtypeStruct((LANES, d), BF16)],
        compiler_params=_cparams(("arbitrary",)),
        name="inproj_weight_prep",
    )(w_t)


def _inproj_kernel(*refs, moe):
    if moe:
        y_ref, tw_ref, x_ref, g2_ref, g_ref, sh_ref, sc_ref, w_ref, wf_ref, proj_ref, fl_ref, xo_ref = refs
        x = _moe_residual(y_ref, tw_ref, x_ref, g2_ref)
        xo_ref[...] = x
    else:
        x_ref, g_ref, sh_ref, sc_ref, w_ref, wf_ref, proj_ref, fl_ref = refs
        x = x_ref[...]
    r = lax.rsqrt(jnp.mean(x * x, axis=-1, keepdims=True) + NORM_EPS)
    h = x * r * g_ref[...]
    h = h * (1.0 + sc_ref[0]) + sh_ref[0]
    hb = h.astype(BF16)
    nt = (((1,), (1,)), ((), ()))
    fl_ref[...] = lax.dot_general(hb, wf_ref[...], nt, preferred_element_type=F32)
    for j in range(MAIN_COLS // COL_TILE_IN):
        cs = slice(j * COL_TILE_IN, (j + 1) * COL_TILE_IN)
        proj_ref[:, cs] = lax.dot_general(hb, w_ref[cs, :], nt, preferred_element_type=F32).astype(BF16)


def _inproj(x2, g, sh, sc, w_main, w_f, seq, moe=None):
    n, d = x2.shape
    tm = min(ROW_TILE_IN, seq)
    per_seq = seq // tm
    resident = pl.Buffered(1)

    def per_b():
        return pl.BlockSpec((1, 1, d), lambda i: (i // per_seq, 0, 0))

    in_specs = [
        pl.BlockSpec((tm, d), lambda i: (i, 0)),
        pl.BlockSpec((1, d), lambda i: (0, 0)),
        per_b(), per_b(),
        pl.BlockSpec((MAIN_COLS, d), lambda i: (0, 0), pipeline_mode=resident),
        pl.BlockSpec((LANES, d), lambda i: (0, 0), pipeline_mode=resident),
    ]
    out_specs = [pl.BlockSpec((tm, MAIN_COLS), lambda i: (i, 0)), pl.BlockSpec((tm, LANES), lambda i: (i, 0))]
    out_shape = [jax.ShapeDtypeStruct((n, MAIN_COLS), BF16), jax.ShapeDtypeStruct((n, LANES), F32)]
    args = [x2, g, sh, sc, w_main, w_f]
    if moe is not None:
        ytok, top_w, g2 = moe
        in_specs = [pl.BlockSpec((2, TOP_K, tm, QUART), lambda i: (0, 0, i, 0)),
                    pl.BlockSpec((tm, LANES), lambda i: (i, 0)), in_specs[0], per_b()] + in_specs[1:]
        args = [ytok, top_w, x2, g2] + args[1:]
        out_specs.append(pl.BlockSpec((tm, d), lambda i: (i, 0)))
        out_shape.append(jax.ShapeDtypeStruct((n, d), F32))
    return pl.pallas_call(
        functools.partial(_inproj_kernel, moe=moe is not None),
        grid=(n // tm,),
        in_specs=in_specs,
        out_specs=out_specs,
        out_shape=out_shape,
        compiler_params=pltpu.CompilerParams(dimension_semantics=("arbitrary",), vmem_limit_bytes=VMEM_LIMIT_INPROJ),
        name="inproj",
    )(*args)


def _split3(v):
    hi = v.astype(BF16)
    r1 = v - hi.astype(F32)
    mid = r1.astype(BF16)
    lo = (r1 - mid.astype(F32)).astype(BF16)
    return hi, mid, lo


def _fcum_kernel(fl_ref, fb_ref, o_ref, *, seq):
    c = min(CUM_CHUNK, seq)
    row = lax.broadcasted_iota(I32, (c, c), 0)
    col = lax.broadcasted_iota(I32, (c, c), 1)
    lower = (col <= row).astype(BF16)
    carry = jnp.zeros((1, LANES), F32)
    for ci in range(seq // c):
        x = fl_ref[pl.ds(ci * c, c), :] + fb_ref[...]
        lf = jnp.minimum(x, 0.0) - jnp.log(1.0 + jnp.exp(-jnp.abs(x)))
        hi, mid, lo = _split3(lf)
        cs = (jnp.dot(lower, hi, preferred_element_type=F32)
              + jnp.dot(lower, mid, preferred_element_type=F32)
              + jnp.dot(lower, lo, preferred_element_type=F32)) + carry
        o_ref[0, :, pl.ds(ci * c, c)] = cs.T[:FOX_HEADS, :]
        carry = cs[c - 1:c, :]


def _fcum(flog, fb_pad, batch, seq):
    return pl.pallas_call(
        functools.partial(_fcum_kernel, seq=seq),
        grid=(batch,),
        in_specs=[
            pl.BlockSpec((seq, LANES), lambda b: (b, 0)),
            pl.BlockSpec((1, LANES), lambda b: (0, 0)),
        ],
        out_specs=pl.BlockSpec((1, FOX_HEADS, seq), lambda b: (b, 0, 0)),
        out_shape=jax.ShapeDtypeStruct((batch, FOX_HEADS, seq), F32),
        compiler_params=_cparams(("arbitrary",)),
        name="fox_decay_cumsum",
    )(flog, fb_pad)


def _attn_kernel(*refs, mode, t, seq, lam_init, rider):
    if rider:
        refs = list(refs)
        wt_scr = refs.pop() if rider == "up" else None
        n_in = 5 if mode == "diff" else 4
        wraw_ref = refs.pop(n_in)
        wprep_ref = refs.pop(n_in + 1)
    if mode == "diff":
        q_ref, k_ref, v_ref, lam_ref, g_ref, o_ref, vx_scr = refs
        f_ref = None
        lv = lam_ref[...]
        lam = (jnp.exp(jnp.sum(lv[0:1, :] * lv[1:2, :])) - jnp.exp(jnp.sum(lv[2:3, :] * lv[3:4, :]))
               + lam_init)
    else:
        q_ref, k_ref, v_ref, f_ref, o_ref, vx_scr = refs
    lane = lax.broadcasted_iota(I32, (t, LANES), 1)
    row = lax.broadcasted_iota(I32, (t, t), 0)
    col = lax.broadcasted_iota(I32, (t, t), 1)
    nt = (((1,), (1,)), ((), ()))

    v_all = v_ref[...]
    one = jnp.ones_like(v_all)
    if mode == "diff":
        vx_scr[0] = jnp.concatenate([v_all, one], axis=1)
    else:
        lane_s = lax.broadcasted_iota(I32, v_all.shape, 1)
        vx_scr[0] = jnp.where(lane_s < HEAD_DIM, v_all, one)
        vx_scr[1] = jnp.where(lane_s >= HEAD_DIM, v_all, one)

    n_strips = seq // t
    pairs = [(i, n_strips - 1 - i) for i in range(n_strips // 2)] if n_strips % 2 == 0 else [(i,) for i in range(n_strips)]
    for gi, group in enumerate(pairs):
        if rider == "up":
            for g in range(gi, wt_scr.shape[0], len(pairs)):
                _prep_up_group(wraw_ref, wprep_ref, wt_scr, g)
        elif rider == "down":
            span = wprep_ref.shape[1] // len(pairs)
            rows = slice(gi * span, (gi + 1) * span)
            wprep_ref[0, rows, :] = wraw_ref[0, 0, rows, :].astype(BF16)
        units = [(qi, idx) for qi in group for idx in range(2)]
        qms = {}
        for qi in group:
            q = q_ref[qi * t:(qi + 1) * t, :] * jnp.asarray(HEAD_DIM ** -0.5, BF16)
            zero = jnp.zeros_like(q)
            qms[qi, 0] = jnp.where(lane < HEAD_DIM, q, zero)
            qms[qi, 1] = jnp.where(lane >= HEAD_DIM, q, zero)

        def scores(u, k0, k1):
            s_ = lax.dot_general(qms[u], k_ref[k0:k1, :], nt, preferred_element_type=F32)
            if f_ref is not None:
                s_ = s_ - f_ref[0, u[1]:u[1] + 1, k0:k1]
            return s_

        sds = {u: jnp.where(col <= row, scores(u, u[0] * t, (u[0] + 1) * t), NEG_INF) for u in units}
        sms = {u: scores(u, 0, u[0] * t) for u in units if u[0] > 0}
        ms = {}
        for u in units:
            m = jnp.max(sds[u], axis=-1, keepdims=True)
            if u in sms:
                m = jnp.maximum(m, jnp.max(sms[u], axis=-1, keepdims=True))
            ms[u] = m
        pds = {u: jnp.exp((sds[u] - ms[u]).astype(BF16)) for u in units}
        pms = {u: jnp.exp((sms[u] - ms[u]).astype(BF16)) for u in sms}
        outs = {}
        for u in units:
            vx = vx_scr.at[0 if mode == "diff" else u[1]]
            lo = u[0] * t
            acc = jnp.dot(pds[u], vx[lo:lo + t, :], preferred_element_type=F32)
            if u in pms:
                acc = acc + jnp.dot(pms[u], vx[0:lo, :], preferred_element_type=F32)
            outs[u] = acc
        for qi in group:
            o0, o1 = outs[qi, 0], outs[qi, 1]
            if mode == "diff":
                a1 = o0[:, :LANES] / o0[:, LANES:]
                a2 = o1[:, :LANES] / o1[:, LANES:]
                o = a1 - lam * a2
                r = lax.rsqrt(jnp.mean(o * o, axis=-1, keepdims=True) + NORM_EPS)
                o = o * r * g_ref[...] * (1.0 - lam_init)
            else:
                num = jnp.where(lane < HEAD_DIM, o0, o1)
                den = jnp.where(lane < HEAD_DIM, pltpu.roll(o0, HEAD_DIM, 1), pltpu.roll(o1, HEAD_DIM, 1))
                o = num / den
            o_ref[qi * t:(qi + 1) * t, :] = o.astype(BF16)


def _attention(proj, mode, col_q, col_k, col_v, batch, seq, extra, lam_init=0.0, rider=None):
    n = proj.shape[0]
    t = min(ATTN_TILE, seq)
    n_blk = BRANCH_WIDTH // LANES
    in_specs = [
        pl.BlockSpec((seq, LANES), lambda b, h: (b, col_q + h)),
        pl.BlockSpec((seq, LANES), lambda b, h: (b, col_k + h)),
        pl.BlockSpec((seq, LANES), lambda b, h: (b, col_v + h)),
    ]
    if mode == "diff":
        lamv, g = extra
        in_specs += [pl.BlockSpec((4, HEAD_DIM), lambda b, h: (0, 0)),
                     pl.BlockSpec((1, LANES), lambda b, h: (0, 0))]
        args = (proj, proj, proj, lamv, g)
    else:
        (frow,) = extra
        in_specs += [pl.BlockSpec((1, 2, seq), lambda b, h: (b * n_blk + h, 0, 0))]
        args = (proj, proj, proj, frow)
    out_specs = [pl.BlockSpec((seq, LANES), lambda b, h: (b, h))]
    out_shape = [jax.ShapeDtypeStruct((n, BRANCH_WIDTH), BF16)]
    scratch = [pltpu.VMEM((1, seq, 2 * LANES) if mode == "diff" else (2, seq, LANES), BF16)]
    if rider is not None:
        kind, w, layer = rider
        _, e, rows, cols = w.shape
        parts = batch * n_blk // e
        kt = rows // parts
        in_specs.append(pl.BlockSpec((1, 1, kt, cols), lambda b, h: (layer, (b * n_blk + h) // parts, (b * n_blk + h) % parts, 0)))
        args = args + (w,)
        if kind == "up":
            out_specs.append(pl.BlockSpec((1, cols, kt), lambda b, h: ((b * n_blk + h) // parts, 0, (b * n_blk + h) % parts)))
            out_shape.append(jax.ShapeDtypeStruct((e, cols, rows), BF16))
            scratch.append(pltpu.VMEM((kt // LANES, cols, LANES), F32))
        else:
            out_specs.append(pl.BlockSpec((1, kt, cols), lambda b, h: ((b * n_blk + h) // parts, (b * n_blk + h) % parts, 0)))
            out_shape.append(jax.ShapeDtypeStruct((e, rows, cols), BF16))
    outs = pl.pallas_call(
        functools.partial(_attn_kernel, mode=mode, t=t, seq=seq, lam_init=lam_init, rider=rider[0] if rider else None),
        grid=(batch, n_blk),
        in_specs=in_specs,
        out_specs=out_specs,
        out_shape=out_shape,
        scratch_shapes=scratch,
        compiler_params=_cparams(("arbitrary", "arbitrary")),
        name="attn_" + mode,
    )(*args)
    return outs if rider is not None else outs[0]


def _rider_fits(batch, w):
    _, e, rows, _ = w.shape
    steps = batch * (BRANCH_WIDTH // LANES)
    return steps % e == 0 and (rows // LANES) % (steps // e) == 0


def _ret_kernel(q_ref, k_ref, v_ref, g_ref, cos_ref, sin_ref, dm_ref, qd_ref, kd_ref, cd_ref,
                o_ref, *, seq, c):
    st = jnp.zeros((RET_HEAD_DIM, RET_HEAD_DIM), F32)
    half = RET_HEAD_DIM // 2

    for ci in range(seq // c):
        sl = pl.ds(ci * c, c)
        cos = cos_ref[sl, :]
        sin = sin_ref[sl, :]
        q = q_ref[sl, :].astype(F32)
        k = k_ref[sl, :].astype(F32)
        qr = q * cos + pltpu.roll(q, half, 1) * sin
        kr = (k * cos + pltpu.roll(k, half, 1) * sin) * (RET_HEAD_DIM ** -0.5)
        v = v_ref[sl, :]
        sc = lax.dot_general(qr.astype(BF16), kr.astype(BF16), (((1,), (1,)), ((), ())),
                             preferred_element_type=F32) * dm_ref[0]
        inner = jnp.dot(sc.astype(BF16), v, preferred_element_type=F32)
        cross = jnp.dot((qr * qd_ref[0]).astype(BF16), st.astype(BF16), preferred_element_type=F32)
        o = inner + cross
        kt = (kr * kd_ref[0]).T.astype(BF16)
        st = st * cd_ref[0] + jnp.dot(kt, v, preferred_element_type=F32)
        mu = jnp.mean(o, axis=-1, keepdims=True)
        oc = o - mu
        var = jnp.mean(oc * oc, axis=-1, keepdims=True)
        on = oc * lax.rsqrt(var + NORM_EPS)
        g = g_ref[sl, :].astype(F32)
        o_ref[sl, :] = (on * (g * _sigmoid(g))).astype(BF16)


def _retention(proj, tables, batch, seq):
    n = proj.shape[0]
    c = min(RET_CHUNK, seq)
    cos, sin, dm, qd, kd, cd = tables
    col = BRANCH_WIDTH * 3 // LANES
    step = BRANCH_WIDTH // LANES

    def pspec(k):
        return pl.BlockSpec((seq, LANES), lambda b, h: (b, col + k * step + h))

    return pl.pallas_call(
        functools.partial(_ret_kernel, seq=seq, c=c),
        grid=(batch, RET_HEADS),
        in_specs=[
            pspec(0), pspec(1), pspec(2), pspec(3),
            pl.BlockSpec((seq, LANES), lambda b, h: (0, 0)),
            pl.BlockSpec((seq, LANES), lambda b, h: (0, 0)),
            pl.BlockSpec((1, c, c), lambda b, h: (h, 0, 0)),
            pl.BlockSpec((1, c, LANES), lambda b, h: (h, 0, 0)),
            pl.BlockSpec((1, c, LANES), lambda b, h: (h, 0, 0)),
            pl.BlockSpec((1, LANES, LANES), lambda b, h: (h, 0, 0)),
        ],
        out_specs=pl.BlockSpec((seq, LANES), lambda b, h: (b, h)),
        out_shape=jax.ShapeDtypeStruct((n, BRANCH_WIDTH), BF16),
        compiler_params=_cparams(("arbitrary", "arbitrary")),
        name="retention",
    )(proj, proj, proj, proj, cos, sin, dm, qd, kd, cd)


def _retention_tables(seq):
    c = min(RET_CHUNK, seq)
    d = RET_HEAD_DIM
    inv = ROPE_BASE ** (-jnp.arange(0, d, 2, dtype=F32) / d)
    ang = jnp.arange(seq).astype(F32)[:, None] * inv[None, :]
    cos = jnp.cos(ang)
    sin = jnp.sin(ang)
    cos_full = jnp.concatenate([cos, cos], axis=-1)
    sin_signed = jnp.concatenate([-sin, sin], axis=-1)
    log_gamma = jnp.log1p(-jnp.exp2(-RET_GAMMA_EXP0 - jnp.arange(RET_HEADS, dtype=F32)))
    idx = jnp.arange(c, dtype=F32)
    dist = idx[:, None] - idx[None, :]
    dm = jnp.where(dist >= 0, jnp.exp(log_gamma[:, None, None] * jnp.maximum(dist, 0.0)), 0.0)
    qd = jnp.exp(log_gamma[:, None] * (idx + 1.0))
    kd = jnp.exp(log_gamma[:, None] * (c - 1.0 - idx))
    cd = jnp.exp(log_gamma * c)
    qd = jnp.broadcast_to(qd[:, :, None], (RET_HEADS, c, LANES))
    kd = jnp.broadcast_to(kd[:, :, None], (RET_HEADS, c, LANES))
    cd = jnp.broadcast_to(cd[:, None, None], (RET_HEADS, LANES, LANES))
    return cos_full, sin_signed, dm, qd, kd, cd


def _merge_kernel(od_ref, or_ref, of_ref, gd_ref, gr_ref, gf_ref, gb_ref, wb_ref, wo_ref, x_ref,
                  g1_ref, n2_ref, sh_ref, sc_ref, rw_ref, rb_ref,
                  xo_ref, hp_ref, ti_ref, tw_ref, cnt_ref, cnt_scr, *, tm, sub):
    @pl.when(pl.program_id(0) == 0)
    def _():
        cnt_scr[...] = jnp.zeros(cnt_scr.shape, F32)

    lane = lax.broadcasted_iota(I32, (sub, LANES), 1)
    r_i = lax.broadcasted_iota(I32, (sub, sub), 0)
    c_i = lax.broadcasted_iota(I32, (sub, sub), 1)
    earlier = (c_i < r_i).astype(BF16)
    counts = cnt_scr[...]

    blocks = [slice(s0, s0 + sub) for s0 in range(0, tm, sub)]

    def phase_branches(rs):
        merged = None
        for i, (o_ref, gate_ref) in enumerate(((od_ref, gd_ref), (or_ref, gr_ref), (of_ref, gf_ref))):
            gate = _sigmoid(gate_ref[rs, :] + gb_ref[i].astype(BF16))
            term = gate.astype(F32) * jnp.dot(o_ref[rs, :], wb_ref[i], preferred_element_type=F32)
            merged = term if merged is None else merged + term
        return merged.astype(BF16)

    def phase_outproj(rs, merged):
        mix = jnp.dot(merged, wo_ref[...], preferred_element_type=F32)
        xn = x_ref[rs, :] + g1_ref[0] * mix
        xo_ref[rs, :] = xn
        r = lax.rsqrt(jnp.mean(xn * xn, axis=-1, keepdims=True) + NORM_EPS)
        h = xn * r * n2_ref[...]
        h = h * (1.0 + sc_ref[0]) + sh_ref[0]
        hp = _pack_pairs(h)
        hp_ref[0, rs, :] = hp[:, :QUART]
        hp_ref[1, rs, :] = hp[:, QUART:]
        hh, hm, _ = _split3(h)
        return (jnp.dot(hh, rw_ref[0], preferred_element_type=F32)
                + jnp.dot(hh, rw_ref[1], preferred_element_type=F32)
                + jnp.dot(hm, rw_ref[0], preferred_element_type=F32)) + rb_ref[...]

    def phase_topk(rs, logits, counts):
        vals, idxs = [], []
        for _ in range(TOP_K):
            m = jnp.max(logits, axis=-1, keepdims=True)
            idx = jnp.min(jnp.where(logits == m, lane, LANES), axis=-1, keepdims=True)
            vals.append(m)
            idxs.append(idx)
            logits = jnp.where(lane == idx, -3.0e38, logits)
        es = [jnp.exp(v - vals[0]) for v in vals]
        den = es[0] + es[1] + es[2] + es[3]
        hits = [lane == idxs[k] for k in range(TOP_K)]
        picked = (hits[0] | hits[1] | hits[2] | hits[3])
        base = jnp.dot(earlier, picked.astype(BF16), preferred_element_type=F32) + counts
        counts = counts + jnp.sum(picked.astype(F32), axis=0, keepdims=True)
        ti = jnp.zeros((sub, LANES), I32)
        tw = jnp.zeros((sub, LANES), F32)
        for k in range(TOP_K):
            rank = jnp.sum(jnp.where(hits[k], base, 0.0), axis=-1, keepdims=True).astype(I32)
            ti = jnp.where(lane == k, idxs[k], ti)
            ti = jnp.where(lane == TOP_K + k, rank, ti)
            tw = jnp.where(lane == k, es[k] / den, tw)
        ti_ref[rs, :] = ti
        tw_ref[rs, :] = tw
        return counts

    merged = [phase_branches(rs) for rs in blocks]
    logits = [phase_outproj(rs, m) for rs, m in zip(blocks, merged)]
    for rs, lg in zip(blocks, logits):
        counts = phase_topk(rs, lg, counts)

    cnt_scr[...] = counts
    cnt_ref[...] = counts.astype(I32)


def _merge(o_diff, o_ret, o_fox, proj, gate_b, w_branch, w_out, x2, g1, n2g, sh2, sc2, rw, rb, seq):
    n, d = x2.shape
    tm = min(MERGE_TILE, seq)
    sub = min(MERGE_SUB, tm)
    per_seq = seq // tm
    gcol = GATE_COL0 // d

    def rows(w):
        return pl.BlockSpec((tm, w), lambda i: (i, 0))

    def gspec(k):
        return pl.BlockSpec((tm, d), lambda i: (i, gcol + k))

    def whole(shape):
        return pl.BlockSpec(shape, lambda i: (0,) * len(shape))

    def per_b():
        return pl.BlockSpec((1, 1, d), lambda i: (i // per_seq, 0, 0))

    return pl.pallas_call(
        functools.partial(_merge_kernel, tm=tm, sub=sub),
        grid=(n // tm,),
        in_specs=[
            rows(BRANCH_WIDTH), rows(BRANCH_WIDTH), rows(BRANCH_WIDTH),
            gspec(0), gspec(1), gspec(2),
            whole((N_BRANCHES, 1, d)), whole((N_BRANCHES, BRANCH_WIDTH, d)), whole((d, d)),
            rows(d), per_b(), whole((1, d)), per_b(), per_b(),
            whole((2, d, LANES)), whole((1, LANES)),
        ],
        out_specs=[rows(d), pl.BlockSpec((2, tm, QUART), lambda i: (0, i, 0)), rows(LANES), rows(LANES),
                   whole((1, LANES))],
        out_shape=[
            jax.ShapeDtypeStruct((n, d), F32),
            jax.ShapeDtypeStruct((2, n, QUART), I32),
            jax.ShapeDtypeStruct((n, LANES), I32),
            jax.ShapeDtypeStruct((n, LANES), F32),
            jax.ShapeDtypeStruct((1, LANES), I32),
        ],
        scratch_shapes=[pltpu.VMEM((1, LANES), F32)],
        compiler_params=_cparams(("arbitrary",)),
        name="merge_outproj_router",
    )(o_diff, o_ret, o_fox, proj, proj, proj, gate_b, w_branch, w_out, x2, g1, n2g, sh2, sc2, rw, rb)


def _prep_up_group(w_ref, o_ref, t_scr, g):
    f = t_scr.shape[1] // 2
    cols = slice(g * LANES, (g + 1) * LANES)
    t_scr[g] = w_ref[0, 0, cols, :].T
    o_ref[0, 0:f, cols] = t_scr[g, pl.ds(0, f, stride=2), :].astype(BF16)
    o_ref[0, f:, cols] = t_scr[g, pl.ds(1, f, stride=2), :].astype(BF16)


def _prep_up_kernel(w_ref, o_ref, t_scr):
    for g in range(t_scr.shape[0]):
        _prep_up_group(w_ref, o_ref, t_scr, g)


def _prep_up(w_up, layer):
    _, e, d, f2 = w_up.shape
    groups = d // LANES
    kt = groups * LANES
    return pl.pallas_call(
        _prep_up_kernel,
        grid=(e, d // kt),
        in_specs=[pl.BlockSpec((1, 1, kt, f2), lambda i, j: (layer, i, j, 0))],
        out_specs=pl.BlockSpec((1, f2, kt), lambda i, j: (i, 0, j)),
        out_shape=jax.ShapeDtypeStruct((e, f2, d), BF16),
        scratch_shapes=[pltpu.VMEM((groups, f2, LANES), F32)],
        compiler_params=_cparams(("arbitrary", "arbitrary")),
        name="expert_up_weight_prep",
    )(w_up)


def _expert_kernel(te_ref, nu_ref, xs_ref, *refs):
    y_ref = refs[-1]
    nt = (((1,), (1,)), ((), ()))
    first_tile = pl.program_id(0) * EXPERT_GROUP

    @pl.when(first_tile < nu_ref[0])
    def _():
        for s in range(EXPERT_GROUP):
            wu_ref, bu_ref, wd_ref, bd_ref = refs[4 * s:4 * s + 4]
            rs = slice(s * EXPERT_TILE, (s + 1) * EXPERT_TILE)
            lo, hi = _unpack_pairs(jnp.concatenate([xs_ref[0, rs, :], xs_ref[1, rs, :]], axis=1))
            lo = lo.astype(BF16)
            hi = hi.astype(BF16)

            def up_half(r0):
                return (lax.dot_general(lo, wu_ref[0, r0:r0 + D_FF, :HALF], nt, preferred_element_type=F32)
                        + lax.dot_general(hi, wu_ref[0, r0:r0 + D_FF, HALF:], nt, preferred_element_type=F32))

            glu = jnp.minimum(up_half(0) + bu_ref[0, :, :D_FF], SWIGLU_LIMIT)
            lin = jnp.clip(up_half(D_FF) + bu_ref[0, :, D_FF:], -SWIGLU_LIMIT, SWIGLU_LIMIT)
            act = glu * _sigmoid(SWIGLU_ALPHA * glu) * (lin + 1.0)
            down = jnp.dot(act.astype(BF16), wd_ref[0], preferred_element_type=F32) + bd_ref[0]
            yp = _pack_pairs(down)
            y_ref[0, rs, :] = yp[:, :QUART]
            y_ref[1, rs, :] = yp[:, QUART:]

    @pl.when(first_tile >= nu_ref[0])
    def _():
        y_ref[...] = jnp.zeros(y_ref.shape, I32)


def _experts(tile_e, n_used, xs, w_up, b_up, w_down, b_down):
    r = xs.shape[1]
    tm = EXPERT_TILE * EXPERT_GROUP
    d = D_MODEL
    in_specs = [pl.BlockSpec((2, tm, QUART), lambda i, te, nu: (0, i, 0))]
    args = [xs]
    for s in range(EXPERT_GROUP):
        def pick(i, te, nu, s=s):
            return (te[i * EXPERT_GROUP + s], 0, 0)
        in_specs += [pl.BlockSpec((1, 2 * D_FF, d), pick), pl.BlockSpec((1, 1, 2 * D_FF), pick),
                     pl.BlockSpec((1, D_FF, d), pick), pl.BlockSpec((1, 1, d), pick)]
        args += [w_up, b_up, w_down, b_down]
    grid_spec = pltpu.PrefetchScalarGridSpec(
        num_scalar_prefetch=2,
        grid=(r // tm,),
        in_specs=in_specs,
        out_specs=pl.BlockSpec((2, tm, QUART), lambda i, te, nu: (0, i, 0)),
    )
    return pl.pallas_call(
        _expert_kernel,
        grid_spec=grid_spec,
        out_shape=jax.ShapeDtypeStruct((2, r, QUART), I32),
        compiler_params=_cparams(("arbitrary",)),
        name="expert_mlp",
    )(tile_e, n_used, *args)


def _combine_kernel(y_ref, tw_ref, x_ref, g2_ref, fg_ref, o_ref):
    xn = _moe_residual(y_ref, tw_ref, x_ref, g2_ref)
    r = lax.rsqrt(jnp.mean(xn * xn, axis=-1, keepdims=True) + NORM_EPS)
    o_ref[...] = xn * r * fg_ref[...]


def _combine(ytok, top_w, x2, g2, final_g, seq):
    n, d = x2.shape
    tm = min(ROW_TILE_OUT, seq)
    per_seq = seq // tm
    return pl.pallas_call(
        _combine_kernel,
        grid=(n // tm,),
        in_specs=[
            pl.BlockSpec((2, TOP_K, tm, QUART), lambda i: (0, 0, i, 0)),
            pl.BlockSpec((tm, LANES), lambda i: (i, 0)),
            pl.BlockSpec((tm, d), lambda i: (i, 0)),
            pl.BlockSpec((1, 1, d), lambda i: (i // per_seq, 0, 0)),
            pl.BlockSpec((1, d), lambda i: (0, 0)),
        ],
        out_specs=pl.BlockSpec((tm, d), lambda i: (i, 0)),
        out_shape=jax.ShapeDtypeStruct((n, d), F32),
        compiler_params=_cparams(("arbitrary",)),
        name="moe_combine",
    )(ytok, top_w, x2, g2, final_g)


def _route(top_idx, counts, n_rows):
    tm = EXPERT_TILE
    e = top_idx[:, :TOP_K]
    rank = top_idx[:, TOP_K:2 * TOP_K]
    sizes = counts[0, :N_EXPERTS]
    padded = ((sizes + tm - 1) // tm) * tm
    pend = jnp.cumsum(padded)
    pstart = pend - padded
    onehot = e[:, :, None] == jnp.arange(N_EXPERTS, dtype=I32)[None, None, :]
    pos = jnp.sum(jnp.where(onehot, pstart[None, None, :], 0), axis=-1) + rank
    tiles = jnp.arange(n_rows // tm, dtype=I32) * tm
    tile_e = jnp.minimum(jnp.sum(tiles[:, None] >= pend[None, :], axis=-1), N_EXPERTS - 1).astype(I32)
    n_used = (pend[-1] // tm).astype(I32).reshape(1)
    return pos.T.astype(I32), tile_e, n_used


def _sc_gather(table, idx):
    m = idx.shape[0]
    width = table.shape[1]
    assert m % (SC_WINDOW * SC_WORKERS) == 0
    mesh = plsc.VectorSubcoreMesh(core_axis_name="c", subcore_axis_name="s")

    @pl.kernel(out_type=jax.ShapeDtypeStruct((m, width), table.dtype), mesh=mesh)
    def gather_rows(t_hbm, i_hbm, o_hbm):
        def body(i_vmem, o_vmem):
            pltpu.sync_copy(t_hbm.at[i_vmem.at[0]], o_vmem)

        pltpu.emit_pipeline(
            body,
            grid=(m // SC_WINDOW,),
            in_specs=[pl.BlockSpec((1, SC_WINDOW), lambda i: (0, i))],
            out_specs=[pl.BlockSpec((SC_WINDOW, width), lambda i: (i, 0))],
            core_axis_name=("c", "s"),
            dimension_semantics=(pltpu.PARALLEL,),
        )(i_hbm, o_hbm)

    return gather_rows(table, idx.reshape(1, m))


def _sc_scatter(rows, idx, n_out):
    t, width = rows.shape
    picks = idx.shape[0]
    assert t % SC_WINDOW == 0
    mesh = plsc.VectorSubcoreMesh(core_axis_name="c", subcore_axis_name="s")

    @pl.kernel(out_type=jax.ShapeDtypeStruct((n_out, width), rows.dtype), mesh=mesh)
    def scatter_rows(x_hbm, i_hbm, o_hbm):
        def body(x_vmem, i_vmem):
            for k in range(picks):
                pltpu.sync_copy(x_vmem, o_hbm.at[i_vmem.at[k]])

        pltpu.emit_pipeline(
            body,
            grid=(t // SC_WINDOW,),
            in_specs=[pl.BlockSpec((SC_WINDOW, width), lambda i: (i, 0)),
                      pl.BlockSpec((picks, SC_WINDOW), lambda i: (0, i))],
            out_specs=[],
            core_axis_name=("c", "s"),
            dimension_semantics=(pltpu.PARALLEL,),
        )(x_hbm, i_hbm)

    return scatter_rows(rows, idx)


def _dispatch(h2p, pos_t, n_rows):
    n = h2p.shape[1]
    idx = jnp.concatenate([pos_t, pos_t + n_rows], axis=1)
    return _sc_scatter(h2p.reshape(2 * n, QUART), idx, 2 * n_rows).reshape(2, n_rows, QUART)


def _collect(y, pos_t):
    n_rows = y.shape[1]
    flat = pos_t.reshape(-1)
    out = _sc_gather(y.reshape(2 * n_rows, QUART), jnp.concatenate([flat, flat + n_rows]))
    return out.reshape(2, TOP_K, pos_t.shape[1], QUART)


def kernel(x, c, norm1_g, norm2_g, ada_w, ada_b, w_in, gate_b, fox_fb, lam_q1, lam_k1, lam_q2, lam_k2,
           diff_subln_g, w_branch, w_out, router_w, router_b, exp_w_up, exp_b_up, exp_w_down,
           exp_b_down, final_g):
    batch, seq, d = x.shape
    depth = ada_w.shape[0]
    n = batch * seq
    x2 = x.reshape(n, d)

    mod = _ada_mod(c, ada_w, ada_b)
    tables = _retention_tables(seq)
    n_rows = n * TOP_K + N_EXPERTS * EXPERT_TILE
    blk = BRANCH_WIDTH // LANES

    w_in_t = jnp.swapaxes(w_in, 1, 2)
    pending = None
    for l in range(depth):
        lam_init = 0.8 - 0.6 * math.exp(-0.3 * l)
        sh1, sc1, g1, sh2, sc2, g2 = [m.reshape(batch, 1, d) for m in jnp.split(mod[l], 6, axis=-1)]

        w_main, w_f = _prep_win(w_in_t, l)
        outs = _inproj(x2, norm1_g[l].reshape(1, d), sh1, sc1, w_main, w_f, seq, pending)
        proj, flog = outs[0], outs[1]
        if pending is not None:
            x2 = outs[2]

        fb_pad = jnp.pad(fox_fb[l], (0, LANES - FOX_HEADS)).reshape(1, LANES)
        frow = _fcum(flog, fb_pad, batch, seq).reshape(batch * blk, 2, seq)

        lamv = jnp.stack([lam_q1[l], lam_k1[l], lam_q2[l], lam_k2[l]])
        diff_extra = (lamv, diff_subln_g[l].reshape(1, LANES))
        if _rider_fits(batch, exp_w_up):
            o_diff, w_up = _attention(proj, "diff", 0, blk, 2 * blk, batch, seq, diff_extra, lam_init,
                                      rider=("up", exp_w_up, l))
        else:
            o_diff = _attention(proj, "diff", 0, blk, 2 * blk, batch, seq, diff_extra, lam_init)
            w_up = _prep_up(exp_w_up, l)
        o_ret = _retention(proj, tables, batch, seq)
        if _rider_fits(batch, exp_w_down):
            o_fox, w_down = _attention(proj, "fox", 7 * blk, 8 * blk, 9 * blk, batch, seq, (frow,),
                                       rider=("down", exp_w_down, l))
        else:
            o_fox = _attention(proj, "fox", 7 * blk, 8 * blk, 9 * blk, batch, seq, (frow,))
            w_down = exp_w_down[l].astype(BF16)

        rw = jnp.pad(router_w[l], ((0, 0), (0, LANES - N_EXPERTS)))
        rw_hi = rw.astype(BF16)
        rw = jnp.stack([rw_hi, (rw - rw_hi.astype(F32)).astype(BF16)])
        rb = jnp.pad(router_b[l], (0, LANES - N_EXPERTS), constant_values=NEG_INF).reshape(1, LANES)
        x2, h2p, top_idx, top_w, counts = _merge(
            o_diff, o_ret, o_fox, proj, gate_b[l].reshape(N_BRANCHES, 1, d), w_branch[l].astype(BF16),
            w_out[l].astype(BF16), x2, g1, norm2_g[l].reshape(1, d), sh2, sc2, rw, rb, seq)

        pos_t, tile_e, n_used = _route(top_idx, counts, n_rows)
        xs = _dispatch(h2p, pos_t, n_rows)

        bu = exp_b_up[l]
        b_up = jnp.concatenate([bu[:, 0::2], bu[:, 1::2]], axis=-1).reshape(N_EXPERTS, 1, 2 * D_FF)
        y = _experts(tile_e, n_used, xs, w_up, b_up, w_down,
                     exp_b_down[l].reshape(N_EXPERTS, 1, d))
        pending = (_collect(y, pos_t), top_w, g2)

    ytok, top_w, g2 = pending
    x2 = _combine(ytok, top_w, x2, g2, final_g.reshape(1, d), seq)
    return x2.reshape(batch, seq, d)
```

```python
import functools
import math

import jax
import jax.numpy as jnp
from jax import lax
from jax.experimental import pallas as pl
from jax.experimental.pallas import tpu as pltpu
from jax.experimental.pallas import tpu_sc as plsc

F32 = jnp.float32
BF16 = jnp.bfloat16
I32 = jnp.int32
U32 = jnp.uint32

D_MODEL = 1024
HEAD_DIM = 64
BRANCH_WIDTH = D_MODEL // 2
N_BRANCHES = 3
FOX_HEADS = BRANCH_WIDTH // HEAD_DIM
RET_HEADS = 4
RET_HEAD_DIM = 128
ROPE_BASE = 10000.0
RET_GAMMA_EXP0 = 5.0
N_EXPERTS = 32
TOP_K = 4
D_FF = D_MODEL
SWIGLU_LIMIT = 7.0
SWIGLU_ALPHA = 1.702
NORM_EPS = 1e-5
NEG_INF = -1e30

LANES = 128
MAIN_COLS = 8192
GATE_COL0 = 10 * BRANCH_WIDTH
HALF = D_MODEL // 2
QUART = HALF // 2
SC_WINDOW = 128
SC_WORKERS = 32
VMEM_LIMIT = 48 * 1024 * 1024
VMEM_LIMIT_INPROJ = 56 * 1024 * 1024

ROW_TILE_IN = 512
COL_TILE_IN = 1024
ATTN_TILE = 256
RET_CHUNK = 256
ROW_TILE_OUT = 256
MERGE_TILE = 1024
MERGE_SUB = 256
EXPERT_TILE = 256
EXPERT_GROUP = 2
CUM_CHUNK = 256


def _cparams(sem):
    return pltpu.CompilerParams(dimension_semantics=sem, vmem_limit_bytes=VMEM_LIMIT)


def _sigmoid(x):
    return 0.5 * jnp.tanh(0.5 * x) + 0.5


def _pack_pairs(v):
    u = lax.bitcast_convert_type(v.astype(BF16).astype(F32), U32)
    half = v.shape[-1] // 2
    w = (u[:, half:] & jnp.uint32(0xFFFF0000)) | (u[:, :half] >> 16)
    return lax.bitcast_convert_type(w, I32)


def _unpack_pairs(w):
    u = lax.bitcast_convert_type(w, U32)
    lo = lax.bitcast_convert_type(u << 16, F32)
    hi = lax.bitcast_convert_type(u & jnp.uint32(0xFFFF0000), F32)
    return lo, hi


def _moe_residual(y_ref, tw_ref, x_ref, g2_ref):
    tw = tw_ref[...]
    parts = [None] * 4
    for k in range(TOP_K):
        w = tw[:, k:k + 1]
        for piece in range(2):
            lo, hi = _unpack_pairs(y_ref[piece, k])
            for slot, v in ((piece, lo), (2 + piece, hi)):
                parts[slot] = w * v if parts[slot] is None else parts[slot] + w * v
    return x_ref[...] + g2_ref[0] * jnp.concatenate(parts, axis=-1)


def _ada_kernel(c_ref, w_ref, b_ref, o_ref):
    c = c_ref[...]
    ca = c * _sigmoid(c)
    o_ref[0] = jnp.dot(ca, w_ref[0], preferred_element_type=F32) + b_ref[0]


def _ada_mod(c, ada_w, ada_b):
    depth, d, cols = ada_w.shape
    b = c.shape[0]
    tn = 1024
    return pl.pallas_call(
        _ada_kernel,
        grid=(depth, cols // tn),
        in_specs=[
            pl.BlockSpec((b, d), lambda l, j: (0, 0)),
            pl.BlockSpec((1, d, tn), lambda l, j: (l, 0, j)),
            pl.BlockSpec((1, 1, tn), lambda l, j: (l, 0, j)),
        ],
        out_specs=pl.BlockSpec((1, b, tn), lambda l, j: (l, 0, j)),
        out_shape=jax.ShapeDtypeStruct((depth, b, cols), F32),
        compiler_params=_cparams(("arbitrary", "arbitrary")),
        name="ada_mod",
    )(c, ada_w, ada_b.reshape(depth, 1, cols))


def _prep_win_kernel(w_ref, wm_ref, wf_ref):
    wm_ref[:GATE_COL0, :] = w_ref[0, :GATE_COL0, :].astype(BF16)
    wm_ref[GATE_COL0:, :] = w_ref[0, GATE_COL0 + FOX_HEADS:, :].astype(BF16)
    wf_ref[...] = jnp.zeros(wf_ref.shape, BF16)
    wf_ref[:FOX_HEADS, :] = w_ref[0, GATE_COL0:GATE_COL0 + FOX_HEADS, :].astype(BF16)


def _prep_win(w_t, layer):
    _, cols, d = w_t.shape
    ct = 256
    return pl.pallas_call(
        _prep_win_kernel,
        grid=(d // ct,),
        in_specs=[pl.BlockSpec((1, cols, ct), lambda i: (layer, 0, i))],
        out_specs=[pl.BlockSpec((MAIN_COLS, ct), lambda i: (0, i)), pl.BlockSpec((LANES, ct), lambda i: (0, i))],
        out_shape=[jax.ShapeDtypeStruct((MAIN_COLS, d), BF16), jax.ShapeDtypeStruct((LANES, d), BF16)],
        compiler_params=_cparams(("arbitrary",)),
        name="inproj_weight_prep",
    )(w_t)


def _inproj_kernel(*refs, moe):
    if moe:
        y_ref, tw_ref, x_ref, g2_ref, g_ref, sh_ref, sc_ref, w_ref, wf_ref, proj_ref, fl_ref, xo_ref = refs
        x = _moe_residual(y_ref, tw_ref, x_ref, g2_ref)
        xo_ref[...] = x
    else:
        x_ref, g_ref, sh_ref, sc_ref, w_ref, wf_ref, proj_ref, fl_ref = refs
        x = x_ref[...]
    r = lax.rsqrt(jnp.mean(x * x, axis=-1, keepdims=True) + NORM_EPS)
    h = x * r * g_ref[...]
    h = h * (1.0 + sc_ref[0]) + sh_ref[0]
    hb = h.astype(BF16)
    nt = (((1,), (1,)), ((), ()))
    fl_ref[...] = lax.dot_general(hb, wf_ref[...], nt, preferred_element_type=F32)
    for j in range(MAIN_COLS // COL_TILE_IN):
        cs = slice(j * COL_TILE_IN, (j + 1) * COL_TILE_IN)
        proj_ref[:, cs] = lax.dot_general(hb, w_ref[cs, :], nt, preferred_element_type=F32).astype(BF16)


def _inproj(x2, g, sh, sc, w_main, w_f, seq, moe=None):
    n, d = x2.shape
    tm = min(ROW_TILE_IN, seq)
    per_seq = seq // tm
    resident = pl.Buffered(1)

    def per_b():
        return pl.BlockSpec((1, 1, d), lambda i: (i // per_seq, 0, 0))

    in_specs = [
        pl.BlockSpec((tm, d), lambda i: (i, 0)),
        pl.BlockSpec((1, d), lambda i: (0, 0)),
        per_b(), per_b(),
        pl.BlockSpec((MAIN_COLS, d), lambda i: (0, 0), pipeline_mode=resident),
        pl.BlockSpec((LANES, d), lambda i: (0, 0), pipeline_mode=resident),
    ]
    out_specs = [pl.BlockSpec((tm, MAIN_COLS), lambda i: (i, 0)), pl.BlockSpec((tm, LANES), lambda i: (i, 0))]
    out_shape = [jax.ShapeDtypeStruct((n, MAIN_COLS), BF16), jax.ShapeDtypeStruct((n, LANES), F32)]
    args = [x2, g, sh, sc, w_main, w_f]
    if moe is not None:
        ytok, top_w, g2 = moe
        in_specs = [pl.BlockSpec((2, TOP_K, tm, QUART), lambda i: (0, 0, i, 0)),
                    pl.BlockSpec((tm, LANES), lambda i: (i, 0)), in_specs[0], per_b()] + in_specs[1:]
        args = [ytok, top_w, x2, g2] + args[1:]
        out_specs.append(pl.BlockSpec((tm, d), lambda i: (i, 0)))
        out_shape.append(jax.ShapeDtypeStruct((n, d), F32))
    return pl.pallas_call(
        functools.partial(_inproj_kernel, moe=moe is not None),
        grid=(n // tm,),
        in_specs=in_specs,
        out_specs=out_specs,
        out_shape=out_shape,
        compiler_params=pltpu.CompilerParams(dimension_semantics=("arbitrary",), vmem_limit_bytes=VMEM_LIMIT_INPROJ),
        name="inproj",
    )(*args)


def _split3(v):
    hi = v.astype(BF16)
    r1 = v - hi.astype(F32)
    mid = r1.astype(BF16)
    lo = (r1 - mid.astype(F32)).astype(BF16)
    return hi, mid, lo


def _fcum_kernel(fl_ref, fb_ref, o_ref, *, seq):
    c = min(CUM_CHUNK, seq)
    row = lax.broadcasted_iota(I32, (c, c), 0)
    col = lax.broadcasted_iota(I32, (c, c), 1)
    lower = (col <= row).astype(BF16)
    carry = jnp.zeros((1, LANES), F32)
    for ci in range(seq // c):
        x = fl_ref[pl.ds(ci * c, c), :] + fb_ref[...]
        lf = jnp.minimum(x, 0.0) - jnp.log(1.0 + jnp.exp(-jnp.abs(x)))
        hi, mid, lo = _split3(lf)
        cs = (jnp.dot(lower, hi, preferred_element_type=F32)
              + jnp.dot(lower, mid, preferred_element_type=F32)
              + jnp.dot(lower, lo, preferred_element_type=F32)) + carry
        o_ref[0, :, pl.ds(ci * c, c)] = cs.T[:FOX_HEADS, :]
        carry = cs[c - 1:c, :]


def _fcum(flog, fb_pad, batch, seq):
    return pl.pallas_call(
        functools.partial(_fcum_kernel, seq=seq),
        grid=(batch,),
        in_specs=[
            pl.BlockSpec((seq, LANES), lambda b: (b, 0)),
            pl.BlockSpec((1, LANES), lambda b: (0, 0)),
        ],
        out_specs=pl.BlockSpec((1, FOX_HEADS, seq), lambda b: (b, 0, 0)),
        out_shape=jax.ShapeDtypeStruct((batch, FOX_HEADS, seq), F32),
        compiler_params=_cparams(("arbitrary",)),
        name="fox_decay_cumsum",
    )(flog, fb_pad)


def _attn_kernel(*refs, mode, t, seq, lam_init, rider):
    if rider:
        refs = list(refs)
        wt_scr = refs.pop() if rider == "up" else None
        n_in = 5 if mode == "diff" else 4
        wraw_ref = refs.pop(n_in)
        wprep_ref = refs.pop(n_in + 1)
    if mode == "diff":
        q_ref, k_ref, v_ref, lam_ref, g_ref, o_ref, vx_scr = refs
        f_ref = None
        lv = lam_ref[...]
        lam = (jnp.exp(jnp.sum(lv[0:1, :] * lv[1:2, :])) - jnp.exp(jnp.sum(lv[2:3, :] * lv[3:4, :]))
               + lam_init)
    else:
        q_ref, k_ref, v_ref, f_ref, o_ref, vx_scr = refs
    lane = lax.broadcasted_iota(I32, (t, LANES), 1)
    row = lax.broadcasted_iota(I32, (t, t), 0)
    col = lax.broadcasted_iota(I32, (t, t), 1)
    nt = (((1,), (1,)), ((), ()))

    v_all = v_ref[...]
    one = jnp.ones_like(v_all)
    if mode == "diff":
        vx_scr[0] = jnp.concatenate([v_all, one], axis=1)
    else:
        lane_s = lax.broadcasted_iota(I32, v_all.shape, 1)
        vx_scr[0] = jnp.where(lane_s < HEAD_DIM, v_all, one)
        vx_scr[1] = jnp.where(lane_s >= HEAD_DIM, v_all, one)

    n_strips = seq // t
    pairs = [(i, n_strips - 1 - i) for i in range(n_strips // 2)] if n_strips % 2 == 0 else [(i,) for i in range(n_strips)]
    for gi, group in enumerate(pairs):
        if rider == "up":
            for g in range(gi, wt_scr.shape[0], len(pairs)):
                _prep_up_group(wraw_ref, wprep_ref, wt_scr, g)
        elif rider == "down":
            span = wprep_ref.shape[1] // len(pairs)
            rows = slice(gi * span, (gi + 1) * span)
            wprep_ref[0, rows, :] = wraw_ref[0, 0, rows, :].astype(BF16)
        units = [(qi, idx) for qi in group for idx in range(2)]
        qms = {}
        for qi in group:
            q = q_ref[qi * t:(qi + 1) * t, :] * jnp.asarray(HEAD_DIM ** -0.5, BF16)
            zero = jnp.zeros_like(q)
            qms[qi, 0] = jnp.where(lane < HEAD_DIM, q, zero)
            qms[qi, 1] = jnp.where(lane >= HEAD_DIM, q, zero)

        def scores(u, k0, k1):
            s_ = lax.dot_general(qms[u], k_ref[k0:k1, :], nt, preferred_element_type=F32)
            if f_ref is not None:
                s_ = s_ - f_ref[0, u[1]:u[1] + 1, k0:k1]
            return s_

        sds = {u: jnp.where(col <= row, scores(u, u[0] * t, (u[0] + 1) * t), NEG_INF) for u in units}
        sms = {u: scores(u, 0, u[0] * t) for u in units if u[0] > 0}
        ms = {}
        for u in units:
            m = jnp.max(sds[u], axis=-1, keepdims=True)
            if u in sms:
                m = jnp.maximum(m, jnp.max(sms[u], axis=-1, keepdims=True))
            ms[u] = m
        pds = {u: jnp.exp((sds[u] - ms[u]).astype(BF16)) for u in units}
        pms = {u: jnp.exp((sms[u] - ms[u]).astype(BF16)) for u in sms}
        outs = {}
        for u in units:
            vx = vx_scr.at[0 if mode == "diff" else u[1]]
            lo = u[0] * t
            acc = jnp.dot(pds[u], vx[lo:lo + t, :], preferred_element_type=F32)
            if u in pms:
                acc = acc + jnp.dot(pms[u], vx[0:lo, :], preferred_element_type=F32)
            outs[u] = acc
        for qi in group:
            o0, o1 = outs[qi, 0], outs[qi, 1]
            if mode == "diff":
                a1 = o0[:, :LANES] / o0[:, LANES:]
                a2 = o1[:, :LANES] / o1[:, LANES:]
                o = a1 - lam * a2
                r = lax.rsqrt(jnp.mean(o * o, axis=-1, keepdims=True) + NORM_EPS)
                o = o * r * g_ref[...] * (1.0 - lam_init)
            else:
                num = jnp.where(lane < HEAD_DIM, o0, o1)
                den = jnp.where(lane < HEAD_DIM, pltpu.roll(o0, HEAD_DIM, 1), pltpu.roll(o1, HEAD_DIM, 1))
                o = num / den
            o_ref[qi * t:(qi + 1) * t, :] = o.astype(BF16)


def _attention(proj, mode, col_q, col_k, col_v, batch, seq, extra, lam_init=0.0, rider=None):
    n = proj.shape[0]
    t = min(ATTN_TILE, seq)
    n_blk = BRANCH_WIDTH // LANES
    in_specs = [
        pl.BlockSpec((seq, LANES), lambda b, h: (b, col_q + h)),
        pl.BlockSpec((seq, LANES), lambda b, h: (b, col_k + h)),
        pl.BlockSpec((seq, LANES), lambda b, h: (b, col_v + h)),
    ]
    if mode == "diff":
        lamv, g = extra
        in_specs += [pl.BlockSpec((4, HEAD_DIM), lambda b, h: (0, 0)),
                     pl.BlockSpec((1, LANES), lambda b, h: (0, 0))]
        args = (proj, proj, proj, lamv, g)
    else:
        (frow,) = extra
        in_specs += [pl.BlockSpec((1, 2, seq), lambda b, h: (b * n_blk + h, 0, 0))]
        args = (proj, proj, proj, frow)
    out_specs = [pl.BlockSpec((seq, LANES), lambda b, h: (b, h))]
    out_shape = [jax.ShapeDtypeStruct((n, BRANCH_WIDTH), BF16)]
    scratch = [pltpu.VMEM((1, seq, 2 * LANES) if mode == "diff" else (2, seq, LANES), BF16)]
    if rider is not None:
        kind, w, layer = rider
        _, e, rows, cols = w.shape
        parts = batch * n_blk // e
        kt = rows // parts
        in_specs.append(pl.BlockSpec((1, 1, kt, cols), lambda b, h: (layer, (b * n_blk + h) // parts, (b * n_blk + h) % parts, 0)))
        args = args + (w,)
        if kind == "up":
            out_specs.append(pl.BlockSpec((1, cols, kt), lambda b, h: ((b * n_blk + h) // parts, 0, (b * n_blk + h) % parts)))
            out_shape.append(jax.ShapeDtypeStruct((e, cols, rows), BF16))
            scratch.append(pltpu.VMEM((kt // LANES, cols, LANES), F32))
        else:
            out_specs.append(pl.BlockSpec((1, kt, cols), lambda b, h: ((b * n_blk + h) // parts, (b * n_blk + h) % parts, 0)))
            out_shape.append(jax.ShapeDtypeStruct((e, rows, cols), BF16))
    outs = pl.pallas_call(
        functools.partial(_attn_kernel, mode=mode, t=t, seq=seq, lam_init=lam_init, rider=rider[0] if rider else None),
        grid=(batch, n_blk),
        in_specs=in_specs,
        out_specs=out_specs,
        out_shape=out_shape,
        scratch_shapes=scratch,
        compiler_params=_cparams(("arbitrary", "arbitrary")),
        name="attn_" + mode,
    )(*args)
    return outs if rider is not None else outs[0]


def _rider_fits(batch, w):
    _, e, rows, _ = w.shape
    steps = batch * (BRANCH_WIDTH // LANES)
    return steps % e == 0 and (rows // LANES) % (steps // e) == 0


def _ret_kernel(q_ref, k_ref, v_ref, g_ref, cos_ref, sin_ref, dm_ref, qd_ref, kd_ref, cd_ref,
                o_ref, *, seq, c):
    st = jnp.zeros((RET_HEAD_DIM, RET_HEAD_DIM), F32)
    half = RET_HEAD_DIM // 2

    for ci in range(seq // c):
        sl = pl.ds(ci * c, c)
        cos = cos_ref[sl, :]
        sin = sin_ref[sl, :]
        q = q_ref[sl, :].astype(F32)
        k = k_ref[sl, :].astype(F32)
        qr = q * cos + pltpu.roll(q, half, 1) * sin
        kr = (k * cos + pltpu.roll(k, half, 1) * sin) * (RET_HEAD_DIM ** -0.5)
        v = v_ref[sl, :]
        sc = lax.dot_general(qr.astype(BF16), kr.astype(BF16), (((1,), (1,)), ((), ())),
                             preferred_element_type=F32) * dm_ref[0]
        inner = jnp.dot(sc.astype(BF16), v, preferred_element_type=F32)
        cross = jnp.dot((qr * qd_ref[0]).astype(BF16), st.astype(BF16), preferred_element_type=F32)
        o = inner + cross
        kt = (kr * kd_ref[0]).T.astype(BF16)
        st = st * cd_ref[0] + jnp.dot(kt, v, preferred_element_type=F32)
        mu = jnp.mean(o, axis=-1, keepdims=True)
        oc = o - mu
        var = jnp.mean(oc * oc, axis=-1, keepdims=True)
        on = oc * lax.rsqrt(var + NORM_EPS)
        g = g_ref[sl, :].astype(F32)
        o_ref[sl, :] = (on * (g * _sigmoid(g))).astype(BF16)


def _retention(proj, tables, batch, seq):
    n = proj.shape[0]
    c = min(RET_CHUNK, seq)
    cos, sin, dm, qd, kd, cd = tables
    col = BRANCH_WIDTH * 3 // LANES
    step = BRANCH_WIDTH // LANES

    def pspec(k):
        return pl.BlockSpec((seq, LANES), lambda b, h: (b, col + k * step + h))

    return pl.pallas_call(
        functools.partial(_ret_kernel, seq=seq, c=c),
        grid=(batch, RET_HEADS),
        in_specs=[
            pspec(0), pspec(1), pspec(2), pspec(3),
            pl.BlockSpec((seq, LANES), lambda b, h: (0, 0)),
            pl.BlockSpec((seq, LANES), lambda b, h: (0, 0)),
            pl.BlockSpec((1, c, c), lambda b, h: (h, 0, 0)),
            pl.BlockSpec((1, c, LANES), lambda b, h: (h, 0, 0)),
            pl.BlockSpec((1, c, LANES), lambda b, h: (h, 0, 0)),
            pl.BlockSpec((1, LANES, LANES), lambda b, h: (h, 0, 0)),
        ],
        out_specs=pl.BlockSpec((seq, LANES), lambda b, h: (b, h)),
        out_shape=jax.ShapeDtypeStruct((n, BRANCH_WIDTH), BF16),
        compiler_params=_cparams(("arbitrary", "arbitrary")),
        name="retention",
    )(proj, proj, proj, proj, cos, sin, dm, qd, kd, cd)


def _retention_tables(seq):
    c = min(RET_CHUNK, seq)
    d = RET_HEAD_DIM
    inv = ROPE_BASE ** (-jnp.arange(0, d, 2, dtype=F32) / d)
    ang = jnp.arange(seq).astype(F32)[:, None] * inv[None, :]
    cos = jnp.cos(ang)
    sin = jnp.sin(ang)
    cos_full = jnp.concatenate([cos, cos], axis=-1)
    sin_signed = jnp.concatenate([-sin, sin], axis=-1)
    log_gamma = jnp.log1p(-jnp.exp2(-RET_GAMMA_EXP0 - jnp.arange(RET_HEADS, dtype=F32)))
    idx = jnp.arange(c, dtype=F32)
    dist = idx[:, None] - idx[None, :]
    dm = jnp.where(dist >= 0, jnp.exp(log_gamma[:, None, None] * jnp.maximum(dist, 0.0)), 0.0)
    qd = jnp.exp(log_gamma[:, None] * (idx + 1.0))
    kd = jnp.exp(log_gamma[:, None] * (c - 1.0 - idx))
    cd = jnp.exp(log_gamma * c)
    qd = jnp.broadcast_to(qd[:, :, None], (RET_HEADS, c, LANES))
    kd = jnp.broadcast_to(kd[:, :, None], (RET_HEADS, c, LANES))
    cd = jnp.broadcast_to(cd[:, None, None], (RET_HEADS, LANES, LANES))
    return cos_full, sin_signed, dm, qd, kd, cd


def _merge_kernel(od_ref, or_ref, of_ref, gd_ref, gr_ref, gf_ref, gb_ref, wb_ref, wo_ref, x_ref,
                  g1_ref, n2_ref, sh_ref, sc_ref, rw_ref, rb_ref,
                  xo_ref, hp_ref, ti_ref, tw_ref, cnt_ref, cnt_scr, *, tm, sub):
    @pl.when(pl.program_id(0) == 0)
    def _():
        cnt_scr[...] = jnp.zeros(cnt_scr.shape, F32)

    lane = lax.broadcasted_iota(I32, (sub, LANES), 1)
    r_i = lax.broadcasted_iota(I32, (sub, sub), 0)
    c_i = lax.broadcasted_iota(I32, (sub, sub), 1)
    earlier = (c_i < r_i).astype(BF16)
    counts = cnt_scr[...]

    blocks = [slice(s0, s0 + sub) for s0 in range(0, tm, sub)]

    def phase_branches(rs):
        merged = None
        for i, (o_ref, gate_ref) in enumerate(((od_ref, gd_ref), (or_ref, gr_ref), (of_ref, gf_ref))):
            gate = _sigmoid(gate_ref[rs, :] + gb_ref[i].astype(BF16))
            term = gate.astype(F32) * jnp.dot(o_ref[rs, :], wb_ref[i], preferred_element_type=F32)
            merged = term if merged is None else merged + term
        return merged.astype(BF16)

    def phase_outproj(rs, merged):
        mix = jnp.dot(merged, wo_ref[...], preferred_element_type=F32)
        xn = x_ref[rs, :] + g1_ref[0] * mix
        xo_ref[rs, :] = xn
        r = lax.rsqrt(jnp.mean(xn * xn, axis=-1, keepdims=True) + NORM_EPS)
        h = xn * r * n2_ref[...]
        h = h * (1.0 + sc_ref[0]) + sh_ref[0]
        hp = _pack_pairs(h)
        hp_ref[0, rs, :] = hp[:, :QUART]
        hp_ref[1, rs, :] = hp[:, QUART:]
        hh, hm, _ = _split3(h)
        return (jnp.dot(hh, rw_ref[0], preferred_element_type=F32)
                + jnp.dot(hh, rw_ref[1], preferred_element_type=F32)
                + jnp.dot(hm, rw_ref[0], preferred_element_type=F32)) + rb_ref[...]

    def phase_topk(rs, logits, counts):
        vals, idxs = [], []
        for _ in range(TOP_K):
            m = jnp.max(logits, axis=-1, keepdims=True)
            idx = jnp.min(jnp.where(logits == m, lane, LANES), axis=-1, keepdims=True)
            vals.append(m)
            idxs.append(idx)
            logits = jnp.where(lane == idx, -3.0e38, logits)
        es = [jnp.exp(v - vals[0]) for v in vals]
        den = es[0] + es[1] + es[2] + es[3]
        hits = [lane == idxs[k] for k in range(TOP_K)]
        picked = (hits[0] | hits[1] | hits[2] | hits[3])
        base = jnp.dot(earlier, picked.astype(BF16), preferred_element_type=F32) + counts
        counts = counts + jnp.sum(picked.astype(F32), axis=0, keepdims=True)
        ti = jnp.zeros((sub, LANES), I32)
        tw = jnp.zeros((sub, LANES), F32)
        for k in range(TOP_K):
            rank = jnp.sum(jnp.where(hits[k], base, 0.0), axis=-1, keepdims=True).astype(I32)
            ti = jnp.where(lane == k, idxs[k], ti)
            ti = jnp.where(lane == TOP_K + k, rank, ti)
            tw = jnp.where(lane == k, es[k] / den, tw)
        ti_ref[rs, :] = ti
        tw_ref[rs, :] = tw
        return counts

    merged = [phase_branches(rs) for rs in blocks]
    logits = [phase_outproj(rs, m) for rs, m in zip(blocks, merged)]
    for rs, lg in zip(blocks, logits):
        counts = phase_topk(rs, lg, counts)

    cnt_scr[...] = counts
    cnt_ref[...] = counts.astype(I32)


def _merge(o_diff, o_ret, o_fox, proj, gate_b, w_branch, w_out, x2, g1, n2g, sh2, sc2, rw, rb, seq):
    n, d = x2.shape
    tm = min(MERGE_TILE, seq)
    sub = min(MERGE_SUB, tm)
    per_seq = seq // tm
    gcol = GATE_COL0 // d

    def rows(w):
        return pl.BlockSpec((tm, w), lambda i: (i, 0))

    def gspec(k):
        return pl.BlockSpec((tm, d), lambda i: (i, gcol + k))

    def whole(shape):
        return pl.BlockSpec(shape, lambda i: (0,) * len(shape))

    def per_b():
        return pl.BlockSpec((1, 1, d), lambda i: (i // per_seq, 0, 0))

    return pl.pallas_call(
        functools.partial(_merge_kernel, tm=tm, sub=sub),
        grid=(n // tm,),
        in_specs=[
            rows(BRANCH_WIDTH), rows(BRANCH_WIDTH), rows(BRANCH_WIDTH),
            gspec(0), gspec(1), gspec(2),
            whole((N_BRANCHES, 1, d)), whole((N_BRANCHES, BRANCH_WIDTH, d)), whole((d, d)),
            rows(d), per_b(), whole((1, d)), per_b(), per_b(),
            whole((2, d, LANES)), whole((1, LANES)),
        ],
        out_specs=[rows(d), pl.BlockSpec((2, tm, QUART), lambda i: (0, i, 0)), rows(LANES), rows(LANES),
                   whole((1, LANES))],
        out_shape=[
            jax.ShapeDtypeStruct((n, d), F32),
            jax.ShapeDtypeStruct((2, n, QUART), I32),
            jax.ShapeDtypeStruct((n, LANES), I32),
            jax.ShapeDtypeStruct((n, LANES), F32),
            jax.ShapeDtypeStruct((1, LANES), I32),
        ],
        scratch_shapes=[pltpu.VMEM((1, LANES), F32)],
        compiler_params=_cparams(("arbitrary",)),
        name="merge_outproj_router",
    )(o_diff, o_ret, o_fox, proj, proj, proj, gate_b, w_branch, w_out, x2, g1, n2g, sh2, sc2, rw, rb)


def _prep_up_group(w_ref, o_ref, t_scr, g):
    f = t_scr.shape[1] // 2
    cols = slice(g * LANES, (g + 1) * LANES)
    t_scr[g] = w_ref[0, 0, cols, :].T
    o_ref[0, 0:f, cols] = t_scr[g, pl.ds(0, f, stride=2), :].astype(BF16)
    o_ref[0, f:, cols] = t_scr[g, pl.ds(1, f, stride=2), :].astype(BF16)


def _prep_up_kernel(w_ref, o_ref, t_scr):
    for g in range(t_scr.shape[0]):
        _prep_up_group(w_ref, o_ref, t_scr, g)


def _prep_up(w_up, layer):
    _, e, d, f2 = w_up.shape
    groups = d // LANES
    kt = groups * LANES
    return pl.pallas_call(
        _prep_up_kernel,
        grid=(e, d // kt),
        in_specs=[pl.BlockSpec((1, 1, kt, f2), lambda i, j: (layer, i, j, 0))],
        out_specs=pl.BlockSpec((1, f2, kt), lambda i, j: (i, 0, j)),
        out_shape=jax.ShapeDtypeStruct((e, f2, d), BF16),
        scratch_shapes=[pltpu.VMEM((groups, f2, LANES), F32)],
        compiler_params=_cparams(("arbitrary", "arbitrary")),
        name="expert_up_weight_prep",
    )(w_up)


def _expert_kernel(te_ref, nu_ref, xs_ref, *refs):
    y_ref = refs[-1]
    nt = (((1,), (1,)), ((), ()))
    first_tile = pl.program_id(0) * EXPERT_GROUP

    @pl.when(first_tile < nu_ref[0])
    def _():
        for s in range(EXPERT_GROUP):
            wu_ref, bu_ref, wd_ref, bd_ref = refs[4 * s:4 * s + 4]
            rs = slice(s * EXPERT_TILE, (s + 1) * EXPERT_TILE)
            lo, hi = _unpack_pairs(jnp.concatenate([xs_ref[0, rs, :], xs_ref[1, rs, :]], axis=1))
            lo = lo.astype(BF16)
            hi = hi.astype(BF16)

            def up_half(r0):
                return (lax.dot_general(lo, wu_ref[0, r0:r0 + D_FF, :HALF], nt, preferred_element_type=F32)
                        + lax.dot_general(hi, wu_ref[0, r0:r0 + D_FF, HALF:], nt, preferred_element_type=F32))

            glu = jnp.minimum(up_half(0) + bu_ref[0, :, :D_FF], SWIGLU_LIMIT)
            lin = jnp.clip(up_half(D_FF) + bu_ref[0, :, D_FF:], -SWIGLU_LIMIT, SWIGLU_LIMIT)
            act = glu * _sigmoid(SWIGLU_ALPHA * glu) * (lin + 1.0)
            down = jnp.dot(act.astype(BF16), wd_ref[0], preferred_element_type=F32) + bd_ref[0]
            yp = _pack_pairs(down)
            y_ref[0, rs, :] = yp[:, :QUART]
            y_ref[1, rs, :] = yp[:, QUART:]

    @pl.when(first_tile >= nu_ref[0])
    def _():
        y_ref[...] = jnp.zeros(y_ref.shape, I32)


def _experts(tile_e, n_used, xs, w_up, b_up, w_down, b_down):
    r = xs.shape[1]
    tm = EXPERT_TILE * EXPERT_GROUP
    d = D_MODEL
    in_specs = [pl.BlockSpec((2, tm, QUART), lambda i, te, nu: (0, i, 0))]
    args = [xs]
    for s in range(EXPERT_GROUP):
        def pick(i, te, nu, s=s):
            return (te[i * EXPERT_GROUP + s], 0, 0)
        in_specs += [pl.BlockSpec((1, 2 * D_FF, d), pick), pl.BlockSpec((1, 1, 2 * D_FF), pick),
                     pl.BlockSpec((1, D_FF, d), pick), pl.BlockSpec((1, 1, d), pick)]
        args += [w_up, b_up, w_down, b_down]
    grid_spec = pltpu.PrefetchScalarGridSpec(
        num_scalar_prefetch=2,
        grid=(r // tm,),
        in_specs=in_specs,
        out_specs=pl.BlockSpec((2, tm, QUART), lambda i, te, nu: (0, i, 0)),
    )
    return pl.pallas_call(
        _expert_kernel,
        grid_spec=grid_spec,
        out_shape=jax.ShapeDtypeStruct((2, r, QUART), I32),
        compiler_params=_cparams(("arbitrary",)),
        name="expert_mlp",
    )(tile_e, n_used, *args)


def _combine_kernel(y_ref, tw_ref, x_ref, g2_ref, fg_ref, o_ref):
    xn = _moe_residual(y_ref, tw_ref, x_ref, g2_ref)
    r = lax.rsqrt(jnp.mean(xn * xn, axis=-1, keepdims=True) + NORM_EPS)
    o_ref[...] = xn * r * fg_ref[...]


def _combine(ytok, top_w, x2, g2, final_g, seq):
    n, d = x2.shape
    tm = min(ROW_TILE_OUT, seq)
    per_seq = seq // tm
    return pl.pallas_call(
        _combine_kernel,
        grid=(n // tm,),
        in_specs=[
            pl.BlockSpec((2, TOP_K, tm, QUART), lambda i: (0, 0, i, 0)),
            pl.BlockSpec((tm, LANES), lambda i: (i, 0)),
            pl.BlockSpec((tm, d), lambda i: (i, 0)),
            pl.BlockSpec((1, 1, d), lambda i: (i // per_seq, 0, 0)),
            pl.BlockSpec((1, d), lambda i: (0, 0)),
        ],
        out_specs=pl.BlockSpec((tm, d), lambda i: (i, 0)),
        out_shape=jax.ShapeDtypeStruct((n, d), F32),
        compiler_params=_cparams(("arbitrary",)),
        name="moe_combine",
    )(ytok, top_w, x2, g2, final_g)


def _route(top_idx, counts, n_rows):
    tm = EXPERT_TILE
    e = top_idx[:, :TOP_K]
    rank = top_idx[:, TOP_K:2 * TOP_K]
    sizes = counts[0, :N_EXPERTS]
    padded = ((sizes + tm - 1) // tm) * tm
    pend = jnp.cumsum(padded)
    pstart = pend - padded
    onehot = e[:, :, None] == jnp.arange(N_EXPERTS, dtype=I32)[None, None, :]
    pos = jnp.sum(jnp.where(onehot, pstart[None, None, :], 0), axis=-1) + rank
    tiles = jnp.arange(n_rows // tm, dtype=I32) * tm
    tile_e = jnp.minimum(jnp.sum(tiles[:, None] >= pend[None, :], axis=-1), N_EXPERTS - 1).astype(I32)
    n_used = (pend[-1] // tm).astype(I32).reshape(1)
    return pos.T.astype(I32), tile_e, n_used


def _sc_gather(table, idx):
    m = idx.shape[0]
    width = table.shape[1]
    assert m % (SC_WINDOW * SC_WORKERS) == 0
    mesh = plsc.VectorSubcoreMesh(core_axis_name="c", subcore_axis_name="s")

    @pl.kernel(out_type=jax.ShapeDtypeStruct((m, width), table.dtype), mesh=mesh)
    def gather_rows(t_hbm, i_hbm, o_hbm):
        def body(i_vmem, o_vmem):
            pltpu.sync_copy(t_hbm.at[i_vmem.at[0]], o_vmem)

        pltpu.emit_pipeline(
            body,
            grid=(m // SC_WINDOW,),
            in_specs=[pl.BlockSpec((1, SC_WINDOW), lambda i: (0, i))],
            out_specs=[pl.BlockSpec((SC_WINDOW, width), lambda i: (i, 0))],
            core_axis_name=("c", "s"),
            dimension_semantics=(pltpu.PARALLEL,),
        )(i_hbm, o_hbm)

    return gather_rows(table, idx.reshape(1, m))


def _sc_scatter(rows, idx, n_out):
    t, width = rows.shape
    picks = idx.shape[0]
    assert t % SC_WINDOW == 0
    mesh = plsc.VectorSubcoreMesh(core_axis_name="c", subcore_axis_name="s")

    @pl.kernel(out_type=jax.ShapeDtypeStruct((n_out, width), rows.dtype), mesh=mesh,
               scratch_types=[pltpu.SemaphoreType.DMA])
    def scatter_rows(x_hbm, i_hbm, o_hbm, sem):
        def body(x_vmem, i_vmem):
            copies = [pltpu.async_copy(x_vmem, o_hbm.at[i_vmem.at[k]], sem) for k in range(picks)]
            for cp in copies:
                cp.wait()

        pltpu.emit_pipeline(
            body,
            grid=(t // SC_WINDOW,),
            in_specs=[pl.BlockSpec((SC_WINDOW, width), lambda i: (i, 0)),
                      pl.BlockSpec((picks, SC_WINDOW), lambda i: (0, i))],
            out_specs=[],
            core_axis_name=("c", "s"),
            dimension_semantics=(pltpu.PARALLEL,),
        )(x_hbm, i_hbm)

    return scatter_rows(rows, idx)


def _dispatch(h2p, pos_t, n_rows):
    n = h2p.shape[1]
    idx = jnp.concatenate([pos_t, pos_t + n_rows], axis=1)
    return _sc_scatter(h2p.reshape(2 * n, QUART), idx, 2 * n_rows).reshape(2, n_rows, QUART)


def _collect(y, pos_t):
    n_rows = y.shape[1]
    flat = pos_t.reshape(-1)
    out = _sc_gather(y.reshape(2 * n_rows, QUART), jnp.concatenate([flat, flat + n_rows]))
    return out.reshape(2, TOP_K, pos_t.shape[1], QUART)


def kernel(x, c, norm1_g, norm2_g, ada_w, ada_b, w_in, gate_b, fox_fb, lam_q1, lam_k1, lam_q2, lam_k2,
           diff_subln_g, w_branch, w_out, router_w, router_b, exp_w_up, exp_b_up, exp_w_down,
           exp_b_down, final_g):
    batch, seq, d = x.shape
    depth = ada_w.shape[0]
    n = batch * seq
    x2 = x.reshape(n, d)

    mod = _ada_mod(c, ada_w, ada_b)
    tables = _retention_tables(seq)
    n_rows = n * TOP_K + N_EXPERTS * EXPERT_TILE
    blk = BRANCH_WIDTH // LANES

    w_in_t = jnp.swapaxes(w_in, 1, 2)
    pending = None
    for l in range(depth):
        lam_init = 0.8 - 0.6 * math.exp(-0.3 * l)
        sh1, sc1, g1, sh2, sc2, g2 = [m.reshape(batch, 1, d) for m in jnp.split(mod[l], 6, axis=-1)]

        w_main, w_f = _prep_win(w_in_t, l)
        outs = _inproj(x2, norm1_g[l].reshape(1, d), sh1, sc1, w_main, w_f, seq, pending)
        proj, flog = outs[0], outs[1]
        if pending is not None:
            x2 = outs[2]

        fb_pad = jnp.pad(fox_fb[l], (0, LANES - FOX_HEADS)).reshape(1, LANES)
        frow = _fcum(flog, fb_pad, batch, seq).reshape(batch * blk, 2, seq)

        lamv = jnp.stack([lam_q1[l], lam_k1[l], lam_q2[l], lam_k2[l]])
        diff_extra = (lamv, diff_subln_g[l].reshape(1, LANES))
        if _rider_fits(batch, exp_w_up):
            o_diff, w_up = _attention(proj, "diff", 0, blk, 2 * blk, batch, seq, diff_extra, lam_init,
                                      rider=("up", exp_w_up, l))
        else:
            o_diff = _attention(proj, "diff", 0, blk, 2 * blk, batch, seq, diff_extra, lam_init)
            w_up = _prep_up(exp_w_up, l)
        o_ret = _retention(proj, tables, batch, seq)
        if _rider_fits(batch, exp_w_down):
            o_fox, w_down = _attention(proj, "fox", 7 * blk, 8 * blk, 9 * blk, batch, seq, (frow,),
                                       rider=("down", exp_w_down, l))
        else:
            o_fox = _attention(proj, "fox", 7 * blk, 8 * blk, 9 * blk, batch, seq, (frow,))
            w_down = exp_w_down[l].astype(BF16)

        rw = jnp.pad(router_w[l], ((0, 0), (0, LANES - N_EXPERTS)))
        rw_hi = rw.astype(BF16)
        rw = jnp.stack([rw_hi, (rw - rw_hi.astype(F32)).astype(BF16)])
        rb = jnp.pad(router_b[l], (0, LANES - N_EXPERTS), constant_values=NEG_INF).reshape(1, LANES)
        x2, h2p, top_idx, top_w, counts = _merge(
            o_diff, o_ret, o_fox, proj, gate_b[l].reshape(N_BRANCHES, 1, d), w_branch[l].astype(BF16),
            w_out[l].astype(BF16), x2, g1, norm2_g[l].reshape(1, d), sh2, sc2, rw, rb, seq)

        pos_t, tile_e, n_used = _route(top_idx, counts, n_rows)
        xs = _dispatch(h2p, pos_t, n_rows)

        bu = exp_b_up[l]
        b_up = jnp.concatenate([bu[:, 0::2], bu[:, 1::2]], axis=-1).reshape(N_EXPERTS, 1, 2 * D_FF)
        y = _experts(tile_e, n_used, xs, w_up, b_up, w_down,
                     exp_b_down[l].reshape(N_EXPERTS, 1, d))
        pending = (_collect(y, pos_t), top_w, g2)

    ytok, top_w, g2 = pending
    x2 = _combine(ytok, top_w, x2, g2, final_g.reshape(1, d), seq)
    return x2.reshape(batch, seq, d)
```

```python
import functools
import math

import jax
import jax.numpy as jnp
from jax import lax
from jax.experimental import pallas as pl
from jax.experimental.pallas import tpu as pltpu
from jax.experimental.pallas import tpu_sc as plsc

F32 = jnp.float32
BF16 = jnp.bfloat16
I32 = jnp.int32
U32 = jnp.uint32

D_MODEL = 1024
HEAD_DIM = 64
BRANCH_WIDTH = D_MODEL // 2
N_BRANCHES = 3
FOX_HEADS = BRANCH_WIDTH // HEAD_DIM
RET_HEADS = 4
RET_HEAD_DIM = 128
ROPE_BASE = 10000.0
RET_GAMMA_EXP0 = 5.0
N_EXPERTS = 32
TOP_K = 4
D_FF = D_MODEL
SWIGLU_LIMIT = 7.0
SWIGLU_ALPHA = 1.702
NORM_EPS = 1e-5
NEG_INF = -1e30

LANES = 128
MAIN_COLS = 8192
GATE_COL0 = 10 * BRANCH_WIDTH
HALF = D_MODEL // 2
QUART = HALF // 2
SC_WINDOW = 128
SC_WORKERS = 32
VMEM_LIMIT = 48 * 1024 * 1024
VMEM_LIMIT_INPROJ = 56 * 1024 * 1024

ROW_TILE_IN = 512
COL_TILE_IN = 1024
ATTN_TILE = 256
RET_CHUNK = 256
ROW_TILE_OUT = 256
MERGE_TILE = 1024
MERGE_SUB = 256
EXPERT_TILE = 256
EXPERT_GROUP = 2
CUM_CHUNK = 256


def _cparams(sem):
    return pltpu.CompilerParams(dimension_semantics=sem, vmem_limit_bytes=VMEM_LIMIT)


def _sigmoid(x):
    return 0.5 * jnp.tanh(0.5 * x) + 0.5


def _pack_pairs(v):
    u = lax.bitcast_convert_type(v.astype(BF16).astype(F32), U32)
    half = v.shape[-1] // 2
    w = (u[:, half:] & jnp.uint32(0xFFFF0000)) | (u[:, :half] >> 16)
    return lax.bitcast_convert_type(w, I32)


def _unpack_pairs(w):
    u = lax.bitcast_convert_type(w, U32)
    lo = lax.bitcast_convert_type(u << 16, F32)
    hi = lax.bitcast_convert_type(u & jnp.uint32(0xFFFF0000), F32)
    return lo, hi


def _moe_residual(y_ref, tw_ref, x_ref, g2_ref):
    tw = tw_ref[...]
    parts = [None] * 4
    for k in range(TOP_K):
        w = tw[:, k:k + 1]
        for piece in range(2):
            lo, hi = _unpack_pairs(y_ref[piece, k])
            for slot, v in ((piece, lo), (2 + piece, hi)):
                parts[slot] = w * v if parts[slot] is None else parts[slot] + w * v
    return x_ref[...] + g2_ref[0] * jnp.concatenate(parts, axis=-1)


def _ada_kernel(c_ref, w_ref, b_ref, o_ref):
    c = c_ref[...]
    ca = c * _sigmoid(c)
    o_ref[0] = jnp.dot(ca, w_ref[0], preferred_element_type=F32) + b_ref[0]


def _ada_mod(c, ada_w, ada_b):
    depth, d, cols = ada_w.shape
    b = c.shape[0]
    tn = 1024
    return pl.pallas_call(
        _ada_kernel,
        grid=(depth, cols // tn),
        in_specs=[
            pl.BlockSpec((b, d), lambda l, j: (0, 0)),
            pl.BlockSpec((1, d, tn), lambda l, j: (l, 0, j)),
            pl.BlockSpec((1, 1, tn), lambda l, j: (l, 0, j)),
        ],
        out_specs=pl.BlockSpec((1, b, tn), lambda l, j: (l, 0, j)),
        out_shape=jax.ShapeDtypeStruct((depth, b, cols), F32),
        compiler_params=_cparams(("arbitrary", "arbitrary")),
        name="ada_mod",
    )(c, ada_w, ada_b.reshape(depth, 1, cols))


def _prep_win_kernel(w_ref, wm_ref, wf_ref):
    wm_ref[:GATE_COL0, :] = w_ref[0, :GATE_COL0, :].astype(BF16)
    wm_ref[GATE_COL0:, :] = w_ref[0, GATE_COL0 + FOX_HEADS:, :].astype(BF16)
    wf_ref[...] = jnp.zeros(wf_ref.shape, BF16)
    wf_ref[:FOX_HEADS, :] = w_ref[0, GATE_COL0:GATE_COL0 + FOX_HEADS, :].astype(BF16)


def _prep_win(w_t, layer):
    _, cols, d = w_t.shape
    ct = 256
    return pl.pallas_call(
        _prep_win_kernel,
        grid=(d // ct,),
        in_specs=[pl.BlockSpec((1, cols, ct), lambda i: (layer, 0, i))],
        out_specs=[pl.BlockSpec((MAIN_COLS, ct), lambda i: (0, i)), pl.BlockSpec((LANES, ct), lambda i: (0, i))],
        out_shape=[jax.ShapeDtypeStruct((MAIN_COLS, d), BF16), jax.ShapeDtypeStruct((LANES, d), BF16)],
        compiler_params=_cparams(("arbitrary",)),
        name="inproj_weight_prep",
    )(w_t)


def _inproj_kernel(*refs, moe):
    if moe:
        y_ref, tw_ref, x_ref, g2_ref, g_ref, sh_ref, sc_ref, w_ref, wf_ref, proj_ref, fl_ref, xo_ref = refs
        x = _moe_residual(y_ref, tw_ref, x_ref, g2_ref)
        xo_ref[...] = x
    else:
        x_ref, g_ref, sh_ref, sc_ref, w_ref, wf_ref, proj_ref, fl_ref = refs
        x = x_ref[...]
    r = lax.rsqrt(jnp.mean(x * x, axis=-1, keepdims=True) + NORM_EPS)
    h = x * r * g_ref[...]
    h = h * (1.0 + sc_ref[0]) + sh_ref[0]
    hb = h.astype(BF16)
    nt = (((1,), (1,)), ((), ()))
    fl_ref[...] = lax.dot_general(hb, wf_ref[...], nt, preferred_element_type=F32)
    for j in range(MAIN_COLS // COL_TILE_IN):
        cs = slice(j * COL_TILE_IN, (j + 1) * COL_TILE_IN)
        proj_ref[:, cs] = lax.dot_general(hb, w_ref[cs, :], nt, preferred_element_type=F32).astype(BF16)


def _inproj(x2, g, sh, sc, w_main, w_f, seq, moe=None):
    n, d = x2.shape
    tm = min(ROW_TILE_IN, seq)
    per_seq = seq // tm
    resident = pl.Buffered(1)

    def per_b():
        return pl.BlockSpec((1, 1, d), lambda i: (i // per_seq, 0, 0))

    in_specs = [
        pl.BlockSpec((tm, d), lambda i: (i, 0)),
        pl.BlockSpec((1, d), lambda i: (0, 0)),
        per_b(), per_b(),
        pl.BlockSpec((MAIN_COLS, d), lambda i: (0, 0), pipeline_mode=resident),
        pl.BlockSpec((LANES, d), lambda i: (0, 0), pipeline_mode=resident),
    ]
    out_specs = [pl.BlockSpec((tm, MAIN_COLS), lambda i: (i, 0)), pl.BlockSpec((tm, LANES), lambda i: (i, 0))]
    out_shape = [jax.ShapeDtypeStruct((n, MAIN_COLS), BF16), jax.ShapeDtypeStruct((n, LANES), F32)]
    args = [x2, g, sh, sc, w_main, w_f]
    if moe is not None:
        ytok, top_w, g2 = moe
        in_specs = [pl.BlockSpec((2, TOP_K, tm, QUART), lambda i: (0, 0, i, 0)),
                    pl.BlockSpec((tm, LANES), lambda i: (i, 0)), in_specs[0], per_b()] + in_specs[1:]
        args = [ytok, top_w, x2, g2] + args[1:]
        out_specs.append(pl.BlockSpec((tm, d), lambda i: (i, 0)))
        out_shape.append(jax.ShapeDtypeStruct((n, d), F32))
    return pl.pallas_call(
        functools.partial(_inproj_kernel, moe=moe is not None),
        grid=(n // tm,),
        in_specs=in_specs,
        out_specs=out_specs,
        out_shape=out_shape,
        compiler_params=pltpu.CompilerParams(dimension_semantics=("arbitrary",), vmem_limit_bytes=VMEM_LIMIT_INPROJ),
        name="inproj",
    )(*args)


def _split3(v):
    hi = v.astype(BF16)
    r1 = v - hi.astype(F32)
    mid = r1.astype(BF16)
    lo = (r1 - mid.astype(F32)).astype(BF16)
    return hi, mid, lo


def _fcum_kernel(fl_ref, fb_ref, o_ref, *, seq):
    c = min(CUM_CHUNK, seq)
    row = lax.broadcasted_iota(I32, (c, c), 0)
    col = lax.broadcasted_iota(I32, (c, c), 1)
    lower = (col <= row).astype(BF16)
    carry = jnp.zeros((1, LANES), F32)
    for ci in range(seq // c):
        x = fl_ref[pl.ds(ci * c, c), :] + fb_ref[...]
        lf = jnp.minimum(x, 0.0) - jnp.log(1.0 + jnp.exp(-jnp.abs(x)))
        hi, mid, lo = _split3(lf)
        cs = (jnp.dot(lower, hi, preferred_element_type=F32)
              + jnp.dot(lower, mid, preferred_element_type=F32)
              + jnp.dot(lower, lo, preferred_element_type=F32)) + carry
        o_ref[0, :, pl.ds(ci * c, c)] = cs.T[:FOX_HEADS, :]
        carry = cs[c - 1:c, :]


def _fcum(flog, fb_pad, batch, seq):
    return pl.pallas_call(
        functools.partial(_fcum_kernel, seq=seq),
        grid=(batch,),
        in_specs=[
            pl.BlockSpec((seq, LANES), lambda b: (b, 0)),
            pl.BlockSpec((1, LANES), lambda b: (0, 0)),
        ],
        out_specs=pl.BlockSpec((1, FOX_HEADS, seq), lambda b: (b, 0, 0)),
        out_shape=jax.ShapeDtypeStruct((batch, FOX_HEADS, seq), F32),
        compiler_params=_cparams(("arbitrary",)),
        name="fox_decay_cumsum",
    )(flog, fb_pad)


def _attn_kernel(*refs, mode, t, seq, lam_init, rider):
    if rider:
        refs = list(refs)
        wt_scr = refs.pop() if rider == "up" else None
        n_in = 5 if mode == "diff" else 4
        wraw_ref = refs.pop(n_in)
        wprep_ref = refs.pop(n_in + 1)
    if mode == "diff":
        q_ref, k_ref, v_ref, lam_ref, g_ref, o_ref, vx_scr = refs
        f_ref = None
        lv = lam_ref[...]
        lam = (jnp.exp(jnp.sum(lv[0:1, :] * lv[1:2, :])) - jnp.exp(jnp.sum(lv[2:3, :] * lv[3:4, :]))
               + lam_init)
    else:
        q_ref, k_ref, v_ref, f_ref, o_ref, vx_scr = refs
    lane = lax.broadcasted_iota(I32, (t, LANES), 1)
    row = lax.broadcasted_iota(I32, (t, t), 0)
    col = lax.broadcasted_iota(I32, (t, t), 1)
    nt = (((1,), (1,)), ((), ()))

    v_all = v_ref[...]
    one = jnp.ones_like(v_all)
    if mode == "diff":
        vx_scr[0] = jnp.concatenate([v_all, one], axis=1)
    else:
        lane_s = lax.broadcasted_iota(I32, v_all.shape, 1)
        vx_scr[0] = jnp.where(lane_s < HEAD_DIM, v_all, one)
        vx_scr[1] = jnp.where(lane_s >= HEAD_DIM, v_all, one)

    n_strips = seq // t
    pairs = [(i, n_strips - 1 - i) for i in range(n_strips // 2)] if n_strips % 2 == 0 else [(i,) for i in range(n_strips)]
    for gi, group in enumerate(pairs):
        if rider == "up":
            for g in range(gi, wt_scr.shape[0], len(pairs)):
                _prep_up_group(wraw_ref, wprep_ref, wt_scr, g)
        elif rider == "down":
            span = wprep_ref.shape[1] // len(pairs)
            rows = slice(gi * span, (gi + 1) * span)
            wprep_ref[0, rows, :] = wraw_ref[0, 0, rows, :].astype(BF16)
        units = [(qi, idx) for qi in group for idx in range(2)]
        qms = {}
        for qi in group:
            q = q_ref[qi * t:(qi + 1) * t, :] * jnp.asarray(HEAD_DIM ** -0.5, BF16)
            zero = jnp.zeros_like(q)
            qms[qi, 0] = jnp.where(lane < HEAD_DIM, q, zero)
            qms[qi, 1] = jnp.where(lane >= HEAD_DIM, q, zero)

        def scores(u, k0, k1):
            s_ = lax.dot_general(qms[u], k_ref[k0:k1, :], nt, preferred_element_type=F32)
            if f_ref is not None:
                s_ = s_ - f_ref[0, u[1]:u[1] + 1, k0:k1]
            return s_

        sds = {u: jnp.where(col <= row, scores(u, u[0] * t, (u[0] + 1) * t), NEG_INF) for u in units}
        sms = {u: scores(u, 0, u[0] * t) for u in units if u[0] > 0}
        ms = {}
        for u in units:
            m = jnp.max(sds[u], axis=-1, keepdims=True)
            if u in sms:
                m = jnp.maximum(m, jnp.max(sms[u], axis=-1, keepdims=True))
            ms[u] = m
        pds = {u: jnp.exp((sds[u] - ms[u]).astype(BF16)) for u in units}
        pms = {u: jnp.exp((sms[u] - ms[u]).astype(BF16)) for u in sms}
        outs = {}
        for u in units:
            vx = vx_scr.at[0 if mode == "diff" else u[1]]
            lo = u[0] * t
            acc = jnp.dot(pds[u], vx[lo:lo + t, :], preferred_element_type=F32)
            if u in pms:
                acc = acc + jnp.dot(pms[u], vx[0:lo, :], preferred_element_type=F32)
            outs[u] = acc
        for qi in group:
            o0, o1 = outs[qi, 0], outs[qi, 1]
            if mode == "diff":
                a1 = o0[:, :LANES] / o0[:, LANES:]
                a2 = o1[:, :LANES] / o1[:, LANES:]
                o = a1 - lam * a2
                r = lax.rsqrt(jnp.mean(o * o, axis=-1, keepdims=True) + NORM_EPS)
                o = o * r * g_ref[...] * (1.0 - lam_init)
            else:
                num = jnp.where(lane < HEAD_DIM, o0, o1)
                den = jnp.where(lane < HEAD_DIM, pltpu.roll(o0, HEAD_DIM, 1), pltpu.roll(o1, HEAD_DIM, 1))
                o = num / den
            o_ref[qi * t:(qi + 1) * t, :] = o.astype(BF16)


def _attention(proj, mode, col_q, col_k, col_v, batch, seq, extra, lam_init=0.0, rider=None):
    n = proj.shape[0]
    t = min(ATTN_TILE, seq)
    n_blk = BRANCH_WIDTH // LANES
    in_specs = [
        pl.BlockSpec((seq, LANES), lambda b, h: (b, col_q + h)),
        pl.BlockSpec((seq, LANES), lambda b, h: (b, col_k + h)),
        pl.BlockSpec((seq, LANES), lambda b, h: (b, col_v + h)),
    ]
    if mode == "diff":
        lamv, g = extra
        in_specs += [pl.BlockSpec((4, HEAD_DIM), lambda b, h: (0, 0)),
                     pl.BlockSpec((1, LANES), lambda b, h: (0, 0))]
        args = (proj, proj, proj, lamv, g)
    else:
        (frow,) = extra
        in_specs += [pl.BlockSpec((1, 2, seq), lambda b, h: (b * n_blk + h, 0, 0))]
        args = (proj, proj, proj, frow)
    out_specs = [pl.BlockSpec((seq, LANES), lambda b, h: (b, h))]
    out_shape = [jax.ShapeDtypeStruct((n, BRANCH_WIDTH), BF16)]
    scratch = [pltpu.VMEM((1, seq, 2 * LANES) if mode == "diff" else (2, seq, LANES), BF16)]
    if rider is not None:
        kind, w, layer = rider
        _, e, rows, cols = w.shape
        parts = batch * n_blk // e
        kt = rows // parts
        in_specs.append(pl.BlockSpec((1, 1, kt, cols), lambda b, h: (layer, (b * n_blk + h) // parts, (b * n_blk + h) % parts, 0)))
        args = args + (w,)
        if kind == "up":
            out_specs.append(pl.BlockSpec((1, cols, kt), lambda b, h: ((b * n_blk + h) // parts, 0, (b * n_blk + h) % parts)))
            out_shape.append(jax.ShapeDtypeStruct((e, cols, rows), BF16))
            scratch.append(pltpu.VMEM((kt // LANES, cols, LANES), F32))
        else:
            out_specs.append(pl.BlockSpec((1, kt, cols), lambda b, h: ((b * n_blk + h) // parts, (b * n_blk + h) % parts, 0)))
            out_shape.append(jax.ShapeDtypeStruct((e, rows, cols), BF16))
    outs = pl.pallas_call(
        functools.partial(_attn_kernel, mode=mode, t=t, seq=seq, lam_init=lam_init, rider=rider[0] if rider else None),
        grid=(batch, n_blk),
        in_specs=in_specs,
        out_specs=out_specs,
        out_shape=out_shape,
        scratch_shapes=scratch,
        compiler_params=_cparams(("arbitrary", "arbitrary")),
        name="attn_" + mode,
    )(*args)
    return outs if rider is not None else outs[0]


def _rider_fits(batch, w):
    _, e, rows, _ = w.shape
    steps = batch * (BRANCH_WIDTH // LANES)
    return steps % e == 0 and (rows // LANES) % (steps // e) == 0


def _ret_kernel(q_ref, k_ref, v_ref, g_ref, cos_ref, sin_ref, dm_ref, qd_ref, kd_ref, cd_ref,
                o_ref, *, seq, c):
    st = jnp.zeros((RET_HEAD_DIM, RET_HEAD_DIM), F32)
    half = RET_HEAD_DIM // 2

    for ci in range(seq // c):
        sl = pl.ds(ci * c, c)
        cos = cos_ref[sl, :]
        sin = sin_ref[sl, :]
        q = q_ref[sl, :].astype(F32)
        k = k_ref[sl, :].astype(F32)
        qr = q * cos + pltpu.roll(q, half, 1) * sin
        kr = (k * cos + pltpu.roll(k, half, 1) * sin) * (RET_HEAD_DIM ** -0.5)
        v = v_ref[sl, :]
        sc = lax.dot_general(qr.astype(BF16), kr.astype(BF16), (((1,), (1,)), ((), ())),
                             preferred_element_type=F32) * dm_ref[0]
        inner = jnp.dot(sc.astype(BF16), v, preferred_element_type=F32)
        cross = jnp.dot((qr * qd_ref[0]).astype(BF16), st.astype(BF16), preferred_element_type=F32)
        o = inner + cross
        kt = (kr * kd_ref[0]).T.astype(BF16)
        st = st * cd_ref[0] + jnp.dot(kt, v, preferred_element_type=F32)
        mu = jnp.mean(o, axis=-1, keepdims=True)
        oc = o - mu
        var = jnp.mean(oc * oc, axis=-1, keepdims=True)
        on = oc * lax.rsqrt(var + NORM_EPS)
        g = g_ref[sl, :].astype(F32)
        o_ref[sl, :] = (on * (g * _sigmoid(g))).astype(BF16)


def _retention(proj, tables, batch, seq):
    n = proj.shape[0]
    c = min(RET_CHUNK, seq)
    cos, sin, dm, qd, kd, cd = tables
    col = BRANCH_WIDTH * 3 // LANES
    step = BRANCH_WIDTH // LANES

    def pspec(k):
        return pl.BlockSpec((seq, LANES), lambda b, h: (b, col + k * step + h))

    return pl.pallas_call(
        functools.partial(_ret_kernel, seq=seq, c=c),
        grid=(batch, RET_HEADS),
        in_specs=[
            pspec(0), pspec(1), pspec(2), pspec(3),
            pl.BlockSpec((seq, LANES), lambda b, h: (0, 0)),
            pl.BlockSpec((seq, LANES), lambda b, h: (0, 0)),
            pl.BlockSpec((1, c, c), lambda b, h: (h, 0, 0)),
            pl.BlockSpec((1, c, LANES), lambda b, h: (h, 0, 0)),
            pl.BlockSpec((1, c, LANES), lambda b, h: (h, 0, 0)),
            pl.BlockSpec((1, LANES, LANES), lambda b, h: (h, 0, 0)),
        ],
        out_specs=pl.BlockSpec((seq, LANES), lambda b, h: (b, h)),
        out_shape=jax.ShapeDtypeStruct((n, BRANCH_WIDTH), BF16),
        compiler_params=_cparams(("arbitrary", "arbitrary")),
        name="retention",
    )(proj, proj, proj, proj, cos, sin, dm, qd, kd, cd)


def _retention_tables(seq):
    c = min(RET_CHUNK, seq)
    d = RET_HEAD_DIM
    inv = ROPE_BASE ** (-jnp.arange(0, d, 2, dtype=F32) / d)
    ang = jnp.arange(seq).astype(F32)[:, None] * inv[None, :]
    cos = jnp.cos(ang)
    sin = jnp.sin(ang)
    cos_full = jnp.concatenate([cos, cos], axis=-1)
    sin_signed = jnp.concatenate([-sin, sin], axis=-1)
    log_gamma = jnp.log1p(-jnp.exp2(-RET_GAMMA_EXP0 - jnp.arange(RET_HEADS, dtype=F32)))
    idx = jnp.arange(c, dtype=F32)
    dist = idx[:, None] - idx[None, :]
    dm = jnp.where(dist >= 0, jnp.exp(log_gamma[:, None, None] * jnp.maximum(dist, 0.0)), 0.0)
    qd = jnp.exp(log_gamma[:, None] * (idx + 1.0))
    kd = jnp.exp(log_gamma[:, None] * (c - 1.0 - idx))
    cd = jnp.exp(log_gamma * c)
    qd = jnp.broadcast_to(qd[:, :, None], (RET_HEADS, c, LANES))
    kd = jnp.broadcast_to(kd[:, :, None], (RET_HEADS, c, LANES))
    cd = jnp.broadcast_to(cd[:, None, None], (RET_HEADS, LANES, LANES))
    return cos_full, sin_signed, dm, qd, kd, cd


def _merge_kernel(od_ref, or_ref, of_ref, gd_ref, gr_ref, gf_ref, gb_ref, wb_ref, wo_ref, x_ref,
                  g1_ref, n2_ref, sh_ref, sc_ref, rw_ref, rb_ref,
                  xo_ref, hp_ref, ti_ref, tw_ref, cnt_ref, cnt_scr, *, tm, sub):
    @pl.when(pl.program_id(0) == 0)
    def _():
        cnt_scr[...] = jnp.zeros(cnt_scr.shape, F32)

    lane = lax.broadcasted_iota(I32, (sub, LANES), 1)
    r_i = lax.broadcasted_iota(I32, (sub, sub), 0)
    c_i = lax.broadcasted_iota(I32, (sub, sub), 1)
    earlier = (c_i < r_i).astype(BF16)
    counts = cnt_scr[...]

    blocks = [slice(s0, s0 + sub) for s0 in range(0, tm, sub)]

    def phase_branches(rs):
        merged = None
        for i, (o_ref, gate_ref) in enumerate(((od_ref, gd_ref), (or_ref, gr_ref), (of_ref, gf_ref))):
            gate = _sigmoid(gate_ref[rs, :] + gb_ref[i].astype(BF16))
            term = gate.astype(F32) * jnp.dot(o_ref[rs, :], wb_ref[i], preferred_element_type=F32)
            merged = term if merged is None else merged + term
        return merged.astype(BF16)

    def phase_outproj(rs, merged):
        mix = jnp.dot(merged, wo_ref[...], preferred_element_type=F32)
        xn = x_ref[rs, :] + g1_ref[0] * mix
        xo_ref[rs, :] = xn
        r = lax.rsqrt(jnp.mean(xn * xn, axis=-1, keepdims=True) + NORM_EPS)
        h = xn * r * n2_ref[...]
        h = h * (1.0 + sc_ref[0]) + sh_ref[0]
        hp = _pack_pairs(h)
        hp_ref[0, rs, :] = hp[:, :QUART]
        hp_ref[1, rs, :] = hp[:, QUART:]
        hh, hm, _ = _split3(h)
        return (jnp.dot(hh, rw_ref[0], preferred_element_type=F32)
                + jnp.dot(hh, rw_ref[1], preferred_element_type=F32)
                + jnp.dot(hm, rw_ref[0], preferred_element_type=F32)) + rb_ref[...]

    def phase_topk(rs, logits, counts):
        vals, idxs = [], []
        for _ in range(TOP_K):
            m = jnp.max(logits, axis=-1, keepdims=True)
            idx = jnp.min(jnp.where(logits == m, lane, LANES), axis=-1, keepdims=True)
            vals.append(m)
            idxs.append(idx)
            logits = jnp.where(lane == idx, -3.0e38, logits)
        es = [jnp.exp(v - vals[0]) for v in vals]
        den = es[0] + es[1] + es[2] + es[3]
        hits = [lane == idxs[k] for k in range(TOP_K)]
        picked = (hits[0] | hits[1] | hits[2] | hits[3])
        base = jnp.dot(earlier, picked.astype(BF16), preferred_element_type=F32) + counts
        counts = counts + jnp.sum(picked.astype(F32), axis=0, keepdims=True)
        ti = jnp.zeros((sub, LANES), I32)
        tw = jnp.zeros((sub, LANES), F32)
        for k in range(TOP_K):
            rank = jnp.sum(jnp.where(hits[k], base, 0.0), axis=-1, keepdims=True).astype(I32)
            ti = jnp.where(lane == k, idxs[k], ti)
            ti = jnp.where(lane == TOP_K + k, rank, ti)
            tw = jnp.where(lane == k, es[k] / den, tw)
        ti_ref[rs, :] = ti
        tw_ref[rs, :] = tw
        return counts

    merged = [phase_branches(rs) for rs in blocks]
    logits = [phase_outproj(rs, m) for rs, m in zip(blocks, merged)]
    for rs, lg in zip(blocks, logits):
        counts = phase_topk(rs, lg, counts)

    cnt_scr[...] = counts
    cnt_ref[...] = counts.astype(I32)


def _merge(o_diff, o_ret, o_fox, proj, gate_b, w_branch, w_out, x2, g1, n2g, sh2, sc2, rw, rb, seq):
    n, d = x2.shape
    tm = min(MERGE_TILE, seq)
    sub = min(MERGE_SUB, tm)
    per_seq = seq // tm
    gcol = GATE_COL0 // d

    def rows(w):
        return pl.BlockSpec((tm, w), lambda i: (i, 0))

    def gspec(k):
        return pl.BlockSpec((tm, d), lambda i: (i, gcol + k))

    def whole(shape):
        return pl.BlockSpec(shape, lambda i: (0,) * len(shape))

    def per_b():
        return pl.BlockSpec((1, 1, d), lambda i: (i // per_seq, 0, 0))

    return pl.pallas_call(
        functools.partial(_merge_kernel, tm=tm, sub=sub),
        grid=(n // tm,),
        in_specs=[
            rows(BRANCH_WIDTH), rows(BRANCH_WIDTH), rows(BRANCH_WIDTH),
            gspec(0), gspec(1), gspec(2),
            whole((N_BRANCHES, 1, d)), whole((N_BRANCHES, BRANCH_WIDTH, d)), whole((d, d)),
            rows(d), per_b(), whole((1, d)), per_b(), per_b(),
            whole((2, d, LANES)), whole((1, LANES)),
        ],
        out_specs=[rows(d), pl.BlockSpec((2, tm, QUART), lambda i: (0, i, 0)), rows(LANES), rows(LANES),
                   whole((1, LANES))],
        out_shape=[
            jax.ShapeDtypeStruct((n, d), F32),
            jax.ShapeDtypeStruct((2, n, QUART), I32),
            jax.ShapeDtypeStruct((n, LANES), I32),
            jax.ShapeDtypeStruct((n, LANES), F32),
            jax.ShapeDtypeStruct((1, LANES), I32),
        ],
        scratch_shapes=[pltpu.VMEM((1, LANES), F32)],
        compiler_params=_cparams(("arbitrary",)),
        name="merge_outproj_router",
    )(o_diff, o_ret, o_fox, proj, proj, proj, gate_b, w_branch, w_out, x2, g1, n2g, sh2, sc2, rw, rb)


def _prep_up_group(w_ref, o_ref, t_scr, g):
    f = t_scr.shape[1] // 2
    cols = slice(g * LANES, (g + 1) * LANES)
    t_scr[g] = w_ref[0, 0, cols, :].T
    o_ref[0, 0:f, cols] = t_scr[g, pl.ds(0, f, stride=2), :].astype(BF16)
    o_ref[0, f:, cols] = t_scr[g, pl.ds(1, f, stride=2), :].astype(BF16)


def _prep_up_kernel(w_ref, o_ref, t_scr):
    for g in range(t_scr.shape[0]):
        _prep_up_group(w_ref, o_ref, t_scr, g)


def _prep_up(w_up, layer):
    _, e, d, f2 = w_up.shape
    groups = d // LANES
    kt = groups * LANES
    return pl.pallas_call(
        _prep_up_kernel,
        grid=(e, d // kt),
        in_specs=[pl.BlockSpec((1, 1, kt, f2), lambda i, j: (layer, i, j, 0))],
        out_specs=pl.BlockSpec((1, f2, kt), lambda i, j: (i, 0, j)),
        out_shape=jax.ShapeDtypeStruct((e, f2, d), BF16),
        scratch_shapes=[pltpu.VMEM((groups, f2, LANES), F32)],
        compiler_params=_cparams(("arbitrary", "arbitrary")),
        name="expert_up_weight_prep",
    )(w_up)


def _expert_kernel(te_ref, nu_ref, xs_ref, *refs):
    y_ref = refs[-1]
    nt = (((1,), (1,)), ((), ()))
    first_tile = pl.program_id(0) * EXPERT_GROUP

    @pl.when(first_tile < nu_ref[0])
    def _():
        for s in range(EXPERT_GROUP):
            wu_ref, bu_ref, wd_ref, bd_ref = refs[4 * s:4 * s + 4]
            rs = slice(s * EXPERT_TILE, (s + 1) * EXPERT_TILE)
            lo, hi = _unpack_pairs(jnp.concatenate([xs_ref[0, rs, :], xs_ref[1, rs, :]], axis=1))
            lo = lo.astype(BF16)
            hi = hi.astype(BF16)

            def up_half(r0):
                return (lax.dot_general(lo, wu_ref[0, r0:r0 + D_FF, :HALF], nt, preferred_element_type=F32)
                        + lax.dot_general(hi, wu_ref[0, r0:r0 + D_FF, HALF:], nt, preferred_element_type=F32))

            glu = jnp.minimum(up_half(0) + bu_ref[0, :, :D_FF], SWIGLU_LIMIT)
            lin = jnp.clip(up_half(D_FF) + bu_ref[0, :, D_FF:], -SWIGLU_LIMIT, SWIGLU_LIMIT)
            act = glu * _sigmoid(SWIGLU_ALPHA * glu) * (lin + 1.0)
            down = jnp.dot(act.astype(BF16), wd_ref[0], preferred_element_type=F32) + bd_ref[0]
            yp = _pack_pairs(down)
            y_ref[0, rs, :] = yp[:, :QUART]
            y_ref[1, rs, :] = yp[:, QUART:]

    @pl.when(first_tile >= nu_ref[0])
    def _():
        y_ref[...] = jnp.zeros(y_ref.shape, I32)


def _experts(tile_e, n_used, xs, w_up, b_up, w_down, b_down):
    r = xs.shape[1]
    tm = EXPERT_TILE * EXPERT_GROUP
    d = D_MODEL
    in_specs = [pl.BlockSpec((2, tm, QUART), lambda i, te, nu: (0, i, 0))]
    args = [xs]
    for s in range(EXPERT_GROUP):
        def pick(i, te, nu, s=s):
            return (te[i * EXPERT_GROUP + s], 0, 0)
        in_specs += [pl.BlockSpec((1, 2 * D_FF, d), pick), pl.BlockSpec((1, 1, 2 * D_FF), pick),
                     pl.BlockSpec((1, D_FF, d), pick), pl.BlockSpec((1, 1, d), pick)]
        args += [w_up, b_up, w_down, b_down]
    grid_spec = pltpu.PrefetchScalarGridSpec(
        num_scalar_prefetch=2,
        grid=(r // tm,),
        in_specs=in_specs,
        out_specs=pl.BlockSpec((2, tm, QUART), lambda i, te, nu: (0, i, 0)),
    )
    return pl.pallas_call(
        _expert_kernel,
        grid_spec=grid_spec,
        out_shape=jax.ShapeDtypeStruct((2, r, QUART), I32),
        compiler_params=_cparams(("arbitrary",)),
        name="expert_mlp",
    )(tile_e, n_used, *args)


def _combine_kernel(y_ref, tw_ref, x_ref, g2_ref, fg_ref, o_ref):
    xn = _moe_residual(y_ref, tw_ref, x_ref, g2_ref)
    r = lax.rsqrt(jnp.mean(xn * xn, axis=-1, keepdims=True) + NORM_EPS)
    o_ref[...] = xn * r * fg_ref[...]


def _combine(ytok, top_w, x2, g2, final_g, seq):
    n, d = x2.shape
    tm = min(ROW_TILE_OUT, seq)
    per_seq = seq // tm
    return pl.pallas_call(
        _combine_kernel,
        grid=(n // tm,),
        in_specs=[
            pl.BlockSpec((2, TOP_K, tm, QUART), lambda i: (0, 0, i, 0)),
            pl.BlockSpec((tm, LANES), lambda i: (i, 0)),
            pl.BlockSpec((tm, d), lambda i: (i, 0)),
            pl.BlockSpec((1, 1, d), lambda i: (i // per_seq, 0, 0)),
            pl.BlockSpec((1, d), lambda i: (0, 0)),
        ],
        out_specs=pl.BlockSpec((tm, d), lambda i: (i, 0)),
        out_shape=jax.ShapeDtypeStruct((n, d), F32),
        compiler_params=_cparams(("arbitrary",)),
        name="moe_combine",
    )(ytok, top_w, x2, g2, final_g)


def _route(top_idx, counts, n_rows):
    tm = EXPERT_TILE
    e = top_idx[:, :TOP_K]
    rank = top_idx[:, TOP_K:2 * TOP_K]
    sizes = counts[0, :N_EXPERTS]
    padded = ((sizes + tm - 1) // tm) * tm
    pend = jnp.cumsum(padded)
    pstart = pend - padded
    onehot = e[:, :, None] == jnp.arange(N_EXPERTS, dtype=I32)[None, None, :]
    pos = jnp.sum(jnp.where(onehot, pstart[None, None, :], 0), axis=-1) + rank
    tiles = jnp.arange(n_rows // tm, dtype=I32) * tm
    tile_e = jnp.minimum(jnp.sum(tiles[:, None] >= pend[None, :], axis=-1), N_EXPERTS - 1).astype(I32)
    n_used = (pend[-1] // tm).astype(I32).reshape(1)
    return pos.T.astype(I32), tile_e, n_used


def _sc_gather(table, idx):
    m = idx.shape[0]
    width = table.shape[1]
    assert m % (SC_WINDOW * SC_WORKERS) == 0
    mesh = plsc.VectorSubcoreMesh(core_axis_name="c", subcore_axis_name="s")

    @pl.kernel(out_type=jax.ShapeDtypeStruct((m, width), table.dtype), mesh=mesh)
    def gather_rows(t_hbm, i_hbm, o_hbm):
        def body(i_vmem, o_vmem):
            pltpu.sync_copy(t_hbm.at[i_vmem.at[0]], o_vmem)

        pltpu.emit_pipeline(
            body,
            grid=(m // SC_WINDOW,),
            in_specs=[pl.BlockSpec((1, SC_WINDOW), lambda i: (0, i))],
            out_specs=[pl.BlockSpec((SC_WINDOW, width), lambda i: (i, 0))],
            core_axis_name=("c", "s"),
            dimension_semantics=(pltpu.PARALLEL,),
        )(i_hbm, o_hbm)

    return gather_rows(table, idx.reshape(1, m))


def _sc_scatter(rows, idx, n_out):
    t, width = rows.shape
    picks = idx.shape[0]
    assert t % SC_WINDOW == 0
    mesh = plsc.VectorSubcoreMesh(core_axis_name="c", subcore_axis_name="s")

    @pl.kernel(out_type=jax.ShapeDtypeStruct((n_out, width), rows.dtype), mesh=mesh)
    def scatter_rows(x_hbm, i_hbm, o_hbm):
        def body(x_vmem, i_vmem):
            for k in range(picks):
                pltpu.sync_copy(x_vmem, o_hbm.at[i_vmem.at[k]])

        pltpu.emit_pipeline(
            body,
            grid=(t // SC_WINDOW,),
            in_specs=[pl.BlockSpec((SC_WINDOW, width), lambda i: (i, 0)),
                      pl.BlockSpec((picks, SC_WINDOW), lambda i: (0, i))],
            out_specs=[],
            core_axis_name=("c", "s"),
            dimension_semantics=(pltpu.PARALLEL,),
        )(x_hbm, i_hbm)

    return scatter_rows(rows, idx)


def _dispatch(h2p, pos_t, n_rows):
    n = h2p.shape[1]
    idx = jnp.concatenate([pos_t, pos_t + n_rows], axis=1)
    return _sc_scatter(h2p.reshape(2 * n, QUART), idx, 2 * n_rows).reshape(2, n_rows, QUART)


def _collect(y, pos_t):
    n_rows = y.shape[1]
    flat = pos_t.reshape(-1)
    out = _sc_gather(y.reshape(2 * n_rows, QUART), jnp.concatenate([flat, flat + n_rows]))
    return out.reshape(2, TOP_K, pos_t.shape[1], QUART)


def kernel(x, c, norm1_g, norm2_g, ada_w, ada_b, w_in, gate_b, fox_fb, lam_q1, lam_k1, lam_q2, lam_k2,
           diff_subln_g, w_branch, w_out, router_w, router_b, exp_w_up, exp_b_up, exp_w_down,
           exp_b_down, final_g):
    batch, seq, d = x.shape
    depth = ada_w.shape[0]
    chains = 2 if batch % 2 == 0 else 1
    bc = batch // chains
    n = bc * seq
    xc = x.reshape(chains, n, d)

    mod = _ada_mod(c, ada_w, ada_b)
    tables = _retention_tables(seq)
    n_rows = n * TOP_K + N_EXPERTS * EXPERT_TILE
    blk = BRANCH_WIDTH // LANES

    w_in_t = jnp.swapaxes(w_in, 1, 2)
    x2s = [xc[h] for h in range(chains)]
    pendings = [None] * chains
    for l in range(depth):
        lam_init = 0.8 - 0.6 * math.exp(-0.3 * l)
        mods = [m.reshape(batch, 1, d) for m in jnp.split(mod[l], 6, axis=-1)]
        w_main, w_f = _prep_win(w_in_t, l)
        fb_pad = jnp.pad(fox_fb[l], (0, LANES - FOX_HEADS)).reshape(1, LANES)
        lamv = jnp.stack([lam_q1[l], lam_k1[l], lam_q2[l], lam_k2[l]])
        diff_extra = (lamv, diff_subln_g[l].reshape(1, LANES))
        rw = jnp.pad(router_w[l], ((0, 0), (0, LANES - N_EXPERTS)))
        rw_hi = rw.astype(BF16)
        rw = jnp.stack([rw_hi, (rw - rw_hi.astype(F32)).astype(BF16)])
        rb = jnp.pad(router_b[l], (0, LANES - N_EXPERTS), constant_values=NEG_INF).reshape(1, LANES)
        bu = exp_b_up[l]
        b_up = jnp.concatenate([bu[:, 0::2], bu[:, 1::2]], axis=-1).reshape(N_EXPERTS, 1, 2 * D_FF)
        w_up = w_down = None

        for h in range(chains):
            sh1, sc1, g1, sh2, sc2, g2 = [m[h * bc:(h + 1) * bc] for m in mods]
            outs = _inproj(x2s[h], norm1_g[l].reshape(1, d), sh1, sc1, w_main, w_f, seq, pendings[h])
            proj, flog = outs[0], outs[1]
            x2 = outs[2] if pendings[h] is not None else x2s[h]

            frow = _fcum(flog, fb_pad, bc, seq).reshape(bc * blk, 2, seq)
            if w_up is None and _rider_fits(bc, exp_w_up):
                o_diff, w_up = _attention(proj, "diff", 0, blk, 2 * blk, bc, seq, diff_extra, lam_init,
                                          rider=("up", exp_w_up, l))
            else:
                o_diff = _attention(proj, "diff", 0, blk, 2 * blk, bc, seq, diff_extra, lam_init)
                w_up = _prep_up(exp_w_up, l) if w_up is None else w_up
            o_ret = _retention(proj, tables, bc, seq)
            if w_down is None and _rider_fits(bc, exp_w_down):
                o_fox, w_down = _attention(proj, "fox", 7 * blk, 8 * blk, 9 * blk, bc, seq, (frow,),
                                           rider=("down", exp_w_down, l))
            else:
                o_fox = _attention(proj, "fox", 7 * blk, 8 * blk, 9 * blk, bc, seq, (frow,))
                w_down = exp_w_down[l].astype(BF16) if w_down is None else w_down

            x2, h2p, top_idx, top_w, counts = _merge(
                o_diff, o_ret, o_fox, proj, gate_b[l].reshape(N_BRANCHES, 1, d), w_branch[l].astype(BF16),
                w_out[l].astype(BF16), x2, g1, norm2_g[l].reshape(1, d), sh2, sc2, rw, rb, seq)

            pos_t, tile_e, n_used = _route(top_idx, counts, n_rows)
            xs = _dispatch(h2p, pos_t, n_rows)
            y = _experts(tile_e, n_used, xs, w_up, b_up, w_down, exp_b_down[l].reshape(N_EXPERTS, 1, d))
            x2s[h] = x2
            pendings[h] = (_collect(y, pos_t), top_w, g2)

    outs = []
    for h in range(chains):
        ytok, top_w, g2 = pendings[h]
        outs.append(_combine(ytok, top_w, x2s[h], g2, final_g.reshape(1, d), seq))
    return jnp.concatenate(outs, axis=0).reshape(batch, seq, d)
```

```python
import functools
import math

import jax
import jax.numpy as jnp
from jax import lax
from jax.experimental import pallas as pl
from jax.experimental.pallas import tpu as pltpu
from jax.experimental.pallas import tpu_sc as plsc

F32 = jnp.float32
BF16 = jnp.bfloat16
I32 = jnp.int32
U32 = jnp.uint32

D_MODEL = 1024
HEAD_DIM = 64
BRANCH_WIDTH = D_MODEL // 2
N_BRANCHES = 3
FOX_HEADS = BRANCH_WIDTH // HEAD_DIM
RET_HEADS = 4
RET_HEAD_DIM = 128
ROPE_BASE = 10000.0
RET_GAMMA_EXP0 = 5.0
N_EXPERTS = 32
TOP_K = 4
D_FF = D_MODEL
SWIGLU_LIMIT = 7.0
SWIGLU_ALPHA = 1.702
NORM_EPS = 1e-5
NEG_INF = -1e30

LANES = 128
MAIN_COLS = 8192
GATE_COL0 = 10 * BRANCH_WIDTH
HALF = D_MODEL // 2
QUART = HALF // 2
SC_WINDOW = 128
SC_WORKERS = 32
VMEM_LIMIT = 48 * 1024 * 1024
VMEM_LIMIT_INPROJ = 56 * 1024 * 1024

ROW_TILE_IN = 512
COL_TILE_IN = 1024
ATTN_TILE = 256
RET_CHUNK = 256
ROW_TILE_OUT = 256
MERGE_TILE = 1024
MERGE_SUB = 512
EXPERT_TILE = 256
EXPERT_GROUP = 2
CUM_CHUNK = 256


def _cparams(sem):
    return pltpu.CompilerParams(dimension_semantics=sem, vmem_limit_bytes=VMEM_LIMIT)


def _sigmoid(x):
    return 0.5 * jnp.tanh(0.5 * x) + 0.5


def _pack_pairs(v):
    u = lax.bitcast_convert_type(v.astype(BF16).astype(F32), U32)
    half = v.shape[-1] // 2
    w = (u[:, half:] & jnp.uint32(0xFFFF0000)) | (u[:, :half] >> 16)
    return lax.bitcast_convert_type(w, I32)


def _unpack_pairs(w):
    u = lax.bitcast_convert_type(w, U32)
    lo = lax.bitcast_convert_type(u << 16, F32)
    hi = lax.bitcast_convert_type(u & jnp.uint32(0xFFFF0000), F32)
    return lo, hi


def _moe_residual(y_ref, tw_ref, x_ref, g2_ref):
    tw = tw_ref[...]
    parts = [None] * 4
    for k in range(TOP_K):
        w = tw[:, k:k + 1]
        for piece in range(2):
            lo, hi = _unpack_pairs(y_ref[piece, k])
            for slot, v in ((piece, lo), (2 + piece, hi)):
                parts[slot] = w * v if parts[slot] is None else parts[slot] + w * v
    return x_ref[...] + g2_ref[0] * jnp.concatenate(parts, axis=-1)


def _ada_kernel(c_ref, w_ref, b_ref, o_ref):
    c = c_ref[...]
    ca = c * _sigmoid(c)
    o_ref[0] = jnp.dot(ca, w_ref[0], preferred_element_type=F32) + b_ref[0]


def _ada_mod(c, ada_w, ada_b):
    depth, d, cols = ada_w.shape
    b = c.shape[0]
    tn = 1024
    return pl.pallas_call(
        _ada_kernel,
        grid=(depth, cols // tn),
        in_specs=[
            pl.BlockSpec((b, d), lambda l, j: (0, 0)),
            pl.BlockSpec((1, d, tn), lambda l, j: (l, 0, j)),
            pl.BlockSpec((1, 1, tn), lambda l, j: (l, 0, j)),
        ],
        out_specs=pl.BlockSpec((1, b, tn), lambda l, j: (l, 0, j)),
        out_shape=jax.ShapeDtypeStruct((depth, b, cols), F32),
        compiler_params=_cparams(("arbitrary", "arbitrary")),
        name="ada_mod",
    )(c, ada_w, ada_b.reshape(depth, 1, cols))


def _prep_win_kernel(w_ref, wm_ref, wf_ref):
    wm_ref[:GATE_COL0, :] = w_ref[0, :GATE_COL0, :].astype(BF16)
    wm_ref[GATE_COL0:, :] = w_ref[0, GATE_COL0 + FOX_HEADS:, :].astype(BF16)
    wf_ref[...] = jnp.zeros(wf_ref.shape, BF16)
    wf_ref[:FOX_HEADS, :] = w_ref[0, GATE_COL0:GATE_COL0 + FOX_HEADS, :].astype(BF16)


def _prep_win(w_t, layer):
    _, cols, d = w_t.shape
    ct = 256
    return pl.pallas_call(
        _prep_win_kernel,
        grid=(d // ct,),
        in_specs=[pl.BlockSpec((1, cols, ct), lambda i: (layer, 0, i))],
        out_specs=[pl.BlockSpec((MAIN_COLS, ct), lambda i: (0, i)), pl.BlockSpec((LANES, ct), lambda i: (0, i))],
        out_shape=[jax.ShapeDtypeStruct((MAIN_COLS, d), BF16), jax.ShapeDtypeStruct((LANES, d), BF16)],
        compiler_params=_cparams(("arbitrary",)),
        name="inproj_weight_prep",
    )(w_t)


def _inproj_kernel(*refs, moe):
    if moe:
        y_ref, tw_ref, x_ref, g2_ref, g_ref, sh_ref, sc_ref, w_ref, wf_ref, proj_ref, fl_ref, xo_ref = refs
        x = _moe_residual(y_ref, tw_ref, x_ref, g2_ref)
        xo_ref[...] = x
    else:
        x_ref, g_ref, sh_ref, sc_ref, w_ref, wf_ref, proj_ref, fl_ref = refs
        x = x_ref[...]
    r = lax.rsqrt(jnp.mean(x * x, axis=-1, keepdims=True) + NORM_EPS)
    h = x * r * g_ref[...]
    h = h * (1.0 + sc_ref[0]) + sh_ref[0]
    hb = h.astype(BF16)
    nt = (((1,), (1,)), ((), ()))
    fl_ref[...] = lax.dot_general(hb, wf_ref[...], nt, preferred_element_type=F32)
    for j in range(MAIN_COLS // COL_TILE_IN):
        cs = slice(j * COL_TILE_IN, (j + 1) * COL_TILE_IN)
        proj_ref[:, cs] = lax.dot_general(hb, w_ref[cs, :], nt, preferred_element_type=F32).astype(BF16)


def _inproj(x2, g, sh, sc, w_main, w_f, seq, moe=None):
    n, d = x2.shape
    tm = min(ROW_TILE_IN, seq)
    per_seq = seq // tm
    resident = pl.Buffered(1)

    def per_b():
        return pl.BlockSpec((1, 1, d), lambda i: (i // per_seq, 0, 0))

    in_specs = [
        pl.BlockSpec((tm, d), lambda i: (i, 0)),
        pl.BlockSpec((1, d), lambda i: (0, 0)),
        per_b(), per_b(),
        pl.BlockSpec((MAIN_COLS, d), lambda i: (0, 0), pipeline_mode=resident),
        pl.BlockSpec((LANES, d), lambda i: (0, 0), pipeline_mode=resident),
    ]
    out_specs = [pl.BlockSpec((tm, MAIN_COLS), lambda i: (i, 0)), pl.BlockSpec((tm, LANES), lambda i: (i, 0))]
    out_shape = [jax.ShapeDtypeStruct((n, MAIN_COLS), BF16), jax.ShapeDtypeStruct((n, LANES), F32)]
    args = [x2, g, sh, sc, w_main, w_f]
    if moe is not None:
        ytok, top_w, g2 = moe
        in_specs = [pl.BlockSpec((2, TOP_K, tm, QUART), lambda i: (0, 0, i, 0)),
                    pl.BlockSpec((tm, LANES), lambda i: (i, 0)), in_specs[0], per_b()] + in_specs[1:]
        args = [ytok, top_w, x2, g2] + args[1:]
        out_specs.append(pl.BlockSpec((tm, d), lambda i: (i, 0)))
        out_shape.append(jax.ShapeDtypeStruct((n, d), F32))
    return pl.pallas_call(
        functools.partial(_inproj_kernel, moe=moe is not None),
        grid=(n // tm,),
        in_specs=in_specs,
        out_specs=out_specs,
        out_shape=out_shape,
        compiler_params=pltpu.CompilerParams(dimension_semantics=("arbitrary",), vmem_limit_bytes=VMEM_LIMIT_INPROJ),
        name="inproj",
    )(*args)


def _split3(v):
    hi = v.astype(BF16)
    r1 = v - hi.astype(F32)
    mid = r1.astype(BF16)
    lo = (r1 - mid.astype(F32)).astype(BF16)
    return hi, mid, lo


def _fcum_kernel(fl_ref, fb_ref, o_ref, *, seq):
    c = min(CUM_CHUNK, seq)
    row = lax.broadcasted_iota(I32, (c, c), 0)
    col = lax.broadcasted_iota(I32, (c, c), 1)
    lower = (col <= row).astype(BF16)
    carry = jnp.zeros((1, LANES), F32)
    for ci in range(seq // c):
        x = fl_ref[pl.ds(ci * c, c), :] + fb_ref[...]
        lf = jnp.minimum(x, 0.0) - jnp.log(1.0 + jnp.exp(-jnp.abs(x)))
        hi, mid, lo = _split3(lf)
        cs = (jnp.dot(lower, hi, preferred_element_type=F32)
              + jnp.dot(lower, mid, preferred_element_type=F32)
              + jnp.dot(lower, lo, preferred_element_type=F32)) + carry
        o_ref[0, :, pl.ds(ci * c, c)] = cs.T[:FOX_HEADS, :]
        carry = cs[c - 1:c, :]


def _fcum(flog, fb_pad, batch, seq):
    return pl.pallas_call(
        functools.partial(_fcum_kernel, seq=seq),
        grid=(batch,),
        in_specs=[
            pl.BlockSpec((seq, LANES), lambda b: (b, 0)),
            pl.BlockSpec((1, LANES), lambda b: (0, 0)),
        ],
        out_specs=pl.BlockSpec((1, FOX_HEADS, seq), lambda b: (b, 0, 0)),
        out_shape=jax.ShapeDtypeStruct((batch, FOX_HEADS, seq), F32),
        compiler_params=_cparams(("arbitrary",)),
        name="fox_decay_cumsum",
    )(flog, fb_pad)


def _attn_kernel(*refs, mode, t, seq, lam_init, rider):
    if rider:
        refs = list(refs)
        wt_scr = refs.pop() if rider == "up" else None
        n_in = 5 if mode == "diff" else 4
        wraw_ref = refs.pop(n_in)
        wprep_ref = refs.pop(n_in + 1)
    if mode == "diff":
        q_ref, k_ref, v_ref, lam_ref, g_ref, o_ref, vx_scr = refs
        f_ref = None
        lv = lam_ref[...]
        lam = (jnp.exp(jnp.sum(lv[0:1, :] * lv[1:2, :])) - jnp.exp(jnp.sum(lv[2:3, :] * lv[3:4, :]))
               + lam_init)
    else:
        q_ref, k_ref, v_ref, f_ref, o_ref, vx_scr = refs
    lane = lax.broadcasted_iota(I32, (t, LANES), 1)
    row = lax.broadcasted_iota(I32, (t, t), 0)
    col = lax.broadcasted_iota(I32, (t, t), 1)
    nt = (((1,), (1,)), ((), ()))

    v_all = v_ref[...]
    one = jnp.ones_like(v_all)
    if mode == "diff":
        vx_scr[0] = jnp.concatenate([v_all, one], axis=1)
    else:
        lane_s = lax.broadcasted_iota(I32, v_all.shape, 1)
        vx_scr[0] = jnp.where(lane_s < HEAD_DIM, v_all, one)
        vx_scr[1] = jnp.where(lane_s >= HEAD_DIM, v_all, one)

    n_strips = seq // t
    pairs = [(i, n_strips - 1 - i) for i in range(n_strips // 2)] if n_strips % 2 == 0 else [(i,) for i in range(n_strips)]
    for gi, group in enumerate(pairs):
        if rider == "up":
            for g in range(gi, wt_scr.shape[0], len(pairs)):
                _prep_up_group(wraw_ref, wprep_ref, wt_scr, g)
        elif rider == "down":
            span = wprep_ref.shape[1] // len(pairs)
            rows = slice(gi * span, (gi + 1) * span)
            wprep_ref[0, rows, :] = wraw_ref[0, 0, rows, :].astype(BF16)
        units = [(qi, idx) for qi in group for idx in range(2)]
        qms = {}
        for qi in group:
            q = q_ref[qi * t:(qi + 1) * t, :] * jnp.asarray(HEAD_DIM ** -0.5, BF16)
            zero = jnp.zeros_like(q)
            qms[qi, 0] = jnp.where(lane < HEAD_DIM, q, zero)
            qms[qi, 1] = jnp.where(lane >= HEAD_DIM, q, zero)

        def scores(u, k0, k1):
            s_ = lax.dot_general(qms[u], k_ref[k0:k1, :], nt, preferred_element_type=F32)
            if f_ref is not None:
                s_ = s_ - f_ref[0, u[1]:u[1] + 1, k0:k1]
            return s_

        sds = {u: jnp.where(col <= row, scores(u, u[0] * t, (u[0] + 1) * t), NEG_INF) for u in units}
        sms = {u: scores(u, 0, u[0] * t) for u in units if u[0] > 0}
        ms = {}
        for u in units:
            m = jnp.max(sds[u], axis=-1, keepdims=True)
            if u in sms:
                m = jnp.maximum(m, jnp.max(sms[u], axis=-1, keepdims=True))
            ms[u] = m
        pds = {u: jnp.exp((sds[u] - ms[u]).astype(BF16)) for u in units}
        pms = {u: jnp.exp((sms[u] - ms[u]).astype(BF16)) for u in sms}
        outs = {}
        for u in units:
            vx = vx_scr.at[0 if mode == "diff" else u[1]]
            lo = u[0] * t
            acc = jnp.dot(pds[u], vx[lo:lo + t, :], preferred_element_type=F32)
            if u in pms:
                acc = acc + jnp.dot(pms[u], vx[0:lo, :], preferred_element_type=F32)
            outs[u] = acc
        for qi in group:
            o0, o1 = outs[qi, 0], outs[qi, 1]
            if mode == "diff":
                a1 = o0[:, :LANES] / o0[:, LANES:]
                a2 = o1[:, :LANES] / o1[:, LANES:]
                o = a1 - lam * a2
                r = lax.rsqrt(jnp.mean(o * o, axis=-1, keepdims=True) + NORM_EPS)
                o = o * r * g_ref[...] * (1.0 - lam_init)
            else:
                num = jnp.where(lane < HEAD_DIM, o0, o1)
                den = jnp.where(lane < HEAD_DIM, pltpu.roll(o0, HEAD_DIM, 1), pltpu.roll(o1, HEAD_DIM, 1))
                o = num / den
            o_ref[qi * t:(qi + 1) * t, :] = o.astype(BF16)


def _attention(proj, mode, col_q, col_k, col_v, batch, seq, extra, lam_init=0.0, rider=None):
    n = proj.shape[0]
    t = min(ATTN_TILE, seq)
    n_blk = BRANCH_WIDTH // LANES
    in_specs = [
        pl.BlockSpec((seq, LANES), lambda b, h: (b, col_q + h)),
        pl.BlockSpec((seq, LANES), lambda b, h: (b, col_k + h)),
        pl.BlockSpec((seq, LANES), lambda b, h: (b, col_v + h)),
    ]
    if mode == "diff":
        lamv, g = extra
        in_specs += [pl.BlockSpec((4, HEAD_DIM), lambda b, h: (0, 0)),
                     pl.BlockSpec((1, LANES), lambda b, h: (0, 0))]
        args = (proj, proj, proj, lamv, g)
    else:
        (frow,) = extra
        in_specs += [pl.BlockSpec((1, 2, seq), lambda b, h: (b * n_blk + h, 0, 0))]
        args = (proj, proj, proj, frow)
    out_specs = [pl.BlockSpec((seq, LANES), lambda b, h: (b, h))]
    out_shape = [jax.ShapeDtypeStruct((n, BRANCH_WIDTH), BF16)]
    scratch = [pltpu.VMEM((1, seq, 2 * LANES) if mode == "diff" else (2, seq, LANES), BF16)]
    if rider is not None:
        kind, w, layer = rider
        _, e, rows, cols = w.shape
        parts = batch * n_blk // e
        kt = rows // parts
        in_specs.append(pl.BlockSpec((1, 1, kt, cols), lambda b, h: (layer, (b * n_blk + h) // parts, (b * n_blk + h) % parts, 0)))
        args = args + (w,)
        if kind == "up":
            out_specs.append(pl.BlockSpec((1, cols, kt), lambda b, h: ((b * n_blk + h) // parts, 0, (b * n_blk + h) % parts)))
            out_shape.append(jax.ShapeDtypeStruct((e, cols, rows), BF16))
            scratch.append(pltpu.VMEM((kt // LANES, cols, LANES), F32))
        else:
            out_specs.append(pl.BlockSpec((1, kt, cols), lambda b, h: ((b * n_blk + h) // parts, (b * n_blk + h) % parts, 0)))
            out_shape.append(jax.ShapeDtypeStruct((e, rows, cols), BF16))
    outs = pl.pallas_call(
        functools.partial(_attn_kernel, mode=mode, t=t, seq=seq, lam_init=lam_init, rider=rider[0] if rider else None),
        grid=(batch, n_blk),
        in_specs=in_specs,
        out_specs=out_specs,
        out_shape=out_shape,
        scratch_shapes=scratch,
        compiler_params=_cparams(("arbitrary", "arbitrary")),
        name="attn_" + mode,
    )(*args)
    return outs if rider is not None else outs[0]


def _rider_fits(batch, w):
    _, e, rows, _ = w.shape
    steps = batch * (BRANCH_WIDTH // LANES)
    return steps % e == 0 and (rows // LANES) % (steps // e) == 0


def _ret_kernel(q_ref, k_ref, v_ref, g_ref, cos_ref, sin_ref, dm_ref, qd_ref, kd_ref, cd_ref,
                o_ref, *, seq, c):
    st = jnp.zeros((RET_HEAD_DIM, RET_HEAD_DIM), F32)
    half = RET_HEAD_DIM // 2

    for ci in range(seq // c):
        sl = pl.ds(ci * c, c)
        cos = cos_ref[sl, :]
        sin = sin_ref[sl, :]
        q = q_ref[sl, :].astype(F32)
        k = k_ref[sl, :].astype(F32)
        qr = q * cos + pltpu.roll(q, half, 1) * sin
        kr = (k * cos + pltpu.roll(k, half, 1) * sin) * (RET_HEAD_DIM ** -0.5)
        v = v_ref[sl, :]
        sc = lax.dot_general(qr.astype(BF16), kr.astype(BF16), (((1,), (1,)), ((), ())),
                             preferred_element_type=F32) * dm_ref[0]
        inner = jnp.dot(sc.astype(BF16), v, preferred_element_type=F32)
        cross = jnp.dot((qr * qd_ref[0]).astype(BF16), st.astype(BF16), preferred_element_type=F32)
        o = inner + cross
        kt = (kr * kd_ref[0]).T.astype(BF16)
        st = st * cd_ref[0] + jnp.dot(kt, v, preferred_element_type=F32)
        mu = jnp.mean(o, axis=-1, keepdims=True)
        oc = o - mu
        var = jnp.mean(oc * oc, axis=-1, keepdims=True)
        on = oc * lax.rsqrt(var + NORM_EPS)
        g = g_ref[sl, :].astype(F32)
        o_ref[sl, :] = (on * (g * _sigmoid(g))).astype(BF16)


def _retention(proj, tables, batch, seq):
    n = proj.shape[0]
    c = min(RET_CHUNK, seq)
    cos, sin, dm, qd, kd, cd = tables
    col = BRANCH_WIDTH * 3 // LANES
    step = BRANCH_WIDTH // LANES

    def pspec(k):
        return pl.BlockSpec((seq, LANES), lambda b, h: (b, col + k * step + h))

    return pl.pallas_call(
        functools.partial(_ret_kernel, seq=seq, c=c),
        grid=(batch, RET_HEADS),
        in_specs=[
            pspec(0), pspec(1), pspec(2), pspec(3),
            pl.BlockSpec((seq, LANES), lambda b, h: (0, 0)),
            pl.BlockSpec((seq, LANES), lambda b, h: (0, 0)),
            pl.BlockSpec((1, c, c), lambda b, h: (h, 0, 0)),
            pl.BlockSpec((1, c, LANES), lambda b, h: (h, 0, 0)),
            pl.BlockSpec((1, c, LANES), lambda b, h: (h, 0, 0)),
            pl.BlockSpec((1, LANES, LANES), lambda b, h: (h, 0, 0)),
        ],
        out_specs=pl.BlockSpec((seq, LANES), lambda b, h: (b, h)),
        out_shape=jax.ShapeDtypeStruct((n, BRANCH_WIDTH), BF16),
        compiler_params=_cparams(("arbitrary", "arbitrary")),
        name="retention",
    )(proj, proj, proj, proj, cos, sin, dm, qd, kd, cd)


def _retention_tables(seq):
    c = min(RET_CHUNK, seq)
    d = RET_HEAD_DIM
    inv = ROPE_BASE ** (-jnp.arange(0, d, 2, dtype=F32) / d)
    ang = jnp.arange(seq).astype(F32)[:, None] * inv[None, :]
    cos = jnp.cos(ang)
    sin = jnp.sin(ang)
    cos_full = jnp.concatenate([cos, cos], axis=-1)
    sin_signed = jnp.concatenate([-sin, sin], axis=-1)
    log_gamma = jnp.log1p(-jnp.exp2(-RET_GAMMA_EXP0 - jnp.arange(RET_HEADS, dtype=F32)))
    idx = jnp.arange(c, dtype=F32)
    dist = idx[:, None] - idx[None, :]
    dm = jnp.where(dist >= 0, jnp.exp(log_gamma[:, None, None] * jnp.maximum(dist, 0.0)), 0.0)
    qd = jnp.exp(log_gamma[:, None] * (idx + 1.0))
    kd = jnp.exp(log_gamma[:, None] * (c - 1.0 - idx))
    cd = jnp.exp(log_gamma * c)
    qd = jnp.broadcast_to(qd[:, :, None], (RET_HEADS, c, LANES))
    kd = jnp.broadcast_to(kd[:, :, None], (RET_HEADS, c, LANES))
    cd = jnp.broadcast_to(cd[:, None, None], (RET_HEADS, LANES, LANES))
    return cos_full, sin_signed, dm, qd, kd, cd


def _merge_kernel(od_ref, or_ref, of_ref, gd_ref, gr_ref, gf_ref, gb_ref, wb_ref, wo_ref, x_ref,
                  g1_ref, n2_ref, sh_ref, sc_ref, rw_ref, rb_ref,
                  xo_ref, hp_ref, ti_ref, tw_ref, cnt_ref, cnt_scr, *, tm, sub):
    @pl.when(pl.program_id(0) == 0)
    def _():
        cnt_scr[...] = jnp.zeros(cnt_scr.shape, F32)

    lane = lax.broadcasted_iota(I32, (sub, LANES), 1)
    r_i = lax.broadcasted_iota(I32, (sub, sub), 0)
    c_i = lax.broadcasted_iota(I32, (sub, sub), 1)
    earlier = (c_i < r_i).astype(BF16)
    counts = cnt_scr[...]

    blocks = [slice(s0, s0 + sub) for s0 in range(0, tm, sub)]

    def phase_branches(rs):
        merged = None
        for i, (o_ref, gate_ref) in enumerate(((od_ref, gd_ref), (or_ref, gr_ref), (of_ref, gf_ref))):
            gate = _sigmoid(gate_ref[rs, :] + gb_ref[i].astype(BF16))
            term = gate.astype(F32) * jnp.dot(o_ref[rs, :], wb_ref[i], preferred_element_type=F32)
            merged = term if merged is None else merged + term
        return merged.astype(BF16)

    def phase_outproj(rs, merged):
        mix = jnp.dot(merged, wo_ref[...], preferred_element_type=F32)
        xn = x_ref[rs, :] + g1_ref[0] * mix
        xo_ref[rs, :] = xn
        r = lax.rsqrt(jnp.mean(xn * xn, axis=-1, keepdims=True) + NORM_EPS)
        h = xn * r * n2_ref[...]
        h = h * (1.0 + sc_ref[0]) + sh_ref[0]
        hp = _pack_pairs(h)
        hp_ref[0, rs, :] = hp[:, :QUART]
        hp_ref[1, rs, :] = hp[:, QUART:]
        hh, hm, _ = _split3(h)
        return (jnp.dot(hh, rw_ref[0], preferred_element_type=F32)
                + jnp.dot(hh, rw_ref[1], preferred_element_type=F32)
                + jnp.dot(hm, rw_ref[0], preferred_element_type=F32)) + rb_ref[...]

    def phase_topk(rs, logits, counts):
        vals, idxs = [], []
        for _ in range(TOP_K):
            m = jnp.max(logits, axis=-1, keepdims=True)
            idx = jnp.min(jnp.where(logits == m, lane, LANES), axis=-1, keepdims=True)
            vals.append(m)
            idxs.append(idx)
            logits = jnp.where(lane == idx, -3.0e38, logits)
        es = [jnp.exp(v - vals[0]) for v in vals]
        den = es[0] + es[1] + es[2] + es[3]
        hits = [lane == idxs[k] for k in range(TOP_K)]
        picked = (hits[0] | hits[1] | hits[2] | hits[3])
        base = jnp.dot(earlier, picked.astype(BF16), preferred_element_type=F32) + counts
        counts = counts + jnp.sum(picked.astype(F32), axis=0, keepdims=True)
        ti = jnp.zeros((sub, LANES), I32)
        tw = jnp.zeros((sub, LANES), F32)
        for k in range(TOP_K):
            rank = jnp.sum(jnp.where(hits[k], base, 0.0), axis=-1, keepdims=True).astype(I32)
            ti = jnp.where(lane == k, idxs[k], ti)
            ti = jnp.where(lane == TOP_K + k, rank, ti)
            tw = jnp.where(lane == k, es[k] / den, tw)
        ti_ref[rs, :] = ti
        tw_ref[rs, :] = tw
        return counts

    merged = [phase_branches(rs) for rs in blocks]
    logits = [phase_outproj(rs, m) for rs, m in zip(blocks, merged)]
    for rs, lg in zip(blocks, logits):
        counts = phase_topk(rs, lg, counts)

    cnt_scr[...] = counts
    cnt_ref[...] = counts.astype(I32)


def _merge(o_diff, o_ret, o_fox, proj, gate_b, w_branch, w_out, x2, g1, n2g, sh2, sc2, rw, rb, seq):
    n, d = x2.shape
    tm = min(MERGE_TILE, seq)
    sub = min(MERGE_SUB, tm)
    per_seq = seq // tm
    gcol = GATE_COL0 // d

    def rows(w):
        return pl.BlockSpec((tm, w), lambda i: (i, 0))

    def gspec(k):
        return pl.BlockSpec((tm, d), lambda i: (i, gcol + k))

    def whole(shape):
        return pl.BlockSpec(shape, lambda i: (0,) * len(shape))

    def per_b():
        return pl.BlockSpec((1, 1, d), lambda i: (i // per_seq, 0, 0))

    return pl.pallas_call(
        functools.partial(_merge_kernel, tm=tm, sub=sub),
        grid=(n // tm,),
        in_specs=[
            rows(BRANCH_WIDTH), rows(BRANCH_WIDTH), rows(BRANCH_WIDTH),
            gspec(0), gspec(1), gspec(2),
            whole((N_BRANCHES, 1, d)), whole((N_BRANCHES, BRANCH_WIDTH, d)), whole((d, d)),
            rows(d), per_b(), whole((1, d)), per_b(), per_b(),
            whole((2, d, LANES)), whole((1, LANES)),
        ],
        out_specs=[rows(d), pl.BlockSpec((2, tm, QUART), lambda i: (0, i, 0)), rows(LANES), rows(LANES),
                   whole((1, LANES))],
        out_shape=[
            jax.ShapeDtypeStruct((n, d), F32),
            jax.ShapeDtypeStruct((2, n, QUART), I32),
            jax.ShapeDtypeStruct((n, LANES), I32),
            jax.ShapeDtypeStruct((n, LANES), F32),
            jax.ShapeDtypeStruct((1, LANES), I32),
        ],
        scratch_shapes=[pltpu.VMEM((1, LANES), F32)],
        compiler_params=_cparams(("arbitrary",)),
        name="merge_outproj_router",
    )(o_diff, o_ret, o_fox, proj, proj, proj, gate_b, w_branch, w_out, x2, g1, n2g, sh2, sc2, rw, rb)


def _prep_up_group(w_ref, o_ref, t_scr, g):
    f = t_scr.shape[1] // 2
    cols = slice(g * LANES, (g + 1) * LANES)
    t_scr[g] = w_ref[0, 0, cols, :].T
    o_ref[0, 0:f, cols] = t_scr[g, pl.ds(0, f, stride=2), :].astype(BF16)
    o_ref[0, f:, cols] = t_scr[g, pl.ds(1, f, stride=2), :].astype(BF16)


def _prep_up_kernel(w_ref, o_ref, t_scr):
    for g in range(t_scr.shape[0]):
        _prep_up_group(w_ref, o_ref, t_scr, g)


def _prep_up(w_up, layer):
    _, e, d, f2 = w_up.shape
    groups = d // LANES
    kt = groups * LANES
    return pl.pallas_call(
        _prep_up_kernel,
        grid=(e, d // kt),
        in_specs=[pl.BlockSpec((1, 1, kt, f2), lambda i, j: (layer, i, j, 0))],
        out_specs=pl.BlockSpec((1, f2, kt), lambda i, j: (i, 0, j)),
        out_shape=jax.ShapeDtypeStruct((e, f2, d), BF16),
        scratch_shapes=[pltpu.VMEM((groups, f2, LANES), F32)],
        compiler_params=_cparams(("arbitrary", "arbitrary")),
        name="expert_up_weight_prep",
    )(w_up)


def _expert_kernel(te_ref, nu_ref, xs_ref, *refs):
    y_ref = refs[-1]
    nt = (((1,), (1,)), ((), ()))
    first_tile = pl.program_id(0) * EXPERT_GROUP

    @pl.when(first_tile < nu_ref[0])
    def _():
        for s in range(EXPERT_GROUP):
            wu_ref, bu_ref, wd_ref, bd_ref = refs[4 * s:4 * s + 4]
            rs = slice(s * EXPERT_TILE, (s + 1) * EXPERT_TILE)
            lo, hi = _unpack_pairs(jnp.concatenate([xs_ref[0, rs, :], xs_ref[1, rs, :]], axis=1))
            lo = lo.astype(BF16)
            hi = hi.astype(BF16)

            def up_half(r0):
                return (lax.dot_general(lo, wu_ref[0, r0:r0 + D_FF, :HALF], nt, preferred_element_type=F32)
                        + lax.dot_general(hi, wu_ref[0, r0:r0 + D_FF, HALF:], nt, preferred_element_type=F32))

            glu = jnp.minimum(up_half(0) + bu_ref[0, :, :D_FF], SWIGLU_LIMIT)
            lin = jnp.clip(up_half(D_FF) + bu_ref[0, :, D_FF:], -SWIGLU_LIMIT, SWIGLU_LIMIT)
            act = glu * _sigmoid(SWIGLU_ALPHA * glu) * (lin + 1.0)
            down = jnp.dot(act.astype(BF16), wd_ref[0], preferred_element_type=F32) + bd_ref[0]
            yp = _pack_pairs(down)
            y_ref[0, rs, :] = yp[:, :QUART]
            y_ref[1, rs, :] = yp[:, QUART:]

    @pl.when(first_tile >= nu_ref[0])
    def _():
        y_ref[...] = jnp.zeros(y_ref.shape, I32)


def _experts(tile_e, n_used, xs, w_up, b_up, w_down, b_down):
    r = xs.shape[1]
    tm = EXPERT_TILE * EXPERT_GROUP
    d = D_MODEL
    in_specs = [pl.BlockSpec((2, tm, QUART), lambda i, te, nu: (0, i, 0))]
    args = [xs]
    for s in range(EXPERT_GROUP):
        def pick(i, te, nu, s=s):
            return (te[i * EXPERT_GROUP + s], 0, 0)
        in_specs += [pl.BlockSpec((1, 2 * D_FF, d), pick), pl.BlockSpec((1, 1, 2 * D_FF), pick),
                     pl.BlockSpec((1, D_FF, d), pick), pl.BlockSpec((1, 1, d), pick)]
        args += [w_up, b_up, w_down, b_down]
    grid_spec = pltpu.PrefetchScalarGridSpec(
        num_scalar_prefetch=2,
        grid=(r // tm,),
        in_specs=in_specs,
        out_specs=pl.BlockSpec((2, tm, QUART), lambda i, te, nu: (0, i, 0)),
    )
    return pl.pallas_call(
        _expert_kernel,
        grid_spec=grid_spec,
        out_shape=jax.ShapeDtypeStruct((2, r, QUART), I32),
        compiler_params=_cparams(("arbitrary",)),
        name="expert_mlp",
    )(tile_e, n_used, *args)


def _combine_kernel(y_ref, tw_ref, x_ref, g2_ref, fg_ref, o_ref):
    xn = _moe_residual(y_ref, tw_ref, x_ref, g2_ref)
    r = lax.rsqrt(jnp.mean(xn * xn, axis=-1, keepdims=True) + NORM_EPS)
    o_ref[...] = xn * r * fg_ref[...]


def _combine(ytok, top_w, x2, g2, final_g, seq):
    n, d = x2.shape
    tm = min(ROW_TILE_OUT, seq)
    per_seq = seq // tm
    return pl.pallas_call(
        _combine_kernel,
        grid=(n // tm,),
        in_specs=[
            pl.BlockSpec((2, TOP_K, tm, QUART), lambda i: (0, 0, i, 0)),
            pl.BlockSpec((tm, LANES), lambda i: (i, 0)),
            pl.BlockSpec((tm, d), lambda i: (i, 0)),
            pl.BlockSpec((1, 1, d), lambda i: (i // per_seq, 0, 0)),
            pl.BlockSpec((1, d), lambda i: (0, 0)),
        ],
        out_specs=pl.BlockSpec((tm, d), lambda i: (i, 0)),
        out_shape=jax.ShapeDtypeStruct((n, d), F32),
        compiler_params=_cparams(("arbitrary",)),
        name="moe_combine",
    )(ytok, top_w, x2, g2, final_g)


def _route(top_idx, counts, n_rows):
    tm = EXPERT_TILE
    e = top_idx[:, :TOP_K]
    rank = top_idx[:, TOP_K:2 * TOP_K]
    sizes = counts[0, :N_EXPERTS]
    padded = ((sizes + tm - 1) // tm) * tm
    pend = jnp.cumsum(padded)
    pstart = pend - padded
    onehot = e[:, :, None] == jnp.arange(N_EXPERTS, dtype=I32)[None, None, :]
    pos = jnp.sum(jnp.where(onehot, pstart[None, None, :], 0), axis=-1) + rank
    tiles = jnp.arange(n_rows // tm, dtype=I32) * tm
    tile_e = jnp.minimum(jnp.sum(tiles[:, None] >= pend[None, :], axis=-1), N_EXPERTS - 1).astype(I32)
    n_used = (pend[-1] // tm).astype(I32).reshape(1)
    return pos.T.astype(I32), tile_e, n_used


def _sc_gather(table, idx):
    m = idx.shape[0]
    width = table.shape[1]
    assert m % (SC_WINDOW * SC_WORKERS) == 0
    mesh = plsc.VectorSubcoreMesh(core_axis_name="c", subcore_axis_name="s")

    @pl.kernel(out_type=jax.ShapeDtypeStruct((m, width), table.dtype), mesh=mesh)
    def gather_rows(t_hbm, i_hbm, o_hbm):
        def body(i_vmem, o_vmem):
            pltpu.sync_copy(t_hbm.at[i_vmem.at[0]], o_vmem)

        pltpu.emit_pipeline(
            body,
            grid=(m // SC_WINDOW,),
            in_specs=[pl.BlockSpec((1, SC_WINDOW), lambda i: (0, i))],
            out_specs=[pl.BlockSpec((SC_WINDOW, width), lambda i: (i, 0))],
            core_axis_name=("c", "s"),
            dimension_semantics=(pltpu.PARALLEL,),
        )(i_hbm, o_hbm)

    return gather_rows(table, idx.reshape(1, m))


def _sc_scatter(rows, idx, n_out):
    t, width = rows.shape
    picks = idx.shape[0]
    assert t % SC_WINDOW == 0
    mesh = plsc.VectorSubcoreMesh(core_axis_name="c", subcore_axis_name="s")

    @pl.kernel(out_type=jax.ShapeDtypeStruct((n_out, width), rows.dtype), mesh=mesh)
    def scatter_rows(x_hbm, i_hbm, o_hbm):
        def body(x_vmem, i_vmem):
            for k in range(picks):
                pltpu.sync_copy(x_vmem, o_hbm.at[i_vmem.at[k]])

        pltpu.emit_pipeline(
            body,
            grid=(t // SC_WINDOW,),
            in_specs=[pl.BlockSpec((SC_WINDOW, width), lambda i: (i, 0)),
                      pl.BlockSpec((picks, SC_WINDOW), lambda i: (0, i))],
            out_specs=[],
            core_axis_name=("c", "s"),
            dimension_semantics=(pltpu.PARALLEL,),
        )(x_hbm, i_hbm)

    return scatter_rows(rows, idx)


def _dispatch(h2p, pos_t, n_rows):
    n = h2p.shape[1]
    idx = jnp.concatenate([pos_t, pos_t + n_rows], axis=1)
    return _sc_scatter(h2p.reshape(2 * n, QUART), idx, 2 * n_rows).reshape(2, n_rows, QUART)


def _collect(y, pos_t):
    n_rows = y.shape[1]
    flat = pos_t.reshape(-1)
    out = _sc_gather(y.reshape(2 * n_rows, QUART), jnp.concatenate([flat, flat + n_rows]))
    return out.reshape(2, TOP_K, pos_t.shape[1], QUART)


def kernel(x, c, norm1_g, norm2_g, ada_w, ada_b, w_in, gate_b, fox_fb, lam_q1, lam_k1, lam_q2, lam_k2,
           diff_subln_g, w_branch, w_out, router_w, router_b, exp_w_up, exp_b_up, exp_w_down,
           exp_b_down, final_g):
    batch, seq, d = x.shape
    depth = ada_w.shape[0]
    n = batch * seq
    x2 = x.reshape(n, d)

    mod = _ada_mod(c, ada_w, ada_b)
    tables = _retention_tables(seq)
    n_rows = n * TOP_K + N_EXPERTS * EXPERT_TILE
    blk = BRANCH_WIDTH // LANES

    w_in_t = jnp.swapaxes(w_in, 1, 2)
    pending = None
    for l in range(depth):
        lam_init = 0.8 - 0.6 * math.exp(-0.3 * l)
        sh1, sc1, g1, sh2, sc2, g2 = [m.reshape(batch, 1, d) for m in jnp.split(mod[l], 6, axis=-1)]

        w_main, w_f = _prep_win(w_in_t, l)
        outs = _inproj(x2, norm1_g[l].reshape(1, d), sh1, sc1, w_main, w_f, seq, pending)
        proj, flog = outs[0], outs[1]
        if pending is not None:
            x2 = outs[2]

        fb_pad = jnp.pad(fox_fb[l], (0, LANES - FOX_HEADS)).reshape(1, LANES)
        frow = _fcum(flog, fb_pad, batch, seq).reshape(batch * blk, 2, seq)

        lamv = jnp.stack([lam_q1[l], lam_k1[l], lam_q2[l], lam_k2[l]])
        diff_extra = (lamv, diff_subln_g[l].reshape(1, LANES))
        if _rider_fits(batch, exp_w_up):
            o_diff, w_up = _attention(proj, "diff", 0, blk, 2 * blk, batch, seq, diff_extra, lam_init,
                                      rider=("up", exp_w_up, l))
        else:
            o_diff = _attention(proj, "diff", 0, blk, 2 * blk, batch, seq, diff_extra, lam_init)
            w_up = _prep_up(exp_w_up, l)
        o_ret = _retention(proj, tables, batch, seq)
        if _rider_fits(batch, exp_w_down):
            o_fox, w_down = _attention(proj, "fox", 7 * blk, 8 * blk, 9 * blk, batch, seq, (frow,),
                                       rider=("down", exp_w_down, l))
        else:
            o_fox = _attention(proj, "fox", 7 * blk, 8 * blk, 9 * blk, batch, seq, (frow,))
            w_down = exp_w_down[l].astype(BF16)

        rw = jnp.pad(router_w[l], ((0, 0), (0, LANES - N_EXPERTS)))
        rw_hi = rw.astype(BF16)
        rw = jnp.stack([rw_hi, (rw - rw_hi.astype(F32)).astype(BF16)])
        rb = jnp.pad(router_b[l], (0, LANES - N_EXPERTS), constant_values=NEG_INF).reshape(1, LANES)
        x2, h2p, top_idx, top_w, counts = _merge(
            o_diff, o_ret, o_fox, proj, gate_b[l].reshape(N_BRANCHES, 1, d), w_branch[l].astype(BF16),
            w_out[l].astype(BF16), x2, g1, norm2_g[l].reshape(1, d), sh2, sc2, rw, rb, seq)

        pos_t, tile_e, n_used = _route(top_idx, counts, n_rows)
        xs = _dispatch(h2p, pos_t, n_rows)

        bu = exp_b_up[l]
        b_up = jnp.concatenate([bu[:, 0::2], bu[:, 1::2]], axis=-1).reshape(N_EXPERTS, 1, 2 * D_FF)
        y = _experts(tile_e, n_used, xs, w_up, b_up, w_down,
                     exp_b_down[l].reshape(N_EXPERTS, 1, d))
        pending = (_collect(y, pos_t), top_w, g2)

    ytok, top_w, g2 = pending
    x2 = _combine(ytok, top_w, x2, g2, final_g.reshape(1, d), seq)
    return x2.reshape(batch, seq, d)
```

```python
import functools
import math

import jax
import jax.numpy as jnp
from jax import lax
from jax.experimental import pallas as pl
from jax.experimental.pallas import tpu as pltpu
from jax.experimental.pallas import tpu_sc as plsc

F32 = jnp.float32
BF16 = jnp.bfloat16
I32 = jnp.int32
U32 = jnp.uint32

D_MODEL = 1024
HEAD_DIM = 64
BRANCH_WIDTH = D_MODEL // 2
N_BRANCHES = 3
FOX_HEADS = BRANCH_WIDTH // HEAD_DIM
RET_HEADS = 4
RET_HEAD_DIM = 128
ROPE_BASE = 10000.0
RET_GAMMA_EXP0 = 5.0
N_EXPERTS = 32
TOP_K = 4
D_FF = D_MODEL
SWIGLU_LIMIT = 7.0
SWIGLU_ALPHA = 1.702
NORM_EPS = 1e-5
NEG_INF = -1e30

LANES = 128
MAIN_COLS = 8192
GATE_COL0 = 10 * BRANCH_WIDTH
HALF = D_MODEL // 2
QUART = HALF // 2
SC_WINDOW = 128
SC_WORKERS = 32
VMEM_LIMIT = 48 * 1024 * 1024
VMEM_LIMIT_INPROJ = 56 * 1024 * 1024

ROW_TILE_IN = 512
COL_TILE_IN = 1024
ATTN_TILE = 256
RET_CHUNK = 256
ROW_TILE_OUT = 256
MERGE_TILE = 1024
MERGE_SUB = 512
EXPERT_TILE = 256
EXPERT_GROUP = 2
CUM_CHUNK = 256


def _cparams(sem):
    return pltpu.CompilerParams(dimension_semantics=sem, vmem_limit_bytes=VMEM_LIMIT)


def _sigmoid(x):
    return 0.5 * jnp.tanh(0.5 * x) + 0.5


def _pack_pairs(v):
    u = lax.bitcast_convert_type(v.astype(BF16).astype(F32), U32)
    half = v.shape[-1] // 2
    w = (u[:, half:] & jnp.uint32(0xFFFF0000)) | (u[:, :half] >> 16)
    return lax.bitcast_convert_type(w, I32)


def _unpack_pairs(w):
    u = lax.bitcast_convert_type(w, U32)
    lo = lax.bitcast_convert_type(u << 16, F32)
    hi = lax.bitcast_convert_type(u & jnp.uint32(0xFFFF0000), F32)
    return lo, hi


def _moe_residual(y_ref, tw_ref, x_ref, g2_ref):
    tw = tw_ref[...]
    parts = [None] * 4
    for k in range(TOP_K):
        w = tw[:, k:k + 1]
        for piece in range(2):
            lo, hi = _unpack_pairs(y_ref[piece, k])
            for slot, v in ((piece, lo), (2 + piece, hi)):
                parts[slot] = w * v if parts[slot] is None else parts[slot] + w * v
    return x_ref[...] + g2_ref[0] * jnp.concatenate(parts, axis=-1)


def _ada_kernel(c_ref, w_ref, b_ref, o_ref):
    c = c_ref[...]
    ca = c * _sigmoid(c)
    o_ref[0] = jnp.dot(ca, w_ref[0], preferred_element_type=F32) + b_ref[0]


def _ada_mod(c, ada_w, ada_b):
    depth, d, cols = ada_w.shape
    b = c.shape[0]
    tn = 1024
    return pl.pallas_call(
        _ada_kernel,
        grid=(depth, cols // tn),
        in_specs=[
            pl.BlockSpec((b, d), lambda l, j: (0, 0)),
            pl.BlockSpec((1, d, tn), lambda l, j: (l, 0, j)),
            pl.BlockSpec((1, 1, tn), lambda l, j: (l, 0, j)),
        ],
        out_specs=pl.BlockSpec((1, b, tn), lambda l, j: (l, 0, j)),
        out_shape=jax.ShapeDtypeStruct((depth, b, cols), F32),
        compiler_params=_cparams(("arbitrary", "arbitrary")),
        name="ada_mod",
    )(c, ada_w, ada_b.reshape(depth, 1, cols))


def _prep_win_kernel(w_ref, wm_ref, wf_ref):
    wm_ref[:GATE_COL0, :] = w_ref[0, :GATE_COL0, :].astype(BF16)
    wm_ref[GATE_COL0:, :] = w_ref[0, GATE_COL0 + FOX_HEADS:, :].astype(BF16)
    wf_ref[...] = jnp.zeros(wf_ref.shape, BF16)
    wf_ref[:FOX_HEADS, :] = w_ref[0, GATE_COL0:GATE_COL0 + FOX_HEADS, :].astype(BF16)


def _prep_win(w_t, layer):
    _, cols, d = w_t.shape
    ct = 256
    return pl.pallas_call(
        _prep_win_kernel,
        grid=(d // ct,),
        in_specs=[pl.BlockSpec((1, cols, ct), lambda i: (layer, 0, i))],
        out_specs=[pl.BlockSpec((MAIN_COLS, ct), lambda i: (0, i)), pl.BlockSpec((LANES, ct), lambda i: (0, i))],
        out_shape=[jax.ShapeDtypeStruct((MAIN_COLS, d), BF16), jax.ShapeDtypeStruct((LANES, d), BF16)],
        compiler_params=_cparams(("arbitrary",)),
        name="inproj_weight_prep",
    )(w_t)


def _inproj_kernel(*refs, moe):
    if moe:
        y_ref, tw_ref, x_ref, g2_ref, g_ref, sh_ref, sc_ref, w_ref, wf_ref, proj_ref, fl_ref, xo_ref = refs
        x = _moe_residual(y_ref, tw_ref, x_ref, g2_ref)
        xo_ref[...] = x
    else:
        x_ref, g_ref, sh_ref, sc_ref, w_ref, wf_ref, proj_ref, fl_ref = refs
        x = x_ref[...]
    r = lax.rsqrt(jnp.mean(x * x, axis=-1, keepdims=True) + NORM_EPS)
    h = x * r * g_ref[...]
    h = h * (1.0 + sc_ref[0]) + sh_ref[0]
    hb = h.astype(BF16)
    nt = (((1,), (1,)), ((), ()))
    fl_ref[...] = lax.dot_general(hb, wf_ref[...], nt, preferred_element_type=F32)
    for j in range(MAIN_COLS // COL_TILE_IN):
        cs = slice(j * COL_TILE_IN, (j + 1) * COL_TILE_IN)
        proj_ref[:, cs] = lax.dot_general(hb, w_ref[cs, :], nt, preferred_element_type=F32).astype(BF16)


def _inproj(x2, g, sh, sc, w_main, w_f, seq, moe=None):
    n, d = x2.shape
    tm = min(ROW_TILE_IN, seq)
    per_seq = seq // tm
    resident = pl.Buffered(1)

    def per_b():
        return pl.BlockSpec((1, 1, d), lambda i: (i // per_seq, 0, 0))

    in_specs = [
        pl.BlockSpec((tm, d), lambda i: (i, 0)),
        pl.BlockSpec((1, d), lambda i: (0, 0)),
        per_b(), per_b(),
        pl.BlockSpec((MAIN_COLS, d), lambda i: (0, 0), pipeline_mode=resident),
        pl.BlockSpec((LANES, d), lambda i: (0, 0), pipeline_mode=resident),
    ]
    out_specs = [pl.BlockSpec((tm, MAIN_COLS), lambda i: (i, 0)), pl.BlockSpec((tm, LANES), lambda i: (i, 0))]
    out_shape = [jax.ShapeDtypeStruct((n, MAIN_COLS), BF16), jax.ShapeDtypeStruct((n, LANES), F32)]
    args = [x2, g, sh, sc, w_main, w_f]
    if moe is not None:
        ytok, top_w, g2 = moe
        in_specs = [pl.BlockSpec((2, TOP_K, tm, QUART), lambda i: (0, 0, i, 0)),
                    pl.BlockSpec((tm, LANES), lambda i: (i, 0)), in_specs[0], per_b()] + in_specs[1:]
        args = [ytok, top_w, x2, g2] + args[1:]
        out_specs.append(pl.BlockSpec((tm, d), lambda i: (i, 0)))
        out_shape.append(jax.ShapeDtypeStruct((n, d), F32))
    return pl.pallas_call(
        functools.partial(_inproj_kernel, moe=moe is not None),
        grid=(n // tm,),
        in_specs=in_specs,
        out_specs=out_specs,
        out_shape=out_shape,
        compiler_params=pltpu.CompilerParams(dimension_semantics=("arbitrary",), vmem_limit_bytes=VMEM_LIMIT_INPROJ),
        name="inproj",
    )(*args)


def _split3(v):
    hi = v.astype(BF16)
    r1 = v - hi.astype(F32)
    mid = r1.astype(BF16)
    lo = (r1 - mid.astype(F32)).astype(BF16)
    return hi, mid, lo


def _fcum_kernel(fl_ref, fb_ref, o_ref, *, seq):
    c = min(CUM_CHUNK, seq)
    row = lax.broadcasted_iota(I32, (c, c), 0)
    col = lax.broadcasted_iota(I32, (c, c), 1)
    lower = (col <= row).astype(BF16)
    carry = jnp.zeros((1, LANES), F32)
    for ci in range(seq // c):
        x = fl_ref[pl.ds(ci * c, c), :] + fb_ref[...]
        lf = jnp.minimum(x, 0.0) - jnp.log(1.0 + jnp.exp(-jnp.abs(x)))
        hi, mid, lo = _split3(lf)
        cs = (jnp.dot(lower, hi, preferred_element_type=F32)
              + jnp.dot(lower, mid, preferred_element_type=F32)
              + jnp.dot(lower, lo, preferred_element_type=F32)) + carry
        o_ref[0, :, pl.ds(ci * c, c)] = cs.T[:FOX_HEADS, :]
        carry = cs[c - 1:c, :]


def _fcum(flog, fb_pad, batch, seq):
    return pl.pallas_call(
        functools.partial(_fcum_kernel, seq=seq),
        grid=(batch,),
        in_specs=[
            pl.BlockSpec((seq, LANES), lambda b: (b, 0)),
            pl.BlockSpec((1, LANES), lambda b: (0, 0)),
        ],
        out_specs=pl.BlockSpec((1, FOX_HEADS, seq), lambda b: (b, 0, 0)),
        out_shape=jax.ShapeDtypeStruct((batch, FOX_HEADS, seq), F32),
        compiler_params=_cparams(("arbitrary",)),
        name="fox_decay_cumsum",
    )(flog, fb_pad)


def _attn_kernel(*refs, mode, t, seq, lam_init, rider):
    if rider:
        refs = list(refs)
        wt_scr = refs.pop() if rider == "up" else None
        n_in = 5 if mode == "diff" else 4
        wraw_ref = refs.pop(n_in)
        wprep_ref = refs.pop(n_in + 1)
    if mode == "diff":
        q_ref, k_ref, v_ref, lam_ref, g_ref, o_ref, vx_scr = refs
        f_ref = None
        lv = lam_ref[...]
        lam = (jnp.exp(jnp.sum(lv[0:1, :] * lv[1:2, :])) - jnp.exp(jnp.sum(lv[2:3, :] * lv[3:4, :]))
               + lam_init)
    else:
        q_ref, k_ref, v_ref, f_ref, o_ref, vx_scr = refs
    lane = lax.broadcasted_iota(I32, (t, LANES), 1)
    row = lax.broadcasted_iota(I32, (t, t), 0)
    col = lax.broadcasted_iota(I32, (t, t), 1)
    nt = (((1,), (1,)), ((), ()))

    v_all = v_ref[...]
    one = jnp.ones_like(v_all)
    if mode == "diff":
        vx_scr[0] = jnp.concatenate([v_all, one], axis=1)
    else:
        lane_s = lax.broadcasted_iota(I32, v_all.shape, 1)
        vx_scr[0] = jnp.where(lane_s < HEAD_DIM, v_all, one)
        vx_scr[1] = jnp.where(lane_s >= HEAD_DIM, v_all, one)

    n_strips = seq // t
    pairs = [(i, n_strips - 1 - i) for i in range(n_strips // 2)] if n_strips % 2 == 0 else [(i,) for i in range(n_strips)]
    for gi, group in enumerate(pairs):
        if rider == "up":
            for g in range(gi, wt_scr.shape[0], len(pairs)):
                _prep_up_group(wraw_ref, wprep_ref, wt_scr, g)
        elif rider == "down":
            span = wprep_ref.shape[1] // len(pairs)
            rows = slice(gi * span, (gi + 1) * span)
            wprep_ref[0, rows, :] = wraw_ref[0, 0, rows, :].astype(BF16)
        units = [(qi, idx) for qi in group for idx in range(2)]
        qms = {}
        for qi in group:
            q = q_ref[qi * t:(qi + 1) * t, :] * jnp.asarray(HEAD_DIM ** -0.5, BF16)
            zero = jnp.zeros_like(q)
            qms[qi, 0] = jnp.where(lane < HEAD_DIM, q, zero)
            qms[qi, 1] = jnp.where(lane >= HEAD_DIM, q, zero)

        def scores(u, k0, k1):
            s_ = lax.dot_general(qms[u], k_ref[k0:k1, :], nt, preferred_element_type=F32)
            if f_ref is not None:
                s_ = s_ - f_ref[0, u[1]:u[1] + 1, k0:k1]
            return s_

        sds = {u: jnp.where(col <= row, scores(u, u[0] * t, (u[0] + 1) * t), NEG_INF) for u in units}
        sms = {u: scores(u, 0, u[0] * t) for u in units if u[0] > 0}
        ms = {}
        for u in units:
            m = jnp.max(sds[u], axis=-1, keepdims=True)
            if u in sms:
                m = jnp.maximum(m, jnp.max(sms[u], axis=-1, keepdims=True))
            ms[u] = m
        pds = {u: jnp.exp((sds[u] - ms[u]).astype(BF16)) for u in units}
        pms = {u: jnp.exp((sms[u] - ms[u]).astype(BF16)) for u in sms}
        outs = {}
        for u in units:
            vx = vx_scr.at[0 if mode == "diff" else u[1]]
            lo = u[0] * t
            acc = jnp.dot(pds[u], vx[lo:lo + t, :], preferred_element_type=F32)
            if u in pms:
                acc = acc + jnp.dot(pms[u], vx[0:lo, :], preferred_element_type=F32)
            outs[u] = acc
        for qi in group:
            o0, o1 = outs[qi, 0], outs[qi, 1]
            if mode == "diff":
                a1 = o0[:, :LANES] / o0[:, LANES:]
                a2 = o1[:, :LANES] / o1[:, LANES:]
                o = a1 - lam * a2
                r = lax.rsqrt(jnp.mean(o * o, axis=-1, keepdims=True) + NORM_EPS)
                o = o * r * g_ref[...] * (1.0 - lam_init)
            else:
                num = jnp.where(lane < HEAD_DIM, o0, o1)
                den = jnp.where(lane < HEAD_DIM, pltpu.roll(o0, HEAD_DIM, 1), pltpu.roll(o1, HEAD_DIM, 1))
                o = num / den
            o_ref[qi * t:(qi + 1) * t, :] = o.astype(BF16)


def _attention(proj, mode, col_q, col_k, col_v, batch, seq, extra, lam_init=0.0, rider=None):
    n = proj.shape[0]
    t = min(ATTN_TILE, seq)
    n_blk = BRANCH_WIDTH // LANES
    in_specs = [
        pl.BlockSpec((seq, LANES), lambda b, h: (b, col_q + h)),
        pl.BlockSpec((seq, LANES), lambda b, h: (b, col_k + h)),
        pl.BlockSpec((seq, LANES), lambda b, h: (b, col_v + h)),
    ]
    if mode == "diff":
        lamv, g = extra
        in_specs += [pl.BlockSpec((4, HEAD_DIM), lambda b, h: (0, 0)),
                     pl.BlockSpec((1, LANES), lambda b, h: (0, 0))]
        args = (proj, proj, proj, lamv, g)
    else:
        (frow,) = extra
        in_specs += [pl.BlockSpec((1, 2, seq), lambda b, h: (b * n_blk + h, 0, 0))]
        args = (proj, proj, proj, frow)
    out_specs = [pl.BlockSpec((seq, LANES), lambda b, h: (b, h))]
    out_shape = [jax.ShapeDtypeStruct((n, BRANCH_WIDTH), BF16)]
    scratch = [pltpu.VMEM((1, seq, 2 * LANES) if mode == "diff" else (2, seq, LANES), BF16)]
    if rider is not None:
        kind, w, layer = rider
        _, e, rows, cols = w.shape
        parts = batch * n_blk // e
        kt = rows // parts
        in_specs.append(pl.BlockSpec((1, 1, kt, cols), lambda b, h: (layer, (b * n_blk + h) // parts, (b * n_blk + h) % parts, 0)))
        args = args + (w,)
        if kind == "up":
            out_specs.append(pl.BlockSpec((1, cols, kt), lambda b, h: ((b * n_blk + h) // parts, 0, (b * n_blk + h) % parts)))
            out_shape.append(jax.ShapeDtypeStruct((e, cols, rows), BF16))
            scratch.append(pltpu.VMEM((kt // LANES, cols, LANES), F32))
        else:
            out_specs.append(pl.BlockSpec((1, kt, cols), lambda b, h: ((b * n_blk + h) // parts, (b * n_blk + h) % parts, 0)))
            out_shape.append(jax.ShapeDtypeStruct((e, rows, cols), BF16))
    outs = pl.pallas_call(
        functools.partial(_attn_kernel, mode=mode, t=t, seq=seq, lam_init=lam_init, rider=rider[0] if rider else None),
        grid=(batch, n_blk),
        in_specs=in_specs,
        out_specs=out_specs,
        out_shape=out_shape,
        scratch_shapes=scratch,
        compiler_params=_cparams(("arbitrary", "arbitrary")),
        name="attn_" + mode,
    )(*args)
    return outs if rider is not None else outs[0]


def _rider_fits(batch, w):
    _, e, rows, _ = w.shape
    steps = batch * (BRANCH_WIDTH // LANES)
    return steps % e == 0 and (rows // LANES) % (steps // e) == 0


def _ret_kernel(q_ref, k_ref, v_ref, g_ref, cos_ref, sin_ref, dm_ref, qd_ref, kd_ref, cd_ref,
                o_ref, *, seq, c):
    st = jnp.zeros((RET_HEAD_DIM, RET_HEAD_DIM), F32)
    half = RET_HEAD_DIM // 2

    for ci in range(seq // c):
        sl = pl.ds(ci * c, c)
        cos = cos_ref[sl, :]
        sin = sin_ref[sl, :]
        q = q_ref[sl, :].astype(F32)
        k = k_ref[sl, :].astype(F32)
        qr = q * cos + pltpu.roll(q, half, 1) * sin
        kr = (k * cos + pltpu.roll(k, half, 1) * sin) * (RET_HEAD_DIM ** -0.5)
        v = v_ref[sl, :]
        sc = lax.dot_general(qr.astype(BF16), kr.astype(BF16), (((1,), (1,)), ((), ())),
                             preferred_element_type=F32) * dm_ref[0]
        inner = jnp.dot(sc.astype(BF16), v, preferred_element_type=F32)
        cross = jnp.dot((qr * qd_ref[0]).astype(BF16), st.astype(BF16), preferred_element_type=F32)
        o = inner + cross
        kt = (kr * kd_ref[0]).T.astype(BF16)
        st = st * cd_ref[0] + jnp.dot(kt, v, preferred_element_type=F32)
        mu = jnp.mean(o, axis=-1, keepdims=True)
        oc = o - mu
        var = jnp.mean(oc * oc, axis=-1, keepdims=True)
        on = oc * lax.rsqrt(var + NORM_EPS)
        g = g_ref[sl, :].astype(F32)
        o_ref[sl, :] = (on * (g * _sigmoid(g))).astype(BF16)


def _retention(proj, tables, batch, seq):
    n = proj.shape[0]
    c = min(RET_CHUNK, seq)
    cos, sin, dm, qd, kd, cd = tables
    col = BRANCH_WIDTH * 3 // LANES
    step = BRANCH_WIDTH // LANES

    def pspec(k):
        return pl.BlockSpec((seq, LANES), lambda b, h: (b, col + k * step + h))

    return pl.pallas_call(
        functools.partial(_ret_kernel, seq=seq, c=c),
        grid=(batch, RET_HEADS),
        in_specs=[
            pspec(0), pspec(1), pspec(2), pspec(3),
            pl.BlockSpec((seq, LANES), lambda b, h: (0, 0)),
            pl.BlockSpec((seq, LANES), lambda b, h: (0, 0)),
            pl.BlockSpec((1, c, c), lambda b, h: (h, 0, 0)),
            pl.BlockSpec((1, c, LANES), lambda b, h: (h, 0, 0)),
            pl.BlockSpec((1, c, LANES), lambda b, h: (h, 0, 0)),
            pl.BlockSpec((1, LANES, LANES), lambda b, h: (h, 0, 0)),
        ],
        out_specs=pl.BlockSpec((seq, LANES), lambda b, h: (b, h)),
        out_shape=jax.ShapeDtypeStruct((n, BRANCH_WIDTH), BF16),
        compiler_params=_cparams(("arbitrary", "arbitrary")),
        name="retention",
    )(proj, proj, proj, proj, cos, sin, dm, qd, kd, cd)


def _retention_tables(seq):
    c = min(RET_CHUNK, seq)
    d = RET_HEAD_DIM
    inv = ROPE_BASE ** (-jnp.arange(0, d, 2, dtype=F32) / d)
    ang = jnp.arange(seq).astype(F32)[:, None] * inv[None, :]
    cos = jnp.cos(ang)
    sin = jnp.sin(ang)
    cos_full = jnp.concatenate([cos, cos], axis=-1)
    sin_signed = jnp.concatenate([-sin, sin], axis=-1)
    log_gamma = jnp.log1p(-jnp.exp2(-RET_GAMMA_EXP0 - jnp.arange(RET_HEADS, dtype=F32)))
    idx = jnp.arange(c, dtype=F32)
    dist = idx[:, None] - idx[None, :]
    dm = jnp.where(dist >= 0, jnp.exp(log_gamma[:, None, None] * jnp.maximum(dist, 0.0)), 0.0)
    qd = jnp.exp(log_gamma[:, None] * (idx + 1.0))
    kd = jnp.exp(log_gamma[:, None] * (c - 1.0 - idx))
    cd = jnp.exp(log_gamma * c)
    qd = jnp.broadcast_to(qd[:, :, None], (RET_HEADS, c, LANES))
    kd = jnp.broadcast_to(kd[:, :, None], (RET_HEADS, c, LANES))
    cd = jnp.broadcast_to(cd[:, None, None], (RET_HEADS, LANES, LANES))
    return cos_full, sin_signed, dm, qd, kd, cd


def _merge_kernel(od_ref, or_ref, of_ref, gd_ref, gr_ref, gf_ref, gb_ref, wb_ref, wo_ref, x_ref,
                  g1_ref, n2_ref, sh_ref, sc_ref, rw_ref, rb_ref,
                  xo_ref, hp_ref, ti_ref, tw_ref, cnt_ref, cnt_scr, *, tm, sub):
    @pl.when(pl.program_id(0) == 0)
    def _():
        cnt_scr[...] = jnp.zeros(cnt_scr.shape, F32)

    lane = lax.broadcasted_iota(I32, (sub, LANES), 1)
    r_i = lax.broadcasted_iota(I32, (sub, sub), 0)
    c_i = lax.broadcasted_iota(I32, (sub, sub), 1)
    earlier = (c_i < r_i).astype(BF16)
    counts = cnt_scr[...]

    blocks = [slice(s0, s0 + sub) for s0 in range(0, tm, sub)]

    def phase_branches(rs):
        merged = None
        for i, (o_ref, gate_ref) in enumerate(((od_ref, gd_ref), (or_ref, gr_ref), (of_ref, gf_ref))):
            gate = _sigmoid(gate_ref[rs, :] + gb_ref[i].astype(BF16))
            term = gate.astype(F32) * jnp.dot(o_ref[rs, :], wb_ref[i], preferred_element_type=F32)
            merged = term if merged is None else merged + term
        return merged.astype(BF16)

    def phase_outproj(rs, merged):
        mix = jnp.dot(merged, wo_ref[...], preferred_element_type=F32)
        xn = x_ref[rs, :] + g1_ref[0] * mix
        xo_ref[rs, :] = xn
        r = lax.rsqrt(jnp.mean(xn * xn, axis=-1, keepdims=True) + NORM_EPS)
        h = xn * r * n2_ref[...]
        h = h * (1.0 + sc_ref[0]) + sh_ref[0]
        hp = _pack_pairs(h)
        hp_ref[0, rs, :] = hp[:, :QUART]
        hp_ref[1, rs, :] = hp[:, QUART:]
        hh, hm, _ = _split3(h)
        both = jnp.dot(hh, rw_ref[...], preferred_element_type=F32)
        return (both[:, :LANES] + both[:, LANES:]
                + jnp.dot(hm, rw_ref[:, :LANES], preferred_element_type=F32)) + rb_ref[...]

    def phase_topk(rs, logits, counts):
        vals, idxs = [], []
        for _ in range(TOP_K):
            m = jnp.max(logits, axis=-1, keepdims=True)
            idx = jnp.min(jnp.where(logits == m, lane, LANES), axis=-1, keepdims=True)
            vals.append(m)
            idxs.append(idx)
            logits = jnp.where(lane == idx, -3.0e38, logits)
        es = [jnp.exp(v - vals[0]) for v in vals]
        den = es[0] + es[1] + es[2] + es[3]
        hits = [lane == idxs[k] for k in range(TOP_K)]
        picked = (hits[0] | hits[1] | hits[2] | hits[3])
        base = jnp.dot(earlier, picked.astype(BF16), preferred_element_type=F32) + counts
        counts = counts + jnp.sum(picked.astype(F32), axis=0, keepdims=True)
        ti = jnp.zeros((sub, LANES), I32)
        tw = jnp.zeros((sub, LANES), F32)
        for k in range(TOP_K):
            rank = jnp.sum(jnp.where(hits[k], base, 0.0), axis=-1, keepdims=True).astype(I32)
            ti = jnp.where(lane == k, idxs[k], ti)
            ti = jnp.where(lane == TOP_K + k, rank, ti)
            tw = jnp.where(lane == k, es[k] / den, tw)
        ti_ref[rs, :] = ti
        tw_ref[rs, :] = tw
        return counts

    merged = [phase_branches(rs) for rs in blocks]
    logits = [phase_outproj(rs, m) for rs, m in zip(blocks, merged)]
    for rs, lg in zip(blocks, logits):
        counts = phase_topk(rs, lg, counts)

    cnt_scr[...] = counts
    cnt_ref[...] = counts.astype(I32)


def _merge(o_diff, o_ret, o_fox, proj, gate_b, w_branch, w_out, x2, g1, n2g, sh2, sc2, rw, rb, seq):
    n, d = x2.shape
    tm = min(MERGE_TILE, seq)
    sub = min(MERGE_SUB, tm)
    per_seq = seq // tm
    gcol = GATE_COL0 // d

    def rows(w):
        return pl.BlockSpec((tm, w), lambda i: (i, 0))

    def gspec(k):
        return pl.BlockSpec((tm, d), lambda i: (i, gcol + k))

    def whole(shape):
        return pl.BlockSpec(shape, lambda i: (0,) * len(shape))

    def per_b():
        return pl.BlockSpec((1, 1, d), lambda i: (i // per_seq, 0, 0))

    return pl.pallas_call(
        functools.partial(_merge_kernel, tm=tm, sub=sub),
        grid=(n // tm,),
        in_specs=[
            rows(BRANCH_WIDTH), rows(BRANCH_WIDTH), rows(BRANCH_WIDTH),
            gspec(0), gspec(1), gspec(2),
            whole((N_BRANCHES, 1, d)), whole((N_BRANCHES, BRANCH_WIDTH, d)), whole((d, d)),
            rows(d), per_b(), whole((1, d)), per_b(), per_b(),
            whole((d, 2 * LANES)), whole((1, LANES)),
        ],
        out_specs=[rows(d), pl.BlockSpec((2, tm, QUART), lambda i: (0, i, 0)), rows(LANES), rows(LANES),
                   whole((1, LANES))],
        out_shape=[
            jax.ShapeDtypeStruct((n, d), F32),
            jax.ShapeDtypeStruct((2, n, QUART), I32),
            jax.ShapeDtypeStruct((n, LANES), I32),
            jax.ShapeDtypeStruct((n, LANES), F32),
            jax.ShapeDtypeStruct((1, LANES), I32),
        ],
        scratch_shapes=[pltpu.VMEM((1, LANES), F32)],
        compiler_params=_cparams(("arbitrary",)),
        name="merge_outproj_router",
    )(o_diff, o_ret, o_fox, proj, proj, proj, gate_b, w_branch, w_out, x2, g1, n2g, sh2, sc2, rw, rb)


def _prep_up_group(w_ref, o_ref, t_scr, g):
    f = t_scr.shape[1] // 2
    cols = slice(g * LANES, (g + 1) * LANES)
    t_scr[g] = w_ref[0, 0, cols, :].T
    o_ref[0, 0:f, cols] = t_scr[g, pl.ds(0, f, stride=2), :].astype(BF16)
    o_ref[0, f:, cols] = t_scr[g, pl.ds(1, f, stride=2), :].astype(BF16)


def _prep_up_kernel(w_ref, o_ref, t_scr):
    for g in range(t_scr.shape[0]):
        _prep_up_group(w_ref, o_ref, t_scr, g)


def _prep_up(w_up, layer):
    _, e, d, f2 = w_up.shape
    groups = d // LANES
    kt = groups * LANES
    return pl.pallas_call(
        _prep_up_kernel,
        grid=(e, d // kt),
        in_specs=[pl.BlockSpec((1, 1, kt, f2), lambda i, j: (layer, i, j, 0))],
        out_specs=pl.BlockSpec((1, f2, kt), lambda i, j: (i, 0, j)),
        out_shape=jax.ShapeDtypeStruct((e, f2, d), BF16),
        scratch_shapes=[pltpu.VMEM((groups, f2, LANES), F32)],
        compiler_params=_cparams(("arbitrary", "arbitrary")),
        name="expert_up_weight_prep",
    )(w_up)


def _expert_kernel(te_ref, nu_ref, xs_ref, *refs):
    y_ref = refs[-1]
    nt = (((1,), (1,)), ((), ()))
    first_tile = pl.program_id(0) * EXPERT_GROUP

    @pl.when(first_tile < nu_ref[0])
    def _():
        for s in range(EXPERT_GROUP):
            wu_ref, bu_ref, wd_ref, bd_ref = refs[4 * s:4 * s + 4]
            rs = slice(s * EXPERT_TILE, (s + 1) * EXPERT_TILE)
            lo, hi = _unpack_pairs(jnp.concatenate([xs_ref[0, rs, :], xs_ref[1, rs, :]], axis=1))
            lo = lo.astype(BF16)
            hi = hi.astype(BF16)

            def up_half(r0):
                return (lax.dot_general(lo, wu_ref[0, r0:r0 + D_FF, :HALF], nt, preferred_element_type=F32)
                        + lax.dot_general(hi, wu_ref[0, r0:r0 + D_FF, HALF:], nt, preferred_element_type=F32))

            glu = jnp.minimum(up_half(0) + bu_ref[0, :, :D_FF], SWIGLU_LIMIT)
            lin = jnp.clip(up_half(D_FF) + bu_ref[0, :, D_FF:], -SWIGLU_LIMIT, SWIGLU_LIMIT)
            act = glu * _sigmoid(SWIGLU_ALPHA * glu) * (lin + 1.0)
            down = jnp.dot(act.astype(BF16), wd_ref[0], preferred_element_type=F32) + bd_ref[0]
            yp = _pack_pairs(down)
            y_ref[0, rs, :] = yp[:, :QUART]
            y_ref[1, rs, :] = yp[:, QUART:]

    @pl.when(first_tile >= nu_ref[0])
    def _():
        y_ref[...] = jnp.zeros(y_ref.shape, I32)


def _experts(tile_e, n_used, xs, w_up, b_up, w_down, b_down):
    r = xs.shape[1]
    tm = EXPERT_TILE * EXPERT_GROUP
    d = D_MODEL
    in_specs = [pl.BlockSpec((2, tm, QUART), lambda i, te, nu: (0, i, 0))]
    args = [xs]
    for s in range(EXPERT_GROUP):
        def pick(i, te, nu, s=s):
            return (te[i * EXPERT_GROUP + s], 0, 0)
        in_specs += [pl.BlockSpec((1, 2 * D_FF, d), pick), pl.BlockSpec((1, 1, 2 * D_FF), pick),
                     pl.BlockSpec((1, D_FF, d), pick), pl.BlockSpec((1, 1, d), pick)]
        args += [w_up, b_up, w_down, b_down]
    grid_spec = pltpu.PrefetchScalarGridSpec(
        num_scalar_prefetch=2,
        grid=(r // tm,),
        in_specs=in_specs,
        out_specs=pl.BlockSpec((2, tm, QUART), lambda i, te, nu: (0, i, 0)),
    )
    return pl.pallas_call(
        _expert_kernel,
        grid_spec=grid_spec,
        out_shape=jax.ShapeDtypeStruct((2, r, QUART), I32),
        compiler_params=_cparams(("arbitrary",)),
        name="expert_mlp",
    )(tile_e, n_used, *args)


def _combine_kernel(y_ref, tw_ref, x_ref, g2_ref, fg_ref, o_ref):
    xn = _moe_residual(y_ref, tw_ref, x_ref, g2_ref)
    r = lax.rsqrt(jnp.mean(xn * xn, axis=-1, keepdims=True) + NORM_EPS)
    o_ref[...] = xn * r * fg_ref[...]


def _combine(ytok, top_w, x2, g2, final_g, seq):
    n, d = x2.shape
    tm = min(ROW_TILE_OUT, seq)
    per_seq = seq // tm
    return pl.pallas_call(
        _combine_kernel,
        grid=(n // tm,),
        in_specs=[
            pl.BlockSpec((2, TOP_K, tm, QUART), lambda i: (0, 0, i, 0)),
            pl.BlockSpec((tm, LANES), lambda i: (i, 0)),
            pl.BlockSpec((tm, d), lambda i: (i, 0)),
            pl.BlockSpec((1, 1, d), lambda i: (i // per_seq, 0, 0)),
            pl.BlockSpec((1, d), lambda i: (0, 0)),
        ],
        out_specs=pl.BlockSpec((tm, d), lambda i: (i, 0)),
        out_shape=jax.ShapeDtypeStruct((n, d), F32),
        compiler_params=_cparams(("arbitrary",)),
        name="moe_combine",
    )(ytok, top_w, x2, g2, final_g)


def _route(top_idx, counts, n_rows):
    tm = EXPERT_TILE
    e = top_idx[:, :TOP_K]
    rank = top_idx[:, TOP_K:2 * TOP_K]
    sizes = counts[0, :N_EXPERTS]
    padded = ((sizes + tm - 1) // tm) * tm
    pend = jnp.cumsum(padded)
    pstart = pend - padded
    onehot = e[:, :, None] == jnp.arange(N_EXPERTS, dtype=I32)[None, None, :]
    pos = jnp.sum(jnp.where(onehot, pstart[None, None, :], 0), axis=-1) + rank
    tiles = jnp.arange(n_rows // tm, dtype=I32) * tm
    tile_e = jnp.minimum(jnp.sum(tiles[:, None] >= pend[None, :], axis=-1), N_EXPERTS - 1).astype(I32)
    n_used = (pend[-1] // tm).astype(I32).reshape(1)
    return pos.T.astype(I32), tile_e, n_used


def _sc_gather(table, idx):
    m = idx.shape[0]
    width = table.shape[1]
    assert m % (SC_WINDOW * SC_WORKERS) == 0
    mesh = plsc.VectorSubcoreMesh(core_axis_name="c", subcore_axis_name="s")

    @pl.kernel(out_type=jax.ShapeDtypeStruct((m, width), table.dtype), mesh=mesh)
    def gather_rows(t_hbm, i_hbm, o_hbm):
        def body(i_vmem, o_vmem):
            pltpu.sync_copy(t_hbm.at[i_vmem.at[0]], o_vmem)

        pltpu.emit_pipeline(
            body,
            grid=(m // SC_WINDOW,),
            in_specs=[pl.BlockSpec((1, SC_WINDOW), lambda i: (0, i))],
            out_specs=[pl.BlockSpec((SC_WINDOW, width), lambda i: (i, 0))],
            core_axis_name=("c", "s"),
            dimension_semantics=(pltpu.PARALLEL,),
        )(i_hbm, o_hbm)

    return gather_rows(table, idx.reshape(1, m))


def _sc_scatter(rows, idx, n_out):
    t, width = rows.shape
    picks = idx.shape[0]
    assert t % SC_WINDOW == 0
    mesh = plsc.VectorSubcoreMesh(core_axis_name="c", subcore_axis_name="s")

    @pl.kernel(out_type=jax.ShapeDtypeStruct((n_out, width), rows.dtype), mesh=mesh)
    def scatter_rows(x_hbm, i_hbm, o_hbm):
        def body(x_vmem, i_vmem):
            for k in range(picks):
                pltpu.sync_copy(x_vmem, o_hbm.at[i_vmem.at[k]])

        pltpu.emit_pipeline(
            body,
            grid=(t // SC_WINDOW,),
            in_specs=[pl.BlockSpec((SC_WINDOW, width), lambda i: (i, 0)),
                      pl.BlockSpec((picks, SC_WINDOW), lambda i: (0, i))],
            out_specs=[],
            core_axis_name=("c", "s"),
            dimension_semantics=(pltpu.PARALLEL,),
        )(x_hbm, i_hbm)

    return scatter_rows(rows, idx)


def _dispatch(h2p, pos_t, n_rows):
    n = h2p.shape[1]
    idx = jnp.concatenate([pos_t, pos_t + n_rows], axis=1)
    return _sc_scatter(h2p.reshape(2 * n, QUART), idx, 2 * n_rows).reshape(2, n_rows, QUART)


def _collect(y, pos_t):
    n_rows = y.shape[1]
    flat = pos_t.reshape(-1)
    out = _sc_gather(y.reshape(2 * n_rows, QUART), jnp.concatenate([flat, flat + n_rows]))
    return out.reshape(2, TOP_K, pos_t.shape[1], QUART)


def kernel(x, c, norm1_g, norm2_g, ada_w, ada_b, w_in, gate_b, fox_fb, lam_q1, lam_k1, lam_q2, lam_k2,
           diff_subln_g, w_branch, w_out, router_w, router_b, exp_w_up, exp_b_up, exp_w_down,
           exp_b_down, final_g):
    batch, seq, d = x.shape
    depth = ada_w.shape[0]
    n = batch * seq
    x2 = x.reshape(n, d)

    mod = _ada_mod(c, ada_w, ada_b)
    tables = _retention_tables(seq)
    n_rows = n * TOP_K + N_EXPERTS * EXPERT_TILE
    blk = BRANCH_WIDTH // LANES

    w_in_t = jnp.swapaxes(w_in, 1, 2)
    pending = None
    for l in range(depth):
        lam_init = 0.8 - 0.6 * math.exp(-0.3 * l)
        sh1, sc1, g1, sh2, sc2, g2 = [m.reshape(batch, 1, d) for m in jnp.split(mod[l], 6, axis=-1)]

        w_main, w_f = _prep_win(w_in_t, l)
        outs = _inproj(x2, norm1_g[l].reshape(1, d), sh1, sc1, w_main, w_f, seq, pending)
        proj, flog = outs[0], outs[1]
        if pending is not None:
            x2 = outs[2]

        fb_pad = jnp.pad(fox_fb[l], (0, LANES - FOX_HEADS)).reshape(1, LANES)
        frow = _fcum(flog, fb_pad, batch, seq).reshape(batch * blk, 2, seq)

        lamv = jnp.stack([lam_q1[l], lam_k1[l], lam_q2[l], lam_k2[l]])
        diff_extra = (lamv, diff_subln_g[l].reshape(1, LANES))
        if _rider_fits(batch, exp_w_up):
            o_diff, w_up = _attention(proj, "diff", 0, blk, 2 * blk, batch, seq, diff_extra, lam_init,
                                      rider=("up", exp_w_up, l))
        else:
            o_diff = _attention(proj, "diff", 0, blk, 2 * blk, batch, seq, diff_extra, lam_init)
            w_up = _prep_up(exp_w_up, l)
        o_ret = _retention(proj, tables, batch, seq)
        if _rider_fits(batch, exp_w_down):
            o_fox, w_down = _attention(proj, "fox", 7 * blk, 8 * blk, 9 * blk, batch, seq, (frow,),
                                       rider=("down", exp_w_down, l))
        else:
            o_fox = _attention(proj, "fox", 7 * blk, 8 * blk, 9 * blk, batch, seq, (frow,))
            w_down = exp_w_down[l].astype(BF16)

        rw = jnp.pad(router_w[l], ((0, 0), (0, LANES - N_EXPERTS)))
        rw_hi = rw.astype(BF16)
        rw = jnp.concatenate([rw_hi, (rw - rw_hi.astype(F32)).astype(BF16)], axis=1)
        rb = jnp.pad(router_b[l], (0, LANES - N_EXPERTS), constant_values=NEG_INF).reshape(1, LANES)
        x2, h2p, top_idx, top_w, counts = _merge(
            o_diff, o_ret, o_fox, proj, gate_b[l].reshape(N_BRANCHES, 1, d), w_branch[l].astype(BF16),
            w_out[l].astype(BF16), x2, g1, norm2_g[l].reshape(1, d), sh2, sc2, rw, rb, seq)

        pos_t, tile_e, n_used = _route(top_idx, counts, n_rows)
        xs = _dispatch(h2p, pos_t, n_rows)

        bu = exp_b_up[l]
        b_up = jnp.concatenate([bu[:, 0::2], bu[:, 1::2]], axis=-1).reshape(N_EXPERTS, 1, 2 * D_FF)
        y = _experts(tile_e, n_used, xs, w_up, b_up, w_down,
                     exp_b_down[l].reshape(N_EXPERTS, 1, d))
        pending = (_collect(y, pos_t), top_w, g2)

    ytok, top_w, g2 = pending
    x2 = _combine(ytok, top_w, x2, g2, final_g.reshape(1, d), seq)
    return x2.reshape(batch, seq, d)
```

```python
import functools
import math

import jax
import jax.numpy as jnp
from jax import lax
from jax.experimental import pallas as pl
from jax.experimental.pallas import tpu as pltpu
from jax.experimental.pallas import tpu_sc as plsc

F32 = jnp.float32
BF16 = jnp.bfloat16
I32 = jnp.int32
U32 = jnp.uint32

D_MODEL = 1024
HEAD_DIM = 64
BRANCH_WIDTH = D_MODEL // 2
N_BRANCHES = 3
FOX_HEADS = BRANCH_WIDTH // HEAD_DIM
RET_HEADS = 4
RET_HEAD_DIM = 128
ROPE_BASE = 10000.0
RET_GAMMA_EXP0 = 5.0
N_EXPERTS = 32
TOP_K = 4
D_FF = D_MODEL
SWIGLU_LIMIT = 7.0
SWIGLU_ALPHA = 1.702
NORM_EPS = 1e-5
NEG_INF = -1e30

LANES = 128
MAIN_COLS = 8192
GATE_COL0 = 10 * BRANCH_WIDTH
HALF = D_MODEL // 2
QUART = HALF // 2
SC_WINDOW = 128
SC_WORKERS = 32
VMEM_LIMIT = 48 * 1024 * 1024
VMEM_LIMIT_INPROJ = 56 * 1024 * 1024

ROW_TILE_IN = 512
COL_TILE_IN = 1024
ATTN_TILE = 256
RET_CHUNK = 256
ROW_TILE_OUT = 256
MERGE_TILE = 1024
MERGE_SUB = 512
EXPERT_TILE = 256
EXPERT_GROUP = 2
CUM_CHUNK = 256


def _cparams(sem):
    return pltpu.CompilerParams(dimension_semantics=sem, vmem_limit_bytes=VMEM_LIMIT)


def _sigmoid(x):
    return 0.5 * jnp.tanh(0.5 * x) + 0.5


def _pack_pairs(v):
    u = lax.bitcast_convert_type(v.astype(BF16).astype(F32), U32)
    half = v.shape[-1] // 2
    w = (u[:, half:] & jnp.uint32(0xFFFF0000)) | (u[:, :half] >> 16)
    return lax.bitcast_convert_type(w, I32)


def _unpack_pairs(w):
    u = lax.bitcast_convert_type(w, U32)
    lo = lax.bitcast_convert_type(u << 16, F32)
    hi = lax.bitcast_convert_type(u & jnp.uint32(0xFFFF0000), F32)
    return lo, hi


def _moe_residual(y_ref, tw_ref, x_ref, g2_ref):
    tw = tw_ref[...]
    parts = [None] * 4
    for k in range(TOP_K):
        w = tw[:, k:k + 1]
        for piece in range(2):
            lo, hi = _unpack_pairs(y_ref[piece, k])
            for slot, v in ((piece, lo), (2 + piece, hi)):
                parts[slot] = w * v if parts[slot] is None else parts[slot] + w * v
    return x_ref[...] + g2_ref[0] * jnp.concatenate(parts, axis=-1)


def _ada_kernel(c_ref, w_ref, b_ref, o_ref):
    c = c_ref[...]
    ca = c * _sigmoid(c)
    o_ref[0] = jnp.dot(ca, w_ref[0], preferred_element_type=F32) + b_ref[0]


def _ada_mod(c, ada_w, ada_b):
    depth, d, cols = ada_w.shape
    b = c.shape[0]
    tn = 1024
    return pl.pallas_call(
        _ada_kernel,
        grid=(depth, cols // tn),
        in_specs=[
            pl.BlockSpec((b, d), lambda l, j: (0, 0)),
            pl.BlockSpec((1, d, tn), lambda l, j: (l, 0, j)),
            pl.BlockSpec((1, 1, tn), lambda l, j: (l, 0, j)),
        ],
        out_specs=pl.BlockSpec((1, b, tn), lambda l, j: (l, 0, j)),
        out_shape=jax.ShapeDtypeStruct((depth, b, cols), F32),
        compiler_params=_cparams(("arbitrary", "arbitrary")),
        name="ada_mod",
    )(c, ada_w, ada_b.reshape(depth, 1, cols))


def _prep_win_kernel(w_ref, wm_ref, wf_ref):
    wm_ref[:GATE_COL0, :] = w_ref[0, :GATE_COL0, :].astype(BF16)
    wm_ref[GATE_COL0:, :] = w_ref[0, GATE_COL0 + FOX_HEADS:, :].astype(BF16)
    wf_ref[...] = jnp.zeros(wf_ref.shape, BF16)
    wf_ref[:FOX_HEADS, :] = w_ref[0, GATE_COL0:GATE_COL0 + FOX_HEADS, :].astype(BF16)


def _prep_win(w_t, layer):
    _, cols, d = w_t.shape
    ct = 256
    return pl.pallas_call(
        _prep_win_kernel,
        grid=(d // ct,),
        in_specs=[pl.BlockSpec((1, cols, ct), lambda i: (layer, 0, i))],
        out_specs=[pl.BlockSpec((MAIN_COLS, ct), lambda i: (0, i)), pl.BlockSpec((LANES, ct), lambda i: (0, i))],
        out_shape=[jax.ShapeDtypeStruct((MAIN_COLS, d), BF16), jax.ShapeDtypeStruct((LANES, d), BF16)],
        compiler_params=_cparams(("arbitrary",)),
        name="inproj_weight_prep",
    )(w_t)


def _inproj_kernel(x_ref, g_ref, sh_ref, sc_ref, w_ref, wf_ref, proj_ref, fl_ref):
    x = x_ref[...]
    r = lax.rsqrt(jnp.mean(x * x, axis=-1, keepdims=True) + NORM_EPS)
    h = x * r * g_ref[...]
    h = h * (1.0 + sc_ref[0]) + sh_ref[0]
    _project(h.astype(BF16), w_ref, wf_ref, proj_ref, fl_ref)


def _project(hb, w_ref, wf_ref, proj_ref, fl_ref, after_chunk=None):
    nt = (((1,), (1,)), ((), ()))
    fl_ref[...] = lax.dot_general(hb, wf_ref[...], nt, preferred_element_type=F32)
    for j in range(MAIN_COLS // COL_TILE_IN):
        cs = slice(j * COL_TILE_IN, (j + 1) * COL_TILE_IN)
        proj_ref[:, cs] = lax.dot_general(hb, w_ref[cs, :], nt, preferred_element_type=F32).astype(BF16)
        if after_chunk is not None:
            after_chunk(j)


def _inproj_moe_kernel(y_ref, tw_ref, x_ref, g2_ref, g_ref, sh_ref, sc_ref, w_ref, wf_ref,
                       proj_ref, fl_ref, xo_ref, h_scr):
    i = pl.program_id(0)

    @pl.when(i == 0)
    def _():
        h_scr[...] = jnp.zeros(h_scr.shape, BF16)

    span = h_scr.shape[1] // (MAIN_COLS // COL_TILE_IN)

    def normalise_slice(j):
        rs = slice(j * span, (j + 1) * span)
        x = _moe_residual(y_ref.at[:, :, rs, :], tw_ref.at[rs, :], x_ref.at[rs, :], g2_ref)
        xo_ref[rs, :] = x
        r = lax.rsqrt(jnp.mean(x * x, axis=-1, keepdims=True) + NORM_EPS)
        h = x * r * g_ref[...]
        h = h * (1.0 + sc_ref[0]) + sh_ref[0]
        h_scr[i % 2, rs, :] = h.astype(BF16)

    _project(h_scr[(i + 1) % 2], w_ref, wf_ref, proj_ref, fl_ref, after_chunk=normalise_slice)


def _inproj(x2, g, sh, sc, w_main, w_f, seq, moe=None):
    n, d = x2.shape
    tm = min(ROW_TILE_IN, seq)
    per_seq = seq // tm
    steps = n // tm
    last = steps - 1
    resident = pl.Buffered(1)
    weights = [pl.BlockSpec((MAIN_COLS, d), lambda i: (0, 0), pipeline_mode=resident),
               pl.BlockSpec((LANES, d), lambda i: (0, 0), pipeline_mode=resident)]
    out_shape = [jax.ShapeDtypeStruct((n, MAIN_COLS), BF16), jax.ShapeDtypeStruct((n, LANES), F32)]
    params = pltpu.CompilerParams(dimension_semantics=("arbitrary",), vmem_limit_bytes=VMEM_LIMIT_INPROJ)

    if moe is None:
        def per_b():
            return pl.BlockSpec((1, 1, d), lambda i: (i // per_seq, 0, 0))

        return pl.pallas_call(
            _inproj_kernel,
            grid=(steps,),
            in_specs=[pl.BlockSpec((tm, d), lambda i: (i, 0)), pl.BlockSpec((1, d), lambda i: (0, 0)),
                      per_b(), per_b()] + weights,
            out_specs=[pl.BlockSpec((tm, MAIN_COLS), lambda i: (i, 0)), pl.BlockSpec((tm, LANES), lambda i: (i, 0))],
            out_shape=out_shape,
            compiler_params=params,
            name="inproj",
        )(x2, g, sh, sc, w_main, w_f)

    def rows_in(w):
        return pl.BlockSpec((tm, w), lambda i: (jnp.minimum(i, last), 0))

    def rows_out(w):
        return pl.BlockSpec((tm, w), lambda i: (jnp.maximum(i - 1, 0), 0))

    def per_b():
        return pl.BlockSpec((1, 1, d), lambda i: (jnp.minimum(i, last) // per_seq, 0, 0))

    ytok, top_w, g2 = moe
    return pl.pallas_call(
        _inproj_moe_kernel,
        grid=(steps + 1,),
        in_specs=[pl.BlockSpec((2, TOP_K, tm, QUART), lambda i: (0, 0, jnp.minimum(i, last), 0)),
                  rows_in(LANES), rows_in(d), per_b(),
                  pl.BlockSpec((1, d), lambda i: (0, 0)), per_b(), per_b()] + weights,
        out_specs=[rows_out(MAIN_COLS), rows_out(LANES), rows_in(d)],
        out_shape=out_shape + [jax.ShapeDtypeStruct((n, d), F32)],
        scratch_shapes=[pltpu.VMEM((2, tm, d), BF16)],
        compiler_params=params,
        name="inproj",
    )(ytok, top_w, x2, g2, g, sh, sc, w_main, w_f)


def _split3(v):
    hi = v.astype(BF16)
    r1 = v - hi.astype(F32)
    mid = r1.astype(BF16)
    lo = (r1 - mid.astype(F32)).astype(BF16)
    return hi, mid, lo


def _fcum_kernel(fl_ref, fb_ref, o_ref, *, seq):
    c = min(CUM_CHUNK, seq)
    row = lax.broadcasted_iota(I32, (c, c), 0)
    col = lax.broadcasted_iota(I32, (c, c), 1)
    lower = (col <= row).astype(BF16)
    carry = jnp.zeros((1, LANES), F32)
    for ci in range(seq // c):
        x = fl_ref[pl.ds(ci * c, c), :] + fb_ref[...]
        lf = jnp.minimum(x, 0.0) - jnp.log(1.0 + jnp.exp(-jnp.abs(x)))
        hi, mid, lo = _split3(lf)
        cs = (jnp.dot(lower, hi, preferred_element_type=F32)
              + jnp.dot(lower, mid, preferred_element_type=F32)
              + jnp.dot(lower, lo, preferred_element_type=F32)) + carry
        o_ref[0, :, pl.ds(ci * c, c)] = cs.T[:FOX_HEADS, :]
        carry = cs[c - 1:c, :]


def _fcum(flog, fb_pad, batch, seq):
    return pl.pallas_call(
        functools.partial(_fcum_kernel, seq=seq),
        grid=(batch,),
        in_specs=[
            pl.BlockSpec((seq, LANES), lambda b: (b, 0)),
            pl.BlockSpec((1, LANES), lambda b: (0, 0)),
        ],
        out_specs=pl.BlockSpec((1, FOX_HEADS, seq), lambda b: (b, 0, 0)),
        out_shape=jax.ShapeDtypeStruct((batch, FOX_HEADS, seq), F32),
        compiler_params=_cparams(("arbitrary",)),
        name="fox_decay_cumsum",
    )(flog, fb_pad)


def _attn_kernel(*refs, mode, t, seq, lam_init, rider):
    if rider:
        refs = list(refs)
        wt_scr = refs.pop() if rider == "up" else None
        n_in = 5 if mode == "diff" else 4
        wraw_ref = refs.pop(n_in)
        wprep_ref = refs.pop(n_in + 1)
    if mode == "diff":
        q_ref, k_ref, v_ref, lam_ref, g_ref, o_ref, vx_scr = refs
        f_ref = None
        lv = lam_ref[...]
        lam = (jnp.exp(jnp.sum(lv[0:1, :] * lv[1:2, :])) - jnp.exp(jnp.sum(lv[2:3, :] * lv[3:4, :]))
               + lam_init)
    else:
        q_ref, k_ref, v_ref, f_ref, o_ref, vx_scr = refs
    lane = lax.broadcasted_iota(I32, (t, LANES), 1)
    row = lax.broadcasted_iota(I32, (t, t), 0)
    col = lax.broadcasted_iota(I32, (t, t), 1)
    nt = (((1,), (1,)), ((), ()))

    v_all = v_ref[...]
    one = jnp.ones_like(v_all)
    if mode == "diff":
        vx_scr[0] = jnp.concatenate([v_all, one], axis=1)
    else:
        lane_s = lax.broadcasted_iota(I32, v_all.shape, 1)
        vx_scr[0] = jnp.where(lane_s < HEAD_DIM, v_all, one)
        vx_scr[1] = jnp.where(lane_s >= HEAD_DIM, v_all, one)

    n_strips = seq // t
    pairs = [(i, n_strips - 1 - i) for i in range(n_strips // 2)] if n_strips % 2 == 0 else [(i,) for i in range(n_strips)]
    for gi, group in enumerate(pairs):
        if rider == "up":
            for g in range(gi, wt_scr.shape[0], len(pairs)):
                _prep_up_group(wraw_ref, wprep_ref, wt_scr, g)
        elif rider == "down":
            span = wprep_ref.shape[1] // len(pairs)
            rows = slice(gi * span, (gi + 1) * span)
            wprep_ref[0, rows, :] = wraw_ref[0, 0, rows, :].astype(BF16)
        units = [(qi, idx) for qi in group for idx in range(2)]
        qms = {}
        for qi in group:
            q = q_ref[qi * t:(qi + 1) * t, :] * jnp.asarray(HEAD_DIM ** -0.5, BF16)
            zero = jnp.zeros_like(q)
            qms[qi, 0] = jnp.where(lane < HEAD_DIM, q, zero)
            qms[qi, 1] = jnp.where(lane >= HEAD_DIM, q, zero)

        def scores(u, k0, k1):
            s_ = lax.dot_general(qms[u], k_ref[k0:k1, :], nt, preferred_element_type=F32)
            if f_ref is not None:
                s_ = s_ - f_ref[0, u[1]:u[1] + 1, k0:k1]
            return s_

        sds = {u: jnp.where(col <= row, scores(u, u[0] * t, (u[0] + 1) * t), NEG_INF) for u in units}
        sms = {u: scores(u, 0, u[0] * t) for u in units if u[0] > 0}
        ms = {}
        for u in units:
            m = jnp.max(sds[u], axis=-1, keepdims=True)
            if u in sms:
                m = jnp.maximum(m, jnp.max(sms[u], axis=-1, keepdims=True))
            ms[u] = m
        pds = {u: jnp.exp((sds[u] - ms[u]).astype(BF16)) for u in units}
        pms = {u: jnp.exp((sms[u] - ms[u]).astype(BF16)) for u in sms}
        outs = {}
        for u in units:
            vx = vx_scr.at[0 if mode == "diff" else u[1]]
            lo = u[0] * t
            acc = jnp.dot(pds[u], vx[lo:lo + t, :], preferred_element_type=F32)
            if u in pms:
                acc = acc + jnp.dot(pms[u], vx[0:lo, :], preferred_element_type=F32)
            outs[u] = acc
        for qi in group:
            o0, o1 = outs[qi, 0], outs[qi, 1]
            if mode == "diff":
                a1 = o0[:, :LANES] / o0[:, LANES:]
                a2 = o1[:, :LANES] / o1[:, LANES:]
                o = a1 - lam * a2
                r = lax.rsqrt(jnp.mean(o * o, axis=-1, keepdims=True) + NORM_EPS)
                o = o * r * g_ref[...] * (1.0 - lam_init)
            else:
                num = jnp.where(lane < HEAD_DIM, o0, o1)
                den = jnp.where(lane < HEAD_DIM, pltpu.roll(o0, HEAD_DIM, 1), pltpu.roll(o1, HEAD_DIM, 1))
                o = num / den
            o_ref[qi * t:(qi + 1) * t, :] = o.astype(BF16)


def _attention(proj, mode, col_q, col_k, col_v, batch, seq, extra, lam_init=0.0, rider=None):
    n = proj.shape[0]
    t = min(ATTN_TILE, seq)
    n_blk = BRANCH_WIDTH // LANES
    in_specs = [
        pl.BlockSpec((seq, LANES), lambda b, h: (b, col_q + h)),
        pl.BlockSpec((seq, LANES), lambda b, h: (b, col_k + h)),
        pl.BlockSpec((seq, LANES), lambda b, h: (b, col_v + h)),
    ]
    if mode == "diff":
        lamv, g = extra
        in_specs += [pl.BlockSpec((4, HEAD_DIM), lambda b, h: (0, 0)),
                     pl.BlockSpec((1, LANES), lambda b, h: (0, 0))]
        args = (proj, proj, proj, lamv, g)
    else:
        (frow,) = extra
        in_specs += [pl.BlockSpec((1, 2, seq), lambda b, h: (b * n_blk + h, 0, 0))]
        args = (proj, proj, proj, frow)
    out_specs = [pl.BlockSpec((seq, LANES), lambda b, h: (b, h))]
    out_shape = [jax.ShapeDtypeStruct((n, BRANCH_WIDTH), BF16)]
    scratch = [pltpu.VMEM((1, seq, 2 * LANES) if mode == "diff" else (2, seq, LANES), BF16)]
    if rider is not None:
        kind, w, layer = rider
        _, e, rows, cols = w.shape
        parts = batch * n_blk // e
        kt = rows // parts
        in_specs.append(pl.BlockSpec((1, 1, kt, cols), lambda b, h: (layer, (b * n_blk + h) // parts, (b * n_blk + h) % parts, 0)))
        args = args + (w,)
        if kind == "up":
            out_specs.append(pl.BlockSpec((1, cols, kt), lambda b, h: ((b * n_blk + h) // parts, 0, (b * n_blk + h) % parts)))
            out_shape.append(jax.ShapeDtypeStruct((e, cols, rows), BF16))
            scratch.append(pltpu.VMEM((kt // LANES, cols, LANES), F32))
        else:
            out_specs.append(pl.BlockSpec((1, kt, cols), lambda b, h: ((b * n_blk + h) // parts, (b * n_blk + h) % parts, 0)))
            out_shape.append(jax.ShapeDtypeStruct((e, rows, cols), BF16))
    outs = pl.pallas_call(
        functools.partial(_attn_kernel, mode=mode, t=t, seq=seq, lam_init=lam_init, rider=rider[0] if rider else None),
        grid=(batch, n_blk),
        in_specs=in_specs,
        out_specs=out_specs,
        out_shape=out_shape,
        scratch_shapes=scratch,
        compiler_params=_cparams(("arbitrary", "arbitrary")),
        name="attn_" + mode,
    )(*args)
    return outs if rider is not None else outs[0]


def _rider_fits(batch, w):
    _, e, rows, _ = w.shape
    steps = batch * (BRANCH_WIDTH // LANES)
    return steps % e == 0 and (rows // LANES) % (steps // e) == 0


def _ret_kernel(q_ref, k_ref, v_ref, g_ref, cos_ref, sin_ref, dm_ref, qd_ref, kd_ref, cd_ref,
                o_ref, *, seq, c):
    st = jnp.zeros((RET_HEAD_DIM, RET_HEAD_DIM), F32)
    half = RET_HEAD_DIM // 2

    for ci in range(seq // c):
        sl = pl.ds(ci * c, c)
        cos = cos_ref[sl, :]
        sin = sin_ref[sl, :]
        q = q_ref[sl, :].astype(F32)
        k = k_ref[sl, :].astype(F32)
        qr = q * cos + pltpu.roll(q, half, 1) * sin
        kr = (k * cos + pltpu.roll(k, half, 1) * sin) * (RET_HEAD_DIM ** -0.5)
        v = v_ref[sl, :]
        sc = lax.dot_general(qr.astype(BF16), kr.astype(BF16), (((1,), (1,)), ((), ())),
                             preferred_element_type=F32) * dm_ref[0]
        inner = jnp.dot(sc.astype(BF16), v, preferred_element_type=F32)
        cross = jnp.dot((qr * qd_ref[0]).astype(BF16), st.astype(BF16), preferred_element_type=F32)
        o = inner + cross
        kt = (kr * kd_ref[0]).T.astype(BF16)
        st = st * cd_ref[0] + jnp.dot(kt, v, preferred_element_type=F32)
        mu = jnp.mean(o, axis=-1, keepdims=True)
        oc = o - mu
        var = jnp.mean(oc * oc, axis=-1, keepdims=True)
        on = oc * lax.rsqrt(var + NORM_EPS)
        g = g_ref[sl, :].astype(F32)
        o_ref[sl, :] = (on * (g * _sigmoid(g))).astype(BF16)


def _retention(proj, tables, batch, seq):
    n = proj.shape[0]
    c = min(RET_CHUNK, seq)
    cos, sin, dm, qd, kd, cd = tables
    col = BRANCH_WIDTH * 3 // LANES
    step = BRANCH_WIDTH // LANES

    def pspec(k):
        return pl.BlockSpec((seq, LANES), lambda b, h: (b, col + k * step + h))

    return pl.pallas_call(
        functools.partial(_ret_kernel, seq=seq, c=c),
        grid=(batch, RET_HEADS),
        in_specs=[
            pspec(0), pspec(1), pspec(2), pspec(3),
            pl.BlockSpec((seq, LANES), lambda b, h: (0, 0)),
            pl.BlockSpec((seq, LANES), lambda b, h: (0, 0)),
            pl.BlockSpec((1, c, c), lambda b, h: (h, 0, 0)),
            pl.BlockSpec((1, c, LANES), lambda b, h: (h, 0, 0)),
            pl.BlockSpec((1, c, LANES), lambda b, h: (h, 0, 0)),
            pl.BlockSpec((1, LANES, LANES), lambda b, h: (h, 0, 0)),
        ],
        out_specs=pl.BlockSpec((seq, LANES), lambda b, h: (b, h)),
        out_shape=jax.ShapeDtypeStruct((n, BRANCH_WIDTH), BF16),
        compiler_params=_cparams(("arbitrary", "arbitrary")),
        name="retention",
    )(proj, proj, proj, proj, cos, sin, dm, qd, kd, cd)


def _retention_tables(seq):
    c = min(RET_CHUNK, seq)
    d = RET_HEAD_DIM
    inv = ROPE_BASE ** (-jnp.arange(0, d, 2, dtype=F32) / d)
    ang = jnp.arange(seq).astype(F32)[:, None] * inv[None, :]
    cos = jnp.cos(ang)
    sin = jnp.sin(ang)
    cos_full = jnp.concatenate([cos, cos], axis=-1)
    sin_signed = jnp.concatenate([-sin, sin], axis=-1)
    log_gamma = jnp.log1p(-jnp.exp2(-RET_GAMMA_EXP0 - jnp.arange(RET_HEADS, dtype=F32)))
    idx = jnp.arange(c, dtype=F32)
    dist = idx[:, None] - idx[None, :]
    dm = jnp.where(dist >= 0, jnp.exp(log_gamma[:, None, None] * jnp.maximum(dist, 0.0)), 0.0)
    qd = jnp.exp(log_gamma[:, None] * (idx + 1.0))
    kd = jnp.exp(log_gamma[:, None] * (c - 1.0 - idx))
    cd = jnp.exp(log_gamma * c)
    qd = jnp.broadcast_to(qd[:, :, None], (RET_HEADS, c, LANES))
    kd = jnp.broadcast_to(kd[:, :, None], (RET_HEADS, c, LANES))
    cd = jnp.broadcast_to(cd[:, None, None], (RET_HEADS, LANES, LANES))
    return cos_full, sin_signed, dm, qd, kd, cd


def _merge_kernel(od_ref, or_ref, of_ref, gd_ref, gr_ref, gf_ref, gb_ref, wb_ref, wo_ref, x_ref,
                  g1_ref, n2_ref, sh_ref, sc_ref, rw_ref, rb_ref,
                  xo_ref, hp_ref, ti_ref, tw_ref, cnt_ref, cnt_scr, *, tm, sub):
    @pl.when(pl.program_id(0) == 0)
    def _():
        cnt_scr[...] = jnp.zeros(cnt_scr.shape, F32)

    lane = lax.broadcasted_iota(I32, (sub, LANES), 1)
    r_i = lax.broadcasted_iota(I32, (sub, sub), 0)
    c_i = lax.broadcasted_iota(I32, (sub, sub), 1)
    earlier = (c_i < r_i).astype(BF16)
    counts = cnt_scr[...]

    blocks = [slice(s0, s0 + sub) for s0 in range(0, tm, sub)]

    def phase_branches(rs):
        merged = None
        for i, (o_ref, gate_ref) in enumerate(((od_ref, gd_ref), (or_ref, gr_ref), (of_ref, gf_ref))):
            gate = _sigmoid(gate_ref[rs, :] + gb_ref[i].astype(BF16))
            term = gate.astype(F32) * jnp.dot(o_ref[rs, :], wb_ref[i], preferred_element_type=F32)
            merged = term if merged is None else merged + term
        return merged.astype(BF16)

    def phase_outproj(rs, merged):
        mix = jnp.dot(merged, wo_ref[...], preferred_element_type=F32)
        xn = x_ref[rs, :] + g1_ref[0] * mix
        xo_ref[rs, :] = xn
        r = lax.rsqrt(jnp.mean(xn * xn, axis=-1, keepdims=True) + NORM_EPS)
        h = xn * r * n2_ref[...]
        h = h * (1.0 + sc_ref[0]) + sh_ref[0]
        hp = _pack_pairs(h)
        hp_ref[0, rs, :] = hp[:, :QUART]
        hp_ref[1, rs, :] = hp[:, QUART:]
        hh, hm, _ = _split3(h)
        both = jnp.dot(hh, rw_ref[...], preferred_element_type=F32)
        return (both[:, :LANES] + both[:, LANES:]
                + jnp.dot(hm, rw_ref[:, :LANES], preferred_element_type=F32)) + rb_ref[...]

    def phase_topk(rs, logits, counts):
        vals, idxs = [], []
        for _ in range(TOP_K):
            m = jnp.max(logits, axis=-1, keepdims=True)
            idx = jnp.min(jnp.where(logits == m, lane, LANES), axis=-1, keepdims=True)
            vals.append(m)
            idxs.append(idx)
            logits = jnp.where(lane == idx, -3.0e38, logits)
        es = [jnp.exp(v - vals[0]) for v in vals]
        den = es[0] + es[1] + es[2] + es[3]
        hits = [lane == idxs[k] for k in range(TOP_K)]
        picked = (hits[0] | hits[1] | hits[2] | hits[3])
        base = jnp.dot(earlier, picked.astype(BF16), preferred_element_type=F32) + counts
        counts = counts + jnp.sum(picked.astype(F32), axis=0, keepdims=True)
        ti = jnp.zeros((sub, LANES), I32)
        tw = jnp.zeros((sub, LANES), F32)
        for k in range(TOP_K):
            rank = jnp.sum(jnp.where(hits[k], base, 0.0), axis=-1, keepdims=True).astype(I32)
            ti = jnp.where(lane == k, idxs[k], ti)
            ti = jnp.where(lane == TOP_K + k, rank, ti)
            tw = jnp.where(lane == k, es[k] / den, tw)
        ti_ref[rs, :] = ti
        tw_ref[rs, :] = tw
        return counts

    merged = [phase_branches(rs) for rs in blocks]
    logits = [phase_outproj(rs, m) for rs, m in zip(blocks, merged)]
    for rs, lg in zip(blocks, logits):
        counts = phase_topk(rs, lg, counts)

    cnt_scr[...] = counts
    cnt_ref[...] = counts.astype(I32)


def _merge(o_diff, o_ret, o_fox, proj, gate_b, w_branch, w_out, x2, g1, n2g, sh2, sc2, rw, rb, seq):
    n, d = x2.shape
    tm = min(MERGE_TILE, seq)
    sub = min(MERGE_SUB, tm)
    per_seq = seq // tm
    gcol = GATE_COL0 // d

    def rows(w):
        return pl.BlockSpec((tm, w), lambda i: (i, 0))

    def gspec(k):
        return pl.BlockSpec((tm, d), lambda i: (i, gcol + k))

    def whole(shape):
        return pl.BlockSpec(shape, lambda i: (0,) * len(shape))

    def per_b():
        return pl.BlockSpec((1, 1, d), lambda i: (i // per_seq, 0, 0))

    return pl.pallas_call(
        functools.partial(_merge_kernel, tm=tm, sub=sub),
        grid=(n // tm,),
        in_specs=[
            rows(BRANCH_WIDTH), rows(BRANCH_WIDTH), rows(BRANCH_WIDTH),
            gspec(0), gspec(1), gspec(2),
            whole((N_BRANCHES, 1, d)), whole((N_BRANCHES, BRANCH_WIDTH, d)), whole((d, d)),
            rows(d), per_b(), whole((1, d)), per_b(), per_b(),
            whole((d, 2 * LANES)), whole((1, LANES)),
        ],
        out_specs=[rows(d), pl.BlockSpec((2, tm, QUART), lambda i: (0, i, 0)), rows(LANES), rows(LANES),
                   whole((1, LANES))],
        out_shape=[
            jax.ShapeDtypeStruct((n, d), F32),
            jax.ShapeDtypeStruct((2, n, QUART), I32),
            jax.ShapeDtypeStruct((n, LANES), I32),
            jax.ShapeDtypeStruct((n, LANES), F32),
            jax.ShapeDtypeStruct((1, LANES), I32),
        ],
        scratch_shapes=[pltpu.VMEM((1, LANES), F32)],
        compiler_params=_cparams(("arbitrary",)),
        name="merge_outproj_router",
    )(o_diff, o_ret, o_fox, proj, proj, proj, gate_b, w_branch, w_out, x2, g1, n2g, sh2, sc2, rw, rb)


def _prep_up_group(w_ref, o_ref, t_scr, g):
    f = t_scr.shape[1] // 2
    cols = slice(g * LANES, (g + 1) * LANES)
    t_scr[g] = w_ref[0, 0, cols, :].T
    o_ref[0, 0:f, cols] = t_scr[g, pl.ds(0, f, stride=2), :].astype(BF16)
    o_ref[0, f:, cols] = t_scr[g, pl.ds(1, f, stride=2), :].astype(BF16)


def _prep_up_kernel(w_ref, o_ref, t_scr):
    for g in range(t_scr.shape[0]):
        _prep_up_group(w_ref, o_ref, t_scr, g)


def _prep_up(w_up, layer):
    _, e, d, f2 = w_up.shape
    groups = d // LANES
    kt = groups * LANES
    return pl.pallas_call(
        _prep_up_kernel,
        grid=(e, d // kt),
        in_specs=[pl.BlockSpec((1, 1, kt, f2), lambda i, j: (layer, i, j, 0))],
        out_specs=pl.BlockSpec((1, f2, kt), lambda i, j: (i, 0, j)),
        out_shape=jax.ShapeDtypeStruct((e, f2, d), BF16),
        scratch_shapes=[pltpu.VMEM((groups, f2, LANES), F32)],
        compiler_params=_cparams(("arbitrary", "arbitrary")),
        name="expert_up_weight_prep",
    )(w_up)


def _expert_kernel(te_ref, nu_ref, xs_ref, *refs):
    y_ref = refs[-1]
    nt = (((1,), (1,)), ((), ()))
    first_tile = pl.program_id(0) * EXPERT_GROUP

    @pl.when(first_tile < nu_ref[0])
    def _():
        for s in range(EXPERT_GROUP):
            wu_ref, bu_ref, wd_ref, bd_ref = refs[4 * s:4 * s + 4]
            rs = slice(s * EXPERT_TILE, (s + 1) * EXPERT_TILE)
            lo, hi = _unpack_pairs(jnp.concatenate([xs_ref[0, rs, :], xs_ref[1, rs, :]], axis=1))
            lo = lo.astype(BF16)
            hi = hi.astype(BF16)

            def up_half(r0):
                return (lax.dot_general(lo, wu_ref[0, r0:r0 + D_FF, :HALF], nt, preferred_element_type=F32)
                        + lax.dot_general(hi, wu_ref[0, r0:r0 + D_FF, HALF:], nt, preferred_element_type=F32))

            glu = jnp.minimum(up_half(0) + bu_ref[0, :, :D_FF], SWIGLU_LIMIT)
            lin = jnp.clip(up_half(D_FF) + bu_ref[0, :, D_FF:], -SWIGLU_LIMIT, SWIGLU_LIMIT)
            act = glu * _sigmoid(SWIGLU_ALPHA * glu) * (lin + 1.0)
            down = jnp.dot(act.astype(BF16), wd_ref[0], preferred_element_type=F32) + bd_ref[0]
            yp = _pack_pairs(down)
            y_ref[0, rs, :] = yp[:, :QUART]
            y_ref[1, rs, :] = yp[:, QUART:]

    @pl.when(first_tile >= nu_ref[0])
    def _():
        y_ref[...] = jnp.zeros(y_ref.shape, I32)


def _experts(tile_e, n_used, xs, w_up, b_up, w_down, b_down):
    r = xs.shape[1]
    tm = EXPERT_TILE * EXPERT_GROUP
    d = D_MODEL
    in_specs = [pl.BlockSpec((2, tm, QUART), lambda i, te, nu: (0, i, 0))]
    args = [xs]
    for s in range(EXPERT_GROUP):
        def pick(i, te, nu, s=s):
            return (te[i * EXPERT_GROUP + s], 0, 0)
        in_specs += [pl.BlockSpec((1, 2 * D_FF, d), pick), pl.BlockSpec((1, 1, 2 * D_FF), pick),
                     pl.BlockSpec((1, D_FF, d), pick), pl.BlockSpec((1, 1, d), pick)]
        args += [w_up, b_up, w_down, b_down]
    grid_spec = pltpu.PrefetchScalarGridSpec(
        num_scalar_prefetch=2,
        grid=(r // tm,),
        in_specs=in_specs,
        out_specs=pl.BlockSpec((2, tm, QUART), lambda i, te, nu: (0, i, 0)),
    )
    return pl.pallas_call(
        _expert_kernel,
        grid_spec=grid_spec,
        out_shape=jax.ShapeDtypeStruct((2, r, QUART), I32),
        compiler_params=_cparams(("arbitrary",)),
        name="expert_mlp",
    )(tile_e, n_used, *args)


def _combine_kernel(y_ref, tw_ref, x_ref, g2_ref, fg_ref, o_ref):
    xn = _moe_residual(y_ref, tw_ref, x_ref, g2_ref)
    r = lax.rsqrt(jnp.mean(xn * xn, axis=-1, keepdims=True) + NORM_EPS)
    o_ref[...] = xn * r * fg_ref[...]


def _combine(ytok, top_w, x2, g2, final_g, seq):
    n, d = x2.shape
    tm = min(ROW_TILE_OUT, seq)
    per_seq = seq // tm
    return pl.pallas_call(
        _combine_kernel,
        grid=(n // tm,),
        in_specs=[
            pl.BlockSpec((2, TOP_K, tm, QUART), lambda i: (0, 0, i, 0)),
            pl.BlockSpec((tm, LANES), lambda i: (i, 0)),
            pl.BlockSpec((tm, d), lambda i: (i, 0)),
            pl.BlockSpec((1, 1, d), lambda i: (i // per_seq, 0, 0)),
            pl.BlockSpec((1, d), lambda i: (0, 0)),
        ],
        out_specs=pl.BlockSpec((tm, d), lambda i: (i, 0)),
        out_shape=jax.ShapeDtypeStruct((n, d), F32),
        compiler_params=_cparams(("arbitrary",)),
        name="moe_combine",
    )(ytok, top_w, x2, g2, final_g)


def _route(top_idx, counts, n_rows):
    tm = EXPERT_TILE
    e = top_idx[:, :TOP_K]
    rank = top_idx[:, TOP_K:2 * TOP_K]
    sizes = counts[0, :N_EXPERTS]
    padded = ((sizes + tm - 1) // tm) * tm
    pend = jnp.cumsum(padded)
    pstart = pend - padded
    onehot = e[:, :, None] == jnp.arange(N_EXPERTS, dtype=I32)[None, None, :]
    pos = jnp.sum(jnp.where(onehot, pstart[None, None, :], 0), axis=-1) + rank
    tiles = jnp.arange(n_rows // tm, dtype=I32) * tm
    tile_e = jnp.minimum(jnp.sum(tiles[:, None] >= pend[None, :], axis=-1), N_EXPERTS - 1).astype(I32)
    n_used = (pend[-1] // tm).astype(I32).reshape(1)
    return pos.T.astype(I32), tile_e, n_used


def _sc_gather(table, idx):
    m = idx.shape[0]
    width = table.shape[1]
    assert m % (SC_WINDOW * SC_WORKERS) == 0
    mesh = plsc.VectorSubcoreMesh(core_axis_name="c", subcore_axis_name="s")

    @pl.kernel(out_type=jax.ShapeDtypeStruct((m, width), table.dtype), mesh=mesh)
    def gather_rows(t_hbm, i_hbm, o_hbm):
        def body(i_vmem, o_vmem):
            pltpu.sync_copy(t_hbm.at[i_vmem.at[0]], o_vmem)

        pltpu.emit_pipeline(
            body,
            grid=(m // SC_WINDOW,),
            in_specs=[pl.BlockSpec((1, SC_WINDOW), lambda i: (0, i))],
            out_specs=[pl.BlockSpec((SC_WINDOW, width), lambda i: (i, 0))],
            core_axis_name=("c", "s"),
            dimension_semantics=(pltpu.PARALLEL,),
        )(i_hbm, o_hbm)

    return gather_rows(table, idx.reshape(1, m))


def _sc_scatter(rows, idx, n_out):
    t, width = rows.shape
    picks = idx.shape[0]
    assert t % SC_WINDOW == 0
    mesh = plsc.VectorSubcoreMesh(core_axis_name="c", subcore_axis_name="s")

    @pl.kernel(out_type=jax.ShapeDtypeStruct((n_out, width), rows.dtype), mesh=mesh)
    def scatter_rows(x_hbm, i_hbm, o_hbm):
        def body(x_vmem, i_vmem):
            for k in range(picks):
                pltpu.sync_copy(x_vmem, o_hbm.at[i_vmem.at[k]])

        pltpu.emit_pipeline(
            body,
            grid=(t // SC_WINDOW,),
            in_specs=[pl.BlockSpec((SC_WINDOW, width), lambda i: (i, 0)),
                      pl.BlockSpec((picks, SC_WINDOW), lambda i: (0, i))],
            out_specs=[],
            core_axis_name=("c", "s"),
            dimension_semantics=(pltpu.PARALLEL,),
        )(x_hbm, i_hbm)

    return scatter_rows(rows, idx)


def _dispatch(h2p, pos_t, n_rows):
    n = h2p.shape[1]
    idx = jnp.concatenate([pos_t, pos_t + n_rows], axis=1)
    return _sc_scatter(h2p.reshape(2 * n, QUART), idx, 2 * n_rows).reshape(2, n_rows, QUART)


def _collect(y, pos_t):
    n_rows = y.shape[1]
    flat = pos_t.reshape(-1)
    out = _sc_gather(y.reshape(2 * n_rows, QUART), jnp.concatenate([flat, flat + n_rows]))
    return out.reshape(2, TOP_K, pos_t.shape[1], QUART)


def kernel(x, c, norm1_g, norm2_g, ada_w, ada_b, w_in, gate_b, fox_fb, lam_q1, lam_k1, lam_q2, lam_k2,
           diff_subln_g, w_branch, w_out, router_w, router_b, exp_w_up, exp_b_up, exp_w_down,
           exp_b_down, final_g):
    batch, seq, d = x.shape
    depth = ada_w.shape[0]
    n = batch * seq
    x2 = x.reshape(n, d)

    mod = _ada_mod(c, ada_w, ada_b)
    tables = _retention_tables(seq)
    n_rows = n * TOP_K + N_EXPERTS * EXPERT_TILE
    blk = BRANCH_WIDTH // LANES

    w_in_t = jnp.swapaxes(w_in, 1, 2)
    pending = None
    for l in range(depth):
        lam_init = 0.8 - 0.6 * math.exp(-0.3 * l)
        sh1, sc1, g1, sh2, sc2, g2 = [m.reshape(batch, 1, d) for m in jnp.split(mod[l], 6, axis=-1)]

        w_main, w_f = _prep_win(w_in_t, l)
        outs = _inproj(x2, norm1_g[l].reshape(1, d), sh1, sc1, w_main, w_f, seq, pending)
        proj, flog = outs[0], outs[1]
        if pending is not None:
            x2 = outs[2]

        fb_pad = jnp.pad(fox_fb[l], (0, LANES - FOX_HEADS)).reshape(1, LANES)
        frow = _fcum(flog, fb_pad, batch, seq).reshape(batch * blk, 2, seq)

        lamv = jnp.stack([lam_q1[l], lam_k1[l], lam_q2[l], lam_k2[l]])
        diff_extra = (lamv, diff_subln_g[l].reshape(1, LANES))
        if _rider_fits(batch, exp_w_up):
            o_diff, w_up = _attention(proj, "diff", 0, blk, 2 * blk, batch, seq, diff_extra, lam_init,
                                      rider=("up", exp_w_up, l))
        else:
            o_diff = _attention(proj, "diff", 0, blk, 2 * blk, batch, seq, diff_extra, lam_init)
            w_up = _prep_up(exp_w_up, l)
        o_ret = _retention(proj, tables, batch, seq)
        if _rider_fits(batch, exp_w_down):
            o_fox, w_down = _attention(proj, "fox", 7 * blk, 8 * blk, 9 * blk, batch, seq, (frow,),
                                       rider=("down", exp_w_down, l))
        else:
            o_fox = _attention(proj, "fox", 7 * blk, 8 * blk, 9 * blk, batch, seq, (frow,))
            w_down = exp_w_down[l].astype(BF16)

        rw = jnp.pad(router_w[l], ((0, 0), (0, LANES - N_EXPERTS)))
        rw_hi = rw.astype(BF16)
        rw = jnp.concatenate([rw_hi, (rw - rw_hi.astype(F32)).astype(BF16)], axis=1)
        rb = jnp.pad(router_b[l], (0, LANES - N_EXPERTS), constant_values=NEG_INF).reshape(1, LANES)
        x2, h2p, top_idx, top_w, counts = _merge(
            o_diff, o_ret, o_fox, proj, gate_b[l].reshape(N_BRANCHES, 1, d), w_branch[l].astype(BF16),
            w_out[l].astype(BF16), x2, g1, norm2_g[l].reshape(1, d), sh2, sc2, rw, rb, seq)

        pos_t, tile_e, n_used = _route(top_idx, counts, n_rows)
        xs = _dispatch(h2p, pos_t, n_rows)

        bu = exp_b_up[l]
        b_up = jnp.concatenate([bu[:, 0::2], bu[:, 1::2]], axis=-1).reshape(N_EXPERTS, 1, 2 * D_FF)
        y = _experts(tile_e, n_used, xs, w_up, b_up, w_down,
                     exp_b_down[l].reshape(N_EXPERTS, 1, d))
        pending = (_collect(y, pos_t), top_w, g2)

    ytok, top_w, g2 = pending
    x2 = _combine(ytok, top_w, x2, g2, final_g.reshape(1, d), seq)
    return x2.reshape(batch, seq, d)
```

```python
import functools
import math

import jax
import jax.numpy as jnp
from jax import lax
from jax.experimental import pallas as pl
from jax.experimental.pallas import tpu as pltpu
from jax.experimental.pallas import tpu_sc as plsc

F32 = jnp.float32
BF16 = jnp.bfloat16
I32 = jnp.int32
U32 = jnp.uint32

D_MODEL = 1024
HEAD_DIM = 64
BRANCH_WIDTH = D_MODEL // 2
N_BRANCHES = 3
FOX_HEADS = BRANCH_WIDTH // HEAD_DIM
RET_HEADS = 4
RET_HEAD_DIM = 128
ROPE_BASE = 10000.0
RET_GAMMA_EXP0 = 5.0
N_EXPERTS = 32
TOP_K = 4
D_FF = D_MODEL
SWIGLU_LIMIT = 7.0
SWIGLU_ALPHA = 1.702
NORM_EPS = 1e-5
NEG_INF = -1e30

LANES = 128
MAIN_COLS = 8192
GATE_COL0 = 10 * BRANCH_WIDTH
HALF = D_MODEL // 2
QUART = HALF // 2
SC_WINDOW = 128
SC_WORKERS = 32
VMEM_LIMIT = 48 * 1024 * 1024
VMEM_LIMIT_INPROJ = 56 * 1024 * 1024

ROW_TILE_IN = 512
COL_TILE_IN = 1024
ATTN_TILE = 256
RET_CHUNK = 256
ROW_TILE_OUT = 256
MERGE_TILE = 1024
MERGE_SUB = 512
EXPERT_TILE = 256
EXPERT_GROUP = 2
CUM_CHUNK = 256


def _cparams(sem):
    return pltpu.CompilerParams(dimension_semantics=sem, vmem_limit_bytes=VMEM_LIMIT)


def _sigmoid(x):
    return 0.5 * jnp.tanh(0.5 * x) + 0.5


def _pack_pairs(v):
    u = lax.bitcast_convert_type(v.astype(BF16).astype(F32), U32)
    half = v.shape[-1] // 2
    w = (u[:, half:] & jnp.uint32(0xFFFF0000)) | (u[:, :half] >> 16)
    return lax.bitcast_convert_type(w, I32)


def _unpack_pairs(w):
    u = lax.bitcast_convert_type(w, U32)
    lo = lax.bitcast_convert_type(u << 16, F32)
    hi = lax.bitcast_convert_type(u & jnp.uint32(0xFFFF0000), F32)
    return lo, hi


def _moe_residual(y_ref, tw_ref, x_ref, g2_ref):
    tw = tw_ref[...]
    parts = [None] * 4
    for k in range(TOP_K):
        w = tw[:, k:k + 1]
        for piece in range(2):
            lo, hi = _unpack_pairs(y_ref[piece, k])
            for slot, v in ((piece, lo), (2 + piece, hi)):
                parts[slot] = w * v if parts[slot] is None else parts[slot] + w * v
    return x_ref[...] + g2_ref[0] * jnp.concatenate(parts, axis=-1)


def _ada_kernel(c_ref, w_ref, b_ref, o_ref):
    c = c_ref[...]
    ca = c * _sigmoid(c)
    o_ref[0] = jnp.dot(ca, w_ref[0], preferred_element_type=F32) + b_ref[0]


def _ada_mod(c, ada_w, ada_b):
    depth, d, cols = ada_w.shape
    b = c.shape[0]
    tn = 1024
    return pl.pallas_call(
        _ada_kernel,
        grid=(depth, cols // tn),
        in_specs=[
            pl.BlockSpec((b, d), lambda l, j: (0, 0)),
            pl.BlockSpec((1, d, tn), lambda l, j: (l, 0, j)),
            pl.BlockSpec((1, 1, tn), lambda l, j: (l, 0, j)),
        ],
        out_specs=pl.BlockSpec((1, b, tn), lambda l, j: (l, 0, j)),
        out_shape=jax.ShapeDtypeStruct((depth, b, cols), F32),
        compiler_params=_cparams(("arbitrary", "arbitrary")),
        name="ada_mod",
    )(c, ada_w, ada_b.reshape(depth, 1, cols))


def _prep_win_kernel(w_ref, wm_ref, wf_ref):
    wm_ref[:GATE_COL0, :] = w_ref[0, :GATE_COL0, :].astype(BF16)
    wm_ref[GATE_COL0:, :] = w_ref[0, GATE_COL0 + FOX_HEADS:, :].astype(BF16)
    wf_ref[...] = jnp.zeros(wf_ref.shape, BF16)
    wf_ref[:FOX_HEADS, :] = w_ref[0, GATE_COL0:GATE_COL0 + FOX_HEADS, :].astype(BF16)


def _prep_win(w_t, layer):
    _, cols, d = w_t.shape
    ct = 256
    return pl.pallas_call(
        _prep_win_kernel,
        grid=(d // ct,),
        in_specs=[pl.BlockSpec((1, cols, ct), lambda i: (layer, 0, i))],
        out_specs=[pl.BlockSpec((MAIN_COLS, ct), lambda i: (0, i)), pl.BlockSpec((LANES, ct), lambda i: (0, i))],
        out_shape=[jax.ShapeDtypeStruct((MAIN_COLS, d), BF16), jax.ShapeDtypeStruct((LANES, d), BF16)],
        compiler_params=_cparams(("arbitrary",)),
        name="inproj_weight_prep",
    )(w_t)


def _inproj_kernel(x_ref, g_ref, sh_ref, sc_ref, w_ref, wf_ref, proj_ref, fl_ref):
    x = x_ref[...]
    r = lax.rsqrt(jnp.mean(x * x, axis=-1, keepdims=True) + NORM_EPS)
    h = x * r * g_ref[...]
    h = h * (1.0 + sc_ref[0]) + sh_ref[0]
    _project(h.astype(BF16), w_ref, wf_ref, proj_ref, fl_ref)


def _project(hb, w_ref, wf_ref, proj_ref, fl_ref, after_chunk=None):
    nt = (((1,), (1,)), ((), ()))
    fl_ref[...] = lax.dot_general(hb, wf_ref[...], nt, preferred_element_type=F32)
    for j in range(MAIN_COLS // COL_TILE_IN):
        cs = slice(j * COL_TILE_IN, (j + 1) * COL_TILE_IN)
        proj_ref[:, cs] = lax.dot_general(hb, w_ref[cs, :], nt, preferred_element_type=F32).astype(BF16)
        if after_chunk is not None:
            after_chunk(j)


def _inproj_moe_kernel(y_ref, tw_ref, x_ref, g2_ref, g_ref, sh_ref, sc_ref, w_ref, wf_ref,
                       proj_ref, fl_ref, xo_ref, h_scr):
    i = pl.program_id(0)

    @pl.when(i == 0)
    def _():
        h_scr[...] = jnp.zeros(h_scr.shape, BF16)

    span = h_scr.shape[1] // (MAIN_COLS // COL_TILE_IN)

    def normalise_slice(j):
        rs = slice(j * span, (j + 1) * span)
        x = _moe_residual(y_ref.at[:, :, rs, :], tw_ref.at[rs, :], x_ref.at[rs, :], g2_ref)
        xo_ref[rs, :] = x
        r = lax.rsqrt(jnp.mean(x * x, axis=-1, keepdims=True) + NORM_EPS)
        h = x * r * g_ref[...]
        h = h * (1.0 + sc_ref[0]) + sh_ref[0]
        h_scr[i % 2, rs, :] = h.astype(BF16)

    _project(h_scr[(i + 1) % 2], w_ref, wf_ref, proj_ref, fl_ref, after_chunk=normalise_slice)


def _inproj(x2, g, sh, sc, w_main, w_f, seq, moe=None):
    n, d = x2.shape
    tm = min(ROW_TILE_IN, seq)
    per_seq = seq // tm
    steps = n // tm
    last = steps - 1
    resident = pl.Buffered(1)
    weights = [pl.BlockSpec((MAIN_COLS, d), lambda i: (0, 0), pipeline_mode=resident),
               pl.BlockSpec((LANES, d), lambda i: (0, 0), pipeline_mode=resident)]
    out_shape = [jax.ShapeDtypeStruct((n, MAIN_COLS), BF16), jax.ShapeDtypeStruct((n, LANES), F32)]
    params = pltpu.CompilerParams(dimension_semantics=("arbitrary",), vmem_limit_bytes=VMEM_LIMIT_INPROJ)

    if moe is None:
        def per_b():
            return pl.BlockSpec((1, 1, d), lambda i: (i // per_seq, 0, 0))

        return pl.pallas_call(
            _inproj_kernel,
            grid=(steps,),
            in_specs=[pl.BlockSpec((tm, d), lambda i: (i, 0)), pl.BlockSpec((1, d), lambda i: (0, 0)),
                      per_b(), per_b()] + weights,
            out_specs=[pl.BlockSpec((tm, MAIN_COLS), lambda i: (i, 0)), pl.BlockSpec((tm, LANES), lambda i: (i, 0))],
            out_shape=out_shape,
            compiler_params=params,
            name="inproj",
        )(x2, g, sh, sc, w_main, w_f)

    def rows_in(w):
        return pl.BlockSpec((tm, w), lambda i: (jnp.minimum(i, last), 0))

    def rows_out(w):
        return pl.BlockSpec((tm, w), lambda i: (jnp.maximum(i - 1, 0), 0))

    def per_b():
        return pl.BlockSpec((1, 1, d), lambda i: (jnp.minimum(i, last) // per_seq, 0, 0))

    ytok, top_w, g2 = moe
    return pl.pallas_call(
        _inproj_moe_kernel,
        grid=(steps + 1,),
        in_specs=[pl.BlockSpec((2, TOP_K, tm, QUART), lambda i: (0, 0, jnp.minimum(i, last), 0)),
                  rows_in(LANES), rows_in(d), per_b(),
                  pl.BlockSpec((1, d), lambda i: (0, 0)), per_b(), per_b()] + weights,
        out_specs=[rows_out(MAIN_COLS), rows_out(LANES), rows_in(d)],
        out_shape=out_shape + [jax.ShapeDtypeStruct((n, d), F32)],
        scratch_shapes=[pltpu.VMEM((2, tm, d), BF16)],
        compiler_params=params,
        name="inproj",
    )(ytok, top_w, x2, g2, g, sh, sc, w_main, w_f)


def _split3(v):
    hi = v.astype(BF16)
    r1 = v - hi.astype(F32)
    mid = r1.astype(BF16)
    lo = (r1 - mid.astype(F32)).astype(BF16)
    return hi, mid, lo


def _fcum_kernel(fl_ref, fb_ref, o_ref, *, seq):
    c = min(CUM_CHUNK, seq)
    row = lax.broadcasted_iota(I32, (c, c), 0)
    col = lax.broadcasted_iota(I32, (c, c), 1)
    lower = (col <= row).astype(BF16)
    carry = jnp.zeros((1, LANES), F32)
    for ci in range(seq // c):
        x = fl_ref[pl.ds(ci * c, c), :] + fb_ref[...]
        lf = jnp.minimum(x, 0.0) - jnp.log(1.0 + jnp.exp(-jnp.abs(x)))
        hi, mid, lo = _split3(lf)
        cs = (jnp.dot(lower, hi, preferred_element_type=F32)
              + jnp.dot(lower, mid, preferred_element_type=F32)
              + jnp.dot(lower, lo, preferred_element_type=F32)) + carry
        o_ref[0, :, pl.ds(ci * c, c)] = cs.T[:FOX_HEADS, :]
        carry = cs[c - 1:c, :]


def _fcum(flog, fb_pad, batch, seq):
    return pl.pallas_call(
        functools.partial(_fcum_kernel, seq=seq),
        grid=(batch,),
        in_specs=[
            pl.BlockSpec((seq, LANES), lambda b: (b, 0)),
            pl.BlockSpec((1, LANES), lambda b: (0, 0)),
        ],
        out_specs=pl.BlockSpec((1, FOX_HEADS, seq), lambda b: (b, 0, 0)),
        out_shape=jax.ShapeDtypeStruct((batch, FOX_HEADS, seq), F32),
        compiler_params=_cparams(("arbitrary",)),
        name="fox_decay_cumsum",
    )(flog, fb_pad)


def _attn_kernel(*refs, mode, t, seq, lam_init, rider):
    if rider:
        refs = list(refs)
        wt_scr = refs.pop() if rider == "up" else None
        n_in = 5 if mode == "diff" else 4
        wraw_ref = refs.pop(n_in)
        wprep_ref = refs.pop(n_in + 1)
    if mode == "diff":
        q_ref, k_ref, v_ref, lam_ref, g_ref, o_ref, vx_scr = refs
        f_ref = None
        lv = lam_ref[...]
        lam = (jnp.exp(jnp.sum(lv[0:1, :] * lv[1:2, :])) - jnp.exp(jnp.sum(lv[2:3, :] * lv[3:4, :]))
               + lam_init)
    else:
        q_ref, k_ref, v_ref, f_ref, o_ref, vx_scr = refs
    lane = lax.broadcasted_iota(I32, (t, LANES), 1)
    row = lax.broadcasted_iota(I32, (t, t), 0)
    col = lax.broadcasted_iota(I32, (t, t), 1)
    nt = (((1,), (1,)), ((), ()))

    v_all = v_ref[...]
    one = jnp.ones_like(v_all)
    if mode == "diff":
        vx_scr[0] = jnp.concatenate([v_all, one], axis=1)
    else:
        lane_s = lax.broadcasted_iota(I32, v_all.shape, 1)
        vx_scr[0] = jnp.where(lane_s < HEAD_DIM, v_all, one)
        vx_scr[1] = jnp.where(lane_s >= HEAD_DIM, v_all, one)

    n_strips = seq // t
    pairs = [(i, n_strips - 1 - i) for i in range(n_strips // 2)] if n_strips % 2 == 0 else [(i,) for i in range(n_strips)]
    for gi, group in enumerate(pairs):
        if rider == "up":
            for g in range(gi, wt_scr.shape[0], len(pairs)):
                _prep_up_group(wraw_ref, wprep_ref, wt_scr, g)
        elif rider == "down":
            span = wprep_ref.shape[1] // len(pairs)
            rows = slice(gi * span, (gi + 1) * span)
            wprep_ref[0, rows, :] = wraw_ref[0, 0, rows, :].astype(BF16)
        units = [(qi, idx) for qi in group for idx in range(2)]
        qms = {}
        for qi in group:
            q = q_ref[qi * t:(qi + 1) * t, :] * jnp.asarray(HEAD_DIM ** -0.5, BF16)
            zero = jnp.zeros_like(q)
            qms[qi, 0] = jnp.where(lane < HEAD_DIM, q, zero)
            qms[qi, 1] = jnp.where(lane >= HEAD_DIM, q, zero)

        def scores(u, k0, k1):
            s_ = lax.dot_general(qms[u], k_ref[k0:k1, :], nt, preferred_element_type=F32)
            if f_ref is not None:
                s_ = s_ - f_ref[0, u[1]:u[1] + 1, k0:k1]
            return s_

        sds = {u: jnp.where(col <= row, scores(u, u[0] * t, (u[0] + 1) * t), NEG_INF) for u in units}
        sms = {u: scores(u, 0, u[0] * t) for u in units if u[0] > 0}
        ms = {}
        for u in units:
            m = jnp.max(sds[u], axis=-1, keepdims=True)
            if u in sms:
                m = jnp.maximum(m, jnp.max(sms[u], axis=-1, keepdims=True))
            ms[u] = m
        pds = {u: jnp.exp((sds[u] - ms[u]).astype(BF16)) for u in units}
        pms = {u: jnp.exp((sms[u] - ms[u]).astype(BF16)) for u in sms}
        outs = {}
        for u in units:
            vx = vx_scr.at[0 if mode == "diff" else u[1]]
            lo = u[0] * t
            acc = jnp.dot(pds[u], vx[lo:lo + t, :], preferred_element_type=F32)
            if u in pms:
                acc = acc + jnp.dot(pms[u], vx[0:lo, :], preferred_element_type=F32)
            outs[u] = acc
        for qi in group:
            o0, o1 = outs[qi, 0], outs[qi, 1]
            if mode == "diff":
                a1 = o0[:, :LANES] / o0[:, LANES:]
                a2 = o1[:, :LANES] / o1[:, LANES:]
                o = a1 - lam * a2
                r = lax.rsqrt(jnp.mean(o * o, axis=-1, keepdims=True) + NORM_EPS)
                o = o * r * g_ref[...] * (1.0 - lam_init)
            else:
                num = jnp.where(lane < HEAD_DIM, o0, o1)
                den = jnp.where(lane < HEAD_DIM, pltpu.roll(o0, HEAD_DIM, 1), pltpu.roll(o1, HEAD_DIM, 1))
                o = num / den
            o_ref[qi * t:(qi + 1) * t, :] = o.astype(BF16)


def _attention(proj, mode, col_q, col_k, col_v, batch, seq, extra, lam_init=0.0, rider=None):
    n = proj.shape[0]
    t = min(ATTN_TILE, seq)
    n_blk = BRANCH_WIDTH // LANES
    in_specs = [
        pl.BlockSpec((seq, LANES), lambda b, h: (b, col_q + h)),
        pl.BlockSpec((seq, LANES), lambda b, h: (b, col_k + h)),
        pl.BlockSpec((seq, LANES), lambda b, h: (b, col_v + h)),
    ]
    if mode == "diff":
        lamv, g = extra
        in_specs += [pl.BlockSpec((4, HEAD_DIM), lambda b, h: (0, 0)),
                     pl.BlockSpec((1, LANES), lambda b, h: (0, 0))]
        args = (proj, proj, proj, lamv, g)
    else:
        (frow,) = extra
        in_specs += [pl.BlockSpec((1, 2, seq), lambda b, h: (b * n_blk + h, 0, 0))]
        args = (proj, proj, proj, frow)
    out_specs = [pl.BlockSpec((seq, LANES), lambda b, h: (b, h))]
    out_shape = [jax.ShapeDtypeStruct((n, BRANCH_WIDTH), BF16)]
    scratch = [pltpu.VMEM((1, seq, 2 * LANES) if mode == "diff" else (2, seq, LANES), BF16)]
    if rider is not None:
        kind, w, layer = rider
        _, e, rows, cols = w.shape
        parts = batch * n_blk // e
        kt = rows // parts
        in_specs.append(pl.BlockSpec((1, 1, kt, cols), lambda b, h: (layer, (b * n_blk + h) // parts, (b * n_blk + h) % parts, 0)))
        args = args + (w,)
        if kind == "up":
            out_specs.append(pl.BlockSpec((1, cols, kt), lambda b, h: ((b * n_blk + h) // parts, 0, (b * n_blk + h) % parts)))
            out_shape.append(jax.ShapeDtypeStruct((e, cols, rows), BF16))
            scratch.append(pltpu.VMEM((kt // LANES, cols, LANES), F32))
        else:
            out_specs.append(pl.BlockSpec((1, kt, cols), lambda b, h: ((b * n_blk + h) // parts, (b * n_blk + h) % parts, 0)))
            out_shape.append(jax.ShapeDtypeStruct((e, rows, cols), BF16))
    outs = pl.pallas_call(
        functools.partial(_attn_kernel, mode=mode, t=t, seq=seq, lam_init=lam_init, rider=rider[0] if rider else None),
        grid=(batch, n_blk),
        in_specs=in_specs,
        out_specs=out_specs,
        out_shape=out_shape,
        scratch_shapes=scratch,
        compiler_params=_cparams(("arbitrary", "arbitrary")),
        name="attn_" + mode,
    )(*args)
    return outs if rider is not None else outs[0]


def _rider_fits(batch, w):
    _, e, rows, _ = w.shape
    steps = batch * (BRANCH_WIDTH // LANES)
    return steps % e == 0 and (rows // LANES) % (steps // e) == 0


def _ret_kernel(q_ref, k_ref, v_ref, g_ref, cos_ref, sin_ref, dm_ref, qd_ref, kd_ref, cd_ref,
                o_ref, *, seq, c):
    st = jnp.zeros((RET_HEAD_DIM, RET_HEAD_DIM), F32)
    half = RET_HEAD_DIM // 2

    n_chunks = seq // c

    def prepare(ci):
        sl = pl.ds(ci * c, c)
        cos = cos_ref[sl, :]
        sin = sin_ref[sl, :]
        q = q_ref[sl, :].astype(F32)
        k = k_ref[sl, :].astype(F32)
        qr = q * cos + pltpu.roll(q, half, 1) * sin
        kr = (k * cos + pltpu.roll(k, half, 1) * sin) * (RET_HEAD_DIM ** -0.5)
        sc = lax.dot_general(qr.astype(BF16), kr.astype(BF16), (((1,), (1,)), ((), ())),
                             preferred_element_type=F32) * dm_ref[0]
        inner = jnp.dot(sc.astype(BF16), v_ref[sl, :], preferred_element_type=F32)
        return inner, (qr * qd_ref[0]).astype(BF16), (kr * kd_ref[0]).T.astype(BF16)

    def finish(ci, o):
        sl = pl.ds(ci * c, c)
        mu = jnp.mean(o, axis=-1, keepdims=True)
        oc = o - mu
        var = jnp.mean(oc * oc, axis=-1, keepdims=True)
        on = oc * lax.rsqrt(var + NORM_EPS)
        g = g_ref[sl, :].astype(F32)
        o_ref[sl, :] = (on * (g * _sigmoid(g))).astype(BF16)

    for c0 in range(0, n_chunks, 2):
        group = list(range(c0, min(c0 + 2, n_chunks)))
        prepared = [prepare(ci) for ci in group]
        outs = []
        for ci, (inner, qdec, kdec_t) in zip(group, prepared):
            outs.append(inner + jnp.dot(qdec, st.astype(BF16), preferred_element_type=F32))
            st = st * cd_ref[0] + jnp.dot(kdec_t, v_ref[pl.ds(ci * c, c), :], preferred_element_type=F32)
        for ci, o in zip(group, outs):
            finish(ci, o)


def _retention(proj, tables, batch, seq):
    n = proj.shape[0]
    c = min(RET_CHUNK, seq)
    cos, sin, dm, qd, kd, cd = tables
    col = BRANCH_WIDTH * 3 // LANES
    step = BRANCH_WIDTH // LANES

    def pspec(k):
        return pl.BlockSpec((seq, LANES), lambda b, h: (b, col + k * step + h))

    return pl.pallas_call(
        functools.partial(_ret_kernel, seq=seq, c=c),
        grid=(batch, RET_HEADS),
        in_specs=[
            pspec(0), pspec(1), pspec(2), pspec(3),
            pl.BlockSpec((seq, LANES), lambda b, h: (0, 0)),
            pl.BlockSpec((seq, LANES), lambda b, h: (0, 0)),
            pl.BlockSpec((1, c, c), lambda b, h: (h, 0, 0)),
            pl.BlockSpec((1, c, LANES), lambda b, h: (h, 0, 0)),
            pl.BlockSpec((1, c, LANES), lambda b, h: (h, 0, 0)),
            pl.BlockSpec((1, LANES, LANES), lambda b, h: (h, 0, 0)),
        ],
        out_specs=pl.BlockSpec((seq, LANES), lambda b, h: (b, h)),
        out_shape=jax.ShapeDtypeStruct((n, BRANCH_WIDTH), BF16),
        compiler_params=_cparams(("arbitrary", "arbitrary")),
        name="retention",
    )(proj, proj, proj, proj, cos, sin, dm, qd, kd, cd)


def _retention_tables(seq):
    c = min(RET_CHUNK, seq)
    d = RET_HEAD_DIM
    inv = ROPE_BASE ** (-jnp.arange(0, d, 2, dtype=F32) / d)
    ang = jnp.arange(seq).astype(F32)[:, None] * inv[None, :]
    cos = jnp.cos(ang)
    sin = jnp.sin(ang)
    cos_full = jnp.concatenate([cos, cos], axis=-1)
    sin_signed = jnp.concatenate([-sin, sin], axis=-1)
    log_gamma = jnp.log1p(-jnp.exp2(-RET_GAMMA_EXP0 - jnp.arange(RET_HEADS, dtype=F32)))
    idx = jnp.arange(c, dtype=F32)
    dist = idx[:, None] - idx[None, :]
    dm = jnp.where(dist >= 0, jnp.exp(log_gamma[:, None, None] * jnp.maximum(dist, 0.0)), 0.0)
    qd = jnp.exp(log_gamma[:, None] * (idx + 1.0))
    kd = jnp.exp(log_gamma[:, None] * (c - 1.0 - idx))
    cd = jnp.exp(log_gamma * c)
    qd = jnp.broadcast_to(qd[:, :, None], (RET_HEADS, c, LANES))
    kd = jnp.broadcast_to(kd[:, :, None], (RET_HEADS, c, LANES))
    cd = jnp.broadcast_to(cd[:, None, None], (RET_HEADS, LANES, LANES))
    return cos_full, sin_signed, dm, qd, kd, cd


def _merge_kernel(od_ref, or_ref, of_ref, gd_ref, gr_ref, gf_ref, gb_ref, wb_ref, wo_ref, x_ref,
                  g1_ref, n2_ref, sh_ref, sc_ref, rw_ref, rb_ref,
                  xo_ref, hp_ref, ti_ref, tw_ref, cnt_ref, cnt_scr, *, tm, sub):
    @pl.when(pl.program_id(0) == 0)
    def _():
        cnt_scr[...] = jnp.zeros(cnt_scr.shape, F32)

    lane = lax.broadcasted_iota(I32, (sub, LANES), 1)
    r_i = lax.broadcasted_iota(I32, (sub, sub), 0)
    c_i = lax.broadcasted_iota(I32, (sub, sub), 1)
    earlier = (c_i < r_i).astype(BF16)
    counts = cnt_scr[...]

    blocks = [slice(s0, s0 + sub) for s0 in range(0, tm, sub)]

    def phase_branches(rs):
        merged = None
        for i, (o_ref, gate_ref) in enumerate(((od_ref, gd_ref), (or_ref, gr_ref), (of_ref, gf_ref))):
            gate = _sigmoid(gate_ref[rs, :] + gb_ref[i].astype(BF16))
            term = gate.astype(F32) * jnp.dot(o_ref[rs, :], wb_ref[i], preferred_element_type=F32)
            merged = term if merged is None else merged + term
        return merged.astype(BF16)

    def phase_outproj(rs, merged):
        mix = jnp.dot(merged, wo_ref[...], preferred_element_type=F32)
        xn = x_ref[rs, :] + g1_ref[0] * mix
        xo_ref[rs, :] = xn
        r = lax.rsqrt(jnp.mean(xn * xn, axis=-1, keepdims=True) + NORM_EPS)
        h = xn * r * n2_ref[...]
        h = h * (1.0 + sc_ref[0]) + sh_ref[0]
        hp = _pack_pairs(h)
        hp_ref[0, rs, :] = hp[:, :QUART]
        hp_ref[1, rs, :] = hp[:, QUART:]
        hh, hm, _ = _split3(h)
        both = jnp.dot(hh, rw_ref[...], preferred_element_type=F32)
        return (both[:, :LANES] + both[:, LANES:]
                + jnp.dot(hm, rw_ref[:, :LANES], preferred_element_type=F32)) + rb_ref[...]

    def phase_topk(rs, logits, counts):
        vals, idxs = [], []
        for _ in range(TOP_K):
            m = jnp.max(logits, axis=-1, keepdims=True)
            idx = jnp.min(jnp.where(logits == m, lane, LANES), axis=-1, keepdims=True)
            vals.append(m)
            idxs.append(idx)
            logits = jnp.where(lane == idx, -3.0e38, logits)
        es = [jnp.exp(v - vals[0]) for v in vals]
        den = es[0] + es[1] + es[2] + es[3]
        hits = [lane == idxs[k] for k in range(TOP_K)]
        picked = (hits[0] | hits[1] | hits[2] | hits[3])
        base = jnp.dot(earlier, picked.astype(BF16), preferred_element_type=F32) + counts
        counts = counts + jnp.sum(picked.astype(F32), axis=0, keepdims=True)
        ti = jnp.zeros((sub, LANES), I32)
        tw = jnp.zeros((sub, LANES), F32)
        for k in range(TOP_K):
            rank = jnp.sum(jnp.where(hits[k], base, 0.0), axis=-1, keepdims=True).astype(I32)
            ti = jnp.where(lane == k, idxs[k], ti)
            ti = jnp.where(lane == TOP_K + k, rank, ti)
            tw = jnp.where(lane == k, es[k] / den, tw)
        ti_ref[rs, :] = ti
        tw_ref[rs, :] = tw
        return counts

    merged = [phase_branches(rs) for rs in blocks]
    logits = [phase_outproj(rs, m) for rs, m in zip(blocks, merged)]
    for rs, lg in zip(blocks, logits):
        counts = phase_topk(rs, lg, counts)

    cnt_scr[...] = counts
    cnt_ref[...] = counts.astype(I32)


def _merge(o_diff, o_ret, o_fox, proj, gate_b, w_branch, w_out, x2, g1, n2g, sh2, sc2, rw, rb, seq):
    n, d = x2.shape
    tm = min(MERGE_TILE, seq)
    sub = min(MERGE_SUB, tm)
    per_seq = seq // tm
    gcol = GATE_COL0 // d

    def rows(w):
        return pl.BlockSpec((tm, w), lambda i: (i, 0))

    def gspec(k):
        return pl.BlockSpec((tm, d), lambda i: (i, gcol + k))

    def whole(shape):
        return pl.BlockSpec(shape, lambda i: (0,) * len(shape))

    def per_b():
        return pl.BlockSpec((1, 1, d), lambda i: (i // per_seq, 0, 0))

    return pl.pallas_call(
        functools.partial(_merge_kernel, tm=tm, sub=sub),
        grid=(n // tm,),
        in_specs=[
            rows(BRANCH_WIDTH), rows(BRANCH_WIDTH), rows(BRANCH_WIDTH),
            gspec(0), gspec(1), gspec(2),
            whole((N_BRANCHES, 1, d)), whole((N_BRANCHES, BRANCH_WIDTH, d)), whole((d, d)),
            rows(d), per_b(), whole((1, d)), per_b(), per_b(),
            whole((d, 2 * LANES)), whole((1, LANES)),
        ],
        out_specs=[rows(d), pl.BlockSpec((2, tm, QUART), lambda i: (0, i, 0)), rows(LANES), rows(LANES),
                   whole((1, LANES))],
        out_shape=[
            jax.ShapeDtypeStruct((n, d), F32),
            jax.ShapeDtypeStruct((2, n, QUART), I32),
            jax.ShapeDtypeStruct((n, LANES), I32),
            jax.ShapeDtypeStruct((n, LANES), F32),
            jax.ShapeDtypeStruct((1, LANES), I32),
        ],
        scratch_shapes=[pltpu.VMEM((1, LANES), F32)],
        compiler_params=_cparams(("arbitrary",)),
        name="merge_outproj_router",
    )(o_diff, o_ret, o_fox, proj, proj, proj, gate_b, w_branch, w_out, x2, g1, n2g, sh2, sc2, rw, rb)


def _prep_up_group(w_ref, o_ref, t_scr, g):
    f = t_scr.shape[1] // 2
    cols = slice(g * LANES, (g + 1) * LANES)
    t_scr[g] = w_ref[0, 0, cols, :].T
    o_ref[0, 0:f, cols] = t_scr[g, pl.ds(0, f, stride=2), :].astype(BF16)
    o_ref[0, f:, cols] = t_scr[g, pl.ds(1, f, stride=2), :].astype(BF16)


def _prep_up_kernel(w_ref, o_ref, t_scr):
    for g in range(t_scr.shape[0]):
        _prep_up_group(w_ref, o_ref, t_scr, g)


def _prep_up(w_up, layer):
    _, e, d, f2 = w_up.shape
    groups = d // LANES
    kt = groups * LANES
    return pl.pallas_call(
        _prep_up_kernel,
        grid=(e, d // kt),
        in_specs=[pl.BlockSpec((1, 1, kt, f2), lambda i, j: (layer, i, j, 0))],
        out_specs=pl.BlockSpec((1, f2, kt), lambda i, j: (i, 0, j)),
        out_shape=jax.ShapeDtypeStruct((e, f2, d), BF16),
        scratch_shapes=[pltpu.VMEM((groups, f2, LANES), F32)],
        compiler_params=_cparams(("arbitrary", "arbitrary")),
        name="expert_up_weight_prep",
    )(w_up)


def _expert_kernel(te_ref, nu_ref, xs_ref, *refs):
    y_ref = refs[-1]
    nt = (((1,), (1,)), ((), ()))
    first_tile = pl.program_id(0) * EXPERT_GROUP

    @pl.when(first_tile < nu_ref[0])
    def _():
        for s in range(EXPERT_GROUP):
            wu_ref, bu_ref, wd_ref, bd_ref = refs[4 * s:4 * s + 4]
            rs = slice(s * EXPERT_TILE, (s + 1) * EXPERT_TILE)
            lo, hi = _unpack_pairs(jnp.concatenate([xs_ref[0, rs, :], xs_ref[1, rs, :]], axis=1))
            lo = lo.astype(BF16)
            hi = hi.astype(BF16)

            def up_half(r0):
                return (lax.dot_general(lo, wu_ref[0, r0:r0 + D_FF, :HALF], nt, preferred_element_type=F32)
                        + lax.dot_general(hi, wu_ref[0, r0:r0 + D_FF, HALF:], nt, preferred_element_type=F32))

            glu = jnp.minimum(up_half(0) + bu_ref[0, :, :D_FF], SWIGLU_LIMIT)
            lin = jnp.clip(up_half(D_FF) + bu_ref[0, :, D_FF:], -SWIGLU_LIMIT, SWIGLU_LIMIT)
            act = glu * _sigmoid(SWIGLU_ALPHA * glu) * (lin + 1.0)
            down = jnp.dot(act.astype(BF16), wd_ref[0], preferred_element_type=F32) + bd_ref[0]
            yp = _pack_pairs(down)
            y_ref[0, rs, :] = yp[:, :QUART]
            y_ref[1, rs, :] = yp[:, QUART:]

    @pl.when(first_tile >= nu_ref[0])
    def _():
        y_ref[...] = jnp.zeros(y_ref.shape, I32)


def _experts(tile_e, n_used, xs, w_up, b_up, w_down, b_down):
    r = xs.shape[1]
    tm = EXPERT_TILE * EXPERT_GROUP
    d = D_MODEL
    in_specs = [pl.BlockSpec((2, tm, QUART), lambda i, te, nu: (0, i, 0))]
    args = [xs]
    for s in range(EXPERT_GROUP):
        def pick(i, te, nu, s=s):
            return (te[i * EXPERT_GROUP + s], 0, 0)
        in_specs += [pl.BlockSpec((1, 2 * D_FF, d), pick), pl.BlockSpec((1, 1, 2 * D_FF), pick),
                     pl.BlockSpec((1, D_FF, d), pick), pl.BlockSpec((1, 1, d), pick)]
        args += [w_up, b_up, w_down, b_down]
    grid_spec = pltpu.PrefetchScalarGridSpec(
        num_scalar_prefetch=2,
        grid=(r // tm,),
        in_specs=in_specs,
        out_specs=pl.BlockSpec((2, tm, QUART), lambda i, te, nu: (0, i, 0)),
    )
    return pl.pallas_call(
        _expert_kernel,
        grid_spec=grid_spec,
        out_shape=jax.ShapeDtypeStruct((2, r, QUART), I32),
        compiler_params=_cparams(("arbitrary",)),
        name="expert_mlp",
    )(tile_e, n_used, *args)


def _combine_kernel(y_ref, tw_ref, x_ref, g2_ref, fg_ref, o_ref):
    xn = _moe_residual(y_ref, tw_ref, x_ref, g2_ref)
    r = lax.rsqrt(jnp.mean(xn * xn, axis=-1, keepdims=True) + NORM_EPS)
    o_ref[...] = xn * r * fg_ref[...]


def _combine(ytok, top_w, x2, g2, final_g, seq):
    n, d = x2.shape
    tm = min(ROW_TILE_OUT, seq)
    per_seq = seq // tm
    return pl.pallas_call(
        _combine_kernel,
        grid=(n // tm,),
        in_specs=[
            pl.BlockSpec((2, TOP_K, tm, QUART), lambda i: (0, 0, i, 0)),
            pl.BlockSpec((tm, LANES), lambda i: (i, 0)),
            pl.BlockSpec((tm, d), lambda i: (i, 0)),
            pl.BlockSpec((1, 1, d), lambda i: (i // per_seq, 0, 0)),
            pl.BlockSpec((1, d), lambda i: (0, 0)),
        ],
        out_specs=pl.BlockSpec((tm, d), lambda i: (i, 0)),
        out_shape=jax.ShapeDtypeStruct((n, d), F32),
        compiler_params=_cparams(("arbitrary",)),
        name="moe_combine",
    )(ytok, top_w, x2, g2, final_g)


def _route(top_idx, counts, n_rows):
    tm = EXPERT_TILE
    e = top_idx[:, :TOP_K]
    rank = top_idx[:, TOP_K:2 * TOP_K]
    sizes = counts[0, :N_EXPERTS]
    padded = ((sizes + tm - 1) // tm) * tm
    pend = jnp.cumsum(padded)
    pstart = pend - padded
    onehot = e[:, :, None] == jnp.arange(N_EXPERTS, dtype=I32)[None, None, :]
    pos = jnp.sum(jnp.where(onehot, pstart[None, None, :], 0), axis=-1) + rank
    tiles = jnp.arange(n_rows // tm, dtype=I32) * tm
    tile_e = jnp.minimum(jnp.sum(tiles[:, None] >= pend[None, :], axis=-1), N_EXPERTS - 1).astype(I32)
    n_used = (pend[-1] // tm).astype(I32).reshape(1)
    return pos.T.astype(I32), tile_e, n_used


def _sc_gather(table, idx):
    m = idx.shape[0]
    width = table.shape[1]
    assert m % (SC_WINDOW * SC_WORKERS) == 0
    mesh = plsc.VectorSubcoreMesh(core_axis_name="c", subcore_axis_name="s")

    @pl.kernel(out_type=jax.ShapeDtypeStruct((m, width), table.dtype), mesh=mesh)
    def gather_rows(t_hbm, i_hbm, o_hbm):
        def body(i_vmem, o_vmem):
            pltpu.sync_copy(t_hbm.at[i_vmem.at[0]], o_vmem)

        pltpu.emit_pipeline(
            body,
            grid=(m // SC_WINDOW,),
            in_specs=[pl.BlockSpec((1, SC_WINDOW), lambda i: (0, i))],
            out_specs=[pl.BlockSpec((SC_WINDOW, width), lambda i: (i, 0))],
            core_axis_name=("c", "s"),
            dimension_semantics=(pltpu.PARALLEL,),
        )(i_hbm, o_hbm)

    return gather_rows(table, idx.reshape(1, m))


def _sc_scatter(rows, idx, n_out):
    t, width = rows.shape
    picks = idx.shape[0]
    assert t % SC_WINDOW == 0
    mesh = plsc.VectorSubcoreMesh(core_axis_name="c", subcore_axis_name="s")

    @pl.kernel(out_type=jax.ShapeDtypeStruct((n_out, width), rows.dtype), mesh=mesh)
    def scatter_rows(x_hbm, i_hbm, o_hbm):
        def body(x_vmem, i_vmem):
            for k in range(picks):
                pltpu.sync_copy(x_vmem, o_hbm.at[i_vmem.at[k]])

        pltpu.emit_pipeline(
            body,
            grid=(t // SC_WINDOW,),
            in_specs=[pl.BlockSpec((SC_WINDOW, width), lambda i: (i, 0)),
                      pl.BlockSpec((picks, SC_WINDOW), lambda i: (0, i))],
            out_specs=[],
            core_axis_name=("c", "s"),
            dimension_semantics=(pltpu.PARALLEL,),
        )(x_hbm, i_hbm)

    return scatter_rows(rows, idx)


def _dispatch(h2p, pos_t, n_rows):
    n = h2p.shape[1]
    idx = jnp.concatenate([pos_t, pos_t + n_rows], axis=1)
    return _sc_scatter(h2p.reshape(2 * n, QUART), idx, 2 * n_rows).reshape(2, n_rows, QUART)


def _collect(y, pos_t):
    n_rows = y.shape[1]
    flat = pos_t.reshape(-1)
    out = _sc_gather(y.reshape(2 * n_rows, QUART), jnp.concatenate([flat, flat + n_rows]))
    return out.reshape(2, TOP_K, pos_t.shape[1], QUART)


def kernel(x, c, norm1_g, norm2_g, ada_w, ada_b, w_in, gate_b, fox_fb, lam_q1, lam_k1, lam_q2, lam_k2,
           diff_subln_g, w_branch, w_out, router_w, router_b, exp_w_up, exp_b_up, exp_w_down,
           exp_b_down, final_g):
    batch, seq, d = x.shape
    depth = ada_w.shape[0]
    n = batch * seq
    x2 = x.reshape(n, d)

    mod = _ada_mod(c, ada_w, ada_b)
    tables = _retention_tables(seq)
    n_rows = n * TOP_K + N_EXPERTS * EXPERT_TILE
    blk = BRANCH_WIDTH // LANES

    w_in_t = jnp.swapaxes(w_in, 1, 2)
    pending = None
    for l in range(depth):
        lam_init = 0.8 - 0.6 * math.exp(-0.3 * l)
        sh1, sc1, g1, sh2, sc2, g2 = [m.reshape(batch, 1, d) for m in jnp.split(mod[l], 6, axis=-1)]

        w_main, w_f = _prep_win(w_in_t, l)
        outs = _inproj(x2, norm1_g[l].reshape(1, d), sh1, sc1, w_main, w_f, seq, pending)
        proj, flog = outs[0], outs[1]
        if pending is not None:
            x2 = outs[2]

        fb_pad = jnp.pad(fox_fb[l], (0, LANES - FOX_HEADS)).reshape(1, LANES)
        frow = _fcum(flog, fb_pad, batch, seq).reshape(batch * blk, 2, seq)

        lamv = jnp.stack([lam_q1[l], lam_k1[l], lam_q2[l], lam_k2[l]])
        diff_extra = (lamv, diff_subln_g[l].reshape(1, LANES))
        if _rider_fits(batch, exp_w_up):
            o_diff, w_up = _attention(proj, "diff", 0, blk, 2 * blk, batch, seq, diff_extra, lam_init,
                                      rider=("up", exp_w_up, l))
        else:
            o_diff = _attention(proj, "diff", 0, blk, 2 * blk, batch, seq, diff_extra, lam_init)
            w_up = _prep_up(exp_w_up, l)
        o_ret = _retention(proj, tables, batch, seq)
        if _rider_fits(batch, exp_w_down):
            o_fox, w_down = _attention(proj, "fox", 7 * blk, 8 * blk, 9 * blk, batch, seq, (frow,),
                                       rider=("down", exp_w_down, l))
        else:
            o_fox = _attention(proj, "fox", 7 * blk, 8 * blk, 9 * blk, batch, seq, (frow,))
            w_down = exp_w_down[l].astype(BF16)

        rw = jnp.pad(router_w[l], ((0, 0), (0, LANES - N_EXPERTS)))
        rw_hi = rw.astype(BF16)
        rw = jnp.concatenate([rw_hi, (rw - rw_hi.astype(F32)).astype(BF16)], axis=1)
        rb = jnp.pad(router_b[l], (0, LANES - N_EXPERTS), constant_values=NEG_INF).reshape(1, LANES)
        x2, h2p, top_idx, top_w, counts = _merge(
            o_diff, o_ret, o_fox, proj, gate_b[l].reshape(N_BRANCHES, 1, d), w_branch[l].astype(BF16),
            w_out[l].astype(BF16), x2, g1, norm2_g[l].reshape(1, d), sh2, sc2, rw, rb, seq)

        pos_t, tile_e, n_used = _route(top_idx, counts, n_rows)
        xs = _dispatch(h2p, pos_t, n_rows)

        bu = exp_b_up[l]
        b_up = jnp.concatenate([bu[:, 0::2], bu[:, 1::2]], axis=-1).reshape(N_EXPERTS, 1, 2 * D_FF)
        y = _experts(tile_e, n_used, xs, w_up, b_up, w_down,
                     exp_b_down[l].reshape(N_EXPERTS, 1, d))
        pending = (_collect(y, pos_t), top_w, g2)

    ytok, top_w, g2 = pending
    x2 = _combine(ytok, top_w, x2, g2, final_g.reshape(1, d), seq)
    return x2.reshape(batch, seq, d)
```

```python
import functools
import math

import jax
import jax.numpy as jnp
from jax import lax
from jax.experimental import pallas as pl
from jax.experimental.pallas import tpu as pltpu
from jax.experimental.pallas import tpu_sc as plsc

F32 = jnp.float32
BF16 = jnp.bfloat16
I32 = jnp.int32
U32 = jnp.uint32

D_MODEL = 1024
HEAD_DIM = 64
BRANCH_WIDTH = D_MODEL // 2
N_BRANCHES = 3
FOX_HEADS = BRANCH_WIDTH // HEAD_DIM
RET_HEADS = 4
RET_HEAD_DIM = 128
ROPE_BASE = 10000.0
RET_GAMMA_EXP0 = 5.0
N_EXPERTS = 32
TOP_K = 4
D_FF = D_MODEL
SWIGLU_LIMIT = 7.0
SWIGLU_ALPHA = 1.702
NORM_EPS = 1e-5
NEG_INF = -1e30

LANES = 128
MAIN_COLS = 8192
GATE_COL0 = 10 * BRANCH_WIDTH
HALF = D_MODEL // 2
QUART = HALF // 2
SC_WINDOW = 128
SC_WORKERS = 32
VMEM_LIMIT = 48 * 1024 * 1024
VMEM_LIMIT_INPROJ = 56 * 1024 * 1024

ROW_TILE_IN = 512
COL_TILE_IN = 1024
ATTN_TILE = 256
RET_CHUNK = 256
ROW_TILE_OUT = 256
MERGE_TILE = 1024
MERGE_SUB = 512
EXPERT_TILE = 256
EXPERT_GROUP = 2
CUM_CHUNK = 256


def _cparams(sem):
    return pltpu.CompilerParams(dimension_semantics=sem, vmem_limit_bytes=VMEM_LIMIT)


def _sigmoid(x):
    return 0.5 * jnp.tanh(0.5 * x) + 0.5


def _pack_pairs(v):
    u = lax.bitcast_convert_type(v.astype(BF16).astype(F32), U32)
    half = v.shape[-1] // 2
    w = (u[:, half:] & jnp.uint32(0xFFFF0000)) | (u[:, :half] >> 16)
    return lax.bitcast_convert_type(w, I32)


def _unpack_pairs(w):
    u = lax.bitcast_convert_type(w, U32)
    lo = lax.bitcast_convert_type(u << 16, F32)
    hi = lax.bitcast_convert_type(u & jnp.uint32(0xFFFF0000), F32)
    return lo, hi


def _moe_residual(y_ref, tw_ref, x_ref, g2_ref):
    tw = tw_ref[...]
    parts = [None] * 4
    for k in range(TOP_K):
        w = tw[:, k:k + 1]
        for piece in range(2):
            lo, hi = _unpack_pairs(y_ref[piece, k])
            for slot, v in ((piece, lo), (2 + piece, hi)):
                parts[slot] = w * v if parts[slot] is None else parts[slot] + w * v
    return x_ref[...] + g2_ref[0] * jnp.concatenate(parts, axis=-1)


def _ada_kernel(c_ref, w_ref, b_ref, o_ref):
    c = c_ref[...]
    ca = c * _sigmoid(c)
    o_ref[0] = jnp.dot(ca, w_ref[0], preferred_element_type=F32) + b_ref[0]


def _ada_mod(c, ada_w, ada_b):
    depth, d, cols = ada_w.shape
    b = c.shape[0]
    tn = 1024
    return pl.pallas_call(
        _ada_kernel,
        grid=(depth, cols // tn),
        in_specs=[
            pl.BlockSpec((b, d), lambda l, j: (0, 0)),
            pl.BlockSpec((1, d, tn), lambda l, j: (l, 0, j)),
            pl.BlockSpec((1, 1, tn), lambda l, j: (l, 0, j)),
        ],
        out_specs=pl.BlockSpec((1, b, tn), lambda l, j: (l, 0, j)),
        out_shape=jax.ShapeDtypeStruct((depth, b, cols), F32),
        compiler_params=_cparams(("arbitrary", "arbitrary")),
        name="ada_mod",
    )(c, ada_w, ada_b.reshape(depth, 1, cols))


def _prep_win_kernel(w_ref, wm_ref, wf_ref):
    wm_ref[:GATE_COL0, :] = w_ref[0, :GATE_COL0, :].astype(BF16)
    wm_ref[GATE_COL0:, :] = w_ref[0, GATE_COL0 + FOX_HEADS:, :].astype(BF16)
    wf_ref[...] = jnp.zeros(wf_ref.shape, BF16)
    wf_ref[:FOX_HEADS, :] = w_ref[0, GATE_COL0:GATE_COL0 + FOX_HEADS, :].astype(BF16)


def _prep_win(w_t, layer):
    _, cols, d = w_t.shape
    ct = 256
    return pl.pallas_call(
        _prep_win_kernel,
        grid=(d // ct,),
        in_specs=[pl.BlockSpec((1, cols, ct), lambda i: (layer, 0, i))],
        out_specs=[pl.BlockSpec((MAIN_COLS, ct), lambda i: (0, i)), pl.BlockSpec((LANES, ct), lambda i: (0, i))],
        out_shape=[jax.ShapeDtypeStruct((MAIN_COLS, d), BF16), jax.ShapeDtypeStruct((LANES, d), BF16)],
        compiler_params=_cparams(("arbitrary",)),
        name="inproj_weight_prep",
    )(w_t)


def _inproj_kernel(x_ref, g_ref, sh_ref, sc_ref, w_ref, wf_ref, proj_ref, fl_ref):
    x = x_ref[...]
    r = lax.rsqrt(jnp.mean(x * x, axis=-1, keepdims=True) + NORM_EPS)
    h = x * r * g_ref[...]
    h = h * (1.0 + sc_ref[0]) + sh_ref[0]
    _project(h.astype(BF16), w_ref, wf_ref, proj_ref, fl_ref)


def _project(hb, w_ref, wf_ref, proj_ref, fl_ref, after_chunk=None):
    nt = (((1,), (1,)), ((), ()))
    fl_ref[...] = lax.dot_general(hb, wf_ref[...], nt, preferred_element_type=F32)
    for j in range(MAIN_COLS // COL_TILE_IN):
        cs = slice(j * COL_TILE_IN, (j + 1) * COL_TILE_IN)
        proj_ref[:, cs] = lax.dot_general(hb, w_ref[cs, :], nt, preferred_element_type=F32).astype(BF16)
        if after_chunk is not None:
            after_chunk(j)


def _inproj_moe_kernel(y_ref, tw_ref, x_ref, g2_ref, g_ref, sh_ref, sc_ref, w_ref, wf_ref,
                       proj_ref, fl_ref, xo_ref, h_scr):
    i = pl.program_id(0)

    @pl.when(i == 0)
    def _():
        h_scr[...] = jnp.zeros(h_scr.shape, BF16)

    span = h_scr.shape[1] // (MAIN_COLS // COL_TILE_IN)

    def normalise_slice(j):
        rs = slice(j * span, (j + 1) * span)
        x = _moe_residual(y_ref.at[:, :, rs, :], tw_ref.at[rs, :], x_ref.at[rs, :], g2_ref)
        xo_ref[rs, :] = x
        r = lax.rsqrt(jnp.mean(x * x, axis=-1, keepdims=True) + NORM_EPS)
        h = x * r * g_ref[...]
        h = h * (1.0 + sc_ref[0]) + sh_ref[0]
        h_scr[i % 2, rs, :] = h.astype(BF16)

    _project(h_scr[(i + 1) % 2], w_ref, wf_ref, proj_ref, fl_ref, after_chunk=normalise_slice)


def _inproj(x2, g, sh, sc, w_main, w_f, seq, moe=None):
    n, d = x2.shape
    tm = min(ROW_TILE_IN, seq)
    per_seq = seq // tm
    steps = n // tm
    last = steps - 1
    resident = pl.Buffered(1)
    weights = [pl.BlockSpec((MAIN_COLS, d), lambda i: (0, 0), pipeline_mode=resident),
               pl.BlockSpec((LANES, d), lambda i: (0, 0), pipeline_mode=resident)]
    out_shape = [jax.ShapeDtypeStruct((n, MAIN_COLS), BF16), jax.ShapeDtypeStruct((n, LANES), F32)]
    params = pltpu.CompilerParams(dimension_semantics=("arbitrary",), vmem_limit_bytes=VMEM_LIMIT_INPROJ)

    if moe is None:
        def per_b():
            return pl.BlockSpec((1, 1, d), lambda i: (i // per_seq, 0, 0))

        return pl.pallas_call(
            _inproj_kernel,
            grid=(steps,),
            in_specs=[pl.BlockSpec((tm, d), lambda i: (i, 0)), pl.BlockSpec((1, d), lambda i: (0, 0)),
                      per_b(), per_b()] + weights,
            out_specs=[pl.BlockSpec((tm, MAIN_COLS), lambda i: (i, 0)), pl.BlockSpec((tm, LANES), lambda i: (i, 0))],
            out_shape=out_shape,
            compiler_params=params,
            name="inproj",
        )(x2, g, sh, sc, w_main, w_f)

    def rows_in(w):
        return pl.BlockSpec((tm, w), lambda i: (jnp.minimum(i, last), 0))

    def rows_out(w):
        return pl.BlockSpec((tm, w), lambda i: (jnp.maximum(i - 1, 0), 0))

    def per_b():
        return pl.BlockSpec((1, 1, d), lambda i: (jnp.minimum(i, last) // per_seq, 0, 0))

    ytok, top_w, g2 = moe
    return pl.pallas_call(
        _inproj_moe_kernel,
        grid=(steps + 1,),
        in_specs=[pl.BlockSpec((2, TOP_K, tm, QUART), lambda i: (0, 0, jnp.minimum(i, last), 0)),
                  rows_in(LANES), rows_in(d), per_b(),
                  pl.BlockSpec((1, d), lambda i: (0, 0)), per_b(), per_b()] + weights,
        out_specs=[rows_out(MAIN_COLS), rows_out(LANES), rows_in(d)],
        out_shape=out_shape + [jax.ShapeDtypeStruct((n, d), F32)],
        scratch_shapes=[pltpu.VMEM((2, tm, d), BF16)],
        compiler_params=params,
        name="inproj",
    )(ytok, top_w, x2, g2, g, sh, sc, w_main, w_f)


def _split3(v):
    hi = v.astype(BF16)
    r1 = v - hi.astype(F32)
    mid = r1.astype(BF16)
    lo = (r1 - mid.astype(F32)).astype(BF16)
    return hi, mid, lo


def _fcum_kernel(fl_ref, fb_ref, o_ref, *, seq):
    c = min(CUM_CHUNK, seq)
    row = lax.broadcasted_iota(I32, (c, c), 0)
    col = lax.broadcasted_iota(I32, (c, c), 1)
    lower = (col <= row).astype(BF16)
    carry = jnp.zeros((1, LANES), F32)
    for ci in range(seq // c):
        x = fl_ref[pl.ds(ci * c, c), :] + fb_ref[...]
        lf = jnp.minimum(x, 0.0) - jnp.log(1.0 + jnp.exp(-jnp.abs(x)))
        hi, mid, lo = _split3(lf)
        cs = (jnp.dot(lower, hi, preferred_element_type=F32)
              + jnp.dot(lower, mid, preferred_element_type=F32)
              + jnp.dot(lower, lo, preferred_element_type=F32)) + carry
        o_ref[0, :, pl.ds(ci * c, c)] = cs.T[:FOX_HEADS, :]
        carry = cs[c - 1:c, :]


def _fcum(flog, fb_pad, batch, seq):
    return pl.pallas_call(
        functools.partial(_fcum_kernel, seq=seq),
        grid=(batch,),
        in_specs=[
            pl.BlockSpec((seq, LANES), lambda b: (b, 0)),
            pl.BlockSpec((1, LANES), lambda b: (0, 0)),
        ],
        out_specs=pl.BlockSpec((1, FOX_HEADS, seq), lambda b: (b, 0, 0)),
        out_shape=jax.ShapeDtypeStruct((batch, FOX_HEADS, seq), F32),
        compiler_params=_cparams(("arbitrary",)),
        name="fox_decay_cumsum",
    )(flog, fb_pad)


def _attn_kernel(*refs, mode, t, seq, lam_init, rider):
    if rider:
        refs = list(refs)
        wt_scr = refs.pop() if rider == "up" else None
        n_in = 5 if mode == "diff" else 4
        wraw_ref = refs.pop(n_in)
        wprep_ref = refs.pop(n_in + 1)
    if mode == "diff":
        q_ref, k_ref, v_ref, lam_ref, g_ref, o_ref, vx_scr = refs
        f_ref = None
        lv = lam_ref[...]
        lam = (jnp.exp(jnp.sum(lv[0:1, :] * lv[1:2, :])) - jnp.exp(jnp.sum(lv[2:3, :] * lv[3:4, :]))
               + lam_init)
    else:
        q_ref, k_ref, v_ref, f_ref, o_ref, vx_scr = refs
    lane = lax.broadcasted_iota(I32, (t, LANES), 1)
    row = lax.broadcasted_iota(I32, (t, t), 0)
    col = lax.broadcasted_iota(I32, (t, t), 1)
    nt = (((1,), (1,)), ((), ()))

    v_all = v_ref[...]
    one = jnp.ones_like(v_all)
    if mode == "diff":
        vx_scr[0] = jnp.concatenate([v_all, one], axis=1)
    else:
        lane_s = lax.broadcasted_iota(I32, v_all.shape, 1)
        vx_scr[0] = jnp.where(lane_s < HEAD_DIM, v_all, one)
        vx_scr[1] = jnp.where(lane_s >= HEAD_DIM, v_all, one)

    n_strips = seq // t
    pairs = [(i, n_strips - 1 - i) for i in range(n_strips // 2)] if n_strips % 2 == 0 else [(i,) for i in range(n_strips)]
    for gi, group in enumerate(pairs):
        if rider == "up":
            for g in range(gi, wt_scr.shape[0], len(pairs)):
                _prep_up_group(wraw_ref, wprep_ref, wt_scr, g)
        elif rider == "down":
            span = wprep_ref.shape[1] // len(pairs)
            rows = slice(gi * span, (gi + 1) * span)
            wprep_ref[0, rows, :] = wraw_ref[0, 0, rows, :].astype(BF16)
        units = [(qi, idx) for qi in group for idx in range(2)]
        qms = {}
        for qi in group:
            q = q_ref[qi * t:(qi + 1) * t, :] * jnp.asarray(HEAD_DIM ** -0.5, BF16)
            zero = jnp.zeros_like(q)
            qms[qi, 0] = jnp.where(lane < HEAD_DIM, q, zero)
            qms[qi, 1] = jnp.where(lane >= HEAD_DIM, q, zero)

        def scores(u, k0, k1):
            s_ = lax.dot_general(qms[u], k_ref[k0:k1, :], nt, preferred_element_type=F32)
            if f_ref is not None:
                s_ = s_ - f_ref[0, u[1]:u[1] + 1, k0:k1]
            return s_

        sds = {u: jnp.where(col <= row, scores(u, u[0] * t, (u[0] + 1) * t), NEG_INF) for u in units}
        sms = {u: scores(u, 0, u[0] * t) for u in units if u[0] > 0}
        ms = {}
        for u in units:
            m = jnp.max(sds[u], axis=-1, keepdims=True)
            if u in sms:
                m = jnp.maximum(m, jnp.max(sms[u], axis=-1, keepdims=True))
            ms[u] = m
        pds = {u: jnp.exp((sds[u] - ms[u]).astype(BF16)) for u in units}
        pms = {u: jnp.exp((sms[u] - ms[u]).astype(BF16)) for u in sms}
        outs = {}
        for u in units:
            vx = vx_scr.at[0 if mode == "diff" else u[1]]
            lo = u[0] * t
            acc = jnp.dot(pds[u], vx[lo:lo + t, :], preferred_element_type=F32)
            if u in pms:
                acc = acc + jnp.dot(pms[u], vx[0:lo, :], preferred_element_type=F32)
            outs[u] = acc
        for qi in group:
            o0, o1 = outs[qi, 0], outs[qi, 1]
            if mode == "diff":
                a1 = o0[:, :LANES] / o0[:, LANES:]
                a2 = o1[:, :LANES] / o1[:, LANES:]
                o = a1 - lam * a2
                r = lax.rsqrt(jnp.mean(o * o, axis=-1, keepdims=True) + NORM_EPS)
                o = o * r * g_ref[...] * (1.0 - lam_init)
            else:
                num = jnp.where(lane < HEAD_DIM, o0, o1)
                den = jnp.where(lane < HEAD_DIM, pltpu.roll(o0, HEAD_DIM, 1), pltpu.roll(o1, HEAD_DIM, 1))
                o = num / den
            o_ref[qi * t:(qi + 1) * t, :] = o.astype(BF16)


def _attention(proj, mode, col_q, col_k, col_v, batch, seq, extra, lam_init=0.0, rider=None):
    n = proj.shape[0]
    t = min(ATTN_TILE, seq)
    n_blk = BRANCH_WIDTH // LANES
    in_specs = [
        pl.BlockSpec((seq, LANES), lambda b, h: (b, col_q + h)),
        pl.BlockSpec((seq, LANES), lambda b, h: (b, col_k + h)),
        pl.BlockSpec((seq, LANES), lambda b, h: (b, col_v + h)),
    ]
    if mode == "diff":
        lamv, g = extra
        in_specs += [pl.BlockSpec((4, HEAD_DIM), lambda b, h: (0, 0)),
                     pl.BlockSpec((1, LANES), lambda b, h: (0, 0))]
        args = (proj, proj, proj, lamv, g)
    else:
        (frow,) = extra
        in_specs += [pl.BlockSpec((1, 2, seq), lambda b, h: (b * n_blk + h, 0, 0))]
        args = (proj, proj, proj, frow)
    out_specs = [pl.BlockSpec((seq, LANES), lambda b, h: (b, h))]
    out_shape = [jax.ShapeDtypeStruct((n, BRANCH_WIDTH), BF16)]
    scratch = [pltpu.VMEM((1, seq, 2 * LANES) if mode == "diff" else (2, seq, LANES), BF16)]
    if rider is not None:
        kind, w, layer = rider
        _, e, rows, cols = w.shape
        parts = batch * n_blk // e
        kt = rows // parts
        in_specs.append(pl.BlockSpec((1, 1, kt, cols), lambda b, h: (layer, (b * n_blk + h) // parts, (b * n_blk + h) % parts, 0)))
        args = args + (w,)
        if kind == "up":
            out_specs.append(pl.BlockSpec((1, cols, kt), lambda b, h: ((b * n_blk + h) // parts, 0, (b * n_blk + h) % parts)))
            out_shape.append(jax.ShapeDtypeStruct((e, cols, rows), BF16))
            scratch.append(pltpu.VMEM((kt // LANES, cols, LANES), F32))
        else:
            out_specs.append(pl.BlockSpec((1, kt, cols), lambda b, h: ((b * n_blk + h) // parts, (b * n_blk + h) % parts, 0)))
            out_shape.append(jax.ShapeDtypeStruct((e, rows, cols), BF16))
    outs = pl.pallas_call(
        functools.partial(_attn_kernel, mode=mode, t=t, seq=seq, lam_init=lam_init, rider=rider[0] if rider else None),
        grid=(batch, n_blk),
        in_specs=in_specs,
        out_specs=out_specs,
        out_shape=out_shape,
        scratch_shapes=scratch,
        compiler_params=_cparams(("arbitrary", "arbitrary")),
        name="attn_" + mode,
    )(*args)
    return outs if rider is not None else outs[0]


def _rider_fits(batch, w):
    _, e, rows, _ = w.shape
    steps = batch * (BRANCH_WIDTH // LANES)
    return steps % e == 0 and (rows // LANES) % (steps // e) == 0


def _ret_kernel(q_ref, k_ref, v_ref, g_ref, cos_ref, sin_ref, dm_ref, qd_ref, kd_ref, cd_ref,
                o_ref, *, seq, c):
    st = jnp.zeros((RET_HEAD_DIM, RET_HEAD_DIM), F32)
    half = RET_HEAD_DIM // 2

    for ci in range(seq // c):
        sl = pl.ds(ci * c, c)
        cos = cos_ref[sl, :]
        sin = sin_ref[sl, :]
        q = q_ref[sl, :].astype(F32)
        k = k_ref[sl, :].astype(F32)
        qr = q * cos + pltpu.roll(q, half, 1) * sin
        kr = (k * cos + pltpu.roll(k, half, 1) * sin) * (RET_HEAD_DIM ** -0.5)
        v = v_ref[sl, :]
        sc = lax.dot_general(qr.astype(BF16), kr.astype(BF16), (((1,), (1,)), ((), ())),
                             preferred_element_type=F32) * dm_ref[0]
        inner = jnp.dot(sc.astype(BF16), v, preferred_element_type=F32)
        cross = jnp.dot((qr * qd_ref[0]).astype(BF16), st.astype(BF16), preferred_element_type=F32)
        o = inner + cross
        kt = (kr * kd_ref[0]).T.astype(BF16)
        st = st * cd_ref[0] + jnp.dot(kt, v, preferred_element_type=F32)
        mu = jnp.mean(o, axis=-1, keepdims=True)
        oc = o - mu
        var = jnp.mean(oc * oc, axis=-1, keepdims=True)
        on = oc * lax.rsqrt(var + NORM_EPS)
        g = g_ref[sl, :].astype(F32)
        o_ref[sl, :] = (on * (g * _sigmoid(g))).astype(BF16)


def _retention(proj, tables, batch, seq):
    n = proj.shape[0]
    c = min(RET_CHUNK, seq)
    cos, sin, dm, qd, kd, cd = tables
    col = BRANCH_WIDTH * 3 // LANES
    step = BRANCH_WIDTH // LANES

    def pspec(k):
        return pl.BlockSpec((seq, LANES), lambda b, h: (b, col + k * step + h))

    return pl.pallas_call(
        functools.partial(_ret_kernel, seq=seq, c=c),
        grid=(batch, RET_HEADS),
        in_specs=[
            pspec(0), pspec(1), pspec(2), pspec(3),
            pl.BlockSpec((seq, LANES), lambda b, h: (0, 0)),
            pl.BlockSpec((seq, LANES), lambda b, h: (0, 0)),
            pl.BlockSpec((1, c, c), lambda b, h: (h, 0, 0)),
            pl.BlockSpec((1, c, LANES), lambda b, h: (h, 0, 0)),
            pl.BlockSpec((1, c, LANES), lambda b, h: (h, 0, 0)),
            pl.BlockSpec((1, LANES, LANES), lambda b, h: (h, 0, 0)),
        ],
        out_specs=pl.BlockSpec((seq, LANES), lambda b, h: (b, h)),
        out_shape=jax.ShapeDtypeStruct((n, BRANCH_WIDTH), BF16),
        compiler_params=_cparams(("arbitrary", "arbitrary")),
        name="retention",
    )(proj, proj, proj, proj, cos, sin, dm, qd, kd, cd)


def _retention_tables(seq):
    c = min(RET_CHUNK, seq)
    d = RET_HEAD_DIM
    inv = ROPE_BASE ** (-jnp.arange(0, d, 2, dtype=F32) / d)
    ang = jnp.arange(seq).astype(F32)[:, None] * inv[None, :]
    cos = jnp.cos(ang)
    sin = jnp.sin(ang)
    cos_full = jnp.concatenate([cos, cos], axis=-1)
    sin_signed = jnp.concatenate([-sin, sin], axis=-1)
    log_gamma = jnp.log1p(-jnp.exp2(-RET_GAMMA_EXP0 - jnp.arange(RET_HEADS, dtype=F32)))
    idx = jnp.arange(c, dtype=F32)
    dist = idx[:, None] - idx[None, :]
    dm = jnp.where(dist >= 0, jnp.exp(log_gamma[:, None, None] * jnp.maximum(dist, 0.0)), 0.0)
    qd = jnp.exp(log_gamma[:, None] * (idx + 1.0))
    kd = jnp.exp(log_gamma[:, None] * (c - 1.0 - idx))
    cd = jnp.exp(log_gamma * c)
    qd = jnp.broadcast_to(qd[:, :, None], (RET_HEADS, c, LANES))
    kd = jnp.broadcast_to(kd[:, :, None], (RET_HEADS, c, LANES))
    cd = jnp.broadcast_to(cd[:, None, None], (RET_HEADS, LANES, LANES))
    return cos_full, sin_signed, dm, qd, kd, cd


def _merge_kernel(od_ref, or_ref, of_ref, gd_ref, gr_ref, gf_ref, gb_ref, wb_ref, wo_ref, x_ref,
                  g1_ref, n2_ref, sh_ref, sc_ref, rw_ref, rb_ref,
                  xo_ref, hp_ref, ti_ref, tw_ref, cnt_ref, cnt_scr, *, tm, sub):
    @pl.when(pl.program_id(0) == 0)
    def _():
        cnt_scr[...] = jnp.zeros(cnt_scr.shape, F32)

    lane = lax.broadcasted_iota(I32, (sub, LANES), 1)
    r_i = lax.broadcasted_iota(I32, (sub, sub), 0)
    c_i = lax.broadcasted_iota(I32, (sub, sub), 1)
    earlier = (c_i < r_i).astype(BF16)
    counts = cnt_scr[...]

    blocks = [slice(s0, s0 + sub) for s0 in range(0, tm, sub)]

    def phase_branches(rs):
        merged = None
        for i, (o_ref, gate_ref) in enumerate(((od_ref, gd_ref), (or_ref, gr_ref), (of_ref, gf_ref))):
            gate = _sigmoid(gate_ref[rs, :] + gb_ref[i].astype(BF16))
            term = gate.astype(F32) * jnp.dot(o_ref[rs, :], wb_ref[i], preferred_element_type=F32)
            merged = term if merged is None else merged + term
        return merged.astype(BF16)

    def phase_outproj(rs, merged):
        mix = jnp.dot(merged, wo_ref[...], preferred_element_type=F32)
        xn = x_ref[rs, :] + g1_ref[0] * mix
        xo_ref[rs, :] = xn
        r = lax.rsqrt(jnp.mean(xn * xn, axis=-1, keepdims=True) + NORM_EPS)
        h = xn * r * n2_ref[...]
        h = h * (1.0 + sc_ref[0]) + sh_ref[0]
        hp = _pack_pairs(h)
        hp_ref[0, rs, :] = hp[:, :QUART]
        hp_ref[1, rs, :] = hp[:, QUART:]
        hh, hm, _ = _split3(h)
        both = jnp.dot(hh, rw_ref[...], preferred_element_type=F32)
        return (both[:, :LANES] + both[:, LANES:]
                + jnp.dot(hm, rw_ref[:, :LANES], preferred_element_type=F32)) + rb_ref[...]

    def phase_topk(rs, logits, counts):
        vals, idxs = [], []
        for _ in range(TOP_K):
            m = jnp.max(logits, axis=-1, keepdims=True)
            idx = jnp.min(jnp.where(logits == m, lane, LANES), axis=-1, keepdims=True)
            vals.append(m)
            idxs.append(idx)
            logits = jnp.where(lane == idx, -3.0e38, logits)
        es = [jnp.exp(v - vals[0]) for v in vals]
        den = es[0] + es[1] + es[2] + es[3]
        hits = [lane == idxs[k] for k in range(TOP_K)]
        picked = (hits[0] | hits[1] | hits[2] | hits[3])
        base = jnp.dot(earlier, picked.astype(BF16), preferred_element_type=F32) + counts
        counts = counts + jnp.sum(picked.astype(F32), axis=0, keepdims=True)
        ti = jnp.zeros((sub, LANES), I32)
        tw = jnp.zeros((sub, LANES), F32)
        for k in range(TOP_K):
            rank = jnp.sum(jnp.where(hits[k], base, 0.0), axis=-1, keepdims=True).astype(I32)
            ti = jnp.where(lane == k, idxs[k], ti)
            ti = jnp.where(lane == TOP_K + k, rank, ti)
            tw = jnp.where(lane == k, es[k] / den, tw)
        ti_ref[rs, :] = ti
        tw_ref[rs, :] = tw
        return counts

    merged = [phase_branches(rs) for rs in blocks]
    logits = [phase_outproj(rs, m) for rs, m in zip(blocks, merged)]
    for rs, lg in zip(blocks, logits):
        counts = phase_topk(rs, lg, counts)

    cnt_scr[...] = counts
    cnt_ref[...] = counts.astype(I32)


def _merge(o_diff, o_ret, o_fox, proj, gate_b, w_branch, w_out, x2, g1, n2g, sh2, sc2, rw, rb, seq):
    n, d = x2.shape
    tm = min(MERGE_TILE, seq)
    sub = min(MERGE_SUB, tm)
    per_seq = seq // tm
    gcol = GATE_COL0 // d

    def rows(w):
        return pl.BlockSpec((tm, w), lambda i: (i, 0))

    def gspec(k):
        return pl.BlockSpec((tm, d), lambda i: (i, gcol + k))

    def whole(shape):
        return pl.BlockSpec(shape, lambda i: (0,) * len(shape))

    def per_b():
        return pl.BlockSpec((1, 1, d), lambda i: (i // per_seq, 0, 0))

    return pl.pallas_call(
        functools.partial(_merge_kernel, tm=tm, sub=sub),
        grid=(n // tm,),
        in_specs=[
            rows(BRANCH_WIDTH), rows(BRANCH_WIDTH), rows(BRANCH_WIDTH),
            gspec(0), gspec(1), gspec(2),
            whole((N_BRANCHES, 1, d)), whole((N_BRANCHES, BRANCH_WIDTH, d)), whole((d, d)),
            rows(d), per_b(), whole((1, d)), per_b(), per_b(),
            whole((d, 2 * LANES)), whole((1, LANES)),
        ],
        out_specs=[rows(d), pl.BlockSpec((2, tm, QUART), lambda i: (0, i, 0)), rows(LANES), rows(LANES),
                   whole((1, LANES))],
        out_shape=[
            jax.ShapeDtypeStruct((n, d), F32),
            jax.ShapeDtypeStruct((2, n, QUART), I32),
            jax.ShapeDtypeStruct((n, LANES), I32),
            jax.ShapeDtypeStruct((n, LANES), F32),
            jax.ShapeDtypeStruct((1, LANES), I32),
        ],
        scratch_shapes=[pltpu.VMEM((1, LANES), F32)],
        compiler_params=_cparams(("arbitrary",)),
        name="merge_outproj_router",
    )(o_diff, o_ret, o_fox, proj, proj, proj, gate_b, w_branch, w_out, x2, g1, n2g, sh2, sc2, rw, rb)


def _prep_up_group(w_ref, o_ref, t_scr, g):
    f = t_scr.shape[1] // 2
    cols = slice(g * LANES, (g + 1) * LANES)
    t_scr[g] = w_ref[0, 0, cols, :].T
    o_ref[0, 0:f, cols] = t_scr[g, pl.ds(0, f, stride=2), :].astype(BF16)
    o_ref[0, f:, cols] = t_scr[g, pl.ds(1, f, stride=2), :].astype(BF16)


def _prep_up_kernel(w_ref, o_ref, t_scr):
    for g in range(t_scr.shape[0]):
        _prep_up_group(w_ref, o_ref, t_scr, g)


def _prep_up(w_up, layer):
    _, e, d, f2 = w_up.shape
    groups = d // LANES
    kt = groups * LANES
    return pl.pallas_call(
        _prep_up_kernel,
        grid=(e, d // kt),
        in_specs=[pl.BlockSpec((1, 1, kt, f2), lambda i, j: (layer, i, j, 0))],
        out_specs=pl.BlockSpec((1, f2, kt), lambda i, j: (i, 0, j)),
        out_shape=jax.ShapeDtypeStruct((e, f2, d), BF16),
        scratch_shapes=[pltpu.VMEM((groups, f2, LANES), F32)],
        compiler_params=_cparams(("arbitrary", "arbitrary")),
        name="expert_up_weight_prep",
    )(w_up)


def _expert_kernel(te_ref, nu_ref, xs_ref, *refs):
    y_ref = refs[-1]
    nt = (((1,), (1,)), ((), ()))
    first_tile = pl.program_id(0) * EXPERT_GROUP

    @pl.when(first_tile < nu_ref[0])
    def _():
        for s in range(EXPERT_GROUP):
            wu_ref, bu_ref, wd_ref, bd_ref = refs[4 * s:4 * s + 4]
            rs = slice(s * EXPERT_TILE, (s + 1) * EXPERT_TILE)
            lo, hi = _unpack_pairs(jnp.concatenate([xs_ref[0, rs, :], xs_ref[1, rs, :]], axis=1))
            lo = lo.astype(BF16)
            hi = hi.astype(BF16)

            def up_half(r0):
                return (lax.dot_general(lo, wu_ref[0, r0:r0 + D_FF, :HALF], nt, preferred_element_type=F32)
                        + lax.dot_general(hi, wu_ref[0, r0:r0 + D_FF, HALF:], nt, preferred_element_type=F32))

            glu = jnp.minimum(up_half(0) + bu_ref[0, :, :D_FF], SWIGLU_LIMIT)
            lin = jnp.clip(up_half(D_FF) + bu_ref[0, :, D_FF:], -SWIGLU_LIMIT, SWIGLU_LIMIT)
            act = glu * _sigmoid(SWIGLU_ALPHA * glu) * (lin + 1.0)
            down = jnp.dot(act.astype(BF16), wd_ref[0], preferred_element_type=F32) + bd_ref[0]
            yp = _pack_pairs(down)
            y_ref[0, rs, :] = yp[:, :QUART]
            y_ref[1, rs, :] = yp[:, QUART:]

    @pl.when(first_tile >= nu_ref[0])
    def _():
        y_ref[...] = jnp.zeros(y_ref.shape, I32)


def _experts(tile_e, n_used, xs, w_up, b_up, w_down, b_down):
    r = xs.shape[1]
    tm = EXPERT_TILE * EXPERT_GROUP
    d = D_MODEL
    in_specs = [pl.BlockSpec((2, tm, QUART), lambda i, te, nu: (0, i, 0))]
    args = [xs]
    for s in range(EXPERT_GROUP):
        def pick(i, te, nu, s=s):
            return (te[i * EXPERT_GROUP + s], 0, 0)
        in_specs += [pl.BlockSpec((1, 2 * D_FF, d), pick), pl.BlockSpec((1, 1, 2 * D_FF), pick),
                     pl.BlockSpec((1, D_FF, d), pick), pl.BlockSpec((1, 1, d), pick)]
        args += [w_up, b_up, w_down, b_down]
    grid_spec = pltpu.PrefetchScalarGridSpec(
        num_scalar_prefetch=2,
        grid=(r // tm,),
        in_specs=in_specs,
        out_specs=pl.BlockSpec((2, tm, QUART), lambda i, te, nu: (0, i, 0)),
    )
    return pl.pallas_call(
        _expert_kernel,
        grid_spec=grid_spec,
        out_shape=jax.ShapeDtypeStruct((2, r, QUART), I32),
        compiler_params=_cparams(("arbitrary",)),
        name="expert_mlp",
    )(tile_e, n_used, *args)


def _combine_kernel(y_ref, tw_ref, x_ref, g2_ref, fg_ref, o_ref):
    xn = _moe_residual(y_ref, tw_ref, x_ref, g2_ref)
    r = lax.rsqrt(jnp.mean(xn * xn, axis=-1, keepdims=True) + NORM_EPS)
    o_ref[...] = xn * r * fg_ref[...]


def _combine(ytok, top_w, x2, g2, final_g, seq):
    n, d = x2.shape
    tm = min(ROW_TILE_OUT, seq)
    per_seq = seq // tm
    deep = pl.Buffered(3)

    def streamed(y_hbm, tw_hbm, x_hbm, g2_hbm, fg_hbm, o_hbm):
        pltpu.emit_pipeline(
            _combine_kernel,
            grid=(n // tm,),
            in_specs=[
                pl.BlockSpec((2, TOP_K, tm, QUART), lambda i: (0, 0, i, 0), pipeline_mode=deep),
                pl.BlockSpec((tm, LANES), lambda i: (i, 0)),
                pl.BlockSpec((tm, d), lambda i: (i, 0), pipeline_mode=deep),
                pl.BlockSpec((1, 1, d), lambda i: (i // per_seq, 0, 0)),
                pl.BlockSpec((1, d), lambda i: (0, 0)),
            ],
            out_specs=[pl.BlockSpec((tm, d), lambda i: (i, 0))],
        )(y_hbm, tw_hbm, x_hbm, g2_hbm, fg_hbm, o_hbm)

    anywhere = pl.BlockSpec(memory_space=pl.ANY)
    return pl.pallas_call(
        streamed,
        in_specs=[anywhere] * 5,
        out_specs=anywhere,
        out_shape=jax.ShapeDtypeStruct((n, d), F32),
        compiler_params=pltpu.CompilerParams(vmem_limit_bytes=VMEM_LIMIT),
        name="moe_combine",
    )(ytok, top_w, x2, g2, final_g)


def _route(top_idx, counts, n_rows):
    tm = EXPERT_TILE
    e = top_idx[:, :TOP_K]
    rank = top_idx[:, TOP_K:2 * TOP_K]
    sizes = counts[0, :N_EXPERTS]
    padded = ((sizes + tm - 1) // tm) * tm
    pend = jnp.cumsum(padded)
    pstart = pend - padded
    onehot = e[:, :, None] == jnp.arange(N_EXPERTS, dtype=I32)[None, None, :]
    pos = jnp.sum(jnp.where(onehot, pstart[None, None, :], 0), axis=-1) + rank
    tiles = jnp.arange(n_rows // tm, dtype=I32) * tm
    tile_e = jnp.minimum(jnp.sum(tiles[:, None] >= pend[None, :], axis=-1), N_EXPERTS - 1).astype(I32)
    n_used = (pend[-1] // tm).astype(I32).reshape(1)
    return pos.T.astype(I32), tile_e, n_used


def _sc_gather(table, idx):
    m = idx.shape[0]
    width = table.shape[1]
    assert m % (SC_WINDOW * SC_WORKERS) == 0
    mesh = plsc.VectorSubcoreMesh(core_axis_name="c", subcore_axis_name="s")

    @pl.kernel(out_type=jax.ShapeDtypeStruct((m, width), table.dtype), mesh=mesh)
    def gather_rows(t_hbm, i_hbm, o_hbm):
        def body(i_vmem, o_vmem):
            pltpu.sync_copy(t_hbm.at[i_vmem.at[0]], o_vmem)

        pltpu.emit_pipeline(
            body,
            grid=(m // SC_WINDOW,),
            in_specs=[pl.BlockSpec((1, SC_WINDOW), lambda i: (0, i))],
            out_specs=[pl.BlockSpec((SC_WINDOW, width), lambda i: (i, 0))],
            core_axis_name=("c", "s"),
            dimension_semantics=(pltpu.PARALLEL,),
        )(i_hbm, o_hbm)

    return gather_rows(table, idx.reshape(1, m))


def _sc_scatter(rows, idx, n_out):
    t, width = rows.shape
    picks = idx.shape[0]
    assert t % SC_WINDOW == 0
    mesh = plsc.VectorSubcoreMesh(core_axis_name="c", subcore_axis_name="s")

    @pl.kernel(out_type=jax.ShapeDtypeStruct((n_out, width), rows.dtype), mesh=mesh)
    def scatter_rows(x_hbm, i_hbm, o_hbm):
        def body(x_vmem, i_vmem):
            for k in range(picks):
                pltpu.sync_copy(x_vmem, o_hbm.at[i_vmem.at[k]])

        pltpu.emit_pipeline(
            body,
            grid=(t // SC_WINDOW,),
            in_specs=[pl.BlockSpec((SC_WINDOW, width), lambda i: (i, 0)),
                      pl.BlockSpec((picks, SC_WINDOW), lambda i: (0, i))],
            out_specs=[],
            core_axis_name=("c", "s"),
            dimension_semantics=(pltpu.PARALLEL,),
        )(x_hbm, i_hbm)

    return scatter_rows(rows, idx)


def _dispatch(h2p, pos_t, n_rows):
    n = h2p.shape[1]
    idx = jnp.concatenate([pos_t, pos_t + n_rows], axis=1)
    return _sc_scatter(h2p.reshape(2 * n, QUART), idx, 2 * n_rows).reshape(2, n_rows, QUART)


def _collect(y, pos_t):
    n_rows = y.shape[1]
    flat = pos_t.reshape(-1)
    out = _sc_gather(y.reshape(2 * n_rows, QUART), jnp.concatenate([flat, flat + n_rows]))
    return out.reshape(2, TOP_K, pos_t.shape[1], QUART)


def kernel(x, c, norm1_g, norm2_g, ada_w, ada_b, w_in, gate_b, fox_fb, lam_q1, lam_k1, lam_q2, lam_k2,
           diff_subln_g, w_branch, w_out, router_w, router_b, exp_w_up, exp_b_up, exp_w_down,
           exp_b_down, final_g):
    batch, seq, d = x.shape
    depth = ada_w.shape[0]
    n = batch * seq
    x2 = x.reshape(n, d)

    mod = _ada_mod(c, ada_w, ada_b)
    tables = _retention_tables(seq)
    n_rows = n * TOP_K + N_EXPERTS * EXPERT_TILE
    blk = BRANCH_WIDTH // LANES

    w_in_t = jnp.swapaxes(w_in, 1, 2)
    pending = None
    for l in range(depth):
        lam_init = 0.8 - 0.6 * math.exp(-0.3 * l)
        sh1, sc1, g1, sh2, sc2, g2 = [m.reshape(batch, 1, d) for m in jnp.split(mod[l], 6, axis=-1)]

        w_main, w_f = _prep_win(w_in_t, l)
        outs = _inproj(x2, norm1_g[l].reshape(1, d), sh1, sc1, w_main, w_f, seq, pending)
        proj, flog = outs[0], outs[1]
        if pending is not None:
            x2 = outs[2]

        fb_pad = jnp.pad(fox_fb[l], (0, LANES - FOX_HEADS)).reshape(1, LANES)
        frow = _fcum(flog, fb_pad, batch, seq).reshape(batch * blk, 2, seq)

        lamv = jnp.stack([lam_q1[l], lam_k1[l], lam_q2[l], lam_k2[l]])
        diff_extra = (lamv, diff_subln_g[l].reshape(1, LANES))
        if _rider_fits(batch, exp_w_up):
            o_diff, w_up = _attention(proj, "diff", 0, blk, 2 * blk, batch, seq, diff_extra, lam_init,
                                      rider=("up", exp_w_up, l))
        else:
            o_diff = _attention(proj, "diff", 0, blk, 2 * blk, batch, seq, diff_extra, lam_init)
            w_up = _prep_up(exp_w_up, l)
        o_ret = _retention(proj, tables, batch, seq)
        if _rider_fits(batch, exp_w_down):
            o_fox, w_down = _attention(proj, "fox", 7 * blk, 8 * blk, 9 * blk, batch, seq, (frow,),
                                       rider=("down", exp_w_down, l))
        else:
            o_fox = _attention(proj, "fox", 7 * blk, 8 * blk, 9 * blk, batch, seq, (frow,))
            w_down = exp_w_down[l].astype(BF16)

        rw = jnp.pad(router_w[l], ((0, 0), (0, LANES - N_EXPERTS)))
        rw_hi = rw.astype(BF16)
        rw = jnp.concatenate([rw_hi, (rw - rw_hi.astype(F32)).astype(BF16)], axis=1)
        rb = jnp.pad(router_b[l], (0, LANES - N_EXPERTS), constant_values=NEG_INF).reshape(1, LANES)
        x2, h2p, top_idx, top_w, counts = _merge(
            o_diff, o_ret, o_fox, proj, gate_b[l].reshape(N_BRANCHES, 1, d), w_branch[l].astype(BF16),
            w_out[l].astype(BF16), x2, g1, norm2_g[l].reshape(1, d), sh2, sc2, rw, rb, seq)

        pos_t, tile_e, n_used = _route(top_idx, counts, n_rows)
        xs = _dispatch(h2p, pos_t, n_rows)

        bu = exp_b_up[l]
        b_up = jnp.concatenate([bu[:, 0::2], bu[:, 1::2]], axis=-1).reshape(N_EXPERTS, 1, 2 * D_FF)
        y = _experts(tile_e, n_used, xs, w_up, b_up, w_down,
                     exp_b_down[l].reshape(N_EXPERTS, 1, d))
        pending = (_collect(y, pos_t), top_w, g2)

    ytok, top_w, g2 = pending
    x2 = _combine(ytok, top_w, x2, g2, final_g.reshape(1, d), seq)
    return x2.reshape(batch, seq, d)
```

```python
import functools
import math

import jax
import jax.numpy as jnp
from jax import lax
from jax.experimental import pallas as pl
from jax.experimental.pallas import tpu as pltpu
from jax.experimental.pallas import tpu_sc as plsc

F32 = jnp.float32
BF16 = jnp.bfloat16
I32 = jnp.int32
U32 = jnp.uint32

D_MODEL = 1024
HEAD_DIM = 64
BRANCH_WIDTH = D_MODEL // 2
N_BRANCHES = 3
FOX_HEADS = BRANCH_WIDTH // HEAD_DIM
RET_HEADS = 4
RET_HEAD_DIM = 128
ROPE_BASE = 10000.0
RET_GAMMA_EXP0 = 5.0
N_EXPERTS = 32
TOP_K = 4
D_FF = D_MODEL
SWIGLU_LIMIT = 7.0
SWIGLU_ALPHA = 1.702
NORM_EPS = 1e-5
NEG_INF = -1e30

LANES = 128
MAIN_COLS = 8192
GATE_COL0 = 10 * BRANCH_WIDTH
HALF = D_MODEL // 2
QUART = HALF // 2
SC_WINDOW = 128
SC_WORKERS = 32
VMEM_LIMIT = 48 * 1024 * 1024
VMEM_LIMIT_INPROJ = 56 * 1024 * 1024

ROW_TILE_IN = 512
COL_TILE_IN = 1024
ATTN_TILE = 256
RET_CHUNK = 256
ROW_TILE_OUT = 256
MERGE_TILE = 1024
MERGE_SUB = 512
EXPERT_TILE = 256
EXPERT_GROUP = 2
CUM_CHUNK = 256


def _cparams(sem):
    return pltpu.CompilerParams(dimension_semantics=sem, vmem_limit_bytes=VMEM_LIMIT)


def _sigmoid(x):
    return 0.5 * jnp.tanh(0.5 * x) + 0.5


def _pack_pairs(v):
    u = lax.bitcast_convert_type(v.astype(BF16).astype(F32), U32)
    half = v.shape[-1] // 2
    w = (u[:, half:] & jnp.uint32(0xFFFF0000)) | (u[:, :half] >> 16)
    return lax.bitcast_convert_type(w, I32)


def _unpack_pairs(w):
    u = lax.bitcast_convert_type(w, U32)
    lo = lax.bitcast_convert_type(u << 16, F32)
    hi = lax.bitcast_convert_type(u & jnp.uint32(0xFFFF0000), F32)
    return lo, hi


def _moe_residual(y_ref, tw_ref, x_ref, g2_ref):
    tw = tw_ref[...]
    parts = [None] * 4
    for k in range(TOP_K):
        w = tw[:, k:k + 1]
        for piece in range(2):
            lo, hi = _unpack_pairs(y_ref[piece, k])
            for slot, v in ((piece, lo), (2 + piece, hi)):
                parts[slot] = w * v if parts[slot] is None else parts[slot] + w * v
    return x_ref[...] + g2_ref[0] * jnp.concatenate(parts, axis=-1)


def _ada_kernel(c_ref, w_ref, b_ref, o_ref):
    c = c_ref[...]
    ca = c * _sigmoid(c)
    o_ref[0] = jnp.dot(ca, w_ref[0], preferred_element_type=F32) + b_ref[0]


def _ada_mod(c, ada_w, ada_b):
    depth, d, cols = ada_w.shape
    b = c.shape[0]
    tn = 1024
    return pl.pallas_call(
        _ada_kernel,
        grid=(depth, cols // tn),
        in_specs=[
            pl.BlockSpec((b, d), lambda l, j: (0, 0)),
            pl.BlockSpec((1, d, tn), lambda l, j: (l, 0, j)),
            pl.BlockSpec((1, 1, tn), lambda l, j: (l, 0, j)),
        ],
        out_specs=pl.BlockSpec((1, b, tn), lambda l, j: (l, 0, j)),
        out_shape=jax.ShapeDtypeStruct((depth, b, cols), F32),
        compiler_params=_cparams(("arbitrary", "arbitrary")),
        name="ada_mod",
    )(c, ada_w, ada_b.reshape(depth, 1, cols))


def _prep_win_kernel(w_ref, wm_ref, wf_ref):
    wm_ref[:GATE_COL0, :] = w_ref[0, :GATE_COL0, :].astype(BF16)
    wm_ref[GATE_COL0:, :] = w_ref[0, GATE_COL0 + FOX_HEADS:, :].astype(BF16)
    wf_ref[...] = jnp.zeros(wf_ref.shape, BF16)
    wf_ref[:FOX_HEADS, :] = w_ref[0, GATE_COL0:GATE_COL0 + FOX_HEADS, :].astype(BF16)


def _prep_win(w_t, layer):
    _, cols, d = w_t.shape
    ct = 256
    return pl.pallas_call(
        _prep_win_kernel,
        grid=(d // ct,),
        in_specs=[pl.BlockSpec((1, cols, ct), lambda i: (layer, 0, i))],
        out_specs=[pl.BlockSpec((MAIN_COLS, ct), lambda i: (0, i)), pl.BlockSpec((LANES, ct), lambda i: (0, i))],
        out_shape=[jax.ShapeDtypeStruct((MAIN_COLS, d), BF16), jax.ShapeDtypeStruct((LANES, d), BF16)],
        compiler_params=_cparams(("arbitrary",)),
        name="inproj_weight_prep",
    )(w_t)


def _inproj_kernel(x_ref, g_ref, sh_ref, sc_ref, w_ref, wf_ref, proj_ref, fl_ref):
    x = x_ref[...]
    r = lax.rsqrt(jnp.mean(x * x, axis=-1, keepdims=True) + NORM_EPS)
    h = x * r * g_ref[...]
    h = h * (1.0 + sc_ref[0]) + sh_ref[0]
    _project(h.astype(BF16), w_ref, wf_ref, proj_ref, fl_ref)


def _project(hb, w_ref, wf_ref, proj_ref, fl_ref, after_chunk=None):
    nt = (((1,), (1,)), ((), ()))
    fl_ref[...] = lax.dot_general(hb, wf_ref[...], nt, preferred_element_type=F32)
    for j in range(MAIN_COLS // COL_TILE_IN):
        cs = slice(j * COL_TILE_IN, (j + 1) * COL_TILE_IN)
        proj_ref[:, cs] = lax.dot_general(hb, w_ref[cs, :], nt, preferred_element_type=F32).astype(BF16)
        if after_chunk is not None:
            after_chunk(j)


def _inproj_moe_kernel(y_ref, tw_ref, x_ref, g2_ref, g_ref, sh_ref, sc_ref, w_ref, wf_ref,
                       proj_ref, fl_ref, xo_ref, h_scr):
    i = pl.program_id(0)

    @pl.when(i == 0)
    def _():
        h_scr[...] = jnp.zeros(h_scr.shape, BF16)

    span = h_scr.shape[1] // (MAIN_COLS // COL_TILE_IN)

    def normalise_slice(j):
        rs = slice(j * span, (j + 1) * span)
        x = _moe_residual(y_ref.at[:, :, rs, :], tw_ref.at[rs, :], x_ref.at[rs, :], g2_ref)
        xo_ref[rs, :] = x
        r = lax.rsqrt(jnp.mean(x * x, axis=-1, keepdims=True) + NORM_EPS)
        h = x * r * g_ref[...]
        h = h * (1.0 + sc_ref[0]) + sh_ref[0]
        h_scr[i % 2, rs, :] = h.astype(BF16)

    _project(h_scr[(i + 1) % 2], w_ref, wf_ref, proj_ref, fl_ref, after_chunk=normalise_slice)


def _inproj(x2, g, sh, sc, w_main, w_f, seq, moe=None):
    n, d = x2.shape
    tm = min(ROW_TILE_IN, seq)
    per_seq = seq // tm
    steps = n // tm
    last = steps - 1
    resident = pl.Buffered(1)
    weights = [pl.BlockSpec((MAIN_COLS, d), lambda i: (0, 0), pipeline_mode=resident),
               pl.BlockSpec((LANES, d), lambda i: (0, 0), pipeline_mode=resident)]
    out_shape = [jax.ShapeDtypeStruct((n, MAIN_COLS), BF16), jax.ShapeDtypeStruct((n, LANES), F32)]
    params = pltpu.CompilerParams(dimension_semantics=("arbitrary",), vmem_limit_bytes=VMEM_LIMIT_INPROJ)

    if moe is None:
        def per_b():
            return pl.BlockSpec((1, 1, d), lambda i: (i // per_seq, 0, 0))

        return pl.pallas_call(
            _inproj_kernel,
            grid=(steps,),
            in_specs=[pl.BlockSpec((tm, d), lambda i: (i, 0)), pl.BlockSpec((1, d), lambda i: (0, 0)),
                      per_b(), per_b()] + weights,
            out_specs=[pl.BlockSpec((tm, MAIN_COLS), lambda i: (i, 0)), pl.BlockSpec((tm, LANES), lambda i: (i, 0))],
            out_shape=out_shape,
            compiler_params=params,
            name="inproj",
        )(x2, g, sh, sc, w_main, w_f)

    def rows_in(w):
        return pl.BlockSpec((tm, w), lambda i: (jnp.minimum(i, last), 0))

    def rows_out(w):
        return pl.BlockSpec((tm, w), lambda i: (jnp.maximum(i - 1, 0), 0))

    def per_b():
        return pl.BlockSpec((1, 1, d), lambda i: (jnp.minimum(i, last) // per_seq, 0, 0))

    ytok, top_w, g2 = moe
    return pl.pallas_call(
        _inproj_moe_kernel,
        grid=(steps + 1,),
        in_specs=[pl.BlockSpec((2, TOP_K, tm, QUART), lambda i: (0, 0, jnp.minimum(i, last), 0)),
                  rows_in(LANES), rows_in(d), per_b(),
                  pl.BlockSpec((1, d), lambda i: (0, 0)), per_b(), per_b()] + weights,
        out_specs=[rows_out(MAIN_COLS), rows_out(LANES), rows_in(d)],
        out_shape=out_shape + [jax.ShapeDtypeStruct((n, d), F32)],
        scratch_shapes=[pltpu.VMEM((2, tm, d), BF16)],
        compiler_params=params,
        name="inproj",
    )(ytok, top_w, x2, g2, g, sh, sc, w_main, w_f)


def _split3(v):
    hi = v.astype(BF16)
    r1 = v - hi.astype(F32)
    mid = r1.astype(BF16)
    lo = (r1 - mid.astype(F32)).astype(BF16)
    return hi, mid, lo


def _fcum_kernel(fl_ref, fb_ref, o_ref, *, seq):
    c = min(CUM_CHUNK, seq)
    row = lax.broadcasted_iota(I32, (c, c), 0)
    col = lax.broadcasted_iota(I32, (c, c), 1)
    lower = (col <= row).astype(BF16)
    carry = jnp.zeros((1, LANES), F32)
    for ci in range(seq // c):
        x = fl_ref[pl.ds(ci * c, c), :] + fb_ref[...]
        lf = jnp.minimum(x, 0.0) - jnp.log(1.0 + jnp.exp(-jnp.abs(x)))
        hi, mid, lo = _split3(lf)
        cs = (jnp.dot(lower, hi, preferred_element_type=F32)
              + jnp.dot(lower, mid, preferred_element_type=F32)
              + jnp.dot(lower, lo, preferred_element_type=F32)) + carry
        o_ref[0, :, pl.ds(ci * c, c)] = cs.T[:FOX_HEADS, :]
        carry = cs[c - 1:c, :]


def _fcum(flog, fb_pad, batch, seq):
    return pl.pallas_call(
        functools.partial(_fcum_kernel, seq=seq),
        grid=(batch,),
        in_specs=[
            pl.BlockSpec((seq, LANES), lambda b: (b, 0)),
            pl.BlockSpec((1, LANES), lambda b: (0, 0)),
        ],
        out_specs=pl.BlockSpec((1, FOX_HEADS, seq), lambda b: (b, 0, 0)),
        out_shape=jax.ShapeDtypeStruct((batch, FOX_HEADS, seq), F32),
        compiler_params=_cparams(("arbitrary",)),
        name="fox_decay_cumsum",
    )(flog, fb_pad)


def _attn_kernel(*refs, mode, t, seq, lam_init, rider):
    if rider:
        refs = list(refs)
        wt_scr = refs.pop() if rider == "up" else None
        n_in = 5 if mode == "diff" else 4
        wraw_ref = refs.pop(n_in)
        wprep_ref = refs.pop(n_in + 1)
    if mode == "diff":
        q_ref, k_ref, v_ref, lam_ref, g_ref, o_ref, vx_scr = refs
        f_ref = None
        lv = lam_ref[...]
        lam = (jnp.exp(jnp.sum(lv[0:1, :] * lv[1:2, :])) - jnp.exp(jnp.sum(lv[2:3, :] * lv[3:4, :]))
               + lam_init)
    else:
        q_ref, k_ref, v_ref, f_ref, o_ref, vx_scr = refs
    lane = lax.broadcasted_iota(I32, (t, LANES), 1)
    row = lax.broadcasted_iota(I32, (t, t), 0)
    col = lax.broadcasted_iota(I32, (t, t), 1)
    nt = (((1,), (1,)), ((), ()))

    v_all = v_ref[...]
    one = jnp.ones_like(v_all)
    if mode == "diff":
        vx_scr[0] = jnp.concatenate([v_all, one], axis=1)
    else:
        lane_s = lax.broadcasted_iota(I32, v_all.shape, 1)
        vx_scr[0] = jnp.where(lane_s < HEAD_DIM, v_all, one)
        vx_scr[1] = jnp.where(lane_s >= HEAD_DIM, v_all, one)

    n_strips = seq // t
    pairs = [(i, n_strips - 1 - i) for i in range(n_strips // 2)] if n_strips % 2 == 0 else [(i,) for i in range(n_strips)]
    for gi, group in enumerate(pairs):
        if rider == "up":
            for g in range(gi, wt_scr.shape[0], len(pairs)):
                _prep_up_group(wraw_ref, wprep_ref, wt_scr, g)
        elif rider == "down":
            span = wprep_ref.shape[1] // len(pairs)
            rows = slice(gi * span, (gi + 1) * span)
            wprep_ref[0, rows, :] = wraw_ref[0, 0, rows, :].astype(BF16)
        units = [(qi, idx) for qi in group for idx in range(2)]
        qms = {}
        for qi in group:
            q = q_ref[qi * t:(qi + 1) * t, :] * jnp.asarray(HEAD_DIM ** -0.5, BF16)
            zero = jnp.zeros_like(q)
            qms[qi, 0] = jnp.where(lane < HEAD_DIM, q, zero)
            qms[qi, 1] = jnp.where(lane >= HEAD_DIM, q, zero)

        def scores(u, k0, k1):
            s_ = lax.dot_general(qms[u], k_ref[k0:k1, :], nt, preferred_element_type=F32)
            if f_ref is not None:
                s_ = s_ - f_ref[0, u[1]:u[1] + 1, k0:k1]
            return s_

        sds = {u: jnp.where(col <= row, scores(u, u[0] * t, (u[0] + 1) * t), NEG_INF) for u in units}
        sms = {u: scores(u, 0, u[0] * t) for u in units if u[0] > 0}
        ms = {}
        for u in units:
            m = jnp.max(sds[u], axis=-1, keepdims=True)
            if u in sms:
                m = jnp.maximum(m, jnp.max(sms[u], axis=-1, keepdims=True))
            ms[u] = m
        pds = {u: jnp.exp((sds[u] - ms[u]).astype(BF16)) for u in units}
        pms = {u: jnp.exp((sms[u] - ms[u]).astype(BF16)) for u in sms}
        outs = {}
        for u in units:
            vx = vx_scr.at[0 if mode == "diff" else u[1]]
            lo = u[0] * t
            acc = jnp.dot(pds[u], vx[lo:lo + t, :], preferred_element_type=F32)
            if u in pms:
                acc = acc + jnp.dot(pms[u], vx[0:lo, :], preferred_element_type=F32)
            outs[u] = acc
        for qi in group:
            o0, o1 = outs[qi, 0], outs[qi, 1]
            if mode == "diff":
                a1 = o0[:, :LANES] / o0[:, LANES:]
                a2 = o1[:, :LANES] / o1[:, LANES:]
                o = a1 - lam * a2
                r = lax.rsqrt(jnp.mean(o * o, axis=-1, keepdims=True) + NORM_EPS)
                o = o * r * g_ref[...] * (1.0 - lam_init)
            else:
                num = jnp.where(lane < HEAD_DIM, o0, o1)
                den = jnp.where(lane < HEAD_DIM, pltpu.roll(o0, HEAD_DIM, 1), pltpu.roll(o1, HEAD_DIM, 1))
                o = num / den
            o_ref[qi * t:(qi + 1) * t, :] = o.astype(BF16)


def _attention(proj, mode, col_q, col_k, col_v, batch, seq, extra, lam_init=0.0, rider=None):
    n = proj.shape[0]
    t = min(ATTN_TILE, seq)
    n_blk = BRANCH_WIDTH // LANES
    in_specs = [
        pl.BlockSpec((seq, LANES), lambda b, h: (b, col_q + h)),
        pl.BlockSpec((seq, LANES), lambda b, h: (b, col_k + h)),
        pl.BlockSpec((seq, LANES), lambda b, h: (b, col_v + h)),
    ]
    if mode == "diff":
        lamv, g = extra
        in_specs += [pl.BlockSpec((4, HEAD_DIM), lambda b, h: (0, 0)),
                     pl.BlockSpec((1, LANES), lambda b, h: (0, 0))]
        args = (proj, proj, proj, lamv, g)
    else:
        (frow,) = extra
        in_specs += [pl.BlockSpec((1, 2, seq), lambda b, h: (b * n_blk + h, 0, 0))]
        args = (proj, proj, proj, frow)
    out_specs = [pl.BlockSpec((seq, LANES), lambda b, h: (b, h))]
    out_shape = [jax.ShapeDtypeStruct((n, BRANCH_WIDTH), BF16)]
    scratch = [pltpu.VMEM((1, seq, 2 * LANES) if mode == "diff" else (2, seq, LANES), BF16)]
    if rider is not None:
        kind, w, layer = rider
        _, e, rows, cols = w.shape
        parts = batch * n_blk // e
        kt = rows // parts
        in_specs.append(pl.BlockSpec((1, 1, kt, cols), lambda b, h: (layer, (b * n_blk + h) // parts, (b * n_blk + h) % parts, 0)))
        args = args + (w,)
        if kind == "up":
            out_specs.append(pl.BlockSpec((1, cols, kt), lambda b, h: ((b * n_blk + h) // parts, 0, (b * n_blk + h) % parts)))
            out_shape.append(jax.ShapeDtypeStruct((e, cols, rows), BF16))
            scratch.append(pltpu.VMEM((kt // LANES, cols, LANES), F32))
        else:
            out_specs.append(pl.BlockSpec((1, kt, cols), lambda b, h: ((b * n_blk + h) // parts, (b * n_blk + h) % parts, 0)))
            out_shape.append(jax.ShapeDtypeStruct((e, rows, cols), BF16))
    outs = pl.pallas_call(
        functools.partial(_attn_kernel, mode=mode, t=t, seq=seq, lam_init=lam_init, rider=rider[0] if rider else None),
        grid=(batch, n_blk),
        in_specs=in_specs,
        out_specs=out_specs,
        out_shape=out_shape,
        scratch_shapes=scratch,
        compiler_params=_cparams(("arbitrary", "arbitrary")),
        name="attn_" + mode,
    )(*args)
    return outs if rider is not None else outs[0]


def _rider_fits(batch, w):
    _, e, rows, _ = w.shape
    steps = batch * (BRANCH_WIDTH // LANES)
    return steps % e == 0 and (rows // LANES) % (steps // e) == 0


def _ret_kernel(q_ref, k_ref, v_ref, g_ref, cos_ref, sin_ref, dm_ref, qd_ref, kd_ref, cd_ref,
                o_ref, *, seq, c):
    st = jnp.zeros((RET_HEAD_DIM, RET_HEAD_DIM), F32)
    half = RET_HEAD_DIM // 2

    for ci in range(seq // c):
        sl = pl.ds(ci * c, c)
        cos = cos_ref[sl, :]
        sin = sin_ref[sl, :]
        q = q_ref[sl, :].astype(F32)
        k = k_ref[sl, :].astype(F32)
        qr = q * cos + pltpu.roll(q, half, 1) * sin
        kr = (k * cos + pltpu.roll(k, half, 1) * sin) * (RET_HEAD_DIM ** -0.5)
        v = v_ref[sl, :]
        sc = lax.dot_general(qr.astype(BF16), kr.astype(BF16), (((1,), (1,)), ((), ())),
                             preferred_element_type=F32) * dm_ref[0]
        inner = jnp.dot(sc.astype(BF16), v, preferred_element_type=F32)
        cross = jnp.dot((qr * qd_ref[0]).astype(BF16), st.astype(BF16), preferred_element_type=F32)
        o = inner + cross
        kt = (kr * kd_ref[0]).T.astype(BF16)
        st = st * cd_ref[0] + jnp.dot(kt, v, preferred_element_type=F32)
        mu = jnp.mean(o, axis=-1, keepdims=True)
        oc = o - mu
        var = jnp.mean(oc * oc, axis=-1, keepdims=True)
        on = oc * lax.rsqrt(var + NORM_EPS)
        g = g_ref[sl, :].astype(F32)
        o_ref[sl, :] = (on * (g * _sigmoid(g))).astype(BF16)


def _retention(proj, tables, batch, seq):
    n = proj.shape[0]
    c = min(RET_CHUNK, seq)
    cos, sin, dm, qd, kd, cd = tables
    col = BRANCH_WIDTH * 3 // LANES
    step = BRANCH_WIDTH // LANES

    def pspec(k):
        return pl.BlockSpec((seq, LANES), lambda b, h: (b, col + k * step + h))

    return pl.pallas_call(
        functools.partial(_ret_kernel, seq=seq, c=c),
        grid=(batch, RET_HEADS),
        in_specs=[
            pspec(0), pspec(1), pspec(2), pspec(3),
            pl.BlockSpec((seq, LANES), lambda b, h: (0, 0)),
            pl.BlockSpec((seq, LANES), lambda b, h: (0, 0)),
            pl.BlockSpec((1, c, c), lambda b, h: (h, 0, 0)),
            pl.BlockSpec((1, c, LANES), lambda b, h: (h, 0, 0)),
            pl.BlockSpec((1, c, LANES), lambda b, h: (h, 0, 0)),
            pl.BlockSpec((1, LANES, LANES), lambda b, h: (h, 0, 0)),
        ],
        out_specs=pl.BlockSpec((seq, LANES), lambda b, h: (b, h)),
        out_shape=jax.ShapeDtypeStruct((n, BRANCH_WIDTH), BF16),
        compiler_params=_cparams(("arbitrary", "arbitrary")),
        name="retention",
    )(proj, proj, proj, proj, cos, sin, dm, qd, kd, cd)


def _retention_tables(seq):
    c = min(RET_CHUNK, seq)
    d = RET_HEAD_DIM
    inv = ROPE_BASE ** (-jnp.arange(0, d, 2, dtype=F32) / d)
    ang = jnp.arange(seq).astype(F32)[:, None] * inv[None, :]
    cos = jnp.cos(ang)
    sin = jnp.sin(ang)
    cos_full = jnp.concatenate([cos, cos], axis=-1)
    sin_signed = jnp.concatenate([-sin, sin], axis=-1)
    log_gamma = jnp.log1p(-jnp.exp2(-RET_GAMMA_EXP0 - jnp.arange(RET_HEADS, dtype=F32)))
    idx = jnp.arange(c, dtype=F32)
    dist = idx[:, None] - idx[None, :]
    dm = jnp.where(dist >= 0, jnp.exp(log_gamma[:, None, None] * jnp.maximum(dist, 0.0)), 0.0)
    qd = jnp.exp(log_gamma[:, None] * (idx + 1.0))
    kd = jnp.exp(log_gamma[:, None] * (c - 1.0 - idx))
    cd = jnp.exp(log_gamma * c)
    qd = jnp.broadcast_to(qd[:, :, None], (RET_HEADS, c, LANES))
    kd = jnp.broadcast_to(kd[:, :, None], (RET_HEADS, c, LANES))
    cd = jnp.broadcast_to(cd[:, None, None], (RET_HEADS, LANES, LANES))
    return cos_full, sin_signed, dm, qd, kd, cd


def _merge_kernel(od_ref, or_ref, of_ref, gd_ref, gr_ref, gf_ref, gb_ref, wb_ref, wo_ref, x_ref,
                  g1_ref, n2_ref, sh_ref, sc_ref, rw_ref, rb_ref,
                  xo_ref, hp_ref, ti_ref, tw_ref, cnt_ref, cnt_scr, *, tm, sub):
    @pl.when(pl.program_id(0) == 0)
    def _():
        cnt_scr[...] = jnp.zeros(cnt_scr.shape, F32)

    lane = lax.broadcasted_iota(I32, (sub, LANES), 1)
    r_i = lax.broadcasted_iota(I32, (sub, sub), 0)
    c_i = lax.broadcasted_iota(I32, (sub, sub), 1)
    earlier = (c_i < r_i).astype(BF16)
    counts = cnt_scr[...]

    blocks = [slice(s0, s0 + sub) for s0 in range(0, tm, sub)]

    def phase_branches(rs):
        merged = None
        for i, (o_ref, gate_ref) in enumerate(((od_ref, gd_ref), (or_ref, gr_ref), (of_ref, gf_ref))):
            gate = _sigmoid(gate_ref[rs, :] + gb_ref[i].astype(BF16))
            term = gate.astype(F32) * jnp.dot(o_ref[rs, :], wb_ref[i], preferred_element_type=F32)
            merged = term if merged is None else merged + term
        return merged.astype(BF16)

    def phase_outproj(rs, merged):
        mix = jnp.dot(merged, wo_ref[...], preferred_element_type=F32)
        xn = x_ref[rs, :] + g1_ref[0] * mix
        xo_ref[rs, :] = xn
        r = lax.rsqrt(jnp.mean(xn * xn, axis=-1, keepdims=True) + NORM_EPS)
        h = xn * r * n2_ref[...]
        h = h * (1.0 + sc_ref[0]) + sh_ref[0]
        hp = _pack_pairs(h)
        hp_ref[0, rs, :] = hp[:, :QUART]
        hp_ref[1, rs, :] = hp[:, QUART:]
        hh, hm, _ = _split3(h)
        both = jnp.dot(hh, rw_ref[...], preferred_element_type=F32)
        return (both[:, :LANES] + both[:, LANES:]
                + jnp.dot(hm, rw_ref[:, :LANES], preferred_element_type=F32)) + rb_ref[...]

    def phase_topk(rs, logits, counts):
        vals, idxs = [], []
        for _ in range(TOP_K):
            m = jnp.max(logits, axis=-1, keepdims=True)
            idx = jnp.min(jnp.where(logits == m, lane, LANES), axis=-1, keepdims=True)
            vals.append(m)
            idxs.append(idx)
            logits = jnp.where(lane == idx, -3.0e38, logits)
        es = [jnp.exp(v - vals[0]) for v in vals]
        den = es[0] + es[1] + es[2] + es[3]
        hits = [lane == idxs[k] for k in range(TOP_K)]
        picked = (hits[0] | hits[1] | hits[2] | hits[3])
        base = jnp.dot(earlier, picked.astype(BF16), preferred_element_type=F32) + counts
        counts = counts + jnp.sum(picked.astype(F32), axis=0, keepdims=True)
        ti = jnp.zeros((sub, LANES), I32)
        tw = jnp.zeros((sub, LANES), F32)
        for k in range(TOP_K):
            rank = jnp.sum(jnp.where(hits[k], base, 0.0), axis=-1, keepdims=True).astype(I32)
            ti = jnp.where(lane == k, idxs[k], ti)
            ti = jnp.where(lane == TOP_K + k, rank, ti)
            tw = jnp.where(lane == k, es[k] / den, tw)
        ti_ref[rs, :] = ti
        tw_ref[rs, :] = tw
        return counts

    merged = [phase_branches(rs) for rs in blocks]
    logits = [phase_outproj(rs, m) for rs, m in zip(blocks, merged)]
    for rs, lg in zip(blocks, logits):
        counts = phase_topk(rs, lg, counts)

    cnt_scr[...] = counts
    cnt_ref[...] = counts.astype(I32)


def _merge(o_diff, o_ret, o_fox, proj, gate_b, w_branch, w_out, x2, g1, n2g, sh2, sc2, rw, rb, seq):
    n, d = x2.shape
    tm = min(MERGE_TILE, seq)
    sub = min(MERGE_SUB, tm)
    per_seq = seq // tm
    gcol = GATE_COL0 // d

    def rows(w):
        return pl.BlockSpec((tm, w), lambda i: (i, 0))

    def gspec(k):
        return pl.BlockSpec((tm, d), lambda i: (i, gcol + k))

    def whole(shape):
        return pl.BlockSpec(shape, lambda i: (0,) * len(shape))

    def per_b():
        return pl.BlockSpec((1, 1, d), lambda i: (i // per_seq, 0, 0))

    return pl.pallas_call(
        functools.partial(_merge_kernel, tm=tm, sub=sub),
        grid=(n // tm,),
        in_specs=[
            rows(BRANCH_WIDTH), rows(BRANCH_WIDTH), rows(BRANCH_WIDTH),
            gspec(0), gspec(1), gspec(2),
            whole((N_BRANCHES, 1, d)), whole((N_BRANCHES, BRANCH_WIDTH, d)), whole((d, d)),
            rows(d), per_b(), whole((1, d)), per_b(), per_b(),
            whole((d, 2 * LANES)), whole((1, LANES)),
        ],
        out_specs=[rows(d), pl.BlockSpec((2, tm, QUART), lambda i: (0, i, 0)), rows(LANES), rows(LANES),
                   whole((1, LANES))],
        out_shape=[
            jax.ShapeDtypeStruct((n, d), F32),
            jax.ShapeDtypeStruct((2, n, QUART), I32),
            jax.ShapeDtypeStruct((n, LANES), I32),
            jax.ShapeDtypeStruct((n, LANES), F32),
            jax.ShapeDtypeStruct((1, LANES), I32),
        ],
        scratch_shapes=[pltpu.VMEM((1, LANES), F32)],
        compiler_params=_cparams(("arbitrary",)),
        name="merge_outproj_router",
    )(o_diff, o_ret, o_fox, proj, proj, proj, gate_b, w_branch, w_out, x2, g1, n2g, sh2, sc2, rw, rb)


def _prep_up_group(w_ref, o_ref, t_scr, g):
    f = t_scr.shape[1] // 2
    cols = slice(g * LANES, (g + 1) * LANES)
    t_scr[g] = w_ref[0, 0, cols, :].T
    o_ref[0, 0:f, cols] = t_scr[g, pl.ds(0, f, stride=2), :].astype(BF16)
    o_ref[0, f:, cols] = t_scr[g, pl.ds(1, f, stride=2), :].astype(BF16)


def _prep_up_kernel(w_ref, o_ref, t_scr):
    for g in range(t_scr.shape[0]):
        _prep_up_group(w_ref, o_ref, t_scr, g)


def _prep_up(w_up, layer):
    _, e, d, f2 = w_up.shape
    groups = d // LANES
    kt = groups * LANES
    return pl.pallas_call(
        _prep_up_kernel,
        grid=(e, d // kt),
        in_specs=[pl.BlockSpec((1, 1, kt, f2), lambda i, j: (layer, i, j, 0))],
        out_specs=pl.BlockSpec((1, f2, kt), lambda i, j: (i, 0, j)),
        out_shape=jax.ShapeDtypeStruct((e, f2, d), BF16),
        scratch_shapes=[pltpu.VMEM((groups, f2, LANES), F32)],
        compiler_params=_cparams(("arbitrary", "arbitrary")),
        name="expert_up_weight_prep",
    )(w_up)


def _expert_kernel(te_ref, nu_ref, xs_ref, *refs):
    y_ref = refs[-1]
    nt = (((1,), (1,)), ((), ()))
    first_tile = pl.program_id(0) * EXPERT_GROUP

    @pl.when(first_tile < nu_ref[0])
    def _():
        for s in range(EXPERT_GROUP):
            wu_ref, bu_ref, wd_ref, bd_ref = refs[4 * s:4 * s + 4]
            rs = slice(s * EXPERT_TILE, (s + 1) * EXPERT_TILE)
            lo, hi = _unpack_pairs(jnp.concatenate([xs_ref[0, rs, :], xs_ref[1, rs, :]], axis=1))
            lo = lo.astype(BF16)
            hi = hi.astype(BF16)

            def up_half(r0):
                return (lax.dot_general(lo, wu_ref[0, r0:r0 + D_FF, :HALF], nt, preferred_element_type=F32)
                        + lax.dot_general(hi, wu_ref[0, r0:r0 + D_FF, HALF:], nt, preferred_element_type=F32))

            glu = jnp.minimum(up_half(0) + bu_ref[0, :, :D_FF], SWIGLU_LIMIT)
            lin = jnp.clip(up_half(D_FF) + bu_ref[0, :, D_FF:], -SWIGLU_LIMIT, SWIGLU_LIMIT)
            act = glu * _sigmoid(SWIGLU_ALPHA * glu) * (lin + 1.0)
            down = jnp.dot(act.astype(BF16), wd_ref[0], preferred_element_type=F32) + bd_ref[0]
            yp = _pack_pairs(down)
            y_ref[0, rs, :] = yp[:, :QUART]
            y_ref[1, rs, :] = yp[:, QUART:]

    @pl.when(first_tile >= nu_ref[0])
    def _():
        y_ref[...] = jnp.zeros(y_ref.shape, I32)


def _experts(tile_e, n_used, xs, w_up, b_up, w_down, b_down):
    r = xs.shape[1]
    tm = EXPERT_TILE * EXPERT_GROUP
    d = D_MODEL
    in_specs = [pl.BlockSpec((2, tm, QUART), lambda i, te, nu: (0, i, 0))]
    args = [xs]
    for s in range(EXPERT_GROUP):
        def pick(i, te, nu, s=s):
            return (te[i * EXPERT_GROUP + s], 0, 0)
        in_specs += [pl.BlockSpec((1, 2 * D_FF, d), pick), pl.BlockSpec((1, 1, 2 * D_FF), pick),
                     pl.BlockSpec((1, D_FF, d), pick), pl.BlockSpec((1, 1, d), pick)]
        args += [w_up, b_up, w_down, b_down]
    grid_spec = pltpu.PrefetchScalarGridSpec(
        num_scalar_prefetch=2,
        grid=(r // tm,),
        in_specs=in_specs,
        out_specs=pl.BlockSpec((2, tm, QUART), lambda i, te, nu: (0, i, 0)),
    )
    return pl.pallas_call(
        _expert_kernel,
        grid_spec=grid_spec,
        out_shape=jax.ShapeDtypeStruct((2, r, QUART), I32),
        compiler_params=_cparams(("arbitrary",)),
        name="expert_mlp",
    )(tile_e, n_used, *args)


def _combine_kernel(y_ref, tw_ref, x_ref, g2_ref, fg_ref, o_ref):
    xn = _moe_residual(y_ref, tw_ref, x_ref, g2_ref)
    r = lax.rsqrt(jnp.mean(xn * xn, axis=-1, keepdims=True) + NORM_EPS)
    o_ref[...] = xn * r * fg_ref[...]


def _combine(ytok, top_w, x2, g2, final_g, seq):
    n, d = x2.shape
    tm = min(ROW_TILE_OUT, seq)
    per_seq = seq // tm
    deep = pl.Buffered(3)

    def streamed(y_hbm, tw_hbm, x_hbm, g2_hbm, fg_hbm, o_hbm):
        pltpu.emit_pipeline(
            _combine_kernel,
            grid=(n // tm,),
            in_specs=[
                pl.BlockSpec((2, TOP_K, tm, QUART), lambda i: (0, 0, i, 0), pipeline_mode=deep),
                pl.BlockSpec((tm, LANES), lambda i: (i, 0)),
                pl.BlockSpec((tm, d), lambda i: (i, 0), pipeline_mode=deep),
                pl.BlockSpec((1, 1, d), lambda i: (i // per_seq, 0, 0)),
                pl.BlockSpec((1, d), lambda i: (0, 0)),
            ],
            out_specs=[pl.BlockSpec((tm, d), lambda i: (i, 0))],
        )(y_hbm, tw_hbm, x_hbm, g2_hbm, fg_hbm, o_hbm)

    anywhere = pl.BlockSpec(memory_space=pl.ANY)
    return pl.pallas_call(
        streamed,
        in_specs=[anywhere] * 5,
        out_specs=anywhere,
        out_shape=jax.ShapeDtypeStruct((n, d), F32),
        compiler_params=pltpu.CompilerParams(vmem_limit_bytes=VMEM_LIMIT),
        name="moe_combine",
    )(ytok, top_w, x2, g2, final_g)


def _route(top_idx, counts, n_rows):
    tm = EXPERT_TILE
    e = top_idx[:, :TOP_K]
    rank = top_idx[:, TOP_K:2 * TOP_K]
    sizes = counts[0, :N_EXPERTS]
    padded = ((sizes + tm - 1) // tm) * tm
    pend = jnp.cumsum(padded)
    pstart = pend - padded
    onehot = e[:, :, None] == jnp.arange(N_EXPERTS, dtype=I32)[None, None, :]
    pos = jnp.sum(jnp.where(onehot, pstart[None, None, :], 0), axis=-1) + rank
    tiles = jnp.arange(n_rows // tm, dtype=I32) * tm
    tile_e = jnp.minimum(jnp.sum(tiles[:, None] >= pend[None, :], axis=-1), N_EXPERTS - 1).astype(I32)
    n_used = (pend[-1] // tm).astype(I32).reshape(1)
    return pos.T.astype(I32), tile_e, n_used


def _sc_gather(table, idx):
    m = idx.shape[0]
    width = table.shape[1]
    assert m % (SC_WINDOW * SC_WORKERS) == 0
    mesh = plsc.VectorSubcoreMesh(core_axis_name="c", subcore_axis_name="s")

    @pl.kernel(out_type=jax.ShapeDtypeStruct((m, width), table.dtype), mesh=mesh)
    def gather_rows(t_hbm, i_hbm, o_hbm):
        def body(i_vmem, o_vmem):
            pltpu.sync_copy(t_hbm.at[i_vmem.at[0]], o_vmem)

        pltpu.emit_pipeline(
            body,
            grid=(m // SC_WINDOW,),
            in_specs=[pl.BlockSpec((1, SC_WINDOW), lambda i: (0, i))],
            out_specs=[pl.BlockSpec((SC_WINDOW, width), lambda i: (i, 0))],
            core_axis_name=("c", "s"),
            dimension_semantics=(pltpu.PARALLEL,),
        )(i_hbm, o_hbm)

    return gather_rows(table, idx.reshape(1, m))


def _sc_scatter(rows, idx, n_out):
    t, width = rows.shape
    picks = idx.shape[0]
    assert t % SC_WINDOW == 0
    mesh = plsc.VectorSubcoreMesh(core_axis_name="c", subcore_axis_name="s")

    @pl.kernel(out_type=jax.ShapeDtypeStruct((n_out, width), rows.dtype), mesh=mesh)
    def scatter_rows(x_hbm, i_hbm, o_hbm):
        def body(x_vmem, i_vmem):
            for k in range(picks):
                pltpu.sync_copy(x_vmem, o_hbm.at[i_vmem.at[k]])

        pltpu.emit_pipeline(
            body,
            grid=(t // SC_WINDOW,),
            in_specs=[pl.BlockSpec((SC_WINDOW, width), lambda i: (i, 0), pipeline_mode=pl.Buffered(3)),
                      pl.BlockSpec((picks, SC_WINDOW), lambda i: (0, i))],
            out_specs=[],
            core_axis_name=("c", "s"),
            dimension_semantics=(pltpu.PARALLEL,),
        )(x_hbm, i_hbm)

    return scatter_rows(rows, idx)


def _dispatch(h2p, pos_t, n_rows):
    n = h2p.shape[1]
    idx = jnp.concatenate([pos_t, pos_t + n_rows], axis=1)
    return _sc_scatter(h2p.reshape(2 * n, QUART), idx, 2 * n_rows).reshape(2, n_rows, QUART)


def _collect(y, pos_t):
    n_rows = y.shape[1]
    flat = pos_t.reshape(-1)
    out = _sc_gather(y.reshape(2 * n_rows, QUART), jnp.concatenate([flat, flat + n_rows]))
    return out.reshape(2, TOP_K, pos_t.shape[1], QUART)


def kernel(x, c, norm1_g, norm2_g, ada_w, ada_b, w_in, gate_b, fox_fb, lam_q1, lam_k1, lam_q2, lam_k2,
           diff_subln_g, w_branch, w_out, router_w, router_b, exp_w_up, exp_b_up, exp_w_down,
           exp_b_down, final_g):
    batch, seq, d = x.shape
    depth = ada_w.shape[0]
    n = batch * seq
    x2 = x.reshape(n, d)

    mod = _ada_mod(c, ada_w, ada_b)
    tables = _retention_tables(seq)
    n_rows = n * TOP_K + N_EXPERTS * EXPERT_TILE
    blk = BRANCH_WIDTH // LANES

    w_in_t = jnp.swapaxes(w_in, 1, 2)
    pending = None
    for l in range(depth):
        lam_init = 0.8 - 0.6 * math.exp(-0.3 * l)
        sh1, sc1, g1, sh2, sc2, g2 = [m.reshape(batch, 1, d) for m in jnp.split(mod[l], 6, axis=-1)]

        w_main, w_f = _prep_win(w_in_t, l)
        outs = _inproj(x2, norm1_g[l].reshape(1, d), sh1, sc1, w_main, w_f, seq, pending)
        proj, flog = outs[0], outs[1]
        if pending is not None:
            x2 = outs[2]

        fb_pad = jnp.pad(fox_fb[l], (0, LANES - FOX_HEADS)).reshape(1, LANES)
        frow = _fcum(flog, fb_pad, batch, seq).reshape(batch * blk, 2, seq)

        lamv = jnp.stack([lam_q1[l], lam_k1[l], lam_q2[l], lam_k2[l]])
        diff_extra = (lamv, diff_subln_g[l].reshape(1, LANES))
        if _rider_fits(batch, exp_w_up):
            o_diff, w_up = _attention(proj, "diff", 0, blk, 2 * blk, batch, seq, diff_extra, lam_init,
                                      rider=("up", exp_w_up, l))
        else:
            o_diff = _attention(proj, "diff", 0, blk, 2 * blk, batch, seq, diff_extra, lam_init)
            w_up = _prep_up(exp_w_up, l)
        o_ret = _retention(proj, tables, batch, seq)
        if _rider_fits(batch, exp_w_down):
            o_fox, w_down = _attention(proj, "fox", 7 * blk, 8 * blk, 9 * blk, batch, seq, (frow,),
                                       rider=("down", exp_w_down, l))
        else:
            o_fox = _attention(proj, "fox", 7 * blk, 8 * blk, 9 * blk, batch, seq, (frow,))
            w_down = exp_w_down[l].astype(BF16)

        rw = jnp.pad(router_w[l], ((0, 0), (0, LANES - N_EXPERTS)))
        rw_hi = rw.astype(BF16)
        rw = jnp.concatenate([rw_hi, (rw - rw_hi.astype(F32)).astype(BF16)], axis=1)
        rb = jnp.pad(router_b[l], (0, LANES - N_EXPERTS), constant_values=NEG_INF).reshape(1, LANES)
        x2, h2p, top_idx, top_w, counts = _merge(
            o_diff, o_ret, o_fox, proj, gate_b[l].reshape(N_BRANCHES, 1, d), w_branch[l].astype(BF16),
            w_out[l].astype(BF16), x2, g1, norm2_g[l].reshape(1, d), sh2, sc2, rw, rb, seq)

        pos_t, tile_e, n_used = _route(top_idx, counts, n_rows)
        xs = _dispatch(h2p, pos_t, n_rows)

        bu = exp_b_up[l]
        b_up = jnp.concatenate([bu[:, 0::2], bu[:, 1::2]], axis=-1).reshape(N_EXPERTS, 1, 2 * D_FF)
        y = _experts(tile_e, n_used, xs, w_up, b_up, w_down,
                     exp_b_down[l].reshape(N_EXPERTS, 1, d))
        pending = (_collect(y, pos_t), top_w, g2)

    ytok, top_w, g2 = pending
    x2 = _combine(ytok, top_w, x2, g2, final_g.reshape(1, d), seq)
    return x2.reshape(batch, seq, d)
```
